```python
import math
import jax, jax.numpy as jnp
from jax import lax
import numpy as np

D_MODEL = 1024
BATCH = 4
SEQ = 8192
DEPTH = 2

N_MIXERS = 4
N_HEADS_PER_MIXER = 4
N_HEADS_TOTAL = N_MIXERS * N_HEADS_PER_MIXER
HEAD_DIM = D_MODEL // N_HEADS_TOTAL
GROUP_W = N_HEADS_PER_MIXER * HEAD_DIM
MIX_WIDTH = N_MIXERS * GROUP_W
PROJ_W = 12 * GROUP_W + N_HEADS_PER_MIXER

DIL_PATTERNS = ((128, 1), (512, 4), (2048, 16))
DIFF_DIM = HEAD_DIM // 2
MOBA_BLOCK = 256
MOBA_TOPK = 3
Q_BLOCK = 128
N_BUCKETS = 32
MAX_DISTANCE = 2048
N_BIAS_HEADS = 3 * N_HEADS_PER_MIXER
N_GROUPS = 4
EXPERTS_PER_GROUP = 8
N_EXPERTS = N_GROUPS * EXPERTS_PER_GROUP
TOP_K_INNER = 2
D_EXPERT = D_MODEL // 4
RMS_EPS = 1e-6
NEG_INF = -1e30

kernel_name = "hybrid_dilated_diff_moba_fox_hmoe"


def rmsnorm(x, g):
    xf = x.astype(jnp.float32)
    y = xf * lax.rsqrt(jnp.mean(xf * xf, axis=-1, keepdims=True) + RMS_EPS)
    return (y * g.astype(jnp.float32)).astype(x.dtype)


def t5_bucket(dist):
    max_exact = N_BUCKETS // 2
    d = jnp.maximum(dist, 0)
    df = jnp.maximum(d, 1).astype(jnp.float32)
    large = max_exact + (jnp.log(df / max_exact) / math.log(MAX_DISTANCE / max_exact)
                         * (N_BUCKETS - max_exact)).astype(jnp.int32)
    large = jnp.minimum(large, N_BUCKETS - 1)
    return jnp.where(d < max_exact, d, large)


def _unblock(o):
    nb, B, H, Q, d = o.shape
    return o.transpose(1, 2, 0, 3, 4).reshape(B, H, nb * Q, d)


def dilated_attention(q, k, v, bias_h):
    B, H, S, hd = q.shape
    scale = hd ** -0.5

    def block(i):
        t0 = i * Q_BLOCK
        t = t0 + jnp.arange(Q_BLOCK)
        qb = lax.dynamic_slice_in_dim(q, t0, Q_BLOCK, axis=2)
        outs, lses = [], []
        for window, dil in DIL_PATTERNS:
            steps = jnp.arange(window // dil + 1)
            pos = t[:, None] - steps[None, :] * dil
            valid = pos >= 0
            idx = jnp.maximum(pos, 0)
            kg = k[:, :, idx]
            vg = v[:, :, idx]
            bias = bias_h[:, t5_bucket(steps * dil)]
            s = jnp.einsum('bhqd,bhqkd->bhqk', qb, kg).astype(jnp.float32) * scale + bias[None, :, None, :]
            s = jnp.where(valid, s, NEG_INF)
            m = jnp.max(s, axis=-1, keepdims=True)
            p = jnp.exp(s - m)
            l = jnp.sum(p, axis=-1, keepdims=True)
            outs.append(jnp.einsum('bhqk,bhqkd->bhqd', (p / l).astype(v.dtype), vg))
            lses.append(m + jnp.log(l))
        alpha = jax.nn.softmax(jnp.concatenate(lses, axis=-1), axis=-1)
        o = alpha[..., 0:1] * outs[0] + alpha[..., 1:2] * outs[1] + alpha[..., 2:3] * outs[2]
        return o.astype(v.dtype)

    return _unblock(lax.map(block, jnp.arange(S // Q_BLOCK)))


def diff_attention(q1, q2, k1, k2, v, lam, bias_h):
    B, H, S, dd = q1.shape
    scale = dd ** -0.5
    kpos = jnp.arange(S)

    def block(i):
        t0 = i * Q_BLOCK
        t = t0 + jnp.arange(Q_BLOCK)
        causal = kpos[None, :] <= t[:, None]
        bias = bias_h[:, t5_bucket(t[:, None] - kpos[None, :])]

        def probs(q, k):
            qb = lax.dynamic_slice_in_dim(q, t0, Q_BLOCK, axis=2)
            s = jnp.einsum('bhqd,bhkd->bhqk', qb, k).astype(jnp.float32) * scale + bias
            return jax.nn.softmax(jnp.where(causal, s, NEG_INF), axis=-1)

        w = probs(q1, k1) - lam * probs(q2, k2)
        return jnp.einsum('bhqk,bhkd->bhqd', w.astype(v.dtype), v)

    return _unblock(lax.map(block, jnp.arange(S // Q_BLOCK)))


def moba_attention(q, k, v, bias_h):
    B, H, S, hd = q.shape
    scale = hd ** -0.5
    n_blk = -(-S // MOBA_BLOCK)
    pad = n_blk * MOBA_BLOCK - S
    k_pad = jnp.pad(k, ((0, 0), (0, 0), (0, pad), (0, 0)))
    v_pad = jnp.pad(v, ((0, 0), (0, 0), (0, pad), (0, 0)))
    k_blocks = k_pad.reshape(B, H, n_blk, MOBA_BLOCK, hd)
    v_blocks = v_pad.reshape(B, H, n_blk, MOBA_BLOCK, hd)
    k_mean = jnp.mean(k_blocks.astype(jnp.float32), axis=3)
    n_sel = min(MOBA_TOPK, n_blk)
    n_past = n_sel * MOBA_BLOCK
    bi = jnp.arange(B)[:, None, None, None]
    hi = jnp.arange(H)[None, :, None, None]
    in_blk = jnp.arange(MOBA_BLOCK)
    blk_ids = jnp.arange(n_blk)

    def block(i):
        t0 = i * Q_BLOCK
        t = t0 + jnp.arange(Q_BLOCK)
        own = t0 // MOBA_BLOCK
        qb = lax.dynamic_slice_in_dim(q, t0, Q_BLOCK, axis=2)
        gate = jnp.einsum('bhqd,bhnd->bhqn', qb.astype(jnp.float32), k_mean)
        gate = jnp.where(blk_ids < own, gate, NEG_INF)
        _, sel = lax.top_k(gate, n_sel)
        ok = jnp.repeat(sel < own, MOBA_BLOCK, axis=-1)
        kg = k_blocks[bi, hi, sel].reshape(B, H, Q_BLOCK, n_past, hd)
        vg = v_blocks[bi, hi, sel].reshape(B, H, Q_BLOCK, n_past, hd)
        pos = (sel[..., None] * MOBA_BLOCK + in_blk).reshape(B, H, Q_BLOCK, n_past)
        s_sel = (jnp.einsum('bhqd,bhqnd->bhqn', qb, kg).astype(jnp.float32) * scale
                 + bias_h[hi, t5_bucket(t[:, None] - pos)])
        s_sel = jnp.where(ok, s_sel, NEG_INF)
        own_pos = own * MOBA_BLOCK + in_blk
        k_own = lax.dynamic_slice_in_dim(k_pad, own * MOBA_BLOCK, MOBA_BLOCK, axis=2)
        v_own = lax.dynamic_slice_in_dim(v_pad, own * MOBA_BLOCK, MOBA_BLOCK, axis=2)
        s_own = (jnp.einsum('bhqd,bhkd->bhqk', qb, k_own).astype(jnp.float32) * scale
                 + bias_h[:, t5_bucket(t[:, None] - own_pos[None, :])])
        s_own = jnp.where(own_pos[None, :] <= t[:, None], s_own, NEG_INF)
        p = jax.nn.softmax(jnp.concatenate([s_sel, s_own], axis=-1), axis=-1).astype(v.dtype)
        return (jnp.einsum('bhqn,bhqnd->bhqd', p[..., :n_past], vg)
                + jnp.einsum('bhqk,bhkd->bhqd', p[..., n_past:], v_own))

    return _unblock(lax.map(block, jnp.arange(S // Q_BLOCK)))


def forgetting_attention(q, k, v, log_f):
    B, H, S, hd = q.shape
    scale = hd ** -0.5
    c = jnp.cumsum(log_f, axis=-1)
    kpos = jnp.arange(S)

    def block(i):
        t0 = i * Q_BLOCK
        t = t0 + jnp.arange(Q_BLOCK)
        qb = lax.dynamic_slice_in_dim(q, t0, Q_BLOCK, axis=2)
        c_q = lax.dynamic_slice_in_dim(c, t0, Q_BLOCK, axis=2)
        s = (jnp.einsum('bhqd,bhkd->bhqk', qb, k).astype(jnp.float32) * scale
             + c_q[..., None] - c[:, :, None, :])
        s = jnp.where(kpos[None, :] <= t[:, None], s, NEG_INF)
        p = jax.nn.softmax(s, axis=-1).astype(v.dtype)
        return jnp.einsum('bhqk,bhkd->bhqd', p, v)

    return _unblock(lax.map(block, jnp.arange(S // Q_BLOCK)))


def hybrid_mixer(h, w_in, w_out, lam_q1, lam_k1, lam_q2, lam_k2, subln_g, b_forget, rel_bias, lambda_init):
    B, S, _ = h.shape
    H = N_HEADS_PER_MIXER
    proj = h @ w_in
    qa, ka, va, qb, kb, vb, qc, kc, vc, qd, kd, vd, zf = jnp.split(
        proj, [GROUP_W * i for i in range(1, 13)], axis=-1)

    def heads(t):
        return t.reshape(B, S, H, -1).transpose(0, 2, 1, 3)

    def diff_heads(t):
        t = t.reshape(B, S, H, 2, DIFF_DIM)
        return t[..., 0, :].transpose(0, 2, 1, 3), t[..., 1, :].transpose(0, 2, 1, 3)

    bias_h = rel_bias.T.astype(jnp.float32)

    o_a = dilated_attention(heads(qa), heads(ka), heads(va), bias_h[0:H])

    q1, q2 = diff_heads(qb)
    k1, k2 = diff_heads(kb)
    lam = (jnp.exp(jnp.sum(lam_q1.astype(jnp.float32) * lam_k1.astype(jnp.float32)))
           - jnp.exp(jnp.sum(lam_q2.astype(jnp.float32) * lam_k2.astype(jnp.float32))) + lambda_init)
    o_b = diff_attention(q1, q2, k1, k2, heads(vb), lam, bias_h[H:2 * H])
    o_b = rmsnorm(o_b, subln_g) * (1.0 - lambda_init)

    o_c = moba_attention(heads(qc), heads(kc), heads(vc), bias_h[2 * H:3 * H])

    log_f = jax.nn.log_sigmoid(zf.astype(jnp.float32) + b_forget.astype(jnp.float32)).transpose(0, 2, 1)
    o_d = forgetting_attention(heads(qd), heads(kd), heads(vd), log_f)

    o = jnp.concatenate([o_a.astype(h.dtype), o_b.astype(h.dtype),
                         o_c.astype(h.dtype), o_d.astype(h.dtype)], axis=1)
    o = o.transpose(0, 2, 1, 3).reshape(B, S, MIX_WIDTH)
    return o @ w_out


def hier_moe(h, w_group, b_group, w_expert, b_expert, w_gate, w_up, w_down):
    B, S, D = h.shape
    t = h.reshape(-1, D)
    n_tok = t.shape[0]
    g_logits = (t @ w_group).astype(jnp.float32) + b_group.astype(jnp.float32)
    g_prob = jax.nn.softmax(g_logits, axis=-1)
    g_idx = jnp.argmax(g_logits, axis=-1)
    g_w = jnp.take_along_axis(g_prob, g_idx[:, None], axis=-1)
    e_all = ((t @ w_expert).astype(jnp.float32).reshape(n_tok, N_GROUPS, EXPERTS_PER_GROUP)
             + b_expert.astype(jnp.float32))
    e_logits = jnp.take_along_axis(e_all, g_idx[:, None, None], axis=1)[:, 0]
    top_v, top_i = lax.top_k(e_logits, TOP_K_INNER)
    comb = g_w * jax.nn.softmax(top_v, axis=-1)
    expert_id = g_idx[:, None] * EXPERTS_PER_GROUP + top_i
    combine = jnp.sum(jax.nn.one_hot(expert_id, N_EXPERTS, dtype=jnp.float32) * comb[..., None], axis=1)
    combine = combine.astype(t.dtype)
    out = jnp.zeros_like(t)
    for e in range(N_EXPERTS):
        hid = jax.nn.silu(t @ w_gate[e]) * (t @ w_up[e])
        out = out + combine[:, e:e + 1] * (hid @ w_down[e])
    return out.reshape(B, S, D)


def setup_inputs(seed: int = 0) -> dict:
    key = jax.random.key(seed)
    ks = jax.random.split(key, 21)
    f32 = jnp.float32

    def nrm(k, shape, scale):
        return jax.random.normal(k, shape, f32) * scale

    return {
        "x": nrm(ks[0], (BATCH, SEQ, D_MODEL), 1.0),
        "rel_bias": nrm(ks[1], (N_BUCKETS, N_BIAS_HEADS), 0.5),
        "ln1": 1.0 + nrm(ks[2], (DEPTH, D_MODEL), 0.02),
        "w_in": nrm(ks[3], (DEPTH, D_MODEL, PROJ_W), D_MODEL ** -0.5),
        "w_out": nrm(ks[4], (DEPTH, MIX_WIDTH, D_MODEL), MIX_WIDTH ** -0.5),
        "lam_q1": nrm(ks[5], (DEPTH, DIFF_DIM), 0.1),
        "lam_k1": nrm(ks[6], (DEPTH, DIFF_DIM), 0.1),
        "lam_q2": nrm(ks[7], (DEPTH, DIFF_DIM), 0.1),
        "lam_k2": nrm(ks[8], (DEPTH, DIFF_DIM), 0.1),
        "subln_g": 1.0 + nrm(ks[9], (DEPTH, HEAD_DIM), 0.02),
        "b_forget": 2.0 + nrm(ks[10], (DEPTH, N_HEADS_PER_MIXER), 0.1),
        "ln2": 1.0 + nrm(ks[11], (DEPTH, D_MODEL), 0.02),
        "w_group": nrm(ks[12], (DEPTH, D_MODEL, N_GROUPS), D_MODEL ** -0.5),
        "b_group": nrm(ks[13], (DEPTH, N_GROUPS), 0.01),
        "w_expert": nrm(ks[14], (DEPTH, D_MODEL, N_EXPERTS), D_MODEL ** -0.5),
        "b_expert": nrm(ks[15], (DEPTH, N_GROUPS, EXPERTS_PER_GROUP), 0.01),
        "w_gate": nrm(ks[16], (DEPTH, N_EXPERTS, D_MODEL, D_EXPERT), D_MODEL ** -0.5),
        "w_up": nrm(ks[17], (DEPTH, N_EXPERTS, D_MODEL, D_EXPERT), D_MODEL ** -0.5),
        "w_down": nrm(ks[18], (DEPTH, N_EXPERTS, D_EXPERT, D_MODEL), D_EXPERT ** -0.5),
        "ln_f": 1.0 + nrm(ks[19], (D_MODEL,), 0.02),
    }


def reference(x, rel_bias, ln1, w_in, w_out, lam_q1, lam_k1, lam_q2, lam_k2, subln_g, b_forget,
              ln2, w_group, b_group, w_expert, b_expert, w_gate, w_up, w_down, ln_f):
    for l in range(DEPTH):
        lambda_init = 0.8 - 0.6 * math.exp(-0.3 * l)
        h = rmsnorm(x, ln1[l])
        x = x + hybrid_mixer(h, w_in[l], w_out[l], lam_q1[l], lam_k1[l], lam_q2[l], lam_k2[l],
                             subln_g[l], b_forget[l], rel_bias, lambda_init)
        h2 = rmsnorm(x, ln2[l])
        x = x + hier_moe(h2, w_group[l], b_group[l], w_expert[l], b_expert[l],
                         w_gate[l], w_up[l], w_down[l])
    return rmsnorm(x, ln_f)
```

```python
import functools
import math

import jax
import jax.numpy as jnp
import numpy as np
from jax import lax
from jax.experimental import pallas as pl
from jax.experimental.pallas import tpu as pltpu

D_MODEL = 1024
HEAD_DIM = 64
HEADS_PER_MIXER = 4
GROUP_W = HEADS_PER_MIXER * HEAD_DIM
QKV_W = 12 * GROUP_W
DIFF_DIM = HEAD_DIM // 2
MOBA_BLOCK = 256
MOBA_TOPK = 3
N_BUCKETS = 32
MAX_DISTANCE = 2048
N_BIAS_HEADS = 12
N_GROUPS = 4
EXPERTS_PER_GROUP = 8
N_EXPERTS = 32
D_EXPERT = 256
RMS_EPS = 1e-6
NEG = -1e30

LANES = 128
PAIR_W = 2 * HEAD_DIM
TQ = 256
TK = 256
N_DIL_TILES = MAX_DISTANCE // TK + 1
FAR_TILE = 7
N_TAB_TILES = max(N_DIL_TILES, FAR_TILE + 1)
VMEM_LIMIT = 48 * 1024 * 1024

_NT = (((1,), (1,)), ((), ()))


def _t5_thresholds():
    d = np.arange(0, 4 * MAX_DISTANCE, dtype=np.int64)
    max_exact = N_BUCKETS // 2
    df = np.maximum(d, 1).astype(np.float32)
    large = max_exact + (np.log(df / np.float32(max_exact)) / np.float32(math.log(MAX_DISTANCE / max_exact))
                         * np.float32(N_BUCKETS - max_exact)).astype(np.int32)
    bucket = np.where(d < max_exact, d, np.minimum(large, N_BUCKETS - 1))
    return [int(np.argmax(bucket >= b)) for b in range(N_BUCKETS)]


_T5_THR = _t5_thresholds()
assert (FAR_TILE - 1) * TK + 1 >= _T5_THR[N_BUCKETS - 1]


def _lane_iota(shape):
    return lax.broadcasted_iota(jnp.int32, shape, len(shape) - 1)


def _bias_table_kernel(bias_ref, out_ref):
    h = pl.program_id(0)
    n = pl.program_id(1)
    i = lax.broadcasted_iota(jnp.int32, (TQ, TK), 0)
    j = lax.broadcasted_iota(jnp.int32, (TQ, TK), 1)
    d = n * TK + i - j
    is_dil = h < HEADS_PER_MIXER
    far = jnp.logical_and(jnp.logical_not(is_dil), n >= FAR_TILE)
    d_b = jnp.where(far, 4 * MAX_DISTANCE, d)
    val = jnp.full((TQ, TK), bias_ref[h, 0], jnp.float32)
    for b in range(1, N_BUCKETS):
        val = jnp.where(d_b >= _T5_THR[b], bias_ref[h, b], val)
    mult = ((d <= 128).astype(jnp.int32)
            + jnp.logical_and((d & 3) == 0, d <= 512).astype(jnp.int32)
            + jnp.logical_and((d & 15) == 0, d <= 2048).astype(jnp.int32))
    logm = jnp.where(mult == 3, math.log(3.0), jnp.where(mult == 2, math.log(2.0), 0.0))
    val = val + jnp.where(is_dil, logm, 0.0)
    valid = jnp.logical_and(d >= 0, jnp.logical_or(mult > 0, jnp.logical_not(is_dil)))
    out_ref[...] = jnp.where(valid, val, NEG)


def _bias_tables(rel_bias):
    bias_h = rel_bias.T.astype(jnp.float32)
    return pl.pallas_call(
        _bias_table_kernel,
        out_shape=jax.ShapeDtypeStruct((N_BIAS_HEADS, N_TAB_TILES, TQ, TK), jnp.float32),
        grid=(N_BIAS_HEADS, N_TAB_TILES),
        in_specs=[pl.BlockSpec(memory_space=pltpu.SMEM)],
        out_specs=pl.BlockSpec((None, None, TQ, TK), lambda h, n: (h, n, 0, 0)),
        name="bias_tables",
    )(bias_h)


PROJ_TM = 512
PROJ_CW = 512
F_ROWS = 16


def _norm_proj_kernel(x_ref, g_ref, w_ref, wf_ref, bf_ref, qkv_ref, lf_ref):
    x = x_ref[...]
    ms = jnp.mean(x * x, axis=1, keepdims=True)
    h = x * lax.rsqrt(ms + RMS_EPS) * g_ref[...]
    hb = h.astype(jnp.bfloat16)
    for c in range(QKV_W // PROJ_CW):
        cols = slice(c * PROJ_CW, (c + 1) * PROJ_CW)
        qkv_ref[:, cols] = jnp.dot(hb, w_ref[:, cols], preferred_element_type=jnp.float32).astype(jnp.bfloat16)
    h_lo = (h - hb.astype(jnp.float32)).astype(jnp.bfloat16)
    wf_hi = wf_ref[0]
    wf_lo = wf_ref[1]
    z = (lax.dot_general(wf_hi, hb, _NT, preferred_element_type=jnp.float32)
         + lax.dot_general(wf_hi, h_lo, _NT, preferred_element_type=jnp.float32)
         + lax.dot_general(wf_lo, hb, _NT, preferred_element_type=jnp.float32))
    z = z + bf_ref[...]
    lf_ref[...] = jnp.minimum(z, 0.0) - jnp.log(1.0 + jnp.exp(-jnp.abs(z)))


def _norm_proj(x2d, g, w_qkv, wf, bf, batch, seq):
    n_tok = x2d.shape[0]
    tm = min(PROJ_TM, seq)
    per_b = seq // tm
    return pl.pallas_call(
        _norm_proj_kernel,
        out_shape=(jax.ShapeDtypeStruct((n_tok, QKV_W), jnp.bfloat16),
                   jax.ShapeDtypeStruct((batch * F_ROWS, seq), jnp.float32)),
        grid=(n_tok // tm,),
        in_specs=[pl.BlockSpec((tm, D_MODEL), lambda i: (i, 0)),
                  pl.BlockSpec((1, D_MODEL), lambda i: (0, 0)),
                  pl.BlockSpec((D_MODEL, QKV_W), lambda i: (0, 0)),
                  pl.BlockSpec((2, F_ROWS, D_MODEL), lambda i: (0, 0, 0)),
                  pl.BlockSpec((F_ROWS, 1), lambda i: (0, 0))],
        out_specs=(pl.BlockSpec((tm, QKV_W), lambda i: (i, 0)),
                   pl.BlockSpec((F_ROWS, tm), lambda i: (i // per_b, i % per_b))),
        compiler_params=pltpu.CompilerParams(dimension_semantics=("arbitrary",), vmem_limit_bytes=VMEM_LIMIT),
        name="norm_proj",
    )(x2d, g, w_qkv, wf, bf)


def _split3(x):
    x1 = x.astype(jnp.bfloat16)
    r1 = x - x1.astype(jnp.float32)
    x2 = r1.astype(jnp.bfloat16)
    x3 = (r1 - x2.astype(jnp.float32)).astype(jnp.bfloat16)
    return x1, x2, x3


def _cumsum_kernel(lf_ref, c_ref):
    rows, seq = lf_ref.shape
    r = lax.broadcasted_iota(jnp.int32, (LANES, LANES), 0)
    c = lax.broadcasted_iota(jnp.int32, (LANES, LANES), 1)
    upper = jnp.where(r <= c, 1.0, 0.0).astype(jnp.bfloat16)

    def body(i, carry):
        off = pl.multiple_of(i * LANES, LANES)
        x1, x2, x3 = _split3(lf_ref[:, pl.ds(off, LANES)])
        cs = (jnp.dot(x1, upper, preferred_element_type=jnp.float32)
              + jnp.dot(x2, upper, preferred_element_type=jnp.float32)
              + jnp.dot(x3, upper, preferred_element_type=jnp.float32)) + carry
        c_ref[:, pl.ds(off, LANES)] = cs
        return cs[:, LANES - 1:LANES]

    lax.fori_loop(0, seq // LANES, body, jnp.zeros((rows, 1), jnp.float32))


def _cumsum(lf):
    return pl.pallas_call(
        _cumsum_kernel,
        out_shape=jax.ShapeDtypeStruct(lf.shape, jnp.float32),
        name="forget_cumsum",
    )(lf)


def _init_state(m_ref, l_ref, acc_ref):
    m_ref[...] = jnp.full(m_ref.shape, NEG, jnp.float32)
    l_ref[...] = jnp.zeros(l_ref.shape, jnp.float32)
    acc_ref[...] = jnp.zeros(acc_ref.shape, jnp.float32)


def _softmax_step(st, s, v, m_ref, l_ref, acc_ref):
    m_prev = m_ref[st]
    m_new = jnp.maximum(m_prev, jnp.max(s, axis=1, keepdims=True))
    alpha = jnp.exp(m_prev - m_new)
    p = jnp.exp(s - m_new)
    l_ref[st] = alpha * l_ref[st] + jnp.sum(p, axis=1, keepdims=True)
    acc_ref[st] = alpha * acc_ref[st] + jnp.dot(p.astype(v.dtype), v, preferred_element_type=jnp.float32)
    m_ref[st] = m_new


def _attn_scratch(n_streams):
    return [pltpu.VMEM((n_streams, TQ, PAIR_W), jnp.bfloat16),
            pltpu.VMEM((n_streams, TQ, 1), jnp.float32),
            pltpu.VMEM((n_streams, TQ, 1), jnp.float32),
            pltpu.VMEM((n_streams, TQ, PAIR_W), jnp.float32)]


def _qkv_specs(group, seq):
    nq = seq // TQ
    qc, kc, vc = 6 * group, 6 * group + 2, 6 * group + 4
    return [pl.BlockSpec((TQ, PAIR_W), lambda p, b, i: (b * nq + i, qc + p)),
            pl.BlockSpec((seq, PAIR_W), lambda p, b, i: (b, kc + p)),
            pl.BlockSpec((seq, PAIR_W), lambda p, b, i: (b, vc + p))]


def _attn_call(kernel, group, qkv, extra_inputs, extra_specs, n_streams, batch, seq, extra_scratch=(), name=None):
    nq = seq // TQ
    return pl.pallas_call(
        kernel,
        out_shape=jax.ShapeDtypeStruct((batch * seq, GROUP_W), jnp.bfloat16),
        grid=(2, batch, nq),
        in_specs=_qkv_specs(group, seq) + list(extra_specs),
        out_specs=pl.BlockSpec((TQ, PAIR_W), lambda p, b, i: (b * nq + i, p)),
        scratch_shapes=_attn_scratch(n_streams) + list(extra_scratch),
        compiler_params=pltpu.CompilerParams(dimension_semantics=("arbitrary", "arbitrary", "arbitrary"),
                                             vmem_limit_bytes=VMEM_LIMIT),
        name=name,
    )(qkv, qkv, qkv, *extra_inputs)


def _tab_spec(first_head):
    return pl.BlockSpec((2, N_TAB_TILES, TQ, TK), lambda p, b, i: (first_head // 2 + p, 0, 0, 0))


def _dilated_kernel(q_ref, k_ref, v_ref, tab_ref, o_ref, qs_ref, m_ref, l_ref, acc_ref):
    qi = pl.program_id(2)
    lane = _lane_iota((TQ, PAIR_W))
    q = q_ref[...]
    for hh in range(2):
        qs_ref[hh] = jnp.where(lane // HEAD_DIM == hh, q, jnp.zeros_like(q))
    _init_state(m_ref, l_ref, acc_ref)
    scale = HEAD_DIM ** -0.5

    def body(n, carry):
        k0 = pl.multiple_of((qi - n) * TK, TK)
        k = k_ref[pl.ds(k0, TK), :]
        v = v_ref[pl.ds(k0, TK), :]
        for hh in range(2):
            s = lax.dot_general(qs_ref[hh], k, _NT, preferred_element_type=jnp.float32) * scale + tab_ref[hh, n]
            _softmax_step(hh, s, v, m_ref, l_ref, acc_ref)
        return carry

    lax.fori_loop(0, jnp.minimum(qi, N_DIL_TILES - 1) + 1, body, 0)
    o = jnp.where(lane < HEAD_DIM, acc_ref[0] / l_ref[0], acc_ref[1] / l_ref[1])
    o_ref[...] = o.astype(o_ref.dtype)


def _diff_kernel(q_ref, k_ref, v_ref, tab_ref, lam_ref, g_ref, o_ref, qs_ref, m_ref, l_ref, acc_ref, *, lambda_init):
    qi = pl.program_id(2)
    lane = _lane_iota((TQ, PAIR_W))
    q = q_ref[...]
    for st in range(4):
        qs_ref[st] = jnp.where(lane // DIFF_DIM == st, q, jnp.zeros_like(q))
    _init_state(m_ref, l_ref, acc_ref)
    scale = DIFF_DIM ** -0.5

    def body(n, carry):
        k0 = pl.multiple_of((qi - n) * TK, TK)
        k = k_ref[pl.ds(k0, TK), :]
        v = v_ref[pl.ds(k0, TK), :]
        idx = jnp.minimum(n, FAR_TILE)
        for st in range(4):
            s = lax.dot_general(qs_ref[st], k, _NT, preferred_element_type=jnp.float32) * scale + tab_ref[st // 2, idx]
            _softmax_step(st, s, v, m_ref, l_ref, acc_ref)
        return carry

    lax.fori_loop(0, qi + 1, body, 0)

    lamv = lam_ref[...]
    lam = (jnp.exp(jnp.sum(lamv[0:1] * lamv[1:2], axis=1, keepdims=True))
           - jnp.exp(jnp.sum(lamv[2:3] * lamv[3:4], axis=1, keepdims=True)) + lambda_init)
    o0 = acc_ref[0] / l_ref[0] - lam * (acc_ref[1] / l_ref[1])
    o1 = acc_ref[2] / l_ref[2] - lam * (acc_ref[3] / l_ref[3])
    first = lane < HEAD_DIM
    o = jnp.where(first, o0, o1)
    sq = o * o
    ms0 = jnp.sum(jnp.where(first, sq, 0.0), axis=1, keepdims=True)
    ms1 = jnp.sum(jnp.where(first, 0.0, sq), axis=1, keepdims=True)
    ms = jnp.where(first, ms0, ms1) * (1.0 / HEAD_DIM)
    y = (o * lax.rsqrt(ms + RMS_EPS) * g_ref[...]) * (1.0 - lambda_init)
    o_ref[...] = y.astype(o_ref.dtype)


def _moba_kernel(q_ref, k_ref, v_ref, tab_ref, o_ref, qs_ref, m_ref, l_ref, acc_ref, km_ref):
    qi = pl.program_id(2)
    seq = k_ref.shape[0]
    n_blk = seq // MOBA_BLOCK
    assert n_blk <= HEAD_DIM
    lane = _lane_iota((TQ, PAIR_W))
    lane_f = lane.astype(jnp.float32)

    @pl.when(qi == 0)
    def _():
        km_ref[...] = jnp.zeros(km_ref.shape, jnp.float32)
        for n in range(n_blk):
            mean = jnp.sum(k_ref[n * MOBA_BLOCK:(n + 1) * MOBA_BLOCK, :].astype(jnp.float32),
                           axis=0, keepdims=True) * (1.0 / MOBA_BLOCK)
            km_ref[0, HEAD_DIM + n:HEAD_DIM + n + 1, :] = mean
            km_ref[1, n:n + 1, :] = mean

    q = q_ref[...]
    scale = HEAD_DIM ** -0.5
    _init_state(m_ref, l_ref, acc_ref)
    for hh in range(2):
        own_lanes = lane // HEAD_DIM == hh
        base = HEAD_DIM * (1 - hh)
        qh = jnp.where(own_lanes, q, jnp.zeros_like(q))
        km = km_ref[hh]
        km_hi = km.astype(jnp.bfloat16)
        km_lo = (km - km_hi.astype(jnp.float32)).astype(jnp.bfloat16)
        gate = (lax.dot_general(qh, km_hi, _NT, preferred_element_type=jnp.float32)
                + lax.dot_general(qh, km_lo, _NT, preferred_element_type=jnp.float32))
        blk = lane - base
        cand = jnp.logical_and(blk >= 0, blk < qi)
        g = jnp.where(cand, gate, NEG)
        sel = jnp.zeros((TQ, PAIR_W), jnp.bool_)
        for _ in range(MOBA_TOPK):
            mx = jnp.max(g, axis=1, keepdims=True)
            first = jnp.min(jnp.where(g == mx, lane_f, 2.0 * PAIR_W), axis=1, keepdims=True)
            pick = lane_f == first
            sel = jnp.logical_or(sel, jnp.logical_and(pick, mx > 0.5 * NEG))
            g = jnp.where(pick, -3e38, g)
        mask_bias = jnp.where(sel, 0.0, NEG).astype(jnp.bfloat16)
        in_mask_lanes = jnp.logical_and(blk >= 0, blk < n_blk)
        qs_ref[hh] = jnp.where(own_lanes, q, jnp.where(in_mask_lanes, mask_bias, jnp.zeros_like(q)))

    klane = _lane_iota((TK, PAIR_W))

    def step(n, own):
        k0 = pl.multiple_of((qi - n) * TK, TK)
        k = k_ref[pl.ds(k0, TK), :]
        v = v_ref[pl.ds(k0, TK), :]
        idx = jnp.minimum(n, FAR_TILE)
        for hh in range(2):
            base = HEAD_DIM * (1 - hh)
            if own:
                qa = jnp.where(lane // HEAD_DIM == hh, q, jnp.zeros_like(q))
                ka = k
            else:
                qa = qs_ref[hh]
                onehot = jnp.where(klane == base + (qi - n), 1.0, 0.0).astype(k.dtype)
                ka = jnp.where(klane // HEAD_DIM == hh, k, onehot)
            s = lax.dot_general(qa, ka, _NT, preferred_element_type=jnp.float32) * scale + tab_ref[hh, idx]
            _softmax_step(hh, s, v, m_ref, l_ref, acc_ref)

    step(0, True)

    def body(n, carry):
        step(n, False)
        return carry

    lax.fori_loop(1, qi + 1, body, 0)
    o = jnp.where(lane < HEAD_DIM, acc_ref[0] / l_ref[0], acc_ref[1] / l_ref[1])
    o_ref[...] = o.astype(o_ref.dtype)


def _fox_kernel(q_ref, k_ref, v_ref, c_ref, o_ref, qs_ref, m_ref, l_ref, acc_ref, cq_ref):
    qi = pl.program_id(2)
    lane = _lane_iota((TQ, PAIR_W))
    q = q_ref[...]
    q0 = pl.multiple_of(qi * TQ, TQ)
    row = lax.broadcasted_iota(jnp.int32, (TQ, TK), 0)
    col = lax.broadcasted_iota(jnp.int32, (TQ, TK), 1)
    for hh in range(2):
        qs_ref[hh] = jnp.where(lane // HEAD_DIM == hh, q, jnp.zeros_like(q))
        c_row = c_ref[hh:hh + 1, pl.ds(q0, TQ)]
        cq_ref[hh] = jnp.sum(jnp.where(row == col, c_row, 0.0), axis=1, keepdims=True)
    _init_state(m_ref, l_ref, acc_ref)
    scale = HEAD_DIM ** -0.5

    def step(n, causal):
        k0 = pl.multiple_of((qi - n) * TK, TK)
        k = k_ref[pl.ds(k0, TK), :]
        v = v_ref[pl.ds(k0, TK), :]
        for hh in range(2):
            ck = c_ref[hh:hh + 1, pl.ds(k0, TK)]
            s = (lax.dot_general(qs_ref[hh], k, _NT, preferred_element_type=jnp.float32) * scale
                 + (cq_ref[hh] - ck))
            if causal:
                s = jnp.where(col <= row, s, NEG)
            _softmax_step(hh, s, v, m_ref, l_ref, acc_ref)

    step(0, True)

    def body(n, carry):
        step(n, False)
        return carry

    lax.fori_loop(1, qi + 1, body, 0)
    o = jnp.where(lane < HEAD_DIM, acc_ref[0] / l_ref[0], acc_ref[1] / l_ref[1])
    o_ref[...] = o.astype(o_ref.dtype)


OUT_TM = 512
ROUTER_W = LANES


def _outproj_router_kernel(oa_ref, ob_ref, oc_ref, od_ref, wo_ref, x_ref, g_ref, wr_ref, br_ref,
                           x1_ref, h2_ref, comb_ref):
    acc = x_ref[...]
    for m, o_ref in enumerate((oa_ref, ob_ref, oc_ref, od_ref)):
        acc = acc + jnp.dot(o_ref[...], wo_ref[m * GROUP_W:(m + 1) * GROUP_W, :], preferred_element_type=jnp.float32)
    x1_ref[...] = acc
    ms = jnp.mean(acc * acc, axis=1, keepdims=True)
    h = acc * lax.rsqrt(ms + RMS_EPS) * g_ref[...]
    hb = h.astype(jnp.bfloat16)
    h2_ref[...] = hb
    h_lo = (h - hb.astype(jnp.float32)).astype(jnp.bfloat16)
    logits = (jnp.dot(hb, wr_ref[0], preferred_element_type=jnp.float32)
              + jnp.dot(h_lo, wr_ref[0], preferred_element_type=jnp.float32)
              + jnp.dot(hb, wr_ref[1], preferred_element_type=jnp.float32)) + br_ref[...]
    lane = _lane_iota(logits.shape).astype(jnp.float32)
    big = 2.0 * LANES
    gmask = jnp.logical_and(lane >= N_EXPERTS, lane < N_EXPERTS + N_GROUPS)
    gl = jnp.where(gmask, logits, NEG)
    gmax = jnp.max(gl, axis=1, keepdims=True)
    glane = jnp.min(jnp.where(gl == gmax, lane, big), axis=1, keepdims=True)
    gsum = jnp.sum(jnp.where(gmask, jnp.exp(gl - gmax), 0.0), axis=1, keepdims=True)
    g_w = 1.0 / gsum
    e0 = (glane - N_EXPERTS) * EXPERTS_PER_GROUP
    emask = jnp.logical_and(lane >= e0, lane < e0 + EXPERTS_PER_GROUP)
    el = jnp.where(emask, logits, NEG)
    v1 = jnp.max(el, axis=1, keepdims=True)
    i1 = jnp.min(jnp.where(el == v1, lane, big), axis=1, keepdims=True)
    el2 = jnp.where(lane == i1, NEG, el)
    v2 = jnp.max(el2, axis=1, keepdims=True)
    i2 = jnp.min(jnp.where(el2 == v2, lane, big), axis=1, keepdims=True)
    e2 = jnp.exp(v2 - v1)
    den = 1.0 + e2
    comb_ref[...] = jnp.where(lane == i1, g_w / den, jnp.where(lane == i2, g_w * e2 / den, 0.0))


def _outproj_router(o_parts, w_out, x2d, g2, wr, br):
    n_tok = x2d.shape[0]
    tm = min(OUT_TM, n_tok)
    row = lambda i: (i, 0)
    fixed = lambda i: (0, 0)
    return pl.pallas_call(
        _outproj_router_kernel,
        out_shape=(jax.ShapeDtypeStruct((n_tok, D_MODEL), jnp.float32),
                   jax.ShapeDtypeStruct((n_tok, D_MODEL), jnp.bfloat16),
                   jax.ShapeDtypeStruct((n_tok, ROUTER_W), jnp.float32)),
        grid=(n_tok // tm,),
        in_specs=[pl.BlockSpec((tm, GROUP_W), row)] * 4 + [
            pl.BlockSpec((D_MODEL, D_MODEL), fixed),
            pl.BlockSpec((tm, D_MODEL), row),
            pl.BlockSpec((1, D_MODEL), fixed),
            pl.BlockSpec((2, D_MODEL, ROUTER_W), lambda i: (0, 0, 0)),
            pl.BlockSpec((1, ROUTER_W), fixed)],
        out_specs=(pl.BlockSpec((tm, D_MODEL), row), pl.BlockSpec((tm, D_MODEL), row),
                   pl.BlockSpec((tm, ROUTER_W), row)),
        compiler_params=pltpu.CompilerParams(dimension_semantics=("arbitrary",), vmem_limit_bytes=VMEM_LIMIT),
        name="outproj_router",
    )(*o_parts, w_out, x2d, g2, wr, br)


MOE_TM = 1024


def _moe_kernel(h_ref, comb_ref, wg_ref, wu_ref, wd_ref, x1_ref, gf_ref, out_ref, acc_ref, *, final_norm):
    e = pl.program_id(1)

    @pl.when(e == 0)
    def _():
        acc_ref[...] = jnp.zeros(acc_ref.shape, jnp.float32)

    h = h_ref[...]
    gate = jnp.dot(h, wg_ref[...], preferred_element_type=jnp.float32)
    up = jnp.dot(h, wu_ref[...], preferred_element_type=jnp.float32)
    hid = (gate * jax.nn.sigmoid(gate)) * up
    comb = comb_ref[...]
    ce = jnp.sum(jnp.where(_lane_iota(comb.shape) == e, comb, 0.0), axis=1, keepdims=True)
    acc_ref[...] += ce * jnp.dot(hid.astype(jnp.bfloat16), wd_ref[...], preferred_element_type=jnp.float32)

    @pl.when(e == N_EXPERTS - 1)
    def _():
        y = x1_ref[...] + acc_ref[...]
        if final_norm:
            ms = jnp.mean(y * y, axis=1, keepdims=True)
            y = y * lax.rsqrt(ms + RMS_EPS) * gf_ref[...]
        out_ref[...] = y


def _moe(h2, comb, wg, wu, wd, x1, gf, final_norm):
    n_tok = h2.shape[0]
    tm = min(MOE_TM, n_tok)
    row = lambda i, e: (i, 0)
    return pl.pallas_call(
        functools.partial(_moe_kernel, final_norm=final_norm),
        out_shape=jax.ShapeDtypeStruct((n_tok, D_MODEL), jnp.float32),
        grid=(n_tok // tm, N_EXPERTS),
        in_specs=[pl.BlockSpec((tm, D_MODEL), row),
                  pl.BlockSpec((tm, ROUTER_W), row),
                  pl.BlockSpec((None, D_MODEL, D_EXPERT), lambda i, e: (e, 0, 0)),
                  pl.BlockSpec((None, D_MODEL, D_EXPERT), lambda i, e: (e, 0, 0)),
                  pl.BlockSpec((None, D_EXPERT, D_MODEL), lambda i, e: (e, 0, 0)),
                  pl.BlockSpec((tm, D_MODEL), row),
                  pl.BlockSpec((1, D_MODEL), lambda i, e: (0, 0))],
        out_specs=pl.BlockSpec((tm, D_MODEL), row),
        scratch_shapes=[pltpu.VMEM((tm, D_MODEL), jnp.float32)],
        compiler_params=pltpu.CompilerParams(dimension_semantics=("arbitrary", "arbitrary"),
                                             vmem_limit_bytes=VMEM_LIMIT),
        name="moe_experts",
    )(h2, comb, wg, wu, wd, x1, gf)


def _split_bf16(w):
    hi = w.astype(jnp.bfloat16)
    lo = (w - hi.astype(jnp.float32)).astype(jnp.bfloat16)
    return jnp.stack([hi, lo])


def _forget_weights(w_f, b_f):
    rows = jnp.zeros((F_ROWS, D_MODEL), jnp.float32)
    bias = jnp.zeros((F_ROWS, 1), jnp.float32)
    for h in range(HEADS_PER_MIXER):
        r = (h // 2) * 8 + (h % 2)
        rows = rows.at[r].set(w_f[:, h])
        bias = bias.at[r, 0].set(b_f[h])
    return _split_bf16(rows), bias


def kernel(x, rel_bias, ln1, w_in, w_out, lam_q1, lam_k1, lam_q2, lam_k2, subln_g, b_forget,
           ln2, w_group, b_group, w_expert, b_expert, w_gate, w_up, w_down, ln_f):
    batch, seq, _ = x.shape
    depth = ln1.shape[0]
    assert seq % TQ == 0 and seq % MOBA_BLOCK == 0
    tabs = _bias_tables(rel_bias)
    xt = x.reshape(batch * seq, D_MODEL)
    gf = ln_f.reshape(1, D_MODEL)
    for l in range(depth):
        lambda_init = 0.8 - 0.6 * math.exp(-0.3 * l)
        wf, bf = _forget_weights(w_in[l][:, QKV_W:], b_forget[l])
        qkv, lf = _norm_proj(xt, ln1[l].reshape(1, D_MODEL), w_in[l][:, :QKV_W].astype(jnp.bfloat16), wf, bf,
                             batch, seq)
        c = _cumsum(lf).reshape(batch, 2, 8, seq)

        o_a = _attn_call(_dilated_kernel, 0, qkv, [tabs], [_tab_spec(0)], 2, batch, seq, name="dilated_attn")

        lamv = jnp.zeros((8, LANES), jnp.float32)
        lamv = lamv.at[0:4, 0:DIFF_DIM].set(jnp.stack([lam_q1[l], lam_k1[l], lam_q2[l], lam_k2[l]]))
        g_sub = jnp.tile(subln_g[l], 2).reshape(1, PAIR_W)
        o_b = _attn_call(functools.partial(_diff_kernel, lambda_init=lambda_init), 1, qkv,
                         [tabs, lamv, g_sub],
                         [_tab_spec(4), pl.BlockSpec((8, LANES), lambda p, b, i: (0, 0)),
                          pl.BlockSpec((1, PAIR_W), lambda p, b, i: (0, 0))],
                         4, batch, seq, name="diff_attn")

        o_c = _attn_call(_moba_kernel, 2, qkv, [tabs], [_tab_spec(8)], 2, batch, seq,
                         extra_scratch=[pltpu.VMEM((2, PAIR_W, PAIR_W), jnp.float32)], name="moba_attn")

        o_d = _attn_call(_fox_kernel, 3, qkv, [c],
                         [pl.BlockSpec((None, None, 8, seq), lambda p, b, i: (b, p, 0, 0))],
                         2, batch, seq, extra_scratch=[pltpu.VMEM((2, TQ, 1), jnp.float32)], name="fox_attn")

        w_router = jnp.zeros((D_MODEL, ROUTER_W), jnp.float32)
        w_router = w_router.at[:, :N_EXPERTS].set(w_expert[l]).at[:, N_EXPERTS:N_EXPERTS + N_GROUPS].set(w_group[l])
        b_router = jnp.zeros((1, ROUTER_W), jnp.float32)
        b_router = b_router.at[0, :N_EXPERTS].set(b_expert[l].reshape(-1)).at[0, N_EXPERTS:N_EXPERTS + N_GROUPS].set(b_group[l])
        x1, h2, comb = _outproj_router([o_a, o_b, o_c, o_d], w_out[l].astype(jnp.bfloat16), xt,
                                       ln2[l].reshape(1, D_MODEL), _split_bf16(w_router), b_router)
        xt = _moe(h2, comb, w_gate[l].astype(jnp.bfloat16), w_up[l].astype(jnp.bfloat16),
                  w_down[l].astype(jnp.bfloat16), x1, gf, final_norm=(l == depth - 1))
    return xt.reshape(batch, seq, D_MODEL)
```

```python
import functools
import math

import jax
import jax.numpy as jnp
import numpy as np
from jax import lax
from jax.experimental import pallas as pl
from jax.experimental.pallas import tpu as pltpu

D_MODEL = 1024
HEAD_DIM = 64
HEADS_PER_MIXER = 4
GROUP_W = HEADS_PER_MIXER * HEAD_DIM
QKV_W = 12 * GROUP_W
DIFF_DIM = HEAD_DIM // 2
MOBA_BLOCK = 256
MOBA_TOPK = 3
N_BUCKETS = 32
MAX_DISTANCE = 2048
N_BIAS_HEADS = 12
N_GROUPS = 4
EXPERTS_PER_GROUP = 8
N_EXPERTS = 32
D_EXPERT = 256
RMS_EPS = 1e-6
NEG = -1e30
LOG2E = math.log2(math.e)

LANES = 128
PAIR_W = 2 * HEAD_DIM
ACC_W = 2 * PAIR_W
TQ = 256
TK = 256
N_DIL_TILES = MAX_DISTANCE // TK + 1
FAR_TILE = 7
N_TAB_TILES = max(N_DIL_TILES, FAR_TILE + 1)
VMEM_LIMIT = 48 * 1024 * 1024

_NT = (((1,), (1,)), ((), ()))


def _t5_thresholds():
    d = np.arange(0, 4 * MAX_DISTANCE, dtype=np.int64)
    max_exact = N_BUCKETS // 2
    df = np.maximum(d, 1).astype(np.float32)
    large = max_exact + (np.log(df / np.float32(max_exact)) / np.float32(math.log(MAX_DISTANCE / max_exact))
                         * np.float32(N_BUCKETS - max_exact)).astype(np.int32)
    bucket = np.where(d < max_exact, d, np.minimum(large, N_BUCKETS - 1))
    return [int(np.argmax(bucket >= b)) for b in range(N_BUCKETS)]


_T5_THR = _t5_thresholds()
assert (FAR_TILE - 1) * TK + 1 >= _T5_THR[N_BUCKETS - 1]


def _lane_iota(shape):
    return lax.broadcasted_iota(jnp.int32, shape, len(shape) - 1)


def _bias_table_kernel(bias_ref, out_ref):
    h = pl.program_id(0)
    n = pl.program_id(1)
    i = lax.broadcasted_iota(jnp.int32, (TQ, TK), 0)
    j = lax.broadcasted_iota(jnp.int32, (TQ, TK), 1)
    d = n * TK + i - j
    is_dil = h < HEADS_PER_MIXER
    far = jnp.logical_and(jnp.logical_not(is_dil), n >= FAR_TILE)
    d_b = jnp.where(far, 4 * MAX_DISTANCE, d)
    val = jnp.full((TQ, TK), bias_ref[h, 0], jnp.float32)
    for b in range(1, N_BUCKETS):
        val = jnp.where(d_b >= _T5_THR[b], bias_ref[h, b], val)
    mult = ((d <= 128).astype(jnp.int32)
            + jnp.logical_and((d & 3) == 0, d <= 512).astype(jnp.int32)
            + jnp.logical_and((d & 15) == 0, d <= 2048).astype(jnp.int32))
    logm = jnp.where(mult == 3, math.log(3.0), jnp.where(mult == 2, math.log(2.0), 0.0))
    val = val + jnp.where(is_dil, logm, 0.0)
    valid = jnp.logical_and(d >= 0, jnp.logical_or(mult > 0, jnp.logical_not(is_dil)))
    out_ref[...] = jnp.where(valid, val * LOG2E, NEG)


def _bias_tables(rel_bias):
    bias_h = rel_bias.T.astype(jnp.float32)
    return pl.pallas_call(
        _bias_table_kernel,
        out_shape=jax.ShapeDtypeStruct((N_BIAS_HEADS, N_TAB_TILES, TQ, TK), jnp.float32),
        grid=(N_BIAS_HEADS, N_TAB_TILES),
        in_specs=[pl.BlockSpec(memory_space=pltpu.SMEM)],
        out_specs=pl.BlockSpec((None, None, TQ, TK), lambda h, n: (h, n, 0, 0)),
        name="bias_tables",
    )(bias_h)


PROJ_TM = 512
PROJ_CW = 512
F_ROWS = 16


def _norm_proj_kernel(x_ref, g_ref, w_ref, wf_ref, bf_ref, qkv_ref, lf_ref):
    x = x_ref[...]
    ms = jnp.mean(x * x, axis=1, keepdims=True)
    h = x * lax.rsqrt(ms + RMS_EPS) * g_ref[...]
    hb = h.astype(jnp.bfloat16)
    for c in range(QKV_W // PROJ_CW):
        cols = slice(c * PROJ_CW, (c + 1) * PROJ_CW)
        qkv_ref[:, cols] = jnp.dot(hb, w_ref[:, cols], preferred_element_type=jnp.float32).astype(jnp.bfloat16)
    h_lo = (h - hb.astype(jnp.float32)).astype(jnp.bfloat16)
    wf_hi = wf_ref[0]
    wf_lo = wf_ref[1]
    z = (lax.dot_general(wf_hi, hb, _NT, preferred_element_type=jnp.float32)
         + lax.dot_general(wf_hi, h_lo, _NT, preferred_element_type=jnp.float32)
         + lax.dot_general(wf_lo, hb, _NT, preferred_element_type=jnp.float32))
    z = z + bf_ref[...]
    lf_ref[...] = jnp.minimum(z, 0.0) - jnp.log(1.0 + jnp.exp(-jnp.abs(z)))


def _norm_proj(x2d, g, w_qkv, wf, bf, batch, seq):
    n_tok = x2d.shape[0]
    tm = min(PROJ_TM, seq)
    per_b = seq // tm
    return pl.pallas_call(
        _norm_proj_kernel,
        out_shape=(jax.ShapeDtypeStruct((n_tok, QKV_W), jnp.bfloat16),
                   jax.ShapeDtypeStruct((batch * F_ROWS, seq), jnp.float32)),
        grid=(n_tok // tm,),
        in_specs=[pl.BlockSpec((tm, D_MODEL), lambda i: (i, 0)),
                  pl.BlockSpec((1, D_MODEL), lambda i: (0, 0)),
                  pl.BlockSpec((D_MODEL, QKV_W), lambda i: (0, 0)),
                  pl.BlockSpec((2, F_ROWS, D_MODEL), lambda i: (0, 0, 0)),
                  pl.BlockSpec((F_ROWS, 1), lambda i: (0, 0))],
        out_specs=(pl.BlockSpec((tm, QKV_W), lambda i: (i, 0)),
                   pl.BlockSpec((F_ROWS, tm), lambda i: (i // per_b, i % per_b))),
        compiler_params=pltpu.CompilerParams(dimension_semantics=("arbitrary",), vmem_limit_bytes=VMEM_LIMIT),
        name="norm_proj",
    )(x2d, g, w_qkv, wf, bf)


def _split3(x):
    x1 = x.astype(jnp.bfloat16)
    r1 = x - x1.astype(jnp.float32)
    x2 = r1.astype(jnp.bfloat16)
    x3 = (r1 - x2.astype(jnp.float32)).astype(jnp.bfloat16)
    return x1, x2, x3


def _cumsum_kernel(lf_ref, c_ref):
    rows, seq = lf_ref.shape
    r = lax.broadcasted_iota(jnp.int32, (LANES, LANES), 0)
    c = lax.broadcasted_iota(jnp.int32, (LANES, LANES), 1)
    upper = jnp.where(r <= c, 1.0, 0.0).astype(jnp.bfloat16)

    def body(i, carry):
        off = pl.multiple_of(i * LANES, LANES)
        x1, x2, x3 = _split3(lf_ref[:, pl.ds(off, LANES)])
        cs = (jnp.dot(x1, upper, preferred_element_type=jnp.float32)
              + jnp.dot(x2, upper, preferred_element_type=jnp.float32)
              + jnp.dot(x3, upper, preferred_element_type=jnp.float32)) + carry
        c_ref[:, pl.ds(off, LANES)] = cs * LOG2E
        return cs[:, LANES - 1:LANES]

    lax.fori_loop(0, seq // LANES, body, jnp.zeros((rows, 1), jnp.float32))


def _cumsum(lf):
    return pl.pallas_call(
        _cumsum_kernel,
        out_shape=jax.ShapeDtypeStruct(lf.shape, jnp.float32),
        name="forget_cumsum",
    )(lf)


def _stack_masked_q(q, qs_ref, n_streams):
    lane = _lane_iota(q.shape)
    width = PAIR_W // n_streams
    for st in range(n_streams):
        qs_ref[st * TQ:(st + 1) * TQ, 0:PAIR_W] = jnp.where(lane // width == st, q, jnp.zeros_like(q))


def _init_state(m_ref, acc_ref):
    m_ref[...] = jnp.full(m_ref.shape, NEG, jnp.float32)
    acc_ref[...] = jnp.zeros(acc_ref.shape, jnp.float32)


def _rows(st):
    return slice(st * TQ, (st + 1) * TQ)


def _flash_step(st, s, v_aug, m_ref, acc_ref):
    rows = _rows(st)
    m_prev = m_ref[rows, :]
    m_new = jnp.maximum(m_prev, jnp.max(s, axis=1, keepdims=True))
    alpha = jnp.exp2(m_prev - m_new)
    p = jnp.concatenate([jnp.exp2(s[:, c * LANES:(c + 1) * LANES] - m_new) for c in range(s.shape[1] // LANES)],
                        axis=1).astype(v_aug.dtype)
    pv = jnp.dot(p, v_aug, preferred_element_type=jnp.float32)
    acc_ref[rows, :] = jnp.concatenate([alpha, alpha], axis=1) * acc_ref[rows, :] + pv
    m_ref[rows, :] = m_new


def _with_ones(v):
    return jnp.concatenate([v, jnp.ones_like(v)], axis=1)


def _normalised(acc_ref, st):
    return acc_ref[_rows(st), 0:PAIR_W] / acc_ref[_rows(st), PAIR_W:ACC_W]


def _table_bias(tab_ref, hh, n, width, cap):
    return jnp.concatenate([tab_ref[hh, jnp.minimum(n + width - 1 - w, cap)] for w in range(width)], axis=1)


def _kv_tiles(k_ref, v_ref, first_tile, width):
    k0 = pl.multiple_of(first_tile * TK, TK)
    return k_ref[pl.ds(k0, width * TK), :], v_ref[pl.ds(k0, width * TK), :]


def _visit_tiles(step, n_past):
    step(0, 1)
    odd = n_past % 2

    @pl.when(odd == 1)
    def _():
        step(1, 1)

    def body(i, carry):
        step(1 + odd + 2 * i, 2)
        return carry

    lax.fori_loop(0, n_past // 2, body, 0)


def _attn_call(kernel, group, qkv, extra_inputs, extra_specs, n_streams, batch, seq, *, q_width=PAIR_W,
               extra_scratch=(), name=None):
    nq = seq // TQ
    qc, kc, vc = 6 * group, 6 * group + 2, 6 * group + 4
    rows = n_streams * TQ
    return pl.pallas_call(
        kernel,
        out_shape=jax.ShapeDtypeStruct((batch * seq, GROUP_W), jnp.bfloat16),
        grid=(2, batch, nq),
        in_specs=[pl.BlockSpec((TQ, PAIR_W), lambda p, b, i: (b * nq + i, qc + p)),
                  pl.BlockSpec((seq, PAIR_W), lambda p, b, i: (b, kc + p)),
                  pl.BlockSpec((seq, PAIR_W), lambda p, b, i: (b, vc + p))] + list(extra_specs),
        out_specs=pl.BlockSpec((TQ, PAIR_W), lambda p, b, i: (b * nq + i, p)),
        scratch_shapes=[pltpu.VMEM((rows, q_width), jnp.bfloat16),
                        pltpu.VMEM((rows, LANES), jnp.float32),
                        pltpu.VMEM((rows, ACC_W), jnp.float32)
                        ] + list(extra_scratch),
        compiler_params=pltpu.CompilerParams(dimension_semantics=("arbitrary", "arbitrary", "arbitrary"),
                                             vmem_limit_bytes=VMEM_LIMIT),
        name=name,
    )(qkv, qkv, qkv, *extra_inputs)


def _tab_spec(first_head):
    return pl.BlockSpec((2, N_TAB_TILES, TQ, TK), lambda p, b, i: (first_head // 2 + p, 0, 0, 0))


def _merge_pair(o0, o1):
    return jnp.where(_lane_iota(o0.shape) < HEAD_DIM, o0, o1)


def _dilated_kernel(q_ref, k_ref, v_ref, tab_ref, o_ref, qs_ref, m_ref, acc_ref):
    qi = pl.program_id(2)
    _stack_masked_q(q_ref[...], qs_ref, 2)
    _init_state(m_ref, acc_ref)

    def step(n, width):
        k, v = _kv_tiles(k_ref, v_ref, qi - n - (width - 1), width)
        v_aug = _with_ones(v)
        for st in range(2):
            s = (lax.dot_general(qs_ref[_rows(st), :], k, _NT, preferred_element_type=jnp.float32)
                 + _table_bias(tab_ref, st, n, width, N_DIL_TILES - 1))
            _flash_step(st, s, v_aug, m_ref, acc_ref)

    _visit_tiles(step, jnp.minimum(qi, N_DIL_TILES - 1))
    o_ref[...] = _merge_pair(_normalised(acc_ref, 0), _normalised(acc_ref, 1)).astype(o_ref.dtype)


def _dilated(qkv, tabs, batch, seq):
    return _attn_call(_dilated_kernel, 0, qkv, [tabs], [_tab_spec(0)], 2, batch, seq, name="dilated_attn")


def _diff_kernel(q_ref, k_ref, v_ref, tab_ref, lam_ref, g_ref, o_ref, qs_ref, m_ref, acc_ref, *, lambda_init):
    qi = pl.program_id(2)
    _stack_masked_q(q_ref[...], qs_ref, 4)
    _init_state(m_ref, acc_ref)

    def step(n, width):
        k, v = _kv_tiles(k_ref, v_ref, qi - n - (width - 1), width)
        v_aug = _with_ones(v)
        for st in range(4):
            s = (lax.dot_general(qs_ref[_rows(st), :], k, _NT, preferred_element_type=jnp.float32)
                 + _table_bias(tab_ref, st // 2, n, width, FAR_TILE))
            _flash_step(st, s, v_aug, m_ref, acc_ref)

    _visit_tiles(step, qi)

    lamv = lam_ref[...]
    lam = (jnp.exp(jnp.sum(lamv[0:1] * lamv[1:2], axis=1, keepdims=True))
           - jnp.exp(jnp.sum(lamv[2:3] * lamv[3:4], axis=1, keepdims=True)) + lambda_init)
    o = _merge_pair(_normalised(acc_ref, 0) - lam * _normalised(acc_ref, 1),
                    _normalised(acc_ref, 2) - lam * _normalised(acc_ref, 3))
    first = _lane_iota(o.shape) < HEAD_DIM
    sq = o * o
    ms0 = jnp.sum(jnp.where(first, sq, 0.0), axis=1, keepdims=True)
    ms1 = jnp.sum(jnp.where(first, 0.0, sq), axis=1, keepdims=True)
    ms = jnp.where(first, ms0, ms1) * (1.0 / HEAD_DIM)
    y = (o * lax.rsqrt(ms + RMS_EPS) * g_ref[...]) * (1.0 - lambda_init)
    o_ref[...] = y.astype(o_ref.dtype)


def _diff(qkv, tabs, lamv, g_sub, lambda_init, batch, seq):
    return _attn_call(functools.partial(_diff_kernel, lambda_init=lambda_init), 1, qkv, [tabs, lamv, g_sub],
                      [_tab_spec(4), pl.BlockSpec((8, LANES), lambda p, b, i: (0, 0)),
                       pl.BlockSpec((1, PAIR_W), lambda p, b, i: (0, 0))],
                      4, batch, seq, name="diff_attn")


def _moba_kernel(q_ref, k_ref, v_ref, tab_ref, o_ref, qs_ref, m_ref, acc_ref, km_ref):
    qi = pl.program_id(2)
    n_blk = k_ref.shape[0] // MOBA_BLOCK
    assert n_blk <= LANES

    @pl.when(qi == 0)
    def _():
        km_ref[...] = jnp.zeros(km_ref.shape, jnp.float32)
        for n in range(n_blk):
            km_ref[n:n + 1, :] = jnp.sum(k_ref[n * MOBA_BLOCK:(n + 1) * MOBA_BLOCK, :].astype(jnp.float32),
                                         axis=0, keepdims=True) * (1.0 / MOBA_BLOCK)

    _stack_masked_q(q_ref[...], qs_ref, 2)
    _init_state(m_ref, acc_ref)
    km = km_ref[...]
    km_hi = km.astype(jnp.bfloat16)
    km_lo = (km - km_hi.astype(jnp.float32)).astype(jnp.bfloat16)
    qs = qs_ref[:, 0:PAIR_W]
    gate = (lax.dot_general(qs, km_hi, _NT, preferred_element_type=jnp.float32)
            + lax.dot_general(qs, km_lo, _NT, preferred_element_type=jnp.float32))
    blk = _lane_iota(gate.shape)
    blk_f = blk.astype(jnp.float32)
    g = jnp.where(blk < qi, gate, NEG)
    sel = blk == qi
    for _ in range(MOBA_TOPK):
        mx = jnp.max(g, axis=1, keepdims=True)
        first = jnp.min(jnp.where(g == mx, blk_f, 2.0 * LANES), axis=1, keepdims=True)
        pick = blk_f == first
        sel = jnp.logical_or(sel, jnp.logical_and(pick, mx > 0.5 * NEG))
        g = jnp.where(pick, -3e38, g)
    qs_ref[:, PAIR_W:2 * PAIR_W] = jnp.where(sel, 0.0, NEG).astype(jnp.bfloat16)

    def step(n, width):
        first = qi - n - (width - 1)
        k, v = _kv_tiles(k_ref, v_ref, first, width)
        block_of_key = first + lax.broadcasted_iota(jnp.int32, (width * TK, LANES), 0) // TK
        onehot = jnp.where(_lane_iota((width * TK, LANES)) == block_of_key, 1.0, 0.0).astype(k.dtype)
        k_aug = jnp.concatenate([k, onehot], axis=1)
        v_aug = _with_ones(v)
        for st in range(2):
            s = (lax.dot_general(qs_ref[_rows(st), :], k_aug, _NT, preferred_element_type=jnp.float32)
                 + _table_bias(tab_ref, st, n, width, FAR_TILE))
            _flash_step(st, s, v_aug, m_ref, acc_ref)

    _visit_tiles(step, qi)
    o_ref[...] = _merge_pair(_normalised(acc_ref, 0), _normalised(acc_ref, 1)).astype(o_ref.dtype)


def _moba(qkv, tabs, batch, seq):
    return _attn_call(_moba_kernel, 2, qkv, [tabs], [_tab_spec(8)], 2, batch, seq, q_width=2 * PAIR_W,
                      extra_scratch=[pltpu.VMEM((LANES, PAIR_W), jnp.float32)], name="moba_attn")


def _fox_kernel(q_ref, k_ref, v_ref, c_ref, o_ref, qs_ref, m_ref, acc_ref, cq_ref):
    qi = pl.program_id(2)
    _stack_masked_q(q_ref[...], qs_ref, 2)
    _init_state(m_ref, acc_ref)
    q0 = pl.multiple_of(qi * TQ, TQ)
    row = lax.broadcasted_iota(jnp.int32, (TQ, TK), 0)
    col = lax.broadcasted_iota(jnp.int32, (TQ, TK), 1)
    for hh in range(2):
        c_row = c_ref[hh:hh + 1, pl.ds(q0, TQ)]
        c_col = jnp.sum(jnp.where(row == col, c_row, 0.0), axis=1, keepdims=True)
        cq_ref[hh * TQ:(hh + 1) * TQ, :] = jnp.broadcast_to(c_col, (TQ, LANES))

    def step(n, width, causal=False):
        first = qi - n - (width - 1)
        k0 = pl.multiple_of(first * TK, TK)
        k, v = _kv_tiles(k_ref, v_ref, first, width)
        v_aug = _with_ones(v)
        for hh in range(2):
            cq = cq_ref[_rows(hh), :]
            decay = jnp.concatenate([cq] * (width * TK // LANES), axis=1) - c_ref[hh:hh + 1, pl.ds(k0, width * TK)]
            s = lax.dot_general(qs_ref[_rows(hh), :], k, _NT, preferred_element_type=jnp.float32) + decay
            if causal:
                s = jnp.where(col <= row, s, NEG)
            _flash_step(hh, s, v_aug, m_ref, acc_ref)

    _visit_tiles(lambda n, width: step(n, width, causal=(isinstance(n, int) and n == 0)), qi)
    o_ref[...] = _merge_pair(_normalised(acc_ref, 0), _normalised(acc_ref, 1)).astype(o_ref.dtype)


def _fox(qkv, c, batch, seq):
    return _attn_call(_fox_kernel, 3, qkv, [c.reshape(batch, 2, 8, seq)],
                      [pl.BlockSpec((None, None, 8, seq), lambda p, b, i: (b, p, 0, 0))],
                      2, batch, seq, extra_scratch=[pltpu.VMEM((2 * TQ, LANES), jnp.float32)], name="fox_attn")


OUT_TM = 512
ROUTER_W = LANES


def _outproj_router_kernel(oa_ref, ob_ref, oc_ref, od_ref, wo_ref, x_ref, g_ref, wr_ref, br_ref,
                           x1_ref, h2_ref, comb_ref):
    acc = x_ref[...]
    for m, o_ref in enumerate((oa_ref, ob_ref, oc_ref, od_ref)):
        acc = acc + jnp.dot(o_ref[...], wo_ref[m * GROUP_W:(m + 1) * GROUP_W, :], preferred_element_type=jnp.float32)
    x1_ref[...] = acc
    ms = jnp.mean(acc * acc, axis=1, keepdims=True)
    h = acc * lax.rsqrt(ms + RMS_EPS) * g_ref[...]
    hb = h.astype(jnp.bfloat16)
    h2_ref[...] = hb
    h_lo = (h - hb.astype(jnp.float32)).astype(jnp.bfloat16)
    logits = (jnp.dot(hb, wr_ref[0], preferred_element_type=jnp.float32)
              + jnp.dot(h_lo, wr_ref[0], preferred_element_type=jnp.float32)
              + jnp.dot(hb, wr_ref[1], preferred_element_type=jnp.float32)) + br_ref[...]
    lane = _lane_iota(logits.shape).astype(jnp.float32)
    big = 2.0 * LANES
    gmask = jnp.logical_and(lane >= N_EXPERTS, lane < N_EXPERTS + N_GROUPS)
    gl = jnp.where(gmask, logits, NEG)
    gmax = jnp.max(gl, axis=1, keepdims=True)
    glane = jnp.min(jnp.where(gl == gmax, lane, big), axis=1, keepdims=True)
    gsum = jnp.sum(jnp.where(gmask, jnp.exp(gl - gmax), 0.0), axis=1, keepdims=True)
    g_w = 1.0 / gsum
    e0 = (glane - N_EXPERTS) * EXPERTS_PER_GROUP
    emask = jnp.logical_and(lane >= e0, lane < e0 + EXPERTS_PER_GROUP)
    el = jnp.where(emask, logits, NEG)
    v1 = jnp.max(el, axis=1, keepdims=True)
    i1 = jnp.min(jnp.where(el == v1, lane, big), axis=1, keepdims=True)
    el2 = jnp.where(lane == i1, NEG, el)
    v2 = jnp.max(el2, axis=1, keepdims=True)
    i2 = jnp.min(jnp.where(el2 == v2, lane, big), axis=1, keepdims=True)
    e2 = jnp.exp(v2 - v1)
    den = 1.0 + e2
    comb_ref[...] = jnp.where(lane == i1, g_w / den, jnp.where(lane == i2, g_w * e2 / den, 0.0))


def _outproj_router(o_parts, w_out, x2d, g2, wr, br):
    n_tok = x2d.shape[0]
    tm = min(OUT_TM, n_tok)
    row = lambda i: (i, 0)
    fixed = lambda i: (0, 0)
    return pl.pallas_call(
        _outproj_router_kernel,
        out_shape=(jax.ShapeDtypeStruct((n_tok, D_MODEL), jnp.float32),
                   jax.ShapeDtypeStruct((n_tok, D_MODEL), jnp.bfloat16),
                   jax.ShapeDtypeStruct((n_tok, ROUTER_W), jnp.float32)),
        grid=(n_tok // tm,),
        in_specs=[pl.BlockSpec((tm, GROUP_W), row)] * 4 + [
            pl.BlockSpec((D_MODEL, D_MODEL), fixed),
            pl.BlockSpec((tm, D_MODEL), row),
            pl.BlockSpec((1, D_MODEL), fixed),
            pl.BlockSpec((2, D_MODEL, ROUTER_W), lambda i: (0, 0, 0)),
            pl.BlockSpec((1, ROUTER_W), fixed)],
        out_specs=(pl.BlockSpec((tm, D_MODEL), row), pl.BlockSpec((tm, D_MODEL), row),
                   pl.BlockSpec((tm, ROUTER_W), row)),
        compiler_params=pltpu.CompilerParams(dimension_semantics=("arbitrary",), vmem_limit_bytes=VMEM_LIMIT),
        name="outproj_router",
    )(*o_parts, w_out, x2d, g2, wr, br)


MOE_TM = 1024


def _moe_kernel(h_ref, comb_ref, wg_ref, wu_ref, wd_ref, x1_ref, gf_ref, out_ref, acc_ref, *, final_norm):
    e = pl.program_id(1)

    @pl.when(e == 0)
    def _():
        acc_ref[...] = jnp.zeros(acc_ref.shape, jnp.float32)

    h = h_ref[...]
    gate = jnp.dot(h, wg_ref[...], preferred_element_type=jnp.float32)
    up = jnp.dot(h, wu_ref[...], preferred_element_type=jnp.float32)
    hid = (gate * jax.nn.sigmoid(gate)) * up
    comb = comb_ref[...]
    ce = jnp.sum(jnp.where(_lane_iota(comb.shape) == e, comb, 0.0), axis=1, keepdims=True)
    acc_ref[...] += ce * jnp.dot(hid.astype(jnp.bfloat16), wd_ref[...], preferred_element_type=jnp.float32)

    @pl.when(e == N_EXPERTS - 1)
    def _():
        y = x1_ref[...] + acc_ref[...]
        if final_norm:
            ms = jnp.mean(y * y, axis=1, keepdims=True)
            y = y * lax.rsqrt(ms + RMS_EPS) * gf_ref[...]
        out_ref[...] = y


def _moe(h2, comb, wg, wu, wd, x1, gf, final_norm):
    n_tok = h2.shape[0]
    tm = min(MOE_TM, n_tok)
    row = lambda i, e: (i, 0)
    return pl.pallas_call(
        functools.partial(_moe_kernel, final_norm=final_norm),
        out_shape=jax.ShapeDtypeStruct((n_tok, D_MODEL), jnp.float32),
        grid=(n_tok // tm, N_EXPERTS),
        in_specs=[pl.BlockSpec((tm, D_MODEL), row),
                  pl.BlockSpec((tm, ROUTER_W), row),
                  pl.BlockSpec((None, D_MODEL, D_EXPERT), lambda i, e: (e, 0, 0)),
                  pl.BlockSpec((None, D_MODEL, D_EXPERT), lambda i, e: (e, 0, 0)),
                  pl.BlockSpec((None, D_EXPERT, D_MODEL), lambda i, e: (e, 0, 0)),
                  pl.BlockSpec((tm, D_MODEL), row),
                  pl.BlockSpec((1, D_MODEL), lambda i, e: (0, 0))],
        out_specs=pl.BlockSpec((tm, D_MODEL), row),
        scratch_shapes=[pltpu.VMEM((tm, D_MODEL), jnp.float32)],
        compiler_params=pltpu.CompilerParams(dimension_semantics=("arbitrary", "arbitrary"),
                                             vmem_limit_bytes=VMEM_LIMIT),
        name="moe_experts",
    )(h2, comb, wg, wu, wd, x1, gf)


def _split_bf16(w):
    hi = w.astype(jnp.bfloat16)
    lo = (w - hi.astype(jnp.float32)).astype(jnp.bfloat16)
    return jnp.stack([hi, lo])


def _qkv_col_scale():
    s = np.ones((QKV_W,), np.float32)
    for group, dim in enumerate((HEAD_DIM, DIFF_DIM, HEAD_DIM, HEAD_DIM)):
        s[3 * group * GROUP_W:(3 * group + 1) * GROUP_W] = LOG2E * dim ** -0.5
    return s


def _forget_weights(w_f, b_f):
    rows = jnp.zeros((F_ROWS, D_MODEL), jnp.float32)
    bias = jnp.zeros((F_ROWS, 1), jnp.float32)
    for h in range(HEADS_PER_MIXER):
        r = (h // 2) * 8 + (h % 2)
        rows = rows.at[r].set(w_f[:, h])
        bias = bias.at[r, 0].set(b_f[h])
    return _split_bf16(rows), bias


def kernel(x, rel_bias, ln1, w_in, w_out, lam_q1, lam_k1, lam_q2, lam_k2, subln_g, b_forget,
           ln2, w_group, b_group, w_expert, b_expert, w_gate, w_up, w_down, ln_f):
    batch, seq, _ = x.shape
    depth = ln1.shape[0]
    assert seq % TQ == 0 and seq % MOBA_BLOCK == 0
    tabs = _bias_tables(rel_bias)
    xt = x.reshape(batch * seq, D_MODEL)
    gf = ln_f.reshape(1, D_MODEL)
    col_scale = _qkv_col_scale()
    for l in range(depth):
        lambda_init = 0.8 - 0.6 * math.exp(-0.3 * l)
        wf, bf = _forget_weights(w_in[l][:, QKV_W:], b_forget[l])
        w_qkv = (w_in[l][:, :QKV_W] * col_scale).astype(jnp.bfloat16)
        qkv, lf = _norm_proj(xt, ln1[l].reshape(1, D_MODEL), w_qkv, wf, bf, batch, seq)
        c = _cumsum(lf)

        o_a = _dilated(qkv, tabs, batch, seq)
        lamv = jnp.zeros((8, LANES), jnp.float32)
        lamv = lamv.at[0:4, 0:DIFF_DIM].set(jnp.stack([lam_q1[l], lam_k1[l], lam_q2[l], lam_k2[l]]))
        g_sub = jnp.tile(subln_g[l], 2).reshape(1, PAIR_W)
        o_b = _diff(qkv, tabs, lamv, g_sub, lambda_init, batch, seq)
        o_c = _moba(qkv, tabs, batch, seq)
        o_d = _fox(qkv, c, batch, seq)

        w_router = jnp.zeros((D_MODEL, ROUTER_W), jnp.float32)
        w_router = w_router.at[:, :N_EXPERTS].set(w_expert[l]).at[:, N_EXPERTS:N_EXPERTS + N_GROUPS].set(w_group[l])
        b_router = jnp.zeros((1, ROUTER_W), jnp.float32)
        b_router = b_router.at[0, :N_EXPERTS].set(b_expert[l].reshape(-1)).at[0, N_EXPERTS:N_EXPERTS + N_GROUPS].set(b_group[l])
        x1, h2, comb = _outproj_router([o_a, o_b, o_c, o_d], w_out[l].astype(jnp.bfloat16), xt,
                                       ln2[l].reshape(1, D_MODEL), _split_bf16(w_router), b_router)
        xt = _moe(h2, comb, w_gate[l].astype(jnp.bfloat16), w_up[l].astype(jnp.bfloat16),
                  w_down[l].astype(jnp.bfloat16), x1, gf, final_norm=(l == depth - 1))
    return xt.reshape(batch, seq, D_MODEL)
```

```python
import functools
import math

import jax
import jax.numpy as jnp
import numpy as np
from jax import lax
from jax.experimental import pallas as pl
from jax.experimental.pallas import tpu as pltpu

D_MODEL = 1024
HEAD_DIM = 64
HEADS_PER_MIXER = 4
GROUP_W = HEADS_PER_MIXER * HEAD_DIM
QKV_W = 12 * GROUP_W
DIFF_DIM = HEAD_DIM // 2
MOBA_BLOCK = 256
MOBA_TOPK = 3
N_BUCKETS = 32
MAX_DISTANCE = 2048
N_BIAS_HEADS = 12
N_GROUPS = 4
EXPERTS_PER_GROUP = 8
N_EXPERTS = 32
D_EXPERT = 256
RMS_EPS = 1e-6
NEG = -1e30
LOG2E = math.log2(math.e)

LANES = 128
PAIR_W = 2 * HEAD_DIM
ACC_W = 2 * PAIR_W
TQ = 256
TK = 256
N_DIL_TILES = MAX_DISTANCE // TK + 1
FAR_TILE = 7
N_TAB_TILES = max(N_DIL_TILES, FAR_TILE + 1)
VMEM_LIMIT = 48 * 1024 * 1024

_NT = (((1,), (1,)), ((), ()))


def _t5_thresholds():
    d = np.arange(0, 4 * MAX_DISTANCE, dtype=np.int64)
    max_exact = N_BUCKETS // 2
    df = np.maximum(d, 1).astype(np.float32)
    large = max_exact + (np.log(df / np.float32(max_exact)) / np.float32(math.log(MAX_DISTANCE / max_exact))
                         * np.float32(N_BUCKETS - max_exact)).astype(np.int32)
    bucket = np.where(d < max_exact, d, np.minimum(large, N_BUCKETS - 1))
    return [int(np.argmax(bucket >= b)) for b in range(N_BUCKETS)]


_T5_THR = _t5_thresholds()
assert (FAR_TILE - 1) * TK + 1 >= _T5_THR[N_BUCKETS - 1]


def _lane_iota(shape):
    return lax.broadcasted_iota(jnp.int32, shape, len(shape) - 1)


def _bias_table_kernel(bias_ref, out_ref):
    h = pl.program_id(0)
    n = pl.program_id(1)
    i = lax.broadcasted_iota(jnp.int32, (TQ, TK), 0)
    j = lax.broadcasted_iota(jnp.int32, (TQ, TK), 1)
    d = n * TK + i - j
    is_dil = h < HEADS_PER_MIXER
    far = jnp.logical_and(jnp.logical_not(is_dil), n >= FAR_TILE)
    d_b = jnp.where(far, 4 * MAX_DISTANCE, d)
    val = jnp.full((TQ, TK), bias_ref[h, 0], jnp.float32)
    for b in range(1, N_BUCKETS):
        val = jnp.where(d_b >= _T5_THR[b], bias_ref[h, b], val)
    mult = ((d <= 128).astype(jnp.int32)
            + jnp.logical_and((d & 3) == 0, d <= 512).astype(jnp.int32)
            + jnp.logical_and((d & 15) == 0, d <= 2048).astype(jnp.int32))
    logm = jnp.where(mult == 3, math.log(3.0), jnp.where(mult == 2, math.log(2.0), 0.0))
    val = val + jnp.where(is_dil, logm, 0.0)
    valid = jnp.logical_and(d >= 0, jnp.logical_or(mult > 0, jnp.logical_not(is_dil)))
    out_ref[...] = jnp.where(valid, val * LOG2E, NEG)


def _bias_tables(rel_bias):
    bias_h = rel_bias.T.astype(jnp.float32)
    return pl.pallas_call(
        _bias_table_kernel,
        out_shape=jax.ShapeDtypeStruct((N_BIAS_HEADS, N_TAB_TILES, TQ, TK), jnp.float32),
        grid=(N_BIAS_HEADS, N_TAB_TILES),
        in_specs=[pl.BlockSpec(memory_space=pltpu.SMEM)],
        out_specs=pl.BlockSpec((None, None, TQ, TK), lambda h, n: (h, n, 0, 0)),
        name="bias_tables",
    )(bias_h)


PROJ_TM = 512
PROJ_CW = 512
F_ROWS = 16


def _norm_proj_kernel(x_ref, g_ref, w_ref, wf_ref, bf_ref, qkv_ref, lf_ref):
    x = x_ref[...]
    ms = jnp.mean(x * x, axis=1, keepdims=True)
    h = x * lax.rsqrt(ms + RMS_EPS) * g_ref[...]
    hb = h.astype(jnp.bfloat16)
    for c in range(QKV_W // PROJ_CW):
        cols = slice(c * PROJ_CW, (c + 1) * PROJ_CW)
        qkv_ref[:, cols] = jnp.dot(hb, w_ref[:, cols], preferred_element_type=jnp.float32).astype(jnp.bfloat16)
    h_lo = (h - hb.astype(jnp.float32)).astype(jnp.bfloat16)
    wf_hi = wf_ref[0]
    wf_lo = wf_ref[1]
    z = (lax.dot_general(wf_hi, hb, _NT, preferred_element_type=jnp.float32)
         + lax.dot_general(wf_hi, h_lo, _NT, preferred_element_type=jnp.float32)
         + lax.dot_general(wf_lo, hb, _NT, preferred_element_type=jnp.float32))
    z = z + bf_ref[...]
    lf_ref[...] = jnp.minimum(z, 0.0) - jnp.log(1.0 + jnp.exp(-jnp.abs(z)))


def _norm_proj(x2d, g, w_qkv, wf, bf, batch, seq):
    n_tok = x2d.shape[0]
    tm = min(PROJ_TM, seq)
    per_b = seq // tm
    return pl.pallas_call(
        _norm_proj_kernel,
        out_shape=(jax.ShapeDtypeStruct((n_tok, QKV_W), jnp.bfloat16),
                   jax.ShapeDtypeStruct((batch * F_ROWS, seq), jnp.float32)),
        grid=(n_tok // tm,),
        in_specs=[pl.BlockSpec((tm, D_MODEL), lambda i: (i, 0)),
                  pl.BlockSpec((1, D_MODEL), lambda i: (0, 0)),
                  pl.BlockSpec((D_MODEL, QKV_W), lambda i: (0, 0)),
                  pl.BlockSpec((2, F_ROWS, D_MODEL), lambda i: (0, 0, 0)),
                  pl.BlockSpec((F_ROWS, 1), lambda i: (0, 0))],
        out_specs=(pl.BlockSpec((tm, QKV_W), lambda i: (i, 0)),
                   pl.BlockSpec((F_ROWS, tm), lambda i: (i // per_b, i % per_b))),
        compiler_params=pltpu.CompilerParams(dimension_semantics=("arbitrary",), vmem_limit_bytes=VMEM_LIMIT),
        name="norm_proj",
    )(x2d, g, w_qkv, wf, bf)


def _split3(x):
    x1 = x.astype(jnp.bfloat16)
    r1 = x - x1.astype(jnp.float32)
    x2 = r1.astype(jnp.bfloat16)
    x3 = (r1 - x2.astype(jnp.float32)).astype(jnp.bfloat16)
    return x1, x2, x3


def _cumsum_kernel(lf_ref, c_ref):
    rows, seq = lf_ref.shape
    r = lax.broadcasted_iota(jnp.int32, (LANES, LANES), 0)
    c = lax.broadcasted_iota(jnp.int32, (LANES, LANES), 1)
    upper = jnp.where(r <= c, 1.0, 0.0).astype(jnp.bfloat16)

    def body(i, carry):
        off = pl.multiple_of(i * LANES, LANES)
        x1, x2, x3 = _split3(lf_ref[:, pl.ds(off, LANES)])
        cs = (jnp.dot(x1, upper, preferred_element_type=jnp.float32)
              + jnp.dot(x2, upper, preferred_element_type=jnp.float32)
              + jnp.dot(x3, upper, preferred_element_type=jnp.float32)) + carry
        c_ref[:, pl.ds(off, LANES)] = cs * LOG2E
        return cs[:, LANES - 1:LANES]

    lax.fori_loop(0, seq // LANES, body, jnp.zeros((rows, 1), jnp.float32))


def _cumsum(lf):
    return pl.pallas_call(
        _cumsum_kernel,
        out_shape=jax.ShapeDtypeStruct(lf.shape, jnp.float32),
        name="forget_cumsum",
    )(lf)


def _stack_masked_q(q, qs_ref, n_streams):
    lane = _lane_iota(q.shape)
    width = PAIR_W // n_streams
    for st in range(n_streams):
        qs_ref[st * TQ:(st + 1) * TQ, 0:PAIR_W] = jnp.where(lane // width == st, q, jnp.zeros_like(q))


def _init_state(m_ref, acc_ref):
    m_ref[...] = jnp.full(m_ref.shape, NEG, jnp.float32)
    acc_ref[...] = jnp.zeros(acc_ref.shape, jnp.float32)


def _rows(st):
    return slice(st * TQ, (st + 1) * TQ)


def _flash_step(st, s, v_aug, m_ref, acc_ref):
    rows = _rows(st)
    m_prev = m_ref[rows, :]
    m_new = jnp.maximum(m_prev, jnp.max(s, axis=1, keepdims=True))
    alpha = jnp.exp2(m_prev - m_new)
    p = jnp.concatenate([jnp.exp2(s[:, c * LANES:(c + 1) * LANES] - m_new) for c in range(s.shape[1] // LANES)],
                        axis=1).astype(v_aug.dtype)
    pv = jnp.dot(p, v_aug, preferred_element_type=jnp.float32)
    acc_ref[rows, :] = jnp.concatenate([alpha, alpha], axis=1) * acc_ref[rows, :] + pv
    m_ref[rows, :] = m_new


def _with_ones(v):
    return jnp.concatenate([v, jnp.ones_like(v)], axis=1)


def _normalised(acc_ref, st):
    return acc_ref[_rows(st), 0:PAIR_W] / acc_ref[_rows(st), PAIR_W:ACC_W]


def _table_bias(tab_ref, hh, n, width, cap):
    return jnp.concatenate([tab_ref[hh, jnp.minimum(n + width - 1 - w, cap)] for w in range(width)], axis=1)


def _tile_rows(ref, first_tile, width):
    k0 = pl.multiple_of(first_tile * TK, TK)
    return ref[pl.ds(k0, width * TK), :]


def _make_attend(n_streams, qi, qs_ref, v_ref, m_ref, acc_ref, s_ref, keys, finish):
    def raw(st, rhs):
        return lax.dot_general(qs_ref[_rows(st), :], rhs, _NT, preferred_element_type=jnp.float32)

    def single(n):
        rhs = keys(qi - n, 1)
        v_aug = _with_ones(_tile_rows(v_ref, qi - n, 1))
        for st in range(n_streams):
            _flash_step(st, finish(st, raw(st, rhs), n, 1), v_aug, m_ref, acc_ref)

    def past(n_past):
        odd = n_past % 2
        n_pairs = n_past // 2
        start = 1 + odd

        @pl.when(odd == 1)
        def _():
            single(1)

        @pl.when(n_pairs > 0)
        def _():
            rhs = keys(qi - start - 1, 2)
            for st in range(n_streams):
                s_ref[_rows(st), :] = raw(st, rhs)

        def body(i, carry):
            n = start + 2 * i
            v_aug = _with_ones(_tile_rows(v_ref, qi - n - 1, 2))
            rhs_next = keys(jnp.maximum(qi - n - 3, 0), 2)
            for st in range(n_streams):
                cur = s_ref[_rows(st), :]
                s_ref[_rows(st), :] = raw(st, rhs_next)
                _flash_step(st, finish(st, cur, n, 2), v_aug, m_ref, acc_ref)
            return carry

        lax.fori_loop(0, n_pairs, body, 0)

    return (lambda: single(0)), past


def _attn_call(kernel, group, qkv, extra_inputs, extra_specs, n_streams, batch, seq, *, q_width=PAIR_W,
               extra_scratch=(), name=None):
    nq = seq // TQ
    qc, kc, vc = 6 * group, 6 * group + 2, 6 * group + 4
    rows = n_streams * TQ
    return pl.pallas_call(
        kernel,
        out_shape=jax.ShapeDtypeStruct((batch * seq, GROUP_W), jnp.bfloat16),
        grid=(2, batch, nq),
        in_specs=[pl.BlockSpec((TQ, PAIR_W), lambda p, b, i: (b * nq + i, qc + p)),
                  pl.BlockSpec((seq, PAIR_W), lambda p, b, i: (b, kc + p)),
                  pl.BlockSpec((seq, PAIR_W), lambda p, b, i: (b, vc + p))] + list(extra_specs),
        out_specs=pl.BlockSpec((TQ, PAIR_W), lambda p, b, i: (b * nq + i, p)),
        scratch_shapes=[pltpu.VMEM((rows, q_width), jnp.bfloat16),
                        pltpu.VMEM((rows, LANES), jnp.float32),
                        pltpu.VMEM((rows, ACC_W), jnp.float32),
                        pltpu.VMEM((rows, 2 * TK), jnp.float32)
                        ] + list(extra_scratch),
        compiler_params=pltpu.CompilerParams(dimension_semantics=("arbitrary", "arbitrary", "arbitrary"),
                                             vmem_limit_bytes=VMEM_LIMIT),
        name=name,
    )(qkv, qkv, qkv, *extra_inputs)


def _tab_spec(first_head):
    return pl.BlockSpec((2, N_TAB_TILES, TQ, TK), lambda p, b, i: (first_head // 2 + p, 0, 0, 0))


def _merge_pair(o0, o1):
    return jnp.where(_lane_iota(o0.shape) < HEAD_DIM, o0, o1)


def _dilated_kernel(q_ref, k_ref, v_ref, tab_ref, o_ref, qs_ref, m_ref, acc_ref, s_ref):
    qi = pl.program_id(2)
    _stack_masked_q(q_ref[...], qs_ref, 2)
    _init_state(m_ref, acc_ref)
    own, past = _make_attend(
        2, qi, qs_ref, v_ref, m_ref, acc_ref, s_ref,
        keys=lambda first, width: _tile_rows(k_ref, first, width),
        finish=lambda st, raw, n, width: raw + _table_bias(tab_ref, st, n, width, N_DIL_TILES - 1))
    own()
    past(jnp.minimum(qi, N_DIL_TILES - 1))
    o_ref[...] = _merge_pair(_normalised(acc_ref, 0), _normalised(acc_ref, 1)).astype(o_ref.dtype)


def _dilated(qkv, tabs, batch, seq):
    return _attn_call(_dilated_kernel, 0, qkv, [tabs], [_tab_spec(0)], 2, batch, seq, name="dilated_attn")


def _diff_kernel(q_ref, k_ref, v_ref, tab_ref, lam_ref, g_ref, o_ref, qs_ref, m_ref, acc_ref, s_ref, *,
                 lambda_init):
    qi = pl.program_id(2)
    _stack_masked_q(q_ref[...], qs_ref, 4)
    _init_state(m_ref, acc_ref)
    own, past = _make_attend(
        4, qi, qs_ref, v_ref, m_ref, acc_ref, s_ref,
        keys=lambda first, width: _tile_rows(k_ref, first, width),
        finish=lambda st, raw, n, width: raw + _table_bias(tab_ref, st // 2, n, width, FAR_TILE))
    own()
    past(qi)

    lamv = lam_ref[...]
    lam = (jnp.exp(jnp.sum(lamv[0:1] * lamv[1:2], axis=1, keepdims=True))
           - jnp.exp(jnp.sum(lamv[2:3] * lamv[3:4], axis=1, keepdims=True)) + lambda_init)
    o = _merge_pair(_normalised(acc_ref, 0) - lam * _normalised(acc_ref, 1),
                    _normalised(acc_ref, 2) - lam * _normalised(acc_ref, 3))
    first = _lane_iota(o.shape) < HEAD_DIM
    sq = o * o
    ms0 = jnp.sum(jnp.where(first, sq, 0.0), axis=1, keepdims=True)
    ms1 = jnp.sum(jnp.where(first, 0.0, sq), axis=1, keepdims=True)
    ms = jnp.where(first, ms0, ms1) * (1.0 / HEAD_DIM)
    y = (o * lax.rsqrt(ms + RMS_EPS) * g_ref[...]) * (1.0 - lambda_init)
    o_ref[...] = y.astype(o_ref.dtype)


def _diff(qkv, tabs, lamv, g_sub, lambda_init, batch, seq):
    return _attn_call(functools.partial(_diff_kernel, lambda_init=lambda_init), 1, qkv, [tabs, lamv, g_sub],
                      [_tab_spec(4), pl.BlockSpec((8, LANES), lambda p, b, i: (0, 0)),
                       pl.BlockSpec((1, PAIR_W), lambda p, b, i: (0, 0))],
                      4, batch, seq, name="diff_attn")


def _moba_kernel(q_ref, k_ref, v_ref, tab_ref, o_ref, qs_ref, m_ref, acc_ref, s_ref, km_ref):
    qi = pl.program_id(2)
    n_blk = k_ref.shape[0] // MOBA_BLOCK
    assert n_blk <= LANES

    @pl.when(qi == 0)
    def _():
        km_ref[...] = jnp.zeros(km_ref.shape, jnp.float32)
        for n in range(n_blk):
            km_ref[n:n + 1, :] = jnp.sum(k_ref[n * MOBA_BLOCK:(n + 1) * MOBA_BLOCK, :].astype(jnp.float32),
                                         axis=0, keepdims=True) * (1.0 / MOBA_BLOCK)

    _stack_masked_q(q_ref[...], qs_ref, 2)
    _init_state(m_ref, acc_ref)
    km = km_ref[...]
    km_hi = km.astype(jnp.bfloat16)
    km_lo = (km - km_hi.astype(jnp.float32)).astype(jnp.bfloat16)
    qs = qs_ref[:, 0:PAIR_W]
    gate = (lax.dot_general(qs, km_hi, _NT, preferred_element_type=jnp.float32)
            + lax.dot_general(qs, km_lo, _NT, preferred_element_type=jnp.float32))
    blk = _lane_iota(gate.shape)
    blk_f = blk.astype(jnp.float32)
    g = jnp.where(blk < qi, gate, NEG)
    sel = blk == qi
    for _ in range(MOBA_TOPK):
        mx = jnp.max(g, axis=1, keepdims=True)
        first = jnp.min(jnp.where(g == mx, blk_f, 2.0 * LANES), axis=1, keepdims=True)
        pick = blk_f == first
        sel = jnp.logical_or(sel, jnp.logical_and(pick, mx > 0.5 * NEG))
        g = jnp.where(pick, -3e38, g)
    qs_ref[:, PAIR_W:2 * PAIR_W] = jnp.where(sel, 0.0, NEG).astype(jnp.bfloat16)

    def keys(first, width):
        k = _tile_rows(k_ref, first, width)
        block_of_key = first + lax.broadcasted_iota(jnp.int32, (width * TK, LANES), 0) // TK
        onehot = jnp.where(_lane_iota((width * TK, LANES)) == block_of_key, 1.0, 0.0).astype(k.dtype)
        return jnp.concatenate([k, onehot], axis=1)

    own, past = _make_attend(
        2, qi, qs_ref, v_ref, m_ref, acc_ref, s_ref, keys=keys,
        finish=lambda st, raw, n, width: raw + _table_bias(tab_ref, st, n, width, FAR_TILE))
    own()
    past(qi)
    o_ref[...] = _merge_pair(_normalised(acc_ref, 0), _normalised(acc_ref, 1)).astype(o_ref.dtype)


def _moba(qkv, tabs, batch, seq):
    return _attn_call(_moba_kernel, 2, qkv, [tabs], [_tab_spec(8)], 2, batch, seq, q_width=2 * PAIR_W,
                      extra_scratch=[pltpu.VMEM((LANES, PAIR_W), jnp.float32)], name="moba_attn")


FOX_SKIP_BITS = 48.0


def _head_row_norms(x, hh):
    xf = x.astype(jnp.float32)
    sq = jnp.where(_lane_iota(xf.shape) // HEAD_DIM == hh, xf * xf, 0.0)
    return jnp.sqrt(jnp.max(jnp.sum(sq, axis=1, keepdims=True), axis=0, keepdims=True))


def _fox_kernel(q_ref, k_ref, v_ref, c_ref, o_ref, qs_ref, m_ref, acc_ref, s_ref, cq_ref, stat_ref):
    qi = pl.program_id(2)
    n_tiles = k_ref.shape[0] // TK
    assert n_tiles <= LANES

    @pl.when(qi == 0)
    def _():
        stat_ref[...] = jnp.zeros(stat_ref.shape, jnp.float32)
        for j in range(n_tiles):
            k_tile = k_ref[j * TK:(j + 1) * TK, :]
            for hh in range(2):
                stat_ref[hh:hh + 1, j:j + 1] = _head_row_norms(k_tile, hh)
                stat_ref[2 + hh:3 + hh, j:j + 1] = jnp.min(c_ref[hh:hh + 1, j * TK:(j + 1) * TK], axis=1, keepdims=True)

    q = q_ref[...]
    _stack_masked_q(q, qs_ref, 2)
    _init_state(m_ref, acc_ref)
    q0 = pl.multiple_of(qi * TQ, TQ)
    row = lax.broadcasted_iota(jnp.int32, (TQ, TK), 0)
    col = lax.broadcasted_iota(jnp.int32, (TQ, TK), 1)
    for hh in range(2):
        c_row = c_ref[hh:hh + 1, pl.ds(q0, TQ)]
        c_col = jnp.sum(jnp.where(row == col, c_row, 0.0), axis=1, keepdims=True)
        cq_ref[hh * TQ:(hh + 1) * TQ, :] = jnp.broadcast_to(c_col, (TQ, LANES))

    def finish(hh, raw, n, width):
        k0 = pl.multiple_of((qi - n - (width - 1)) * TK, TK)
        cq = cq_ref[_rows(hh), :]
        decay = jnp.concatenate([cq] * (width * TK // LANES), axis=1) - c_ref[hh:hh + 1, pl.ds(k0, width * TK)]
        s = raw + decay
        if isinstance(n, int) and n == 0:
            s = jnp.where(col <= row, s, NEG)
        return s

    own, past = _make_attend(2, qi, qs_ref, v_ref, m_ref, acc_ref, s_ref,
                             keys=lambda first, width: _tile_rows(k_ref, first, width), finish=finish)
    own()

    tile = _lane_iota((1, LANES))
    needed = jnp.zeros((1, LANES), jnp.bool_)
    for hh in range(2):
        m_min = jnp.min(m_ref[_rows(hh), :], axis=0, keepdims=True)
        cq_max = jnp.max(cq_ref[_rows(hh), :], axis=0, keepdims=True)
        bound = _head_row_norms(q, hh) * stat_ref[hh:hh + 1, :] + cq_max - stat_ref[2 + hh:3 + hh, :] + 1.0
        needed = jnp.logical_or(needed, bound >= m_min - FOX_SKIP_BITS)
    first_needed = jnp.min(jnp.where(jnp.logical_and(needed, tile < qi), tile, qi))
    past(qi - first_needed)
    o_ref[...] = _merge_pair(_normalised(acc_ref, 0), _normalised(acc_ref, 1)).astype(o_ref.dtype)


def _fox(qkv, c, batch, seq):
    return _attn_call(_fox_kernel, 3, qkv, [c.reshape(batch, 2, 8, seq)],
                      [pl.BlockSpec((None, None, 8, seq), lambda p, b, i: (b, p, 0, 0))],
                      2, batch, seq, extra_scratch=[pltpu.VMEM((2 * TQ, LANES), jnp.float32),
                                                    pltpu.VMEM((8, LANES), jnp.float32)], name="fox_attn")


OUT_TM = 512
ROUTER_W = LANES


def _outproj_router_kernel(oa_ref, ob_ref, oc_ref, od_ref, wo_ref, x_ref, g_ref, wr_ref, br_ref,
                           x1_ref, h2_ref, comb_ref):
    acc = x_ref[...]
    for m, o_ref in enumerate((oa_ref, ob_ref, oc_ref, od_ref)):
        acc = acc + jnp.dot(o_ref[...], wo_ref[m * GROUP_W:(m + 1) * GROUP_W, :], preferred_element_type=jnp.float32)
    x1_ref[...] = acc
    ms = jnp.mean(acc * acc, axis=1, keepdims=True)
    h = acc * lax.rsqrt(ms + RMS_EPS) * g_ref[...]
    hb = h.astype(jnp.bfloat16)
    h2_ref[...] = hb
    h_lo = (h - hb.astype(jnp.float32)).astype(jnp.bfloat16)
    logits = (jnp.dot(hb, wr_ref[0], preferred_element_type=jnp.float32)
              + jnp.dot(h_lo, wr_ref[0], preferred_element_type=jnp.float32)
              + jnp.dot(hb, wr_ref[1], preferred_element_type=jnp.float32)) + br_ref[...]
    lane = _lane_iota(logits.shape).astype(jnp.float32)
    big = 2.0 * LANES
    gmask = jnp.logical_and(lane >= N_EXPERTS, lane < N_EXPERTS + N_GROUPS)
    gl = jnp.where(gmask, logits, NEG)
    gmax = jnp.max(gl, axis=1, keepdims=True)
    glane = jnp.min(jnp.where(gl == gmax, lane, big), axis=1, keepdims=True)
    gsum = jnp.sum(jnp.where(gmask, jnp.exp(gl - gmax), 0.0), axis=1, keepdims=True)
    g_w = 1.0 / gsum
    e0 = (glane - N_EXPERTS) * EXPERTS_PER_GROUP
    emask = jnp.logical_and(lane >= e0, lane < e0 + EXPERTS_PER_GROUP)
    el = jnp.where(emask, logits, NEG)
    v1 = jnp.max(el, axis=1, keepdims=True)
    i1 = jnp.min(jnp.where(el == v1, lane, big), axis=1, keepdims=True)
    el2 = jnp.where(lane == i1, NEG, el)
    v2 = jnp.max(el2, axis=1, keepdims=True)
    i2 = jnp.min(jnp.where(el2 == v2, lane, big), axis=1, keepdims=True)
    e2 = jnp.exp(v2 - v1)
    den = 1.0 + e2
    comb_ref[...] = jnp.where(lane == i1, g_w / den, jnp.where(lane == i2, g_w * e2 / den, 0.0))


def _outproj_router(o_parts, w_out, x2d, g2, wr, br):
    n_tok = x2d.shape[0]
    tm = min(OUT_TM, n_tok)
    row = lambda i: (i, 0)
    fixed = lambda i: (0, 0)
    return pl.pallas_call(
        _outproj_router_kernel,
        out_shape=(jax.ShapeDtypeStruct((n_tok, D_MODEL), jnp.float32),
                   jax.ShapeDtypeStruct((n_tok, D_MODEL), jnp.bfloat16),
                   jax.ShapeDtypeStruct((n_tok, ROUTER_W), jnp.float32)),
        grid=(n_tok // tm,),
        in_specs=[pl.BlockSpec((tm, GROUP_W), row)] * 4 + [
            pl.BlockSpec((D_MODEL, D_MODEL), fixed),
            pl.BlockSpec((tm, D_MODEL), row),
            pl.BlockSpec((1, D_MODEL), fixed),
            pl.BlockSpec((2, D_MODEL, ROUTER_W), lambda i: (0, 0, 0)),
            pl.BlockSpec((1, ROUTER_W), fixed)],
        out_specs=(pl.BlockSpec((tm, D_MODEL), row), pl.BlockSpec((tm, D_MODEL), row),
                   pl.BlockSpec((tm, ROUTER_W), row)),
        compiler_params=pltpu.CompilerParams(dimension_semantics=("arbitrary",), vmem_limit_bytes=VMEM_LIMIT),
        name="outproj_router",
    )(*o_parts, w_out, x2d, g2, wr, br)


MOE_TM = 1024


def _moe_kernel(h_ref, comb_ref, wg_ref, wu_ref, wd_ref, x1_ref, gf_ref, out_ref, acc_ref, *, final_norm):
    e = pl.program_id(1)

    @pl.when(e == 0)
    def _():
        acc_ref[...] = jnp.zeros(acc_ref.shape, jnp.float32)

    h = h_ref[...]
    gate = jnp.dot(h, wg_ref[...], preferred_element_type=jnp.float32)
    up = jnp.dot(h, wu_ref[...], preferred_element_type=jnp.float32)
    hid = (gate * jax.nn.sigmoid(gate)) * up
    comb = comb_ref[...]
    ce = jnp.sum(jnp.where(_lane_iota(comb.shape) == e, comb, 0.0), axis=1, keepdims=True)
    acc_ref[...] += ce * jnp.dot(hid.astype(jnp.bfloat16), wd_ref[...], preferred_element_type=jnp.float32)

    @pl.when(e == N_EXPERTS - 1)
    def _():
        y = x1_ref[...] + acc_ref[...]
        if final_norm:
            ms = jnp.mean(y * y, axis=1, keepdims=True)
            y = y * lax.rsqrt(ms + RMS_EPS) * gf_ref[...]
        out_ref[...] = y


def _moe(h2, comb, wg, wu, wd, x1, gf, final_norm):
    n_tok = h2.shape[0]
    tm = min(MOE_TM, n_tok)
    row = lambda i, e: (i, 0)
    return pl.pallas_call(
        functools.partial(_moe_kernel, final_norm=final_norm),
        out_shape=jax.ShapeDtypeStruct((n_tok, D_MODEL), jnp.float32),
        grid=(n_tok // tm, N_EXPERTS),
        in_specs=[pl.BlockSpec((tm, D_MODEL), row),
                  pl.BlockSpec((tm, ROUTER_W), row),
                  pl.BlockSpec((None, D_MODEL, D_EXPERT), lambda i, e: (e, 0, 0)),
                  pl.BlockSpec((None, D_MODEL, D_EXPERT), lambda i, e: (e, 0, 0)),
                  pl.BlockSpec((None, D_EXPERT, D_MODEL), lambda i, e: (e, 0, 0)),
                  pl.BlockSpec((tm, D_MODEL), row),
                  pl.BlockSpec((1, D_MODEL), lambda i, e: (0, 0))],
        out_specs=pl.BlockSpec((tm, D_MODEL), row),
        scratch_shapes=[pltpu.VMEM((tm, D_MODEL), jnp.float32)],
        compiler_params=pltpu.CompilerParams(dimension_semantics=("arbitrary", "arbitrary"),
                                             vmem_limit_bytes=VMEM_LIMIT),
        name="moe_experts",
    )(h2, comb, wg, wu, wd, x1, gf)


def _split_bf16(w):
    hi = w.astype(jnp.bfloat16)
    lo = (w - hi.astype(jnp.float32)).astype(jnp.bfloat16)
    return jnp.stack([hi, lo])


def _qkv_col_scale():
    s = np.ones((QKV_W,), np.float32)
    for group, dim in enumerate((HEAD_DIM, DIFF_DIM, HEAD_DIM, HEAD_DIM)):
        s[3 * group * GROUP_W:(3 * group + 1) * GROUP_W] = LOG2E * dim ** -0.5
    return s


def _forget_weights(w_f, b_f):
    rows = jnp.zeros((F_ROWS, D_MODEL), jnp.float32)
    bias = jnp.zeros((F_ROWS, 1), jnp.float32)
    for h in range(HEADS_PER_MIXER):
        r = (h // 2) * 8 + (h % 2)
        rows = rows.at[r].set(w_f[:, h])
        bias = bias.at[r, 0].set(b_f[h])
    return _split_bf16(rows), bias


def kernel(x, rel_bias, ln1, w_in, w_out, lam_q1, lam_k1, lam_q2, lam_k2, subln_g, b_forget,
           ln2, w_group, b_group, w_expert, b_expert, w_gate, w_up, w_down, ln_f):
    batch, seq, _ = x.shape
    depth = ln1.shape[0]
    assert seq % TQ == 0 and seq % MOBA_BLOCK == 0
    tabs = _bias_tables(rel_bias)
    xt = x.reshape(batch * seq, D_MODEL)
    gf = ln_f.reshape(1, D_MODEL)
    col_scale = _qkv_col_scale()
    for l in range(depth):
        lambda_init = 0.8 - 0.6 * math.exp(-0.3 * l)
        wf, bf = _forget_weights(w_in[l][:, QKV_W:], b_forget[l])
        w_qkv = (w_in[l][:, :QKV_W] * col_scale).astype(jnp.bfloat16)
        qkv, lf = _norm_proj(xt, ln1[l].reshape(1, D_MODEL), w_qkv, wf, bf, batch, seq)
        c = _cumsum(lf)

        o_a = _dilated(qkv, tabs, batch, seq)
        lamv = jnp.zeros((8, LANES), jnp.float32)
        lamv = lamv.at[0:4, 0:DIFF_DIM].set(jnp.stack([lam_q1[l], lam_k1[l], lam_q2[l], lam_k2[l]]))
        g_sub = jnp.tile(subln_g[l], 2).reshape(1, PAIR_W)
        o_b = _diff(qkv, tabs, lamv, g_sub, lambda_init, batch, seq)
        o_c = _moba(qkv, tabs, batch, seq)
        o_d = _fox(qkv, c, batch, seq)

        w_router = jnp.zeros((D_MODEL, ROUTER_W), jnp.float32)
        w_router = w_router.at[:, :N_EXPERTS].set(w_expert[l]).at[:, N_EXPERTS:N_EXPERTS + N_GROUPS].set(w_group[l])
        b_router = jnp.zeros((1, ROUTER_W), jnp.float32)
        b_router = b_router.at[0, :N_EXPERTS].set(b_expert[l].reshape(-1)).at[0, N_EXPERTS:N_EXPERTS + N_GROUPS].set(b_group[l])
        x1, h2, comb = _outproj_router([o_a, o_b, o_c, o_d], w_out[l].astype(jnp.bfloat16), xt,
                                       ln2[l].reshape(1, D_MODEL), _split_bf16(w_router), b_router)
        xt = _moe(h2, comb, w_gate[l].astype(jnp.bfloat16), w_up[l].astype(jnp.bfloat16),
                  w_down[l].astype(jnp.bfloat16), x1, gf, final_norm=(l == depth - 1))
    return xt.reshape(batch, seq, D_MODEL)
```

```python
import functools
import math

import jax
import jax.numpy as jnp
import numpy as np
from jax import lax
from jax.experimental import pallas as pl
from jax.experimental.pallas import tpu as pltpu

D_MODEL = 1024
HEAD_DIM = 64
HEADS_PER_MIXER = 4
GROUP_W = HEADS_PER_MIXER * HEAD_DIM
QKV_W = 12 * GROUP_W
DIFF_DIM = HEAD_DIM // 2
MOBA_BLOCK = 256
MOBA_TOPK = 3
N_BUCKETS = 32
MAX_DISTANCE = 2048
N_BIAS_HEADS = 12
N_GROUPS = 4
EXPERTS_PER_GROUP = 8
N_EXPERTS = 32
D_EXPERT = 256
RMS_EPS = 1e-6
NEG = -1e30
LOG2E = math.log2(math.e)

LANES = 128
PAIR_W = 2 * HEAD_DIM
ACC_W = 2 * PAIR_W
TQ = 256
TK = 256
N_SUB = 2
QBLK = N_SUB * TQ
STEP_TILES = 2
STEP_KEYS = STEP_TILES * TK
N_DIL_TILES = MAX_DISTANCE // TK + 1
FAR_TILE = 7
MIN_DIST = -1
N_TAB_TILES = N_DIL_TILES + 1 - MIN_DIST
VMEM_LIMIT = 48 * 1024 * 1024

_NT = (((1,), (1,)), ((), ()))


def _t5_thresholds():
    d = np.arange(0, 4 * MAX_DISTANCE, dtype=np.int64)
    max_exact = N_BUCKETS // 2
    df = np.maximum(d, 1).astype(np.float32)
    large = max_exact + (np.log(df / np.float32(max_exact)) / np.float32(math.log(MAX_DISTANCE / max_exact))
                         * np.float32(N_BUCKETS - max_exact)).astype(np.int32)
    bucket = np.where(d < max_exact, d, np.minimum(large, N_BUCKETS - 1))
    return [int(np.argmax(bucket >= b)) for b in range(N_BUCKETS)]


_T5_THR = _t5_thresholds()
assert (FAR_TILE - 1) * TK + 1 >= _T5_THR[N_BUCKETS - 1]


def _lane_iota(shape):
    return lax.broadcasted_iota(jnp.int32, shape, len(shape) - 1)


def _bias_table_kernel(bias_ref, out_ref):
    h = pl.program_id(0)
    n = pl.program_id(1) + MIN_DIST
    i = lax.broadcasted_iota(jnp.int32, (TQ, TK), 0)
    j = lax.broadcasted_iota(jnp.int32, (TQ, TK), 1)
    d = n * TK + i - j
    is_dil = h < HEADS_PER_MIXER
    far = jnp.logical_and(jnp.logical_not(is_dil), n >= FAR_TILE)
    d_b = jnp.where(far, 4 * MAX_DISTANCE, d)
    val = jnp.full((TQ, TK), bias_ref[h, 0], jnp.float32)
    for b in range(1, N_BUCKETS):
        val = jnp.where(d_b >= _T5_THR[b], bias_ref[h, b], val)
    mult = ((d <= 128).astype(jnp.int32)
            + jnp.logical_and((d & 3) == 0, d <= 512).astype(jnp.int32)
            + jnp.logical_and((d & 15) == 0, d <= 2048).astype(jnp.int32))
    logm = jnp.where(mult == 3, math.log(3.0), jnp.where(mult == 2, math.log(2.0), 0.0))
    val = val + jnp.where(is_dil, logm, 0.0)
    valid = jnp.logical_and(d >= 0, jnp.logical_or(mult > 0, jnp.logical_not(is_dil)))
    out_ref[...] = jnp.where(valid, val * LOG2E, NEG)


def _bias_tables(rel_bias):
    bias_h = rel_bias.T.astype(jnp.float32)
    return pl.pallas_call(
        _bias_table_kernel,
        out_shape=jax.ShapeDtypeStruct((N_BIAS_HEADS, N_TAB_TILES, TQ, TK), jnp.float32),
        grid=(N_BIAS_HEADS, N_TAB_TILES),
        in_specs=[pl.BlockSpec(memory_space=pltpu.SMEM)],
        out_specs=pl.BlockSpec((None, None, TQ, TK), lambda h, n: (h, n, 0, 0)),
        name="bias_tables",
    )(bias_h)


PROJ_TM = 512
PROJ_CW = 512
F_ROWS = 16


def _norm_proj_kernel(x_ref, g_ref, w_ref, wf_ref, bf_ref, qkv_ref, lf_ref):
    x = x_ref[...]
    ms = jnp.mean(x * x, axis=1, keepdims=True)
    h = x * lax.rsqrt(ms + RMS_EPS) * g_ref[...]
    hb = h.astype(jnp.bfloat16)
    for c in range(QKV_W // PROJ_CW):
        cols = slice(c * PROJ_CW, (c + 1) * PROJ_CW)
        qkv_ref[:, cols] = jnp.dot(hb, w_ref[:, cols], preferred_element_type=jnp.float32).astype(jnp.bfloat16)
    h_lo = (h - hb.astype(jnp.float32)).astype(jnp.bfloat16)
    wf_hi = wf_ref[0]
    wf_lo = wf_ref[1]
    z = (lax.dot_general(wf_hi, hb, _NT, preferred_element_type=jnp.float32)
         + lax.dot_general(wf_hi, h_lo, _NT, preferred_element_type=jnp.float32)
         + lax.dot_general(wf_lo, hb, _NT, preferred_element_type=jnp.float32))
    z = z + bf_ref[...]
    lf_ref[...] = jnp.minimum(z, 0.0) - jnp.log(1.0 + jnp.exp(-jnp.abs(z)))


def _norm_proj(x2d, g, w_qkv, wf, bf, batch, seq):
    n_tok = x2d.shape[0]
    tm = min(PROJ_TM, seq)
    per_b = seq // tm
    return pl.pallas_call(
        _norm_proj_kernel,
        out_shape=(jax.ShapeDtypeStruct((n_tok, QKV_W), jnp.bfloat16),
                   jax.ShapeDtypeStruct((batch * F_ROWS, seq), jnp.float32)),
        grid=(n_tok // tm,),
        in_specs=[pl.BlockSpec((tm, D_MODEL), lambda i: (i, 0)),
                  pl.BlockSpec((1, D_MODEL), lambda i: (0, 0)),
                  pl.BlockSpec((D_MODEL, QKV_W), lambda i: (0, 0)),
                  pl.BlockSpec((2, F_ROWS, D_MODEL), lambda i: (0, 0, 0)),
                  pl.BlockSpec((F_ROWS, 1), lambda i: (0, 0))],
        out_specs=(pl.BlockSpec((tm, QKV_W), lambda i: (i, 0)),
                   pl.BlockSpec((F_ROWS, tm), lambda i: (i // per_b, i % per_b))),
        compiler_params=pltpu.CompilerParams(dimension_semantics=("arbitrary",), vmem_limit_bytes=VMEM_LIMIT),
        name="norm_proj",
    )(x2d, g, w_qkv, wf, bf)


def _split3(x):
    x1 = x.astype(jnp.bfloat16)
    r1 = x - x1.astype(jnp.float32)
    x2 = r1.astype(jnp.bfloat16)
    x3 = (r1 - x2.astype(jnp.float32)).astype(jnp.bfloat16)
    return x1, x2, x3


def _cumsum_kernel(lf_ref, c_ref):
    rows, seq = lf_ref.shape
    r = lax.broadcasted_iota(jnp.int32, (LANES, LANES), 0)
    c = lax.broadcasted_iota(jnp.int32, (LANES, LANES), 1)
    upper = jnp.where(r <= c, 1.0, 0.0).astype(jnp.bfloat16)

    def body(i, carry):
        off = pl.multiple_of(i * LANES, LANES)
        x1, x2, x3 = _split3(lf_ref[:, pl.ds(off, LANES)])
        cs = (jnp.dot(x1, upper, preferred_element_type=jnp.float32)
              + jnp.dot(x2, upper, preferred_element_type=jnp.float32)
              + jnp.dot(x3, upper, preferred_element_type=jnp.float32)) + carry
        c_ref[:, pl.ds(off, LANES)] = cs * LOG2E
        return cs[:, LANES - 1:LANES]

    lax.fori_loop(0, seq // LANES, body, jnp.zeros((rows, 1), jnp.float32))


def _cumsum(lf):
    return pl.pallas_call(
        _cumsum_kernel,
        out_shape=jax.ShapeDtypeStruct(lf.shape, jnp.float32),
        name="forget_cumsum",
    )(lf)


def _rows(n_streams, sub, st):
    rb = sub * n_streams + st
    return slice(rb * TQ, (rb + 1) * TQ)


def _stack_masked_q(q, qs_ref, n_streams):
    lane = _lane_iota((TQ, PAIR_W))
    width = PAIR_W // n_streams
    for sub in range(N_SUB):
        q_half = q[sub * TQ:(sub + 1) * TQ]
        for st in range(n_streams):
            qs_ref[_rows(n_streams, sub, st), 0:PAIR_W] = jnp.where(lane // width == st, q_half,
                                                                    jnp.zeros_like(q_half))


def _init_state(m_ref, acc_ref):
    m_ref[...] = jnp.full(m_ref.shape, NEG, jnp.float32)
    acc_ref[...] = jnp.zeros(acc_ref.shape, jnp.float32)


def _flash_step(rows, s, v_aug, m_ref, acc_ref):
    m_prev = m_ref[rows, :]
    m_new = jnp.maximum(m_prev, jnp.max(s, axis=1, keepdims=True))
    alpha = jnp.exp2(m_prev - m_new)
    p = jnp.concatenate([jnp.exp2(s[:, c * LANES:(c + 1) * LANES] - m_new) for c in range(s.shape[1] // LANES)],
                        axis=1).astype(v_aug.dtype)
    pv = jnp.dot(p, v_aug, preferred_element_type=jnp.float32)
    acc_ref[rows, :] = jnp.concatenate([alpha, alpha], axis=1) * acc_ref[rows, :] + pv
    m_ref[rows, :] = m_new


def _normalised(acc_ref, rows):
    return acc_ref[rows, 0:PAIR_W] / acc_ref[rows, PAIR_W:ACC_W]


def _table_bias(tab_ref, hh, m, sub, cap):
    return jnp.concatenate([tab_ref[hh, jnp.minimum(2 * m + sub - w, cap) - MIN_DIST] for w in range(STEP_TILES)],
                           axis=1)


def _step_rows(ref, blk, m):
    k0 = pl.multiple_of((blk - m) * STEP_KEYS, STEP_KEYS)
    return ref[pl.ds(k0, STEP_KEYS), :]


def _make_attend(n_streams, blk, qs_ref, v_ref, m_ref, acc_ref, s_ref, keys, finish):
    blocks = [(sub, st) for sub in range(N_SUB) for st in range(n_streams)]

    def raw(rows, rhs):
        return lax.dot_general(qs_ref[rows, :], rhs, _NT, preferred_element_type=jnp.float32)

    def v_aug(m):
        v = _step_rows(v_ref, blk, m)
        return jnp.concatenate([v, jnp.ones_like(v)], axis=1)

    def single(m):
        rhs = keys(m)
        va = v_aug(m)
        for sub, st in blocks:
            rows = _rows(n_streams, sub, st)
            _flash_step(rows, finish(sub, st, raw(rows, rhs), m), va, m_ref, acc_ref)

    def run(m0, count):
        @pl.when(count > 0)
        def _():
            rhs = keys(m0)
            for sub, st in blocks:
                rows = _rows(n_streams, sub, st)
                s_ref[rows, :] = raw(rows, rhs)

        def body(i, carry):
            m = m0 + i
            va = v_aug(m)
            rhs_next = keys(jnp.minimum(m + 1, blk))
            for sub, st in blocks:
                rows = _rows(n_streams, sub, st)
                cur = s_ref[rows, :]
                s_ref[rows, :] = raw(rows, rhs_next)
                _flash_step(rows, finish(sub, st, cur, m), va, m_ref, acc_ref)
            return carry

        lax.fori_loop(0, count, body, 0)

    return single, run


def _attn_call(kernel, group, qkv, extra_inputs, extra_specs, n_streams, batch, seq, *, q_width=PAIR_W,
               extra_scratch=(), name=None):
    nq = seq // QBLK
    qc, kc, vc = 6 * group, 6 * group + 2, 6 * group + 4
    rows = N_SUB * n_streams * TQ
    return pl.pallas_call(
        kernel,
        out_shape=jax.ShapeDtypeStruct((batch * seq, GROUP_W), jnp.bfloat16),
        grid=(2, batch, nq),
        in_specs=[pl.BlockSpec((QBLK, PAIR_W), lambda p, b, i: (b * nq + i, qc + p)),
                  pl.BlockSpec((seq, PAIR_W), lambda p, b, i: (b, kc + p)),
                  pl.BlockSpec((seq, PAIR_W), lambda p, b, i: (b, vc + p))] + list(extra_specs),
        out_specs=pl.BlockSpec((QBLK, PAIR_W), lambda p, b, i: (b * nq + i, p)),
        scratch_shapes=[pltpu.VMEM((rows, q_width), jnp.bfloat16),
                        pltpu.VMEM((rows, LANES), jnp.float32),
                        pltpu.VMEM((rows, ACC_W), jnp.float32),
                        pltpu.VMEM((rows, STEP_KEYS), jnp.float32)
                        ] + list(extra_scratch),
        compiler_params=pltpu.CompilerParams(dimension_semantics=("arbitrary", "arbitrary", "arbitrary"),
                                             vmem_limit_bytes=VMEM_LIMIT),
        name=name,
    )(qkv, qkv, qkv, *extra_inputs)


def _tab_spec(first_head):
    return pl.BlockSpec((2, N_TAB_TILES, TQ, TK), lambda p, b, i: (first_head // 2 + p, 0, 0, 0))


def _merge_pair(o0, o1):
    return jnp.where(_lane_iota(o0.shape) < HEAD_DIM, o0, o1)


def _store_pair_outputs(o_ref, acc_ref):
    for sub in range(N_SUB):
        o = _merge_pair(_normalised(acc_ref, _rows(2, sub, 0)), _normalised(acc_ref, _rows(2, sub, 1)))
        o_ref[sub * TQ:(sub + 1) * TQ, :] = o.astype(o_ref.dtype)


def _dilated_kernel(q_ref, k_ref, v_ref, tab_ref, o_ref, qs_ref, m_ref, acc_ref, s_ref):
    blk = pl.program_id(2)
    _stack_masked_q(q_ref[...], qs_ref, 2)
    _init_state(m_ref, acc_ref)
    _, run = _make_attend(
        2, blk, qs_ref, v_ref, m_ref, acc_ref, s_ref,
        keys=lambda m: _step_rows(k_ref, blk, m),
        finish=lambda sub, st, raw, m: raw + _table_bias(tab_ref, st, m, sub, N_DIL_TILES))
    run(0, jnp.minimum(blk, (N_DIL_TILES - 1) // STEP_TILES) + 1)
    _store_pair_outputs(o_ref, acc_ref)


def _dilated(qkv, tabs, batch, seq):
    return _attn_call(_dilated_kernel, 0, qkv, [tabs], [_tab_spec(0)], 2, batch, seq, name="dilated_attn")


def _diff_kernel(q_ref, k_ref, v_ref, tab_ref, lam_ref, g_ref, o_ref, qs_ref, m_ref, acc_ref, s_ref, *,
                 lambda_init):
    blk = pl.program_id(2)
    _stack_masked_q(q_ref[...], qs_ref, 4)
    _init_state(m_ref, acc_ref)
    _, run = _make_attend(
        4, blk, qs_ref, v_ref, m_ref, acc_ref, s_ref,
        keys=lambda m: _step_rows(k_ref, blk, m),
        finish=lambda sub, st, raw, m: raw + _table_bias(tab_ref, st // 2, m, sub, FAR_TILE))
    run(0, blk + 1)

    lamv = lam_ref[...]
    lam = (jnp.exp(jnp.sum(lamv[0:1] * lamv[1:2], axis=1, keepdims=True))
           - jnp.exp(jnp.sum(lamv[2:3] * lamv[3:4], axis=1, keepdims=True)) + lambda_init)
    first = _lane_iota((TQ, PAIR_W)) < HEAD_DIM
    for sub in range(N_SUB):
        part = [_normalised(acc_ref, _rows(4, sub, st)) for st in range(4)]
        o = _merge_pair(part[0] - lam * part[1], part[2] - lam * part[3])
        sq = o * o
        ms0 = jnp.sum(jnp.where(first, sq, 0.0), axis=1, keepdims=True)
        ms1 = jnp.sum(jnp.where(first, 0.0, sq), axis=1, keepdims=True)
        ms = jnp.where(first, ms0, ms1) * (1.0 / HEAD_DIM)
        y = (o * lax.rsqrt(ms + RMS_EPS) * g_ref[...]) * (1.0 - lambda_init)
        o_ref[sub * TQ:(sub + 1) * TQ, :] = y.astype(o_ref.dtype)


def _diff(qkv, tabs, lamv, g_sub, lambda_init, batch, seq):
    return _attn_call(functools.partial(_diff_kernel, lambda_init=lambda_init), 1, qkv, [tabs, lamv, g_sub],
                      [_tab_spec(4), pl.BlockSpec((8, LANES), lambda p, b, i: (0, 0)),
                       pl.BlockSpec((1, PAIR_W), lambda p, b, i: (0, 0))],
                      4, batch, seq, name="diff_attn")


def _moba_kernel(q_ref, k_ref, v_ref, tab_ref, o_ref, qs_ref, m_ref, acc_ref, s_ref, km_ref):
    blk = pl.program_id(2)
    n_blk = k_ref.shape[0] // MOBA_BLOCK
    assert n_blk <= LANES and MOBA_BLOCK == TQ == TK

    @pl.when(blk == 0)
    def _():
        km_ref[...] = jnp.zeros(km_ref.shape, jnp.float32)
        for n in range(n_blk):
            km_ref[n:n + 1, :] = jnp.sum(k_ref[n * MOBA_BLOCK:(n + 1) * MOBA_BLOCK, :].astype(jnp.float32),
                                         axis=0, keepdims=True) * (1.0 / MOBA_BLOCK)

    _stack_masked_q(q_ref[...], qs_ref, 2)
    _init_state(m_ref, acc_ref)
    km = km_ref[...]
    km_hi = km.astype(jnp.bfloat16)
    km_lo = (km - km_hi.astype(jnp.float32)).astype(jnp.bfloat16)
    qs = qs_ref[:, 0:PAIR_W]
    gate = (lax.dot_general(qs, km_hi, _NT, preferred_element_type=jnp.float32)
            + lax.dot_general(qs, km_lo, _NT, preferred_element_type=jnp.float32))
    cand = _lane_iota(gate.shape)
    cand_f = cand.astype(jnp.float32)
    own = N_SUB * blk + lax.broadcasted_iota(jnp.int32, gate.shape, 0) // (2 * TQ)
    g = jnp.where(cand < own, gate, NEG)
    sel = cand == own
    for _ in range(MOBA_TOPK):
        mx = jnp.max(g, axis=1, keepdims=True)
        first = jnp.min(jnp.where(g == mx, cand_f, 2.0 * LANES), axis=1, keepdims=True)
        pick = cand_f == first
        sel = jnp.logical_or(sel, jnp.logical_and(pick, mx > 0.5 * NEG))
        g = jnp.where(pick, -3e38, g)
    qs_ref[:, PAIR_W:2 * PAIR_W] = jnp.where(sel, 0.0, NEG).astype(jnp.bfloat16)

    def keys(m):
        k = _step_rows(k_ref, blk, m)
        block_of_key = N_SUB * (blk - m) + lax.broadcasted_iota(jnp.int32, (STEP_KEYS, LANES), 0) // TK
        onehot = jnp.where(_lane_iota((STEP_KEYS, LANES)) == block_of_key, 1.0, 0.0).astype(k.dtype)
        return jnp.concatenate([k, onehot], axis=1)

    _, run = _make_attend(
        2, blk, qs_ref, v_ref, m_ref, acc_ref, s_ref, keys=keys,
        finish=lambda sub, st, raw, m: raw + _table_bias(tab_ref, st, m, sub, FAR_TILE))
    run(0, blk + 1)
    _store_pair_outputs(o_ref, acc_ref)


def _moba(qkv, tabs, batch, seq):
    return _attn_call(_moba_kernel, 2, qkv, [tabs], [_tab_spec(8)], 2, batch, seq, q_width=2 * PAIR_W,
                      extra_scratch=[pltpu.VMEM((LANES, PAIR_W), jnp.float32)], name="moba_attn")


FOX_SKIP_BITS = 48.0


def _head_row_norms(x, hh):
    xf = x.astype(jnp.float32)
    sq = jnp.where(_lane_iota(xf.shape) // HEAD_DIM == hh, xf * xf, 0.0)
    return jnp.sqrt(jnp.max(jnp.sum(sq, axis=1, keepdims=True), axis=0, keepdims=True))


def _fox_kernel(q_ref, k_ref, v_ref, c_ref, o_ref, qs_ref, m_ref, acc_ref, s_ref, cq_ref, stat_ref):
    blk = pl.program_id(2)
    n_tiles = k_ref.shape[0] // TK
    assert n_tiles <= LANES

    @pl.when(blk == 0)
    def _():
        stat_ref[...] = jnp.zeros(stat_ref.shape, jnp.float32)
        for j in range(n_tiles):
            k_tile = k_ref[j * TK:(j + 1) * TK, :]
            for hh in range(2):
                stat_ref[hh:hh + 1, j:j + 1] = _head_row_norms(k_tile, hh)
                stat_ref[2 + hh:3 + hh, j:j + 1] = jnp.min(c_ref[hh:hh + 1, j * TK:(j + 1) * TK], axis=1, keepdims=True)

    q = q_ref[...]
    _stack_masked_q(q, qs_ref, 2)
    _init_state(m_ref, acc_ref)
    q0 = pl.multiple_of(blk * QBLK, QBLK)
    row = lax.broadcasted_iota(jnp.int32, (TQ, TQ), 0)
    col = lax.broadcasted_iota(jnp.int32, (TQ, TQ), 1)
    for sub in range(N_SUB):
        for hh in range(2):
            c_row = c_ref[hh:hh + 1, pl.ds(q0 + sub * TQ, TQ)]
            c_col = jnp.sum(jnp.where(row == col, c_row, 0.0), axis=1, keepdims=True)
            cq_ref[_rows(2, sub, hh), :] = jnp.broadcast_to(c_col, (TQ, LANES))

    key_pos = lax.broadcasted_iota(jnp.int32, (TQ, STEP_KEYS), 1)
    query_pos = lax.broadcasted_iota(jnp.int32, (TQ, STEP_KEYS), 0)

    def finish(sub, hh, raw, m):
        k0 = pl.multiple_of((blk - m) * STEP_KEYS, STEP_KEYS)
        cq = cq_ref[_rows(2, sub, hh), :]
        s = raw + (jnp.concatenate([cq] * (STEP_KEYS // LANES), axis=1) - c_ref[hh:hh + 1, pl.ds(k0, STEP_KEYS)])
        if isinstance(m, int) and m == 0:
            s = jnp.where(key_pos <= query_pos + sub * TQ, s, NEG)
        return s

    single, run = _make_attend(2, blk, qs_ref, v_ref, m_ref, acc_ref, s_ref,
                               keys=lambda m: _step_rows(k_ref, blk, m), finish=finish)
    single(0)

    tile = _lane_iota((1, LANES))
    needed = jnp.zeros((1, LANES), jnp.bool_)
    for hh in range(2):
        rows_h = [_rows(2, sub, hh) for sub in range(N_SUB)]
        m_min = functools.reduce(jnp.minimum, [jnp.min(m_ref[r, :], axis=0, keepdims=True) for r in rows_h])
        cq_max = functools.reduce(jnp.maximum, [jnp.max(cq_ref[r, :], axis=0, keepdims=True) for r in rows_h])
        bound = _head_row_norms(q, hh) * stat_ref[hh:hh + 1, :] + cq_max - stat_ref[2 + hh:3 + hh, :] + 1.0
        needed = jnp.logical_or(needed, bound >= m_min - FOX_SKIP_BITS)
    n_own = N_SUB * blk
    first_needed = jnp.min(jnp.where(jnp.logical_and(needed, tile < n_own), tile, n_own))
    run(1, blk - first_needed // STEP_TILES)
    _store_pair_outputs(o_ref, acc_ref)


def _fox(qkv, c, batch, seq):
    return _attn_call(_fox_kernel, 3, qkv, [c.reshape(batch, 2, 8, seq)],
                      [pl.BlockSpec((None, None, 8, seq), lambda p, b, i: (b, p, 0, 0))],
                      2, batch, seq, extra_scratch=[pltpu.VMEM((N_SUB * 2 * TQ, LANES), jnp.float32),
                                                    pltpu.VMEM((8, LANES), jnp.float32)], name="fox_attn")


OUT_TM = 512
ROUTER_W = LANES


def _outproj_router_kernel(oa_ref, ob_ref, oc_ref, od_ref, wo_ref, x_ref, g_ref, wr_ref, br_ref,
                           x1_ref, h2_ref, comb_ref):
    acc = x_ref[...]
    for m, o_ref in enumerate((oa_ref, ob_ref, oc_ref, od_ref)):
        acc = acc + jnp.dot(o_ref[...], wo_ref[m * GROUP_W:(m + 1) * GROUP_W, :], preferred_element_type=jnp.float32)
    x1_ref[...] = acc
    ms = jnp.mean(acc * acc, axis=1, keepdims=True)
    h = acc * lax.rsqrt(ms + RMS_EPS) * g_ref[...]
    hb = h.astype(jnp.bfloat16)
    h2_ref[...] = hb
    h_lo = (h - hb.astype(jnp.float32)).astype(jnp.bfloat16)
    logits = (jnp.dot(hb, wr_ref[0], preferred_element_type=jnp.float32)
              + jnp.dot(h_lo, wr_ref[0], preferred_element_type=jnp.float32)
              + jnp.dot(hb, wr_ref[1], preferred_element_type=jnp.float32)) + br_ref[...]
    lane = _lane_iota(logits.shape).astype(jnp.float32)
    big = 2.0 * LANES
    gmask = jnp.logical_and(lane >= N_EXPERTS, lane < N_EXPERTS + N_GROUPS)
    gl = jnp.where(gmask, logits, NEG)
    gmax = jnp.max(gl, axis=1, keepdims=True)
    glane = jnp.min(jnp.where(gl == gmax, lane, big), axis=1, keepdims=True)
    gsum = jnp.sum(jnp.where(gmask, jnp.exp(gl - gmax), 0.0), axis=1, keepdims=True)
    g_w = 1.0 / gsum
    e0 = (glane - N_EXPERTS) * EXPERTS_PER_GROUP
    emask = jnp.logical_and(lane >= e0, lane < e0 + EXPERTS_PER_GROUP)
    el = jnp.where(emask, logits, NEG)
    v1 = jnp.max(el, axis=1, keepdims=True)
    i1 = jnp.min(jnp.where(el == v1, lane, big), axis=1, keepdims=True)
    el2 = jnp.where(lane == i1, NEG, el)
    v2 = jnp.max(el2, axis=1, keepdims=True)
    i2 = jnp.min(jnp.where(el2 == v2, lane, big), axis=1, keepdims=True)
    e2 = jnp.exp(v2 - v1)
    den = 1.0 + e2
    comb_ref[...] = jnp.where(lane == i1, g_w / den, jnp.where(lane == i2, g_w * e2 / den, 0.0))


def _outproj_router(o_parts, w_out, x2d, g2, wr, br):
    n_tok = x2d.shape[0]
    tm = min(OUT_TM, n_tok)
    row = lambda i: (i, 0)
    fixed = lambda i: (0, 0)
    return pl.pallas_call(
        _outproj_router_kernel,
        out_shape=(jax.ShapeDtypeStruct((n_tok, D_MODEL), jnp.float32),
                   jax.ShapeDtypeStruct((n_tok, D_MODEL), jnp.bfloat16),
                   jax.ShapeDtypeStruct((n_tok, ROUTER_W), jnp.float32)),
        grid=(n_tok // tm,),
        in_specs=[pl.BlockSpec((tm, GROUP_W), row)] * 4 + [
            pl.BlockSpec((D_MODEL, D_MODEL), fixed),
            pl.BlockSpec((tm, D_MODEL), row),
            pl.BlockSpec((1, D_MODEL), fixed),
            pl.BlockSpec((2, D_MODEL, ROUTER_W), lambda i: (0, 0, 0)),
            pl.BlockSpec((1, ROUTER_W), fixed)],
        out_specs=(pl.BlockSpec((tm, D_MODEL), row), pl.BlockSpec((tm, D_MODEL), row),
                   pl.BlockSpec((tm, ROUTER_W), row)),
        compiler_params=pltpu.CompilerParams(dimension_semantics=("arbitrary",), vmem_limit_bytes=VMEM_LIMIT),
        name="outproj_router",
    )(*o_parts, w_out, x2d, g2, wr, br)


MOE_TM = 1024


def _moe_kernel(h_ref, comb_ref, wg_ref, wu_ref, wd_ref, x1_ref, gf_ref, out_ref, acc_ref, *, final_norm):
    e = pl.program_id(1)

    @pl.when(e == 0)
    def _():
        acc_ref[...] = jnp.zeros(acc_ref.shape, jnp.float32)

    h = h_ref[...]
    gate = jnp.dot(h, wg_ref[...], preferred_element_type=jnp.float32)
    up = jnp.dot(h, wu_ref[...], preferred_element_type=jnp.float32)
    hid = (gate * jax.nn.sigmoid(gate)) * up
    comb = comb_ref[...]
    ce = jnp.sum(jnp.where(_lane_iota(comb.shape) == e, comb, 0.0), axis=1, keepdims=True)
    acc_ref[...] += ce * jnp.dot(hid.astype(jnp.bfloat16), wd_ref[...], preferred_element_type=jnp.float32)

    @pl.when(e == N_EXPERTS - 1)
    def _():
        y = x1_ref[...] + acc_ref[...]
        if final_norm:
            ms = jnp.mean(y * y, axis=1, keepdims=True)
            y = y * lax.rsqrt(ms + RMS_EPS) * gf_ref[...]
        out_ref[...] = y


def _moe(h2, comb, wg, wu, wd, x1, gf, final_norm):
    n_tok = h2.shape[0]
    tm = min(MOE_TM, n_tok)
    row = lambda i, e: (i, 0)
    return pl.pallas_call(
        functools.partial(_moe_kernel, final_norm=final_norm),
        out_shape=jax.ShapeDtypeStruct((n_tok, D_MODEL), jnp.float32),
        grid=(n_tok // tm, N_EXPERTS),
        in_specs=[pl.BlockSpec((tm, D_MODEL), row),
                  pl.BlockSpec((tm, ROUTER_W), row),
                  pl.BlockSpec((None, D_MODEL, D_EXPERT), lambda i, e: (e, 0, 0)),
                  pl.BlockSpec((None, D_MODEL, D_EXPERT), lambda i, e: (e, 0, 0)),
                  pl.BlockSpec((None, D_EXPERT, D_MODEL), lambda i, e: (e, 0, 0)),
                  pl.BlockSpec((tm, D_MODEL), row),
                  pl.BlockSpec((1, D_MODEL), lambda i, e: (0, 0))],
        out_specs=pl.BlockSpec((tm, D_MODEL), row),
        scratch_shapes=[pltpu.VMEM((tm, D_MODEL), jnp.float32)],
        compiler_params=pltpu.CompilerParams(dimension_semantics=("arbitrary", "arbitrary"),
                                             vmem_limit_bytes=VMEM_LIMIT),
        name="moe_experts",
    )(h2, comb, wg, wu, wd, x1, gf)


def _split_bf16(w):
    hi = w.astype(jnp.bfloat16)
    lo = (w - hi.astype(jnp.float32)).astype(jnp.bfloat16)
    return jnp.stack([hi, lo])


def _qkv_col_scale():
    s = np.ones((QKV_W,), np.float32)
    for group, dim in enumerate((HEAD_DIM, DIFF_DIM, HEAD_DIM, HEAD_DIM)):
        s[3 * group * GROUP_W:(3 * group + 1) * GROUP_W] = LOG2E * dim ** -0.5
    return s


def _forget_weights(w_f, b_f):
    rows = jnp.zeros((F_ROWS, D_MODEL), jnp.float32)
    bias = jnp.zeros((F_ROWS, 1), jnp.float32)
    for h in range(HEADS_PER_MIXER):
        r = (h // 2) * 8 + (h % 2)
        rows = rows.at[r].set(w_f[:, h])
        bias = bias.at[r, 0].set(b_f[h])
    return _split_bf16(rows), bias


def kernel(x, rel_bias, ln1, w_in, w_out, lam_q1, lam_k1, lam_q2, lam_k2, subln_g, b_forget,
           ln2, w_group, b_group, w_expert, b_expert, w_gate, w_up, w_down, ln_f):
    batch, seq, _ = x.shape
    depth = ln1.shape[0]
    assert seq % QBLK == 0 and seq % STEP_KEYS == 0
    tabs = _bias_tables(rel_bias)
    xt = x.reshape(batch * seq, D_MODEL)
    gf = ln_f.reshape(1, D_MODEL)
    col_scale = _qkv_col_scale()
    for l in range(depth):
        lambda_init = 0.8 - 0.6 * math.exp(-0.3 * l)
        wf, bf = _forget_weights(w_in[l][:, QKV_W:], b_forget[l])
        w_qkv = (w_in[l][:, :QKV_W] * col_scale).astype(jnp.bfloat16)
        qkv, lf = _norm_proj(xt, ln1[l].reshape(1, D_MODEL), w_qkv, wf, bf, batch, seq)
        c = _cumsum(lf)

        o_a = _dilated(qkv, tabs, batch, seq)
        lamv = jnp.zeros((8, LANES), jnp.float32)
        lamv = lamv.at[0:4, 0:DIFF_DIM].set(jnp.stack([lam_q1[l], lam_k1[l], lam_q2[l], lam_k2[l]]))
        g_sub = jnp.tile(subln_g[l], 2).reshape(1, PAIR_W)
        o_b = _diff(qkv, tabs, lamv, g_sub, lambda_init, batch, seq)
        o_c = _moba(qkv, tabs, batch, seq)
        o_d = _fox(qkv, c, batch, seq)

        w_router = jnp.zeros((D_MODEL, ROUTER_W), jnp.float32)
        w_router = w_router.at[:, :N_EXPERTS].set(w_expert[l]).at[:, N_EXPERTS:N_EXPERTS + N_GROUPS].set(w_group[l])
        b_router = jnp.zeros((1, ROUTER_W), jnp.float32)
        b_router = b_router.at[0, :N_EXPERTS].set(b_expert[l].reshape(-1)).at[0, N_EXPERTS:N_EXPERTS + N_GROUPS].set(b_group[l])
        x1, h2, comb = _outproj_router([o_a, o_b, o_c, o_d], w_out[l].astype(jnp.bfloat16), xt,
                                       ln2[l].reshape(1, D_MODEL), _split_bf16(w_router), b_router)
        xt = _moe(h2, comb, w_gate[l].astype(jnp.bfloat16), w_up[l].astype(jnp.bfloat16),
                  w_down[l].astype(jnp.bfloat16), x1, gf, final_norm=(l == depth - 1))
    return xt.reshape(batch, seq, D_MODEL)
```

```python
import functools
import math

import jax
import jax.numpy as jnp
import numpy as np
from jax import lax
from jax.experimental import pallas as pl
from jax.experimental.pallas import tpu as pltpu

D_MODEL = 1024
HEAD_DIM = 64
HEADS_PER_MIXER = 4
GROUP_W = HEADS_PER_MIXER * HEAD_DIM
QKV_W = 12 * GROUP_W
DIFF_DIM = HEAD_DIM // 2
MOBA_BLOCK = 256
MOBA_TOPK = 3
N_BUCKETS = 32
MAX_DISTANCE = 2048
N_BIAS_HEADS = 12
N_GROUPS = 4
EXPERTS_PER_GROUP = 8
N_EXPERTS = 32
D_EXPERT = 256
RMS_EPS = 1e-6
NEG = -1e30
LOG2E = math.log2(math.e)

LANES = 128
PAIR_W = 2 * HEAD_DIM
ACC_W = 2 * PAIR_W
TQ = 256
TK = 256
N_SUB = 2
QBLK = N_SUB * TQ
STEP_TILES = 2
STEP_KEYS = STEP_TILES * TK
N_DIL_TILES = MAX_DISTANCE // TK + 1
FAR_TILE = 7
MIN_DIST = -1
N_TAB_TILES = N_DIL_TILES + 1 - MIN_DIST
VMEM_LIMIT = 48 * 1024 * 1024

_NT = (((1,), (1,)), ((), ()))


def _t5_thresholds():
    d = np.arange(0, 4 * MAX_DISTANCE, dtype=np.int64)
    max_exact = N_BUCKETS // 2
    df = np.maximum(d, 1).astype(np.float32)
    large = max_exact + (np.log(df / np.float32(max_exact)) / np.float32(math.log(MAX_DISTANCE / max_exact))
                         * np.float32(N_BUCKETS - max_exact)).astype(np.int32)
    bucket = np.where(d < max_exact, d, np.minimum(large, N_BUCKETS - 1))
    return [int(np.argmax(bucket >= b)) for b in range(N_BUCKETS)]


_T5_THR = _t5_thresholds()
assert (FAR_TILE - 1) * TK + 1 >= _T5_THR[N_BUCKETS - 1]


def _lane_iota(shape):
    return lax.broadcasted_iota(jnp.int32, shape, len(shape) - 1)


def _bias_table_kernel(bias_ref, out_ref):
    h = pl.program_id(0)
    n = pl.program_id(1) + MIN_DIST
    i = lax.broadcasted_iota(jnp.int32, (TQ, TK), 0)
    j = lax.broadcasted_iota(jnp.int32, (TQ, TK), 1)
    d = n * TK + i - j
    is_dil = h < HEADS_PER_MIXER
    far = jnp.logical_and(jnp.logical_not(is_dil), n >= FAR_TILE)
    d_b = jnp.where(far, 4 * MAX_DISTANCE, d)
    val = jnp.full((TQ, TK), bias_ref[h, 0], jnp.float32)
    for b in range(1, N_BUCKETS):
        val = jnp.where(d_b >= _T5_THR[b], bias_ref[h, b], val)
    mult = ((d <= 128).astype(jnp.int32)
            + jnp.logical_and((d & 3) == 0, d <= 512).astype(jnp.int32)
            + jnp.logical_and((d & 15) == 0, d <= 2048).astype(jnp.int32))
    logm = jnp.where(mult == 3, math.log(3.0), jnp.where(mult == 2, math.log(2.0), 0.0))
    val = val + jnp.where(is_dil, logm, 0.0)
    valid = jnp.logical_and(d >= 0, jnp.logical_or(mult > 0, jnp.logical_not(is_dil)))
    out_ref[...] = jnp.where(valid, val * LOG2E, NEG)


def _bias_tables(rel_bias):
    bias_h = rel_bias.T.astype(jnp.float32)
    return pl.pallas_call(
        _bias_table_kernel,
        out_shape=jax.ShapeDtypeStruct((N_BIAS_HEADS, N_TAB_TILES, TQ, TK), jnp.float32),
        grid=(N_BIAS_HEADS, N_TAB_TILES),
        in_specs=[pl.BlockSpec(memory_space=pltpu.SMEM)],
        out_specs=pl.BlockSpec((None, None, TQ, TK), lambda h, n: (h, n, 0, 0)),
        name="bias_tables",
    )(bias_h)


PROJ_TM = 512
PROJ_CW = 512
F_ROWS = 16


def _norm_proj_kernel(x_ref, g_ref, w_ref, wf_ref, bf_ref, qkv_ref, lf_ref):
    x = x_ref[...]
    ms = jnp.mean(x * x, axis=1, keepdims=True)
    h = x * lax.rsqrt(ms + RMS_EPS) * g_ref[...]
    hb = h.astype(jnp.bfloat16)
    for c in range(QKV_W // PROJ_CW):
        cols = slice(c * PROJ_CW, (c + 1) * PROJ_CW)
        qkv_ref[:, cols] = jnp.dot(hb, w_ref[:, cols], preferred_element_type=jnp.float32).astype(jnp.bfloat16)
    h_lo = (h - hb.astype(jnp.float32)).astype(jnp.bfloat16)
    wf_hi = wf_ref[0]
    wf_lo = wf_ref[1]
    z = (lax.dot_general(wf_hi, hb, _NT, preferred_element_type=jnp.float32)
         + lax.dot_general(wf_hi, h_lo, _NT, preferred_element_type=jnp.float32)
         + lax.dot_general(wf_lo, hb, _NT, preferred_element_type=jnp.float32))
    z = z + bf_ref[...]
    lf_ref[...] = jnp.minimum(z, 0.0) - jnp.log(1.0 + jnp.exp(-jnp.abs(z)))


def _norm_proj(x2d, g, w_qkv, wf, bf, batch, seq):
    n_tok = x2d.shape[0]
    tm = min(PROJ_TM, seq)
    per_b = seq // tm
    return pl.pallas_call(
        _norm_proj_kernel,
        out_shape=(jax.ShapeDtypeStruct((n_tok, QKV_W), jnp.bfloat16),
                   jax.ShapeDtypeStruct((batch * F_ROWS, seq), jnp.float32)),
        grid=(n_tok // tm,),
        in_specs=[pl.BlockSpec((tm, D_MODEL), lambda i: (i, 0)),
                  pl.BlockSpec((1, D_MODEL), lambda i: (0, 0)),
                  pl.BlockSpec((D_MODEL, QKV_W), lambda i: (0, 0)),
                  pl.BlockSpec((2, F_ROWS, D_MODEL), lambda i: (0, 0, 0)),
                  pl.BlockSpec((F_ROWS, 1), lambda i: (0, 0))],
        out_specs=(pl.BlockSpec((tm, QKV_W), lambda i: (i, 0)),
                   pl.BlockSpec((F_ROWS, tm), lambda i: (i // per_b, i % per_b))),
        compiler_params=pltpu.CompilerParams(dimension_semantics=("arbitrary",), vmem_limit_bytes=VMEM_LIMIT),
        name="norm_proj",
    )(x2d, g, w_qkv, wf, bf)


def _split3(x):
    x1 = x.astype(jnp.bfloat16)
    r1 = x - x1.astype(jnp.float32)
    x2 = r1.astype(jnp.bfloat16)
    x3 = (r1 - x2.astype(jnp.float32)).astype(jnp.bfloat16)
    return x1, x2, x3


def _cumsum_kernel(lf_ref, c_ref):
    rows, seq = lf_ref.shape
    r = lax.broadcasted_iota(jnp.int32, (LANES, LANES), 0)
    c = lax.broadcasted_iota(jnp.int32, (LANES, LANES), 1)
    upper = jnp.where(r <= c, 1.0, 0.0).astype(jnp.bfloat16)

    def body(i, carry):
        off = pl.multiple_of(i * LANES, LANES)
        x1, x2, x3 = _split3(lf_ref[:, pl.ds(off, LANES)])
        cs = (jnp.dot(x1, upper, preferred_element_type=jnp.float32)
              + jnp.dot(x2, upper, preferred_element_type=jnp.float32)
              + jnp.dot(x3, upper, preferred_element_type=jnp.float32)) + carry
        c_ref[:, pl.ds(off, LANES)] = cs * LOG2E
        return cs[:, LANES - 1:LANES]

    lax.fori_loop(0, seq // LANES, body, jnp.zeros((rows, 1), jnp.float32))


def _cumsum(lf):
    return pl.pallas_call(
        _cumsum_kernel,
        out_shape=jax.ShapeDtypeStruct(lf.shape, jnp.float32),
        name="forget_cumsum",
    )(lf)


def _rows(n_streams, sub, st):
    rb = sub * n_streams + st
    return slice(rb * TQ, (rb + 1) * TQ)


def _stack_masked_q(q, qs_ref, n_streams):
    lane = _lane_iota((TQ, PAIR_W))
    width = PAIR_W // n_streams
    for sub in range(N_SUB):
        q_half = q[sub * TQ:(sub + 1) * TQ]
        for st in range(n_streams):
            qs_ref[_rows(n_streams, sub, st), 0:PAIR_W] = jnp.where(lane // width == st, q_half,
                                                                    jnp.zeros_like(q_half))


def _init_state(m_ref, acc_ref):
    m_ref[...] = jnp.full(m_ref.shape, NEG, jnp.float32)
    acc_ref[...] = jnp.zeros(acc_ref.shape, jnp.float32)


def _flash_step(rows, s, v_aug, m_ref, acc_ref):
    m_prev = m_ref[rows, :]
    m_new = jnp.maximum(m_prev, jnp.max(s, axis=1, keepdims=True))
    alpha = jnp.exp2(m_prev - m_new)
    p = jnp.concatenate([jnp.exp2(s[:, c * LANES:(c + 1) * LANES] - m_new) for c in range(s.shape[1] // LANES)],
                        axis=1).astype(v_aug.dtype)
    pv = jnp.dot(p, v_aug, preferred_element_type=jnp.float32)
    acc_ref[rows, :] = jnp.concatenate([alpha, alpha], axis=1) * acc_ref[rows, :] + pv
    m_ref[rows, :] = m_new


def _normalised(acc_ref, rows):
    return acc_ref[rows, 0:PAIR_W] / acc_ref[rows, PAIR_W:ACC_W]


def _table_bias(tab_ref, hh, m, sub, cap):
    return jnp.concatenate([tab_ref[hh, jnp.minimum(2 * m + sub - w, cap) - MIN_DIST] for w in range(STEP_TILES)],
                           axis=1)


def _step_rows(ref, blk, m):
    k0 = pl.multiple_of((blk - m) * STEP_KEYS, STEP_KEYS)
    return ref[pl.ds(k0, STEP_KEYS), :]


def _make_attend(n_streams, blk, qs_ref, v_ref, m_ref, acc_ref, s_ref, keys, finish):
    blocks = [(sub, st) for sub in range(N_SUB) for st in range(n_streams)]

    def raw(rows, rhs):
        return lax.dot_general(qs_ref[rows, :], rhs, _NT, preferred_element_type=jnp.float32)

    def v_aug(m):
        v = _step_rows(v_ref, blk, m)
        return jnp.concatenate([v, jnp.ones_like(v)], axis=1)

    def single(m):
        rhs = keys(m)
        va = v_aug(m)
        for sub, st in blocks:
            rows = _rows(n_streams, sub, st)
            _flash_step(rows, finish(sub, st, raw(rows, rhs), m), va, m_ref, acc_ref)

    def run(m0, count):
        @pl.when(count > 0)
        def _():
            rhs = keys(m0)
            for sub, st in blocks:
                rows = _rows(n_streams, sub, st)
                s_ref[rows, :] = raw(rows, rhs)

        def body(i, carry):
            m = m0 + i
            va = v_aug(m)
            rhs_next = keys(jnp.minimum(m + 1, blk))
            for sub, st in blocks:
                rows = _rows(n_streams, sub, st)
                cur = s_ref[rows, :]
                s_ref[rows, :] = raw(rows, rhs_next)
                _flash_step(rows, finish(sub, st, cur, m), va, m_ref, acc_ref)
            return carry

        lax.fori_loop(0, count, body, 0)

    return single, run


def _attn_call(kernel, group, qkv, extra_inputs, extra_specs, n_streams, batch, seq, *, q_width=PAIR_W,
               extra_scratch=(), name=None):
    nq = seq // QBLK
    qc, kc, vc = 6 * group, 6 * group + 2, 6 * group + 4
    rows = N_SUB * n_streams * TQ
    return pl.pallas_call(
        kernel,
        out_shape=jax.ShapeDtypeStruct((batch * seq, GROUP_W), jnp.bfloat16),
        grid=(2, batch, nq),
        in_specs=[pl.BlockSpec((QBLK, PAIR_W), lambda p, b, i: (b * nq + i, qc + p)),
                  pl.BlockSpec((seq, PAIR_W), lambda p, b, i: (b, kc + p)),
                  pl.BlockSpec((seq, PAIR_W), lambda p, b, i: (b, vc + p))] + list(extra_specs),
        out_specs=pl.BlockSpec((QBLK, PAIR_W), lambda p, b, i: (b * nq + i, p)),
        scratch_shapes=[pltpu.VMEM((rows, q_width), jnp.bfloat16),
                        pltpu.VMEM((rows, LANES), jnp.float32),
                        pltpu.VMEM((rows, ACC_W), jnp.float32),
                        pltpu.VMEM((rows, STEP_KEYS), jnp.float32)
                        ] + list(extra_scratch),
        compiler_params=pltpu.CompilerParams(dimension_semantics=("arbitrary", "arbitrary", "arbitrary"),
                                             vmem_limit_bytes=VMEM_LIMIT),
        name=name,
    )(qkv, qkv, qkv, *extra_inputs)


def _tab_spec(first_head):
    return pl.BlockSpec((2, N_TAB_TILES, TQ, TK), lambda p, b, i: (first_head // 2 + p, 0, 0, 0))


def _merge_pair(o0, o1):
    return jnp.where(_lane_iota(o0.shape) < HEAD_DIM, o0, o1)


def _store_pair_outputs(o_ref, acc_ref):
    for sub in range(N_SUB):
        o = _merge_pair(_normalised(acc_ref, _rows(2, sub, 0)), _normalised(acc_ref, _rows(2, sub, 1)))
        o_ref[sub * TQ:(sub + 1) * TQ, :] = o.astype(o_ref.dtype)


def _dilated_kernel(q_ref, k_ref, v_ref, tab_ref, o_ref, qs_ref, m_ref, acc_ref, s_ref):
    blk = pl.program_id(2)
    _stack_masked_q(q_ref[...], qs_ref, 2)
    _init_state(m_ref, acc_ref)
    _, run = _make_attend(
        2, blk, qs_ref, v_ref, m_ref, acc_ref, s_ref,
        keys=lambda m: _step_rows(k_ref, blk, m),
        finish=lambda sub, st, raw, m: raw + _table_bias(tab_ref, st, m, sub, N_DIL_TILES))
    run(0, jnp.minimum(blk, (N_DIL_TILES - 1) // STEP_TILES) + 1)
    _store_pair_outputs(o_ref, acc_ref)


def _dilated(qkv, tabs, batch, seq):
    return _attn_call(_dilated_kernel, 0, qkv, [tabs], [_tab_spec(0)], 2, batch, seq, name="dilated_attn")


def _diff_kernel(q_ref, k_ref, v_ref, tab_ref, lam_ref, g_ref, o_ref, qs_ref, m_ref, acc_ref, s_ref, *,
                 lambda_init):
    blk = pl.program_id(2)
    _stack_masked_q(q_ref[...], qs_ref, 4)
    _init_state(m_ref, acc_ref)
    _, run = _make_attend(
        4, blk, qs_ref, v_ref, m_ref, acc_ref, s_ref,
        keys=lambda m: _step_rows(k_ref, blk, m),
        finish=lambda sub, st, raw, m: raw + _table_bias(tab_ref, st // 2, m, sub, FAR_TILE))
    run(0, blk + 1)

    lamv = lam_ref[...]
    lam = (jnp.exp(jnp.sum(lamv[0:1] * lamv[1:2], axis=1, keepdims=True))
           - jnp.exp(jnp.sum(lamv[2:3] * lamv[3:4], axis=1, keepdims=True)) + lambda_init)
    first = _lane_iota((TQ, PAIR_W)) < HEAD_DIM
    for sub in range(N_SUB):
        part = [_normalised(acc_ref, _rows(4, sub, st)) for st in range(4)]
        o = _merge_pair(part[0] - lam * part[1], part[2] - lam * part[3])
        sq = o * o
        ms0 = jnp.sum(jnp.where(first, sq, 0.0), axis=1, keepdims=True)
        ms1 = jnp.sum(jnp.where(first, 0.0, sq), axis=1, keepdims=True)
        ms = jnp.where(first, ms0, ms1) * (1.0 / HEAD_DIM)
        y = (o * lax.rsqrt(ms + RMS_EPS) * g_ref[...]) * (1.0 - lambda_init)
        o_ref[sub * TQ:(sub + 1) * TQ, :] = y.astype(o_ref.dtype)


def _diff(qkv, tabs, lamv, g_sub, lambda_init, batch, seq):
    return _attn_call(functools.partial(_diff_kernel, lambda_init=lambda_init), 1, qkv, [tabs, lamv, g_sub],
                      [_tab_spec(4), pl.BlockSpec((8, LANES), lambda p, b, i: (0, 0)),
                       pl.BlockSpec((1, PAIR_W), lambda p, b, i: (0, 0))],
                      4, batch, seq, name="diff_attn")


def _moba_kernel(q_ref, k_ref, v_ref, tab_ref, o_ref, qs_ref, m_ref, acc_ref, s_ref, km_ref):
    blk = pl.program_id(2)
    n_blk = k_ref.shape[0] // MOBA_BLOCK
    assert n_blk <= LANES and MOBA_BLOCK == TQ == TK

    @pl.when(blk == 0)
    def _():
        km_ref[...] = jnp.zeros(km_ref.shape, jnp.float32)
        for n in range(n_blk):
            km_ref[n:n + 1, :] = jnp.sum(k_ref[n * MOBA_BLOCK:(n + 1) * MOBA_BLOCK, :].astype(jnp.float32),
                                         axis=0, keepdims=True) * (1.0 / MOBA_BLOCK)

    _stack_masked_q(q_ref[...], qs_ref, 2)
    _init_state(m_ref, acc_ref)
    km = km_ref[...]
    km_hi = km.astype(jnp.bfloat16)
    km_lo = (km - km_hi.astype(jnp.float32)).astype(jnp.bfloat16)
    qs = qs_ref[:, 0:PAIR_W]
    gate = (lax.dot_general(qs, km_hi, _NT, preferred_element_type=jnp.float32)
            + lax.dot_general(qs, km_lo, _NT, preferred_element_type=jnp.float32))
    cand = _lane_iota(gate.shape)
    cand_f = cand.astype(jnp.float32)
    own = N_SUB * blk + lax.broadcasted_iota(jnp.int32, gate.shape, 0) // (2 * TQ)
    g = jnp.where(cand < own, gate, NEG)
    sel = cand == own
    for _ in range(MOBA_TOPK):
        mx = jnp.max(g, axis=1, keepdims=True)
        first = jnp.min(jnp.where(g == mx, cand_f, 2.0 * LANES), axis=1, keepdims=True)
        pick = cand_f == first
        sel = jnp.logical_or(sel, jnp.logical_and(pick, mx > 0.5 * NEG))
        g = jnp.where(pick, -3e38, g)
    qs_ref[:, PAIR_W:2 * PAIR_W] = jnp.where(sel, 0.0, NEG).astype(jnp.bfloat16)

    def keys(m):
        k = _step_rows(k_ref, blk, m)
        block_of_key = N_SUB * (blk - m) + lax.broadcasted_iota(jnp.int32, (STEP_KEYS, LANES), 0) // TK
        onehot = jnp.where(_lane_iota((STEP_KEYS, LANES)) == block_of_key, 1.0, 0.0).astype(k.dtype)
        return jnp.concatenate([k, onehot], axis=1)

    _, run = _make_attend(
        2, blk, qs_ref, v_ref, m_ref, acc_ref, s_ref, keys=keys,
        finish=lambda sub, st, raw, m: raw + _table_bias(tab_ref, st, m, sub, FAR_TILE))
    run(0, blk + 1)
    _store_pair_outputs(o_ref, acc_ref)


def _moba(qkv, tabs, batch, seq):
    return _attn_call(_moba_kernel, 2, qkv, [tabs], [_tab_spec(8)], 2, batch, seq, q_width=2 * PAIR_W,
                      extra_scratch=[pltpu.VMEM((LANES, PAIR_W), jnp.float32)], name="moba_attn")


FOX_SKIP_BITS = 48.0


def _head_row_norms(x, hh):
    xf = x.astype(jnp.float32)
    sq = jnp.where(_lane_iota(xf.shape) // HEAD_DIM == hh, xf * xf, 0.0)
    return jnp.sqrt(jnp.max(jnp.sum(sq, axis=1, keepdims=True), axis=0, keepdims=True))


def _fox_kernel(q_ref, k_ref, v_ref, c_ref, o_ref, qs_ref, m_ref, acc_ref, s_ref, cq_ref, stat_ref):
    blk = pl.program_id(2)
    n_tiles = k_ref.shape[0] // TK
    assert n_tiles <= LANES

    @pl.when(blk == 0)
    def _():
        stat_ref[...] = jnp.zeros(stat_ref.shape, jnp.float32)
        for j in range(n_tiles):
            k_tile = k_ref[j * TK:(j + 1) * TK, :]
            for hh in range(2):
                stat_ref[hh:hh + 1, j:j + 1] = _head_row_norms(k_tile, hh)
                stat_ref[2 + hh:3 + hh, j:j + 1] = jnp.min(c_ref[hh:hh + 1, j * TK:(j + 1) * TK], axis=1, keepdims=True)

    q = q_ref[...]
    _stack_masked_q(q, qs_ref, 2)
    _init_state(m_ref, acc_ref)
    q0 = pl.multiple_of(blk * QBLK, QBLK)
    row = lax.broadcasted_iota(jnp.int32, (TQ, TQ), 0)
    col = lax.broadcasted_iota(jnp.int32, (TQ, TQ), 1)
    for sub in range(N_SUB):
        for hh in range(2):
            c_row = c_ref[hh:hh + 1, pl.ds(q0 + sub * TQ, TQ)]
            c_col = jnp.sum(jnp.where(row == col, c_row, 0.0), axis=1, keepdims=True)
            cq_ref[_rows(2, sub, hh), :] = jnp.broadcast_to(c_col, (TQ, LANES))

    key_pos = lax.broadcasted_iota(jnp.int32, (TQ, STEP_KEYS), 1)
    query_pos = lax.broadcasted_iota(jnp.int32, (TQ, STEP_KEYS), 0)

    def finish(sub, hh, raw, m):
        k0 = pl.multiple_of((blk - m) * STEP_KEYS, STEP_KEYS)
        cq = cq_ref[_rows(2, sub, hh), :]
        s = raw + (jnp.concatenate([cq] * (STEP_KEYS // LANES), axis=1) - c_ref[hh:hh + 1, pl.ds(k0, STEP_KEYS)])
        if isinstance(m, int) and m == 0:
            s = jnp.where(key_pos <= query_pos + sub * TQ, s, NEG)
        return s

    single, run = _make_attend(2, blk, qs_ref, v_ref, m_ref, acc_ref, s_ref,
                               keys=lambda m: _step_rows(k_ref, blk, m), finish=finish)
    single(0)

    tile = _lane_iota((1, LANES))
    needed = jnp.zeros((1, LANES), jnp.bool_)
    for hh in range(2):
        rows_h = [_rows(2, sub, hh) for sub in range(N_SUB)]
        m_min = functools.reduce(jnp.minimum, [jnp.min(m_ref[r, :], axis=0, keepdims=True) for r in rows_h])
        cq_max = functools.reduce(jnp.maximum, [jnp.max(cq_ref[r, :], axis=0, keepdims=True) for r in rows_h])
        bound = _head_row_norms(q, hh) * stat_ref[hh:hh + 1, :] + cq_max - stat_ref[2 + hh:3 + hh, :] + 1.0
        needed = jnp.logical_or(needed, bound >= m_min - FOX_SKIP_BITS)
    n_own = N_SUB * blk
    first_needed = jnp.min(jnp.where(jnp.logical_and(needed, tile < n_own), tile, n_own))
    run(1, blk - first_needed // STEP_TILES)
    _store_pair_outputs(o_ref, acc_ref)


def _fox(qkv, c, batch, seq):
    return _attn_call(_fox_kernel, 3, qkv, [c.reshape(batch, 2, 8, seq)],
                      [pl.BlockSpec((None, None, 8, seq), lambda p, b, i: (b, p, 0, 0))],
                      2, batch, seq, extra_scratch=[pltpu.VMEM((N_SUB * 2 * TQ, LANES), jnp.float32),
                                                    pltpu.VMEM((8, LANES), jnp.float32)], name="fox_attn")


OUT_TM = 512
ROUTER_W = LANES


def _outproj_router_kernel(oa_ref, ob_ref, oc_ref, od_ref, wo_ref, x_ref, g_ref, wr_ref, br_ref,
                           x1_ref, h2_ref, comb_ref):
    acc = x_ref[...]
    for m, o_ref in enumerate((oa_ref, ob_ref, oc_ref, od_ref)):
        acc = acc + jnp.dot(o_ref[...], wo_ref[m * GROUP_W:(m + 1) * GROUP_W, :], preferred_element_type=jnp.float32)
    x1_ref[...] = acc
    ms = jnp.mean(acc * acc, axis=1, keepdims=True)
    h = acc * lax.rsqrt(ms + RMS_EPS) * g_ref[...]
    hb = h.astype(jnp.bfloat16)
    h2_ref[...] = hb
    h_lo = (h - hb.astype(jnp.float32)).astype(jnp.bfloat16)
    logits = (jnp.dot(hb, wr_ref[0], preferred_element_type=jnp.float32)
              + jnp.dot(h_lo, wr_ref[0], preferred_element_type=jnp.float32)
              + jnp.dot(hb, wr_ref[1], preferred_element_type=jnp.float32)) + br_ref[...]
    lane = _lane_iota(logits.shape).astype(jnp.float32)
    big = 2.0 * LANES
    gmask = jnp.logical_and(lane >= N_EXPERTS, lane < N_EXPERTS + N_GROUPS)
    gl = jnp.where(gmask, logits, NEG)
    gmax = jnp.max(gl, axis=1, keepdims=True)
    glane = jnp.min(jnp.where(gl == gmax, lane, big), axis=1, keepdims=True)
    gsum = jnp.sum(jnp.where(gmask, jnp.exp(gl - gmax), 0.0), axis=1, keepdims=True)
    g_w = 1.0 / gsum
    e0 = (glane - N_EXPERTS) * EXPERTS_PER_GROUP
    emask = jnp.logical_and(lane >= e0, lane < e0 + EXPERTS_PER_GROUP)
    el = jnp.where(emask, logits, NEG)
    v1 = jnp.max(el, axis=1, keepdims=True)
    i1 = jnp.min(jnp.where(el == v1, lane, big), axis=1, keepdims=True)
    el2 = jnp.where(lane == i1, NEG, el)
    v2 = jnp.max(el2, axis=1, keepdims=True)
    i2 = jnp.min(jnp.where(el2 == v2, lane, big), axis=1, keepdims=True)
    e2 = jnp.exp(v2 - v1)
    den = 1.0 + e2
    comb_ref[...] = jnp.where(lane == i1, g_w / den, jnp.where(lane == i2, g_w * e2 / den, 0.0))


def _outproj_router(o_parts, w_out, x2d, g2, wr, br):
    n_tok = x2d.shape[0]
    tm = min(OUT_TM, n_tok)
    row = lambda i: (i, 0)
    fixed = lambda i: (0, 0)
    return pl.pallas_call(
        _outproj_router_kernel,
        out_shape=(jax.ShapeDtypeStruct((n_tok, D_MODEL), jnp.float32),
                   jax.ShapeDtypeStruct((n_tok, D_MODEL), jnp.bfloat16),
                   jax.ShapeDtypeStruct((n_tok, ROUTER_W), jnp.float32)),
        grid=(n_tok // tm,),
        in_specs=[pl.BlockSpec((tm, GROUP_W), row)] * 4 + [
            pl.BlockSpec((D_MODEL, D_MODEL), fixed),
            pl.BlockSpec((tm, D_MODEL), row),
            pl.BlockSpec((1, D_MODEL), fixed),
            pl.BlockSpec((2, D_MODEL, ROUTER_W), lambda i: (0, 0, 0)),
            pl.BlockSpec((1, ROUTER_W), fixed)],
        out_specs=(pl.BlockSpec((tm, D_MODEL), row), pl.BlockSpec((tm, D_MODEL), row),
                   pl.BlockSpec((tm, ROUTER_W), row)),
        compiler_params=pltpu.CompilerParams(dimension_semantics=("arbitrary",), vmem_limit_bytes=VMEM_LIMIT),
        name="outproj_router",
    )(*o_parts, w_out, x2d, g2, wr, br)


MOE_TM = 1024
MOE_ESTEP = 4
MOE_W = MOE_ESTEP * D_EXPERT
MOE_VMEM_LIMIT = 56 * 1024 * 1024


def _moe_kernel(h_ref, comb_ref, wg_ref, wu_ref, wd_ref, x1_ref, gf_ref, out_ref, acc_ref, *, final_norm):
    j = pl.program_id(1)

    @pl.when(j == 0)
    def _():
        acc_ref[...] = jnp.zeros(acc_ref.shape, jnp.float32)

    h = h_ref[...]
    gate = jnp.dot(h, wg_ref[...], preferred_element_type=jnp.float32)
    up = jnp.dot(h, wu_ref[...], preferred_element_type=jnp.float32)
    hid = (gate * jax.nn.sigmoid(gate)) * up
    comb = comb_ref[...]
    lane = _lane_iota(comb.shape)
    parts = []
    for i in range(MOE_ESTEP):
        ce = jnp.sum(jnp.where(lane == j * MOE_ESTEP + i, comb, 0.0), axis=1, keepdims=True)
        parts.append((hid[:, i * D_EXPERT:(i + 1) * D_EXPERT] * ce).astype(jnp.bfloat16))
    acc_ref[...] += jnp.dot(jnp.concatenate(parts, axis=1), wd_ref[...], preferred_element_type=jnp.float32)

    @pl.when(j == N_EXPERTS // MOE_ESTEP - 1)
    def _():
        y = x1_ref[...] + acc_ref[...]
        if final_norm:
            ms = jnp.mean(y * y, axis=1, keepdims=True)
            y = y * lax.rsqrt(ms + RMS_EPS) * gf_ref[...]
        out_ref[...] = y


def _moe(h2, comb, wg, wu, wd, x1, gf, final_norm):
    n_tok = h2.shape[0]
    tm = min(MOE_TM, n_tok)
    row = lambda i, e: (i, 0)
    return pl.pallas_call(
        functools.partial(_moe_kernel, final_norm=final_norm),
        out_shape=jax.ShapeDtypeStruct((n_tok, D_MODEL), jnp.float32),
        grid=(n_tok // tm, N_EXPERTS // MOE_ESTEP),
        in_specs=[pl.BlockSpec((tm, D_MODEL), row),
                  pl.BlockSpec((tm, ROUTER_W), row),
                  pl.BlockSpec((None, D_MODEL, MOE_W), lambda i, e: (e, 0, 0)),
                  pl.BlockSpec((None, D_MODEL, MOE_W), lambda i, e: (e, 0, 0)),
                  pl.BlockSpec((None, MOE_W, D_MODEL), lambda i, e: (e, 0, 0)),
                  pl.BlockSpec((tm, D_MODEL), row),
                  pl.BlockSpec((1, D_MODEL), lambda i, e: (0, 0))],
        out_specs=pl.BlockSpec((tm, D_MODEL), row),
        scratch_shapes=[pltpu.VMEM((tm, D_MODEL), jnp.float32)],
        compiler_params=pltpu.CompilerParams(dimension_semantics=("arbitrary", "arbitrary"),
                                             vmem_limit_bytes=MOE_VMEM_LIMIT),
        name="moe_experts",
    )(h2, comb, wg, wu, wd, x1, gf)


def _expert_weights(w_gate, w_up, w_down):
    n_steps = N_EXPERTS // MOE_ESTEP

    def side_by_side(w):
        w = w.astype(jnp.bfloat16).reshape(n_steps, MOE_ESTEP, D_MODEL, D_EXPERT)
        return w.transpose(0, 2, 1, 3).reshape(n_steps, D_MODEL, MOE_W)

    return side_by_side(w_gate), side_by_side(w_up), w_down.astype(jnp.bfloat16).reshape(n_steps, MOE_W, D_MODEL)


def _split_bf16(w):
    hi = w.astype(jnp.bfloat16)
    lo = (w - hi.astype(jnp.float32)).astype(jnp.bfloat16)
    return jnp.stack([hi, lo])


def _qkv_col_scale():
    s = np.ones((QKV_W,), np.float32)
    for group, dim in enumerate((HEAD_DIM, DIFF_DIM, HEAD_DIM, HEAD_DIM)):
        s[3 * group * GROUP_W:(3 * group + 1) * GROUP_W] = LOG2E * dim ** -0.5
    return s


def _forget_weights(w_f, b_f):
    rows = jnp.zeros((F_ROWS, D_MODEL), jnp.float32)
    bias = jnp.zeros((F_ROWS, 1), jnp.float32)
    for h in range(HEADS_PER_MIXER):
        r = (h // 2) * 8 + (h % 2)
        rows = rows.at[r].set(w_f[:, h])
        bias = bias.at[r, 0].set(b_f[h])
    return _split_bf16(rows), bias


def kernel(x, rel_bias, ln1, w_in, w_out, lam_q1, lam_k1, lam_q2, lam_k2, subln_g, b_forget,
           ln2, w_group, b_group, w_expert, b_expert, w_gate, w_up, w_down, ln_f):
    batch, seq, _ = x.shape
    depth = ln1.shape[0]
    assert seq % QBLK == 0 and seq % STEP_KEYS == 0
    tabs = _bias_tables(rel_bias)
    xt = x.reshape(batch * seq, D_MODEL)
    gf = ln_f.reshape(1, D_MODEL)
    col_scale = _qkv_col_scale()
    for l in range(depth):
        lambda_init = 0.8 - 0.6 * math.exp(-0.3 * l)
        wf, bf = _forget_weights(w_in[l][:, QKV_W:], b_forget[l])
        w_qkv = (w_in[l][:, :QKV_W] * col_scale).astype(jnp.bfloat16)
        qkv, lf = _norm_proj(xt, ln1[l].reshape(1, D_MODEL), w_qkv, wf, bf, batch, seq)
        c = _cumsum(lf)

        o_a = _dilated(qkv, tabs, batch, seq)
        lamv = jnp.zeros((8, LANES), jnp.float32)
        lamv = lamv.at[0:4, 0:DIFF_DIM].set(jnp.stack([lam_q1[l], lam_k1[l], lam_q2[l], lam_k2[l]]))
        g_sub = jnp.tile(subln_g[l], 2).reshape(1, PAIR_W)
        o_b = _diff(qkv, tabs, lamv, g_sub, lambda_init, batch, seq)
        o_c = _moba(qkv, tabs, batch, seq)
        o_d = _fox(qkv, c, batch, seq)

        w_router = jnp.zeros((D_MODEL, ROUTER_W), jnp.float32)
        w_router = w_router.at[:, :N_EXPERTS].set(w_expert[l]).at[:, N_EXPERTS:N_EXPERTS + N_GROUPS].set(w_group[l])
        b_router = jnp.zeros((1, ROUTER_W), jnp.float32)
        b_router = b_router.at[0, :N_EXPERTS].set(b_expert[l].reshape(-1)).at[0, N_EXPERTS:N_EXPERTS + N_GROUPS].set(b_group[l])
        x1, h2, comb = _outproj_router([o_a, o_b, o_c, o_d], w_out[l].astype(jnp.bfloat16), xt,
                                       ln2[l].reshape(1, D_MODEL), _split_bf16(w_router), b_router)
        xt = _moe(h2, comb, *_expert_weights(w_gate[l], w_up[l], w_down[l]), x1, gf, final_norm=(l == depth - 1))
    return xt.reshape(batch, seq, D_MODEL)
```

```python
import functools
import math

import jax
import jax.numpy as jnp
import numpy as np
from jax import lax
from jax.experimental import pallas as pl
from jax.experimental.pallas import tpu as pltpu

D_MODEL = 1024
HEAD_DIM = 64
HEADS_PER_MIXER = 4
GROUP_W = HEADS_PER_MIXER * HEAD_DIM
QKV_W = 12 * GROUP_W
DIFF_DIM = HEAD_DIM // 2
MOBA_BLOCK = 256
MOBA_TOPK = 3
N_BUCKETS = 32
MAX_DISTANCE = 2048
N_BIAS_HEADS = 12
N_GROUPS = 4
EXPERTS_PER_GROUP = 8
N_EXPERTS = 32
D_EXPERT = 256
RMS_EPS = 1e-6
NEG = -1e30
LOG2E = math.log2(math.e)

LANES = 128
PAIR_W = 2 * HEAD_DIM
ACC_W = 2 * PAIR_W
TQ = 256
TK = 256
N_SUB = 2
QBLK = N_SUB * TQ
STEP_TILES = 2
STEP_KEYS = STEP_TILES * TK
N_DIL_TILES = MAX_DISTANCE // TK + 1
FAR_TILE = 7
MIN_DIST = -1
N_TAB_TILES = N_DIL_TILES + 1 - MIN_DIST
VMEM_LIMIT = 48 * 1024 * 1024

_NT = (((1,), (1,)), ((), ()))


def _t5_thresholds():
    d = np.arange(0, 4 * MAX_DISTANCE, dtype=np.int64)
    max_exact = N_BUCKETS // 2
    df = np.maximum(d, 1).astype(np.float32)
    large = max_exact + (np.log(df / np.float32(max_exact)) / np.float32(math.log(MAX_DISTANCE / max_exact))
                         * np.float32(N_BUCKETS - max_exact)).astype(np.int32)
    bucket = np.where(d < max_exact, d, np.minimum(large, N_BUCKETS - 1))
    return [int(np.argmax(bucket >= b)) for b in range(N_BUCKETS)]


_T5_THR = _t5_thresholds()
assert (FAR_TILE - 1) * TK + 1 >= _T5_THR[N_BUCKETS - 1]


def _lane_iota(shape):
    return lax.broadcasted_iota(jnp.int32, shape, len(shape) - 1)


def _bias_table_kernel(bias_ref, out_ref):
    h = pl.program_id(0)
    n = pl.program_id(1) + MIN_DIST
    i = lax.broadcasted_iota(jnp.int32, (TQ, TK), 0)
    j = lax.broadcasted_iota(jnp.int32, (TQ, TK), 1)
    d = n * TK + i - j
    is_dil = h < HEADS_PER_MIXER
    far = jnp.logical_and(jnp.logical_not(is_dil), n >= FAR_TILE)
    d_b = jnp.where(far, 4 * MAX_DISTANCE, d)
    val = jnp.full((TQ, TK), bias_ref[h, 0], jnp.float32)
    for b in range(1, N_BUCKETS):
        val = jnp.where(d_b >= _T5_THR[b], bias_ref[h, b], val)
    mult = ((d <= 128).astype(jnp.int32)
            + jnp.logical_and((d & 3) == 0, d <= 512).astype(jnp.int32)
            + jnp.logical_and((d & 15) == 0, d <= 2048).astype(jnp.int32))
    logm = jnp.where(mult == 3, math.log(3.0), jnp.where(mult == 2, math.log(2.0), 0.0))
    val = val + jnp.where(is_dil, logm, 0.0)
    valid = jnp.logical_and(d >= 0, jnp.logical_or(mult > 0, jnp.logical_not(is_dil)))
    out_ref[...] = jnp.where(valid, val * LOG2E, NEG)


def _bias_tables(rel_bias):
    bias_h = rel_bias.T.astype(jnp.float32)
    return pl.pallas_call(
        _bias_table_kernel,
        out_shape=jax.ShapeDtypeStruct((N_BIAS_HEADS, N_TAB_TILES, TQ, TK), jnp.float32),
        grid=(N_BIAS_HEADS, N_TAB_TILES),
        in_specs=[pl.BlockSpec(memory_space=pltpu.SMEM)],
        out_specs=pl.BlockSpec((None, None, TQ, TK), lambda h, n: (h, n, 0, 0)),
        name="bias_tables",
    )(bias_h)


PROJ_TM = 512
PROJ_CW = 512
F_ROWS = 16


def _norm_proj_kernel(x_ref, g_ref, w_ref, wf_ref, bf_ref, qkv_ref, lf_ref):
    x = x_ref[...]
    ms = jnp.mean(x * x, axis=1, keepdims=True)
    h = x * lax.rsqrt(ms + RMS_EPS) * g_ref[...]
    hb = h.astype(jnp.bfloat16)
    for c in range(QKV_W // PROJ_CW):
        cols = slice(c * PROJ_CW, (c + 1) * PROJ_CW)
        qkv_ref[:, cols] = jnp.dot(hb, w_ref[:, cols], preferred_element_type=jnp.float32).astype(jnp.bfloat16)
    h_lo = (h - hb.astype(jnp.float32)).astype(jnp.bfloat16)
    wf_hi = wf_ref[0]
    wf_lo = wf_ref[1]
    z = (lax.dot_general(wf_hi, hb, _NT, preferred_element_type=jnp.float32)
         + lax.dot_general(wf_hi, h_lo, _NT, preferred_element_type=jnp.float32)
         + lax.dot_general(wf_lo, hb, _NT, preferred_element_type=jnp.float32))
    z = z + bf_ref[...]
    lf_ref[...] = jnp.minimum(z, 0.0) - jnp.log(1.0 + jnp.exp(-jnp.abs(z)))


def _norm_proj(x2d, g, w_qkv, wf, bf, batch, seq):
    n_tok = x2d.shape[0]
    tm = min(PROJ_TM, seq)
    per_b = seq // tm
    return pl.pallas_call(
        _norm_proj_kernel,
        out_shape=(jax.ShapeDtypeStruct((n_tok, QKV_W), jnp.bfloat16),
                   jax.ShapeDtypeStruct((batch * F_ROWS, seq), jnp.float32)),
        grid=(n_tok // tm,),
        in_specs=[pl.BlockSpec((tm, D_MODEL), lambda i: (i, 0)),
                  pl.BlockSpec((1, D_MODEL), lambda i: (0, 0)),
                  pl.BlockSpec((D_MODEL, QKV_W), lambda i: (0, 0)),
                  pl.BlockSpec((2, F_ROWS, D_MODEL), lambda i: (0, 0, 0)),
                  pl.BlockSpec((F_ROWS, 1), lambda i: (0, 0))],
        out_specs=(pl.BlockSpec((tm, QKV_W), lambda i: (i, 0)),
                   pl.BlockSpec((F_ROWS, tm), lambda i: (i // per_b, i % per_b))),
        compiler_params=pltpu.CompilerParams(dimension_semantics=("arbitrary",), vmem_limit_bytes=VMEM_LIMIT),
        name="norm_proj",
    )(x2d, g, w_qkv, wf, bf)


def _split3(x):
    x1 = x.astype(jnp.bfloat16)
    r1 = x - x1.astype(jnp.float32)
    x2 = r1.astype(jnp.bfloat16)
    x3 = (r1 - x2.astype(jnp.float32)).astype(jnp.bfloat16)
    return x1, x2, x3


def _cumsum_kernel(lf_ref, c_ref):
    rows, seq = lf_ref.shape
    r = lax.broadcasted_iota(jnp.int32, (LANES, LANES), 0)
    c = lax.broadcasted_iota(jnp.int32, (LANES, LANES), 1)
    upper = jnp.where(r <= c, 1.0, 0.0).astype(jnp.bfloat16)

    def body(i, carry):
        off = pl.multiple_of(i * LANES, LANES)
        x1, x2, x3 = _split3(lf_ref[:, pl.ds(off, LANES)])
        cs = (jnp.dot(x1, upper, preferred_element_type=jnp.float32)
              + jnp.dot(x2, upper, preferred_element_type=jnp.float32)
              + jnp.dot(x3, upper, preferred_element_type=jnp.float32)) + carry
        c_ref[:, pl.ds(off, LANES)] = cs * LOG2E
        return cs[:, LANES - 1:LANES]

    lax.fori_loop(0, seq // LANES, body, jnp.zeros((rows, 1), jnp.float32))


def _cumsum(lf):
    return pl.pallas_call(
        _cumsum_kernel,
        out_shape=jax.ShapeDtypeStruct(lf.shape, jnp.float32),
        name="forget_cumsum",
    )(lf)


def _rows(n_streams, sub, st):
    rb = sub * n_streams + st
    return slice(rb * TQ, (rb + 1) * TQ)


def _stack_masked_q(q, qs_ref, n_streams):
    lane = _lane_iota((TQ, PAIR_W))
    width = PAIR_W // n_streams
    for sub in range(N_SUB):
        q_half = q[sub * TQ:(sub + 1) * TQ]
        for st in range(n_streams):
            qs_ref[_rows(n_streams, sub, st), 0:PAIR_W] = jnp.where(lane // width == st, q_half,
                                                                    jnp.zeros_like(q_half))


def _init_state(m_ref, acc_ref):
    m_ref[...] = jnp.full(m_ref.shape, NEG, jnp.float32)
    acc_ref[...] = jnp.zeros(acc_ref.shape, jnp.float32)


def _flash_step(rows, s, v_aug, m_ref, acc_ref):
    m_prev = m_ref[rows, :]
    m_new = jnp.maximum(m_prev, jnp.max(s, axis=1, keepdims=True))
    alpha = jnp.exp2(m_prev - m_new)
    p = jnp.concatenate([jnp.exp2(s[:, c * LANES:(c + 1) * LANES] - m_new) for c in range(s.shape[1] // LANES)],
                        axis=1).astype(v_aug.dtype)
    pv = jnp.dot(p, v_aug, preferred_element_type=jnp.float32)
    acc_ref[rows, :] = jnp.concatenate([alpha, alpha], axis=1) * acc_ref[rows, :] + pv
    m_ref[rows, :] = m_new


def _normalised(acc_ref, rows):
    return acc_ref[rows, 0:PAIR_W] / acc_ref[rows, PAIR_W:ACC_W]


def _table_bias(tab_ref, hh, m, sub, cap):
    return jnp.concatenate([tab_ref[hh, jnp.minimum(2 * m + sub - w, cap) - MIN_DIST] for w in range(STEP_TILES)],
                           axis=1)


def _step_rows(ref, blk, m):
    k0 = pl.multiple_of((blk - m) * STEP_KEYS, STEP_KEYS)
    return ref[pl.ds(k0, STEP_KEYS), :]


def _make_attend(n_streams, blk, qs_ref, v_ref, m_ref, acc_ref, s_ref, keys, finish):
    blocks = [(sub, st) for sub in range(N_SUB) for st in range(n_streams)]

    def raw(rows, rhs):
        return lax.dot_general(qs_ref[rows, :], rhs, _NT, preferred_element_type=jnp.float32)

    def v_aug(m):
        v = _step_rows(v_ref, blk, m)
        return jnp.concatenate([v, jnp.ones_like(v)], axis=1)

    def single(m):
        rhs = keys(m)
        va = v_aug(m)
        for sub, st in blocks:
            rows = _rows(n_streams, sub, st)
            _flash_step(rows, finish(sub, st, raw(rows, rhs), m), va, m_ref, acc_ref)

    def run(m0, count):
        @pl.when(count > 0)
        def _():
            rhs = keys(m0)
            for sub, st in blocks:
                rows = _rows(n_streams, sub, st)
                s_ref[rows, :] = raw(rows, rhs)

        def body(i, carry):
            m = m0 + i
            va = v_aug(m)
            rhs_next = keys(jnp.minimum(m + 1, blk))
            for sub, st in blocks:
                rows = _rows(n_streams, sub, st)
                cur = s_ref[rows, :]
                s_ref[rows, :] = raw(rows, rhs_next)
                _flash_step(rows, finish(sub, st, cur, m), va, m_ref, acc_ref)
            return carry

        lax.fori_loop(0, count, body, 0)

    return single, run


def _attn_call(kernel, group, qkv, extra_inputs, extra_specs, n_streams, batch, seq, *, q_width=PAIR_W,
               extra_scratch=(), name=None):
    nq = seq // QBLK
    qc, kc, vc = 6 * group, 6 * group + 2, 6 * group + 4
    rows = N_SUB * n_streams * TQ
    return pl.pallas_call(
        kernel,
        out_shape=jax.ShapeDtypeStruct((batch * seq, GROUP_W), jnp.bfloat16),
        grid=(2, batch, nq),
        in_specs=[pl.BlockSpec((QBLK, PAIR_W), lambda p, b, i: (b * nq + i, qc + p)),
                  pl.BlockSpec((seq, PAIR_W), lambda p, b, i: (b, kc + p)),
                  pl.BlockSpec((seq, PAIR_W), lambda p, b, i: (b, vc + p))] + list(extra_specs),
        out_specs=pl.BlockSpec((QBLK, PAIR_W), lambda p, b, i: (b * nq + i, p)),
        scratch_shapes=[pltpu.VMEM((rows, q_width), jnp.bfloat16),
                        pltpu.VMEM((rows, LANES), jnp.float32),
                        pltpu.VMEM((rows, ACC_W), jnp.float32),
                        pltpu.VMEM((rows, STEP_KEYS), jnp.float32)
                        ] + list(extra_scratch),
        compiler_params=pltpu.CompilerParams(dimension_semantics=("arbitrary", "arbitrary", "arbitrary"),
                                             vmem_limit_bytes=VMEM_LIMIT),
        name=name,
    )(qkv, qkv, qkv, *extra_inputs)


def _tab_spec(first_head):
    return pl.BlockSpec((2, N_TAB_TILES, TQ, TK), lambda p, b, i: (first_head // 2 + p, 0, 0, 0))


def _merge_pair(o0, o1):
    return jnp.where(_lane_iota(o0.shape) < HEAD_DIM, o0, o1)


def _store_pair_outputs(o_ref, acc_ref):
    for sub in range(N_SUB):
        o = _merge_pair(_normalised(acc_ref, _rows(2, sub, 0)), _normalised(acc_ref, _rows(2, sub, 1)))
        o_ref[sub * TQ:(sub + 1) * TQ, :] = o.astype(o_ref.dtype)


def _dilated_kernel(q_ref, k_ref, v_ref, tab_ref, o_ref, qs_ref, m_ref, acc_ref, s_ref):
    blk = pl.program_id(2)
    _stack_masked_q(q_ref[...], qs_ref, 2)
    _init_state(m_ref, acc_ref)
    _, run = _make_attend(
        2, blk, qs_ref, v_ref, m_ref, acc_ref, s_ref,
        keys=lambda m: _step_rows(k_ref, blk, m),
        finish=lambda sub, st, raw, m: raw + _table_bias(tab_ref, st, m, sub, N_DIL_TILES))
    run(0, jnp.minimum(blk, (N_DIL_TILES - 1) // STEP_TILES) + 1)
    _store_pair_outputs(o_ref, acc_ref)


def _dilated(qkv, tabs, batch, seq):
    return _attn_call(_dilated_kernel, 0, qkv, [tabs], [_tab_spec(0)], 2, batch, seq, name="dilated_attn")


def _diff_kernel(q_ref, k_ref, v_ref, tab_ref, lam_ref, g_ref, o_ref, qs_ref, m_ref, acc_ref, s_ref, *,
                 lambda_init):
    blk = pl.program_id(2)
    _stack_masked_q(q_ref[...], qs_ref, 4)
    _init_state(m_ref, acc_ref)
    _, run = _make_attend(
        4, blk, qs_ref, v_ref, m_ref, acc_ref, s_ref,
        keys=lambda m: _step_rows(k_ref, blk, m),
        finish=lambda sub, st, raw, m: raw + _table_bias(tab_ref, st // 2, m, sub, FAR_TILE))
    run(0, blk + 1)

    lamv = lam_ref[...]
    lam = (jnp.exp(jnp.sum(lamv[0:1] * lamv[1:2], axis=1, keepdims=True))
           - jnp.exp(jnp.sum(lamv[2:3] * lamv[3:4], axis=1, keepdims=True)) + lambda_init)
    first = _lane_iota((TQ, PAIR_W)) < HEAD_DIM
    for sub in range(N_SUB):
        part = [_normalised(acc_ref, _rows(4, sub, st)) for st in range(4)]
        o = _merge_pair(part[0] - lam * part[1], part[2] - lam * part[3])
        sq = o * o
        ms0 = jnp.sum(jnp.where(first, sq, 0.0), axis=1, keepdims=True)
        ms1 = jnp.sum(jnp.where(first, 0.0, sq), axis=1, keepdims=True)
        ms = jnp.where(first, ms0, ms1) * (1.0 / HEAD_DIM)
        y = (o * lax.rsqrt(ms + RMS_EPS) * g_ref[...]) * (1.0 - lambda_init)
        o_ref[sub * TQ:(sub + 1) * TQ, :] = y.astype(o_ref.dtype)


def _diff(qkv, tabs, lamv, g_sub, lambda_init, batch, seq):
    return _attn_call(functools.partial(_diff_kernel, lambda_init=lambda_init), 1, qkv, [tabs, lamv, g_sub],
                      [_tab_spec(4), pl.BlockSpec((8, LANES), lambda p, b, i: (0, 0)),
                       pl.BlockSpec((1, PAIR_W), lambda p, b, i: (0, 0))],
                      4, batch, seq, name="diff_attn")


def _moba_kernel(q_ref, k_ref, v_ref, tab_ref, o_ref, qs_ref, m_ref, acc_ref, s_ref, km_ref):
    blk = pl.program_id(2)
    n_blk = k_ref.shape[0] // MOBA_BLOCK
    assert n_blk <= LANES and MOBA_BLOCK == TQ == TK

    @pl.when(blk == 0)
    def _():
        km_ref[...] = jnp.zeros(km_ref.shape, jnp.float32)
        for n in range(n_blk):
            km_ref[n:n + 1, :] = jnp.sum(k_ref[n * MOBA_BLOCK:(n + 1) * MOBA_BLOCK, :].astype(jnp.float32),
                                         axis=0, keepdims=True) * (1.0 / MOBA_BLOCK)

    _stack_masked_q(q_ref[...], qs_ref, 2)
    _init_state(m_ref, acc_ref)
    km = km_ref[...]
    km_hi = km.astype(jnp.bfloat16)
    km_lo = (km - km_hi.astype(jnp.float32)).astype(jnp.bfloat16)
    qs = qs_ref[:, 0:PAIR_W]
    gate = (lax.dot_general(qs, km_hi, _NT, preferred_element_type=jnp.float32)
            + lax.dot_general(qs, km_lo, _NT, preferred_element_type=jnp.float32))
    cand = _lane_iota(gate.shape)
    cand_f = cand.astype(jnp.float32)
    own = N_SUB * blk + lax.broadcasted_iota(jnp.int32, gate.shape, 0) // (2 * TQ)
    g = jnp.where(cand < own, gate, NEG)
    sel = cand == own
    for _ in range(MOBA_TOPK):
        mx = jnp.max(g, axis=1, keepdims=True)
        first = jnp.min(jnp.where(g == mx, cand_f, 2.0 * LANES), axis=1, keepdims=True)
        pick = cand_f == first
        sel = jnp.logical_or(sel, jnp.logical_and(pick, mx > 0.5 * NEG))
        g = jnp.where(pick, -3e38, g)
    qs_ref[:, PAIR_W:2 * PAIR_W] = jnp.where(sel, 0.0, NEG).astype(jnp.bfloat16)

    def keys(m):
        k = _step_rows(k_ref, blk, m)
        block_of_key = N_SUB * (blk - m) + lax.broadcasted_iota(jnp.int32, (STEP_KEYS, LANES), 0) // TK
        onehot = jnp.where(_lane_iota((STEP_KEYS, LANES)) == block_of_key, 1.0, 0.0).astype(k.dtype)
        return jnp.concatenate([k, onehot], axis=1)

    _, run = _make_attend(
        2, blk, qs_ref, v_ref, m_ref, acc_ref, s_ref, keys=keys,
        finish=lambda sub, st, raw, m: raw + _table_bias(tab_ref, st, m, sub, FAR_TILE))
    run(0, blk + 1)
    _store_pair_outputs(o_ref, acc_ref)


def _moba(qkv, tabs, batch, seq):
    return _attn_call(_moba_kernel, 2, qkv, [tabs], [_tab_spec(8)], 2, batch, seq, q_width=2 * PAIR_W,
                      extra_scratch=[pltpu.VMEM((LANES, PAIR_W), jnp.float32)], name="moba_attn")


FOX_SKIP_BITS = 48.0


def _head_row_norms(x, hh):
    xf = x.astype(jnp.float32)
    sq = jnp.where(_lane_iota(xf.shape) // HEAD_DIM == hh, xf * xf, 0.0)
    return jnp.sqrt(jnp.max(jnp.sum(sq, axis=1, keepdims=True), axis=0, keepdims=True))


def _fox_kernel(q_ref, k_ref, v_ref, c_ref, o_ref, qs_ref, m_ref, acc_ref, s_ref, cq_ref, stat_ref):
    blk = pl.program_id(2)
    n_tiles = k_ref.shape[0] // TK
    assert n_tiles <= LANES

    @pl.when(blk == 0)
    def _():
        stat_ref[...] = jnp.zeros(stat_ref.shape, jnp.float32)
        for j in range(n_tiles):
            k_tile = k_ref[j * TK:(j + 1) * TK, :]
            for hh in range(2):
                stat_ref[hh:hh + 1, j:j + 1] = _head_row_norms(k_tile, hh)
                stat_ref[2 + hh:3 + hh, j:j + 1] = jnp.min(c_ref[hh:hh + 1, j * TK:(j + 1) * TK], axis=1, keepdims=True)

    q = q_ref[...]
    _stack_masked_q(q, qs_ref, 2)
    _init_state(m_ref, acc_ref)
    q0 = pl.multiple_of(blk * QBLK, QBLK)
    row = lax.broadcasted_iota(jnp.int32, (TQ, TQ), 0)
    col = lax.broadcasted_iota(jnp.int32, (TQ, TQ), 1)
    for sub in range(N_SUB):
        for hh in range(2):
            c_row = c_ref[hh:hh + 1, pl.ds(q0 + sub * TQ, TQ)]
            c_col = jnp.sum(jnp.where(row == col, c_row, 0.0), axis=1, keepdims=True)
            cq_ref[_rows(2, sub, hh), :] = jnp.broadcast_to(c_col, (TQ, LANES))

    key_pos = lax.broadcasted_iota(jnp.int32, (TQ, STEP_KEYS), 1)
    query_pos = lax.broadcasted_iota(jnp.int32, (TQ, STEP_KEYS), 0)

    def finish(sub, hh, raw, m):
        k0 = pl.multiple_of((blk - m) * STEP_KEYS, STEP_KEYS)
        cq = cq_ref[_rows(2, sub, hh), :]
        s = raw + (jnp.concatenate([cq] * (STEP_KEYS // LANES), axis=1) - c_ref[hh:hh + 1, pl.ds(k0, STEP_KEYS)])
        if isinstance(m, int) and m == 0:
            s = jnp.where(key_pos <= query_pos + sub * TQ, s, NEG)
        return s

    single, run = _make_attend(2, blk, qs_ref, v_ref, m_ref, acc_ref, s_ref,
                               keys=lambda m: _step_rows(k_ref, blk, m), finish=finish)
    single(0)

    tile = _lane_iota((1, LANES))
    needed = jnp.zeros((1, LANES), jnp.bool_)
    for hh in range(2):
        rows_h = [_rows(2, sub, hh) for sub in range(N_SUB)]
        m_min = functools.reduce(jnp.minimum, [jnp.min(m_ref[r, :], axis=0, keepdims=True) for r in rows_h])
        cq_max = functools.reduce(jnp.maximum, [jnp.max(cq_ref[r, :], axis=0, keepdims=True) for r in rows_h])
        bound = _head_row_norms(q, hh) * stat_ref[hh:hh + 1, :] + cq_max - stat_ref[2 + hh:3 + hh, :] + 1.0
        needed = jnp.logical_or(needed, bound >= m_min - FOX_SKIP_BITS)
    n_own = N_SUB * blk
    first_needed = jnp.min(jnp.where(jnp.logical_and(needed, tile < n_own), tile, n_own))
    run(1, blk - first_needed // STEP_TILES)
    _store_pair_outputs(o_ref, acc_ref)


def _fox(qkv, c, batch, seq):
    return _attn_call(_fox_kernel, 3, qkv, [c.reshape(batch, 2, 8, seq)],
                      [pl.BlockSpec((None, None, 8, seq), lambda p, b, i: (b, p, 0, 0))],
                      2, batch, seq, extra_scratch=[pltpu.VMEM((N_SUB * 2 * TQ, LANES), jnp.float32),
                                                    pltpu.VMEM((8, LANES), jnp.float32)], name="fox_attn")


OUT_TM = 512
ROUTER_W = LANES
ROUTE_ROWS = 8


PACK_W = D_MODEL // 2


def _pack_bf16_pair(lo, hi):
    lo_bits = lax.bitcast_convert_type(lo.astype(jnp.bfloat16).astype(jnp.float32), jnp.uint32)
    hi_bits = lax.bitcast_convert_type(hi.astype(jnp.bfloat16).astype(jnp.float32), jnp.uint32)
    word = lax.shift_right_logical(lo_bits, jnp.uint32(16)) | (hi_bits & jnp.uint32(0xFFFF0000))
    return lax.bitcast_convert_type(word, jnp.int32)


def _unpack_bf16_pair(word):
    bits = lax.bitcast_convert_type(word, jnp.uint32)
    lo = lax.bitcast_convert_type(lax.shift_left(bits, jnp.uint32(16)), jnp.float32)
    hi = lax.bitcast_convert_type(bits & jnp.uint32(0xFFFF0000), jnp.float32)
    return lo, hi


def _outproj_router_kernel(oa_ref, ob_ref, oc_ref, od_ref, wo_ref, x_ref, g_ref, wr_ref, br_ref,
                           x1_ref, h2_ref, route_ref, cw_ref):
    acc = x_ref[...]
    for m, o_ref in enumerate((oa_ref, ob_ref, oc_ref, od_ref)):
        acc = acc + jnp.dot(o_ref[...], wo_ref[m * GROUP_W:(m + 1) * GROUP_W, :], preferred_element_type=jnp.float32)
    x1_ref[...] = acc
    ms = jnp.mean(acc * acc, axis=1, keepdims=True)
    h = acc * lax.rsqrt(ms + RMS_EPS) * g_ref[...]
    hb = h.astype(jnp.bfloat16)
    h2_ref[...] = _pack_bf16_pair(h[:, 0:PACK_W], h[:, PACK_W:D_MODEL])
    h_lo = (h - hb.astype(jnp.float32)).astype(jnp.bfloat16)
    logits = (jnp.dot(hb, wr_ref[0], preferred_element_type=jnp.float32)
              + jnp.dot(h_lo, wr_ref[0], preferred_element_type=jnp.float32)
              + jnp.dot(hb, wr_ref[1], preferred_element_type=jnp.float32)) + br_ref[...]
    lane = _lane_iota(logits.shape).astype(jnp.float32)
    big = 2.0 * LANES
    gmask = jnp.logical_and(lane >= N_EXPERTS, lane < N_EXPERTS + N_GROUPS)
    gl = jnp.where(gmask, logits, NEG)
    gmax = jnp.max(gl, axis=1, keepdims=True)
    glane = jnp.min(jnp.where(gl == gmax, lane, big), axis=1, keepdims=True)
    gsum = jnp.sum(jnp.where(gmask, jnp.exp(gl - gmax), 0.0), axis=1, keepdims=True)
    g_w = 1.0 / gsum
    e0 = (glane - N_EXPERTS) * EXPERTS_PER_GROUP
    emask = jnp.logical_and(lane >= e0, lane < e0 + EXPERTS_PER_GROUP)
    el = jnp.where(emask, logits, NEG)
    v1 = jnp.max(el, axis=1, keepdims=True)
    i1 = jnp.min(jnp.where(el == v1, lane, big), axis=1, keepdims=True)
    el2 = jnp.where(lane == i1, NEG, el)
    v2 = jnp.max(el2, axis=1, keepdims=True)
    i2 = jnp.min(jnp.where(el2 == v2, lane, big), axis=1, keepdims=True)
    e2 = jnp.exp(v2 - v1)
    den = 1.0 + e2
    c1 = g_w / den
    c2 = g_w * e2 / den
    quad = jnp.where(lane == 0.0, i1, jnp.where(lane == 1.0, i2, jnp.where(lane == 2.0, c1,
                                                                             jnp.where(lane == 3.0, c2, 0.0))))
    cw_ref[...] = jnp.where(lane == 0.0, c1, jnp.where(lane == 1.0, c2, 0.0))
    route_ref[...] = quad.T[0:ROUTE_ROWS, :]


def _outproj_router(o_parts, w_out, x2d, g2, wr, br):
    n_tok = x2d.shape[0]
    tm = min(OUT_TM, n_tok)
    row = lambda i: (i, 0)
    fixed = lambda i: (0, 0)
    return pl.pallas_call(
        _outproj_router_kernel,
        out_shape=(jax.ShapeDtypeStruct((n_tok, D_MODEL), jnp.float32),
                   jax.ShapeDtypeStruct((n_tok, PACK_W), jnp.int32),
                   jax.ShapeDtypeStruct((ROUTE_ROWS, n_tok), jnp.float32),
                   jax.ShapeDtypeStruct((n_tok, ROUTER_W), jnp.float32)),
        grid=(n_tok // tm,),
        in_specs=[pl.BlockSpec((tm, GROUP_W), row)] * 4 + [
            pl.BlockSpec((D_MODEL, D_MODEL), fixed),
            pl.BlockSpec((tm, D_MODEL), row),
            pl.BlockSpec((1, D_MODEL), fixed),
            pl.BlockSpec((2, D_MODEL, ROUTER_W), lambda i: (0, 0, 0)),
            pl.BlockSpec((1, ROUTER_W), fixed)],
        out_specs=(pl.BlockSpec((tm, D_MODEL), row), pl.BlockSpec((tm, PACK_W), row),
                   pl.BlockSpec((ROUTE_ROWS, tm), lambda i: (0, i)), pl.BlockSpec((tm, ROUTER_W), row)),
        compiler_params=pltpu.CompilerParams(dimension_semantics=("arbitrary",), vmem_limit_bytes=VMEM_LIMIT),
        name="outproj_router",
    )(*o_parts, w_out, x2d, g2, wr, br)


ROW_TILE = 256
DISP_TM = 1024
SC_CORES = 2
SC_SUBCORES = 16
SC_WORKERS = SC_CORES * SC_SUBCORES
SC_CHUNK = 128


def _dispatch_kernel(route_ref, rank_ref, cnt_ref, tri_ref, carry_ref):
    step = pl.program_id(0)
    tm = route_ref.shape[1]

    @pl.when(step == 0)
    def _():
        r = lax.broadcasted_iota(jnp.int32, (tm, tm), 0)
        c = lax.broadcasted_iota(jnp.int32, (tm, tm), 1)
        tri_ref[...] = jnp.where(r <= c, 1.0, 0.0).astype(jnp.bfloat16)
        carry_ref[...] = jnp.zeros(carry_ref.shape, jnp.float32)

    e1 = route_ref[0:1, :]
    e2 = route_ref[1:2, :]
    expert = lax.broadcasted_iota(jnp.int32, (N_EXPERTS, tm), 0).astype(jnp.float32)
    hit1 = e1 == expert
    hit2 = e2 == expert
    hits = jnp.where(jnp.logical_or(hit1, hit2), 1.0, 0.0).astype(jnp.bfloat16)
    cum = jnp.dot(hits, tri_ref[...], preferred_element_type=jnp.float32) + carry_ref[...]
    rank1 = jnp.sum(jnp.where(hit1, cum - 1.0, 0.0), axis=0, keepdims=True)
    rank2 = jnp.sum(jnp.where(hit2, cum - 1.0, 0.0), axis=0, keepdims=True)
    row = lax.broadcasted_iota(jnp.int32, (ROUTE_ROWS, tm), 0)
    rank_ref[...] = jnp.where(row == 0, rank1, jnp.where(row == 1, rank2, 0.0))
    total = cum[:, tm - 1:tm]
    carry_ref[...] = total
    cnt_ref[...] = jnp.broadcast_to(total, cnt_ref.shape)


def _dispatch(route):
    n_tok = route.shape[1]
    tm = min(DISP_TM, n_tok)
    return pl.pallas_call(
        _dispatch_kernel,
        out_shape=(jax.ShapeDtypeStruct((ROUTE_ROWS, n_tok), jnp.float32),
                   jax.ShapeDtypeStruct((N_EXPERTS, LANES), jnp.float32)),
        grid=(n_tok // tm,),
        in_specs=[pl.BlockSpec((ROUTE_ROWS, tm), lambda i: (0, i))],
        out_specs=(pl.BlockSpec((ROUTE_ROWS, tm), lambda i: (0, i)),
                   pl.BlockSpec((N_EXPERTS, LANES), lambda i: (0, 0))),
        scratch_shapes=[pltpu.VMEM((tm, tm), jnp.bfloat16), pltpu.VMEM((N_EXPERTS, 1), jnp.float32)],
        compiler_params=pltpu.CompilerParams(dimension_semantics=("arbitrary",)),
        name="moe_ranks",
    )(route)


def _positions_kernel(route_ref, rank_ref, off_ref, pos_ref):
    tm = route_ref.shape[1]
    expert = lax.broadcasted_iota(jnp.int32, (N_EXPERTS, tm), 0).astype(jnp.float32)
    off = off_ref[...]
    start1 = jnp.sum(jnp.where(route_ref[0:1, :] == expert, off, 0.0), axis=0, keepdims=True)
    start2 = jnp.sum(jnp.where(route_ref[1:2, :] == expert, off, 0.0), axis=0, keepdims=True)
    row = lax.broadcasted_iota(jnp.int32, (ROUTE_ROWS, tm), 0)
    pos_ref[...] = jnp.where(row == 0, start1 + rank_ref[0:1, :], jnp.where(row == 1, start2 + rank_ref[1:2, :], 0.0))


def _positions(route, rank, seg_start):
    n_tok = route.shape[1]
    tm = min(DISP_TM, n_tok)
    blk = pl.BlockSpec((ROUTE_ROWS, tm), lambda i: (0, i))
    return pl.pallas_call(
        _positions_kernel,
        out_shape=jax.ShapeDtypeStruct((ROUTE_ROWS, n_tok), jnp.float32),
        grid=(n_tok // tm,),
        in_specs=[blk, blk, pl.BlockSpec((N_EXPERTS, 1), lambda i: (0, 0))],
        out_specs=blk,
        name="moe_positions",
    )(route, rank, seg_start)


def _sc_mesh():
    from jax.experimental.pallas import tpu_sc as plsc
    return plsc.VectorSubcoreMesh(core_axis_name="c", subcore_axis_name="s")


def _sc_worker_id():
    return lax.axis_index("s") * SC_CORES + lax.axis_index("c")


def _sc_scatter_rows(rows, pos, n_out):
    n_tok, width = rows.shape
    per_worker = n_tok // SC_WORKERS
    assert per_worker % SC_CHUNK == 0

    @functools.partial(
        pl.kernel, mesh=_sc_mesh(),
        out_type=jax.ShapeDtypeStruct((n_out, width), rows.dtype),
        scratch_types=[pltpu.VMEM((SC_CHUNK,), jnp.int32), pltpu.VMEM((SC_CHUNK,), jnp.int32),
                       pltpu.VMEM((SC_CHUNK, width), rows.dtype)],
    )
    def scatter(rows_hbm, pos_hbm, out_hbm, p1_v, p2_v, rows_v):
        base = _sc_worker_id() * per_worker

        def body(i, carry):
            off = base + i * SC_CHUNK
            pltpu.sync_copy(pos_hbm.at[pl.ds(off, SC_CHUNK)], p1_v)
            pltpu.sync_copy(pos_hbm.at[pl.ds(n_tok + off, SC_CHUNK)], p2_v)
            pltpu.sync_copy(rows_hbm.at[pl.ds(off, SC_CHUNK)], rows_v)
            pltpu.sync_copy(rows_v, out_hbm.at[p1_v])
            pltpu.sync_copy(rows_v, out_hbm.at[p2_v])
            return carry

        lax.fori_loop(0, per_worker // SC_CHUNK, body, 0)

    return scatter(rows, pos)


def _sc_gather_rows(table, idx):
    n_idx = idx.shape[0]
    width = table.shape[1]
    per_worker = n_idx // SC_WORKERS
    assert per_worker % SC_CHUNK == 0

    @functools.partial(
        pl.kernel, mesh=_sc_mesh(),
        out_type=jax.ShapeDtypeStruct((n_idx, width), table.dtype),
        scratch_types=[pltpu.VMEM((SC_CHUNK,), jnp.int32), pltpu.VMEM((SC_CHUNK, width), table.dtype),
                       pltpu.SemaphoreType.DMA],
    )
    def gather(table_hbm, idx_hbm, out_hbm, idx_v, rows_v, sem):
        base = _sc_worker_id() * per_worker

        def body(i, carry):
            off = base + i * SC_CHUNK
            pltpu.sync_copy(idx_hbm.at[pl.ds(off, SC_CHUNK)], idx_v)
            pltpu.async_copy(table_hbm.at[idx_v], rows_v, sem).wait()
            pltpu.sync_copy(rows_v, out_hbm.at[pl.ds(off, SC_CHUNK)])
            return carry

        lax.fori_loop(0, per_worker // SC_CHUNK, body, 0)

    return gather(table, idx)


def _experts_kernel(tile_expert_ref, n_used_ref, hs_ref, wg_ref, wu_ref, wd_ref, ys_ref):
    del tile_expert_ref

    @pl.when(pl.program_id(0) < n_used_ref[0])
    def _():
        lo, hi = _unpack_bf16_pair(hs_ref[...])
        lo = lo.astype(jnp.bfloat16)
        hi = hi.astype(jnp.bfloat16)
        gate = (jnp.dot(lo, wg_ref[0:PACK_W, :], preferred_element_type=jnp.float32)
                + jnp.dot(hi, wg_ref[PACK_W:D_MODEL, :], preferred_element_type=jnp.float32))
        up = (jnp.dot(lo, wu_ref[0:PACK_W, :], preferred_element_type=jnp.float32)
              + jnp.dot(hi, wu_ref[PACK_W:D_MODEL, :], preferred_element_type=jnp.float32))
        hid = ((gate * jax.nn.sigmoid(gate)) * up).astype(jnp.bfloat16)
        y = jnp.dot(hid, wd_ref[...], preferred_element_type=jnp.float32)
        ys_ref[...] = _pack_bf16_pair(y[:, 0:PACK_W], y[:, PACK_W:D_MODEL])


def _experts(hs, tile_expert, n_used, wg, wu, wd):
    n_rows = hs.shape[0]
    weights = lambda i, te, nu: (te[i], 0, 0)
    return pl.pallas_call(
        _experts_kernel,
        out_shape=jax.ShapeDtypeStruct((n_rows, PACK_W), jnp.int32),
        grid_spec=pltpu.PrefetchScalarGridSpec(
            num_scalar_prefetch=2,
            grid=(n_rows // ROW_TILE,),
            in_specs=[pl.BlockSpec((ROW_TILE, PACK_W), lambda i, te, nu: (i, 0)),
                      pl.BlockSpec((None, D_MODEL, D_EXPERT), weights),
                      pl.BlockSpec((None, D_MODEL, D_EXPERT), weights),
                      pl.BlockSpec((None, D_EXPERT, D_MODEL), weights)],
            out_specs=pl.BlockSpec((ROW_TILE, PACK_W), lambda i, te, nu: (i, 0))),
        compiler_params=pltpu.CompilerParams(dimension_semantics=("arbitrary",), vmem_limit_bytes=VMEM_LIMIT),
        name="moe_experts",
    )(tile_expert, n_used, hs, wg, wu, wd)


COMB_TM = 1024


def _combine_kernel(x1_ref, y1_ref, y2_ref, cw_ref, gf_ref, out_ref, *, final_norm):
    c1 = cw_ref[:, 0:1]
    c2 = cw_ref[:, 1:2]
    lo1, hi1 = _unpack_bf16_pair(y1_ref[...])
    lo2, hi2 = _unpack_bf16_pair(y2_ref[...])
    y = jnp.concatenate([x1_ref[:, 0:PACK_W] + (c1 * lo1 + c2 * lo2),
                         x1_ref[:, PACK_W:D_MODEL] + (c1 * hi1 + c2 * hi2)], axis=1)
    if final_norm:
        ms = jnp.mean(y * y, axis=1, keepdims=True)
        y = y * lax.rsqrt(ms + RMS_EPS) * gf_ref[...]
    out_ref[...] = y


def _combine(x1, y12, cw, gf, final_norm):
    n_tok = x1.shape[0]
    tm = min(COMB_TM, n_tok)
    n_blk = n_tok // tm
    return pl.pallas_call(
        functools.partial(_combine_kernel, final_norm=final_norm),
        out_shape=jax.ShapeDtypeStruct((n_tok, D_MODEL), jnp.float32),
        grid=(n_blk,),
        in_specs=[pl.BlockSpec((tm, D_MODEL), lambda i: (i, 0)),
                  pl.BlockSpec((tm, PACK_W), lambda i: (i, 0)),
                  pl.BlockSpec((tm, PACK_W), lambda i: (n_blk + i, 0)),
                  pl.BlockSpec((tm, ROUTER_W), lambda i: (i, 0)),
                  pl.BlockSpec((1, D_MODEL), lambda i: (0, 0))],
        out_specs=pl.BlockSpec((tm, D_MODEL), lambda i: (i, 0)),
        compiler_params=pltpu.CompilerParams(dimension_semantics=("arbitrary",), vmem_limit_bytes=VMEM_LIMIT),
        name="moe_combine",
    )(x1, y12, y12, cw, gf)


def _sparse_moe(x1, h2p, route, cw, w_gate, w_up, w_down, gf, final_norm):
    n_tok = x1.shape[0]
    n_rows = 2 * n_tok + N_EXPERTS * ROW_TILE
    rank, counts = _dispatch(route)
    padded = (counts[:, 0].astype(jnp.int32) + ROW_TILE - 1) // ROW_TILE * ROW_TILE
    ends = jnp.cumsum(padded)
    starts = ends - padded
    tile_start = jnp.arange(n_rows // ROW_TILE, dtype=jnp.int32) * ROW_TILE
    tile_expert = jnp.minimum(jnp.searchsorted(ends, tile_start, side="right"), N_EXPERTS - 1).astype(jnp.int32)
    n_used = (ends[N_EXPERTS - 1:] // ROW_TILE).astype(jnp.int32)

    pos = _positions(route, rank, starts.astype(jnp.float32).reshape(N_EXPERTS, 1))
    pos12 = jnp.concatenate([pos[0], pos[1]]).astype(jnp.int32)
    hs = _sc_scatter_rows(h2p, pos12, n_rows)
    ys = _experts(hs, tile_expert, n_used, w_gate.astype(jnp.bfloat16), w_up.astype(jnp.bfloat16),
                  w_down.astype(jnp.bfloat16))
    y12 = _sc_gather_rows(ys, pos12)
    return _combine(x1, y12, cw, gf, final_norm)


MOE_TM = 1024
MOE_ESTEP = 4
MOE_W = MOE_ESTEP * D_EXPERT
MOE_VMEM_LIMIT = 56 * 1024 * 1024


def _moe_kernel(h_ref, comb_ref, wg_ref, wu_ref, wd_ref, x1_ref, gf_ref, out_ref, acc_ref, *, final_norm):
    j = pl.program_id(1)

    @pl.when(j == 0)
    def _():
        acc_ref[...] = jnp.zeros(acc_ref.shape, jnp.float32)

    h = h_ref[...]
    gate = jnp.dot(h, wg_ref[...], preferred_element_type=jnp.float32)
    up = jnp.dot(h, wu_ref[...], preferred_element_type=jnp.float32)
    hid = (gate * jax.nn.sigmoid(gate)) * up
    comb = comb_ref[...]
    lane = _lane_iota(comb.shape)
    parts = []
    for i in range(MOE_ESTEP):
        ce = jnp.sum(jnp.where(lane == j * MOE_ESTEP + i, comb, 0.0), axis=1, keepdims=True)
        parts.append((hid[:, i * D_EXPERT:(i + 1) * D_EXPERT] * ce).astype(jnp.bfloat16))
    acc_ref[...] += jnp.dot(jnp.concatenate(parts, axis=1), wd_ref[...], preferred_element_type=jnp.float32)

    @pl.when(j == N_EXPERTS // MOE_ESTEP - 1)
    def _():
        y = x1_ref[...] + acc_ref[...]
        if final_norm:
            ms = jnp.mean(y * y, axis=1, keepdims=True)
            y = y * lax.rsqrt(ms + RMS_EPS) * gf_ref[...]
        out_ref[...] = y


def _moe(h2, comb, wg, wu, wd, x1, gf, final_norm):
    n_tok = h2.shape[0]
    tm = min(MOE_TM, n_tok)
    row = lambda i, e: (i, 0)
    return pl.pallas_call(
        functools.partial(_moe_kernel, final_norm=final_norm),
        out_shape=jax.ShapeDtypeStruct((n_tok, D_MODEL), jnp.float32),
        grid=(n_tok // tm, N_EXPERTS // MOE_ESTEP),
        in_specs=[pl.BlockSpec((tm, D_MODEL), row),
                  pl.BlockSpec((tm, ROUTER_W), row),
                  pl.BlockSpec((None, D_MODEL, MOE_W), lambda i, e: (e, 0, 0)),
                  pl.BlockSpec((None, D_MODEL, MOE_W), lambda i, e: (e, 0, 0)),
                  pl.BlockSpec((None, MOE_W, D_MODEL), lambda i, e: (e, 0, 0)),
                  pl.BlockSpec((tm, D_MODEL), row),
                  pl.BlockSpec((1, D_MODEL), lambda i, e: (0, 0))],
        out_specs=pl.BlockSpec((tm, D_MODEL), row),
        scratch_shapes=[pltpu.VMEM((tm, D_MODEL), jnp.float32)],
        compiler_params=pltpu.CompilerParams(dimension_semantics=("arbitrary", "arbitrary"),
                                             vmem_limit_bytes=MOE_VMEM_LIMIT),
        name="moe_experts",
    )(h2, comb, wg, wu, wd, x1, gf)


def _expert_weights(w_gate, w_up, w_down):
    n_steps = N_EXPERTS // MOE_ESTEP

    def side_by_side(w):
        w = w.astype(jnp.bfloat16).reshape(n_steps, MOE_ESTEP, D_MODEL, D_EXPERT)
        return w.transpose(0, 2, 1, 3).reshape(n_steps, D_MODEL, MOE_W)

    return side_by_side(w_gate), side_by_side(w_up), w_down.astype(jnp.bfloat16).reshape(n_steps, MOE_W, D_MODEL)


def _split_bf16(w):
    hi = w.astype(jnp.bfloat16)
    lo = (w - hi.astype(jnp.float32)).astype(jnp.bfloat16)
    return jnp.stack([hi, lo])


def _qkv_col_scale():
    s = np.ones((QKV_W,), np.float32)
    for group, dim in enumerate((HEAD_DIM, DIFF_DIM, HEAD_DIM, HEAD_DIM)):
        s[3 * group * GROUP_W:(3 * group + 1) * GROUP_W] = LOG2E * dim ** -0.5
    return s


def _forget_weights(w_f, b_f):
    rows = jnp.zeros((F_ROWS, D_MODEL), jnp.float32)
    bias = jnp.zeros((F_ROWS, 1), jnp.float32)
    for h in range(HEADS_PER_MIXER):
        r = (h // 2) * 8 + (h % 2)
        rows = rows.at[r].set(w_f[:, h])
        bias = bias.at[r, 0].set(b_f[h])
    return _split_bf16(rows), bias


def kernel(x, rel_bias, ln1, w_in, w_out, lam_q1, lam_k1, lam_q2, lam_k2, subln_g, b_forget,
           ln2, w_group, b_group, w_expert, b_expert, w_gate, w_up, w_down, ln_f):
    batch, seq, _ = x.shape
    depth = ln1.shape[0]
    assert seq % QBLK == 0 and seq % STEP_KEYS == 0
    tabs = _bias_tables(rel_bias)
    xt = x.reshape(batch * seq, D_MODEL)
    gf = ln_f.reshape(1, D_MODEL)
    col_scale = _qkv_col_scale()
    for l in range(depth):
        lambda_init = 0.8 - 0.6 * math.exp(-0.3 * l)
        wf, bf = _forget_weights(w_in[l][:, QKV_W:], b_forget[l])
        w_qkv = (w_in[l][:, :QKV_W] * col_scale).astype(jnp.bfloat16)
        qkv, lf = _norm_proj(xt, ln1[l].reshape(1, D_MODEL), w_qkv, wf, bf, batch, seq)
        c = _cumsum(lf)

        o_a = _dilated(qkv, tabs, batch, seq)
        lamv = jnp.zeros((8, LANES), jnp.float32)
        lamv = lamv.at[0:4, 0:DIFF_DIM].set(jnp.stack([lam_q1[l], lam_k1[l], lam_q2[l], lam_k2[l]]))
        g_sub = jnp.tile(subln_g[l], 2).reshape(1, PAIR_W)
        o_b = _diff(qkv, tabs, lamv, g_sub, lambda_init, batch, seq)
        o_c = _moba(qkv, tabs, batch, seq)
        o_d = _fox(qkv, c, batch, seq)

        w_router = jnp.zeros((D_MODEL, ROUTER_W), jnp.float32)
        w_router = w_router.at[:, :N_EXPERTS].set(w_expert[l]).at[:, N_EXPERTS:N_EXPERTS + N_GROUPS].set(w_group[l])
        b_router = jnp.zeros((1, ROUTER_W), jnp.float32)
        b_router = b_router.at[0, :N_EXPERTS].set(b_expert[l].reshape(-1)).at[0, N_EXPERTS:N_EXPERTS + N_GROUPS].set(b_group[l])
        x1, h2p, route, cw = _outproj_router([o_a, o_b, o_c, o_d], w_out[l].astype(jnp.bfloat16), xt,
                                             ln2[l].reshape(1, D_MODEL), _split_bf16(w_router), b_router)
        xt = _sparse_moe(x1, h2p, route, cw, w_gate[l], w_up[l], w_down[l], gf, final_norm=(l == depth - 1))
    return xt.reshape(batch, seq, D_MODEL)
```

```python
import functools
import math

import jax
import jax.numpy as jnp
import numpy as np
from jax import lax
from jax.experimental import pallas as pl
from jax.experimental.pallas import tpu as pltpu

D_MODEL = 1024
HEAD_DIM = 64
HEADS_PER_MIXER = 4
GROUP_W = HEADS_PER_MIXER * HEAD_DIM
QKV_W = 12 * GROUP_W
DIFF_DIM = HEAD_DIM // 2
MOBA_BLOCK = 256
MOBA_TOPK = 3
N_BUCKETS = 32
MAX_DISTANCE = 2048
N_BIAS_HEADS = 12
N_GROUPS = 4
EXPERTS_PER_GROUP = 8
N_EXPERTS = 32
D_EXPERT = 256
RMS_EPS = 1e-6
NEG = -1e30
LOG2E = math.log2(math.e)

LANES = 128
PAIR_W = 2 * HEAD_DIM
ACC_W = 2 * PAIR_W
TQ = 256
TK = 256
N_SUB = 2
QBLK = N_SUB * TQ
STEP_TILES = 2
STEP_KEYS = STEP_TILES * TK
N_DIL_TILES = MAX_DISTANCE // TK + 1
FAR_TILE = 7
MIN_DIST = -1
N_TAB_TILES = N_DIL_TILES + 1 - MIN_DIST
VMEM_LIMIT = 48 * 1024 * 1024

_NT = (((1,), (1,)), ((), ()))


def _t5_thresholds():
    d = np.arange(0, 4 * MAX_DISTANCE, dtype=np.int64)
    max_exact = N_BUCKETS // 2
    df = np.maximum(d, 1).astype(np.float32)
    large = max_exact + (np.log(df / np.float32(max_exact)) / np.float32(math.log(MAX_DISTANCE / max_exact))
                         * np.float32(N_BUCKETS - max_exact)).astype(np.int32)
    bucket = np.where(d < max_exact, d, np.minimum(large, N_BUCKETS - 1))
    return [int(np.argmax(bucket >= b)) for b in range(N_BUCKETS)]


_T5_THR = _t5_thresholds()
assert (FAR_TILE - 1) * TK + 1 >= _T5_THR[N_BUCKETS - 1]


def _lane_iota(shape):
    return lax.broadcasted_iota(jnp.int32, shape, len(shape) - 1)


def _bias_table_kernel(bias_ref, out_ref):
    h = pl.program_id(0)
    n = pl.program_id(1) + MIN_DIST
    i = lax.broadcasted_iota(jnp.int32, (TQ, TK), 0)
    j = lax.broadcasted_iota(jnp.int32, (TQ, TK), 1)
    d = n * TK + i - j
    is_dil = h < HEADS_PER_MIXER
    far = jnp.logical_and(jnp.logical_not(is_dil), n >= FAR_TILE)
    d_b = jnp.where(far, 4 * MAX_DISTANCE, d)
    val = jnp.full((TQ, TK), bias_ref[h, 0], jnp.float32)
    for b in range(1, N_BUCKETS):
        val = jnp.where(d_b >= _T5_THR[b], bias_ref[h, b], val)
    mult = ((d <= 128).astype(jnp.int32)
            + jnp.logical_and((d & 3) == 0, d <= 512).astype(jnp.int32)
            + jnp.logical_and((d & 15) == 0, d <= 2048).astype(jnp.int32))
    logm = jnp.where(mult == 3, math.log(3.0), jnp.where(mult == 2, math.log(2.0), 0.0))
    val = val + jnp.where(is_dil, logm, 0.0)
    valid = jnp.logical_and(d >= 0, jnp.logical_or(mult > 0, jnp.logical_not(is_dil)))
    out_ref[...] = jnp.where(valid, val * LOG2E, NEG)


def _bias_tables(rel_bias):
    bias_h = rel_bias.T.astype(jnp.float32)
    return pl.pallas_call(
        _bias_table_kernel,
        out_shape=jax.ShapeDtypeStruct((N_BIAS_HEADS, N_TAB_TILES, TQ, TK), jnp.float32),
        grid=(N_BIAS_HEADS, N_TAB_TILES),
        in_specs=[pl.BlockSpec(memory_space=pltpu.SMEM)],
        out_specs=pl.BlockSpec((None, None, TQ, TK), lambda h, n: (h, n, 0, 0)),
        name="bias_tables",
    )(bias_h)


PROJ_TM = 512
PROJ_CW = 512
F_ROWS = 16


def _norm_proj_kernel(x_ref, g_ref, w_ref, wf_ref, bf_ref, qkv_ref, lf_ref):
    x = x_ref[...]
    ms = jnp.mean(x * x, axis=1, keepdims=True)
    h = x * lax.rsqrt(ms + RMS_EPS) * g_ref[...]
    hb = h.astype(jnp.bfloat16)
    for c in range(QKV_W // PROJ_CW):
        cols = slice(c * PROJ_CW, (c + 1) * PROJ_CW)
        qkv_ref[:, cols] = jnp.dot(hb, w_ref[:, cols], preferred_element_type=jnp.float32).astype(jnp.bfloat16)
    h_lo = (h - hb.astype(jnp.float32)).astype(jnp.bfloat16)
    wf_hi = wf_ref[0]
    wf_lo = wf_ref[1]
    z = (lax.dot_general(wf_hi, hb, _NT, preferred_element_type=jnp.float32)
         + lax.dot_general(wf_hi, h_lo, _NT, preferred_element_type=jnp.float32)
         + lax.dot_general(wf_lo, hb, _NT, preferred_element_type=jnp.float32))
    z = z + bf_ref[...]
    lf_ref[...] = jnp.minimum(z, 0.0) - jnp.log(1.0 + jnp.exp(-jnp.abs(z)))


def _norm_proj(x2d, g, w_qkv, wf, bf, batch, seq):
    n_tok = x2d.shape[0]
    tm = min(PROJ_TM, seq)
    per_b = seq // tm
    return pl.pallas_call(
        _norm_proj_kernel,
        out_shape=(jax.ShapeDtypeStruct((n_tok, QKV_W), jnp.bfloat16),
                   jax.ShapeDtypeStruct((batch * F_ROWS, seq), jnp.float32)),
        grid=(n_tok // tm,),
        in_specs=[pl.BlockSpec((tm, D_MODEL), lambda i: (i, 0)),
                  pl.BlockSpec((1, D_MODEL), lambda i: (0, 0)),
                  pl.BlockSpec((D_MODEL, QKV_W), lambda i: (0, 0)),
                  pl.BlockSpec((2, F_ROWS, D_MODEL), lambda i: (0, 0, 0)),
                  pl.BlockSpec((F_ROWS, 1), lambda i: (0, 0))],
        out_specs=(pl.BlockSpec((tm, QKV_W), lambda i: (i, 0)),
                   pl.BlockSpec((F_ROWS, tm), lambda i: (i // per_b, i % per_b))),
        compiler_params=pltpu.CompilerParams(dimension_semantics=("arbitrary",), vmem_limit_bytes=VMEM_LIMIT),
        name="norm_proj",
    )(x2d, g, w_qkv, wf, bf)


def _split3(x):
    x1 = x.astype(jnp.bfloat16)
    r1 = x - x1.astype(jnp.float32)
    x2 = r1.astype(jnp.bfloat16)
    x3 = (r1 - x2.astype(jnp.float32)).astype(jnp.bfloat16)
    return x1, x2, x3


def _cumsum_kernel(lf_ref, c_ref):
    rows, seq = lf_ref.shape
    r = lax.broadcasted_iota(jnp.int32, (LANES, LANES), 0)
    c = lax.broadcasted_iota(jnp.int32, (LANES, LANES), 1)
    upper = jnp.where(r <= c, 1.0, 0.0).astype(jnp.bfloat16)

    def body(i, carry):
        off = pl.multiple_of(i * LANES, LANES)
        x1, x2, x3 = _split3(lf_ref[:, pl.ds(off, LANES)])
        cs = (jnp.dot(x1, upper, preferred_element_type=jnp.float32)
              + jnp.dot(x2, upper, preferred_element_type=jnp.float32)
              + jnp.dot(x3, upper, preferred_element_type=jnp.float32)) + carry
        c_ref[:, pl.ds(off, LANES)] = cs * LOG2E
        return cs[:, LANES - 1:LANES]

    lax.fori_loop(0, seq // LANES, body, jnp.zeros((rows, 1), jnp.float32))


def _cumsum(lf):
    return pl.pallas_call(
        _cumsum_kernel,
        out_shape=jax.ShapeDtypeStruct(lf.shape, jnp.float32),
        name="forget_cumsum",
    )(lf)


def _rows(n_streams, sub, st):
    rb = sub * n_streams + st
    return slice(rb * TQ, (rb + 1) * TQ)


def _stack_masked_q(q, qs_ref, n_streams):
    lane = _lane_iota((TQ, PAIR_W))
    width = PAIR_W // n_streams
    for sub in range(N_SUB):
        q_half = q[sub * TQ:(sub + 1) * TQ]
        for st in range(n_streams):
            qs_ref[_rows(n_streams, sub, st), 0:PAIR_W] = jnp.where(lane // width == st, q_half,
                                                                    jnp.zeros_like(q_half))


def _init_state(m_ref, acc_ref):
    m_ref[...] = jnp.full(m_ref.shape, NEG, jnp.float32)
    acc_ref[...] = jnp.zeros(acc_ref.shape, jnp.float32)


def _flash_step(rows, s, v_aug, m_ref, acc_ref):
    m_prev = m_ref[rows, :]
    m_new = jnp.maximum(m_prev, jnp.max(s, axis=1, keepdims=True))
    alpha = jnp.exp2(m_prev - m_new)
    p = jnp.concatenate([jnp.exp2(s[:, c * LANES:(c + 1) * LANES] - m_new) for c in range(s.shape[1] // LANES)],
                        axis=1).astype(v_aug.dtype)
    pv = jnp.dot(p, v_aug, preferred_element_type=jnp.float32)
    acc_ref[rows, :] = jnp.concatenate([alpha, alpha], axis=1) * acc_ref[rows, :] + pv
    m_ref[rows, :] = m_new


def _normalised(acc_ref, rows):
    return acc_ref[rows, 0:PAIR_W] / acc_ref[rows, PAIR_W:ACC_W]


def _table_bias(tab_ref, hh, m, sub, cap):
    return jnp.concatenate([tab_ref[hh, jnp.minimum(2 * m + sub - w, cap) - MIN_DIST] for w in range(STEP_TILES)],
                           axis=1)


def _step_rows(ref, blk, m):
    k0 = pl.multiple_of((blk - m) * STEP_KEYS, STEP_KEYS)
    return ref[pl.ds(k0, STEP_KEYS), :]


def _make_attend(n_streams, blk, qs_ref, v_ref, m_ref, acc_ref, s_ref, keys, finish):
    blocks = [(sub, st) for sub in range(N_SUB) for st in range(n_streams)]

    def raw(rows, rhs):
        return lax.dot_general(qs_ref[rows, :], rhs, _NT, preferred_element_type=jnp.float32)

    def v_aug(m):
        v = _step_rows(v_ref, blk, m)
        return jnp.concatenate([v, jnp.ones_like(v)], axis=1)

    def single(m):
        rhs = keys(m)
        va = v_aug(m)
        for sub, st in blocks:
            rows = _rows(n_streams, sub, st)
            _flash_step(rows, finish(sub, st, raw(rows, rhs), m), va, m_ref, acc_ref)

    def run(m0, count):
        @pl.when(count > 0)
        def _():
            rhs = keys(m0)
            for sub, st in blocks:
                rows = _rows(n_streams, sub, st)
                s_ref[rows, :] = raw(rows, rhs)

        def body(i, carry):
            m = m0 + i
            va = v_aug(m)
            rhs_next = keys(jnp.minimum(m + 1, blk))
            for sub, st in blocks:
                rows = _rows(n_streams, sub, st)
                cur = s_ref[rows, :]
                s_ref[rows, :] = raw(rows, rhs_next)
                _flash_step(rows, finish(sub, st, cur, m), va, m_ref, acc_ref)
            return carry

        lax.fori_loop(0, count, body, 0)

    return single, run


def _attn_call(kernel, group, qkv, extra_inputs, extra_specs, n_streams, batch, seq, *, q_width=PAIR_W,
               extra_scratch=(), name=None):
    nq = seq // QBLK
    qc, kc, vc = 6 * group, 6 * group + 2, 6 * group + 4
    rows = N_SUB * n_streams * TQ
    return pl.pallas_call(
        kernel,
        out_shape=jax.ShapeDtypeStruct((batch * seq, GROUP_W), jnp.bfloat16),
        grid=(2, batch, nq),
        in_specs=[pl.BlockSpec((QBLK, PAIR_W), lambda p, b, i: (b * nq + i, qc + p)),
                  pl.BlockSpec((seq, PAIR_W), lambda p, b, i: (b, kc + p)),
                  pl.BlockSpec((seq, PAIR_W), lambda p, b, i: (b, vc + p))] + list(extra_specs),
        out_specs=pl.BlockSpec((QBLK, PAIR_W), lambda p, b, i: (b * nq + i, p)),
        scratch_shapes=[pltpu.VMEM((rows, q_width), jnp.bfloat16),
                        pltpu.VMEM((rows, LANES), jnp.float32),
                        pltpu.VMEM((rows, ACC_W), jnp.float32),
                        pltpu.VMEM((rows, STEP_KEYS), jnp.float32)
                        ] + list(extra_scratch),
        compiler_params=pltpu.CompilerParams(dimension_semantics=("arbitrary", "arbitrary", "arbitrary"),
                                             vmem_limit_bytes=VMEM_LIMIT),
        name=name,
    )(qkv, qkv, qkv, *extra_inputs)


def _tab_spec(first_head):
    return pl.BlockSpec((2, N_TAB_TILES, TQ, TK), lambda p, b, i: (first_head // 2 + p, 0, 0, 0))


def _merge_pair(o0, o1):
    return jnp.where(_lane_iota(o0.shape) < HEAD_DIM, o0, o1)


def _store_pair_outputs(o_ref, acc_ref):
    for sub in range(N_SUB):
        o = _merge_pair(_normalised(acc_ref, _rows(2, sub, 0)), _normalised(acc_ref, _rows(2, sub, 1)))
        o_ref[sub * TQ:(sub + 1) * TQ, :] = o.astype(o_ref.dtype)


def _dilated_kernel(q_ref, k_ref, v_ref, tab_ref, o_ref, qs_ref, m_ref, acc_ref, s_ref):
    blk = pl.program_id(2)
    _stack_masked_q(q_ref[...], qs_ref, 2)
    _init_state(m_ref, acc_ref)
    _, run = _make_attend(
        2, blk, qs_ref, v_ref, m_ref, acc_ref, s_ref,
        keys=lambda m: _step_rows(k_ref, blk, m),
        finish=lambda sub, st, raw, m: raw + _table_bias(tab_ref, st, m, sub, N_DIL_TILES))
    run(0, jnp.minimum(blk, (N_DIL_TILES - 1) // STEP_TILES) + 1)
    _store_pair_outputs(o_ref, acc_ref)


def _dilated(qkv, tabs, batch, seq):
    return _attn_call(_dilated_kernel, 0, qkv, [tabs], [_tab_spec(0)], 2, batch, seq, name="dilated_attn")


def _diff_kernel(q_ref, k_ref, v_ref, tab_ref, lam_ref, g_ref, o_ref, qs_ref, m_ref, acc_ref, s_ref, *,
                 lambda_init):
    blk = pl.program_id(2)
    _stack_masked_q(q_ref[...], qs_ref, 4)
    _init_state(m_ref, acc_ref)
    _, run = _make_attend(
        4, blk, qs_ref, v_ref, m_ref, acc_ref, s_ref,
        keys=lambda m: _step_rows(k_ref, blk, m),
        finish=lambda sub, st, raw, m: raw + _table_bias(tab_ref, st // 2, m, sub, FAR_TILE))
    run(0, blk + 1)

    lamv = lam_ref[...]
    lam = (jnp.exp(jnp.sum(lamv[0:1] * lamv[1:2], axis=1, keepdims=True))
           - jnp.exp(jnp.sum(lamv[2:3] * lamv[3:4], axis=1, keepdims=True)) + lambda_init)
    first = _lane_iota((TQ, PAIR_W)) < HEAD_DIM
    for sub in range(N_SUB):
        part = [_normalised(acc_ref, _rows(4, sub, st)) for st in range(4)]
        o = _merge_pair(part[0] - lam * part[1], part[2] - lam * part[3])
        sq = o * o
        ms0 = jnp.sum(jnp.where(first, sq, 0.0), axis=1, keepdims=True)
        ms1 = jnp.sum(jnp.where(first, 0.0, sq), axis=1, keepdims=True)
        ms = jnp.where(first, ms0, ms1) * (1.0 / HEAD_DIM)
        y = (o * lax.rsqrt(ms + RMS_EPS) * g_ref[...]) * (1.0 - lambda_init)
        o_ref[sub * TQ:(sub + 1) * TQ, :] = y.astype(o_ref.dtype)


def _diff(qkv, tabs, lamv, g_sub, lambda_init, batch, seq):
    return _attn_call(functools.partial(_diff_kernel, lambda_init=lambda_init), 1, qkv, [tabs, lamv, g_sub],
                      [_tab_spec(4), pl.BlockSpec((8, LANES), lambda p, b, i: (0, 0)),
                       pl.BlockSpec((1, PAIR_W), lambda p, b, i: (0, 0))],
                      4, batch, seq, name="diff_attn")


def _moba_kernel(q_ref, k_ref, v_ref, tab_ref, o_ref, qs_ref, m_ref, acc_ref, s_ref, km_ref):
    blk = pl.program_id(2)
    n_blk = k_ref.shape[0] // MOBA_BLOCK
    assert n_blk <= LANES and MOBA_BLOCK == TQ == TK

    @pl.when(blk == 0)
    def _():
        km_ref[...] = jnp.zeros(km_ref.shape, jnp.float32)
        for n in range(n_blk):
            km_ref[n:n + 1, :] = jnp.sum(k_ref[n * MOBA_BLOCK:(n + 1) * MOBA_BLOCK, :].astype(jnp.float32),
                                         axis=0, keepdims=True) * (1.0 / MOBA_BLOCK)

    _stack_masked_q(q_ref[...], qs_ref, 2)
    _init_state(m_ref, acc_ref)
    km = km_ref[...]
    km_hi = km.astype(jnp.bfloat16)
    km_lo = (km - km_hi.astype(jnp.float32)).astype(jnp.bfloat16)
    qs = qs_ref[:, 0:PAIR_W]
    gate = (lax.dot_general(qs, km_hi, _NT, preferred_element_type=jnp.float32)
            + lax.dot_general(qs, km_lo, _NT, preferred_element_type=jnp.float32))
    cand = _lane_iota(gate.shape)
    cand_f = cand.astype(jnp.float32)
    own = N_SUB * blk + lax.broadcasted_iota(jnp.int32, gate.shape, 0) // (2 * TQ)
    g = jnp.where(cand < own, gate, NEG)
    sel = cand == own
    for _ in range(MOBA_TOPK):
        mx = jnp.max(g, axis=1, keepdims=True)
        first = jnp.min(jnp.where(g == mx, cand_f, 2.0 * LANES), axis=1, keepdims=True)
        pick = cand_f == first
        sel = jnp.logical_or(sel, jnp.logical_and(pick, mx > 0.5 * NEG))
        g = jnp.where(pick, -3e38, g)
    qs_ref[:, PAIR_W:2 * PAIR_W] = jnp.where(sel, 0.0, NEG).astype(jnp.bfloat16)

    def keys(m):
        k = _step_rows(k_ref, blk, m)
        block_of_key = N_SUB * (blk - m) + lax.broadcasted_iota(jnp.int32, (STEP_KEYS, LANES), 0) // TK
        onehot = jnp.where(_lane_iota((STEP_KEYS, LANES)) == block_of_key, 1.0, 0.0).astype(k.dtype)
        return jnp.concatenate([k, onehot], axis=1)

    _, run = _make_attend(
        2, blk, qs_ref, v_ref, m_ref, acc_ref, s_ref, keys=keys,
        finish=lambda sub, st, raw, m: raw + _table_bias(tab_ref, st, m, sub, FAR_TILE))
    run(0, blk + 1)
    _store_pair_outputs(o_ref, acc_ref)


def _moba(qkv, tabs, batch, seq):
    return _attn_call(_moba_kernel, 2, qkv, [tabs], [_tab_spec(8)], 2, batch, seq, q_width=2 * PAIR_W,
                      extra_scratch=[pltpu.VMEM((LANES, PAIR_W), jnp.float32)], name="moba_attn")


FOX_SKIP_BITS = 48.0


def _head_row_norms(x, hh):
    xf = x.astype(jnp.float32)
    sq = jnp.where(_lane_iota(xf.shape) // HEAD_DIM == hh, xf * xf, 0.0)
    return jnp.sqrt(jnp.max(jnp.sum(sq, axis=1, keepdims=True), axis=0, keepdims=True))


def _fox_kernel(q_ref, k_ref, v_ref, c_ref, o_ref, qs_ref, m_ref, acc_ref, s_ref, cq_ref, stat_ref):
    blk = pl.program_id(2)
    n_tiles = k_ref.shape[0] // TK
    assert n_tiles <= LANES

    @pl.when(blk == 0)
    def _():
        stat_ref[...] = jnp.zeros(stat_ref.shape, jnp.float32)
        for j in range(n_tiles):
            k_tile = k_ref[j * TK:(j + 1) * TK, :]
            for hh in range(2):
                stat_ref[hh:hh + 1, j:j + 1] = _head_row_norms(k_tile, hh)
                stat_ref[2 + hh:3 + hh, j:j + 1] = jnp.min(c_ref[hh:hh + 1, j * TK:(j + 1) * TK], axis=1, keepdims=True)

    q = q_ref[...]
    _stack_masked_q(q, qs_ref, 2)
    _init_state(m_ref, acc_ref)
    q0 = pl.multiple_of(blk * QBLK, QBLK)
    row = lax.broadcasted_iota(jnp.int32, (TQ, TQ), 0)
    col = lax.broadcasted_iota(jnp.int32, (TQ, TQ), 1)
    for sub in range(N_SUB):
        for hh in range(2):
            c_row = c_ref[hh:hh + 1, pl.ds(q0 + sub * TQ, TQ)]
            c_col = jnp.sum(jnp.where(row == col, c_row, 0.0), axis=1, keepdims=True)
            cq_ref[_rows(2, sub, hh), :] = jnp.broadcast_to(c_col, (TQ, LANES))

    key_pos = lax.broadcasted_iota(jnp.int32, (TQ, STEP_KEYS), 1)
    query_pos = lax.broadcasted_iota(jnp.int32, (TQ, STEP_KEYS), 0)

    def finish(sub, hh, raw, m):
        k0 = pl.multiple_of((blk - m) * STEP_KEYS, STEP_KEYS)
        cq = cq_ref[_rows(2, sub, hh), :]
        s = raw + (jnp.concatenate([cq] * (STEP_KEYS // LANES), axis=1) - c_ref[hh:hh + 1, pl.ds(k0, STEP_KEYS)])
        if isinstance(m, int) and m == 0:
            s = jnp.where(key_pos <= query_pos + sub * TQ, s, NEG)
        return s

    single, run = _make_attend(2, blk, qs_ref, v_ref, m_ref, acc_ref, s_ref,
                               keys=lambda m: _step_rows(k_ref, blk, m), finish=finish)
    single(0)

    tile = _lane_iota((1, LANES))
    needed = jnp.zeros((1, LANES), jnp.bool_)
    for hh in range(2):
        rows_h = [_rows(2, sub, hh) for sub in range(N_SUB)]
        m_min = functools.reduce(jnp.minimum, [jnp.min(m_ref[r, :], axis=0, keepdims=True) for r in rows_h])
        cq_max = functools.reduce(jnp.maximum, [jnp.max(cq_ref[r, :], axis=0, keepdims=True) for r in rows_h])
        bound = _head_row_norms(q, hh) * stat_ref[hh:hh + 1, :] + cq_max - stat_ref[2 + hh:3 + hh, :] + 1.0
        needed = jnp.logical_or(needed, bound >= m_min - FOX_SKIP_BITS)
    n_own = N_SUB * blk
    first_needed = jnp.min(jnp.where(jnp.logical_and(needed, tile < n_own), tile, n_own))
    run(1, blk - first_needed // STEP_TILES)
    _store_pair_outputs(o_ref, acc_ref)


def _fox(qkv, c, batch, seq):
    return _attn_call(_fox_kernel, 3, qkv, [c.reshape(batch, 2, 8, seq)],
                      [pl.BlockSpec((None, None, 8, seq), lambda p, b, i: (b, p, 0, 0))],
                      2, batch, seq, extra_scratch=[pltpu.VMEM((N_SUB * 2 * TQ, LANES), jnp.float32),
                                                    pltpu.VMEM((8, LANES), jnp.float32)], name="fox_attn")


OUT_TM = 512
ROUTER_W = LANES
ROUTE_ROWS = 8


PACK_W = D_MODEL // 2


def _pack_bf16_pair(lo, hi):
    lo_bits = lax.bitcast_convert_type(lo.astype(jnp.bfloat16).astype(jnp.float32), jnp.uint32)
    hi_bits = lax.bitcast_convert_type(hi.astype(jnp.bfloat16).astype(jnp.float32), jnp.uint32)
    word = lax.shift_right_logical(lo_bits, jnp.uint32(16)) | (hi_bits & jnp.uint32(0xFFFF0000))
    return lax.bitcast_convert_type(word, jnp.int32)


def _unpack_bf16_pair(word):
    bits = lax.bitcast_convert_type(word, jnp.uint32)
    lo = lax.bitcast_convert_type(lax.shift_left(bits, jnp.uint32(16)), jnp.float32)
    hi = lax.bitcast_convert_type(bits & jnp.uint32(0xFFFF0000), jnp.float32)
    return lo, hi


def _outproj_router_kernel(oa_ref, ob_ref, oc_ref, od_ref, wo_ref, x_ref, g_ref, wr_ref, br_ref,
                           x1_ref, h2_ref, route_ref, cw_ref):
    acc = x_ref[...]
    for m, o_ref in enumerate((oa_ref, ob_ref, oc_ref, od_ref)):
        acc = acc + jnp.dot(o_ref[...], wo_ref[m * GROUP_W:(m + 1) * GROUP_W, :], preferred_element_type=jnp.float32)
    x1_ref[...] = acc
    ms = jnp.mean(acc * acc, axis=1, keepdims=True)
    h = acc * lax.rsqrt(ms + RMS_EPS) * g_ref[...]
    hb = h.astype(jnp.bfloat16)
    h2_ref[...] = _pack_bf16_pair(h[:, 0:PACK_W], h[:, PACK_W:D_MODEL])
    h_lo = (h - hb.astype(jnp.float32)).astype(jnp.bfloat16)
    logits = (jnp.dot(hb, wr_ref[0], preferred_element_type=jnp.float32)
              + jnp.dot(h_lo, wr_ref[0], preferred_element_type=jnp.float32)
              + jnp.dot(hb, wr_ref[1], preferred_element_type=jnp.float32)) + br_ref[...]
    lane = _lane_iota(logits.shape).astype(jnp.float32)
    big = 2.0 * LANES
    gmask = jnp.logical_and(lane >= N_EXPERTS, lane < N_EXPERTS + N_GROUPS)
    gl = jnp.where(gmask, logits, NEG)
    gmax = jnp.max(gl, axis=1, keepdims=True)
    glane = jnp.min(jnp.where(gl == gmax, lane, big), axis=1, keepdims=True)
    gsum = jnp.sum(jnp.where(gmask, jnp.exp(gl - gmax), 0.0), axis=1, keepdims=True)
    g_w = 1.0 / gsum
    e0 = (glane - N_EXPERTS) * EXPERTS_PER_GROUP
    emask = jnp.logical_and(lane >= e0, lane < e0 + EXPERTS_PER_GROUP)
    el = jnp.where(emask, logits, NEG)
    v1 = jnp.max(el, axis=1, keepdims=True)
    i1 = jnp.min(jnp.where(el == v1, lane, big), axis=1, keepdims=True)
    el2 = jnp.where(lane == i1, NEG, el)
    v2 = jnp.max(el2, axis=1, keepdims=True)
    i2 = jnp.min(jnp.where(el2 == v2, lane, big), axis=1, keepdims=True)
    e2 = jnp.exp(v2 - v1)
    den = 1.0 + e2
    c1 = g_w / den
    c2 = g_w * e2 / den
    quad = jnp.where(lane == 0.0, i1, jnp.where(lane == 1.0, i2, jnp.where(lane == 2.0, c1,
                                                                             jnp.where(lane == 3.0, c2, 0.0))))
    cw_ref[...] = jnp.where(lane == 0.0, c1, jnp.where(lane == 1.0, c2, 0.0))
    route_ref[...] = quad.T[0:ROUTE_ROWS, :]


def _outproj_router(o_parts, w_out, x2d, g2, wr, br):
    n_tok = x2d.shape[0]
    tm = min(OUT_TM, n_tok)
    row = lambda i: (i, 0)
    fixed = lambda i: (0, 0)
    return pl.pallas_call(
        _outproj_router_kernel,
        out_shape=(jax.ShapeDtypeStruct((n_tok, D_MODEL), jnp.float32),
                   jax.ShapeDtypeStruct((n_tok, PACK_W), jnp.int32),
                   jax.ShapeDtypeStruct((ROUTE_ROWS, n_tok), jnp.float32),
                   jax.ShapeDtypeStruct((n_tok, ROUTER_W), jnp.float32)),
        grid=(n_tok // tm,),
        in_specs=[pl.BlockSpec((tm, GROUP_W), row)] * 4 + [
            pl.BlockSpec((D_MODEL, D_MODEL), fixed),
            pl.BlockSpec((tm, D_MODEL), row),
            pl.BlockSpec((1, D_MODEL), fixed),
            pl.BlockSpec((2, D_MODEL, ROUTER_W), lambda i: (0, 0, 0)),
            pl.BlockSpec((1, ROUTER_W), fixed)],
        out_specs=(pl.BlockSpec((tm, D_MODEL), row), pl.BlockSpec((tm, PACK_W), row),
                   pl.BlockSpec((ROUTE_ROWS, tm), lambda i: (0, i)), pl.BlockSpec((tm, ROUTER_W), row)),
        compiler_params=pltpu.CompilerParams(dimension_semantics=("arbitrary",), vmem_limit_bytes=VMEM_LIMIT),
        name="outproj_router",
    )(*o_parts, w_out, x2d, g2, wr, br)


ROW_TILE = 256
DISP_TM = 1024
SC_CORES = 2
SC_SUBCORES = 16
SC_WORKERS = SC_CORES * SC_SUBCORES
SC_CHUNK = 128


def _dispatch_kernel(route_ref, rank_ref, cnt_ref, tri_ref, carry_ref):
    step = pl.program_id(0)
    tm = route_ref.shape[1]

    @pl.when(step == 0)
    def _():
        r = lax.broadcasted_iota(jnp.int32, (tm, tm), 0)
        c = lax.broadcasted_iota(jnp.int32, (tm, tm), 1)
        tri_ref[...] = jnp.where(r <= c, 1.0, 0.0).astype(jnp.bfloat16)
        carry_ref[...] = jnp.zeros(carry_ref.shape, jnp.float32)

    e1 = route_ref[0:1, :]
    e2 = route_ref[1:2, :]
    expert = lax.broadcasted_iota(jnp.int32, (N_EXPERTS, tm), 0).astype(jnp.float32)
    hit1 = e1 == expert
    hit2 = e2 == expert
    hits = jnp.where(jnp.logical_or(hit1, hit2), 1.0, 0.0).astype(jnp.bfloat16)
    cum = jnp.dot(hits, tri_ref[...], preferred_element_type=jnp.float32) + carry_ref[...]
    rank1 = jnp.sum(jnp.where(hit1, cum - 1.0, 0.0), axis=0, keepdims=True)
    rank2 = jnp.sum(jnp.where(hit2, cum - 1.0, 0.0), axis=0, keepdims=True)
    row = lax.broadcasted_iota(jnp.int32, (ROUTE_ROWS, tm), 0)
    rank_ref[...] = jnp.where(row == 0, rank1, jnp.where(row == 1, rank2, 0.0))
    total = cum[:, tm - 1:tm]
    carry_ref[...] = total
    cnt_ref[...] = jnp.broadcast_to(total, cnt_ref.shape)


def _dispatch(route):
    n_tok = route.shape[1]
    tm = min(DISP_TM, n_tok)
    return pl.pallas_call(
        _dispatch_kernel,
        out_shape=(jax.ShapeDtypeStruct((ROUTE_ROWS, n_tok), jnp.float32),
                   jax.ShapeDtypeStruct((N_EXPERTS, LANES), jnp.float32)),
        grid=(n_tok // tm,),
        in_specs=[pl.BlockSpec((ROUTE_ROWS, tm), lambda i: (0, i))],
        out_specs=(pl.BlockSpec((ROUTE_ROWS, tm), lambda i: (0, i)),
                   pl.BlockSpec((N_EXPERTS, LANES), lambda i: (0, 0))),
        scratch_shapes=[pltpu.VMEM((tm, tm), jnp.bfloat16), pltpu.VMEM((N_EXPERTS, 1), jnp.float32)],
        compiler_params=pltpu.CompilerParams(dimension_semantics=("arbitrary",)),
        name="moe_ranks",
    )(route)


def _positions_kernel(route_ref, rank_ref, off_ref, pos_ref):
    tm = route_ref.shape[1]
    expert = lax.broadcasted_iota(jnp.int32, (N_EXPERTS, tm), 0).astype(jnp.float32)
    off = off_ref[...]
    start1 = jnp.sum(jnp.where(route_ref[0:1, :] == expert, off, 0.0), axis=0, keepdims=True)
    start2 = jnp.sum(jnp.where(route_ref[1:2, :] == expert, off, 0.0), axis=0, keepdims=True)
    row = lax.broadcasted_iota(jnp.int32, (ROUTE_ROWS, tm), 0)
    pos_ref[...] = jnp.where(row == 0, start1 + rank_ref[0:1, :], jnp.where(row == 1, start2 + rank_ref[1:2, :], 0.0))


def _positions(route, rank, seg_start):
    n_tok = route.shape[1]
    tm = min(DISP_TM, n_tok)
    blk = pl.BlockSpec((ROUTE_ROWS, tm), lambda i: (0, i))
    return pl.pallas_call(
        _positions_kernel,
        out_shape=jax.ShapeDtypeStruct((ROUTE_ROWS, n_tok), jnp.float32),
        grid=(n_tok // tm,),
        in_specs=[blk, blk, pl.BlockSpec((N_EXPERTS, 1), lambda i: (0, 0))],
        out_specs=blk,
        name="moe_positions",
    )(route, rank, seg_start)


def _sc_mesh():
    from jax.experimental.pallas import tpu_sc as plsc
    return plsc.VectorSubcoreMesh(core_axis_name="c", subcore_axis_name="s")


def _sc_worker_id():
    return lax.axis_index("s") * SC_CORES + lax.axis_index("c")


def _sc_scatter_rows(rows, pos, n_out):
    n_tok, width = rows.shape
    per_worker = n_tok // SC_WORKERS
    assert per_worker % SC_CHUNK == 0

    @functools.partial(
        pl.kernel, mesh=_sc_mesh(),
        out_type=jax.ShapeDtypeStruct((n_out, width), rows.dtype),
        scratch_types=[pltpu.VMEM((SC_CHUNK,), jnp.int32), pltpu.VMEM((SC_CHUNK,), jnp.int32),
                       pltpu.VMEM((SC_CHUNK, width), rows.dtype)],
    )
    def scatter(rows_hbm, pos_hbm, out_hbm, p1_v, p2_v, rows_v):
        base = _sc_worker_id() * per_worker

        def body(i, carry):
            off = base + i * SC_CHUNK
            pltpu.sync_copy(pos_hbm.at[pl.ds(off, SC_CHUNK)], p1_v)
            pltpu.sync_copy(pos_hbm.at[pl.ds(n_tok + off, SC_CHUNK)], p2_v)
            pltpu.sync_copy(rows_hbm.at[pl.ds(off, SC_CHUNK)], rows_v)
            pltpu.sync_copy(rows_v, out_hbm.at[p1_v])
            pltpu.sync_copy(rows_v, out_hbm.at[p2_v])
            return carry

        lax.fori_loop(0, per_worker // SC_CHUNK, body, 0)

    return scatter(rows, pos)


def _sc_gather_rows(table, idx):
    n_idx = idx.shape[0]
    width = table.shape[1]
    per_worker = n_idx // SC_WORKERS
    assert per_worker % SC_CHUNK == 0

    @functools.partial(
        pl.kernel, mesh=_sc_mesh(),
        out_type=jax.ShapeDtypeStruct((n_idx, width), table.dtype),
        scratch_types=[pltpu.VMEM((SC_CHUNK,), jnp.int32), pltpu.VMEM((SC_CHUNK, width), table.dtype),
                       pltpu.SemaphoreType.DMA],
    )
    def gather(table_hbm, idx_hbm, out_hbm, idx_v, rows_v, sem):
        base = _sc_worker_id() * per_worker

        def body(i, carry):
            off = base + i * SC_CHUNK
            pltpu.sync_copy(idx_hbm.at[pl.ds(off, SC_CHUNK)], idx_v)
            pltpu.async_copy(table_hbm.at[idx_v], rows_v, sem).wait()
            pltpu.sync_copy(rows_v, out_hbm.at[pl.ds(off, SC_CHUNK)])
            return carry

        lax.fori_loop(0, per_worker // SC_CHUNK, body, 0)

    return gather(table, idx)


def _experts_kernel(tile_expert_ref, n_used_ref, hs_ref, wg_ref, wu_ref, wd_ref, ys_ref):
    del tile_expert_ref

    @pl.when(pl.program_id(0) < n_used_ref[0])
    def _():
        lo, hi = _unpack_bf16_pair(hs_ref[...])
        lo = lo.astype(jnp.bfloat16)
        hi = hi.astype(jnp.bfloat16)
        bf = jnp.bfloat16
        gate = (jnp.dot(lo, wg_ref[0:PACK_W, :].astype(bf), preferred_element_type=jnp.float32)
                + jnp.dot(hi, wg_ref[PACK_W:D_MODEL, :].astype(bf), preferred_element_type=jnp.float32))
        up = (jnp.dot(lo, wu_ref[0:PACK_W, :].astype(bf), preferred_element_type=jnp.float32)
              + jnp.dot(hi, wu_ref[PACK_W:D_MODEL, :].astype(bf), preferred_element_type=jnp.float32))
        hid = ((gate * jax.nn.sigmoid(gate)) * up).astype(bf)
        y = jnp.dot(hid, wd_ref[...].astype(bf), preferred_element_type=jnp.float32)
        ys_ref[...] = _pack_bf16_pair(y[:, 0:PACK_W], y[:, PACK_W:D_MODEL])


def _experts(hs, tile_expert, n_used, wg, wu, wd):
    n_rows = hs.shape[0]
    weights = lambda i, te, nu: (te[i], 0, 0)
    return pl.pallas_call(
        _experts_kernel,
        out_shape=jax.ShapeDtypeStruct((n_rows, PACK_W), jnp.int32),
        grid_spec=pltpu.PrefetchScalarGridSpec(
            num_scalar_prefetch=2,
            grid=(n_rows // ROW_TILE,),
            in_specs=[pl.BlockSpec((ROW_TILE, PACK_W), lambda i, te, nu: (i, 0)),
                      pl.BlockSpec((None, D_MODEL, D_EXPERT), weights),
                      pl.BlockSpec((None, D_MODEL, D_EXPERT), weights),
                      pl.BlockSpec((None, D_EXPERT, D_MODEL), weights)],
            out_specs=pl.BlockSpec((ROW_TILE, PACK_W), lambda i, te, nu: (i, 0))),
        compiler_params=pltpu.CompilerParams(dimension_semantics=("arbitrary",), vmem_limit_bytes=VMEM_LIMIT),
        name="moe_experts",
    )(tile_expert, n_used, hs, wg, wu, wd)


COMB_TM = 1024


def _combine_kernel(x1_ref, y1_ref, y2_ref, cw_ref, gf_ref, out_ref, *, final_norm):
    c1 = cw_ref[:, 0:1]
    c2 = cw_ref[:, 1:2]
    lo1, hi1 = _unpack_bf16_pair(y1_ref[...])
    lo2, hi2 = _unpack_bf16_pair(y2_ref[...])
    y = jnp.concatenate([x1_ref[:, 0:PACK_W] + (c1 * lo1 + c2 * lo2),
                         x1_ref[:, PACK_W:D_MODEL] + (c1 * hi1 + c2 * hi2)], axis=1)
    if final_norm:
        ms = jnp.mean(y * y, axis=1, keepdims=True)
        y = y * lax.rsqrt(ms + RMS_EPS) * gf_ref[...]
    out_ref[...] = y


def _combine(x1, y12, cw, gf, final_norm):
    n_tok = x1.shape[0]
    tm = min(COMB_TM, n_tok)
    n_blk = n_tok // tm
    return pl.pallas_call(
        functools.partial(_combine_kernel, final_norm=final_norm),
        out_shape=jax.ShapeDtypeStruct((n_tok, D_MODEL), jnp.float32),
        grid=(n_blk,),
        in_specs=[pl.BlockSpec((tm, D_MODEL), lambda i: (i, 0)),
                  pl.BlockSpec((tm, PACK_W), lambda i: (i, 0)),
                  pl.BlockSpec((tm, PACK_W), lambda i: (n_blk + i, 0)),
                  pl.BlockSpec((tm, ROUTER_W), lambda i: (i, 0)),
                  pl.BlockSpec((1, D_MODEL), lambda i: (0, 0))],
        out_specs=pl.BlockSpec((tm, D_MODEL), lambda i: (i, 0)),
        compiler_params=pltpu.CompilerParams(dimension_semantics=("arbitrary",), vmem_limit_bytes=VMEM_LIMIT),
        name="moe_combine",
    )(x1, y12, y12, cw, gf)


def _sparse_moe(x1, h2p, route, cw, w_gate, w_up, w_down, gf, final_norm):
    n_tok = x1.shape[0]
    n_rows = 2 * n_tok + N_EXPERTS * ROW_TILE
    rank, counts = _dispatch(route)
    padded = (counts[:, 0].astype(jnp.int32) + ROW_TILE - 1) // ROW_TILE * ROW_TILE
    ends = jnp.cumsum(padded)
    starts = ends - padded
    tile_start = jnp.arange(n_rows // ROW_TILE, dtype=jnp.int32) * ROW_TILE
    tile_expert = jnp.minimum(jnp.sum(tile_start[:, None] >= ends[None, :], axis=1), N_EXPERTS - 1).astype(jnp.int32)
    n_used = (ends[N_EXPERTS - 1:] // ROW_TILE).astype(jnp.int32)

    pos = _positions(route, rank, starts.astype(jnp.float32).reshape(N_EXPERTS, 1))
    pos12 = jnp.concatenate([pos[0], pos[1]]).astype(jnp.int32)
    hs = _sc_scatter_rows(h2p, pos12, n_rows)
    ys = _experts(hs, tile_expert, n_used, w_gate, w_up, w_down)
    y12 = _sc_gather_rows(ys, pos12)
    return _combine(x1, y12, cw, gf, final_norm)


def _split_bf16(w):
    hi = w.astype(jnp.bfloat16)
    lo = (w - hi.astype(jnp.float32)).astype(jnp.bfloat16)
    return jnp.stack([hi, lo])


def _qkv_col_scale():
    s = np.ones((QKV_W,), np.float32)
    for group, dim in enumerate((HEAD_DIM, DIFF_DIM, HEAD_DIM, HEAD_DIM)):
        s[3 * group * GROUP_W:(3 * group + 1) * GROUP_W] = LOG2E * dim ** -0.5
    return s


def _forget_weights(w_f, b_f):
    rows = jnp.zeros((F_ROWS, D_MODEL), jnp.float32)
    bias = jnp.zeros((F_ROWS, 1), jnp.float32)
    for h in range(HEADS_PER_MIXER):
        r = (h // 2) * 8 + (h % 2)
        rows = rows.at[r].set(w_f[:, h])
        bias = bias.at[r, 0].set(b_f[h])
    return _split_bf16(rows), bias


def kernel(x, rel_bias, ln1, w_in, w_out, lam_q1, lam_k1, lam_q2, lam_k2, subln_g, b_forget,
           ln2, w_group, b_group, w_expert, b_expert, w_gate, w_up, w_down, ln_f):
    batch, seq, _ = x.shape
    depth = ln1.shape[0]
    assert seq % QBLK == 0 and seq % STEP_KEYS == 0
    tabs = _bias_tables(rel_bias)
    xt = x.reshape(batch * seq, D_MODEL)
    gf = ln_f.reshape(1, D_MODEL)
    col_scale = _qkv_col_scale()
    for l in range(depth):
        lambda_init = 0.8 - 0.6 * math.exp(-0.3 * l)
        wf, bf = _forget_weights(w_in[l][:, QKV_W:], b_forget[l])
        w_qkv = (w_in[l][:, :QKV_W] * col_scale).astype(jnp.bfloat16)
        qkv, lf = _norm_proj(xt, ln1[l].reshape(1, D_MODEL), w_qkv, wf, bf, batch, seq)
        c = _cumsum(lf)

        o_a = _dilated(qkv, tabs, batch, seq)
        lamv = jnp.zeros((8, LANES), jnp.float32)
        lamv = lamv.at[0:4, 0:DIFF_DIM].set(jnp.stack([lam_q1[l], lam_k1[l], lam_q2[l], lam_k2[l]]))
        g_sub = jnp.tile(subln_g[l], 2).reshape(1, PAIR_W)
        o_b = _diff(qkv, tabs, lamv, g_sub, lambda_init, batch, seq)
        o_c = _moba(qkv, tabs, batch, seq)
        o_d = _fox(qkv, c, batch, seq)

        w_router = jnp.zeros((D_MODEL, ROUTER_W), jnp.float32)
        w_router = w_router.at[:, :N_EXPERTS].set(w_expert[l]).at[:, N_EXPERTS:N_EXPERTS + N_GROUPS].set(w_group[l])
        b_router = jnp.zeros((1, ROUTER_W), jnp.float32)
        b_router = b_router.at[0, :N_EXPERTS].set(b_expert[l].reshape(-1)).at[0, N_EXPERTS:N_EXPERTS + N_GROUPS].set(b_group[l])
        x1, h2p, route, cw = _outproj_router([o_a, o_b, o_c, o_d], w_out[l].astype(jnp.bfloat16), xt,
                                             ln2[l].reshape(1, D_MODEL), _split_bf16(w_router), b_router)
        xt = _sparse_moe(x1, h2p, route, cw, w_gate[l], w_up[l], w_down[l], gf, final_norm=(l == depth - 1))
    return xt.reshape(batch, seq, D_MODEL)
```

```python
import functools
import math

import jax
import jax.numpy as jnp
import numpy as np
from jax import lax
from jax.experimental import pallas as pl
from jax.experimental.pallas import tpu as pltpu

D_MODEL = 1024
HEAD_DIM = 64
HEADS_PER_MIXER = 4
GROUP_W = HEADS_PER_MIXER * HEAD_DIM
QKV_W = 12 * GROUP_W
DIFF_DIM = HEAD_DIM // 2
MOBA_BLOCK = 256
MOBA_TOPK = 3
N_BUCKETS = 32
MAX_DISTANCE = 2048
N_BIAS_HEADS = 12
N_GROUPS = 4
EXPERTS_PER_GROUP = 8
N_EXPERTS = 32
D_EXPERT = 256
RMS_EPS = 1e-6
NEG = -1e30
LOG2E = math.log2(math.e)

LANES = 128
PAIR_W = 2 * HEAD_DIM
ACC_W = 2 * PAIR_W
TQ = 256
TK = 256
N_SUB = 2
QBLK = N_SUB * TQ
STEP_TILES = 2
STEP_KEYS = STEP_TILES * TK
N_DIL_TILES = MAX_DISTANCE // TK + 1
FAR_TILE = 7
MIN_DIST = -1
N_TAB_TILES = N_DIL_TILES + 1 - MIN_DIST
VMEM_LIMIT = 48 * 1024 * 1024

_NT = (((1,), (1,)), ((), ()))


def _t5_thresholds():
    d = np.arange(0, 4 * MAX_DISTANCE, dtype=np.int64)
    max_exact = N_BUCKETS // 2
    df = np.maximum(d, 1).astype(np.float32)
    large = max_exact + (np.log(df / np.float32(max_exact)) / np.float32(math.log(MAX_DISTANCE / max_exact))
                         * np.float32(N_BUCKETS - max_exact)).astype(np.int32)
    bucket = np.where(d < max_exact, d, np.minimum(large, N_BUCKETS - 1))
    return [int(np.argmax(bucket >= b)) for b in range(N_BUCKETS)]


_T5_THR = _t5_thresholds()
assert (FAR_TILE - 1) * TK + 1 >= _T5_THR[N_BUCKETS - 1]


def _lane_iota(shape):
    return lax.broadcasted_iota(jnp.int32, shape, len(shape) - 1)


def _bias_table_kernel(bias_ref, out_ref):
    h = pl.program_id(0)
    n = pl.program_id(1) + MIN_DIST
    i = lax.broadcasted_iota(jnp.int32, (TQ, TK), 0)
    j = lax.broadcasted_iota(jnp.int32, (TQ, TK), 1)
    d = n * TK + i - j
    is_dil = h < HEADS_PER_MIXER
    far = jnp.logical_and(jnp.logical_not(is_dil), n >= FAR_TILE)
    d_b = jnp.where(far, 4 * MAX_DISTANCE, d)
    val = jnp.full((TQ, TK), bias_ref[h, 0], jnp.float32)
    for b in range(1, N_BUCKETS):
        val = jnp.where(d_b >= _T5_THR[b], bias_ref[h, b], val)
    mult = ((d <= 128).astype(jnp.int32)
            + jnp.logical_and((d & 3) == 0, d <= 512).astype(jnp.int32)
            + jnp.logical_and((d & 15) == 0, d <= 2048).astype(jnp.int32))
    logm = jnp.where(mult == 3, math.log(3.0), jnp.where(mult == 2, math.log(2.0), 0.0))
    val = val + jnp.where(is_dil, logm, 0.0)
    valid = jnp.logical_and(d >= 0, jnp.logical_or(mult > 0, jnp.logical_not(is_dil)))
    out_ref[...] = jnp.where(valid, val * LOG2E, NEG)


def _bias_tables(rel_bias):
    bias_h = rel_bias.T.astype(jnp.float32)
    return pl.pallas_call(
        _bias_table_kernel,
        out_shape=jax.ShapeDtypeStruct((N_BIAS_HEADS, N_TAB_TILES, TQ, TK), jnp.float32),
        grid=(N_BIAS_HEADS, N_TAB_TILES),
        in_specs=[pl.BlockSpec(memory_space=pltpu.SMEM)],
        out_specs=pl.BlockSpec((None, None, TQ, TK), lambda h, n: (h, n, 0, 0)),
        name="bias_tables",
    )(bias_h)


PROJ_TM = 1024
PROJ_CW = 512
F_ROWS = 16


def _norm_proj_kernel(x_ref, g_ref, w_ref, wf_ref, bf_ref, qkv_ref, lf_ref):
    x = x_ref[...]
    ms = jnp.mean(x * x, axis=1, keepdims=True)
    h = x * lax.rsqrt(ms + RMS_EPS) * g_ref[...]
    hb = h.astype(jnp.bfloat16)
    for c in range(QKV_W // PROJ_CW):
        cols = slice(c * PROJ_CW, (c + 1) * PROJ_CW)
        qkv_ref[:, cols] = jnp.dot(hb, w_ref[:, cols], preferred_element_type=jnp.float32).astype(jnp.bfloat16)
    h_lo = (h - hb.astype(jnp.float32)).astype(jnp.bfloat16)
    wf_hi = wf_ref[0]
    wf_lo = wf_ref[1]
    z = (lax.dot_general(wf_hi, hb, _NT, preferred_element_type=jnp.float32)
         + lax.dot_general(wf_hi, h_lo, _NT, preferred_element_type=jnp.float32)
         + lax.dot_general(wf_lo, hb, _NT, preferred_element_type=jnp.float32))
    z = z + bf_ref[...]
    lf_ref[...] = jnp.minimum(z, 0.0) - jnp.log(1.0 + jnp.exp(-jnp.abs(z)))


def _norm_proj(x2d, g, w_qkv, wf, bf, batch, seq):
    n_tok = x2d.shape[0]
    tm = min(PROJ_TM, seq)
    per_b = seq // tm
    return pl.pallas_call(
        _norm_proj_kernel,
        out_shape=(jax.ShapeDtypeStruct((n_tok, QKV_W), jnp.bfloat16),
                   jax.ShapeDtypeStruct((batch * F_ROWS, seq), jnp.float32)),
        grid=(n_tok // tm,),
        in_specs=[pl.BlockSpec((tm, D_MODEL), lambda i: (i, 0)),
                  pl.BlockSpec((1, D_MODEL), lambda i: (0, 0)),
                  pl.BlockSpec((D_MODEL, QKV_W), lambda i: (0, 0)),
                  pl.BlockSpec((2, F_ROWS, D_MODEL), lambda i: (0, 0, 0)),
                  pl.BlockSpec((F_ROWS, 1), lambda i: (0, 0))],
        out_specs=(pl.BlockSpec((tm, QKV_W), lambda i: (i, 0)),
                   pl.BlockSpec((F_ROWS, tm), lambda i: (i // per_b, i % per_b))),
        compiler_params=pltpu.CompilerParams(dimension_semantics=("arbitrary",), vmem_limit_bytes=VMEM_LIMIT),
        name="norm_proj",
    )(x2d, g, w_qkv, wf, bf)


def _split3(x):
    x1 = x.astype(jnp.bfloat16)
    r1 = x - x1.astype(jnp.float32)
    x2 = r1.astype(jnp.bfloat16)
    x3 = (r1 - x2.astype(jnp.float32)).astype(jnp.bfloat16)
    return x1, x2, x3


def _cumsum_kernel(lf_ref, c_ref):
    rows, seq = lf_ref.shape
    r = lax.broadcasted_iota(jnp.int32, (LANES, LANES), 0)
    c = lax.broadcasted_iota(jnp.int32, (LANES, LANES), 1)
    upper = jnp.where(r <= c, 1.0, 0.0).astype(jnp.bfloat16)

    def body(i, carry):
        off = pl.multiple_of(i * LANES, LANES)
        x1, x2, x3 = _split3(lf_ref[:, pl.ds(off, LANES)])
        cs = (jnp.dot(x1, upper, preferred_element_type=jnp.float32)
              + jnp.dot(x2, upper, preferred_element_type=jnp.float32)
              + jnp.dot(x3, upper, preferred_element_type=jnp.float32)) + carry
        c_ref[:, pl.ds(off, LANES)] = cs * LOG2E
        return cs[:, LANES - 1:LANES]

    lax.fori_loop(0, seq // LANES, body, jnp.zeros((rows, 1), jnp.float32))


def _cumsum(lf):
    return pl.pallas_call(
        _cumsum_kernel,
        out_shape=jax.ShapeDtypeStruct(lf.shape, jnp.float32),
        name="forget_cumsum",
    )(lf)


def _rows(n_streams, sub, st):
    rb = sub * n_streams + st
    return slice(rb * TQ, (rb + 1) * TQ)


def _stack_masked_q(q, qs_ref, n_streams):
    lane = _lane_iota((TQ, PAIR_W))
    width = PAIR_W // n_streams
    for sub in range(N_SUB):
        q_half = q[sub * TQ:(sub + 1) * TQ]
        for st in range(n_streams):
            qs_ref[_rows(n_streams, sub, st), 0:PAIR_W] = jnp.where(lane // width == st, q_half,
                                                                    jnp.zeros_like(q_half))


def _init_state(m_ref, acc_ref):
    m_ref[...] = jnp.full(m_ref.shape, NEG, jnp.float32)
    acc_ref[...] = jnp.zeros(acc_ref.shape, jnp.float32)


def _flash_step(rows, s, v_aug, m_ref, acc_ref):
    m_prev = m_ref[rows, :]
    m_new = jnp.maximum(m_prev, jnp.max(s, axis=1, keepdims=True))
    alpha = jnp.exp2(m_prev - m_new)
    p = jnp.concatenate([jnp.exp2(s[:, c * LANES:(c + 1) * LANES] - m_new) for c in range(s.shape[1] // LANES)],
                        axis=1).astype(v_aug.dtype)
    pv = jnp.dot(p, v_aug, preferred_element_type=jnp.float32)
    acc_ref[rows, :] = jnp.concatenate([alpha, alpha], axis=1) * acc_ref[rows, :] + pv
    m_ref[rows, :] = m_new


def _normalised(acc_ref, rows):
    return acc_ref[rows, 0:PAIR_W] / acc_ref[rows, PAIR_W:ACC_W]


def _table_bias(tab_ref, hh, m, sub, cap):
    return jnp.concatenate([tab_ref[hh, jnp.minimum(2 * m + sub - w, cap) - MIN_DIST] for w in range(STEP_TILES)],
                           axis=1)


def _step_rows(ref, blk, m):
    k0 = pl.multiple_of((blk - m) * STEP_KEYS, STEP_KEYS)
    return ref[pl.ds(k0, STEP_KEYS), :]


def _make_attend(n_streams, blk, qs_ref, v_ref, m_ref, acc_ref, s_ref, keys, finish):
    blocks = [(sub, st) for sub in range(N_SUB) for st in range(n_streams)]

    def raw(rows, rhs):
        return lax.dot_general(qs_ref[rows, :], rhs, _NT, preferred_element_type=jnp.float32)

    def v_aug(m):
        v = _step_rows(v_ref, blk, m)
        return jnp.concatenate([v, jnp.ones_like(v)], axis=1)

    def single(m):
        rhs = keys(m)
        va = v_aug(m)
        for sub, st in blocks:
            rows = _rows(n_streams, sub, st)
            _flash_step(rows, finish(sub, st, raw(rows, rhs), m), va, m_ref, acc_ref)

    def run(m0, count):
        @pl.when(count > 0)
        def _():
            rhs = keys(m0)
            for sub, st in blocks:
                rows = _rows(n_streams, sub, st)
                s_ref[rows, :] = raw(rows, rhs)

        def body(i, carry):
            m = m0 + i
            va = v_aug(m)
            rhs_next = keys(jnp.minimum(m + 1, blk))
            for sub, st in blocks:
                rows = _rows(n_streams, sub, st)
                cur = s_ref[rows, :]
                s_ref[rows, :] = raw(rows, rhs_next)
                _flash_step(rows, finish(sub, st, cur, m), va, m_ref, acc_ref)
            return carry

        lax.fori_loop(0, count, body, 0)

    return single, run


def _attn_call(kernel, group, qkv, extra_inputs, extra_specs, n_streams, batch, seq, *, q_width=PAIR_W,
               extra_scratch=(), name=None):
    nq = seq // QBLK
    qc, kc, vc = 6 * group, 6 * group + 2, 6 * group + 4
    rows = N_SUB * n_streams * TQ
    return pl.pallas_call(
        kernel,
        out_shape=jax.ShapeDtypeStruct((batch * seq, GROUP_W), jnp.bfloat16),
        grid=(2, batch, nq),
        in_specs=[pl.BlockSpec((QBLK, PAIR_W), lambda p, b, i: (b * nq + i, qc + p)),
                  pl.BlockSpec((seq, PAIR_W), lambda p, b, i: (b, kc + p)),
                  pl.BlockSpec((seq, PAIR_W), lambda p, b, i: (b, vc + p))] + list(extra_specs),
        out_specs=pl.BlockSpec((QBLK, PAIR_W), lambda p, b, i: (b * nq + i, p)),
        scratch_shapes=[pltpu.VMEM((rows, q_width), jnp.bfloat16),
                        pltpu.VMEM((rows, LANES), jnp.float32),
                        pltpu.VMEM((rows, ACC_W), jnp.float32),
                        pltpu.VMEM((rows, STEP_KEYS), jnp.float32)
                        ] + list(extra_scratch),
        compiler_params=pltpu.CompilerParams(dimension_semantics=("arbitrary", "arbitrary", "arbitrary"),
                                             vmem_limit_bytes=VMEM_LIMIT),
        name=name,
    )(qkv, qkv, qkv, *extra_inputs)


def _tab_spec(first_head):
    return pl.BlockSpec((2, N_TAB_TILES, TQ, TK), lambda p, b, i: (first_head // 2 + p, 0, 0, 0))


def _merge_pair(o0, o1):
    return jnp.where(_lane_iota(o0.shape) < HEAD_DIM, o0, o1)


def _store_pair_outputs(o_ref, acc_ref):
    for sub in range(N_SUB):
        o = _merge_pair(_normalised(acc_ref, _rows(2, sub, 0)), _normalised(acc_ref, _rows(2, sub, 1)))
        o_ref[sub * TQ:(sub + 1) * TQ, :] = o.astype(o_ref.dtype)


def _dilated_kernel(q_ref, k_ref, v_ref, tab_ref, o_ref, qs_ref, m_ref, acc_ref, s_ref):
    blk = pl.program_id(2)
    _stack_masked_q(q_ref[...], qs_ref, 2)
    _init_state(m_ref, acc_ref)
    _, run = _make_attend(
        2, blk, qs_ref, v_ref, m_ref, acc_ref, s_ref,
        keys=lambda m: _step_rows(k_ref, blk, m),
        finish=lambda sub, st, raw, m: raw + _table_bias(tab_ref, st, m, sub, N_DIL_TILES))
    run(0, jnp.minimum(blk, (N_DIL_TILES - 1) // STEP_TILES) + 1)
    _store_pair_outputs(o_ref, acc_ref)


def _dilated(qkv, tabs, batch, seq):
    return _attn_call(_dilated_kernel, 0, qkv, [tabs], [_tab_spec(0)], 2, batch, seq, name="dilated_attn")


def _diff_kernel(q_ref, k_ref, v_ref, tab_ref, lam_ref, g_ref, o_ref, qs_ref, m_ref, acc_ref, s_ref, *,
                 lambda_init):
    blk = pl.program_id(2)
    _stack_masked_q(q_ref[...], qs_ref, 4)
    _init_state(m_ref, acc_ref)
    _, run = _make_attend(
        4, blk, qs_ref, v_ref, m_ref, acc_ref, s_ref,
        keys=lambda m: _step_rows(k_ref, blk, m),
        finish=lambda sub, st, raw, m: raw + _table_bias(tab_ref, st // 2, m, sub, FAR_TILE))
    run(0, blk + 1)

    lamv = lam_ref[...]
    lam = (jnp.exp(jnp.sum(lamv[0:1] * lamv[1:2], axis=1, keepdims=True))
           - jnp.exp(jnp.sum(lamv[2:3] * lamv[3:4], axis=1, keepdims=True)) + lambda_init)
    first = _lane_iota((TQ, PAIR_W)) < HEAD_DIM
    for sub in range(N_SUB):
        part = [_normalised(acc_ref, _rows(4, sub, st)) for st in range(4)]
        o = _merge_pair(part[0] - lam * part[1], part[2] - lam * part[3])
        sq = o * o
        ms0 = jnp.sum(jnp.where(first, sq, 0.0), axis=1, keepdims=True)
        ms1 = jnp.sum(jnp.where(first, 0.0, sq), axis=1, keepdims=True)
        ms = jnp.where(first, ms0, ms1) * (1.0 / HEAD_DIM)
        y = (o * lax.rsqrt(ms + RMS_EPS) * g_ref[...]) * (1.0 - lambda_init)
        o_ref[sub * TQ:(sub + 1) * TQ, :] = y.astype(o_ref.dtype)


def _diff(qkv, tabs, lamv, g_sub, lambda_init, batch, seq):
    return _attn_call(functools.partial(_diff_kernel, lambda_init=lambda_init), 1, qkv, [tabs, lamv, g_sub],
                      [_tab_spec(4), pl.BlockSpec((8, LANES), lambda p, b, i: (0, 0)),
                       pl.BlockSpec((1, PAIR_W), lambda p, b, i: (0, 0))],
                      4, batch, seq, name="diff_attn")


def _moba_kernel(q_ref, k_ref, v_ref, tab_ref, o_ref, qs_ref, m_ref, acc_ref, s_ref, km_ref):
    blk = pl.program_id(2)
    n_blk = k_ref.shape[0] // MOBA_BLOCK
    assert n_blk <= LANES and MOBA_BLOCK == TQ == TK

    @pl.when(blk == 0)
    def _():
        km_ref[...] = jnp.zeros(km_ref.shape, jnp.float32)
        for n in range(n_blk):
            km_ref[n:n + 1, :] = jnp.sum(k_ref[n * MOBA_BLOCK:(n + 1) * MOBA_BLOCK, :].astype(jnp.float32),
                                         axis=0, keepdims=True) * (1.0 / MOBA_BLOCK)

    _stack_masked_q(q_ref[...], qs_ref, 2)
    _init_state(m_ref, acc_ref)
    km = km_ref[...]
    km_hi = km.astype(jnp.bfloat16)
    km_lo = (km - km_hi.astype(jnp.float32)).astype(jnp.bfloat16)
    qs = qs_ref[:, 0:PAIR_W]
    gate = (lax.dot_general(qs, km_hi, _NT, preferred_element_type=jnp.float32)
            + lax.dot_general(qs, km_lo, _NT, preferred_element_type=jnp.float32))
    cand = _lane_iota(gate.shape)
    cand_f = cand.astype(jnp.float32)
    own = N_SUB * blk + lax.broadcasted_iota(jnp.int32, gate.shape, 0) // (2 * TQ)
    g = jnp.where(cand < own, gate, NEG)
    sel = cand == own
    for _ in range(MOBA_TOPK):
        mx = jnp.max(g, axis=1, keepdims=True)
        first = jnp.min(jnp.where(g == mx, cand_f, 2.0 * LANES), axis=1, keepdims=True)
        pick = cand_f == first
        sel = jnp.logical_or(sel, jnp.logical_and(pick, mx > 0.5 * NEG))
        g = jnp.where(pick, -3e38, g)
    qs_ref[:, PAIR_W:2 * PAIR_W] = jnp.where(sel, 0.0, NEG).astype(jnp.bfloat16)

    def keys(m):
        k = _step_rows(k_ref, blk, m)
        block_of_key = N_SUB * (blk - m) + lax.broadcasted_iota(jnp.int32, (STEP_KEYS, LANES), 0) // TK
        onehot = jnp.where(_lane_iota((STEP_KEYS, LANES)) == block_of_key, 1.0, 0.0).astype(k.dtype)
        return jnp.concatenate([k, onehot], axis=1)

    _, run = _make_attend(
        2, blk, qs_ref, v_ref, m_ref, acc_ref, s_ref, keys=keys,
        finish=lambda sub, st, raw, m: raw + _table_bias(tab_ref, st, m, sub, FAR_TILE))
    run(0, blk + 1)
    _store_pair_outputs(o_ref, acc_ref)


def _moba(qkv, tabs, batch, seq):
    return _attn_call(_moba_kernel, 2, qkv, [tabs], [_tab_spec(8)], 2, batch, seq, q_width=2 * PAIR_W,
                      extra_scratch=[pltpu.VMEM((LANES, PAIR_W), jnp.float32)], name="moba_attn")


FOX_SKIP_BITS = 48.0


def _head_row_norms(x, hh):
    xf = x.astype(jnp.float32)
    sq = jnp.where(_lane_iota(xf.shape) // HEAD_DIM == hh, xf * xf, 0.0)
    return jnp.sqrt(jnp.max(jnp.sum(sq, axis=1, keepdims=True), axis=0, keepdims=True))


def _fox_kernel(q_ref, k_ref, v_ref, c_ref, o_ref, qs_ref, m_ref, acc_ref, s_ref, cq_ref, stat_ref):
    blk = pl.program_id(2)
    n_tiles = k_ref.shape[0] // TK
    assert n_tiles <= LANES

    @pl.when(blk == 0)
    def _():
        stat_ref[...] = jnp.zeros(stat_ref.shape, jnp.float32)
        for j in range(n_tiles):
            k_tile = k_ref[j * TK:(j + 1) * TK, :]
            for hh in range(2):
                stat_ref[hh:hh + 1, j:j + 1] = _head_row_norms(k_tile, hh)
                stat_ref[2 + hh:3 + hh, j:j + 1] = jnp.min(c_ref[hh:hh + 1, j * TK:(j + 1) * TK], axis=1, keepdims=True)

    q = q_ref[...]
    _stack_masked_q(q, qs_ref, 2)
    _init_state(m_ref, acc_ref)
    q0 = pl.multiple_of(blk * QBLK, QBLK)
    row = lax.broadcasted_iota(jnp.int32, (TQ, TQ), 0)
    col = lax.broadcasted_iota(jnp.int32, (TQ, TQ), 1)
    for sub in range(N_SUB):
        for hh in range(2):
            c_row = c_ref[hh:hh + 1, pl.ds(q0 + sub * TQ, TQ)]
            c_col = jnp.sum(jnp.where(row == col, c_row, 0.0), axis=1, keepdims=True)
            cq_ref[_rows(2, sub, hh), :] = jnp.broadcast_to(c_col, (TQ, LANES))

    key_pos = lax.broadcasted_iota(jnp.int32, (TQ, STEP_KEYS), 1)
    query_pos = lax.broadcasted_iota(jnp.int32, (TQ, STEP_KEYS), 0)

    def finish(sub, hh, raw, m):
        k0 = pl.multiple_of((blk - m) * STEP_KEYS, STEP_KEYS)
        cq = cq_ref[_rows(2, sub, hh), :]
        s = raw + (jnp.concatenate([cq] * (STEP_KEYS // LANES), axis=1) - c_ref[hh:hh + 1, pl.ds(k0, STEP_KEYS)])
        if isinstance(m, int) and m == 0:
            s = jnp.where(key_pos <= query_pos + sub * TQ, s, NEG)
        return s

    single, run = _make_attend(2, blk, qs_ref, v_ref, m_ref, acc_ref, s_ref,
                               keys=lambda m: _step_rows(k_ref, blk, m), finish=finish)
    single(0)

    tile = _lane_iota((1, LANES))
    needed = jnp.zeros((1, LANES), jnp.bool_)
    for hh in range(2):
        rows_h = [_rows(2, sub, hh) for sub in range(N_SUB)]
        m_min = functools.reduce(jnp.minimum, [jnp.min(m_ref[r, :], axis=0, keepdims=True) for r in rows_h])
        cq_max = functools.reduce(jnp.maximum, [jnp.max(cq_ref[r, :], axis=0, keepdims=True) for r in rows_h])
        bound = _head_row_norms(q, hh) * stat_ref[hh:hh + 1, :] + cq_max - stat_ref[2 + hh:3 + hh, :] + 1.0
        needed = jnp.logical_or(needed, bound >= m_min - FOX_SKIP_BITS)
    n_own = N_SUB * blk
    first_needed = jnp.min(jnp.where(jnp.logical_and(needed, tile < n_own), tile, n_own))
    run(1, blk - first_needed // STEP_TILES)
    _store_pair_outputs(o_ref, acc_ref)


def _fox(qkv, c, batch, seq):
    return _attn_call(_fox_kernel, 3, qkv, [c.reshape(batch, 2, 8, seq)],
                      [pl.BlockSpec((None, None, 8, seq), lambda p, b, i: (b, p, 0, 0))],
                      2, batch, seq, extra_scratch=[pltpu.VMEM((N_SUB * 2 * TQ, LANES), jnp.float32),
                                                    pltpu.VMEM((8, LANES), jnp.float32)], name="fox_attn")


OUT_TM = 1024
ROUTER_W = LANES
ROUTE_ROWS = 8


PACK_W = D_MODEL // 2


def _pack_bf16_pair(lo, hi):
    lo_bits = lax.bitcast_convert_type(lo.astype(jnp.bfloat16).astype(jnp.float32), jnp.uint32)
    hi_bits = lax.bitcast_convert_type(hi.astype(jnp.bfloat16).astype(jnp.float32), jnp.uint32)
    word = lax.shift_right_logical(lo_bits, jnp.uint32(16)) | (hi_bits & jnp.uint32(0xFFFF0000))
    return lax.bitcast_convert_type(word, jnp.int32)


def _unpack_bf16_pair(word):
    bits = lax.bitcast_convert_type(word, jnp.uint32)
    lo = lax.bitcast_convert_type(lax.shift_left(bits, jnp.uint32(16)), jnp.float32)
    hi = lax.bitcast_convert_type(bits & jnp.uint32(0xFFFF0000), jnp.float32)
    return lo, hi


def _outproj_router_kernel(oa_ref, ob_ref, oc_ref, od_ref, wo_ref, x_ref, g_ref, wr_ref, br_ref,
                           x1_ref, h2_ref, route_ref, cw_ref):
    acc = x_ref[...]
    for m, o_ref in enumerate((oa_ref, ob_ref, oc_ref, od_ref)):
        acc = acc + jnp.dot(o_ref[...], wo_ref[m * GROUP_W:(m + 1) * GROUP_W, :], preferred_element_type=jnp.float32)
    x1_ref[...] = acc
    ms = jnp.mean(acc * acc, axis=1, keepdims=True)
    h = acc * lax.rsqrt(ms + RMS_EPS) * g_ref[...]
    hb = h.astype(jnp.bfloat16)
    h2_ref[...] = _pack_bf16_pair(h[:, 0:PACK_W], h[:, PACK_W:D_MODEL])
    h_lo = (h - hb.astype(jnp.float32)).astype(jnp.bfloat16)
    logits = (jnp.dot(hb, wr_ref[0], preferred_element_type=jnp.float32)
              + jnp.dot(h_lo, wr_ref[0], preferred_element_type=jnp.float32)
              + jnp.dot(hb, wr_ref[1], preferred_element_type=jnp.float32)) + br_ref[...]
    lane = _lane_iota(logits.shape).astype(jnp.float32)
    big = 2.0 * LANES
    gmask = jnp.logical_and(lane >= N_EXPERTS, lane < N_EXPERTS + N_GROUPS)
    gl = jnp.where(gmask, logits, NEG)
    gmax = jnp.max(gl, axis=1, keepdims=True)
    glane = jnp.min(jnp.where(gl == gmax, lane, big), axis=1, keepdims=True)
    gsum = jnp.sum(jnp.where(gmask, jnp.exp(gl - gmax), 0.0), axis=1, keepdims=True)
    g_w = 1.0 / gsum
    e0 = (glane - N_EXPERTS) * EXPERTS_PER_GROUP
    emask = jnp.logical_and(lane >= e0, lane < e0 + EXPERTS_PER_GROUP)
    el = jnp.where(emask, logits, NEG)
    v1 = jnp.max(el, axis=1, keepdims=True)
    i1 = jnp.min(jnp.where(el == v1, lane, big), axis=1, keepdims=True)
    el2 = jnp.where(lane == i1, NEG, el)
    v2 = jnp.max(el2, axis=1, keepdims=True)
    i2 = jnp.min(jnp.where(el2 == v2, lane, big), axis=1, keepdims=True)
    e2 = jnp.exp(v2 - v1)
    den = 1.0 + e2
    c1 = g_w / den
    c2 = g_w * e2 / den
    quad = jnp.where(lane == 0.0, i1, jnp.where(lane == 1.0, i2, jnp.where(lane == 2.0, c1,
                                                                             jnp.where(lane == 3.0, c2, 0.0))))
    cw_ref[...] = jnp.where(lane == 0.0, c1, jnp.where(lane == 1.0, c2, 0.0))
    route_ref[...] = quad.T[0:ROUTE_ROWS, :]


def _outproj_router(o_parts, w_out, x2d, g2, wr, br):
    n_tok = x2d.shape[0]
    tm = min(OUT_TM, n_tok)
    row = lambda i: (i, 0)
    fixed = lambda i: (0, 0)
    return pl.pallas_call(
        _outproj_router_kernel,
        out_shape=(jax.ShapeDtypeStruct((n_tok, D_MODEL), jnp.float32),
                   jax.ShapeDtypeStruct((n_tok, PACK_W), jnp.int32),
                   jax.ShapeDtypeStruct((ROUTE_ROWS, n_tok), jnp.float32),
                   jax.ShapeDtypeStruct((n_tok, ROUTER_W), jnp.float32)),
        grid=(n_tok // tm,),
        in_specs=[pl.BlockSpec((tm, GROUP_W), row)] * 4 + [
            pl.BlockSpec((D_MODEL, D_MODEL), fixed),
            pl.BlockSpec((tm, D_MODEL), row),
            pl.BlockSpec((1, D_MODEL), fixed),
            pl.BlockSpec((2, D_MODEL, ROUTER_W), lambda i: (0, 0, 0)),
            pl.BlockSpec((1, ROUTER_W), fixed)],
        out_specs=(pl.BlockSpec((tm, D_MODEL), row), pl.BlockSpec((tm, PACK_W), row),
                   pl.BlockSpec((ROUTE_ROWS, tm), lambda i: (0, i)), pl.BlockSpec((tm, ROUTER_W), row)),
        compiler_params=pltpu.CompilerParams(dimension_semantics=("arbitrary",), vmem_limit_bytes=VMEM_LIMIT),
        name="outproj_router",
    )(*o_parts, w_out, x2d, g2, wr, br)


ROW_TILE = 256
DISP_TM = 1024
SC_CORES = 2
SC_SUBCORES = 16
SC_WORKERS = SC_CORES * SC_SUBCORES
SC_CHUNK = 64


def _dispatch_kernel(route_ref, rank_ref, cnt_ref, tri_ref, carry_ref):
    step = pl.program_id(0)
    tm = route_ref.shape[1]

    @pl.when(step == 0)
    def _():
        r = lax.broadcasted_iota(jnp.int32, (tm, tm), 0)
        c = lax.broadcasted_iota(jnp.int32, (tm, tm), 1)
        tri_ref[...] = jnp.where(r <= c, 1.0, 0.0).astype(jnp.bfloat16)
        carry_ref[...] = jnp.zeros(carry_ref.shape, jnp.float32)

    e1 = route_ref[0:1, :]
    e2 = route_ref[1:2, :]
    expert = lax.broadcasted_iota(jnp.int32, (N_EXPERTS, tm), 0).astype(jnp.float32)
    hit1 = e1 == expert
    hit2 = e2 == expert
    hits = jnp.where(jnp.logical_or(hit1, hit2), 1.0, 0.0).astype(jnp.bfloat16)
    cum = jnp.dot(hits, tri_ref[...], preferred_element_type=jnp.float32) + carry_ref[...]
    rank1 = jnp.sum(jnp.where(hit1, cum - 1.0, 0.0), axis=0, keepdims=True)
    rank2 = jnp.sum(jnp.where(hit2, cum - 1.0, 0.0), axis=0, keepdims=True)
    row = lax.broadcasted_iota(jnp.int32, (ROUTE_ROWS, tm), 0)
    rank_ref[...] = jnp.where(row == 0, rank1, jnp.where(row == 1, rank2, 0.0))
    total = cum[:, tm - 1:tm]
    carry_ref[...] = total
    cnt_ref[...] = jnp.broadcast_to(total, cnt_ref.shape)


def _dispatch(route):
    n_tok = route.shape[1]
    tm = min(DISP_TM, n_tok)
    return pl.pallas_call(
        _dispatch_kernel,
        out_shape=(jax.ShapeDtypeStruct((ROUTE_ROWS, n_tok), jnp.float32),
                   jax.ShapeDtypeStruct((N_EXPERTS, LANES), jnp.float32)),
        grid=(n_tok // tm,),
        in_specs=[pl.BlockSpec((ROUTE_ROWS, tm), lambda i: (0, i))],
        out_specs=(pl.BlockSpec((ROUTE_ROWS, tm), lambda i: (0, i)),
                   pl.BlockSpec((N_EXPERTS, LANES), lambda i: (0, 0))),
        scratch_shapes=[pltpu.VMEM((tm, tm), jnp.bfloat16), pltpu.VMEM((N_EXPERTS, 1), jnp.float32)],
        compiler_params=pltpu.CompilerParams(dimension_semantics=("arbitrary",)),
        name="moe_ranks",
    )(route)


def _positions_kernel(route_ref, rank_ref, off_ref, pos_ref):
    tm = route_ref.shape[1]
    expert = lax.broadcasted_iota(jnp.int32, (N_EXPERTS, tm), 0).astype(jnp.float32)
    off = off_ref[...]
    start1 = jnp.sum(jnp.where(route_ref[0:1, :] == expert, off, 0.0), axis=0, keepdims=True)
    start2 = jnp.sum(jnp.where(route_ref[1:2, :] == expert, off, 0.0), axis=0, keepdims=True)
    row = lax.broadcasted_iota(jnp.int32, (ROUTE_ROWS, tm), 0)
    pos_ref[...] = jnp.where(row == 0, start1 + rank_ref[0:1, :], jnp.where(row == 1, start2 + rank_ref[1:2, :], 0.0))


def _positions(route, rank, seg_start):
    n_tok = route.shape[1]
    tm = min(DISP_TM, n_tok)
    blk = pl.BlockSpec((ROUTE_ROWS, tm), lambda i: (0, i))
    return pl.pallas_call(
        _positions_kernel,
        out_shape=jax.ShapeDtypeStruct((ROUTE_ROWS, n_tok), jnp.float32),
        grid=(n_tok // tm,),
        in_specs=[blk, blk, pl.BlockSpec((N_EXPERTS, 1), lambda i: (0, 0))],
        out_specs=blk,
        name="moe_positions",
    )(route, rank, seg_start)


def _sc_mesh():
    from jax.experimental.pallas import tpu_sc as plsc
    return plsc.VectorSubcoreMesh(core_axis_name="c", subcore_axis_name="s")


def _sc_worker_id():
    return lax.axis_index("s") * SC_CORES + lax.axis_index("c")


def _sc_scatter_rows(rows, pos, n_out):
    n_tok, width = rows.shape
    per_worker = n_tok // SC_WORKERS
    n_pairs = per_worker // (2 * SC_CHUNK)
    assert per_worker % (2 * SC_CHUNK) == 0
    buf = [pltpu.VMEM((SC_CHUNK,), jnp.int32), pltpu.VMEM((SC_CHUNK,), jnp.int32),
           pltpu.VMEM((SC_CHUNK, width), rows.dtype), pltpu.SemaphoreType.DMA, pltpu.SemaphoreType.DMA]

    @functools.partial(pl.kernel, mesh=_sc_mesh(), out_type=jax.ShapeDtypeStruct((n_out, width), rows.dtype),
                       scratch_types=buf + buf)
    def scatter(rows_hbm, pos_hbm, out_hbm, *scratch):
        base = _sc_worker_id() * per_worker
        buf_a, buf_b = scratch[:5], scratch[5:]

        def load(chunk, p1_v, p2_v, rows_v, sem_in, sem_out):
            off = base + chunk * SC_CHUNK
            pltpu.sync_copy(pos_hbm.at[pl.ds(off, SC_CHUNK)], p1_v)
            pltpu.sync_copy(pos_hbm.at[pl.ds(n_tok + off, SC_CHUNK)], p2_v)
            pltpu.async_copy(rows_hbm.at[pl.ds(off, SC_CHUNK)], rows_v, sem_in)

        def flush(chunk, p1_v, p2_v, rows_v, sem_in, sem_out):
            off = base + chunk * SC_CHUNK
            pltpu.make_async_copy(rows_hbm.at[pl.ds(off, SC_CHUNK)], rows_v, sem_in).wait()
            first = pltpu.async_copy(rows_v, out_hbm.at[p1_v], sem_out)
            second = pltpu.async_copy(rows_v, out_hbm.at[p2_v], sem_out)
            first.wait()
            second.wait()

        load(0, *buf_a)
        def body(j, carry):
            load(2 * j + 1, *buf_b)
            flush(2 * j, *buf_a)

            @pl.when(j + 1 < n_pairs)
            def _():
                load(2 * j + 2, *buf_a)

            flush(2 * j + 1, *buf_b)
            return carry

        lax.fori_loop(0, n_pairs, body, 0)

    return scatter(rows, pos)


def _sc_gather_rows(table, idx):
    n_idx = idx.shape[0]
    width = table.shape[1]
    per_worker = n_idx // SC_WORKERS
    n_pairs = per_worker // (2 * SC_CHUNK)
    assert per_worker % (2 * SC_CHUNK) == 0
    buf = [pltpu.VMEM((SC_CHUNK,), jnp.int32), pltpu.VMEM((SC_CHUNK, width), table.dtype), pltpu.SemaphoreType.DMA]

    @functools.partial(pl.kernel, mesh=_sc_mesh(), out_type=jax.ShapeDtypeStruct((n_idx, width), table.dtype),
                       scratch_types=buf + buf)
    def gather(table_hbm, idx_hbm, out_hbm, *scratch):
        base = _sc_worker_id() * per_worker
        buf_a, buf_b = scratch[:3], scratch[3:]

        def fetch(chunk, idx_v, rows_v, sem):
            pltpu.sync_copy(idx_hbm.at[pl.ds(base + chunk * SC_CHUNK, SC_CHUNK)], idx_v)
            pltpu.async_copy(table_hbm.at[idx_v], rows_v, sem)

        def drain(chunk, idx_v, rows_v, sem):
            pltpu.make_async_copy(table_hbm.at[idx_v], rows_v, sem).wait()
            pltpu.sync_copy(rows_v, out_hbm.at[pl.ds(base + chunk * SC_CHUNK, SC_CHUNK)])

        fetch(0, *buf_a)
        def body(j, carry):
            fetch(2 * j + 1, *buf_b)
            drain(2 * j, *buf_a)

            @pl.when(j + 1 < n_pairs)
            def _():
                fetch(2 * j + 2, *buf_a)

            drain(2 * j + 1, *buf_b)
            return carry

        lax.fori_loop(0, n_pairs, body, 0)

    return gather(table, idx)


def _experts_kernel(tile_expert_ref, n_used_ref, hs_ref, wg_ref, wu_ref, wd_ref, ys_ref):
    del tile_expert_ref

    @pl.when(pl.program_id(0) < n_used_ref[0])
    def _():
        lo, hi = _unpack_bf16_pair(hs_ref[...])
        lo = lo.astype(jnp.bfloat16)
        hi = hi.astype(jnp.bfloat16)
        bf = jnp.bfloat16
        gate = (jnp.dot(lo, wg_ref[0:PACK_W, :].astype(bf), preferred_element_type=jnp.float32)
                + jnp.dot(hi, wg_ref[PACK_W:D_MODEL, :].astype(bf), preferred_element_type=jnp.float32))
        up = (jnp.dot(lo, wu_ref[0:PACK_W, :].astype(bf), preferred_element_type=jnp.float32)
              + jnp.dot(hi, wu_ref[PACK_W:D_MODEL, :].astype(bf), preferred_element_type=jnp.float32))
        hid = ((gate * jax.nn.sigmoid(gate)) * up).astype(bf)
        y = jnp.dot(hid, wd_ref[...].astype(bf), preferred_element_type=jnp.float32)
        ys_ref[...] = _pack_bf16_pair(y[:, 0:PACK_W], y[:, PACK_W:D_MODEL])


def _experts(hs, tile_expert, n_used, wg, wu, wd, layer):
    n_rows = hs.shape[0]
    weights = lambda i, te, nu: (layer, te[i], 0, 0)
    return pl.pallas_call(
        _experts_kernel,
        out_shape=jax.ShapeDtypeStruct((n_rows, PACK_W), jnp.int32),
        grid_spec=pltpu.PrefetchScalarGridSpec(
            num_scalar_prefetch=2,
            grid=(n_rows // ROW_TILE,),
            in_specs=[pl.BlockSpec((ROW_TILE, PACK_W), lambda i, te, nu: (i, 0)),
                      pl.BlockSpec((None, None, D_MODEL, D_EXPERT), weights),
                      pl.BlockSpec((None, None, D_MODEL, D_EXPERT), weights),
                      pl.BlockSpec((None, None, D_EXPERT, D_MODEL), weights)],
            out_specs=pl.BlockSpec((ROW_TILE, PACK_W), lambda i, te, nu: (i, 0))),
        compiler_params=pltpu.CompilerParams(dimension_semantics=("arbitrary",), vmem_limit_bytes=VMEM_LIMIT),
        name="moe_experts",
    )(tile_expert, n_used, hs, wg, wu, wd)


COMB_TM = 1024


def _combine_kernel(x1_ref, y1_ref, y2_ref, cw_ref, gf_ref, out_ref, *, final_norm):
    c1 = cw_ref[:, 0:1]
    c2 = cw_ref[:, 1:2]
    lo1, hi1 = _unpack_bf16_pair(y1_ref[...])
    lo2, hi2 = _unpack_bf16_pair(y2_ref[...])
    y = jnp.concatenate([x1_ref[:, 0:PACK_W] + (c1 * lo1 + c2 * lo2),
                         x1_ref[:, PACK_W:D_MODEL] + (c1 * hi1 + c2 * hi2)], axis=1)
    if final_norm:
        ms = jnp.mean(y * y, axis=1, keepdims=True)
        y = y * lax.rsqrt(ms + RMS_EPS) * gf_ref[...]
    out_ref[...] = y


def _combine(x1, y12, cw, gf, final_norm):
    n_tok = x1.shape[0]
    tm = min(COMB_TM, n_tok)
    n_blk = n_tok // tm
    return pl.pallas_call(
        functools.partial(_combine_kernel, final_norm=final_norm),
        out_shape=jax.ShapeDtypeStruct((n_tok, D_MODEL), jnp.float32),
        grid=(n_blk,),
        in_specs=[pl.BlockSpec((tm, D_MODEL), lambda i: (i, 0)),
                  pl.BlockSpec((tm, PACK_W), lambda i: (i, 0)),
                  pl.BlockSpec((tm, PACK_W), lambda i: (n_blk + i, 0)),
                  pl.BlockSpec((tm, ROUTER_W), lambda i: (i, 0)),
                  pl.BlockSpec((1, D_MODEL), lambda i: (0, 0))],
        out_specs=pl.BlockSpec((tm, D_MODEL), lambda i: (i, 0)),
        compiler_params=pltpu.CompilerParams(dimension_semantics=("arbitrary",), vmem_limit_bytes=VMEM_LIMIT),
        name="moe_combine",
    )(x1, y12, y12, cw, gf)


def _sparse_moe(x1, h2p, route, cw, w_gate, w_up, w_down, layer, gf, final_norm):
    n_tok = x1.shape[0]
    n_rows = 2 * n_tok + N_EXPERTS * ROW_TILE
    rank, counts = _dispatch(route)
    padded = (counts[:, 0].astype(jnp.int32) + ROW_TILE - 1) // ROW_TILE * ROW_TILE
    ends = jnp.cumsum(padded)
    starts = ends - padded
    tile_start = jnp.arange(n_rows // ROW_TILE, dtype=jnp.int32) * ROW_TILE
    tile_expert = jnp.minimum(jnp.sum(tile_start[:, None] >= ends[None, :], axis=1), N_EXPERTS - 1).astype(jnp.int32)
    n_used = (ends[N_EXPERTS - 1:] // ROW_TILE).astype(jnp.int32)

    pos = _positions(route, rank, starts.astype(jnp.float32).reshape(N_EXPERTS, 1))
    pos12 = jnp.concatenate([pos[0], pos[1]]).astype(jnp.int32)
    hs = _sc_scatter_rows(h2p, pos12, n_rows)
    ys = _experts(hs, tile_expert, n_used, w_gate, w_up, w_down, layer)
    y12 = _sc_gather_rows(ys, pos12)
    return _combine(x1, y12, cw, gf, final_norm)


def _split_bf16(w):
    hi = w.astype(jnp.bfloat16)
    lo = (w - hi.astype(jnp.float32)).astype(jnp.bfloat16)
    return jnp.stack([hi, lo])


def _qkv_col_scale():
    s = np.ones((QKV_W,), np.float32)
    for group, dim in enumerate((HEAD_DIM, DIFF_DIM, HEAD_DIM, HEAD_DIM)):
        s[3 * group * GROUP_W:(3 * group + 1) * GROUP_W] = LOG2E * dim ** -0.5
    return s


def _forget_weights(w_f, b_f):
    rows = jnp.zeros((F_ROWS, D_MODEL), jnp.float32)
    bias = jnp.zeros((F_ROWS, 1), jnp.float32)
    for h in range(HEADS_PER_MIXER):
        r = (h // 2) * 8 + (h % 2)
        rows = rows.at[r].set(w_f[:, h])
        bias = bias.at[r, 0].set(b_f[h])
    return _split_bf16(rows), bias


def kernel(x, rel_bias, ln1, w_in, w_out, lam_q1, lam_k1, lam_q2, lam_k2, subln_g, b_forget,
           ln2, w_group, b_group, w_expert, b_expert, w_gate, w_up, w_down, ln_f):
    batch, seq, _ = x.shape
    depth = ln1.shape[0]
    assert seq % QBLK == 0 and seq % STEP_KEYS == 0
    tabs = _bias_tables(rel_bias)
    xt = x.reshape(batch * seq, D_MODEL)
    gf = ln_f.reshape(1, D_MODEL)
    col_scale = _qkv_col_scale()
    for l in range(depth):
        lambda_init = 0.8 - 0.6 * math.exp(-0.3 * l)
        wf, bf = _forget_weights(w_in[l][:, QKV_W:], b_forget[l])
        w_qkv = (w_in[l][:, :QKV_W] * col_scale).astype(jnp.bfloat16)
        qkv, lf = _norm_proj(xt, ln1[l].reshape(1, D_MODEL), w_qkv, wf, bf, batch, seq)
        c = _cumsum(lf)

        o_a = _dilated(qkv, tabs, batch, seq)
        lamv = jnp.zeros((8, LANES), jnp.float32)
        lamv = lamv.at[0:4, 0:DIFF_DIM].set(jnp.stack([lam_q1[l], lam_k1[l], lam_q2[l], lam_k2[l]]))
        g_sub = jnp.tile(subln_g[l], 2).reshape(1, PAIR_W)
        o_b = _diff(qkv, tabs, lamv, g_sub, lambda_init, batch, seq)
        o_c = _moba(qkv, tabs, batch, seq)
        o_d = _fox(qkv, c, batch, seq)

        w_router = jnp.zeros((D_MODEL, ROUTER_W), jnp.float32)
        w_router = w_router.at[:, :N_EXPERTS].set(w_expert[l]).at[:, N_EXPERTS:N_EXPERTS + N_GROUPS].set(w_group[l])
        b_router = jnp.zeros((1, ROUTER_W), jnp.float32)
        b_router = b_router.at[0, :N_EXPERTS].set(b_expert[l].reshape(-1)).at[0, N_EXPERTS:N_EXPERTS + N_GROUPS].set(b_group[l])
        x1, h2p, route, cw = _outproj_router([o_a, o_b, o_c, o_d], w_out[l].astype(jnp.bfloat16), xt,
                                             ln2[l].reshape(1, D_MODEL), _split_bf16(w_router), b_router)
        xt = _sparse_moe(x1, h2p, route, cw, w_gate, w_up, w_down, l, gf, final_norm=(l == depth - 1))
    return xt.reshape(batch, seq, D_MODEL)
```

```python
import functools
import math

import jax
import jax.numpy as jnp
import numpy as np
from jax import lax
from jax.experimental import pallas as pl
from jax.experimental.pallas import tpu as pltpu

D_MODEL = 1024
HEAD_DIM = 64
HEADS_PER_MIXER = 4
GROUP_W = HEADS_PER_MIXER * HEAD_DIM
QKV_W = 12 * GROUP_W
DIFF_DIM = HEAD_DIM // 2
MOBA_BLOCK = 256
MOBA_TOPK = 3
N_BUCKETS = 32
MAX_DISTANCE = 2048
N_BIAS_HEADS = 12
N_GROUPS = 4
EXPERTS_PER_GROUP = 8
N_EXPERTS = 32
D_EXPERT = 256
RMS_EPS = 1e-6
NEG = -1e30
LOG2E = math.log2(math.e)

LANES = 128
PAIR_W = 2 * HEAD_DIM
ACC_W = 2 * PAIR_W
TQ = 256
TK = 256
N_SUB = 2
QBLK = N_SUB * TQ
STEP_TILES = 2
STEP_KEYS = STEP_TILES * TK
N_DIL_TILES = MAX_DISTANCE // TK + 1
FAR_TILE = 7
MIN_DIST = -1
N_TAB_TILES = N_DIL_TILES + 1 - MIN_DIST
VMEM_LIMIT = 48 * 1024 * 1024

_NT = (((1,), (1,)), ((), ()))


def _t5_thresholds():
    d = np.arange(0, 4 * MAX_DISTANCE, dtype=np.int64)
    max_exact = N_BUCKETS // 2
    df = np.maximum(d, 1).astype(np.float32)
    large = max_exact + (np.log(df / np.float32(max_exact)) / np.float32(math.log(MAX_DISTANCE / max_exact))
                         * np.float32(N_BUCKETS - max_exact)).astype(np.int32)
    bucket = np.where(d < max_exact, d, np.minimum(large, N_BUCKETS - 1))
    return [int(np.argmax(bucket >= b)) for b in range(N_BUCKETS)]


_T5_THR = _t5_thresholds()
assert (FAR_TILE - 1) * TK + 1 >= _T5_THR[N_BUCKETS - 1]


def _lane_iota(shape):
    return lax.broadcasted_iota(jnp.int32, shape, len(shape) - 1)


def _bias_table_kernel(bias_ref, out_ref):
    h = pl.program_id(0)
    n = pl.program_id(1) + MIN_DIST
    i = lax.broadcasted_iota(jnp.int32, (TQ, TK), 0)
    j = lax.broadcasted_iota(jnp.int32, (TQ, TK), 1)
    d = n * TK + i - j
    is_dil = h < HEADS_PER_MIXER
    far = jnp.logical_and(jnp.logical_not(is_dil), n >= FAR_TILE)
    d_b = jnp.where(far, 4 * MAX_DISTANCE, d)
    val = jnp.full((TQ, TK), bias_ref[h, 0], jnp.float32)
    for b in range(1, N_BUCKETS):
        val = jnp.where(d_b >= _T5_THR[b], bias_ref[h, b], val)
    mult = ((d <= 128).astype(jnp.int32)
            + jnp.logical_and((d & 3) == 0, d <= 512).astype(jnp.int32)
            + jnp.logical_and((d & 15) == 0, d <= 2048).astype(jnp.int32))
    logm = jnp.where(mult == 3, math.log(3.0), jnp.where(mult == 2, math.log(2.0), 0.0))
    val = val + jnp.where(is_dil, logm, 0.0)
    valid = jnp.logical_and(d >= 0, jnp.logical_or(mult > 0, jnp.logical_not(is_dil)))
    out_ref[...] = jnp.where(valid, val * LOG2E, NEG)


def _bias_tables(rel_bias):
    bias_h = rel_bias.T.astype(jnp.float32)
    return pl.pallas_call(
        _bias_table_kernel,
        out_shape=jax.ShapeDtypeStruct((N_BIAS_HEADS, N_TAB_TILES, TQ, TK), jnp.float32),
        grid=(N_BIAS_HEADS, N_TAB_TILES),
        in_specs=[pl.BlockSpec(memory_space=pltpu.SMEM)],
        out_specs=pl.BlockSpec((None, None, TQ, TK), lambda h, n: (h, n, 0, 0)),
        name="bias_tables",
    )(bias_h)


PROJ_TM = 1024
PROJ_CW = 512
F_ROWS = 16


def _norm_proj_kernel(x_ref, g_ref, w_ref, wf_ref, bf_ref, qkv_ref, lf_ref):
    x = x_ref[...]
    ms = jnp.mean(x * x, axis=1, keepdims=True)
    h = x * lax.rsqrt(ms + RMS_EPS) * g_ref[...]
    hb = h.astype(jnp.bfloat16)
    for c in range(QKV_W // PROJ_CW):
        cols = slice(c * PROJ_CW, (c + 1) * PROJ_CW)
        qkv_ref[:, cols] = jnp.dot(hb, w_ref[:, cols], preferred_element_type=jnp.float32).astype(jnp.bfloat16)
    h_lo = (h - hb.astype(jnp.float32)).astype(jnp.bfloat16)
    wf_hi = wf_ref[0]
    wf_lo = wf_ref[1]
    z = (lax.dot_general(wf_hi, hb, _NT, preferred_element_type=jnp.float32)
         + lax.dot_general(wf_hi, h_lo, _NT, preferred_element_type=jnp.float32)
         + lax.dot_general(wf_lo, hb, _NT, preferred_element_type=jnp.float32))
    z = z + bf_ref[...]
    lf_ref[...] = jnp.minimum(z, 0.0) - jnp.log(1.0 + jnp.exp(-jnp.abs(z)))


def _norm_proj(x2d, g, w_qkv, wf, bf, batch, seq):
    n_tok = x2d.shape[0]
    tm = min(PROJ_TM, seq)
    per_b = seq // tm
    return pl.pallas_call(
        _norm_proj_kernel,
        out_shape=(jax.ShapeDtypeStruct((n_tok, QKV_W), jnp.bfloat16),
                   jax.ShapeDtypeStruct((batch * F_ROWS, seq), jnp.float32)),
        grid=(n_tok // tm,),
        in_specs=[pl.BlockSpec((tm, D_MODEL), lambda i: (i, 0)),
                  pl.BlockSpec((1, D_MODEL), lambda i: (0, 0)),
                  pl.BlockSpec((D_MODEL, QKV_W), lambda i: (0, 0)),
                  pl.BlockSpec((2, F_ROWS, D_MODEL), lambda i: (0, 0, 0)),
                  pl.BlockSpec((F_ROWS, 1), lambda i: (0, 0))],
        out_specs=(pl.BlockSpec((tm, QKV_W), lambda i: (i, 0)),
                   pl.BlockSpec((F_ROWS, tm), lambda i: (i // per_b, i % per_b))),
        compiler_params=pltpu.CompilerParams(dimension_semantics=("arbitrary",), vmem_limit_bytes=VMEM_LIMIT),
        name="norm_proj",
    )(x2d, g, w_qkv, wf, bf)


def _split3(x):
    x1 = x.astype(jnp.bfloat16)
    r1 = x - x1.astype(jnp.float32)
    x2 = r1.astype(jnp.bfloat16)
    x3 = (r1 - x2.astype(jnp.float32)).astype(jnp.bfloat16)
    return x1, x2, x3


def _cumsum_kernel(lf_ref, c_ref):
    rows, seq = lf_ref.shape
    r = lax.broadcasted_iota(jnp.int32, (LANES, LANES), 0)
    c = lax.broadcasted_iota(jnp.int32, (LANES, LANES), 1)
    upper = jnp.where(r <= c, 1.0, 0.0).astype(jnp.bfloat16)

    def body(i, carry):
        off = pl.multiple_of(i * LANES, LANES)
        x1, x2, x3 = _split3(lf_ref[:, pl.ds(off, LANES)])
        cs = (jnp.dot(x1, upper, preferred_element_type=jnp.float32)
              + jnp.dot(x2, upper, preferred_element_type=jnp.float32)
              + jnp.dot(x3, upper, preferred_element_type=jnp.float32)) + carry
        c_ref[:, pl.ds(off, LANES)] = cs * LOG2E
        return cs[:, LANES - 1:LANES]

    lax.fori_loop(0, seq // LANES, body, jnp.zeros((rows, 1), jnp.float32))


def _cumsum(lf):
    return pl.pallas_call(
        _cumsum_kernel,
        out_shape=jax.ShapeDtypeStruct(lf.shape, jnp.float32),
        name="forget_cumsum",
    )(lf)


def _rows(n_streams, sub, st):
    rb = sub * n_streams + st
    return slice(rb * TQ, (rb + 1) * TQ)


def _stack_masked_q(q, qs_ref, n_streams):
    lane = _lane_iota((TQ, PAIR_W))
    width = PAIR_W // n_streams
    for sub in range(N_SUB):
        q_half = q[sub * TQ:(sub + 1) * TQ]
        for st in range(n_streams):
            qs_ref[_rows(n_streams, sub, st), 0:PAIR_W] = jnp.where(lane // width == st, q_half,
                                                                    jnp.zeros_like(q_half))


def _init_state(m_ref, acc_ref):
    m_ref[...] = jnp.full(m_ref.shape, NEG, jnp.float32)
    acc_ref[...] = jnp.zeros(acc_ref.shape, jnp.float32)


def _flash_step(rows, s, v_aug, m_ref, acc_ref):
    m_prev = m_ref[rows, :]
    m_new = jnp.maximum(m_prev, jnp.max(s, axis=1, keepdims=True))
    alpha = jnp.exp2(m_prev - m_new)
    p = jnp.concatenate([jnp.exp2(s[:, c * LANES:(c + 1) * LANES] - m_new) for c in range(s.shape[1] // LANES)],
                        axis=1).astype(v_aug.dtype)
    pv = jnp.dot(p, v_aug, preferred_element_type=jnp.float32)
    acc_ref[rows, :] = jnp.concatenate([alpha, alpha], axis=1) * acc_ref[rows, :] + pv
    m_ref[rows, :] = m_new


def _normalised(acc_ref, rows):
    return acc_ref[rows, 0:PAIR_W] / acc_ref[rows, PAIR_W:ACC_W]


def _table_bias(tab_ref, hh, m, sub, cap):
    return jnp.concatenate([tab_ref[hh, jnp.minimum(2 * m + sub - w, cap) - MIN_DIST] for w in range(STEP_TILES)],
                           axis=1)


def _step_rows(ref, blk, m):
    k0 = pl.multiple_of((blk - m) * STEP_KEYS, STEP_KEYS)
    return ref[pl.ds(k0, STEP_KEYS), :]


def _make_attend(n_streams, blk, scratch, v_ref, keys, finish):
    qs_ref, m_ref, acc_ref, s_ref = scratch
    blocks = [(sub, st, _rows(n_streams, sub, st)) for sub in range(N_SUB) for st in range(n_streams)]

    def raw(rows, rhs):
        return lax.dot_general(qs_ref[rows, :], rhs, _NT, preferred_element_type=jnp.float32)

    def v_aug(m):
        v = _step_rows(v_ref, blk, m)
        return jnp.concatenate([v, jnp.ones_like(v)], axis=1)

    def single(m):
        rhs = keys(m)
        va = v_aug(m)
        for sub, st, rows in blocks:
            _flash_step(rows, finish(sub, st, raw(rows, rhs), m), va, m_ref, acc_ref)

    def run(m0, count):
        @pl.when(count > 0)
        def _():
            rhs = keys(m0)
            for _, _, rows in blocks:
                s_ref[rows, :] = raw(rows, rhs)

        def body(i, carry):
            m = m0 + i
            va = v_aug(m)
            rhs_next = keys(jnp.minimum(m + 1, blk))
            for sub, st, rows in blocks:
                cur = s_ref[rows, :]
                s_ref[rows, :] = raw(rows, rhs_next)
                _flash_step(rows, finish(sub, st, cur, m), va, m_ref, acc_ref)
            return carry

        lax.fori_loop(0, count, body, 0)

    return single, run


def _attn_call(kernel, group, qkv, extra_inputs, extra_specs, n_streams, batch, seq, *, q_width=PAIR_W,
               extra_scratch=(), name=None):
    nq = seq // QBLK
    qc, kc, vc = 6 * group, 6 * group + 2, 6 * group + 4
    rows = N_SUB * n_streams * TQ
    return pl.pallas_call(
        kernel,
        out_shape=jax.ShapeDtypeStruct((batch * seq, GROUP_W), jnp.bfloat16),
        grid=(2, batch, nq),
        in_specs=[pl.BlockSpec((QBLK, PAIR_W), lambda p, b, i: (b * nq + i, qc + p)),
                  pl.BlockSpec((seq, PAIR_W), lambda p, b, i: (b, kc + p)),
                  pl.BlockSpec((seq, PAIR_W), lambda p, b, i: (b, vc + p))] + list(extra_specs),
        out_specs=pl.BlockSpec((QBLK, PAIR_W), lambda p, b, i: (b * nq + i, p)),
        scratch_shapes=[pltpu.VMEM((rows, q_width), jnp.bfloat16),
                        pltpu.VMEM((rows, LANES), jnp.float32),
                        pltpu.VMEM((rows, ACC_W), jnp.float32),
                        pltpu.VMEM((rows, STEP_KEYS), jnp.float32)
                        ] + list(extra_scratch),
        compiler_params=pltpu.CompilerParams(dimension_semantics=("arbitrary", "arbitrary", "arbitrary"),
                                             vmem_limit_bytes=VMEM_LIMIT),
        name=name,
    )(qkv, qkv, qkv, *extra_inputs)


def _tab_spec(first_head):
    return pl.BlockSpec((2, N_TAB_TILES, TQ, TK), lambda p, b, i: (first_head // 2 + p, 0, 0, 0))


def _merge_pair(o0, o1):
    return jnp.where(_lane_iota(o0.shape) < HEAD_DIM, o0, o1)


def _store_pair_outputs(o_ref, acc_ref):
    for sub in range(N_SUB):
        o = _merge_pair(_normalised(acc_ref, _rows(2, sub, 0)), _normalised(acc_ref, _rows(2, sub, 1)))
        o_ref[sub * TQ:(sub + 1) * TQ, :] = o.astype(o_ref.dtype)


def _dilated_kernel(q_ref, k_ref, v_ref, tab_ref, o_ref, qs_ref, m_ref, acc_ref, s_ref):
    blk = pl.program_id(2)
    _stack_masked_q(q_ref[...], qs_ref, 2)
    _init_state(m_ref, acc_ref)
    _, run = _make_attend(
        2, blk, (qs_ref, m_ref, acc_ref, s_ref), v_ref,
        keys=lambda m: _step_rows(k_ref, blk, m),
        finish=lambda sub, st, raw, m: raw + _table_bias(tab_ref, st, m, sub, N_DIL_TILES))
    run(0, jnp.minimum(blk, (N_DIL_TILES - 1) // STEP_TILES) + 1)
    _store_pair_outputs(o_ref, acc_ref)


def _dilated(qkv, tabs, batch, seq):
    return _attn_call(_dilated_kernel, 0, qkv, [tabs], [_tab_spec(0)], 2, batch, seq, name="dilated_attn")


def _diff_kernel(q_ref, k_ref, v_ref, tab_ref, lam_ref, g_ref, o_ref, qs_ref, m_ref, acc_ref, s_ref, *,
                 lambda_init):
    blk = pl.program_id(2)
    _stack_masked_q(q_ref[...], qs_ref, 4)
    _init_state(m_ref, acc_ref)
    _, run = _make_attend(
        4, blk, (qs_ref, m_ref, acc_ref, s_ref), v_ref,
        keys=lambda m: _step_rows(k_ref, blk, m),
        finish=lambda sub, st, raw, m: raw + _table_bias(tab_ref, st // 2, m, sub, FAR_TILE))
    run(0, blk + 1)

    lamv = lam_ref[...]
    lam = (jnp.exp(jnp.sum(lamv[0:1] * lamv[1:2], axis=1, keepdims=True))
           - jnp.exp(jnp.sum(lamv[2:3] * lamv[3:4], axis=1, keepdims=True)) + lambda_init)
    first = _lane_iota((TQ, PAIR_W)) < HEAD_DIM
    for sub in range(N_SUB):
        part = [_normalised(acc_ref, _rows(4, sub, st)) for st in range(4)]
        o = _merge_pair(part[0] - lam * part[1], part[2] - lam * part[3])
        sq = o * o
        ms0 = jnp.sum(jnp.where(first, sq, 0.0), axis=1, keepdims=True)
        ms1 = jnp.sum(jnp.where(first, 0.0, sq), axis=1, keepdims=True)
        ms = jnp.where(first, ms0, ms1) * (1.0 / HEAD_DIM)
        y = (o * lax.rsqrt(ms + RMS_EPS) * g_ref[...]) * (1.0 - lambda_init)
        o_ref[sub * TQ:(sub + 1) * TQ, :] = y.astype(o_ref.dtype)


def _diff(qkv, tabs, lamv, g_sub, lambda_init, batch, seq):
    return _attn_call(functools.partial(_diff_kernel, lambda_init=lambda_init), 1, qkv, [tabs, lamv, g_sub],
                      [_tab_spec(4), pl.BlockSpec((8, LANES), lambda p, b, i: (0, 0)),
                       pl.BlockSpec((1, PAIR_W), lambda p, b, i: (0, 0))],
                      4, batch, seq, name="diff_attn")


def _moba_kernel(q_ref, k_ref, v_ref, tab_ref, o_ref, qs_ref, m_ref, acc_ref, s_ref, km_ref):
    blk = pl.program_id(2)
    n_blk = k_ref.shape[0] // MOBA_BLOCK
    assert n_blk <= LANES and MOBA_BLOCK == TQ == TK

    @pl.when(blk == 0)
    def _():
        km_ref[...] = jnp.zeros(km_ref.shape, jnp.float32)
        for n in range(n_blk):
            km_ref[n:n + 1, :] = jnp.sum(k_ref[n * MOBA_BLOCK:(n + 1) * MOBA_BLOCK, :].astype(jnp.float32),
                                         axis=0, keepdims=True) * (1.0 / MOBA_BLOCK)

    _stack_masked_q(q_ref[...], qs_ref, 2)
    _init_state(m_ref, acc_ref)
    nb = -(-n_blk // 8) * 8
    km = km_ref[0:nb, :]
    km_hi = km.astype(jnp.bfloat16)
    km_lo = (km - km_hi.astype(jnp.float32)).astype(jnp.bfloat16)
    qs = qs_ref[:, 0:PAIR_W]
    gate = (lax.dot_general(km_hi, qs, _NT, preferred_element_type=jnp.float32)
            + lax.dot_general(km_lo, qs, _NT, preferred_element_type=jnp.float32))
    cand = lax.broadcasted_iota(jnp.int32, gate.shape, 0)
    cand_f = cand.astype(jnp.float32)
    own = N_SUB * blk + _lane_iota(gate.shape) // (2 * TQ)
    g = jnp.where(cand < own, gate, NEG)
    sel = cand == own
    for _ in range(MOBA_TOPK):
        mx = jnp.max(g, axis=0, keepdims=True)
        first = jnp.min(jnp.where(g == mx, cand_f, 2.0 * LANES), axis=0, keepdims=True)
        pick = cand_f == first
        sel = jnp.logical_or(sel, jnp.logical_and(pick, mx > 0.5 * NEG))
        g = jnp.where(pick, -3e38, g)
    mask = jnp.where(sel, 0.0, NEG)
    if nb < LANES:
        mask = jnp.concatenate([mask, jnp.full((LANES - nb, mask.shape[1]), NEG, jnp.float32)], axis=0)
    qs_ref[:, PAIR_W:2 * PAIR_W] = mask.T.astype(jnp.bfloat16)

    def keys(m):
        k = _step_rows(k_ref, blk, m)
        block_of_key = N_SUB * (blk - m) + lax.broadcasted_iota(jnp.int32, (STEP_KEYS, LANES), 0) // TK
        onehot = jnp.where(_lane_iota((STEP_KEYS, LANES)) == block_of_key, 1.0, 0.0).astype(k.dtype)
        return jnp.concatenate([k, onehot], axis=1)

    _, run = _make_attend(
        2, blk, (qs_ref, m_ref, acc_ref, s_ref), v_ref, keys=keys,
        finish=lambda sub, st, raw, m: raw + _table_bias(tab_ref, st, m, sub, FAR_TILE))
    run(0, blk + 1)
    _store_pair_outputs(o_ref, acc_ref)


def _moba(qkv, tabs, batch, seq):
    return _attn_call(_moba_kernel, 2, qkv, [tabs], [_tab_spec(8)], 2, batch, seq, q_width=2 * PAIR_W,
                      extra_scratch=[pltpu.VMEM((LANES, PAIR_W), jnp.float32)], name="moba_attn")


FOX_SKIP_BITS = 48.0


def _head_row_norms(x, hh):
    xf = x.astype(jnp.float32)
    sq = jnp.where(_lane_iota(xf.shape) // HEAD_DIM == hh, xf * xf, 0.0)
    return jnp.sqrt(jnp.max(jnp.sum(sq, axis=1, keepdims=True), axis=0, keepdims=True))


def _fox_kernel(q_ref, k_ref, v_ref, c_ref, o_ref, qs_ref, m_ref, acc_ref, s_ref, cq_ref, stat_ref):
    blk = pl.program_id(2)
    n_tiles = k_ref.shape[0] // TK
    assert n_tiles <= LANES

    @pl.when(blk == 0)
    def _():
        stat_ref[...] = jnp.zeros(stat_ref.shape, jnp.float32)
        for j in range(n_tiles):
            k_tile = k_ref[j * TK:(j + 1) * TK, :]
            for hh in range(2):
                stat_ref[hh:hh + 1, j:j + 1] = _head_row_norms(k_tile, hh)
                stat_ref[2 + hh:3 + hh, j:j + 1] = jnp.min(c_ref[hh:hh + 1, j * TK:(j + 1) * TK], axis=1, keepdims=True)

    q = q_ref[...]
    _stack_masked_q(q, qs_ref, 2)
    _init_state(m_ref, acc_ref)
    q0 = pl.multiple_of(blk * QBLK, QBLK)
    row = lax.broadcasted_iota(jnp.int32, (TQ, TQ), 0)
    col = lax.broadcasted_iota(jnp.int32, (TQ, TQ), 1)
    for sub in range(N_SUB):
        for hh in range(2):
            c_row = c_ref[hh:hh + 1, pl.ds(q0 + sub * TQ, TQ)]
            c_col = jnp.sum(jnp.where(row == col, c_row, 0.0), axis=1, keepdims=True)
            cq_ref[_rows(2, sub, hh), :] = jnp.broadcast_to(c_col, (TQ, LANES))

    key_pos = lax.broadcasted_iota(jnp.int32, (TQ, STEP_KEYS), 1)
    query_pos = lax.broadcasted_iota(jnp.int32, (TQ, STEP_KEYS), 0)

    def finish(sub, hh, raw, m):
        k0 = pl.multiple_of((blk - m) * STEP_KEYS, STEP_KEYS)
        cq = cq_ref[_rows(2, sub, hh), :]
        s = raw + (jnp.concatenate([cq] * (STEP_KEYS // LANES), axis=1) - c_ref[hh:hh + 1, pl.ds(k0, STEP_KEYS)])
        if isinstance(m, int) and m == 0:
            s = jnp.where(key_pos <= query_pos + sub * TQ, s, NEG)
        return s

    single, run = _make_attend(2, blk, (qs_ref, m_ref, acc_ref, s_ref), v_ref,
                               keys=lambda m: _step_rows(k_ref, blk, m), finish=finish)
    single(0)

    tile = _lane_iota((1, LANES))
    needed = jnp.zeros((1, LANES), jnp.bool_)
    for hh in range(2):
        rows_h = [_rows(2, sub, hh) for sub in range(N_SUB)]
        m_min = functools.reduce(jnp.minimum, [jnp.min(m_ref[r, :], axis=0, keepdims=True) for r in rows_h])
        cq_max = functools.reduce(jnp.maximum, [jnp.max(cq_ref[r, :], axis=0, keepdims=True) for r in rows_h])
        bound = _head_row_norms(q, hh) * stat_ref[hh:hh + 1, :] + cq_max - stat_ref[2 + hh:3 + hh, :] + 1.0
        needed = jnp.logical_or(needed, bound >= m_min - FOX_SKIP_BITS)
    n_own = N_SUB * blk
    first_needed = jnp.min(jnp.where(jnp.logical_and(needed, tile < n_own), tile, n_own))
    run(1, blk - first_needed // STEP_TILES)
    _store_pair_outputs(o_ref, acc_ref)


def _fox(qkv, c, batch, seq):
    return _attn_call(_fox_kernel, 3, qkv, [c.reshape(batch, 2, 8, seq)],
                      [pl.BlockSpec((None, None, 8, seq), lambda p, b, i: (b, p, 0, 0))],
                      2, batch, seq, extra_scratch=[pltpu.VMEM((N_SUB * 2 * TQ, LANES), jnp.float32),
                                                    pltpu.VMEM((8, LANES), jnp.float32)], name="fox_attn")


OUT_TM = 1024
ROUTER_W = LANES
ROUTE_ROWS = 8


PACK_W = D_MODEL // 2


def _pack_bf16_pair(lo, hi):
    lo_bits = lax.bitcast_convert_type(lo.astype(jnp.bfloat16).astype(jnp.float32), jnp.uint32)
    hi_bits = lax.bitcast_convert_type(hi.astype(jnp.bfloat16).astype(jnp.float32), jnp.uint32)
    word = lax.shift_right_logical(lo_bits, jnp.uint32(16)) | (hi_bits & jnp.uint32(0xFFFF0000))
    return lax.bitcast_convert_type(word, jnp.int32)


def _unpack_bf16_pair(word):
    bits = lax.bitcast_convert_type(word, jnp.uint32)
    lo = lax.bitcast_convert_type(lax.shift_left(bits, jnp.uint32(16)), jnp.float32)
    hi = lax.bitcast_convert_type(bits & jnp.uint32(0xFFFF0000), jnp.float32)
    return lo, hi


def _outproj_router_kernel(oa_ref, ob_ref, oc_ref, od_ref, wo_ref, x_ref, g_ref, wr_ref, br_ref,
                           x1_ref, h2_ref, route_ref, cw_ref):
    acc = x_ref[...]
    for m, o_ref in enumerate((oa_ref, ob_ref, oc_ref, od_ref)):
        acc = acc + jnp.dot(o_ref[...], wo_ref[m * GROUP_W:(m + 1) * GROUP_W, :], preferred_element_type=jnp.float32)
    x1_ref[...] = acc
    ms = jnp.mean(acc * acc, axis=1, keepdims=True)
    h = acc * lax.rsqrt(ms + RMS_EPS) * g_ref[...]
    hb = h.astype(jnp.bfloat16)
    h2_ref[...] = _pack_bf16_pair(h[:, 0:PACK_W], h[:, PACK_W:D_MODEL])
    h_lo = (h - hb.astype(jnp.float32)).astype(jnp.bfloat16)
    logits = (jnp.dot(hb, wr_ref[0], preferred_element_type=jnp.float32)
              + jnp.dot(h_lo, wr_ref[0], preferred_element_type=jnp.float32)
              + jnp.dot(hb, wr_ref[1], preferred_element_type=jnp.float32)) + br_ref[...]
    lane = _lane_iota(logits.shape).astype(jnp.float32)
    big = 2.0 * LANES
    gmask = jnp.logical_and(lane >= N_EXPERTS, lane < N_EXPERTS + N_GROUPS)
    gl = jnp.where(gmask, logits, NEG)
    gmax = jnp.max(gl, axis=1, keepdims=True)
    glane = jnp.min(jnp.where(gl == gmax, lane, big), axis=1, keepdims=True)
    gsum = jnp.sum(jnp.where(gmask, jnp.exp(gl - gmax), 0.0), axis=1, keepdims=True)
    g_w = 1.0 / gsum
    e0 = (glane - N_EXPERTS) * EXPERTS_PER_GROUP
    emask = jnp.logical_and(lane >= e0, lane < e0 + EXPERTS_PER_GROUP)
    el = jnp.where(emask, logits, NEG)
    v1 = jnp.max(el, axis=1, keepdims=True)
    i1 = jnp.min(jnp.where(el == v1, lane, big), axis=1, keepdims=True)
    el2 = jnp.where(lane == i1, NEG, el)
    v2 = jnp.max(el2, axis=1, keepdims=True)
    i2 = jnp.min(jnp.where(el2 == v2, lane, big), axis=1, keepdims=True)
    e2 = jnp.exp(v2 - v1)
    den = 1.0 + e2
    c1 = g_w / den
    c2 = g_w * e2 / den
    quad = jnp.where(lane == 0.0, i1, jnp.where(lane == 1.0, i2, jnp.where(lane == 2.0, c1,
                                                                             jnp.where(lane == 3.0, c2, 0.0))))
    cw_ref[...] = jnp.where(lane == 0.0, c1, jnp.where(lane == 1.0, c2, 0.0))
    route_ref[...] = quad.T[0:ROUTE_ROWS, :]


def _outproj_router(o_parts, w_out, x2d, g2, wr, br):
    n_tok = x2d.shape[0]
    tm = min(OUT_TM, n_tok)
    row = lambda i: (i, 0)
    fixed = lambda i: (0, 0)
    return pl.pallas_call(
        _outproj_router_kernel,
        out_shape=(jax.ShapeDtypeStruct((n_tok, D_MODEL), jnp.float32),
                   jax.ShapeDtypeStruct((n_tok, PACK_W), jnp.int32),
                   jax.ShapeDtypeStruct((ROUTE_ROWS, n_tok), jnp.float32),
                   jax.ShapeDtypeStruct((n_tok, ROUTER_W), jnp.float32)),
        grid=(n_tok // tm,),
        in_specs=[pl.BlockSpec((tm, GROUP_W), row)] * 4 + [
            pl.BlockSpec((D_MODEL, D_MODEL), fixed),
            pl.BlockSpec((tm, D_MODEL), row),
            pl.BlockSpec((1, D_MODEL), fixed),
            pl.BlockSpec((2, D_MODEL, ROUTER_W), lambda i: (0, 0, 0)),
            pl.BlockSpec((1, ROUTER_W), fixed)],
        out_specs=(pl.BlockSpec((tm, D_MODEL), row), pl.BlockSpec((tm, PACK_W), row),
                   pl.BlockSpec((ROUTE_ROWS, tm), lambda i: (0, i)), pl.BlockSpec((tm, ROUTER_W), row)),
        compiler_params=pltpu.CompilerParams(dimension_semantics=("arbitrary",), vmem_limit_bytes=VMEM_LIMIT),
        name="outproj_router",
    )(*o_parts, w_out, x2d, g2, wr, br)


ROW_TILE = 512
DISP_TM = 1024
SC_CORES = 2
SC_SUBCORES = 16
SC_WORKERS = SC_CORES * SC_SUBCORES
SC_CHUNK = 64


def _dispatch_kernel(route_ref, rank_ref, cnt_ref, tri_ref, carry_ref):
    step = pl.program_id(0)
    tm = route_ref.shape[1]

    @pl.when(step == 0)
    def _():
        r = lax.broadcasted_iota(jnp.int32, (tm, tm), 0)
        c = lax.broadcasted_iota(jnp.int32, (tm, tm), 1)
        tri_ref[...] = jnp.where(r <= c, 1.0, 0.0).astype(jnp.bfloat16)
        carry_ref[...] = jnp.zeros(carry_ref.shape, jnp.float32)

    e1 = route_ref[0:1, :]
    e2 = route_ref[1:2, :]
    expert = lax.broadcasted_iota(jnp.int32, (N_EXPERTS, tm), 0).astype(jnp.float32)
    hit1 = e1 == expert
    hit2 = e2 == expert
    hits = jnp.where(jnp.logical_or(hit1, hit2), 1.0, 0.0).astype(jnp.bfloat16)
    cum = jnp.dot(hits, tri_ref[...], preferred_element_type=jnp.float32) + carry_ref[...]
    rank1 = jnp.sum(jnp.where(hit1, cum - 1.0, 0.0), axis=0, keepdims=True)
    rank2 = jnp.sum(jnp.where(hit2, cum - 1.0, 0.0), axis=0, keepdims=True)
    row = lax.broadcasted_iota(jnp.int32, (ROUTE_ROWS, tm), 0)
    rank_ref[...] = jnp.where(row == 0, rank1, jnp.where(row == 1, rank2, 0.0))
    total = cum[:, tm - 1:tm]
    carry_ref[...] = total
    cnt_ref[...] = jnp.broadcast_to(total, cnt_ref.shape)


def _dispatch(route):
    n_tok = route.shape[1]
    tm = min(DISP_TM, n_tok)
    return pl.pallas_call(
        _dispatch_kernel,
        out_shape=(jax.ShapeDtypeStruct((ROUTE_ROWS, n_tok), jnp.float32),
                   jax.ShapeDtypeStruct((N_EXPERTS, LANES), jnp.float32)),
        grid=(n_tok // tm,),
        in_specs=[pl.BlockSpec((ROUTE_ROWS, tm), lambda i: (0, i))],
        out_specs=(pl.BlockSpec((ROUTE_ROWS, tm), lambda i: (0, i)),
                   pl.BlockSpec((N_EXPERTS, LANES), lambda i: (0, 0))),
        scratch_shapes=[pltpu.VMEM((tm, tm), jnp.bfloat16), pltpu.VMEM((N_EXPERTS, 1), jnp.float32)],
        compiler_params=pltpu.CompilerParams(dimension_semantics=("arbitrary",)),
        name="moe_ranks",
    )(route)


def _positions_kernel(route_ref, rank_ref, off_ref, pos_ref):
    tm = route_ref.shape[1]
    expert = lax.broadcasted_iota(jnp.int32, (N_EXPERTS, tm), 0).astype(jnp.float32)
    off = off_ref[...]
    start1 = jnp.sum(jnp.where(route_ref[0:1, :] == expert, off, 0.0), axis=0, keepdims=True)
    start2 = jnp.sum(jnp.where(route_ref[1:2, :] == expert, off, 0.0), axis=0, keepdims=True)
    row = lax.broadcasted_iota(jnp.int32, (ROUTE_ROWS, tm), 0)
    pos_ref[...] = jnp.where(row == 0, start1 + rank_ref[0:1, :], jnp.where(row == 1, start2 + rank_ref[1:2, :], 0.0))


def _positions(route, rank, seg_start):
    n_tok = route.shape[1]
    tm = min(DISP_TM, n_tok)
    blk = pl.BlockSpec((ROUTE_ROWS, tm), lambda i: (0, i))
    return pl.pallas_call(
        _positions_kernel,
        out_shape=jax.ShapeDtypeStruct((ROUTE_ROWS, n_tok), jnp.float32),
        grid=(n_tok // tm,),
        in_specs=[blk, blk, pl.BlockSpec((N_EXPERTS, 1), lambda i: (0, 0))],
        out_specs=blk,
        name="moe_positions",
    )(route, rank, seg_start)


def _sc_mesh():
    from jax.experimental.pallas import tpu_sc as plsc
    return plsc.VectorSubcoreMesh(core_axis_name="c", subcore_axis_name="s")


def _sc_worker_id():
    return lax.axis_index("s") * SC_CORES + lax.axis_index("c")


def _sc_scatter_rows(rows, pos, n_out):
    n_tok, width = rows.shape
    per_worker = n_tok // SC_WORKERS
    n_pairs = per_worker // (2 * SC_CHUNK)
    assert per_worker % (2 * SC_CHUNK) == 0
    buf = [pltpu.VMEM((SC_CHUNK,), jnp.int32), pltpu.VMEM((SC_CHUNK,), jnp.int32),
           pltpu.VMEM((SC_CHUNK, width), rows.dtype), pltpu.SemaphoreType.DMA, pltpu.SemaphoreType.DMA]

    @functools.partial(pl.kernel, mesh=_sc_mesh(), out_type=jax.ShapeDtypeStruct((n_out, width), rows.dtype),
                       scratch_types=buf + buf)
    def scatter(rows_hbm, pos_hbm, out_hbm, *scratch):
        base = _sc_worker_id() * per_worker
        buf_a, buf_b = scratch[:5], scratch[5:]

        def load(chunk, p1_v, p2_v, rows_v, sem_in, sem_out):
            off = base + chunk * SC_CHUNK
            pltpu.sync_copy(pos_hbm.at[pl.ds(off, SC_CHUNK)], p1_v)
            pltpu.sync_copy(pos_hbm.at[pl.ds(n_tok + off, SC_CHUNK)], p2_v)
            pltpu.async_copy(rows_hbm.at[pl.ds(off, SC_CHUNK)], rows_v, sem_in)

        def flush(chunk, p1_v, p2_v, rows_v, sem_in, sem_out):
            off = base + chunk * SC_CHUNK
            pltpu.make_async_copy(rows_hbm.at[pl.ds(off, SC_CHUNK)], rows_v, sem_in).wait()
            first = pltpu.async_copy(rows_v, out_hbm.at[p1_v], sem_out)
            second = pltpu.async_copy(rows_v, out_hbm.at[p2_v], sem_out)
            first.wait()
            second.wait()

        load(0, *buf_a)
        def body(j, carry):
            load(2 * j + 1, *buf_b)
            flush(2 * j, *buf_a)

            @pl.when(j + 1 < n_pairs)
            def _():
                load(2 * j + 2, *buf_a)

            flush(2 * j + 1, *buf_b)
            return carry

        lax.fori_loop(0, n_pairs, body, 0)

    return scatter(rows, pos)


def _sc_gather_rows(table, idx):
    n_idx = idx.shape[0]
    width = table.shape[1]
    per_worker = n_idx // SC_WORKERS
    n_pairs = per_worker // (2 * SC_CHUNK)
    assert per_worker % (2 * SC_CHUNK) == 0
    buf = [pltpu.VMEM((SC_CHUNK,), jnp.int32), pltpu.VMEM((SC_CHUNK, width), table.dtype), pltpu.SemaphoreType.DMA]

    @functools.partial(pl.kernel, mesh=_sc_mesh(), out_type=jax.ShapeDtypeStruct((n_idx, width), table.dtype),
                       scratch_types=buf + buf)
    def gather(table_hbm, idx_hbm, out_hbm, *scratch):
        base = _sc_worker_id() * per_worker
        buf_a, buf_b = scratch[:3], scratch[3:]

        def fetch(chunk, idx_v, rows_v, sem):
            pltpu.sync_copy(idx_hbm.at[pl.ds(base + chunk * SC_CHUNK, SC_CHUNK)], idx_v)
            pltpu.async_copy(table_hbm.at[idx_v], rows_v, sem)

        def drain(chunk, idx_v, rows_v, sem):
            pltpu.make_async_copy(table_hbm.at[idx_v], rows_v, sem).wait()
            pltpu.sync_copy(rows_v, out_hbm.at[pl.ds(base + chunk * SC_CHUNK, SC_CHUNK)])

        fetch(0, *buf_a)
        def body(j, carry):
            fetch(2 * j + 1, *buf_b)
            drain(2 * j, *buf_a)

            @pl.when(j + 1 < n_pairs)
            def _():
                fetch(2 * j + 2, *buf_a)

            drain(2 * j + 1, *buf_b)
            return carry

        lax.fori_loop(0, n_pairs, body, 0)

    return gather(table, idx)


def _experts_kernel(tile_expert_ref, n_used_ref, hs_ref, wg_ref, wu_ref, wd_ref, ys_ref):
    del tile_expert_ref

    @pl.when(pl.program_id(0) < n_used_ref[0])
    def _():
        lo, hi = _unpack_bf16_pair(hs_ref[...])
        lo = lo.astype(jnp.bfloat16)
        hi = hi.astype(jnp.bfloat16)
        bf = jnp.bfloat16
        gate = (jnp.dot(lo, wg_ref[0:PACK_W, :].astype(bf), preferred_element_type=jnp.float32)
                + jnp.dot(hi, wg_ref[PACK_W:D_MODEL, :].astype(bf), preferred_element_type=jnp.float32))
        up = (jnp.dot(lo, wu_ref[0:PACK_W, :].astype(bf), preferred_element_type=jnp.float32)
              + jnp.dot(hi, wu_ref[PACK_W:D_MODEL, :].astype(bf), preferred_element_type=jnp.float32))
        hid = ((gate * jax.nn.sigmoid(gate)) * up).astype(bf)
        y = jnp.dot(hid, wd_ref[...].astype(bf), preferred_element_type=jnp.float32)
        ys_ref[...] = _pack_bf16_pair(y[:, 0:PACK_W], y[:, PACK_W:D_MODEL])


def _experts(hs, tile_expert, n_used, wg, wu, wd, layer):
    n_rows = hs.shape[0]
    weights = lambda i, te, nu: (layer, te[i], 0, 0)
    return pl.pallas_call(
        _experts_kernel,
        out_shape=jax.ShapeDtypeStruct((n_rows, PACK_W), jnp.int32),
        grid_spec=pltpu.PrefetchScalarGridSpec(
            num_scalar_prefetch=2,
            grid=(n_rows // ROW_TILE,),
            in_specs=[pl.BlockSpec((ROW_TILE, PACK_W), lambda i, te, nu: (i, 0)),
                      pl.BlockSpec((None, None, D_MODEL, D_EXPERT), weights),
                      pl.BlockSpec((None, None, D_MODEL, D_EXPERT), weights),
                      pl.BlockSpec((None, None, D_EXPERT, D_MODEL), weights)],
            out_specs=pl.BlockSpec((ROW_TILE, PACK_W), lambda i, te, nu: (i, 0))),
        compiler_params=pltpu.CompilerParams(dimension_semantics=("arbitrary",), vmem_limit_bytes=VMEM_LIMIT),
        name="moe_experts",
    )(tile_expert, n_used, hs, wg, wu, wd)


COMB_TM = 1024


def _combine_kernel(x1_ref, y1_ref, y2_ref, cw_ref, gf_ref, out_ref, *, final_norm):
    c1 = cw_ref[:, 0:1]
    c2 = cw_ref[:, 1:2]
    lo1, hi1 = _unpack_bf16_pair(y1_ref[...])
    lo2, hi2 = _unpack_bf16_pair(y2_ref[...])
    y = jnp.concatenate([x1_ref[:, 0:PACK_W] + (c1 * lo1 + c2 * lo2),
                         x1_ref[:, PACK_W:D_MODEL] + (c1 * hi1 + c2 * hi2)], axis=1)
    if final_norm:
        ms = jnp.mean(y * y, axis=1, keepdims=True)
        y = y * lax.rsqrt(ms + RMS_EPS) * gf_ref[...]
    out_ref[...] = y


def _combine(x1, y12, cw, gf, final_norm):
    n_tok = x1.shape[0]
    tm = min(COMB_TM, n_tok)
    n_blk = n_tok // tm
    return pl.pallas_call(
        functools.partial(_combine_kernel, final_norm=final_norm),
        out_shape=jax.ShapeDtypeStruct((n_tok, D_MODEL), jnp.float32),
        grid=(n_blk,),
        in_specs=[pl.BlockSpec((tm, D_MODEL), lambda i: (i, 0)),
                  pl.BlockSpec((tm, PACK_W), lambda i: (i, 0)),
                  pl.BlockSpec((tm, PACK_W), lambda i: (n_blk + i, 0)),
                  pl.BlockSpec((tm, ROUTER_W), lambda i: (i, 0)),
                  pl.BlockSpec((1, D_MODEL), lambda i: (0, 0))],
        out_specs=pl.BlockSpec((tm, D_MODEL), lambda i: (i, 0)),
        compiler_params=pltpu.CompilerParams(dimension_semantics=("arbitrary",), vmem_limit_bytes=VMEM_LIMIT),
        name="moe_combine",
    )(x1, y12, y12, cw, gf)


def _sparse_moe(x1, h2p, route, cw, w_gate, w_up, w_down, layer, gf, final_norm):
    n_tok = x1.shape[0]
    n_rows = 2 * n_tok + N_EXPERTS * ROW_TILE
    rank, counts = _dispatch(route)
    padded = (counts[:, 0].astype(jnp.int32) + ROW_TILE - 1) // ROW_TILE * ROW_TILE
    ends = jnp.cumsum(padded)
    starts = ends - padded
    tile_start = jnp.arange(n_rows // ROW_TILE, dtype=jnp.int32) * ROW_TILE
    tile_expert = jnp.minimum(jnp.sum(tile_start[:, None] >= ends[None, :], axis=1), N_EXPERTS - 1).astype(jnp.int32)
    n_used = (ends[N_EXPERTS - 1:] // ROW_TILE).astype(jnp.int32)

    pos = _positions(route, rank, starts.astype(jnp.float32).reshape(N_EXPERTS, 1))
    pos12 = jnp.concatenate([pos[0], pos[1]]).astype(jnp.int32)
    hs = _sc_scatter_rows(h2p, pos12, n_rows)
    ys = _experts(hs, tile_expert, n_used, w_gate, w_up, w_down, layer)
    y12 = _sc_gather_rows(ys, pos12)
    return _combine(x1, y12, cw, gf, final_norm)


def _split_bf16(w):
    hi = w.astype(jnp.bfloat16)
    lo = (w - hi.astype(jnp.float32)).astype(jnp.bfloat16)
    return jnp.stack([hi, lo])


def _qkv_col_scale():
    s = np.ones((QKV_W,), np.float32)
    for group, dim in enumerate((HEAD_DIM, DIFF_DIM, HEAD_DIM, HEAD_DIM)):
        s[3 * group * GROUP_W:(3 * group + 1) * GROUP_W] = LOG2E * dim ** -0.5
    return s


def _forget_weights(w_f, b_f):
    rows = jnp.zeros((F_ROWS, D_MODEL), jnp.float32)
    bias = jnp.zeros((F_ROWS, 1), jnp.float32)
    for h in range(HEADS_PER_MIXER):
        r = (h // 2) * 8 + (h % 2)
        rows = rows.at[r].set(w_f[:, h])
        bias = bias.at[r, 0].set(b_f[h])
    return _split_bf16(rows), bias


def kernel(x, rel_bias, ln1, w_in, w_out, lam_q1, lam_k1, lam_q2, lam_k2, subln_g, b_forget,
           ln2, w_group, b_group, w_expert, b_expert, w_gate, w_up, w_down, ln_f):
    batch, seq, _ = x.shape
    depth = ln1.shape[0]
    assert seq % QBLK == 0 and seq % STEP_KEYS == 0
    tabs = _bias_tables(rel_bias)
    xt = x.reshape(batch * seq, D_MODEL)
    gf = ln_f.reshape(1, D_MODEL)
    col_scale = _qkv_col_scale()
    for l in range(depth):
        lambda_init = 0.8 - 0.6 * math.exp(-0.3 * l)
        wf, bf = _forget_weights(w_in[l][:, QKV_W:], b_forget[l])
        w_qkv = (w_in[l][:, :QKV_W] * col_scale).astype(jnp.bfloat16)
        qkv, lf = _norm_proj(xt, ln1[l].reshape(1, D_MODEL), w_qkv, wf, bf, batch, seq)
        c = _cumsum(lf)

        o_a = _dilated(qkv, tabs, batch, seq)
        lamv = jnp.zeros((8, LANES), jnp.float32)
        lamv = lamv.at[0:4, 0:DIFF_DIM].set(jnp.stack([lam_q1[l], lam_k1[l], lam_q2[l], lam_k2[l]]))
        g_sub = jnp.tile(subln_g[l], 2).reshape(1, PAIR_W)
        o_b = _diff(qkv, tabs, lamv, g_sub, lambda_init, batch, seq)
        o_c = _moba(qkv, tabs, batch, seq)
        o_d = _fox(qkv, c, batch, seq)

        w_router = jnp.zeros((D_MODEL, ROUTER_W), jnp.float32)
        w_router = w_router.at[:, :N_EXPERTS].set(w_expert[l]).at[:, N_EXPERTS:N_EXPERTS + N_GROUPS].set(w_group[l])
        b_router = jnp.zeros((1, ROUTER_W), jnp.float32)
        b_router = b_router.at[0, :N_EXPERTS].set(b_expert[l].reshape(-1)).at[0, N_EXPERTS:N_EXPERTS + N_GROUPS].set(b_group[l])
        x1, h2p, route, cw = _outproj_router([o_a, o_b, o_c, o_d], w_out[l].astype(jnp.bfloat16), xt,
                                             ln2[l].reshape(1, D_MODEL), _split_bf16(w_router), b_router)
        xt = _sparse_moe(x1, h2p, route, cw, w_gate, w_up, w_down, l, gf, final_norm=(l == depth - 1))
    return xt.reshape(batch, seq, D_MODEL)
```

```python
import functools
import math
from typing import NamedTuple

import jax
import jax.numpy as jnp
import numpy as np
from jax import lax
from jax.experimental import pallas as pl
from jax.experimental.pallas import tpu as pltpu

D_MODEL = 1024
HEAD_DIM = 64
HEADS_PER_MIXER = 4
GROUP_W = HEADS_PER_MIXER * HEAD_DIM
QKV_W = 12 * GROUP_W
DIFF_DIM = HEAD_DIM // 2
MOBA_BLOCK = 256
MOBA_TOPK = 3
N_BUCKETS = 32
MAX_DISTANCE = 2048
N_BIAS_HEADS = 12
N_GROUPS = 4
EXPERTS_PER_GROUP = 8
N_EXPERTS = 32
D_EXPERT = 256
RMS_EPS = 1e-6
NEG = -1e30
LOG2E = math.log2(math.e)

LANES = 128
PAIR_W = 2 * HEAD_DIM
ACC_W = 2 * PAIR_W
TQ = 256
TK = 256
MAX_SUB = 4
STEP_TILES = 2
STEP_KEYS = STEP_TILES * TK
N_DIL_TILES = MAX_DISTANCE // TK + 1
FAR_TILE = 7
MIN_DIST = 1 - MAX_SUB
N_TAB_TILES = N_DIL_TILES + 1 - MIN_DIST
VMEM_LIMIT = 48 * 1024 * 1024

_NT = (((1,), (1,)), ((), ()))


def _t5_thresholds():
    d = np.arange(0, 4 * MAX_DISTANCE, dtype=np.int64)
    max_exact = N_BUCKETS // 2
    df = np.maximum(d, 1).astype(np.float32)
    large = max_exact + (np.log(df / np.float32(max_exact)) / np.float32(math.log(MAX_DISTANCE / max_exact))
                         * np.float32(N_BUCKETS - max_exact)).astype(np.int32)
    bucket = np.where(d < max_exact, d, np.minimum(large, N_BUCKETS - 1))
    return [int(np.argmax(bucket >= b)) for b in range(N_BUCKETS)]


_T5_THR = _t5_thresholds()
assert (FAR_TILE - 1) * TK + 1 >= _T5_THR[N_BUCKETS - 1]


def _lane_iota(shape):
    return lax.broadcasted_iota(jnp.int32, shape, len(shape) - 1)


def _bias_table_kernel(bias_ref, out_ref):
    h = pl.program_id(0)
    n = pl.program_id(1) + MIN_DIST
    i = lax.broadcasted_iota(jnp.int32, (TQ, TK), 0)
    j = lax.broadcasted_iota(jnp.int32, (TQ, TK), 1)
    d = n * TK + i - j
    is_dil = h < HEADS_PER_MIXER
    far = jnp.logical_and(jnp.logical_not(is_dil), n >= FAR_TILE)
    d_b = jnp.where(far, 4 * MAX_DISTANCE, d)
    val = jnp.full((TQ, TK), bias_ref[h, 0], jnp.float32)
    for b in range(1, N_BUCKETS):
        val = jnp.where(d_b >= _T5_THR[b], bias_ref[h, b], val)
    mult = ((d <= 128).astype(jnp.int32)
            + jnp.logical_and((d & 3) == 0, d <= 512).astype(jnp.int32)
            + jnp.logical_and((d & 15) == 0, d <= 2048).astype(jnp.int32))
    logm = jnp.where(mult == 3, math.log(3.0), jnp.where(mult == 2, math.log(2.0), 0.0))
    val = val + jnp.where(is_dil, logm, 0.0)
    valid = jnp.logical_and(d >= 0, jnp.logical_or(mult > 0, jnp.logical_not(is_dil)))
    out_ref[...] = jnp.where(valid, val * LOG2E, NEG)


def _bias_tables(rel_bias):
    bias_h = rel_bias.T.astype(jnp.float32)
    return pl.pallas_call(
        _bias_table_kernel,
        out_shape=jax.ShapeDtypeStruct((N_BIAS_HEADS, N_TAB_TILES, TQ, TK), jnp.float32),
        grid=(N_BIAS_HEADS, N_TAB_TILES),
        in_specs=[pl.BlockSpec(memory_space=pltpu.SMEM)],
        out_specs=pl.BlockSpec((None, None, TQ, TK), lambda h, n: (h, n, 0, 0)),
        name="bias_tables",
    )(bias_h)


PROJ_TM = 1024
PROJ_CW = 512
F_ROWS = 16


def _norm_proj_kernel(x_ref, g_ref, w_ref, wf_ref, bf_ref, qkv_ref, lf_ref):
    x = x_ref[...]
    ms = jnp.mean(x * x, axis=1, keepdims=True)
    h = x * lax.rsqrt(ms + RMS_EPS) * g_ref[...]
    hb = h.astype(jnp.bfloat16)
    for c in range(QKV_W // PROJ_CW):
        cols = slice(c * PROJ_CW, (c + 1) * PROJ_CW)
        qkv_ref[:, cols] = jnp.dot(hb, w_ref[:, cols], preferred_element_type=jnp.float32).astype(jnp.bfloat16)
    h_lo = (h - hb.astype(jnp.float32)).astype(jnp.bfloat16)
    wf_hi = wf_ref[0]
    wf_lo = wf_ref[1]
    z = (lax.dot_general(wf_hi, hb, _NT, preferred_element_type=jnp.float32)
         + lax.dot_general(wf_hi, h_lo, _NT, preferred_element_type=jnp.float32)
         + lax.dot_general(wf_lo, hb, _NT, preferred_element_type=jnp.float32))
    z = z + bf_ref[...]
    lf_ref[...] = jnp.minimum(z, 0.0) - jnp.log(1.0 + jnp.exp(-jnp.abs(z)))


def _norm_proj(x2d, g, w_qkv, wf, bf, batch, seq):
    n_tok = x2d.shape[0]
    tm = min(PROJ_TM, seq)
    per_b = seq // tm
    return pl.pallas_call(
        _norm_proj_kernel,
        out_shape=(jax.ShapeDtypeStruct((n_tok, QKV_W), jnp.bfloat16),
                   jax.ShapeDtypeStruct((batch * F_ROWS, seq), jnp.float32)),
        grid=(n_tok // tm,),
        in_specs=[pl.BlockSpec((tm, D_MODEL), lambda i: (i, 0)),
                  pl.BlockSpec((1, D_MODEL), lambda i: (0, 0)),
                  pl.BlockSpec((D_MODEL, QKV_W), lambda i: (0, 0)),
                  pl.BlockSpec((2, F_ROWS, D_MODEL), lambda i: (0, 0, 0)),
                  pl.BlockSpec((F_ROWS, 1), lambda i: (0, 0))],
        out_specs=(pl.BlockSpec((tm, QKV_W), lambda i: (i, 0)),
                   pl.BlockSpec((F_ROWS, tm), lambda i: (i // per_b, i % per_b))),
        compiler_params=pltpu.CompilerParams(dimension_semantics=("arbitrary",), vmem_limit_bytes=VMEM_LIMIT),
        name="norm_proj",
    )(x2d, g, w_qkv, wf, bf)


def _split3(x):
    x1 = x.astype(jnp.bfloat16)
    r1 = x - x1.astype(jnp.float32)
    x2 = r1.astype(jnp.bfloat16)
    x3 = (r1 - x2.astype(jnp.float32)).astype(jnp.bfloat16)
    return x1, x2, x3


def _cumsum_kernel(lf_ref, c_ref):
    rows, seq = lf_ref.shape
    r = lax.broadcasted_iota(jnp.int32, (LANES, LANES), 0)
    c = lax.broadcasted_iota(jnp.int32, (LANES, LANES), 1)
    upper = jnp.where(r <= c, 1.0, 0.0).astype(jnp.bfloat16)

    def body(i, carry):
        off = pl.multiple_of(i * LANES, LANES)
        x1, x2, x3 = _split3(lf_ref[:, pl.ds(off, LANES)])
        cs = (jnp.dot(x1, upper, preferred_element_type=jnp.float32)
              + jnp.dot(x2, upper, preferred_element_type=jnp.float32)
              + jnp.dot(x3, upper, preferred_element_type=jnp.float32)) + carry
        c_ref[:, pl.ds(off, LANES)] = cs * LOG2E
        return cs[:, LANES - 1:LANES]

    lax.fori_loop(0, seq // LANES, body, jnp.zeros((rows, 1), jnp.float32))


def _cumsum(lf):
    return pl.pallas_call(
        _cumsum_kernel,
        out_shape=jax.ShapeDtypeStruct(lf.shape, jnp.float32),
        name="forget_cumsum",
    )(lf)


class _Geom(NamedTuple):
    n_streams: int
    n_sub: int

    @property
    def q_rows(self):
        return self.n_sub * TQ

    def rows(self, sub, st):
        rb = sub * self.n_streams + st
        return slice(rb * TQ, (rb + 1) * TQ)

    def blocks(self):
        return [(sub, st, self.rows(sub, st)) for sub in range(self.n_sub) for st in range(self.n_streams)]

    def first_tile(self, blk, m):
        return self.n_sub * blk + self.n_sub - STEP_TILES * (m + 1)

    def dist(self, m, sub, w):
        return STEP_TILES * (m + 1) - self.n_sub + sub - w

    def n_steps(self, blk):
        return (self.n_sub // STEP_TILES) * (blk + 1)

    def own_steps(self):
        return self.n_sub // STEP_TILES


def _stack_masked_q(q, qs_ref, geom):
    lane = _lane_iota((TQ, PAIR_W))
    width = PAIR_W // geom.n_streams
    for sub, st, rows in geom.blocks():
        q_tile = q[sub * TQ:(sub + 1) * TQ]
        qs_ref[rows, 0:PAIR_W] = jnp.where(lane // width == st, q_tile, jnp.zeros_like(q_tile))


def _init_state(m_ref, acc_ref):
    m_ref[...] = jnp.full(m_ref.shape, NEG, jnp.float32)
    acc_ref[...] = jnp.zeros(acc_ref.shape, jnp.float32)


def _flash_step(rows, s, v_aug, m_ref, acc_ref):
    m_prev = m_ref[rows, :]
    m_new = jnp.maximum(m_prev, jnp.max(s, axis=1, keepdims=True))
    alpha = jnp.exp2(m_prev - m_new)
    p = jnp.concatenate([jnp.exp2(s[:, c * LANES:(c + 1) * LANES] - m_new) for c in range(s.shape[1] // LANES)],
                        axis=1).astype(v_aug.dtype)
    pv = jnp.dot(p, v_aug, preferred_element_type=jnp.float32)
    acc_ref[rows, :] = jnp.concatenate([alpha, alpha], axis=1) * acc_ref[rows, :] + pv
    m_ref[rows, :] = m_new


def _normalised(acc_ref, rows):
    return acc_ref[rows, 0:PAIR_W] / acc_ref[rows, PAIR_W:ACC_W]


def _table_bias(tab_ref, hh, geom, m, sub, cap):
    return jnp.concatenate([tab_ref[hh, jnp.minimum(geom.dist(m, sub, w), cap) - MIN_DIST]
                            for w in range(STEP_TILES)], axis=1)


def _step_rows(ref, geom, blk, m):
    k0 = pl.multiple_of(geom.first_tile(blk, m) * TK, STEP_KEYS)
    return ref[pl.ds(k0, STEP_KEYS), :]


def _make_attend(geom, blk, scratch, v_ref, keys, finish):
    qs_ref, m_ref, acc_ref, s_ref = scratch
    blocks = geom.blocks()
    last_step = geom.n_steps(blk) - 1

    def raw(rows, rhs):
        return lax.dot_general(qs_ref[rows, :], rhs, _NT, preferred_element_type=jnp.float32)

    def v_aug(m):
        v = _step_rows(v_ref, geom, blk, m)
        return jnp.concatenate([v, jnp.ones_like(v)], axis=1)

    def single(m):
        rhs = keys(m)
        va = v_aug(m)
        for sub, st, rows in blocks:
            _flash_step(rows, finish(sub, st, raw(rows, rhs), m), va, m_ref, acc_ref)

    def run(m0, count):
        @pl.when(count > 0)
        def _():
            rhs = keys(m0)
            for _, _, rows in blocks:
                s_ref[rows, :] = raw(rows, rhs)

        def body(i, carry):
            m = m0 + i
            va = v_aug(m)
            rhs_next = keys(jnp.minimum(m + 1, last_step))
            for sub, st, rows in blocks:
                cur = s_ref[rows, :]
                s_ref[rows, :] = raw(rows, rhs_next)
                _flash_step(rows, finish(sub, st, cur, m), va, m_ref, acc_ref)
            return carry

        lax.fori_loop(0, count, body, 0)

    return single, run


def _attn_call(kernel, group, qkv, extra_inputs, extra_specs, geom, batch, seq, *, q_width=PAIR_W,
               extra_scratch=(), name=None):
    assert seq % geom.q_rows == 0 and seq % STEP_KEYS == 0
    nq = seq // geom.q_rows
    qc, kc, vc = 6 * group, 6 * group + 2, 6 * group + 4
    rows = geom.n_sub * geom.n_streams * TQ
    return pl.pallas_call(
        functools.partial(kernel, geom=geom),
        out_shape=jax.ShapeDtypeStruct((batch * seq, GROUP_W), jnp.bfloat16),
        grid=(2, batch, nq),
        in_specs=[pl.BlockSpec((geom.q_rows, PAIR_W), lambda p, b, i: (b * nq + i, qc + p)),
                  pl.BlockSpec((seq, PAIR_W), lambda p, b, i: (b, kc + p)),
                  pl.BlockSpec((seq, PAIR_W), lambda p, b, i: (b, vc + p))] + list(extra_specs),
        out_specs=pl.BlockSpec((geom.q_rows, PAIR_W), lambda p, b, i: (b * nq + i, p)),
        scratch_shapes=[pltpu.VMEM((rows, q_width), jnp.bfloat16),
                        pltpu.VMEM((rows, LANES), jnp.float32),
                        pltpu.VMEM((rows, ACC_W), jnp.float32),
                        pltpu.VMEM((rows, STEP_KEYS), jnp.float32)
                        ] + list(extra_scratch),
        compiler_params=pltpu.CompilerParams(dimension_semantics=("arbitrary", "arbitrary", "arbitrary"),
                                             vmem_limit_bytes=VMEM_LIMIT),
        name=name,
    )(qkv, qkv, qkv, *extra_inputs)


def _tab_spec(first_head):
    return pl.BlockSpec((2, N_TAB_TILES, TQ, TK), lambda p, b, i: (first_head // 2 + p, 0, 0, 0))


def _merge_pair(o0, o1):
    return jnp.where(_lane_iota(o0.shape) < HEAD_DIM, o0, o1)


def _store_pair_outputs(o_ref, acc_ref, geom):
    for sub in range(geom.n_sub):
        o = _merge_pair(_normalised(acc_ref, geom.rows(sub, 0)), _normalised(acc_ref, geom.rows(sub, 1)))
        o_ref[sub * TQ:(sub + 1) * TQ, :] = o.astype(o_ref.dtype)


PAIR_GEOM = _Geom(n_streams=2, n_sub=MAX_SUB)
DIFF_GEOM = _Geom(n_streams=4, n_sub=2)


def _dilated_kernel(q_ref, k_ref, v_ref, tab_ref, o_ref, qs_ref, m_ref, acc_ref, s_ref, *, geom):
    blk = pl.program_id(2)
    _stack_masked_q(q_ref[...], qs_ref, geom)
    _init_state(m_ref, acc_ref)
    _, run = _make_attend(
        geom, blk, (qs_ref, m_ref, acc_ref, s_ref), v_ref,
        keys=lambda m: _step_rows(k_ref, geom, blk, m),
        finish=lambda sub, st, raw, m: raw + _table_bias(tab_ref, st, geom, m, sub, N_DIL_TILES))
    reach = -(-(N_DIL_TILES - 1 + geom.n_sub) // STEP_TILES)
    run(0, jnp.minimum(geom.n_steps(blk), reach))
    _store_pair_outputs(o_ref, acc_ref, geom)


def _dilated(qkv, tabs, batch, seq):
    return _attn_call(_dilated_kernel, 0, qkv, [tabs], [_tab_spec(0)], PAIR_GEOM, batch, seq, name="dilated_attn")


def _diff_kernel(q_ref, k_ref, v_ref, tab_ref, lam_ref, g_ref, o_ref, qs_ref, m_ref, acc_ref, s_ref, *,
                 lambda_init, geom):
    blk = pl.program_id(2)
    _stack_masked_q(q_ref[...], qs_ref, geom)
    _init_state(m_ref, acc_ref)
    _, run = _make_attend(
        geom, blk, (qs_ref, m_ref, acc_ref, s_ref), v_ref,
        keys=lambda m: _step_rows(k_ref, geom, blk, m),
        finish=lambda sub, st, raw, m: raw + _table_bias(tab_ref, st // 2, geom, m, sub, FAR_TILE))
    run(0, geom.n_steps(blk))

    lamv = lam_ref[...]
    lam = (jnp.exp(jnp.sum(lamv[0:1] * lamv[1:2], axis=1, keepdims=True))
           - jnp.exp(jnp.sum(lamv[2:3] * lamv[3:4], axis=1, keepdims=True)) + lambda_init)
    first = _lane_iota((TQ, PAIR_W)) < HEAD_DIM
    for sub in range(geom.n_sub):
        part = [_normalised(acc_ref, geom.rows(sub, st)) for st in range(geom.n_streams)]
        o = _merge_pair(part[0] - lam * part[1], part[2] - lam * part[3])
        sq = o * o
        ms0 = jnp.sum(jnp.where(first, sq, 0.0), axis=1, keepdims=True)
        ms1 = jnp.sum(jnp.where(first, 0.0, sq), axis=1, keepdims=True)
        ms = jnp.where(first, ms0, ms1) * (1.0 / HEAD_DIM)
        y = (o * lax.rsqrt(ms + RMS_EPS) * g_ref[...]) * (1.0 - lambda_init)
        o_ref[sub * TQ:(sub + 1) * TQ, :] = y.astype(o_ref.dtype)


def _diff(qkv, tabs, lamv, g_sub, lambda_init, batch, seq):
    return _attn_call(functools.partial(_diff_kernel, lambda_init=lambda_init), 1, qkv, [tabs, lamv, g_sub],
                      [_tab_spec(4), pl.BlockSpec((8, LANES), lambda p, b, i: (0, 0)),
                       pl.BlockSpec((1, PAIR_W), lambda p, b, i: (0, 0))],
                      DIFF_GEOM, batch, seq, name="diff_attn")


def _moba_kernel(q_ref, k_ref, v_ref, tab_ref, o_ref, qs_ref, m_ref, acc_ref, s_ref, km_ref, *, geom):
    blk = pl.program_id(2)
    n_blk = k_ref.shape[0] // MOBA_BLOCK
    assert n_blk <= LANES and MOBA_BLOCK == TQ == TK

    @pl.when(blk == 0)
    def _():
        km_ref[...] = jnp.zeros(km_ref.shape, jnp.float32)
        for n in range(n_blk):
            km_ref[n:n + 1, :] = jnp.sum(k_ref[n * MOBA_BLOCK:(n + 1) * MOBA_BLOCK, :].astype(jnp.float32),
                                         axis=0, keepdims=True) * (1.0 / MOBA_BLOCK)

    _stack_masked_q(q_ref[...], qs_ref, geom)
    _init_state(m_ref, acc_ref)
    nb = -(-n_blk // 8) * 8
    km = km_ref[0:nb, :]
    km_hi = km.astype(jnp.bfloat16)
    km_lo = (km - km_hi.astype(jnp.float32)).astype(jnp.bfloat16)
    qs = qs_ref[:, 0:PAIR_W]
    gate = (lax.dot_general(km_hi, qs, _NT, preferred_element_type=jnp.float32)
            + lax.dot_general(km_lo, qs, _NT, preferred_element_type=jnp.float32))
    cand = lax.broadcasted_iota(jnp.int32, gate.shape, 0)
    cand_f = cand.astype(jnp.float32)
    own = geom.n_sub * blk + _lane_iota(gate.shape) // (geom.n_streams * TQ)
    g = jnp.where(cand < own, gate, NEG)
    sel = cand == own
    for _ in range(MOBA_TOPK):
        mx = jnp.max(g, axis=0, keepdims=True)
        first = jnp.min(jnp.where(g == mx, cand_f, 2.0 * LANES), axis=0, keepdims=True)
        pick = cand_f == first
        sel = jnp.logical_or(sel, jnp.logical_and(pick, mx > 0.5 * NEG))
        g = jnp.where(pick, -3e38, g)
    mask = jnp.where(sel, 0.0, NEG)
    if nb < LANES:
        mask = jnp.concatenate([mask, jnp.full((LANES - nb, mask.shape[1]), NEG, jnp.float32)], axis=0)
    qs_ref[:, PAIR_W:2 * PAIR_W] = mask.T.astype(jnp.bfloat16)

    def keys(m):
        k = _step_rows(k_ref, geom, blk, m)
        block_of_key = geom.first_tile(blk, m) + lax.broadcasted_iota(jnp.int32, (STEP_KEYS, LANES), 0) // TK
        onehot = jnp.where(_lane_iota((STEP_KEYS, LANES)) == block_of_key, 1.0, 0.0).astype(k.dtype)
        return jnp.concatenate([k, onehot], axis=1)

    _, run = _make_attend(
        geom, blk, (qs_ref, m_ref, acc_ref, s_ref), v_ref, keys=keys,
        finish=lambda sub, st, raw, m: raw + _table_bias(tab_ref, st, geom, m, sub, FAR_TILE))
    run(0, geom.n_steps(blk))
    _store_pair_outputs(o_ref, acc_ref, geom)


def _moba(qkv, tabs, batch, seq):
    return _attn_call(_moba_kernel, 2, qkv, [tabs], [_tab_spec(8)], PAIR_GEOM, batch, seq, q_width=2 * PAIR_W,
                      extra_scratch=[pltpu.VMEM((LANES, PAIR_W), jnp.float32)], name="moba_attn")


FOX_SKIP_BITS = 48.0


def _head_row_norms(x, hh):
    xf = x.astype(jnp.float32)
    sq = jnp.where(_lane_iota(xf.shape) // HEAD_DIM == hh, xf * xf, 0.0)
    return jnp.sqrt(jnp.max(jnp.sum(sq, axis=1, keepdims=True), axis=0, keepdims=True))


def _fox_kernel(q_ref, k_ref, v_ref, c_ref, o_ref, qs_ref, m_ref, acc_ref, s_ref, cq_ref, stat_ref, *, geom):
    blk = pl.program_id(2)
    n_tiles = k_ref.shape[0] // TK
    assert n_tiles <= LANES

    @pl.when(blk == 0)
    def _():
        stat_ref[...] = jnp.zeros(stat_ref.shape, jnp.float32)
        for j in range(n_tiles):
            k_tile = k_ref[j * TK:(j + 1) * TK, :]
            for hh in range(2):
                stat_ref[hh:hh + 1, j:j + 1] = _head_row_norms(k_tile, hh)
                stat_ref[2 + hh:3 + hh, j:j + 1] = jnp.min(c_ref[hh:hh + 1, j * TK:(j + 1) * TK], axis=1, keepdims=True)

    q = q_ref[...]
    _stack_masked_q(q, qs_ref, geom)
    _init_state(m_ref, acc_ref)
    q0 = pl.multiple_of(blk * geom.q_rows, geom.q_rows)
    row = lax.broadcasted_iota(jnp.int32, (TQ, TQ), 0)
    col = lax.broadcasted_iota(jnp.int32, (TQ, TQ), 1)
    for sub, hh, rows in geom.blocks():
        c_row = c_ref[hh:hh + 1, pl.ds(q0 + sub * TQ, TQ)]
        c_col = jnp.sum(jnp.where(row == col, c_row, 0.0), axis=1, keepdims=True)
        cq_ref[rows, :] = jnp.broadcast_to(c_col, (TQ, LANES))

    key_pos = lax.broadcasted_iota(jnp.int32, (TQ, STEP_KEYS), 1)
    query_pos = lax.broadcasted_iota(jnp.int32, (TQ, STEP_KEYS), 0)

    def finish(sub, hh, raw, m):
        k0 = pl.multiple_of(geom.first_tile(blk, m) * TK, STEP_KEYS)
        cq = cq_ref[geom.rows(sub, hh), :]
        s = raw + (jnp.concatenate([cq] * (STEP_KEYS // LANES), axis=1) - c_ref[hh:hh + 1, pl.ds(k0, STEP_KEYS)])
        if isinstance(m, int) and m < geom.own_steps():
            first_key = (geom.n_sub - STEP_TILES * (m + 1)) * TK
            s = jnp.where(key_pos + first_key <= query_pos + sub * TQ, s, NEG)
        return s

    single, run = _make_attend(geom, blk, (qs_ref, m_ref, acc_ref, s_ref), v_ref,
                               keys=lambda m: _step_rows(k_ref, geom, blk, m), finish=finish)
    for m in range(geom.own_steps()):
        single(m)

    tile = _lane_iota((1, LANES))
    needed = jnp.zeros((1, LANES), jnp.bool_)
    for hh in range(2):
        rows_h = [geom.rows(sub, hh) for sub in range(geom.n_sub)]
        m_min = functools.reduce(jnp.minimum, [jnp.min(m_ref[r, :], axis=0, keepdims=True) for r in rows_h])
        cq_max = functools.reduce(jnp.maximum, [jnp.max(cq_ref[r, :], axis=0, keepdims=True) for r in rows_h])
        bound = _head_row_norms(q, hh) * stat_ref[hh:hh + 1, :] + cq_max - stat_ref[2 + hh:3 + hh, :] + 1.0
        needed = jnp.logical_or(needed, bound >= m_min - FOX_SKIP_BITS)
    n_before = geom.n_sub * blk
    first_needed = jnp.min(jnp.where(jnp.logical_and(needed, tile < n_before), tile, n_before))
    run(geom.own_steps(), (n_before - first_needed + STEP_TILES - 1) // STEP_TILES)
    _store_pair_outputs(o_ref, acc_ref, geom)


def _fox(qkv, c, batch, seq):
    return _attn_call(_fox_kernel, 3, qkv, [c.reshape(batch, 2, 8, seq)],
                      [pl.BlockSpec((None, None, 8, seq), lambda p, b, i: (b, p, 0, 0))],
                      PAIR_GEOM, batch, seq,
                      extra_scratch=[pltpu.VMEM((PAIR_GEOM.n_sub * PAIR_GEOM.n_streams * TQ, LANES), jnp.float32),
                                     pltpu.VMEM((8, LANES), jnp.float32)], name="fox_attn")


OUT_TM = 1024
ROUTER_W = LANES
ROUTE_ROWS = 8


PACK_W = D_MODEL // 2


def _pack_bf16_pair(lo, hi):
    lo_bits = lax.bitcast_convert_type(lo.astype(jnp.bfloat16).astype(jnp.float32), jnp.uint32)
    hi_bits = lax.bitcast_convert_type(hi.astype(jnp.bfloat16).astype(jnp.float32), jnp.uint32)
    word = lax.shift_right_logical(lo_bits, jnp.uint32(16)) | (hi_bits & jnp.uint32(0xFFFF0000))
    return lax.bitcast_convert_type(word, jnp.int32)


def _unpack_bf16_pair(word):
    bits = lax.bitcast_convert_type(word, jnp.uint32)
    lo = lax.bitcast_convert_type(lax.shift_left(bits, jnp.uint32(16)), jnp.float32)
    hi = lax.bitcast_convert_type(bits & jnp.uint32(0xFFFF0000), jnp.float32)
    return lo, hi


def _outproj_router_kernel(oa_ref, ob_ref, oc_ref, od_ref, wo_ref, x_ref, g_ref, wr_ref, br_ref,
                           x1_ref, h2_ref, route_ref, cw_ref):
    acc = x_ref[...]
    for m, o_ref in enumerate((oa_ref, ob_ref, oc_ref, od_ref)):
        acc = acc + jnp.dot(o_ref[...], wo_ref[m * GROUP_W:(m + 1) * GROUP_W, :], preferred_element_type=jnp.float32)
    x1_ref[...] = acc
    ms = jnp.mean(acc * acc, axis=1, keepdims=True)
    h = acc * lax.rsqrt(ms + RMS_EPS) * g_ref[...]
    hb = h.astype(jnp.bfloat16)
    h2_ref[...] = _pack_bf16_pair(h[:, 0:PACK_W], h[:, PACK_W:D_MODEL])
    h_lo = (h - hb.astype(jnp.float32)).astype(jnp.bfloat16)
    logits = (jnp.dot(hb, wr_ref[0], preferred_element_type=jnp.float32)
              + jnp.dot(h_lo, wr_ref[0], preferred_element_type=jnp.float32)
              + jnp.dot(hb, wr_ref[1], preferred_element_type=jnp.float32)) + br_ref[...]
    lane = _lane_iota(logits.shape).astype(jnp.float32)
    big = 2.0 * LANES
    gmask = jnp.logical_and(lane >= N_EXPERTS, lane < N_EXPERTS + N_GROUPS)
    gl = jnp.where(gmask, logits, NEG)
    gmax = jnp.max(gl, axis=1, keepdims=True)
    glane = jnp.min(jnp.where(gl == gmax, lane, big), axis=1, keepdims=True)
    gsum = jnp.sum(jnp.where(gmask, jnp.exp(gl - gmax), 0.0), axis=1, keepdims=True)
    g_w = 1.0 / gsum
    e0 = (glane - N_EXPERTS) * EXPERTS_PER_GROUP
    emask = jnp.logical_and(lane >= e0, lane < e0 + EXPERTS_PER_GROUP)
    el = jnp.where(emask, logits, NEG)
    v1 = jnp.max(el, axis=1, keepdims=True)
    i1 = jnp.min(jnp.where(el == v1, lane, big), axis=1, keepdims=True)
    el2 = jnp.where(lane == i1, NEG, el)
    v2 = jnp.max(el2, axis=1, keepdims=True)
    i2 = jnp.min(jnp.where(el2 == v2, lane, big), axis=1, keepdims=True)
    e2 = jnp.exp(v2 - v1)
    den = 1.0 + e2
    c1 = g_w / den
    c2 = g_w * e2 / den
    quad = jnp.where(lane == 0.0, i1, jnp.where(lane == 1.0, i2, jnp.where(lane == 2.0, c1,
                                                                             jnp.where(lane == 3.0, c2, 0.0))))
    cw_ref[...] = jnp.where(lane == 0.0, c1, jnp.where(lane == 1.0, c2, 0.0))
    route_ref[...] = quad.T[0:ROUTE_ROWS, :]


def _outproj_router(o_parts, w_out, x2d, g2, wr, br):
    n_tok = x2d.shape[0]
    tm = min(OUT_TM, n_tok)
    row = lambda i: (i, 0)
    fixed = lambda i: (0, 0)
    return pl.pallas_call(
        _outproj_router_kernel,
        out_shape=(jax.ShapeDtypeStruct((n_tok, D_MODEL), jnp.float32),
                   jax.ShapeDtypeStruct((n_tok, PACK_W), jnp.int32),
                   jax.ShapeDtypeStruct((ROUTE_ROWS, n_tok), jnp.float32),
                   jax.ShapeDtypeStruct((n_tok, ROUTER_W), jnp.float32)),
        grid=(n_tok // tm,),
        in_specs=[pl.BlockSpec((tm, GROUP_W), row)] * 4 + [
            pl.BlockSpec((D_MODEL, D_MODEL), fixed),
            pl.BlockSpec((tm, D_MODEL), row),
            pl.BlockSpec((1, D_MODEL), fixed),
            pl.BlockSpec((2, D_MODEL, ROUTER_W), lambda i: (0, 0, 0)),
            pl.BlockSpec((1, ROUTER_W), fixed)],
        out_specs=(pl.BlockSpec((tm, D_MODEL), row), pl.BlockSpec((tm, PACK_W), row),
                   pl.BlockSpec((ROUTE_ROWS, tm), lambda i: (0, i)), pl.BlockSpec((tm, ROUTER_W), row)),
        compiler_params=pltpu.CompilerParams(dimension_semantics=("arbitrary",), vmem_limit_bytes=VMEM_LIMIT),
        name="outproj_router",
    )(*o_parts, w_out, x2d, g2, wr, br)


ROW_TILE = 512
DISP_TM = 1024
SC_CORES = 2
SC_SUBCORES = 16
SC_WORKERS = SC_CORES * SC_SUBCORES
SC_CHUNK = 64


def _dispatch_kernel(route_ref, rank_ref, cnt_ref, tri_ref, carry_ref):
    step = pl.program_id(0)
    tm = route_ref.shape[1]

    @pl.when(step == 0)
    def _():
        r = lax.broadcasted_iota(jnp.int32, (tm, tm), 0)
        c = lax.broadcasted_iota(jnp.int32, (tm, tm), 1)
        tri_ref[...] = jnp.where(r <= c, 1.0, 0.0).astype(jnp.bfloat16)
        carry_ref[...] = jnp.zeros(carry_ref.shape, jnp.float32)

    e1 = route_ref[0:1, :]
    e2 = route_ref[1:2, :]
    expert = lax.broadcasted_iota(jnp.int32, (N_EXPERTS, tm), 0).astype(jnp.float32)
    hit1 = e1 == expert
    hit2 = e2 == expert
    hits = jnp.where(jnp.logical_or(hit1, hit2), 1.0, 0.0).astype(jnp.bfloat16)
    cum = jnp.dot(hits, tri_ref[...], preferred_element_type=jnp.float32) + carry_ref[...]
    rank1 = jnp.sum(jnp.where(hit1, cum - 1.0, 0.0), axis=0, keepdims=True)
    rank2 = jnp.sum(jnp.where(hit2, cum - 1.0, 0.0), axis=0, keepdims=True)
    row = lax.broadcasted_iota(jnp.int32, (ROUTE_ROWS, tm), 0)
    rank_ref[...] = jnp.where(row == 0, rank1, jnp.where(row == 1, rank2, 0.0))
    total = cum[:, tm - 1:tm]
    carry_ref[...] = total
    cnt_ref[...] = jnp.broadcast_to(total, cnt_ref.shape)


def _dispatch(route):
    n_tok = route.shape[1]
    tm = min(DISP_TM, n_tok)
    return pl.pallas_call(
        _dispatch_kernel,
        out_shape=(jax.ShapeDtypeStruct((ROUTE_ROWS, n_tok), jnp.float32),
                   jax.ShapeDtypeStruct((N_EXPERTS, LANES), jnp.float32)),
        grid=(n_tok // tm,),
        in_specs=[pl.BlockSpec((ROUTE_ROWS, tm), lambda i: (0, i))],
        out_specs=(pl.BlockSpec((ROUTE_ROWS, tm), lambda i: (0, i)),
                   pl.BlockSpec((N_EXPERTS, LANES), lambda i: (0, 0))),
        scratch_shapes=[pltpu.VMEM((tm, tm), jnp.bfloat16), pltpu.VMEM((N_EXPERTS, 1), jnp.float32)],
        compiler_params=pltpu.CompilerParams(dimension_semantics=("arbitrary",)),
        name="moe_ranks",
    )(route)


def _positions_kernel(route_ref, rank_ref, off_ref, pos_ref):
    tm = route_ref.shape[1]
    expert = lax.broadcasted_iota(jnp.int32, (N_EXPERTS, tm), 0).astype(jnp.float32)
    off = off_ref[...]
    start1 = jnp.sum(jnp.where(route_ref[0:1, :] == expert, off, 0.0), axis=0, keepdims=True)
    start2 = jnp.sum(jnp.where(route_ref[1:2, :] == expert, off, 0.0), axis=0, keepdims=True)
    row = lax.broadcasted_iota(jnp.int32, (ROUTE_ROWS, tm), 0)
    pos_ref[...] = jnp.where(row == 0, start1 + rank_ref[0:1, :], jnp.where(row == 1, start2 + rank_ref[1:2, :], 0.0))


def _positions(route, rank, seg_start):
    n_tok = route.shape[1]
    tm = min(DISP_TM, n_tok)
    blk = pl.BlockSpec((ROUTE_ROWS, tm), lambda i: (0, i))
    return pl.pallas_call(
        _positions_kernel,
        out_shape=jax.ShapeDtypeStruct((ROUTE_ROWS, n_tok), jnp.float32),
        grid=(n_tok // tm,),
        in_specs=[blk, blk, pl.BlockSpec((N_EXPERTS, 1), lambda i: (0, 0))],
        out_specs=blk,
        name="moe_positions",
    )(route, rank, seg_start)


def _sc_mesh():
    from jax.experimental.pallas import tpu_sc as plsc
    return plsc.VectorSubcoreMesh(core_axis_name="c", subcore_axis_name="s")


def _sc_worker_id():
    return lax.axis_index("s") * SC_CORES + lax.axis_index("c")


def _sc_scatter_rows(rows, pos, n_out):
    n_tok, width = rows.shape
    per_worker = n_tok // SC_WORKERS
    n_pairs = per_worker // (2 * SC_CHUNK)
    assert per_worker % (2 * SC_CHUNK) == 0
    buf = [pltpu.VMEM((SC_CHUNK,), jnp.int32), pltpu.VMEM((SC_CHUNK,), jnp.int32),
           pltpu.VMEM((SC_CHUNK, width), rows.dtype), pltpu.SemaphoreType.DMA, pltpu.SemaphoreType.DMA]

    @functools.partial(pl.kernel, mesh=_sc_mesh(), out_type=jax.ShapeDtypeStruct((n_out, width), rows.dtype),
                       scratch_types=buf + buf)
    def scatter(rows_hbm, pos_hbm, out_hbm, *scratch):
        base = _sc_worker_id() * per_worker
        buf_a, buf_b = scratch[:5], scratch[5:]

        def load(chunk, p1_v, p2_v, rows_v, sem_in, sem_out):
            off = base + chunk * SC_CHUNK
            pltpu.sync_copy(pos_hbm.at[pl.ds(off, SC_CHUNK)], p1_v)
            pltpu.sync_copy(pos_hbm.at[pl.ds(n_tok + off, SC_CHUNK)], p2_v)
            pltpu.async_copy(rows_hbm.at[pl.ds(off, SC_CHUNK)], rows_v, sem_in)

        def flush(chunk, p1_v, p2_v, rows_v, sem_in, sem_out):
            off = base + chunk * SC_CHUNK
            pltpu.make_async_copy(rows_hbm.at[pl.ds(off, SC_CHUNK)], rows_v, sem_in).wait()
            first = pltpu.async_copy(rows_v, out_hbm.at[p1_v], sem_out)
            second = pltpu.async_copy(rows_v, out_hbm.at[p2_v], sem_out)
            first.wait()
            second.wait()

        load(0, *buf_a)
        def body(j, carry):
            load(2 * j + 1, *buf_b)
            flush(2 * j, *buf_a)

            @pl.when(j + 1 < n_pairs)
            def _():
                load(2 * j + 2, *buf_a)

            flush(2 * j + 1, *buf_b)
            return carry

        lax.fori_loop(0, n_pairs, body, 0)

    return scatter(rows, pos)


def _sc_gather_rows(table, idx):
    n_idx = idx.shape[0]
    width = table.shape[1]
    per_worker = n_idx // SC_WORKERS
    n_pairs = per_worker // (2 * SC_CHUNK)
    assert per_worker % (2 * SC_CHUNK) == 0
    buf = [pltpu.VMEM((SC_CHUNK,), jnp.int32), pltpu.VMEM((SC_CHUNK, width), table.dtype), pltpu.SemaphoreType.DMA]

    @functools.partial(pl.kernel, mesh=_sc_mesh(), out_type=jax.ShapeDtypeStruct((n_idx, width), table.dtype),
                       scratch_types=buf + buf)
    def gather(table_hbm, idx_hbm, out_hbm, *scratch):
        base = _sc_worker_id() * per_worker
        buf_a, buf_b = scratch[:3], scratch[3:]

        def fetch(chunk, idx_v, rows_v, sem):
            pltpu.sync_copy(idx_hbm.at[pl.ds(base + chunk * SC_CHUNK, SC_CHUNK)], idx_v)
            pltpu.async_copy(table_hbm.at[idx_v], rows_v, sem)

        def drain(chunk, idx_v, rows_v, sem):
            pltpu.make_async_copy(table_hbm.at[idx_v], rows_v, sem).wait()
            pltpu.sync_copy(rows_v, out_hbm.at[pl.ds(base + chunk * SC_CHUNK, SC_CHUNK)])

        fetch(0, *buf_a)
        def body(j, carry):
            fetch(2 * j + 1, *buf_b)
            drain(2 * j, *buf_a)

            @pl.when(j + 1 < n_pairs)
            def _():
                fetch(2 * j + 2, *buf_a)

            drain(2 * j + 1, *buf_b)
            return carry

        lax.fori_loop(0, n_pairs, body, 0)

    return gather(table, idx)


def _experts_kernel(tile_expert_ref, n_used_ref, hs_ref, wg_ref, wu_ref, wd_ref, ys_ref):
    del tile_expert_ref

    @pl.when(pl.program_id(0) < n_used_ref[0])
    def _():
        lo, hi = _unpack_bf16_pair(hs_ref[...])
        lo = lo.astype(jnp.bfloat16)
        hi = hi.astype(jnp.bfloat16)
        bf = jnp.bfloat16
        gate = (jnp.dot(lo, wg_ref[0:PACK_W, :].astype(bf), preferred_element_type=jnp.float32)
                + jnp.dot(hi, wg_ref[PACK_W:D_MODEL, :].astype(bf), preferred_element_type=jnp.float32))
        up = (jnp.dot(lo, wu_ref[0:PACK_W, :].astype(bf), preferred_element_type=jnp.float32)
              + jnp.dot(hi, wu_ref[PACK_W:D_MODEL, :].astype(bf), preferred_element_type=jnp.float32))
        hid = ((gate * jax.nn.sigmoid(gate)) * up).astype(bf)
        y = jnp.dot(hid, wd_ref[...].astype(bf), preferred_element_type=jnp.float32)
        ys_ref[...] = _pack_bf16_pair(y[:, 0:PACK_W], y[:, PACK_W:D_MODEL])


def _experts(hs, tile_expert, n_used, wg, wu, wd, layer):
    n_rows = hs.shape[0]
    weights = lambda i, te, nu: (layer, te[i], 0, 0)
    return pl.pallas_call(
        _experts_kernel,
        out_shape=jax.ShapeDtypeStruct((n_rows, PACK_W), jnp.int32),
        grid_spec=pltpu.PrefetchScalarGridSpec(
            num_scalar_prefetch=2,
            grid=(n_rows // ROW_TILE,),
            in_specs=[pl.BlockSpec((ROW_TILE, PACK_W), lambda i, te, nu: (i, 0)),
                      pl.BlockSpec((None, None, D_MODEL, D_EXPERT), weights),
                      pl.BlockSpec((None, None, D_MODEL, D_EXPERT), weights),
                      pl.BlockSpec((None, None, D_EXPERT, D_MODEL), weights)],
            out_specs=pl.BlockSpec((ROW_TILE, PACK_W), lambda i, te, nu: (i, 0))),
        compiler_params=pltpu.CompilerParams(dimension_semantics=("arbitrary",), vmem_limit_bytes=VMEM_LIMIT),
        name="moe_experts",
    )(tile_expert, n_used, hs, wg, wu, wd)


COMB_TM = 1024


def _combine_kernel(x1_ref, y1_ref, y2_ref, cw_ref, gf_ref, out_ref, *, final_norm):
    c1 = cw_ref[:, 0:1]
    c2 = cw_ref[:, 1:2]
    lo1, hi1 = _unpack_bf16_pair(y1_ref[...])
    lo2, hi2 = _unpack_bf16_pair(y2_ref[...])
    y = jnp.concatenate([x1_ref[:, 0:PACK_W] + (c1 * lo1 + c2 * lo2),
                         x1_ref[:, PACK_W:D_MODEL] + (c1 * hi1 + c2 * hi2)], axis=1)
    if final_norm:
        ms = jnp.mean(y * y, axis=1, keepdims=True)
        y = y * lax.rsqrt(ms + RMS_EPS) * gf_ref[...]
    out_ref[...] = y


def _combine(x1, y12, cw, gf, final_norm):
    n_tok = x1.shape[0]
    tm = min(COMB_TM, n_tok)
    n_blk = n_tok // tm
    return pl.pallas_call(
        functools.partial(_combine_kernel, final_norm=final_norm),
        out_shape=jax.ShapeDtypeStruct((n_tok, D_MODEL), jnp.float32),
        grid=(n_blk,),
        in_specs=[pl.BlockSpec((tm, D_MODEL), lambda i: (i, 0)),
                  pl.BlockSpec((tm, PACK_W), lambda i: (i, 0)),
                  pl.BlockSpec((tm, PACK_W), lambda i: (n_blk + i, 0)),
                  pl.BlockSpec((tm, ROUTER_W), lambda i: (i, 0)),
                  pl.BlockSpec((1, D_MODEL), lambda i: (0, 0))],
        out_specs=pl.BlockSpec((tm, D_MODEL), lambda i: (i, 0)),
        compiler_params=pltpu.CompilerParams(dimension_semantics=("arbitrary",), vmem_limit_bytes=VMEM_LIMIT),
        name="moe_combine",
    )(x1, y12, y12, cw, gf)


def _sparse_moe(x1, h2p, route, cw, w_gate, w_up, w_down, layer, gf, final_norm):
    n_tok = x1.shape[0]
    n_rows = 2 * n_tok + N_EXPERTS * ROW_TILE
    rank, counts = _dispatch(route)
    padded = (counts[:, 0].astype(jnp.int32) + ROW_TILE - 1) // ROW_TILE * ROW_TILE
    ends = jnp.cumsum(padded)
    starts = ends - padded
    tile_start = jnp.arange(n_rows // ROW_TILE, dtype=jnp.int32) * ROW_TILE
    tile_expert = jnp.minimum(jnp.sum(tile_start[:, None] >= ends[None, :], axis=1), N_EXPERTS - 1).astype(jnp.int32)
    n_used = (ends[N_EXPERTS - 1:] // ROW_TILE).astype(jnp.int32)

    pos = _positions(route, rank, starts.astype(jnp.float32).reshape(N_EXPERTS, 1))
    pos12 = jnp.concatenate([pos[0], pos[1]]).astype(jnp.int32)
    hs = _sc_scatter_rows(h2p, pos12, n_rows)
    ys = _experts(hs, tile_expert, n_used, w_gate, w_up, w_down, layer)
    y12 = _sc_gather_rows(ys, pos12)
    return _combine(x1, y12, cw, gf, final_norm)


def _split_bf16(w):
    hi = w.astype(jnp.bfloat16)
    lo = (w - hi.astype(jnp.float32)).astype(jnp.bfloat16)
    return jnp.stack([hi, lo])


def _qkv_col_scale():
    s = np.ones((QKV_W,), np.float32)
    for group, dim in enumerate((HEAD_DIM, DIFF_DIM, HEAD_DIM, HEAD_DIM)):
        s[3 * group * GROUP_W:(3 * group + 1) * GROUP_W] = LOG2E * dim ** -0.5
    return s


def _forget_weights(w_f, b_f):
    rows = jnp.zeros((F_ROWS, D_MODEL), jnp.float32)
    bias = jnp.zeros((F_ROWS, 1), jnp.float32)
    for h in range(HEADS_PER_MIXER):
        r = (h // 2) * 8 + (h % 2)
        rows = rows.at[r].set(w_f[:, h])
        bias = bias.at[r, 0].set(b_f[h])
    return _split_bf16(rows), bias


def kernel(x, rel_bias, ln1, w_in, w_out, lam_q1, lam_k1, lam_q2, lam_k2, subln_g, b_forget,
           ln2, w_group, b_group, w_expert, b_expert, w_gate, w_up, w_down, ln_f):
    batch, seq, _ = x.shape
    depth = ln1.shape[0]
    tabs = _bias_tables(rel_bias)
    xt = x.reshape(batch * seq, D_MODEL)
    gf = ln_f.reshape(1, D_MODEL)
    col_scale = _qkv_col_scale()
    for l in range(depth):
        lambda_init = 0.8 - 0.6 * math.exp(-0.3 * l)
        wf, bf = _forget_weights(w_in[l][:, QKV_W:], b_forget[l])
        w_qkv = (w_in[l][:, :QKV_W] * col_scale).astype(jnp.bfloat16)
        qkv, lf = _norm_proj(xt, ln1[l].reshape(1, D_MODEL), w_qkv, wf, bf, batch, seq)
        c = _cumsum(lf)

        o_a = _dilated(qkv, tabs, batch, seq)
        lamv = jnp.zeros((8, LANES), jnp.float32)
        lamv = lamv.at[0:4, 0:DIFF_DIM].set(jnp.stack([lam_q1[l], lam_k1[l], lam_q2[l], lam_k2[l]]))
        g_sub = jnp.tile(subln_g[l], 2).reshape(1, PAIR_W)
        o_b = _diff(qkv, tabs, lamv, g_sub, lambda_init, batch, seq)
        o_c = _moba(qkv, tabs, batch, seq)
        o_d = _fox(qkv, c, batch, seq)

        w_router = jnp.zeros((D_MODEL, ROUTER_W), jnp.float32)
        w_router = w_router.at[:, :N_EXPERTS].set(w_expert[l]).at[:, N_EXPERTS:N_EXPERTS + N_GROUPS].set(w_group[l])
        b_router = jnp.zeros((1, ROUTER_W), jnp.float32)
        b_router = b_router.at[0, :N_EXPERTS].set(b_expert[l].reshape(-1)).at[0, N_EXPERTS:N_EXPERTS + N_GROUPS].set(b_group[l])
        x1, h2p, route, cw = _outproj_router([o_a, o_b, o_c, o_d], w_out[l].astype(jnp.bfloat16), xt,
                                             ln2[l].reshape(1, D_MODEL), _split_bf16(w_router), b_router)
        xt = _sparse_moe(x1, h2p, route, cw, w_gate, w_up, w_down, l, gf, final_norm=(l == depth - 1))
    return xt.reshape(batch, seq, D_MODEL)
```

```python
import functools
import math
from typing import NamedTuple

import jax
import jax.numpy as jnp
import numpy as np
from jax import lax
from jax.experimental import pallas as pl
from jax.experimental.pallas import tpu as pltpu

D_MODEL = 1024
HEAD_DIM = 64
HEADS_PER_MIXER = 4
GROUP_W = HEADS_PER_MIXER * HEAD_DIM
QKV_W = 12 * GROUP_W
DIFF_DIM = HEAD_DIM // 2
MOBA_BLOCK = 256
MOBA_TOPK = 3
N_BUCKETS = 32
MAX_DISTANCE = 2048
N_BIAS_HEADS = 12
N_GROUPS = 4
EXPERTS_PER_GROUP = 8
N_EXPERTS = 32
D_EXPERT = 256
RMS_EPS = 1e-6
NEG = -1e30
LOG2E = math.log2(math.e)

LANES = 128
PAIR_W = 2 * HEAD_DIM
ACC_W = 2 * PAIR_W
TQ = 256
TK = 256
MAX_SUB = 4
STEP_TILES = 2
STEP_KEYS = STEP_TILES * TK
N_DIL_TILES = MAX_DISTANCE // TK + 1
FAR_TILE = 7
MIN_DIST = 1 - MAX_SUB
N_TAB_TILES = N_DIL_TILES + 1 - MIN_DIST
VMEM_LIMIT = 48 * 1024 * 1024

_NT = (((1,), (1,)), ((), ()))


def _t5_thresholds():
    d = np.arange(0, 4 * MAX_DISTANCE, dtype=np.int64)
    max_exact = N_BUCKETS // 2
    df = np.maximum(d, 1).astype(np.float32)
    large = max_exact + (np.log(df / np.float32(max_exact)) / np.float32(math.log(MAX_DISTANCE / max_exact))
                         * np.float32(N_BUCKETS - max_exact)).astype(np.int32)
    bucket = np.where(d < max_exact, d, np.minimum(large, N_BUCKETS - 1))
    return [int(np.argmax(bucket >= b)) for b in range(N_BUCKETS)]


_T5_THR = _t5_thresholds()
assert (FAR_TILE - 1) * TK + 1 >= _T5_THR[N_BUCKETS - 1]


def _lane_iota(shape):
    return lax.broadcasted_iota(jnp.int32, shape, len(shape) - 1)


def _bias_table_kernel(bias_ref, out_ref):
    h = pl.program_id(0)
    n = pl.program_id(1) + MIN_DIST
    i = lax.broadcasted_iota(jnp.int32, (TQ, TK), 0)
    j = lax.broadcasted_iota(jnp.int32, (TQ, TK), 1)
    d = n * TK + i - j
    is_dil = h < HEADS_PER_MIXER
    far = jnp.logical_and(jnp.logical_not(is_dil), n >= FAR_TILE)
    d_b = jnp.where(far, 4 * MAX_DISTANCE, d)
    val = jnp.full((TQ, TK), bias_ref[h, 0], jnp.float32)
    for b in range(1, N_BUCKETS):
        val = jnp.where(d_b >= _T5_THR[b], bias_ref[h, b], val)
    mult = ((d <= 128).astype(jnp.int32)
            + jnp.logical_and((d & 3) == 0, d <= 512).astype(jnp.int32)
            + jnp.logical_and((d & 15) == 0, d <= 2048).astype(jnp.int32))
    logm = jnp.where(mult == 3, math.log(3.0), jnp.where(mult == 2, math.log(2.0), 0.0))
    val = val + jnp.where(is_dil, logm, 0.0)
    valid = jnp.logical_and(d >= 0, jnp.logical_or(mult > 0, jnp.logical_not(is_dil)))
    out_ref[...] = jnp.where(valid, val * LOG2E, NEG)


def _bias_tables(rel_bias):
    bias_h = rel_bias.T.astype(jnp.float32)
    return pl.pallas_call(
        _bias_table_kernel,
        out_shape=jax.ShapeDtypeStruct((N_BIAS_HEADS, N_TAB_TILES, TQ, TK), jnp.float32),
        grid=(N_BIAS_HEADS, N_TAB_TILES),
        in_specs=[pl.BlockSpec(memory_space=pltpu.SMEM)],
        out_specs=pl.BlockSpec((None, None, TQ, TK), lambda h, n: (h, n, 0, 0)),
        name="bias_tables",
    )(bias_h)


PROJ_TM = 1024
PROJ_CW = 512
F_ROWS = 16


def _norm_proj_kernel(x_ref, g_ref, w_ref, wf_ref, bf_ref, qkv_ref, lf_ref):
    x = x_ref[...]
    ms = jnp.mean(x * x, axis=1, keepdims=True)
    h = x * lax.rsqrt(ms + RMS_EPS) * g_ref[...]
    hb = h.astype(jnp.bfloat16)
    for c in range(QKV_W // PROJ_CW):
        cols = slice(c * PROJ_CW, (c + 1) * PROJ_CW)
        qkv_ref[:, cols] = jnp.dot(hb, w_ref[:, cols], preferred_element_type=jnp.float32).astype(jnp.bfloat16)
    h_lo = (h - hb.astype(jnp.float32)).astype(jnp.bfloat16)
    wf_hi = wf_ref[0]
    wf_lo = wf_ref[1]
    z = (lax.dot_general(wf_hi, hb, _NT, preferred_element_type=jnp.float32)
         + lax.dot_general(wf_hi, h_lo, _NT, preferred_element_type=jnp.float32)
         + lax.dot_general(wf_lo, hb, _NT, preferred_element_type=jnp.float32))
    z = z + bf_ref[...]
    lf_ref[...] = jnp.minimum(z, 0.0) - jnp.log(1.0 + jnp.exp(-jnp.abs(z)))


def _norm_proj(x2d, g, w_qkv, wf, bf, batch, seq):
    n_tok = x2d.shape[0]
    tm = min(PROJ_TM, seq)
    per_b = seq // tm
    return pl.pallas_call(
        _norm_proj_kernel,
        out_shape=(jax.ShapeDtypeStruct((n_tok, QKV_W), jnp.bfloat16),
                   jax.ShapeDtypeStruct((batch * F_ROWS, seq), jnp.float32)),
        grid=(n_tok // tm,),
        in_specs=[pl.BlockSpec((tm, D_MODEL), lambda i: (i, 0)),
                  pl.BlockSpec((1, D_MODEL), lambda i: (0, 0)),
                  pl.BlockSpec((D_MODEL, QKV_W), lambda i: (0, 0)),
                  pl.BlockSpec((2, F_ROWS, D_MODEL), lambda i: (0, 0, 0)),
                  pl.BlockSpec((F_ROWS, 1), lambda i: (0, 0))],
        out_specs=(pl.BlockSpec((tm, QKV_W), lambda i: (i, 0)),
                   pl.BlockSpec((F_ROWS, tm), lambda i: (i // per_b, i % per_b))),
        compiler_params=pltpu.CompilerParams(dimension_semantics=("arbitrary",), vmem_limit_bytes=VMEM_LIMIT),
        name="norm_proj",
    )(x2d, g, w_qkv, wf, bf)


def _split3(x):
    x1 = x.astype(jnp.bfloat16)
    r1 = x - x1.astype(jnp.float32)
    x2 = r1.astype(jnp.bfloat16)
    x3 = (r1 - x2.astype(jnp.float32)).astype(jnp.bfloat16)
    return x1, x2, x3


def _cumsum_kernel(lf_ref, c_ref):
    rows, seq = lf_ref.shape
    r = lax.broadcasted_iota(jnp.int32, (LANES, LANES), 0)
    c = lax.broadcasted_iota(jnp.int32, (LANES, LANES), 1)
    upper = jnp.where(r <= c, 1.0, 0.0).astype(jnp.bfloat16)

    def body(i, carry):
        off = pl.multiple_of(i * LANES, LANES)
        x1, x2, x3 = _split3(lf_ref[:, pl.ds(off, LANES)])
        cs = (jnp.dot(x1, upper, preferred_element_type=jnp.float32)
              + jnp.dot(x2, upper, preferred_element_type=jnp.float32)
              + jnp.dot(x3, upper, preferred_element_type=jnp.float32)) + carry
        c_ref[:, pl.ds(off, LANES)] = cs * LOG2E
        return cs[:, LANES - 1:LANES]

    lax.fori_loop(0, seq // LANES, body, jnp.zeros((rows, 1), jnp.float32))


def _cumsum(lf):
    return pl.pallas_call(
        _cumsum_kernel,
        out_shape=jax.ShapeDtypeStruct(lf.shape, jnp.float32),
        name="forget_cumsum",
    )(lf)


class _Geom(NamedTuple):
    n_streams: int
    n_sub: int

    @property
    def q_rows(self):
        return self.n_sub * TQ

    def rows(self, sub, st):
        rb = sub * self.n_streams + st
        return slice(rb * TQ, (rb + 1) * TQ)

    def blocks(self):
        return [(sub, st, self.rows(sub, st)) for sub in range(self.n_sub) for st in range(self.n_streams)]

    def first_tile(self, blk, m):
        return self.n_sub * blk + self.n_sub - STEP_TILES * (m + 1)

    def dist(self, m, sub, w):
        return STEP_TILES * (m + 1) - self.n_sub + sub - w

    def n_steps(self, blk):
        return (self.n_sub // STEP_TILES) * (blk + 1)

    def own_steps(self):
        return self.n_sub // STEP_TILES


def _stack_masked_q(q, qs_ref, geom):
    lane = _lane_iota((TQ, PAIR_W))
    width = PAIR_W // geom.n_streams
    for sub, st, rows in geom.blocks():
        q_tile = q[sub * TQ:(sub + 1) * TQ]
        qs_ref[rows, 0:PAIR_W] = jnp.where(lane // width == st, q_tile, jnp.zeros_like(q_tile))


def _init_state(m_ref, acc_ref):
    m_ref[...] = jnp.full(m_ref.shape, NEG, jnp.float32)
    acc_ref[...] = jnp.zeros(acc_ref.shape, jnp.float32)


def _flash_step(rows, s, v_aug, m_ref, acc_ref):
    m_prev = m_ref[rows, :]
    m_new = jnp.maximum(m_prev, jnp.max(s, axis=1, keepdims=True))
    alpha = jnp.exp2(m_prev - m_new)
    p = jnp.concatenate([jnp.exp2(s[:, c * LANES:(c + 1) * LANES] - m_new) for c in range(s.shape[1] // LANES)],
                        axis=1).astype(v_aug.dtype)
    pv = jnp.dot(p, v_aug, preferred_element_type=jnp.float32)
    acc_ref[rows, :] = jnp.concatenate([alpha, alpha], axis=1) * acc_ref[rows, :] + pv
    m_ref[rows, :] = m_new


def _normalised(acc_ref, rows):
    return acc_ref[rows, 0:PAIR_W] / acc_ref[rows, PAIR_W:ACC_W]


def _table_bias(tab_ref, hh, geom, m, sub, cap):
    return jnp.concatenate([tab_ref[hh, jnp.minimum(geom.dist(m, sub, w), cap) - MIN_DIST]
                            for w in range(STEP_TILES)], axis=1)


def _step_rows(ref, geom, blk, m):
    k0 = pl.multiple_of(geom.first_tile(blk, m) * TK, STEP_KEYS)
    return ref[pl.ds(k0, STEP_KEYS), :]


def _make_attend(geom, blk, scratch, v_ref, keys, finish):
    qs_ref, m_ref, acc_ref, s_ref = scratch
    blocks = geom.blocks()
    last_step = geom.n_steps(blk) - 1

    def raw(rows, rhs):
        return lax.dot_general(qs_ref[rows, :], rhs, _NT, preferred_element_type=jnp.float32)

    def v_aug(m):
        v = _step_rows(v_ref, geom, blk, m)
        return jnp.concatenate([v, jnp.ones_like(v)], axis=1)

    def all_future(m, sub):
        return isinstance(m, int) and geom.dist(m, sub, 0) < 0

    def single(m):
        rhs = keys(m)
        va = v_aug(m)
        for sub, st, rows in blocks:
            if not all_future(m, sub):
                _flash_step(rows, finish(sub, st, raw(rows, rhs), m), va, m_ref, acc_ref)

    def step(m):
        va = v_aug(m)
        rhs_next = keys(jnp.minimum(m + 1, last_step))
        for sub, st, rows in blocks:
            if all_future(m, sub):
                s_ref[rows, :] = raw(rows, rhs_next)
                continue
            cur = s_ref[rows, :]
            s_ref[rows, :] = raw(rows, rhs_next)
            _flash_step(rows, finish(sub, st, cur, m), va, m_ref, acc_ref)

    def run(m0, count):
        peeled = [m0 + i for i in range(geom.own_steps())
                  if isinstance(m0, int) and any(all_future(m0 + i, sub) for sub in range(geom.n_sub))]

        @pl.when(count > 0)
        def _():
            rhs = keys(m0)
            for sub, _, rows in blocks:
                if not all_future(m0, sub):
                    s_ref[rows, :] = raw(rows, rhs)

        for m in peeled:
            pl.when(count > m - m0)(functools.partial(step, m))

        def body(i, carry):
            step(m0 + i)
            return carry

        lax.fori_loop(len(peeled), count, body, 0)

    return single, run


def _attn_call(kernel, group, qkv, extra_inputs, extra_specs, geom, batch, seq, *, q_width=PAIR_W,
               extra_scratch=(), name=None):
    assert seq % geom.q_rows == 0 and seq % STEP_KEYS == 0
    nq = seq // geom.q_rows
    qc, kc, vc = 6 * group, 6 * group + 2, 6 * group + 4
    rows = geom.n_sub * geom.n_streams * TQ
    return pl.pallas_call(
        functools.partial(kernel, geom=geom),
        out_shape=jax.ShapeDtypeStruct((batch * seq, GROUP_W), jnp.bfloat16),
        grid=(2, batch, nq),
        in_specs=[pl.BlockSpec((geom.q_rows, PAIR_W), lambda p, b, i: (b * nq + i, qc + p)),
                  pl.BlockSpec((seq, PAIR_W), lambda p, b, i: (b, kc + p)),
                  pl.BlockSpec((seq, PAIR_W), lambda p, b, i: (b, vc + p))] + list(extra_specs),
        out_specs=pl.BlockSpec((geom.q_rows, PAIR_W), lambda p, b, i: (b * nq + i, p)),
        scratch_shapes=[pltpu.VMEM((rows, q_width), jnp.bfloat16),
                        pltpu.VMEM((rows, LANES), jnp.float32),
                        pltpu.VMEM((rows, ACC_W), jnp.float32),
                        pltpu.VMEM((rows, STEP_KEYS), jnp.float32)
                        ] + list(extra_scratch),
        compiler_params=pltpu.CompilerParams(dimension_semantics=("arbitrary", "arbitrary", "arbitrary"),
                                             vmem_limit_bytes=VMEM_LIMIT),
        name=name,
    )(qkv, qkv, qkv, *extra_inputs)


def _tab_spec(first_head):
    return pl.BlockSpec((2, N_TAB_TILES, TQ, TK), lambda p, b, i: (first_head // 2 + p, 0, 0, 0))


def _merge_pair(o0, o1):
    return jnp.where(_lane_iota(o0.shape) < HEAD_DIM, o0, o1)


def _store_pair_outputs(o_ref, acc_ref, geom):
    for sub in range(geom.n_sub):
        o = _merge_pair(_normalised(acc_ref, geom.rows(sub, 0)), _normalised(acc_ref, geom.rows(sub, 1)))
        o_ref[sub * TQ:(sub + 1) * TQ, :] = o.astype(o_ref.dtype)


PAIR_GEOM = _Geom(n_streams=2, n_sub=MAX_SUB)
FOX_GEOM = _Geom(n_streams=2, n_sub=2)
DIFF_GEOM = _Geom(n_streams=4, n_sub=4)


def _dilated_kernel(q_ref, k_ref, v_ref, tab_ref, o_ref, qs_ref, m_ref, acc_ref, s_ref, *, geom):
    blk = pl.program_id(2)
    _stack_masked_q(q_ref[...], qs_ref, geom)
    _init_state(m_ref, acc_ref)
    _, run = _make_attend(
        geom, blk, (qs_ref, m_ref, acc_ref, s_ref), v_ref,
        keys=lambda m: _step_rows(k_ref, geom, blk, m),
        finish=lambda sub, st, raw, m: raw + _table_bias(tab_ref, st, geom, m, sub, N_DIL_TILES))
    reach = -(-(N_DIL_TILES - 1 + geom.n_sub) // STEP_TILES)
    run(0, jnp.minimum(geom.n_steps(blk), reach))
    _store_pair_outputs(o_ref, acc_ref, geom)


def _dilated(qkv, tabs, batch, seq):
    return _attn_call(_dilated_kernel, 0, qkv, [tabs], [_tab_spec(0)], PAIR_GEOM, batch, seq, name="dilated_attn")


def _diff_kernel(q_ref, k_ref, v_ref, tab_ref, lam_ref, g_ref, o_ref, qs_ref, m_ref, acc_ref, s_ref, *,
                 lambda_init, geom):
    blk = pl.program_id(2)
    _stack_masked_q(q_ref[...], qs_ref, geom)
    _init_state(m_ref, acc_ref)
    _, run = _make_attend(
        geom, blk, (qs_ref, m_ref, acc_ref, s_ref), v_ref,
        keys=lambda m: _step_rows(k_ref, geom, blk, m),
        finish=lambda sub, st, raw, m: raw + _table_bias(tab_ref, st // 2, geom, m, sub, FAR_TILE))
    run(0, geom.n_steps(blk))

    lamv = lam_ref[...]
    lam = (jnp.exp(jnp.sum(lamv[0:1] * lamv[1:2], axis=1, keepdims=True))
           - jnp.exp(jnp.sum(lamv[2:3] * lamv[3:4], axis=1, keepdims=True)) + lambda_init)
    first = _lane_iota((TQ, PAIR_W)) < HEAD_DIM
    for sub in range(geom.n_sub):
        part = [_normalised(acc_ref, geom.rows(sub, st)) for st in range(geom.n_streams)]
        o = _merge_pair(part[0] - lam * part[1], part[2] - lam * part[3])
        sq = o * o
        ms0 = jnp.sum(jnp.where(first, sq, 0.0), axis=1, keepdims=True)
        ms1 = jnp.sum(jnp.where(first, 0.0, sq), axis=1, keepdims=True)
        ms = jnp.where(first, ms0, ms1) * (1.0 / HEAD_DIM)
        y = (o * lax.rsqrt(ms + RMS_EPS) * g_ref[...]) * (1.0 - lambda_init)
        o_ref[sub * TQ:(sub + 1) * TQ, :] = y.astype(o_ref.dtype)


def _diff(qkv, tabs, lamv, g_sub, lambda_init, batch, seq):
    return _attn_call(functools.partial(_diff_kernel, lambda_init=lambda_init), 1, qkv, [tabs, lamv, g_sub],
                      [_tab_spec(4), pl.BlockSpec((8, LANES), lambda p, b, i: (0, 0)),
                       pl.BlockSpec((1, PAIR_W), lambda p, b, i: (0, 0))],
                      DIFF_GEOM, batch, seq, name="diff_attn")


def _moba_kernel(q_ref, k_ref, v_ref, tab_ref, o_ref, qs_ref, m_ref, acc_ref, s_ref, km_ref, *, geom):
    blk = pl.program_id(2)
    n_blk = k_ref.shape[0] // MOBA_BLOCK
    assert n_blk <= LANES and MOBA_BLOCK == TQ == TK

    @pl.when(blk == 0)
    def _():
        km_ref[...] = jnp.zeros(km_ref.shape, jnp.float32)
        for n in range(n_blk):
            km_ref[n:n + 1, :] = jnp.sum(k_ref[n * MOBA_BLOCK:(n + 1) * MOBA_BLOCK, :].astype(jnp.float32),
                                         axis=0, keepdims=True) * (1.0 / MOBA_BLOCK)

    _stack_masked_q(q_ref[...], qs_ref, geom)
    _init_state(m_ref, acc_ref)
    nb = -(-n_blk // 8) * 8
    km = km_ref[0:nb, :]
    km_hi = km.astype(jnp.bfloat16)
    km_lo = (km - km_hi.astype(jnp.float32)).astype(jnp.bfloat16)
    qs = qs_ref[:, 0:PAIR_W]
    gate = (lax.dot_general(km_hi, qs, _NT, preferred_element_type=jnp.float32)
            + lax.dot_general(km_lo, qs, _NT, preferred_element_type=jnp.float32))
    cand = lax.broadcasted_iota(jnp.int32, gate.shape, 0)
    cand_f = cand.astype(jnp.float32)
    own = geom.n_sub * blk + _lane_iota(gate.shape) // (geom.n_streams * TQ)
    g = jnp.where(cand < own, gate, NEG)
    sel = cand == own
    for _ in range(MOBA_TOPK):
        mx = jnp.max(g, axis=0, keepdims=True)
        first = jnp.min(jnp.where(g == mx, cand_f, 2.0 * LANES), axis=0, keepdims=True)
        pick = cand_f == first
        sel = jnp.logical_or(sel, jnp.logical_and(pick, mx > 0.5 * NEG))
        g = jnp.where(pick, -3e38, g)
    mask = jnp.where(sel, 0.0, NEG)
    if nb < LANES:
        mask = jnp.concatenate([mask, jnp.full((LANES - nb, mask.shape[1]), NEG, jnp.float32)], axis=0)
    qs_ref[:, PAIR_W:2 * PAIR_W] = mask.T.astype(jnp.bfloat16)

    def keys(m):
        k = _step_rows(k_ref, geom, blk, m)
        block_of_key = geom.first_tile(blk, m) + lax.broadcasted_iota(jnp.int32, (STEP_KEYS, LANES), 0) // TK
        onehot = jnp.where(_lane_iota((STEP_KEYS, LANES)) == block_of_key, 1.0, 0.0).astype(k.dtype)
        return jnp.concatenate([k, onehot], axis=1)

    _, run = _make_attend(
        geom, blk, (qs_ref, m_ref, acc_ref, s_ref), v_ref, keys=keys,
        finish=lambda sub, st, raw, m: raw + _table_bias(tab_ref, st, geom, m, sub, FAR_TILE))
    run(0, geom.n_steps(blk))
    _store_pair_outputs(o_ref, acc_ref, geom)


def _moba(qkv, tabs, batch, seq):
    return _attn_call(_moba_kernel, 2, qkv, [tabs], [_tab_spec(8)], PAIR_GEOM, batch, seq, q_width=2 * PAIR_W,
                      extra_scratch=[pltpu.VMEM((LANES, PAIR_W), jnp.float32)], name="moba_attn")


FOX_SKIP_BITS = 48.0


def _head_row_norms(x, hh):
    xf = x.astype(jnp.float32)
    sq = jnp.where(_lane_iota(xf.shape) // HEAD_DIM == hh, xf * xf, 0.0)
    return jnp.sqrt(jnp.max(jnp.sum(sq, axis=1, keepdims=True), axis=0, keepdims=True))


def _fox_kernel(q_ref, k_ref, v_ref, c_ref, o_ref, qs_ref, m_ref, acc_ref, s_ref, cq_ref, stat_ref, *, geom):
    blk = pl.program_id(2)
    n_tiles = k_ref.shape[0] // TK
    assert n_tiles <= LANES

    @pl.when(blk == 0)
    def _():
        stat_ref[...] = jnp.zeros(stat_ref.shape, jnp.float32)
        for j in range(n_tiles):
            k_tile = k_ref[j * TK:(j + 1) * TK, :]
            for hh in range(2):
                stat_ref[hh:hh + 1, j:j + 1] = _head_row_norms(k_tile, hh)
                stat_ref[2 + hh:3 + hh, j:j + 1] = jnp.min(c_ref[hh:hh + 1, j * TK:(j + 1) * TK], axis=1, keepdims=True)

    q = q_ref[...]
    _stack_masked_q(q, qs_ref, geom)
    _init_state(m_ref, acc_ref)
    q0 = pl.multiple_of(blk * geom.q_rows, geom.q_rows)
    row = lax.broadcasted_iota(jnp.int32, (TQ, TQ), 0)
    col = lax.broadcasted_iota(jnp.int32, (TQ, TQ), 1)
    for sub, hh, rows in geom.blocks():
        c_row = c_ref[hh:hh + 1, pl.ds(q0 + sub * TQ, TQ)]
        c_col = jnp.sum(jnp.where(row == col, c_row, 0.0), axis=1, keepdims=True)
        cq_ref[rows, :] = jnp.broadcast_to(c_col, (TQ, LANES))

    key_pos = lax.broadcasted_iota(jnp.int32, (TQ, STEP_KEYS), 1)
    query_pos = lax.broadcasted_iota(jnp.int32, (TQ, STEP_KEYS), 0)

    def finish(sub, hh, raw, m):
        k0 = pl.multiple_of(geom.first_tile(blk, m) * TK, STEP_KEYS)
        cq = cq_ref[geom.rows(sub, hh), :]
        s = raw + (jnp.concatenate([cq] * (STEP_KEYS // LANES), axis=1) - c_ref[hh:hh + 1, pl.ds(k0, STEP_KEYS)])
        if isinstance(m, int) and m < geom.own_steps():
            first_key = (geom.n_sub - STEP_TILES * (m + 1)) * TK
            s = jnp.where(key_pos + first_key <= query_pos + sub * TQ, s, NEG)
        return s

    single, run = _make_attend(geom, blk, (qs_ref, m_ref, acc_ref, s_ref), v_ref,
                               keys=lambda m: _step_rows(k_ref, geom, blk, m), finish=finish)
    for m in range(geom.own_steps()):
        single(m)

    tile = _lane_iota((1, LANES))
    needed = jnp.zeros((1, LANES), jnp.bool_)
    for hh in range(2):
        rows_h = [geom.rows(sub, hh) for sub in range(geom.n_sub)]
        m_min = functools.reduce(jnp.minimum, [jnp.min(m_ref[r, :], axis=0, keepdims=True) for r in rows_h])
        cq_max = functools.reduce(jnp.maximum, [jnp.max(cq_ref[r, :], axis=0, keepdims=True) for r in rows_h])
        bound = _head_row_norms(q, hh) * stat_ref[hh:hh + 1, :] + cq_max - stat_ref[2 + hh:3 + hh, :] + 1.0
        needed = jnp.logical_or(needed, bound >= m_min - FOX_SKIP_BITS)
    n_before = geom.n_sub * blk
    first_needed = jnp.min(jnp.where(jnp.logical_and(needed, tile < n_before), tile, n_before))
    run(geom.own_steps(), (n_before - first_needed + STEP_TILES - 1) // STEP_TILES)
    _store_pair_outputs(o_ref, acc_ref, geom)


def _fox(qkv, c, batch, seq):
    return _attn_call(_fox_kernel, 3, qkv, [c.reshape(batch, 2, 8, seq)],
                      [pl.BlockSpec((None, None, 8, seq), lambda p, b, i: (b, p, 0, 0))],
                      FOX_GEOM, batch, seq,
                      extra_scratch=[pltpu.VMEM((FOX_GEOM.n_sub * FOX_GEOM.n_streams * TQ, LANES), jnp.float32),
                                     pltpu.VMEM((8, LANES), jnp.float32)], name="fox_attn")


OUT_TM = 1024
ROUTER_W = LANES
ROUTE_ROWS = 8


PACK_W = D_MODEL // 2


def _pack_bf16_pair(lo, hi):
    lo_bits = lax.bitcast_convert_type(lo.astype(jnp.bfloat16).astype(jnp.float32), jnp.uint32)
    hi_bits = lax.bitcast_convert_type(hi.astype(jnp.bfloat16).astype(jnp.float32), jnp.uint32)
    word = lax.shift_right_logical(lo_bits, jnp.uint32(16)) | (hi_bits & jnp.uint32(0xFFFF0000))
    return lax.bitcast_convert_type(word, jnp.int32)


def _unpack_bf16_pair(word):
    bits = lax.bitcast_convert_type(word, jnp.uint32)
    lo = lax.bitcast_convert_type(lax.shift_left(bits, jnp.uint32(16)), jnp.float32)
    hi = lax.bitcast_convert_type(bits & jnp.uint32(0xFFFF0000), jnp.float32)
    return lo, hi


def _outproj_router_kernel(oa_ref, ob_ref, oc_ref, od_ref, wo_ref, x_ref, g_ref, wr_ref, br_ref,
                           x1_ref, h2_ref, route_ref, cw_ref):
    acc = x_ref[...]
    for m, o_ref in enumerate((oa_ref, ob_ref, oc_ref, od_ref)):
        acc = acc + jnp.dot(o_ref[...], wo_ref[m * GROUP_W:(m + 1) * GROUP_W, :], preferred_element_type=jnp.float32)
    x1_ref[...] = acc
    ms = jnp.mean(acc * acc, axis=1, keepdims=True)
    h = acc * lax.rsqrt(ms + RMS_EPS) * g_ref[...]
    hb = h.astype(jnp.bfloat16)
    h2_ref[...] = _pack_bf16_pair(h[:, 0:PACK_W], h[:, PACK_W:D_MODEL])
    h_lo = (h - hb.astype(jnp.float32)).astype(jnp.bfloat16)
    logits = (jnp.dot(hb, wr_ref[0], preferred_element_type=jnp.float32)
              + jnp.dot(h_lo, wr_ref[0], preferred_element_type=jnp.float32)
              + jnp.dot(hb, wr_ref[1], preferred_element_type=jnp.float32)) + br_ref[...]
    lane = _lane_iota(logits.shape).astype(jnp.float32)
    big = 2.0 * LANES
    gmask = jnp.logical_and(lane >= N_EXPERTS, lane < N_EXPERTS + N_GROUPS)
    gl = jnp.where(gmask, logits, NEG)
    gmax = jnp.max(gl, axis=1, keepdims=True)
    glane = jnp.min(jnp.where(gl == gmax, lane, big), axis=1, keepdims=True)
    gsum = jnp.sum(jnp.where(gmask, jnp.exp(gl - gmax), 0.0), axis=1, keepdims=True)
    g_w = 1.0 / gsum
    e0 = (glane - N_EXPERTS) * EXPERTS_PER_GROUP
    emask = jnp.logical_and(lane >= e0, lane < e0 + EXPERTS_PER_GROUP)
    el = jnp.where(emask, logits, NEG)
    v1 = jnp.max(el, axis=1, keepdims=True)
    i1 = jnp.min(jnp.where(el == v1, lane, big), axis=1, keepdims=True)
    el2 = jnp.where(lane == i1, NEG, el)
    v2 = jnp.max(el2, axis=1, keepdims=True)
    i2 = jnp.min(jnp.where(el2 == v2, lane, big), axis=1, keepdims=True)
    e2 = jnp.exp(v2 - v1)
    den = 1.0 + e2
    c1 = g_w / den
    c2 = g_w * e2 / den
    quad = jnp.where(lane == 0.0, i1, jnp.where(lane == 1.0, i2, jnp.where(lane == 2.0, c1,
                                                                             jnp.where(lane == 3.0, c2, 0.0))))
    cw_ref[...] = jnp.where(lane == 0.0, c1, jnp.where(lane == 1.0, c2, 0.0))
    route_ref[...] = quad.T[0:ROUTE_ROWS, :]


def _outproj_router(o_parts, w_out, x2d, g2, wr, br):
    n_tok = x2d.shape[0]
    tm = min(OUT_TM, n_tok)
    row = lambda i: (i, 0)
    fixed = lambda i: (0, 0)
    return pl.pallas_call(
        _outproj_router_kernel,
        out_shape=(jax.ShapeDtypeStruct((n_tok, D_MODEL), jnp.float32),
                   jax.ShapeDtypeStruct((n_tok, PACK_W), jnp.int32),
                   jax.ShapeDtypeStruct((ROUTE_ROWS, n_tok), jnp.float32),
                   jax.ShapeDtypeStruct((n_tok, ROUTER_W), jnp.float32)),
        grid=(n_tok // tm,),
        in_specs=[pl.BlockSpec((tm, GROUP_W), row)] * 4 + [
            pl.BlockSpec((D_MODEL, D_MODEL), fixed),
            pl.BlockSpec((tm, D_MODEL), row),
            pl.BlockSpec((1, D_MODEL), fixed),
            pl.BlockSpec((2, D_MODEL, ROUTER_W), lambda i: (0, 0, 0)),
            pl.BlockSpec((1, ROUTER_W), fixed)],
        out_specs=(pl.BlockSpec((tm, D_MODEL), row), pl.BlockSpec((tm, PACK_W), row),
                   pl.BlockSpec((ROUTE_ROWS, tm), lambda i: (0, i)), pl.BlockSpec((tm, ROUTER_W), row)),
        compiler_params=pltpu.CompilerParams(dimension_semantics=("arbitrary",), vmem_limit_bytes=VMEM_LIMIT),
        name="outproj_router",
    )(*o_parts, w_out, x2d, g2, wr, br)


ROW_TILE = 512
DISP_TM = 1024
SC_CORES = 2
SC_SUBCORES = 16
SC_WORKERS = SC_CORES * SC_SUBCORES
SC_CHUNK = 64


def _dispatch_kernel(route_ref, rank_ref, cnt_ref, tri_ref, carry_ref):
    step = pl.program_id(0)
    tm = route_ref.shape[1]

    @pl.when(step == 0)
    def _():
        r = lax.broadcasted_iota(jnp.int32, (tm, tm), 0)
        c = lax.broadcasted_iota(jnp.int32, (tm, tm), 1)
        tri_ref[...] = jnp.where(r <= c, 1.0, 0.0).astype(jnp.bfloat16)
        carry_ref[...] = jnp.zeros(carry_ref.shape, jnp.float32)

    e1 = route_ref[0:1, :]
    e2 = route_ref[1:2, :]
    expert = lax.broadcasted_iota(jnp.int32, (N_EXPERTS, tm), 0).astype(jnp.float32)
    hit1 = e1 == expert
    hit2 = e2 == expert
    hits = jnp.where(jnp.logical_or(hit1, hit2), 1.0, 0.0).astype(jnp.bfloat16)
    cum = jnp.dot(hits, tri_ref[...], preferred_element_type=jnp.float32) + carry_ref[...]
    rank1 = jnp.sum(jnp.where(hit1, cum - 1.0, 0.0), axis=0, keepdims=True)
    rank2 = jnp.sum(jnp.where(hit2, cum - 1.0, 0.0), axis=0, keepdims=True)
    row = lax.broadcasted_iota(jnp.int32, (ROUTE_ROWS, tm), 0)
    rank_ref[...] = jnp.where(row == 0, rank1, jnp.where(row == 1, rank2, 0.0))
    total = cum[:, tm - 1:tm]
    carry_ref[...] = total
    cnt_ref[...] = jnp.broadcast_to(total, cnt_ref.shape)


def _dispatch(route):
    n_tok = route.shape[1]
    tm = min(DISP_TM, n_tok)
    return pl.pallas_call(
        _dispatch_kernel,
        out_shape=(jax.ShapeDtypeStruct((ROUTE_ROWS, n_tok), jnp.float32),
                   jax.ShapeDtypeStruct((N_EXPERTS, LANES), jnp.float32)),
        grid=(n_tok // tm,),
        in_specs=[pl.BlockSpec((ROUTE_ROWS, tm), lambda i: (0, i))],
        out_specs=(pl.BlockSpec((ROUTE_ROWS, tm), lambda i: (0, i)),
                   pl.BlockSpec((N_EXPERTS, LANES), lambda i: (0, 0))),
        scratch_shapes=[pltpu.VMEM((tm, tm), jnp.bfloat16), pltpu.VMEM((N_EXPERTS, 1), jnp.float32)],
        compiler_params=pltpu.CompilerParams(dimension_semantics=("arbitrary",)),
        name="moe_ranks",
    )(route)


def _positions_kernel(route_ref, rank_ref, off_ref, pos_ref):
    tm = route_ref.shape[1]
    expert = lax.broadcasted_iota(jnp.int32, (N_EXPERTS, tm), 0).astype(jnp.float32)
    off = off_ref[...]
    start1 = jnp.sum(jnp.where(route_ref[0:1, :] == expert, off, 0.0), axis=0, keepdims=True)
    start2 = jnp.sum(jnp.where(route_ref[1:2, :] == expert, off, 0.0), axis=0, keepdims=True)
    row = lax.broadcasted_iota(jnp.int32, (ROUTE_ROWS, tm), 0)
    pos_ref[...] = jnp.where(row == 0, start1 + rank_ref[0:1, :], jnp.where(row == 1, start2 + rank_ref[1:2, :], 0.0))


def _positions(route, rank, seg_start):
    n_tok = route.shape[1]
    tm = min(DISP_TM, n_tok)
    blk = pl.BlockSpec((ROUTE_ROWS, tm), lambda i: (0, i))
    return pl.pallas_call(
        _positions_kernel,
        out_shape=jax.ShapeDtypeStruct((ROUTE_ROWS, n_tok), jnp.float32),
        grid=(n_tok // tm,),
        in_specs=[blk, blk, pl.BlockSpec((N_EXPERTS, 1), lambda i: (0, 0))],
        out_specs=blk,
        name="moe_positions",
    )(route, rank, seg_start)


def _sc_mesh():
    from jax.experimental.pallas import tpu_sc as plsc
    return plsc.VectorSubcoreMesh(core_axis_name="c", subcore_axis_name="s")


def _sc_worker_id():
    return lax.axis_index("s") * SC_CORES + lax.axis_index("c")


def _sc_scatter_rows(rows, pos, n_out):
    n_tok, width = rows.shape
    per_worker = n_tok // SC_WORKERS
    n_pairs = per_worker // (2 * SC_CHUNK)
    assert per_worker % (2 * SC_CHUNK) == 0
    buf = [pltpu.VMEM((SC_CHUNK,), jnp.int32), pltpu.VMEM((SC_CHUNK,), jnp.int32),
           pltpu.VMEM((SC_CHUNK, width), rows.dtype), pltpu.SemaphoreType.DMA, pltpu.SemaphoreType.DMA]

    @functools.partial(pl.kernel, mesh=_sc_mesh(), out_type=jax.ShapeDtypeStruct((n_out, width), rows.dtype),
                       scratch_types=buf + buf)
    def scatter(rows_hbm, pos_hbm, out_hbm, *scratch):
        base = _sc_worker_id() * per_worker
        buf_a, buf_b = scratch[:5], scratch[5:]

        def load(chunk, p1_v, p2_v, rows_v, sem_in, sem_out):
            off = base + chunk * SC_CHUNK
            pltpu.sync_copy(pos_hbm.at[pl.ds(off, SC_CHUNK)], p1_v)
            pltpu.sync_copy(pos_hbm.at[pl.ds(n_tok + off, SC_CHUNK)], p2_v)
            pltpu.async_copy(rows_hbm.at[pl.ds(off, SC_CHUNK)], rows_v, sem_in)

        def flush(chunk, p1_v, p2_v, rows_v, sem_in, sem_out):
            off = base + chunk * SC_CHUNK
            pltpu.make_async_copy(rows_hbm.at[pl.ds(off, SC_CHUNK)], rows_v, sem_in).wait()
            first = pltpu.async_copy(rows_v, out_hbm.at[p1_v], sem_out)
            second = pltpu.async_copy(rows_v, out_hbm.at[p2_v], sem_out)
            first.wait()
            second.wait()

        load(0, *buf_a)
        def body(j, carry):
            load(2 * j + 1, *buf_b)
            flush(2 * j, *buf_a)

            @pl.when(j + 1 < n_pairs)
            def _():
                load(2 * j + 2, *buf_a)

            flush(2 * j + 1, *buf_b)
            return carry

        lax.fori_loop(0, n_pairs, body, 0)

    return scatter(rows, pos)


def _sc_gather_rows(table, idx):
    n_idx = idx.shape[0]
    width = table.shape[1]
    per_worker = n_idx // SC_WORKERS
    n_pairs = per_worker // (2 * SC_CHUNK)
    assert per_worker % (2 * SC_CHUNK) == 0
    buf = [pltpu.VMEM((SC_CHUNK,), jnp.int32), pltpu.VMEM((SC_CHUNK, width), table.dtype), pltpu.SemaphoreType.DMA]

    @functools.partial(pl.kernel, mesh=_sc_mesh(), out_type=jax.ShapeDtypeStruct((n_idx, width), table.dtype),
                       scratch_types=buf + buf)
    def gather(table_hbm, idx_hbm, out_hbm, *scratch):
        base = _sc_worker_id() * per_worker
        buf_a, buf_b = scratch[:3], scratch[3:]

        def fetch(chunk, idx_v, rows_v, sem):
            pltpu.sync_copy(idx_hbm.at[pl.ds(base + chunk * SC_CHUNK, SC_CHUNK)], idx_v)
            pltpu.async_copy(table_hbm.at[idx_v], rows_v, sem)

        def drain(chunk, idx_v, rows_v, sem):
            pltpu.make_async_copy(table_hbm.at[idx_v], rows_v, sem).wait()
            pltpu.sync_copy(rows_v, out_hbm.at[pl.ds(base + chunk * SC_CHUNK, SC_CHUNK)])

        fetch(0, *buf_a)
        def body(j, carry):
            fetch(2 * j + 1, *buf_b)
            drain(2 * j, *buf_a)

            @pl.when(j + 1 < n_pairs)
            def _():
                fetch(2 * j + 2, *buf_a)

            drain(2 * j + 1, *buf_b)
            return carry

        lax.fori_loop(0, n_pairs, body, 0)

    return gather(table, idx)


def _experts_kernel(tile_expert_ref, n_used_ref, hs_ref, wg_ref, wu_ref, wd_ref, ys_ref):
    del tile_expert_ref

    @pl.when(pl.program_id(0) < n_used_ref[0])
    def _():
        lo, hi = _unpack_bf16_pair(hs_ref[...])
        lo = lo.astype(jnp.bfloat16)
        hi = hi.astype(jnp.bfloat16)
        bf = jnp.bfloat16
        gate = (jnp.dot(lo, wg_ref[0:PACK_W, :].astype(bf), preferred_element_type=jnp.float32)
                + jnp.dot(hi, wg_ref[PACK_W:D_MODEL, :].astype(bf), preferred_element_type=jnp.float32))
        up = (jnp.dot(lo, wu_ref[0:PACK_W, :].astype(bf), preferred_element_type=jnp.float32)
              + jnp.dot(hi, wu_ref[PACK_W:D_MODEL, :].astype(bf), preferred_element_type=jnp.float32))
        hid = ((gate * jax.nn.sigmoid(gate)) * up).astype(bf)
        y = jnp.dot(hid, wd_ref[...].astype(bf), preferred_element_type=jnp.float32)
        ys_ref[...] = _pack_bf16_pair(y[:, 0:PACK_W], y[:, PACK_W:D_MODEL])


def _experts(hs, tile_expert, n_used, wg, wu, wd, layer):
    n_rows = hs.shape[0]
    weights = lambda i, te, nu: (layer, te[i], 0, 0)
    return pl.pallas_call(
        _experts_kernel,
        out_shape=jax.ShapeDtypeStruct((n_rows, PACK_W), jnp.int32),
        grid_spec=pltpu.PrefetchScalarGridSpec(
            num_scalar_prefetch=2,
            grid=(n_rows // ROW_TILE,),
            in_specs=[pl.BlockSpec((ROW_TILE, PACK_W), lambda i, te, nu: (i, 0)),
                      pl.BlockSpec((None, None, D_MODEL, D_EXPERT), weights),
                      pl.BlockSpec((None, None, D_MODEL, D_EXPERT), weights),
                      pl.BlockSpec((None, None, D_EXPERT, D_MODEL), weights)],
            out_specs=pl.BlockSpec((ROW_TILE, PACK_W), lambda i, te, nu: (i, 0))),
        compiler_params=pltpu.CompilerParams(dimension_semantics=("arbitrary",), vmem_limit_bytes=VMEM_LIMIT),
        name="moe_experts",
    )(tile_expert, n_used, hs, wg, wu, wd)


COMB_TM = 1024


def _combine_kernel(x1_ref, y1_ref, y2_ref, cw_ref, gf_ref, out_ref, *, final_norm):
    c1 = cw_ref[:, 0:1]
    c2 = cw_ref[:, 1:2]
    lo1, hi1 = _unpack_bf16_pair(y1_ref[...])
    lo2, hi2 = _unpack_bf16_pair(y2_ref[...])
    y = jnp.concatenate([x1_ref[:, 0:PACK_W] + (c1 * lo1 + c2 * lo2),
                         x1_ref[:, PACK_W:D_MODEL] + (c1 * hi1 + c2 * hi2)], axis=1)
    if final_norm:
        ms = jnp.mean(y * y, axis=1, keepdims=True)
        y = y * lax.rsqrt(ms + RMS_EPS) * gf_ref[...]
    out_ref[...] = y


def _combine(x1, y12, cw, gf, final_norm):
    n_tok = x1.shape[0]
    tm = min(COMB_TM, n_tok)
    n_blk = n_tok // tm
    return pl.pallas_call(
        functools.partial(_combine_kernel, final_norm=final_norm),
        out_shape=jax.ShapeDtypeStruct((n_tok, D_MODEL), jnp.float32),
        grid=(n_blk,),
        in_specs=[pl.BlockSpec((tm, D_MODEL), lambda i: (i, 0)),
                  pl.BlockSpec((tm, PACK_W), lambda i: (i, 0)),
                  pl.BlockSpec((tm, PACK_W), lambda i: (n_blk + i, 0)),
                  pl.BlockSpec((tm, ROUTER_W), lambda i: (i, 0)),
                  pl.BlockSpec((1, D_MODEL), lambda i: (0, 0))],
        out_specs=pl.BlockSpec((tm, D_MODEL), lambda i: (i, 0)),
        compiler_params=pltpu.CompilerParams(dimension_semantics=("arbitrary",), vmem_limit_bytes=VMEM_LIMIT),
        name="moe_combine",
    )(x1, y12, y12, cw, gf)


def _sparse_moe(x1, h2p, route, cw, w_gate, w_up, w_down, layer, gf, final_norm):
    n_tok = x1.shape[0]
    n_rows = 2 * n_tok + N_EXPERTS * ROW_TILE
    rank, counts = _dispatch(route)
    padded = (counts[:, 0].astype(jnp.int32) + ROW_TILE - 1) // ROW_TILE * ROW_TILE
    ends = jnp.cumsum(padded)
    starts = ends - padded
    tile_start = jnp.arange(n_rows // ROW_TILE, dtype=jnp.int32) * ROW_TILE
    tile_expert = jnp.minimum(jnp.sum(tile_start[:, None] >= ends[None, :], axis=1), N_EXPERTS - 1).astype(jnp.int32)
    n_used = (ends[N_EXPERTS - 1:] // ROW_TILE).astype(jnp.int32)

    pos = _positions(route, rank, starts.astype(jnp.float32).reshape(N_EXPERTS, 1))
    pos12 = jnp.concatenate([pos[0], pos[1]]).astype(jnp.int32)
    hs = _sc_scatter_rows(h2p, pos12, n_rows)
    ys = _experts(hs, tile_expert, n_used, w_gate, w_up, w_down, layer)
    y12 = _sc_gather_rows(ys, pos12)
    return _combine(x1, y12, cw, gf, final_norm)


def _split_bf16(w):
    hi = w.astype(jnp.bfloat16)
    lo = (w - hi.astype(jnp.float32)).astype(jnp.bfloat16)
    return jnp.stack([hi, lo])


def _qkv_col_scale():
    s = np.ones((QKV_W,), np.float32)
    for group, dim in enumerate((HEAD_DIM, DIFF_DIM, HEAD_DIM, HEAD_DIM)):
        s[3 * group * GROUP_W:(3 * group + 1) * GROUP_W] = LOG2E * dim ** -0.5
    return s


def _forget_weights(w_f, b_f):
    rows = jnp.zeros((F_ROWS, D_MODEL), jnp.float32)
    bias = jnp.zeros((F_ROWS, 1), jnp.float32)
    for h in range(HEADS_PER_MIXER):
        r = (h // 2) * 8 + (h % 2)
        rows = rows.at[r].set(w_f[:, h])
        bias = bias.at[r, 0].set(b_f[h])
    return _split_bf16(rows), bias


def kernel(x, rel_bias, ln1, w_in, w_out, lam_q1, lam_k1, lam_q2, lam_k2, subln_g, b_forget,
           ln2, w_group, b_group, w_expert, b_expert, w_gate, w_up, w_down, ln_f):
    batch, seq, _ = x.shape
    depth = ln1.shape[0]
    tabs = _bias_tables(rel_bias)
    xt = x.reshape(batch * seq, D_MODEL)
    gf = ln_f.reshape(1, D_MODEL)
    col_scale = _qkv_col_scale()
    for l in range(depth):
        lambda_init = 0.8 - 0.6 * math.exp(-0.3 * l)
        wf, bf = _forget_weights(w_in[l][:, QKV_W:], b_forget[l])
        w_qkv = (w_in[l][:, :QKV_W] * col_scale).astype(jnp.bfloat16)
        qkv, lf = _norm_proj(xt, ln1[l].reshape(1, D_MODEL), w_qkv, wf, bf, batch, seq)
        c = _cumsum(lf)

        o_a = _dilated(qkv, tabs, batch, seq)
        lamv = jnp.zeros((8, LANES), jnp.float32)
        lamv = lamv.at[0:4, 0:DIFF_DIM].set(jnp.stack([lam_q1[l], lam_k1[l], lam_q2[l], lam_k2[l]]))
        g_sub = jnp.tile(subln_g[l], 2).reshape(1, PAIR_W)
        o_b = _diff(qkv, tabs, lamv, g_sub, lambda_init, batch, seq)
        o_c = _moba(qkv, tabs, batch, seq)
        o_d = _fox(qkv, c, batch, seq)

        w_router = jnp.zeros((D_MODEL, ROUTER_W), jnp.float32)
        w_router = w_router.at[:, :N_EXPERTS].set(w_expert[l]).at[:, N_EXPERTS:N_EXPERTS + N_GROUPS].set(w_group[l])
        b_router = jnp.zeros((1, ROUTER_W), jnp.float32)
        b_router = b_router.at[0, :N_EXPERTS].set(b_expert[l].reshape(-1)).at[0, N_EXPERTS:N_EXPERTS + N_GROUPS].set(b_group[l])
        x1, h2p, route, cw = _outproj_router([o_a, o_b, o_c, o_d], w_out[l].astype(jnp.bfloat16), xt,
                                             ln2[l].reshape(1, D_MODEL), _split_bf16(w_router), b_router)
        xt = _sparse_moe(x1, h2p, route, cw, w_gate, w_up, w_down, l, gf, final_norm=(l == depth - 1))
    return xt.reshape(batch, seq, D_MODEL)
```

```python
import functools
import math
from typing import NamedTuple

import jax
import jax.numpy as jnp
import numpy as np
from jax import lax
from jax.experimental import pallas as pl
from jax.experimental.pallas import tpu as pltpu

D_MODEL = 1024
HEAD_DIM = 64
HEADS_PER_MIXER = 4
GROUP_W = HEADS_PER_MIXER * HEAD_DIM
QKV_W = 12 * GROUP_W
DIFF_DIM = HEAD_DIM // 2
MOBA_BLOCK = 256
MOBA_TOPK = 3
N_BUCKETS = 32
MAX_DISTANCE = 2048
N_BIAS_HEADS = 12
N_GROUPS = 4
EXPERTS_PER_GROUP = 8
N_EXPERTS = 32
D_EXPERT = 256
RMS_EPS = 1e-6
NEG = -1e30
LOG2E = math.log2(math.e)

LANES = 128
PAIR_W = 2 * HEAD_DIM
ACC_W = 2 * PAIR_W
TQ = 256
TK = 256
MAX_SUB = 8
STEP_TILES = 2
STEP_KEYS = STEP_TILES * TK
N_DIL_TILES = MAX_DISTANCE // TK + 1
FAR_TILE = 7
MIN_DIST = 1 - MAX_SUB
N_TAB_TILES = N_DIL_TILES + 1 - MIN_DIST
VMEM_LIMIT = 48 * 1024 * 1024

_NT = (((1,), (1,)), ((), ()))


def _t5_thresholds():
    d = np.arange(0, 4 * MAX_DISTANCE, dtype=np.int64)
    max_exact = N_BUCKETS // 2
    df = np.maximum(d, 1).astype(np.float32)
    large = max_exact + (np.log(df / np.float32(max_exact)) / np.float32(math.log(MAX_DISTANCE / max_exact))
                         * np.float32(N_BUCKETS - max_exact)).astype(np.int32)
    bucket = np.where(d < max_exact, d, np.minimum(large, N_BUCKETS - 1))
    return [int(np.argmax(bucket >= b)) for b in range(N_BUCKETS)]


_T5_THR = _t5_thresholds()
assert (FAR_TILE - 1) * TK + 1 >= _T5_THR[N_BUCKETS - 1]


def _lane_iota(shape):
    return lax.broadcasted_iota(jnp.int32, shape, len(shape) - 1)


def _bias_table_kernel(bias_ref, out_ref):
    h = pl.program_id(0)
    n = pl.program_id(1) + MIN_DIST
    i = lax.broadcasted_iota(jnp.int32, (TQ, TK), 0)
    j = lax.broadcasted_iota(jnp.int32, (TQ, TK), 1)
    d = n * TK + i - j
    is_dil = h < HEADS_PER_MIXER

    def store(val):
        mult = ((d <= 128).astype(jnp.int32)
                + jnp.logical_and((d & 3) == 0, d <= 512).astype(jnp.int32)
                + jnp.logical_and((d & 15) == 0, d <= 2048).astype(jnp.int32))
        logm = jnp.where(mult == 3, math.log(3.0), jnp.where(mult == 2, math.log(2.0), 0.0))
        val = val + jnp.where(is_dil, logm, 0.0)
        valid = jnp.logical_and(d >= 0, jnp.logical_or(mult > 0, jnp.logical_not(is_dil)))
        out_ref[...] = jnp.where(valid, val * LOG2E, NEG)

    @pl.when(n < 0)
    def _():
        out_ref[...] = jnp.full((TQ, TK), NEG, jnp.float32)

    @pl.when(n >= FAR_TILE)
    def _():
        store(jnp.full((TQ, TK), bias_ref[h, N_BUCKETS - 1], jnp.float32))

    @pl.when(jnp.logical_and(n >= 0, n < FAR_TILE))
    def _():
        val = jnp.full((TQ, TK), bias_ref[h, 0], jnp.float32)
        for b in range(1, N_BUCKETS):
            val = jnp.where(d >= _T5_THR[b], bias_ref[h, b], val)
        store(val)


def _bias_tables(rel_bias):
    bias_h = rel_bias.T.astype(jnp.float32)
    return pl.pallas_call(
        _bias_table_kernel,
        out_shape=jax.ShapeDtypeStruct((N_BIAS_HEADS, N_TAB_TILES, TQ, TK), jnp.float32),
        grid=(N_BIAS_HEADS, N_TAB_TILES),
        in_specs=[pl.BlockSpec(memory_space=pltpu.SMEM)],
        out_specs=pl.BlockSpec((None, None, TQ, TK), lambda h, n: (h, n, 0, 0)),
        name="bias_tables",
    )(bias_h)


PROJ_TM = 1024
PROJ_CW = 512
F_ROWS = 16


def _norm_proj_kernel(x_ref, g_ref, w_ref, wf_ref, bf_ref, qkv_ref, lf_ref):
    x = x_ref[...]
    ms = jnp.mean(x * x, axis=1, keepdims=True)
    h = x * lax.rsqrt(ms + RMS_EPS) * g_ref[...]
    hb = h.astype(jnp.bfloat16)
    for c in range(QKV_W // PROJ_CW):
        cols = slice(c * PROJ_CW, (c + 1) * PROJ_CW)
        qkv_ref[:, cols] = jnp.dot(hb, w_ref[:, cols], preferred_element_type=jnp.float32).astype(jnp.bfloat16)
    h_lo = (h - hb.astype(jnp.float32)).astype(jnp.bfloat16)
    wf_hi = wf_ref[0]
    wf_lo = wf_ref[1]
    z = (lax.dot_general(wf_hi, hb, _NT, preferred_element_type=jnp.float32)
         + lax.dot_general(wf_hi, h_lo, _NT, preferred_element_type=jnp.float32)
         + lax.dot_general(wf_lo, hb, _NT, preferred_element_type=jnp.float32))
    z = z + bf_ref[...]
    lf_ref[...] = jnp.minimum(z, 0.0) - jnp.log(1.0 + jnp.exp(-jnp.abs(z)))


def _norm_proj(x2d, g, w_qkv, wf, bf, batch, seq):
    n_tok = x2d.shape[0]
    tm = min(PROJ_TM, seq)
    per_b = seq // tm
    return pl.pallas_call(
        _norm_proj_kernel,
        out_shape=(jax.ShapeDtypeStruct((n_tok, QKV_W), jnp.bfloat16),
                   jax.ShapeDtypeStruct((batch * F_ROWS, seq), jnp.float32)),
        grid=(n_tok // tm,),
        in_specs=[pl.BlockSpec((tm, D_MODEL), lambda i: (i, 0)),
                  pl.BlockSpec((1, D_MODEL), lambda i: (0, 0)),
                  pl.BlockSpec((D_MODEL, QKV_W), lambda i: (0, 0)),
                  pl.BlockSpec((2, F_ROWS, D_MODEL), lambda i: (0, 0, 0)),
                  pl.BlockSpec((F_ROWS, 1), lambda i: (0, 0))],
        out_specs=(pl.BlockSpec((tm, QKV_W), lambda i: (i, 0)),
                   pl.BlockSpec((F_ROWS, tm), lambda i: (i // per_b, i % per_b))),
        compiler_params=pltpu.CompilerParams(dimension_semantics=("arbitrary",), vmem_limit_bytes=VMEM_LIMIT),
        name="norm_proj",
    )(x2d, g, w_qkv, wf, bf)


def _split3(x):
    x1 = x.astype(jnp.bfloat16)
    r1 = x - x1.astype(jnp.float32)
    x2 = r1.astype(jnp.bfloat16)
    x3 = (r1 - x2.astype(jnp.float32)).astype(jnp.bfloat16)
    return x1, x2, x3


def _cumsum_kernel(lf_ref, c_ref):
    rows, seq = lf_ref.shape
    r = lax.broadcasted_iota(jnp.int32, (LANES, LANES), 0)
    c = lax.broadcasted_iota(jnp.int32, (LANES, LANES), 1)
    upper = jnp.where(r <= c, 1.0, 0.0).astype(jnp.bfloat16)

    def body(i, carry):
        off = pl.multiple_of(i * LANES, LANES)
        x1, x2, x3 = _split3(lf_ref[:, pl.ds(off, LANES)])
        cs = (jnp.dot(x1, upper, preferred_element_type=jnp.float32)
              + jnp.dot(x2, upper, preferred_element_type=jnp.float32)
              + jnp.dot(x3, upper, preferred_element_type=jnp.float32)) + carry
        c_ref[:, pl.ds(off, LANES)] = cs * LOG2E
        return cs[:, LANES - 1:LANES]

    lax.fori_loop(0, seq // LANES, body, jnp.zeros((rows, 1), jnp.float32))


def _cumsum(lf):
    return pl.pallas_call(
        _cumsum_kernel,
        out_shape=jax.ShapeDtypeStruct(lf.shape, jnp.float32),
        name="forget_cumsum",
    )(lf)


class _Geom(NamedTuple):
    n_streams: int
    n_sub: int

    @property
    def q_rows(self):
        return self.n_sub * TQ

    def rows(self, sub, st):
        rb = sub * self.n_streams + st
        return slice(rb * TQ, (rb + 1) * TQ)

    def blocks(self):
        return [(sub, st, self.rows(sub, st)) for sub in range(self.n_sub) for st in range(self.n_streams)]

    def first_tile(self, blk, m):
        return self.n_sub * blk + self.n_sub - STEP_TILES * (m + 1)

    def dist(self, m, sub, w):
        return STEP_TILES * (m + 1) - self.n_sub + sub - w

    def n_steps(self, blk):
        return (self.n_sub // STEP_TILES) * (blk + 1)

    def own_steps(self):
        return self.n_sub // STEP_TILES


def _stack_masked_q(q, qs_ref, geom):
    lane = _lane_iota((TQ, PAIR_W))
    width = PAIR_W // geom.n_streams
    for sub, st, rows in geom.blocks():
        q_tile = q[sub * TQ:(sub + 1) * TQ]
        qs_ref[rows, 0:PAIR_W] = jnp.where(lane // width == st, q_tile, jnp.zeros_like(q_tile))


def _init_state(m_ref, acc_ref):
    m_ref[...] = jnp.full(m_ref.shape, NEG, jnp.float32)
    acc_ref[...] = jnp.zeros(acc_ref.shape, jnp.float32)


def _flash_step(rows, s, v_aug, m_ref, acc_ref):
    m_prev = m_ref[rows, :]
    m_new = jnp.maximum(m_prev, jnp.max(s, axis=1, keepdims=True))
    alpha = jnp.exp2(m_prev - m_new)
    p = jnp.concatenate([jnp.exp2(s[:, c * LANES:(c + 1) * LANES] - m_new) for c in range(s.shape[1] // LANES)],
                        axis=1).astype(v_aug.dtype)
    pv = jnp.dot(p, v_aug, preferred_element_type=jnp.float32)
    acc_ref[rows, :] = jnp.concatenate([alpha, alpha], axis=1) * acc_ref[rows, :] + pv
    m_ref[rows, :] = m_new


def _normalised(acc_ref, rows):
    return acc_ref[rows, 0:PAIR_W] / acc_ref[rows, PAIR_W:ACC_W]


def _table_bias(tab_ref, hh, geom, m, sub, cap):
    return jnp.concatenate([tab_ref[hh, jnp.minimum(geom.dist(m, sub, w), cap) - MIN_DIST]
                            for w in range(STEP_TILES)], axis=1)


def _step_rows(ref, geom, blk, m):
    k0 = pl.multiple_of(geom.first_tile(blk, m) * TK, STEP_KEYS)
    return ref[pl.ds(k0, STEP_KEYS), :]


def _make_attend(geom, blk, scratch, v_ref, keys, finish):
    qs_ref, m_ref, acc_ref, s_ref = scratch
    blocks = geom.blocks()
    last_step = geom.n_steps(blk) - 1

    def raw(rows, rhs):
        return lax.dot_general(qs_ref[rows, :], rhs, _NT, preferred_element_type=jnp.float32)

    def v_aug(m):
        v = _step_rows(v_ref, geom, blk, m)
        return jnp.concatenate([v, jnp.ones_like(v)], axis=1)

    def all_future(m, sub):
        return isinstance(m, int) and geom.dist(m, sub, 0) < 0

    def single(m):
        rhs = keys(m)
        va = v_aug(m)
        for sub, st, rows in blocks:
            if not all_future(m, sub):
                _flash_step(rows, finish(sub, st, raw(rows, rhs), m), va, m_ref, acc_ref)

    def step(m):
        va = v_aug(m)
        rhs_next = keys(jnp.minimum(m + 1, last_step))
        for sub, st, rows in blocks:
            if all_future(m, sub):
                s_ref[rows, :] = raw(rows, rhs_next)
                continue
            cur = s_ref[rows, :]
            s_ref[rows, :] = raw(rows, rhs_next)
            _flash_step(rows, finish(sub, st, cur, m), va, m_ref, acc_ref)

    def run(m0, count):
        peeled = [m0 + i for i in range(geom.own_steps())
                  if isinstance(m0, int) and any(all_future(m0 + i, sub) for sub in range(geom.n_sub))]

        @pl.when(count > 0)
        def _():
            rhs = keys(m0)
            for sub, _, rows in blocks:
                if not all_future(m0, sub):
                    s_ref[rows, :] = raw(rows, rhs)

        for m in peeled:
            pl.when(count > m - m0)(functools.partial(step, m))

        def body(i, carry):
            step(m0 + i)
            return carry

        lax.fori_loop(len(peeled), count, body, 0)

    return single, run


def _attn_call(kernel, group, qkv, extra_inputs, extra_specs, geom, batch, seq, *, q_width=PAIR_W,
               extra_scratch=(), name=None):
    assert seq % geom.q_rows == 0 and seq % STEP_KEYS == 0
    nq = seq // geom.q_rows
    qc, kc, vc = 6 * group, 6 * group + 2, 6 * group + 4
    rows = geom.n_sub * geom.n_streams * TQ
    return pl.pallas_call(
        functools.partial(kernel, geom=geom),
        out_shape=jax.ShapeDtypeStruct((batch * seq, GROUP_W), jnp.bfloat16),
        grid=(2, batch, nq),
        in_specs=[pl.BlockSpec((geom.q_rows, PAIR_W), lambda p, b, i: (b * nq + i, qc + p)),
                  pl.BlockSpec((seq, PAIR_W), lambda p, b, i: (b, kc + p)),
                  pl.BlockSpec((seq, PAIR_W), lambda p, b, i: (b, vc + p))] + list(extra_specs),
        out_specs=pl.BlockSpec((geom.q_rows, PAIR_W), lambda p, b, i: (b * nq + i, p)),
        scratch_shapes=[pltpu.VMEM((rows, q_width), jnp.bfloat16),
                        pltpu.VMEM((rows, LANES), jnp.float32),
                        pltpu.VMEM((rows, ACC_W), jnp.float32),
                        pltpu.VMEM((rows, STEP_KEYS), jnp.float32)
                        ] + list(extra_scratch),
        compiler_params=pltpu.CompilerParams(dimension_semantics=("arbitrary", "arbitrary", "arbitrary"),
                                             vmem_limit_bytes=VMEM_LIMIT),
        name=name,
    )(qkv, qkv, qkv, *extra_inputs)


def _tab_spec(first_head):
    return pl.BlockSpec((2, N_TAB_TILES, TQ, TK), lambda p, b, i: (first_head // 2 + p, 0, 0, 0))


def _merge_pair(o0, o1):
    return jnp.where(_lane_iota(o0.shape) < HEAD_DIM, o0, o1)


def _store_pair_outputs(o_ref, acc_ref, geom):
    for sub in range(geom.n_sub):
        o = _merge_pair(_normalised(acc_ref, geom.rows(sub, 0)), _normalised(acc_ref, geom.rows(sub, 1)))
        o_ref[sub * TQ:(sub + 1) * TQ, :] = o.astype(o_ref.dtype)


PAIR_GEOM = _Geom(n_streams=2, n_sub=4)
MOBA_GEOM = _Geom(n_streams=2, n_sub=MAX_SUB)
FOX_GEOM = _Geom(n_streams=2, n_sub=2)
DIFF_GEOM = _Geom(n_streams=4, n_sub=4)


def _dilated_kernel(q_ref, k_ref, v_ref, tab_ref, o_ref, qs_ref, m_ref, acc_ref, s_ref, *, geom):
    blk = pl.program_id(2)
    _stack_masked_q(q_ref[...], qs_ref, geom)
    _init_state(m_ref, acc_ref)
    _, run = _make_attend(
        geom, blk, (qs_ref, m_ref, acc_ref, s_ref), v_ref,
        keys=lambda m: _step_rows(k_ref, geom, blk, m),
        finish=lambda sub, st, raw, m: raw + _table_bias(tab_ref, st, geom, m, sub, N_DIL_TILES))
    reach = -(-(N_DIL_TILES - 1 + geom.n_sub) // STEP_TILES)
    run(0, jnp.minimum(geom.n_steps(blk), reach))
    _store_pair_outputs(o_ref, acc_ref, geom)


def _dilated(qkv, tabs, batch, seq):
    return _attn_call(_dilated_kernel, 0, qkv, [tabs], [_tab_spec(0)], PAIR_GEOM, batch, seq, name="dilated_attn")


def _diff_kernel(q_ref, k_ref, v_ref, tab_ref, lam_ref, g_ref, o_ref, qs_ref, m_ref, acc_ref, s_ref, *,
                 lambda_init, geom):
    blk = pl.program_id(2)
    _stack_masked_q(q_ref[...], qs_ref, geom)
    _init_state(m_ref, acc_ref)
    _, run = _make_attend(
        geom, blk, (qs_ref, m_ref, acc_ref, s_ref), v_ref,
        keys=lambda m: _step_rows(k_ref, geom, blk, m),
        finish=lambda sub, st, raw, m: raw + _table_bias(tab_ref, st // 2, geom, m, sub, FAR_TILE))
    run(0, geom.n_steps(blk))

    lamv = lam_ref[...]
    lam = (jnp.exp(jnp.sum(lamv[0:1] * lamv[1:2], axis=1, keepdims=True))
           - jnp.exp(jnp.sum(lamv[2:3] * lamv[3:4], axis=1, keepdims=True)) + lambda_init)
    first = _lane_iota((TQ, PAIR_W)) < HEAD_DIM
    for sub in range(geom.n_sub):
        part = [_normalised(acc_ref, geom.rows(sub, st)) for st in range(geom.n_streams)]
        o = _merge_pair(part[0] - lam * part[1], part[2] - lam * part[3])
        sq = o * o
        ms0 = jnp.sum(jnp.where(first, sq, 0.0), axis=1, keepdims=True)
        ms1 = jnp.sum(jnp.where(first, 0.0, sq), axis=1, keepdims=True)
        ms = jnp.where(first, ms0, ms1) * (1.0 / HEAD_DIM)
        y = (o * lax.rsqrt(ms + RMS_EPS) * g_ref[...]) * (1.0 - lambda_init)
        o_ref[sub * TQ:(sub + 1) * TQ, :] = y.astype(o_ref.dtype)


def _diff(qkv, tabs, lamv, g_sub, lambda_init, batch, seq):
    return _attn_call(functools.partial(_diff_kernel, lambda_init=lambda_init), 1, qkv, [tabs, lamv, g_sub],
                      [_tab_spec(4), pl.BlockSpec((8, LANES), lambda p, b, i: (0, 0)),
                       pl.BlockSpec((1, PAIR_W), lambda p, b, i: (0, 0))],
                      DIFF_GEOM, batch, seq, name="diff_attn")


def _moba_kernel(q_ref, k_ref, v_ref, tab_ref, o_ref, qs_ref, m_ref, acc_ref, s_ref, km_ref, *, geom):
    blk = pl.program_id(2)
    n_blk = k_ref.shape[0] // MOBA_BLOCK
    assert n_blk <= LANES and MOBA_BLOCK == TQ == TK

    @pl.when(blk == 0)
    def _():
        km_ref[...] = jnp.zeros(km_ref.shape, jnp.float32)
        for n in range(n_blk):
            km_ref[n:n + 1, :] = jnp.sum(k_ref[n * MOBA_BLOCK:(n + 1) * MOBA_BLOCK, :].astype(jnp.float32),
                                         axis=0, keepdims=True) * (1.0 / MOBA_BLOCK)

    _stack_masked_q(q_ref[...], qs_ref, geom)
    _init_state(m_ref, acc_ref)
    nb = -(-n_blk // 8) * 8
    km = km_ref[0:nb, :]
    km_hi = km.astype(jnp.bfloat16)
    km_lo = (km - km_hi.astype(jnp.float32)).astype(jnp.bfloat16)
    qs = qs_ref[:, 0:PAIR_W]
    gate = (lax.dot_general(km_hi, qs, _NT, preferred_element_type=jnp.float32)
            + lax.dot_general(km_lo, qs, _NT, preferred_element_type=jnp.float32))
    cand = lax.broadcasted_iota(jnp.int32, gate.shape, 0)
    cand_f = cand.astype(jnp.float32)
    own = geom.n_sub * blk + _lane_iota(gate.shape) // (geom.n_streams * TQ)
    g = jnp.where(cand < own, gate, NEG)
    sel = cand == own
    for _ in range(MOBA_TOPK):
        mx = jnp.max(g, axis=0, keepdims=True)
        first = jnp.min(jnp.where(g == mx, cand_f, 2.0 * LANES), axis=0, keepdims=True)
        pick = cand_f == first
        sel = jnp.logical_or(sel, jnp.logical_and(pick, mx > 0.5 * NEG))
        g = jnp.where(pick, -3e38, g)
    mask = jnp.where(sel, 0.0, NEG)
    if nb < LANES:
        mask = jnp.concatenate([mask, jnp.full((LANES - nb, mask.shape[1]), NEG, jnp.float32)], axis=0)
    qs_ref[:, PAIR_W:2 * PAIR_W] = mask.T.astype(jnp.bfloat16)

    def keys(m):
        k = _step_rows(k_ref, geom, blk, m)
        block_of_key = geom.first_tile(blk, m) + lax.broadcasted_iota(jnp.int32, (STEP_KEYS, LANES), 0) // TK
        onehot = jnp.where(_lane_iota((STEP_KEYS, LANES)) == block_of_key, 1.0, 0.0).astype(k.dtype)
        return jnp.concatenate([k, onehot], axis=1)

    _, run = _make_attend(
        geom, blk, (qs_ref, m_ref, acc_ref, s_ref), v_ref, keys=keys,
        finish=lambda sub, st, raw, m: raw + _table_bias(tab_ref, st, geom, m, sub, FAR_TILE))
    run(0, geom.n_steps(blk))
    _store_pair_outputs(o_ref, acc_ref, geom)


def _moba(qkv, tabs, batch, seq):
    return _attn_call(_moba_kernel, 2, qkv, [tabs], [_tab_spec(8)], MOBA_GEOM, batch, seq, q_width=2 * PAIR_W,
                      extra_scratch=[pltpu.VMEM((LANES, PAIR_W), jnp.float32)], name="moba_attn")


FOX_SKIP_BITS = 48.0


def _head_row_norms(x, hh):
    xf = x.astype(jnp.float32)
    sq = jnp.where(_lane_iota(xf.shape) // HEAD_DIM == hh, xf * xf, 0.0)
    return jnp.sqrt(jnp.max(jnp.sum(sq, axis=1, keepdims=True), axis=0, keepdims=True))


def _fox_kernel(q_ref, k_ref, v_ref, c_ref, o_ref, qs_ref, m_ref, acc_ref, s_ref, cq_ref, stat_ref, *, geom):
    blk = pl.program_id(2)
    n_tiles = k_ref.shape[0] // TK
    assert n_tiles <= LANES

    @pl.when(blk == 0)
    def _():
        stat_ref[...] = jnp.zeros(stat_ref.shape, jnp.float32)
        for j in range(n_tiles):
            k_tile = k_ref[j * TK:(j + 1) * TK, :]
            for hh in range(2):
                stat_ref[hh:hh + 1, j:j + 1] = _head_row_norms(k_tile, hh)
                stat_ref[2 + hh:3 + hh, j:j + 1] = jnp.min(c_ref[hh:hh + 1, j * TK:(j + 1) * TK], axis=1, keepdims=True)

    q = q_ref[...]
    _stack_masked_q(q, qs_ref, geom)
    _init_state(m_ref, acc_ref)
    q0 = pl.multiple_of(blk * geom.q_rows, geom.q_rows)
    row = lax.broadcasted_iota(jnp.int32, (TQ, TQ), 0)
    col = lax.broadcasted_iota(jnp.int32, (TQ, TQ), 1)
    for sub, hh, rows in geom.blocks():
        c_row = c_ref[hh:hh + 1, pl.ds(q0 + sub * TQ, TQ)]
        c_col = jnp.sum(jnp.where(row == col, c_row, 0.0), axis=1, keepdims=True)
        cq_ref[rows, :] = jnp.broadcast_to(c_col, (TQ, LANES))

    key_pos = lax.broadcasted_iota(jnp.int32, (TQ, STEP_KEYS), 1)
    query_pos = lax.broadcasted_iota(jnp.int32, (TQ, STEP_KEYS), 0)

    def finish(sub, hh, raw, m):
        k0 = pl.multiple_of(geom.first_tile(blk, m) * TK, STEP_KEYS)
        cq = cq_ref[geom.rows(sub, hh), :]
        s = raw + (jnp.concatenate([cq] * (STEP_KEYS // LANES), axis=1) - c_ref[hh:hh + 1, pl.ds(k0, STEP_KEYS)])
        if isinstance(m, int) and m < geom.own_steps():
            first_key = (geom.n_sub - STEP_TILES * (m + 1)) * TK
            s = jnp.where(key_pos + first_key <= query_pos + sub * TQ, s, NEG)
        return s

    single, run = _make_attend(geom, blk, (qs_ref, m_ref, acc_ref, s_ref), v_ref,
                               keys=lambda m: _step_rows(k_ref, geom, blk, m), finish=finish)
    for m in range(geom.own_steps()):
        single(m)

    tile = _lane_iota((1, LANES))
    needed = jnp.zeros((1, LANES), jnp.bool_)
    for hh in range(2):
        rows_h = [geom.rows(sub, hh) for sub in range(geom.n_sub)]
        m_min = functools.reduce(jnp.minimum, [jnp.min(m_ref[r, :], axis=0, keepdims=True) for r in rows_h])
        cq_max = functools.reduce(jnp.maximum, [jnp.max(cq_ref[r, :], axis=0, keepdims=True) for r in rows_h])
        bound = _head_row_norms(q, hh) * stat_ref[hh:hh + 1, :] + cq_max - stat_ref[2 + hh:3 + hh, :] + 1.0
        needed = jnp.logical_or(needed, bound >= m_min - FOX_SKIP_BITS)
    n_before = geom.n_sub * blk
    first_needed = jnp.min(jnp.where(jnp.logical_and(needed, tile < n_before), tile, n_before))
    run(geom.own_steps(), (n_before - first_needed + STEP_TILES - 1) // STEP_TILES)
    _store_pair_outputs(o_ref, acc_ref, geom)


def _fox(qkv, c, batch, seq):
    return _attn_call(_fox_kernel, 3, qkv, [c.reshape(batch, 2, 8, seq)],
                      [pl.BlockSpec((None, None, 8, seq), lambda p, b, i: (b, p, 0, 0))],
                      FOX_GEOM, batch, seq,
                      extra_scratch=[pltpu.VMEM((FOX_GEOM.n_sub * FOX_GEOM.n_streams * TQ, LANES), jnp.float32),
                                     pltpu.VMEM((8, LANES), jnp.float32)], name="fox_attn")


OUT_TM = 1024
ROUTER_W = LANES
ROUTE_ROWS = 8


PACK_W = D_MODEL // 2


def _pack_bf16_pair(lo, hi):
    lo_bits = lax.bitcast_convert_type(lo.astype(jnp.bfloat16).astype(jnp.float32), jnp.uint32)
    hi_bits = lax.bitcast_convert_type(hi.astype(jnp.bfloat16).astype(jnp.float32), jnp.uint32)
    word = lax.shift_right_logical(lo_bits, jnp.uint32(16)) | (hi_bits & jnp.uint32(0xFFFF0000))
    return lax.bitcast_convert_type(word, jnp.int32)


def _unpack_bf16_pair(word):
    bits = lax.bitcast_convert_type(word, jnp.uint32)
    lo = lax.bitcast_convert_type(lax.shift_left(bits, jnp.uint32(16)), jnp.float32)
    hi = lax.bitcast_convert_type(bits & jnp.uint32(0xFFFF0000), jnp.float32)
    return lo, hi


def _outproj_router_kernel(oa_ref, ob_ref, oc_ref, od_ref, wo_ref, x_ref, g_ref, wr_ref, br_ref,
                           x1_ref, h2_ref, route_ref, cw_ref):
    acc = x_ref[...]
    for m, o_ref in enumerate((oa_ref, ob_ref, oc_ref, od_ref)):
        acc = acc + jnp.dot(o_ref[...], wo_ref[m * GROUP_W:(m + 1) * GROUP_W, :], preferred_element_type=jnp.float32)
    x1_ref[...] = acc
    ms = jnp.mean(acc * acc, axis=1, keepdims=True)
    h = acc * lax.rsqrt(ms + RMS_EPS) * g_ref[...]
    hb = h.astype(jnp.bfloat16)
    h2_ref[...] = _pack_bf16_pair(h[:, 0:PACK_W], h[:, PACK_W:D_MODEL])
    h_lo = (h - hb.astype(jnp.float32)).astype(jnp.bfloat16)
    logits = (jnp.dot(hb, wr_ref[0], preferred_element_type=jnp.float32)
              + jnp.dot(h_lo, wr_ref[0], preferred_element_type=jnp.float32)
              + jnp.dot(hb, wr_ref[1], preferred_element_type=jnp.float32)) + br_ref[...]
    lane = _lane_iota(logits.shape).astype(jnp.float32)
    big = 2.0 * LANES
    gmask = jnp.logical_and(lane >= N_EXPERTS, lane < N_EXPERTS + N_GROUPS)
    gl = jnp.where(gmask, logits, NEG)
    gmax = jnp.max(gl, axis=1, keepdims=True)
    glane = jnp.min(jnp.where(gl == gmax, lane, big), axis=1, keepdims=True)
    gsum = jnp.sum(jnp.where(gmask, jnp.exp(gl - gmax), 0.0), axis=1, keepdims=True)
    g_w = 1.0 / gsum
    e0 = (glane - N_EXPERTS) * EXPERTS_PER_GROUP
    emask = jnp.logical_and(lane >= e0, lane < e0 + EXPERTS_PER_GROUP)
    el = jnp.where(emask, logits, NEG)
    v1 = jnp.max(el, axis=1, keepdims=True)
    i1 = jnp.min(jnp.where(el == v1, lane, big), axis=1, keepdims=True)
    el2 = jnp.where(lane == i1, NEG, el)
    v2 = jnp.max(el2, axis=1, keepdims=True)
    i2 = jnp.min(jnp.where(el2 == v2, lane, big), axis=1, keepdims=True)
    e2 = jnp.exp(v2 - v1)
    den = 1.0 + e2
    c1 = g_w / den
    c2 = g_w * e2 / den
    quad = jnp.where(lane == 0.0, i1, jnp.where(lane == 1.0, i2, jnp.where(lane == 2.0, c1,
                                                                             jnp.where(lane == 3.0, c2, 0.0))))
    cw_ref[...] = jnp.where(lane == 0.0, c1, jnp.where(lane == 1.0, c2, 0.0))
    route_ref[...] = quad.T[0:ROUTE_ROWS, :]


def _outproj_router(o_parts, w_out, x2d, g2, wr, br):
    n_tok = x2d.shape[0]
    tm = min(OUT_TM, n_tok)
    row = lambda i: (i, 0)
    fixed = lambda i: (0, 0)
    return pl.pallas_call(
        _outproj_router_kernel,
        out_shape=(jax.ShapeDtypeStruct((n_tok, D_MODEL), jnp.float32),
                   jax.ShapeDtypeStruct((n_tok, PACK_W), jnp.int32),
                   jax.ShapeDtypeStruct((ROUTE_ROWS, n_tok), jnp.float32),
                   jax.ShapeDtypeStruct((n_tok, ROUTER_W), jnp.float32)),
        grid=(n_tok // tm,),
        in_specs=[pl.BlockSpec((tm, GROUP_W), row)] * 4 + [
            pl.BlockSpec((D_MODEL, D_MODEL), fixed),
            pl.BlockSpec((tm, D_MODEL), row),
            pl.BlockSpec((1, D_MODEL), fixed),
            pl.BlockSpec((2, D_MODEL, ROUTER_W), lambda i: (0, 0, 0)),
            pl.BlockSpec((1, ROUTER_W), fixed)],
        out_specs=(pl.BlockSpec((tm, D_MODEL), row), pl.BlockSpec((tm, PACK_W), row),
                   pl.BlockSpec((ROUTE_ROWS, tm), lambda i: (0, i)), pl.BlockSpec((tm, ROUTER_W), row)),
        compiler_params=pltpu.CompilerParams(dimension_semantics=("arbitrary",), vmem_limit_bytes=VMEM_LIMIT),
        name="outproj_router",
    )(*o_parts, w_out, x2d, g2, wr, br)


ROW_TILE = 512
DISP_TM = 1024
SC_CORES = 2
SC_SUBCORES = 16
SC_WORKERS = SC_CORES * SC_SUBCORES
SC_CHUNK = 64


def _dispatch_kernel(route_ref, rank_ref, cnt_ref, tri_ref, carry_ref):
    step = pl.program_id(0)
    tm = route_ref.shape[1]

    @pl.when(step == 0)
    def _():
        r = lax.broadcasted_iota(jnp.int32, (tm, tm), 0)
        c = lax.broadcasted_iota(jnp.int32, (tm, tm), 1)
        tri_ref[...] = jnp.where(r <= c, 1.0, 0.0).astype(jnp.bfloat16)
        carry_ref[...] = jnp.zeros(carry_ref.shape, jnp.float32)

    e1 = route_ref[0:1, :]
    e2 = route_ref[1:2, :]
    expert = lax.broadcasted_iota(jnp.int32, (N_EXPERTS, tm), 0).astype(jnp.float32)
    hit1 = e1 == expert
    hit2 = e2 == expert
    hits = jnp.where(jnp.logical_or(hit1, hit2), 1.0, 0.0).astype(jnp.bfloat16)
    cum = jnp.dot(hits, tri_ref[...], preferred_element_type=jnp.float32) + carry_ref[...]
    rank1 = jnp.sum(jnp.where(hit1, cum - 1.0, 0.0), axis=0, keepdims=True)
    rank2 = jnp.sum(jnp.where(hit2, cum - 1.0, 0.0), axis=0, keepdims=True)
    row = lax.broadcasted_iota(jnp.int32, (ROUTE_ROWS, tm), 0)
    rank_ref[...] = jnp.where(row == 0, rank1, jnp.where(row == 1, rank2, 0.0))
    total = cum[:, tm - 1:tm]
    carry_ref[...] = total
    cnt_ref[...] = jnp.broadcast_to(total, cnt_ref.shape)


def _dispatch(route):
    n_tok = route.shape[1]
    tm = min(DISP_TM, n_tok)
    return pl.pallas_call(
        _dispatch_kernel,
        out_shape=(jax.ShapeDtypeStruct((ROUTE_ROWS, n_tok), jnp.float32),
                   jax.ShapeDtypeStruct((N_EXPERTS, LANES), jnp.float32)),
        grid=(n_tok // tm,),
        in_specs=[pl.BlockSpec((ROUTE_ROWS, tm), lambda i: (0, i))],
        out_specs=(pl.BlockSpec((ROUTE_ROWS, tm), lambda i: (0, i)),
                   pl.BlockSpec((N_EXPERTS, LANES), lambda i: (0, 0))),
        scratch_shapes=[pltpu.VMEM((tm, tm), jnp.bfloat16), pltpu.VMEM((N_EXPERTS, 1), jnp.float32)],
        compiler_params=pltpu.CompilerParams(dimension_semantics=("arbitrary",)),
        name="moe_ranks",
    )(route)


def _positions_kernel(route_ref, rank_ref, off_ref, pos_ref):
    tm = route_ref.shape[1]
    expert = lax.broadcasted_iota(jnp.int32, (N_EXPERTS, tm), 0).astype(jnp.float32)
    off = off_ref[...]
    start1 = jnp.sum(jnp.where(route_ref[0:1, :] == expert, off, 0.0), axis=0, keepdims=True)
    start2 = jnp.sum(jnp.where(route_ref[1:2, :] == expert, off, 0.0), axis=0, keepdims=True)
    row = lax.broadcasted_iota(jnp.int32, (ROUTE_ROWS, tm), 0)
    pos_ref[...] = jnp.where(row == 0, start1 + rank_ref[0:1, :], jnp.where(row == 1, start2 + rank_ref[1:2, :], 0.0))


def _positions(route, rank, seg_start):
    n_tok = route.shape[1]
    tm = min(DISP_TM, n_tok)
    blk = pl.BlockSpec((ROUTE_ROWS, tm), lambda i: (0, i))
    return pl.pallas_call(
        _positions_kernel,
        out_shape=jax.ShapeDtypeStruct((ROUTE_ROWS, n_tok), jnp.float32),
        grid=(n_tok // tm,),
        in_specs=[blk, blk, pl.BlockSpec((N_EXPERTS, 1), lambda i: (0, 0))],
        out_specs=blk,
        name="moe_positions",
    )(route, rank, seg_start)


def _sc_mesh():
    from jax.experimental.pallas import tpu_sc as plsc
    return plsc.VectorSubcoreMesh(core_axis_name="c", subcore_axis_name="s")


def _sc_worker_id():
    return lax.axis_index("s") * SC_CORES + lax.axis_index("c")


def _sc_scatter_rows(rows, pos, n_out):
    n_tok, width = rows.shape
    per_worker = n_tok // SC_WORKERS
    n_pairs = per_worker // (2 * SC_CHUNK)
    assert per_worker % (2 * SC_CHUNK) == 0
    buf = [pltpu.VMEM((SC_CHUNK,), jnp.int32), pltpu.VMEM((SC_CHUNK,), jnp.int32),
           pltpu.VMEM((SC_CHUNK, width), rows.dtype), pltpu.SemaphoreType.DMA, pltpu.SemaphoreType.DMA]

    @functools.partial(pl.kernel, mesh=_sc_mesh(), out_type=jax.ShapeDtypeStruct((n_out, width), rows.dtype),
                       scratch_types=buf + buf)
    def scatter(rows_hbm, pos_hbm, out_hbm, *scratch):
        base = _sc_worker_id() * per_worker
        buf_a, buf_b = scratch[:5], scratch[5:]

        def load(chunk, p1_v, p2_v, rows_v, sem_in, sem_out):
            off = base + chunk * SC_CHUNK
            pltpu.sync_copy(pos_hbm.at[pl.ds(off, SC_CHUNK)], p1_v)
            pltpu.sync_copy(pos_hbm.at[pl.ds(n_tok + off, SC_CHUNK)], p2_v)
            pltpu.async_copy(rows_hbm.at[pl.ds(off, SC_CHUNK)], rows_v, sem_in)

        def flush(chunk, p1_v, p2_v, rows_v, sem_in, sem_out):
            off = base + chunk * SC_CHUNK
            pltpu.make_async_copy(rows_hbm.at[pl.ds(off, SC_CHUNK)], rows_v, sem_in).wait()
            first = pltpu.async_copy(rows_v, out_hbm.at[p1_v], sem_out)
            second = pltpu.async_copy(rows_v, out_hbm.at[p2_v], sem_out)
            first.wait()
            second.wait()

        load(0, *buf_a)
        def body(j, carry):
            load(2 * j + 1, *buf_b)
            flush(2 * j, *buf_a)

            @pl.when(j + 1 < n_pairs)
            def _():
                load(2 * j + 2, *buf_a)

            flush(2 * j + 1, *buf_b)
            return carry

        lax.fori_loop(0, n_pairs, body, 0)

    return scatter(rows, pos)


def _sc_gather_rows(table, idx):
    n_idx = idx.shape[0]
    width = table.shape[1]
    per_worker = n_idx // SC_WORKERS
    n_pairs = per_worker // (2 * SC_CHUNK)
    assert per_worker % (2 * SC_CHUNK) == 0
    buf = [pltpu.VMEM((SC_CHUNK,), jnp.int32), pltpu.VMEM((SC_CHUNK, width), table.dtype), pltpu.SemaphoreType.DMA]

    @functools.partial(pl.kernel, mesh=_sc_mesh(), out_type=jax.ShapeDtypeStruct((n_idx, width), table.dtype),
                       scratch_types=buf + buf)
    def gather(table_hbm, idx_hbm, out_hbm, *scratch):
        base = _sc_worker_id() * per_worker
        buf_a, buf_b = scratch[:3], scratch[3:]

        def fetch(chunk, idx_v, rows_v, sem):
            pltpu.sync_copy(idx_hbm.at[pl.ds(base + chunk * SC_CHUNK, SC_CHUNK)], idx_v)
            pltpu.async_copy(table_hbm.at[idx_v], rows_v, sem)

        def drain(chunk, idx_v, rows_v, sem):
            pltpu.make_async_copy(table_hbm.at[idx_v], rows_v, sem).wait()
            pltpu.sync_copy(rows_v, out_hbm.at[pl.ds(base + chunk * SC_CHUNK, SC_CHUNK)])

        fetch(0, *buf_a)
        def body(j, carry):
            fetch(2 * j + 1, *buf_b)
            drain(2 * j, *buf_a)

            @pl.when(j + 1 < n_pairs)
            def _():
                fetch(2 * j + 2, *buf_a)

            drain(2 * j + 1, *buf_b)
            return carry

        lax.fori_loop(0, n_pairs, body, 0)

    return gather(table, idx)


def _experts_kernel(tile_expert_ref, n_used_ref, hs_ref, wg_ref, wu_ref, wd_ref, ys_ref):
    del tile_expert_ref

    @pl.when(pl.program_id(0) < n_used_ref[0])
    def _():
        lo, hi = _unpack_bf16_pair(hs_ref[...])
        lo = lo.astype(jnp.bfloat16)
        hi = hi.astype(jnp.bfloat16)
        bf = jnp.bfloat16
        gate = (jnp.dot(lo, wg_ref[0:PACK_W, :].astype(bf), preferred_element_type=jnp.float32)
                + jnp.dot(hi, wg_ref[PACK_W:D_MODEL, :].astype(bf), preferred_element_type=jnp.float32))
        up = (jnp.dot(lo, wu_ref[0:PACK_W, :].astype(bf), preferred_element_type=jnp.float32)
              + jnp.dot(hi, wu_ref[PACK_W:D_MODEL, :].astype(bf), preferred_element_type=jnp.float32))
        hid = ((gate * jax.nn.sigmoid(gate)) * up).astype(bf)
        y = jnp.dot(hid, wd_ref[...].astype(bf), preferred_element_type=jnp.float32)
        ys_ref[...] = _pack_bf16_pair(y[:, 0:PACK_W], y[:, PACK_W:D_MODEL])


def _experts(hs, tile_expert, n_used, wg, wu, wd, layer):
    n_rows = hs.shape[0]
    weights = lambda i, te, nu: (layer, te[i], 0, 0)
    return pl.pallas_call(
        _experts_kernel,
        out_shape=jax.ShapeDtypeStruct((n_rows, PACK_W), jnp.int32),
        grid_spec=pltpu.PrefetchScalarGridSpec(
            num_scalar_prefetch=2,
            grid=(n_rows // ROW_TILE,),
            in_specs=[pl.BlockSpec((ROW_TILE, PACK_W), lambda i, te, nu: (i, 0)),
                      pl.BlockSpec((None, None, D_MODEL, D_EXPERT), weights),
                      pl.BlockSpec((None, None, D_MODEL, D_EXPERT), weights),
                      pl.BlockSpec((None, None, D_EXPERT, D_MODEL), weights)],
            out_specs=pl.BlockSpec((ROW_TILE, PACK_W), lambda i, te, nu: (i, 0))),
        compiler_params=pltpu.CompilerParams(dimension_semantics=("arbitrary",), vmem_limit_bytes=VMEM_LIMIT),
        name="moe_experts",
    )(tile_expert, n_used, hs, wg, wu, wd)


COMB_TM = 1024


def _combine_kernel(x1_ref, y1_ref, y2_ref, cw_ref, gf_ref, out_ref, *, final_norm):
    c1 = cw_ref[:, 0:1]
    c2 = cw_ref[:, 1:2]
    lo1, hi1 = _unpack_bf16_pair(y1_ref[...])
    lo2, hi2 = _unpack_bf16_pair(y2_ref[...])
    y = jnp.concatenate([x1_ref[:, 0:PACK_W] + (c1 * lo1 + c2 * lo2),
                         x1_ref[:, PACK_W:D_MODEL] + (c1 * hi1 + c2 * hi2)], axis=1)
    if final_norm:
        ms = jnp.mean(y * y, axis=1, keepdims=True)
        y = y * lax.rsqrt(ms + RMS_EPS) * gf_ref[...]
    out_ref[...] = y


def _combine(x1, y12, cw, gf, final_norm):
    n_tok = x1.shape[0]
    tm = min(COMB_TM, n_tok)
    n_blk = n_tok // tm
    return pl.pallas_call(
        functools.partial(_combine_kernel, final_norm=final_norm),
        out_shape=jax.ShapeDtypeStruct((n_tok, D_MODEL), jnp.float32),
        grid=(n_blk,),
        in_specs=[pl.BlockSpec((tm, D_MODEL), lambda i: (i, 0)),
                  pl.BlockSpec((tm, PACK_W), lambda i: (i, 0)),
                  pl.BlockSpec((tm, PACK_W), lambda i: (n_blk + i, 0)),
                  pl.BlockSpec((tm, ROUTER_W), lambda i: (i, 0)),
                  pl.BlockSpec((1, D_MODEL), lambda i: (0, 0))],
        out_specs=pl.BlockSpec((tm, D_MODEL), lambda i: (i, 0)),
        compiler_params=pltpu.CompilerParams(dimension_semantics=("arbitrary",), vmem_limit_bytes=VMEM_LIMIT),
        name="moe_combine",
    )(x1, y12, y12, cw, gf)


def _sparse_moe(x1, h2p, route, cw, w_gate, w_up, w_down, layer, gf, final_norm):
    n_tok = x1.shape[0]
    n_rows = 2 * n_tok + N_EXPERTS * ROW_TILE
    rank, counts = _dispatch(route)
    padded = (counts[:, 0].astype(jnp.int32) + ROW_TILE - 1) // ROW_TILE * ROW_TILE
    ends = jnp.cumsum(padded)
    starts = ends - padded
    tile_start = jnp.arange(n_rows // ROW_TILE, dtype=jnp.int32) * ROW_TILE
    tile_expert = jnp.minimum(jnp.sum(tile_start[:, None] >= ends[None, :], axis=1), N_EXPERTS - 1).astype(jnp.int32)
    n_used = (ends[N_EXPERTS - 1:] // ROW_TILE).astype(jnp.int32)

    pos = _positions(route, rank, starts.astype(jnp.float32).reshape(N_EXPERTS, 1))
    pos12 = jnp.concatenate([pos[0], pos[1]]).astype(jnp.int32)
    hs = _sc_scatter_rows(h2p, pos12, n_rows)
    ys = _experts(hs, tile_expert, n_used, w_gate, w_up, w_down, layer)
    y12 = _sc_gather_rows(ys, pos12)
    return _combine(x1, y12, cw, gf, final_norm)


def _split_bf16(w):
    hi = w.astype(jnp.bfloat16)
    lo = (w - hi.astype(jnp.float32)).astype(jnp.bfloat16)
    return jnp.stack([hi, lo])


def _qkv_col_scale():
    s = np.ones((QKV_W,), np.float32)
    for group, dim in enumerate((HEAD_DIM, DIFF_DIM, HEAD_DIM, HEAD_DIM)):
        s[3 * group * GROUP_W:(3 * group + 1) * GROUP_W] = LOG2E * dim ** -0.5
    return s


def _forget_weights(w_f, b_f):
    rows = jnp.zeros((F_ROWS, D_MODEL), jnp.float32)
    bias = jnp.zeros((F_ROWS, 1), jnp.float32)
    for h in range(HEADS_PER_MIXER):
        r = (h // 2) * 8 + (h % 2)
        rows = rows.at[r].set(w_f[:, h])
        bias = bias.at[r, 0].set(b_f[h])
    return _split_bf16(rows), bias


def kernel(x, rel_bias, ln1, w_in, w_out, lam_q1, lam_k1, lam_q2, lam_k2, subln_g, b_forget,
           ln2, w_group, b_group, w_expert, b_expert, w_gate, w_up, w_down, ln_f):
    batch, seq, _ = x.shape
    depth = ln1.shape[0]
    tabs = _bias_tables(rel_bias)
    xt = x.reshape(batch * seq, D_MODEL)
    gf = ln_f.reshape(1, D_MODEL)
    col_scale = _qkv_col_scale()
    for l in range(depth):
        lambda_init = 0.8 - 0.6 * math.exp(-0.3 * l)
        wf, bf = _forget_weights(w_in[l][:, QKV_W:], b_forget[l])
        w_qkv = (w_in[l][:, :QKV_W] * col_scale).astype(jnp.bfloat16)
        qkv, lf = _norm_proj(xt, ln1[l].reshape(1, D_MODEL), w_qkv, wf, bf, batch, seq)
        c = _cumsum(lf)

        o_a = _dilated(qkv, tabs, batch, seq)
        lamv = jnp.zeros((8, LANES), jnp.float32)
        lamv = lamv.at[0:4, 0:DIFF_DIM].set(jnp.stack([lam_q1[l], lam_k1[l], lam_q2[l], lam_k2[l]]))
        g_sub = jnp.tile(subln_g[l], 2).reshape(1, PAIR_W)
        o_b = _diff(qkv, tabs, lamv, g_sub, lambda_init, batch, seq)
        o_c = _moba(qkv, tabs, batch, seq)
        o_d = _fox(qkv, c, batch, seq)

        w_router = jnp.zeros((D_MODEL, ROUTER_W), jnp.float32)
        w_router = w_router.at[:, :N_EXPERTS].set(w_expert[l]).at[:, N_EXPERTS:N_EXPERTS + N_GROUPS].set(w_group[l])
        b_router = jnp.zeros((1, ROUTER_W), jnp.float32)
        b_router = b_router.at[0, :N_EXPERTS].set(b_expert[l].reshape(-1)).at[0, N_EXPERTS:N_EXPERTS + N_GROUPS].set(b_group[l])
        x1, h2p, route, cw = _outproj_router([o_a, o_b, o_c, o_d], w_out[l].astype(jnp.bfloat16), xt,
                                             ln2[l].reshape(1, D_MODEL), _split_bf16(w_router), b_router)
        xt = _sparse_moe(x1, h2p, route, cw, w_gate, w_up, w_down, l, gf, final_norm=(l == depth - 1))
    return xt.reshape(batch, seq, D_MODEL)
```

```python
import functools
import math
from typing import NamedTuple

import jax
import jax.numpy as jnp
import numpy as np
from jax import lax
from jax.experimental import pallas as pl
from jax.experimental.pallas import tpu as pltpu

D_MODEL = 1024
HEAD_DIM = 64
HEADS_PER_MIXER = 4
GROUP_W = HEADS_PER_MIXER * HEAD_DIM
QKV_W = 12 * GROUP_W
DIFF_DIM = HEAD_DIM // 2
MOBA_BLOCK = 256
MOBA_TOPK = 3
N_BUCKETS = 32
MAX_DISTANCE = 2048
N_BIAS_HEADS = 12
N_GROUPS = 4
EXPERTS_PER_GROUP = 8
N_EXPERTS = 32
D_EXPERT = 256
RMS_EPS = 1e-6
NEG = -1e30
LOG2E = math.log2(math.e)

LANES = 128
PAIR_W = 2 * HEAD_DIM
ACC_W = 2 * PAIR_W
TQ = 256
TK = 256
MAX_SUB = 8
STEP_TILES = 2
STEP_KEYS = STEP_TILES * TK
N_DIL_TILES = MAX_DISTANCE // TK + 1
FAR_TILE = 7
MIN_DIST = 1 - MAX_SUB
N_TAB_TILES = N_DIL_TILES + 1 - MIN_DIST
VMEM_LIMIT = 48 * 1024 * 1024

_NT = (((1,), (1,)), ((), ()))


def _t5_thresholds():
    d = np.arange(0, 4 * MAX_DISTANCE, dtype=np.int64)
    max_exact = N_BUCKETS // 2
    df = np.maximum(d, 1).astype(np.float32)
    large = max_exact + (np.log(df / np.float32(max_exact)) / np.float32(math.log(MAX_DISTANCE / max_exact))
                         * np.float32(N_BUCKETS - max_exact)).astype(np.int32)
    bucket = np.where(d < max_exact, d, np.minimum(large, N_BUCKETS - 1))
    return [int(np.argmax(bucket >= b)) for b in range(N_BUCKETS)]


_T5_THR = _t5_thresholds()
assert (FAR_TILE - 1) * TK + 1 >= _T5_THR[N_BUCKETS - 1]


def _lane_iota(shape):
    return lax.broadcasted_iota(jnp.int32, shape, len(shape) - 1)


def _bias_table_kernel(bias_ref, out_ref):
    h = pl.program_id(0)
    n = pl.program_id(1) + MIN_DIST
    i = lax.broadcasted_iota(jnp.int32, (TQ, TK), 0)
    j = lax.broadcasted_iota(jnp.int32, (TQ, TK), 1)
    d = n * TK + i - j
    is_dil = h < HEADS_PER_MIXER

    def store(val):
        mult = ((d <= 128).astype(jnp.int32)
                + jnp.logical_and((d & 3) == 0, d <= 512).astype(jnp.int32)
                + jnp.logical_and((d & 15) == 0, d <= 2048).astype(jnp.int32))
        logm = jnp.where(mult == 3, math.log(3.0), jnp.where(mult == 2, math.log(2.0), 0.0))
        val = val + jnp.where(is_dil, logm, 0.0)
        valid = jnp.logical_and(d >= 0, jnp.logical_or(mult > 0, jnp.logical_not(is_dil)))
        out_ref[...] = jnp.where(valid, val * LOG2E, NEG)

    @pl.when(n < 0)
    def _():
        out_ref[...] = jnp.full((TQ, TK), NEG, jnp.float32)

    @pl.when(n >= FAR_TILE)
    def _():
        store(jnp.full((TQ, TK), bias_ref[h, N_BUCKETS - 1], jnp.float32))

    @pl.when(jnp.logical_and(n >= 0, n < FAR_TILE))
    def _():
        val = jnp.full((TQ, TK), bias_ref[h, 0], jnp.float32)
        for b in range(1, N_BUCKETS):
            val = jnp.where(d >= _T5_THR[b], bias_ref[h, b], val)
        store(val)


def _bias_tables(rel_bias):
    bias_h = rel_bias.T.astype(jnp.float32)
    return pl.pallas_call(
        _bias_table_kernel,
        out_shape=jax.ShapeDtypeStruct((N_BIAS_HEADS, N_TAB_TILES, TQ, TK), jnp.float32),
        grid=(N_BIAS_HEADS, N_TAB_TILES),
        in_specs=[pl.BlockSpec(memory_space=pltpu.SMEM)],
        out_specs=pl.BlockSpec((None, None, TQ, TK), lambda h, n: (h, n, 0, 0)),
        name="bias_tables",
    )(bias_h)


PROJ_TM = 1024
PROJ_CW = 512
F_ROWS = 16


def _norm_proj_kernel(x_ref, g_ref, w_ref, wf_ref, bf_ref, qkv_ref, lf_ref):
    x = x_ref[...]
    ms = jnp.mean(x * x, axis=1, keepdims=True)
    h = x * lax.rsqrt(ms + RMS_EPS) * g_ref[...]
    hb = h.astype(jnp.bfloat16)
    for c in range(QKV_W // PROJ_CW):
        cols = slice(c * PROJ_CW, (c + 1) * PROJ_CW)
        qkv_ref[:, cols] = jnp.dot(hb, w_ref[:, cols], preferred_element_type=jnp.float32).astype(jnp.bfloat16)
    h_lo = (h - hb.astype(jnp.float32)).astype(jnp.bfloat16)
    wf_hi = wf_ref[0]
    wf_lo = wf_ref[1]
    z = (lax.dot_general(wf_hi, hb, _NT, preferred_element_type=jnp.float32)
         + lax.dot_general(wf_hi, h_lo, _NT, preferred_element_type=jnp.float32)
         + lax.dot_general(wf_lo, hb, _NT, preferred_element_type=jnp.float32))
    z = z + bf_ref[...]
    lf_ref[...] = jnp.minimum(z, 0.0) - jnp.log(1.0 + jnp.exp(-jnp.abs(z)))


def _norm_proj(x2d, g, w_qkv, wf, bf, batch, seq):
    n_tok = x2d.shape[0]
    tm = min(PROJ_TM, seq)
    per_b = seq // tm
    return pl.pallas_call(
        _norm_proj_kernel,
        out_shape=(jax.ShapeDtypeStruct((n_tok, QKV_W), jnp.bfloat16),
                   jax.ShapeDtypeStruct((batch * F_ROWS, seq), jnp.float32)),
        grid=(n_tok // tm,),
        in_specs=[pl.BlockSpec((tm, D_MODEL), lambda i: (i, 0)),
                  pl.BlockSpec((1, D_MODEL), lambda i: (0, 0)),
                  pl.BlockSpec((D_MODEL, QKV_W), lambda i: (0, 0)),
                  pl.BlockSpec((2, F_ROWS, D_MODEL), lambda i: (0, 0, 0)),
                  pl.BlockSpec((F_ROWS, 1), lambda i: (0, 0))],
        out_specs=(pl.BlockSpec((tm, QKV_W), lambda i: (i, 0)),
                   pl.BlockSpec((F_ROWS, tm), lambda i: (i // per_b, i % per_b))),
        compiler_params=pltpu.CompilerParams(dimension_semantics=("arbitrary",), vmem_limit_bytes=VMEM_LIMIT),
        name="norm_proj",
    )(x2d, g, w_qkv, wf, bf)


def _split3(x):
    x1 = x.astype(jnp.bfloat16)
    r1 = x - x1.astype(jnp.float32)
    x2 = r1.astype(jnp.bfloat16)
    x3 = (r1 - x2.astype(jnp.float32)).astype(jnp.bfloat16)
    return x1, x2, x3


def _cumsum_kernel(lf_ref, c_ref):
    rows, seq = lf_ref.shape
    r = lax.broadcasted_iota(jnp.int32, (LANES, LANES), 0)
    c = lax.broadcasted_iota(jnp.int32, (LANES, LANES), 1)
    upper = jnp.where(r <= c, 1.0, 0.0).astype(jnp.bfloat16)

    def body(i, carry):
        off = pl.multiple_of(i * LANES, LANES)
        x1, x2, x3 = _split3(lf_ref[:, pl.ds(off, LANES)])
        cs = (jnp.dot(x1, upper, preferred_element_type=jnp.float32)
              + jnp.dot(x2, upper, preferred_element_type=jnp.float32)
              + jnp.dot(x3, upper, preferred_element_type=jnp.float32)) + carry
        c_ref[:, pl.ds(off, LANES)] = cs * LOG2E
        return cs[:, LANES - 1:LANES]

    lax.fori_loop(0, seq // LANES, body, jnp.zeros((rows, 1), jnp.float32))


def _cumsum(lf):
    return pl.pallas_call(
        _cumsum_kernel,
        out_shape=jax.ShapeDtypeStruct(lf.shape, jnp.float32),
        name="forget_cumsum",
    )(lf)


class _Geom(NamedTuple):
    n_streams: int
    n_sub: int

    @property
    def q_rows(self):
        return self.n_sub * TQ

    def rows(self, sub, st):
        rb = sub * self.n_streams + st
        return slice(rb * TQ, (rb + 1) * TQ)

    def blocks(self):
        return [(sub, st, self.rows(sub, st)) for sub in range(self.n_sub) for st in range(self.n_streams)]

    def first_tile(self, blk, m):
        return self.n_sub * blk + self.n_sub - STEP_TILES * (m + 1)

    def dist(self, m, sub, w):
        return STEP_TILES * (m + 1) - self.n_sub + sub - w

    def n_steps(self, blk):
        return (self.n_sub // STEP_TILES) * (blk + 1)

    def own_steps(self):
        return self.n_sub // STEP_TILES


def _stack_masked_q(q, qs_ref, geom):
    lane = _lane_iota((TQ, PAIR_W))
    width = PAIR_W // geom.n_streams
    for sub, st, rows in geom.blocks():
        q_tile = q[sub * TQ:(sub + 1) * TQ]
        qs_ref[rows, 0:PAIR_W] = jnp.where(lane // width == st, q_tile, jnp.zeros_like(q_tile))


def _init_state(m_ref, acc_ref):
    m_ref[...] = jnp.full(m_ref.shape, NEG, jnp.float32)
    acc_ref[...] = jnp.zeros(acc_ref.shape, jnp.float32)


def _flash_step(rows, s, v_aug, m_ref, acc_ref):
    m_prev = m_ref[rows, :]
    m_new = jnp.maximum(m_prev, jnp.max(s, axis=1, keepdims=True))
    alpha = jnp.exp2(m_prev - m_new)
    p = jnp.concatenate([jnp.exp2(s[:, c * LANES:(c + 1) * LANES] - m_new) for c in range(s.shape[1] // LANES)],
                        axis=1).astype(v_aug.dtype)
    pv = jnp.dot(p, v_aug, preferred_element_type=jnp.float32)
    acc_ref[rows, :] = jnp.concatenate([alpha, alpha], axis=1) * acc_ref[rows, :] + pv
    m_ref[rows, :] = m_new


def _normalised(acc_ref, rows):
    return acc_ref[rows, 0:PAIR_W] / acc_ref[rows, PAIR_W:ACC_W]


def _table_bias(tab_ref, hh, geom, m, sub, cap):
    return jnp.concatenate([tab_ref[hh, jnp.minimum(geom.dist(m, sub, w), cap) - MIN_DIST]
                            for w in range(STEP_TILES)], axis=1)


def _step_rows(ref, geom, blk, m):
    k0 = pl.multiple_of(geom.first_tile(blk, m) * TK, STEP_KEYS)
    return ref[pl.ds(k0, STEP_KEYS), :]


def _make_attend(geom, blk, scratch, v_ref, keys, finish):
    qs_ref, m_ref, acc_ref, s_ref = scratch
    blocks = geom.blocks()
    last_step = geom.n_steps(blk) - 1

    def raw(rows, rhs):
        return lax.dot_general(qs_ref[rows, :], rhs, _NT, preferred_element_type=jnp.float32)

    def v_aug(m):
        v = _step_rows(v_ref, geom, blk, m)
        return jnp.concatenate([v, jnp.ones_like(v)], axis=1)

    def all_future(m, sub):
        return isinstance(m, int) and geom.dist(m, sub, 0) < 0

    def single(m):
        rhs = keys(m)
        va = v_aug(m)
        for sub, st, rows in blocks:
            if not all_future(m, sub):
                _flash_step(rows, finish(sub, st, raw(rows, rhs), m), va, m_ref, acc_ref)

    def step(m):
        va = v_aug(m)
        rhs_next = keys(jnp.minimum(m + 1, last_step))
        for sub, st, rows in blocks:
            if all_future(m, sub):
                s_ref[rows, :] = raw(rows, rhs_next)
                continue
            cur = s_ref[rows, :]
            s_ref[rows, :] = raw(rows, rhs_next)
            _flash_step(rows, finish(sub, st, cur, m), va, m_ref, acc_ref)

    def run(m0, count):
        peeled = [m0 + i for i in range(geom.own_steps())
                  if isinstance(m0, int) and any(all_future(m0 + i, sub) for sub in range(geom.n_sub))]

        @pl.when(count > 0)
        def _():
            rhs = keys(m0)
            for sub, _, rows in blocks:
                if not all_future(m0, sub):
                    s_ref[rows, :] = raw(rows, rhs)

        for m in peeled:
            pl.when(count > m - m0)(functools.partial(step, m))

        def body(i, carry):
            step(m0 + i)
            return carry

        lax.fori_loop(len(peeled), count, body, 0)

    return single, run


def _attn_call(kernel, group, qkv, extra_inputs, extra_specs, geom, batch, seq, *, q_width=PAIR_W,
               extra_scratch=(), name=None):
    assert seq % geom.q_rows == 0 and seq % STEP_KEYS == 0
    nq = seq // geom.q_rows
    qc, kc, vc = 6 * group, 6 * group + 2, 6 * group + 4
    rows = geom.n_sub * geom.n_streams * TQ
    return pl.pallas_call(
        functools.partial(kernel, geom=geom),
        out_shape=jax.ShapeDtypeStruct((batch * seq, GROUP_W), jnp.bfloat16),
        grid=(2, batch, nq),
        in_specs=[pl.BlockSpec((geom.q_rows, PAIR_W), lambda p, b, i: (b * nq + i, qc + p)),
                  pl.BlockSpec((seq, PAIR_W), lambda p, b, i: (b, kc + p)),
                  pl.BlockSpec((seq, PAIR_W), lambda p, b, i: (b, vc + p))] + list(extra_specs),
        out_specs=pl.BlockSpec((geom.q_rows, PAIR_W), lambda p, b, i: (b * nq + i, p)),
        scratch_shapes=[pltpu.VMEM((rows, q_width), jnp.bfloat16),
                        pltpu.VMEM((rows, LANES), jnp.float32),
                        pltpu.VMEM((rows, ACC_W), jnp.float32),
                        pltpu.VMEM((rows, STEP_KEYS), jnp.float32)
                        ] + list(extra_scratch),
        compiler_params=pltpu.CompilerParams(dimension_semantics=("arbitrary", "arbitrary", "arbitrary"),
                                             vmem_limit_bytes=VMEM_LIMIT),
        name=name,
    )(qkv, qkv, qkv, *extra_inputs)


def _tab_spec(first_head):
    return pl.BlockSpec((2, N_TAB_TILES, TQ, TK), lambda p, b, i: (first_head // 2 + p, 0, 0, 0))


def _merge_pair(o0, o1):
    return jnp.where(_lane_iota(o0.shape) < HEAD_DIM, o0, o1)


def _store_pair_outputs(o_ref, acc_ref, geom):
    for sub in range(geom.n_sub):
        o = _merge_pair(_normalised(acc_ref, geom.rows(sub, 0)), _normalised(acc_ref, geom.rows(sub, 1)))
        o_ref[sub * TQ:(sub + 1) * TQ, :] = o.astype(o_ref.dtype)


PAIR_GEOM = _Geom(n_streams=2, n_sub=4)
MOBA_GEOM = _Geom(n_streams=2, n_sub=MAX_SUB)
FOX_GEOM = _Geom(n_streams=2, n_sub=2)
DIFF_GEOM = _Geom(n_streams=4, n_sub=4)


def _dilated_kernel(q_ref, k_ref, v_ref, tab_ref, o_ref, qs_ref, m_ref, acc_ref, s_ref, *, geom):
    blk = pl.program_id(2)
    _stack_masked_q(q_ref[...], qs_ref, geom)
    _init_state(m_ref, acc_ref)
    _, run = _make_attend(
        geom, blk, (qs_ref, m_ref, acc_ref, s_ref), v_ref,
        keys=lambda m: _step_rows(k_ref, geom, blk, m),
        finish=lambda sub, st, raw, m: raw + _table_bias(tab_ref, st, geom, m, sub, N_DIL_TILES))
    reach = -(-(N_DIL_TILES - 1 + geom.n_sub) // STEP_TILES)
    run(0, jnp.minimum(geom.n_steps(blk), reach))
    _store_pair_outputs(o_ref, acc_ref, geom)


def _dilated(qkv, tabs, batch, seq):
    return _attn_call(_dilated_kernel, 0, qkv, [tabs], [_tab_spec(0)], PAIR_GEOM, batch, seq, name="dilated_attn")


def _diff_kernel(q_ref, k_ref, v_ref, tab_ref, lam_ref, g_ref, o_ref, qs_ref, m_ref, acc_ref, s_ref, *,
                 lambda_init, geom):
    blk = pl.program_id(2)
    _stack_masked_q(q_ref[...], qs_ref, geom)
    _init_state(m_ref, acc_ref)
    _, run = _make_attend(
        geom, blk, (qs_ref, m_ref, acc_ref, s_ref), v_ref,
        keys=lambda m: _step_rows(k_ref, geom, blk, m),
        finish=lambda sub, st, raw, m: raw + _table_bias(tab_ref, st // 2, geom, m, sub, FAR_TILE))
    run(0, geom.n_steps(blk))

    lamv = lam_ref[...]
    lam = (jnp.exp(jnp.sum(lamv[0:1] * lamv[1:2], axis=1, keepdims=True))
           - jnp.exp(jnp.sum(lamv[2:3] * lamv[3:4], axis=1, keepdims=True)) + lambda_init)
    first = _lane_iota((TQ, PAIR_W)) < HEAD_DIM
    for sub in range(geom.n_sub):
        part = [_normalised(acc_ref, geom.rows(sub, st)) for st in range(geom.n_streams)]
        o = _merge_pair(part[0] - lam * part[1], part[2] - lam * part[3])
        sq = o * o
        ms0 = jnp.sum(jnp.where(first, sq, 0.0), axis=1, keepdims=True)
        ms1 = jnp.sum(jnp.where(first, 0.0, sq), axis=1, keepdims=True)
        ms = jnp.where(first, ms0, ms1) * (1.0 / HEAD_DIM)
        y = (o * lax.rsqrt(ms + RMS_EPS) * g_ref[...]) * (1.0 - lambda_init)
        o_ref[sub * TQ:(sub + 1) * TQ, :] = y.astype(o_ref.dtype)


def _diff(qkv, tabs, lamv, g_sub, lambda_init, batch, seq):
    return _attn_call(functools.partial(_diff_kernel, lambda_init=lambda_init), 1, qkv, [tabs, lamv, g_sub],
                      [_tab_spec(4), pl.BlockSpec((8, LANES), lambda p, b, i: (0, 0)),
                       pl.BlockSpec((1, PAIR_W), lambda p, b, i: (0, 0))],
                      DIFF_GEOM, batch, seq, name="diff_attn")


def _moba_kernel(q_ref, k_ref, v_ref, tab_ref, o_ref, qs_ref, m_ref, acc_ref, s_ref, km_ref, *, geom):
    blk = pl.program_id(2)
    n_blk = k_ref.shape[0] // MOBA_BLOCK
    assert n_blk <= LANES and MOBA_BLOCK == TQ == TK

    @pl.when(blk == 0)
    def _():
        km_ref[...] = jnp.zeros(km_ref.shape, jnp.float32)
        for n in range(n_blk):
            km_ref[n:n + 1, :] = jnp.sum(k_ref[n * MOBA_BLOCK:(n + 1) * MOBA_BLOCK, :].astype(jnp.float32),
                                         axis=0, keepdims=True) * (1.0 / MOBA_BLOCK)

    _stack_masked_q(q_ref[...], qs_ref, geom)
    _init_state(m_ref, acc_ref)
    nb = -(-n_blk // 8) * 8
    km = km_ref[0:nb, :]
    km_hi = km.astype(jnp.bfloat16)
    km_lo = (km - km_hi.astype(jnp.float32)).astype(jnp.bfloat16)
    qs = qs_ref[:, 0:PAIR_W]
    gate = (lax.dot_general(km_hi, qs, _NT, preferred_element_type=jnp.float32)
            + lax.dot_general(km_lo, qs, _NT, preferred_element_type=jnp.float32))
    cand = lax.broadcasted_iota(jnp.int32, gate.shape, 0)
    cand_f = cand.astype(jnp.float32)
    own = geom.n_sub * blk + _lane_iota(gate.shape) // (geom.n_streams * TQ)
    g = jnp.where(cand < own, gate, NEG)
    sel = cand == own
    for _ in range(MOBA_TOPK):
        mx = jnp.max(g, axis=0, keepdims=True)
        first = jnp.min(jnp.where(g == mx, cand_f, 2.0 * LANES), axis=0, keepdims=True)
        pick = cand_f == first
        sel = jnp.logical_or(sel, jnp.logical_and(pick, mx > 0.5 * NEG))
        g = jnp.where(pick, -3e38, g)
    mask = jnp.where(sel, 0.0, NEG)
    if nb < LANES:
        mask = jnp.concatenate([mask, jnp.full((LANES - nb, mask.shape[1]), NEG, jnp.float32)], axis=0)
    qs_ref[:, PAIR_W:2 * PAIR_W] = mask.T.astype(jnp.bfloat16)

    def keys(m):
        k = _step_rows(k_ref, geom, blk, m)
        block_of_key = geom.first_tile(blk, m) + lax.broadcasted_iota(jnp.int32, (STEP_KEYS, LANES), 0) // TK
        onehot = jnp.where(_lane_iota((STEP_KEYS, LANES)) == block_of_key, 1.0, 0.0).astype(k.dtype)
        return jnp.concatenate([k, onehot], axis=1)

    _, run = _make_attend(
        geom, blk, (qs_ref, m_ref, acc_ref, s_ref), v_ref, keys=keys,
        finish=lambda sub, st, raw, m: raw + _table_bias(tab_ref, st, geom, m, sub, FAR_TILE))
    run(0, geom.n_steps(blk))
    _store_pair_outputs(o_ref, acc_ref, geom)


def _moba(qkv, tabs, batch, seq):
    return _attn_call(_moba_kernel, 2, qkv, [tabs], [_tab_spec(8)], MOBA_GEOM, batch, seq, q_width=2 * PAIR_W,
                      extra_scratch=[pltpu.VMEM((LANES, PAIR_W), jnp.float32)], name="moba_attn")


FOX_SKIP_BITS = 48.0


def _head_row_norms(x, hh):
    xf = x.astype(jnp.float32)
    sq = jnp.where(_lane_iota(xf.shape) // HEAD_DIM == hh, xf * xf, 0.0)
    return jnp.sqrt(jnp.max(jnp.sum(sq, axis=1, keepdims=True), axis=0, keepdims=True))


def _fox_kernel(q_ref, k_ref, v_ref, c_ref, o_ref, qs_ref, m_ref, acc_ref, s_ref, cq_ref, stat_ref, *, geom):
    blk = pl.program_id(2)
    n_tiles = k_ref.shape[0] // TK
    assert n_tiles <= LANES

    @pl.when(blk == 0)
    def _():
        stat_ref[...] = jnp.zeros(stat_ref.shape, jnp.float32)
        for j in range(n_tiles):
            k_tile = k_ref[j * TK:(j + 1) * TK, :]
            for hh in range(2):
                stat_ref[hh:hh + 1, j:j + 1] = _head_row_norms(k_tile, hh)
                stat_ref[2 + hh:3 + hh, j:j + 1] = jnp.min(c_ref[hh:hh + 1, j * TK:(j + 1) * TK], axis=1, keepdims=True)

    q = q_ref[...]
    _stack_masked_q(q, qs_ref, geom)
    _init_state(m_ref, acc_ref)
    q0 = pl.multiple_of(blk * geom.q_rows, geom.q_rows)
    row = lax.broadcasted_iota(jnp.int32, (TQ, TQ), 0)
    col = lax.broadcasted_iota(jnp.int32, (TQ, TQ), 1)
    for sub, hh, rows in geom.blocks():
        c_row = c_ref[hh:hh + 1, pl.ds(q0 + sub * TQ, TQ)]
        c_col = jnp.sum(jnp.where(row == col, c_row, 0.0), axis=1, keepdims=True)
        cq_ref[rows, :] = jnp.broadcast_to(c_col, (TQ, LANES))

    key_pos = lax.broadcasted_iota(jnp.int32, (TQ, STEP_KEYS), 1)
    query_pos = lax.broadcasted_iota(jnp.int32, (TQ, STEP_KEYS), 0)

    def finish(sub, hh, raw, m):
        k0 = pl.multiple_of(geom.first_tile(blk, m) * TK, STEP_KEYS)
        cq = cq_ref[geom.rows(sub, hh), :]
        s = raw + (jnp.concatenate([cq] * (STEP_KEYS // LANES), axis=1) - c_ref[hh:hh + 1, pl.ds(k0, STEP_KEYS)])
        if isinstance(m, int) and m < geom.own_steps():
            first_key = (geom.n_sub - STEP_TILES * (m + 1)) * TK
            s = jnp.where(key_pos + first_key <= query_pos + sub * TQ, s, NEG)
        return s

    single, run = _make_attend(geom, blk, (qs_ref, m_ref, acc_ref, s_ref), v_ref,
                               keys=lambda m: _step_rows(k_ref, geom, blk, m), finish=finish)
    for m in range(geom.own_steps()):
        single(m)

    tile = _lane_iota((1, LANES))
    needed = jnp.zeros((1, LANES), jnp.bool_)
    for hh in range(2):
        rows_h = [geom.rows(sub, hh) for sub in range(geom.n_sub)]
        m_min = functools.reduce(jnp.minimum, [jnp.min(m_ref[r, :], axis=0, keepdims=True) for r in rows_h])
        cq_max = functools.reduce(jnp.maximum, [jnp.max(cq_ref[r, :], axis=0, keepdims=True) for r in rows_h])
        bound = _head_row_norms(q, hh) * stat_ref[hh:hh + 1, :] + cq_max - stat_ref[2 + hh:3 + hh, :] + 1.0
        needed = jnp.logical_or(needed, bound >= m_min - FOX_SKIP_BITS)
    n_before = geom.n_sub * blk
    first_needed = jnp.min(jnp.where(jnp.logical_and(needed, tile < n_before), tile, n_before))
    run(geom.own_steps(), (n_before - first_needed + STEP_TILES - 1) // STEP_TILES)
    _store_pair_outputs(o_ref, acc_ref, geom)


def _fox(qkv, c, batch, seq):
    return _attn_call(_fox_kernel, 3, qkv, [c.reshape(batch, 2, 8, seq)],
                      [pl.BlockSpec((None, None, 8, seq), lambda p, b, i: (b, p, 0, 0))],
                      FOX_GEOM, batch, seq,
                      extra_scratch=[pltpu.VMEM((FOX_GEOM.n_sub * FOX_GEOM.n_streams * TQ, LANES), jnp.float32),
                                     pltpu.VMEM((8, LANES), jnp.float32)], name="fox_attn")


OUT_TM = 1024
ROUTER_W = LANES
ROUTE_ROWS = 8


PACK_W = D_MODEL // 2


def _pack_bf16_pair(lo, hi):
    lo_bits = lax.bitcast_convert_type(lo.astype(jnp.bfloat16).astype(jnp.float32), jnp.uint32)
    hi_bits = lax.bitcast_convert_type(hi.astype(jnp.bfloat16).astype(jnp.float32), jnp.uint32)
    word = lax.shift_right_logical(lo_bits, jnp.uint32(16)) | (hi_bits & jnp.uint32(0xFFFF0000))
    return lax.bitcast_convert_type(word, jnp.int32)


def _unpack_bf16_pair(word):
    bits = lax.bitcast_convert_type(word, jnp.uint32)
    lo = lax.bitcast_convert_type(lax.shift_left(bits, jnp.uint32(16)), jnp.float32)
    hi = lax.bitcast_convert_type(bits & jnp.uint32(0xFFFF0000), jnp.float32)
    return lo, hi


def _outproj_router_kernel(oa_ref, ob_ref, oc_ref, od_ref, wo_ref, x_ref, g_ref, wr_ref, br_ref,
                           x1_ref, h2_ref, route_ref, cw_ref):
    acc = x_ref[...]
    for m, o_ref in enumerate((oa_ref, ob_ref, oc_ref, od_ref)):
        acc = acc + jnp.dot(o_ref[...], wo_ref[m * GROUP_W:(m + 1) * GROUP_W, :], preferred_element_type=jnp.float32)
    x1_ref[...] = acc
    ms = jnp.mean(acc * acc, axis=1, keepdims=True)
    h = acc * lax.rsqrt(ms + RMS_EPS) * g_ref[...]
    hb = h.astype(jnp.bfloat16)
    h2_ref[...] = _pack_bf16_pair(h[:, 0:PACK_W], h[:, PACK_W:D_MODEL])
    h_lo = (h - hb.astype(jnp.float32)).astype(jnp.bfloat16)
    logits = (jnp.dot(hb, wr_ref[0], preferred_element_type=jnp.float32)
              + jnp.dot(h_lo, wr_ref[0], preferred_element_type=jnp.float32)
              + jnp.dot(hb, wr_ref[1], preferred_element_type=jnp.float32)) + br_ref[...]
    lane = _lane_iota(logits.shape).astype(jnp.float32)
    big = 2.0 * LANES
    gmask = jnp.logical_and(lane >= N_EXPERTS, lane < N_EXPERTS + N_GROUPS)
    gl = jnp.where(gmask, logits, NEG)
    gmax = jnp.max(gl, axis=1, keepdims=True)
    glane = jnp.min(jnp.where(gl == gmax, lane, big), axis=1, keepdims=True)
    gsum = jnp.sum(jnp.where(gmask, jnp.exp(gl - gmax), 0.0), axis=1, keepdims=True)
    g_w = 1.0 / gsum
    e0 = (glane - N_EXPERTS) * EXPERTS_PER_GROUP
    emask = jnp.logical_and(lane >= e0, lane < e0 + EXPERTS_PER_GROUP)
    el = jnp.where(emask, logits, NEG)
    v1 = jnp.max(el, axis=1, keepdims=True)
    i1 = jnp.min(jnp.where(el == v1, lane, big), axis=1, keepdims=True)
    el2 = jnp.where(lane == i1, NEG, el)
    v2 = jnp.max(el2, axis=1, keepdims=True)
    i2 = jnp.min(jnp.where(el2 == v2, lane, big), axis=1, keepdims=True)
    e2 = jnp.exp(v2 - v1)
    den = 1.0 + e2
    c1 = g_w / den
    c2 = g_w * e2 / den
    quad = jnp.where(lane == 0.0, i1, jnp.where(lane == 1.0, i2, jnp.where(lane == 2.0, c1,
                                                                             jnp.where(lane == 3.0, c2, 0.0))))
    cw_ref[...] = jnp.where(lane == 0.0, c1, jnp.where(lane == 1.0, c2, 0.0))
    route_ref[...] = quad.T[0:ROUTE_ROWS, :]


def _outproj_router(o_parts, w_out, x2d, g2, wr, br):
    n_tok = x2d.shape[0]
    tm = min(OUT_TM, n_tok)
    row = lambda i: (i, 0)
    fixed = lambda i: (0, 0)
    return pl.pallas_call(
        _outproj_router_kernel,
        out_shape=(jax.ShapeDtypeStruct((n_tok, D_MODEL), jnp.float32),
                   jax.ShapeDtypeStruct((n_tok, PACK_W), jnp.int32),
                   jax.ShapeDtypeStruct((ROUTE_ROWS, n_tok), jnp.float32),
                   jax.ShapeDtypeStruct((n_tok, ROUTER_W), jnp.float32)),
        grid=(n_tok // tm,),
        in_specs=[pl.BlockSpec((tm, GROUP_W), row)] * 4 + [
            pl.BlockSpec((D_MODEL, D_MODEL), fixed),
            pl.BlockSpec((tm, D_MODEL), row),
            pl.BlockSpec((1, D_MODEL), fixed),
            pl.BlockSpec((2, D_MODEL, ROUTER_W), lambda i: (0, 0, 0)),
            pl.BlockSpec((1, ROUTER_W), fixed)],
        out_specs=(pl.BlockSpec((tm, D_MODEL), row), pl.BlockSpec((tm, PACK_W), row),
                   pl.BlockSpec((ROUTE_ROWS, tm), lambda i: (0, i)), pl.BlockSpec((tm, ROUTER_W), row)),
        compiler_params=pltpu.CompilerParams(dimension_semantics=("arbitrary",), vmem_limit_bytes=VMEM_LIMIT),
        name="outproj_router",
    )(*o_parts, w_out, x2d, g2, wr, br)


ROW_TILE = 512
DISP_TM = 1024
SC_CORES = 2
SC_SUBCORES = 16
SC_WORKERS = SC_CORES * SC_SUBCORES
SC_CHUNK = 64


def _dispatch_kernel(route_ref, rank_ref, cnt_ref, tri_ref, carry_ref):
    step = pl.program_id(0)
    tm = route_ref.shape[1]

    @pl.when(step == 0)
    def _():
        r = lax.broadcasted_iota(jnp.int32, (tm, tm), 0)
        c = lax.broadcasted_iota(jnp.int32, (tm, tm), 1)
        tri_ref[...] = jnp.where(r <= c, 1.0, 0.0).astype(jnp.bfloat16)
        carry_ref[...] = jnp.zeros(carry_ref.shape, jnp.float32)

    e1 = route_ref[0:1, :]
    e2 = route_ref[1:2, :]
    expert = lax.broadcasted_iota(jnp.int32, (N_EXPERTS, tm), 0).astype(jnp.float32)
    hit1 = e1 == expert
    hit2 = e2 == expert
    hits = jnp.where(jnp.logical_or(hit1, hit2), 1.0, 0.0).astype(jnp.bfloat16)
    cum = jnp.dot(hits, tri_ref[...], preferred_element_type=jnp.float32) + carry_ref[...]
    rank1 = jnp.sum(jnp.where(hit1, cum - 1.0, 0.0), axis=0, keepdims=True)
    rank2 = jnp.sum(jnp.where(hit2, cum - 1.0, 0.0), axis=0, keepdims=True)
    row = lax.broadcasted_iota(jnp.int32, (ROUTE_ROWS, tm), 0)
    rank_ref[...] = jnp.where(row == 0, rank1, jnp.where(row == 1, rank2, 0.0))
    total = cum[:, tm - 1:tm]
    carry_ref[...] = total
    cnt_ref[...] = jnp.broadcast_to(total, cnt_ref.shape)


def _dispatch(route):
    n_tok = route.shape[1]
    tm = min(DISP_TM, n_tok)
    return pl.pallas_call(
        _dispatch_kernel,
        out_shape=(jax.ShapeDtypeStruct((ROUTE_ROWS, n_tok), jnp.float32),
                   jax.ShapeDtypeStruct((N_EXPERTS, LANES), jnp.float32)),
        grid=(n_tok // tm,),
        in_specs=[pl.BlockSpec((ROUTE_ROWS, tm), lambda i: (0, i))],
        out_specs=(pl.BlockSpec((ROUTE_ROWS, tm), lambda i: (0, i)),
                   pl.BlockSpec((N_EXPERTS, LANES), lambda i: (0, 0))),
        scratch_shapes=[pltpu.VMEM((tm, tm), jnp.bfloat16), pltpu.VMEM((N_EXPERTS, 1), jnp.float32)],
        compiler_params=pltpu.CompilerParams(dimension_semantics=("arbitrary",)),
        name="moe_ranks",
    )(route)


def _positions_kernel(route_ref, rank_ref, off_ref, pos_ref):
    tm = route_ref.shape[1]
    expert = lax.broadcasted_iota(jnp.int32, (N_EXPERTS, tm), 0).astype(jnp.float32)
    off = off_ref[...]
    start1 = jnp.sum(jnp.where(route_ref[0:1, :] == expert, off, 0.0), axis=0, keepdims=True)
    start2 = jnp.sum(jnp.where(route_ref[1:2, :] == expert, off, 0.0), axis=0, keepdims=True)
    row = lax.broadcasted_iota(jnp.int32, (ROUTE_ROWS, tm), 0)
    pos_ref[...] = jnp.where(row == 0, start1 + rank_ref[0:1, :], jnp.where(row == 1, start2 + rank_ref[1:2, :], 0.0))


def _positions(route, rank, seg_start):
    n_tok = route.shape[1]
    tm = min(DISP_TM, n_tok)
    blk = pl.BlockSpec((ROUTE_ROWS, tm), lambda i: (0, i))
    return pl.pallas_call(
        _positions_kernel,
        out_shape=jax.ShapeDtypeStruct((ROUTE_ROWS, n_tok), jnp.float32),
        grid=(n_tok // tm,),
        in_specs=[blk, blk, pl.BlockSpec((N_EXPERTS, 1), lambda i: (0, 0))],
        out_specs=blk,
        name="moe_positions",
    )(route, rank, seg_start)


def _sc_mesh():
    from jax.experimental.pallas import tpu_sc as plsc
    return plsc.VectorSubcoreMesh(core_axis_name="c", subcore_axis_name="s")


def _sc_worker_id():
    return lax.axis_index("s") * SC_CORES + lax.axis_index("c")


def _sc_scatter_rows(rows, pos, n_out):
    n_tok, width = rows.shape
    per_worker = n_tok // SC_WORKERS
    n_pairs = per_worker // (2 * SC_CHUNK)
    assert per_worker % (2 * SC_CHUNK) == 0
    buf = [pltpu.VMEM((SC_CHUNK,), jnp.int32), pltpu.VMEM((SC_CHUNK,), jnp.int32),
           pltpu.VMEM((SC_CHUNK, width), rows.dtype), pltpu.SemaphoreType.DMA, pltpu.SemaphoreType.DMA]

    @functools.partial(pl.kernel, mesh=_sc_mesh(), out_type=jax.ShapeDtypeStruct((n_out, width), rows.dtype),
                       scratch_types=buf + buf)
    def scatter(rows_hbm, pos_hbm, out_hbm, *scratch):
        base = _sc_worker_id() * per_worker
        buf_a, buf_b = scratch[:5], scratch[5:]

        def load(chunk, p1_v, p2_v, rows_v, sem_in, sem_out):
            off = base + chunk * SC_CHUNK
            pltpu.sync_copy(pos_hbm.at[pl.ds(off, SC_CHUNK)], p1_v)
            pltpu.sync_copy(pos_hbm.at[pl.ds(n_tok + off, SC_CHUNK)], p2_v)
            pltpu.async_copy(rows_hbm.at[pl.ds(off, SC_CHUNK)], rows_v, sem_in)

        def flush(chunk, p1_v, p2_v, rows_v, sem_in, sem_out):
            off = base + chunk * SC_CHUNK
            pltpu.make_async_copy(rows_hbm.at[pl.ds(off, SC_CHUNK)], rows_v, sem_in).wait()
            first = pltpu.async_copy(rows_v, out_hbm.at[p1_v], sem_out)
            second = pltpu.async_copy(rows_v, out_hbm.at[p2_v], sem_out)
            first.wait()
            second.wait()

        load(0, *buf_a)
        def body(j, carry):
            load(2 * j + 1, *buf_b)
            flush(2 * j, *buf_a)

            @pl.when(j + 1 < n_pairs)
            def _():
                load(2 * j + 2, *buf_a)

            flush(2 * j + 1, *buf_b)
            return carry

        lax.fori_loop(0, n_pairs, body, 0)

    return scatter(rows, pos)


def _sc_gather_rows(table, idx):
    n_idx = idx.shape[0]
    width = table.shape[1]
    per_worker = n_idx // SC_WORKERS
    n_pairs = per_worker // (2 * SC_CHUNK)
    assert per_worker % (2 * SC_CHUNK) == 0
    buf = [pltpu.VMEM((SC_CHUNK,), jnp.int32), pltpu.VMEM((SC_CHUNK, width), table.dtype), pltpu.SemaphoreType.DMA]

    @functools.partial(pl.kernel, mesh=_sc_mesh(), out_type=jax.ShapeDtypeStruct((n_idx, width), table.dtype),
                       scratch_types=buf + buf)
    def gather(table_hbm, idx_hbm, out_hbm, *scratch):
        base = _sc_worker_id() * per_worker
        buf_a, buf_b = scratch[:3], scratch[3:]

        def fetch(chunk, idx_v, rows_v, sem):
            pltpu.sync_copy(idx_hbm.at[pl.ds(base + chunk * SC_CHUNK, SC_CHUNK)], idx_v)
            pltpu.async_copy(table_hbm.at[idx_v], rows_v, sem)

        def drain(chunk, idx_v, rows_v, sem):
            pltpu.make_async_copy(table_hbm.at[idx_v], rows_v, sem).wait()
            pltpu.sync_copy(rows_v, out_hbm.at[pl.ds(base + chunk * SC_CHUNK, SC_CHUNK)])

        fetch(0, *buf_a)
        def body(j, carry):
            fetch(2 * j + 1, *buf_b)
            drain(2 * j, *buf_a)

            @pl.when(j + 1 < n_pairs)
            def _():
                fetch(2 * j + 2, *buf_a)

            drain(2 * j + 1, *buf_b)
            return carry

        lax.fori_loop(0, n_pairs, body, 0)

    return gather(table, idx)


def _experts_kernel(tile_expert_ref, n_used_ref, hs_ref, wg_ref, wu_ref, wd_ref, ys_ref):
    del tile_expert_ref

    @pl.when(pl.program_id(0) < n_used_ref[0])
    def _():
        lo, hi = _unpack_bf16_pair(hs_ref[...])
        lo = lo.astype(jnp.bfloat16)
        hi = hi.astype(jnp.bfloat16)
        bf = jnp.bfloat16
        gate = (jnp.dot(lo, wg_ref[0:PACK_W, :].astype(bf), preferred_element_type=jnp.float32)
                + jnp.dot(hi, wg_ref[PACK_W:D_MODEL, :].astype(bf), preferred_element_type=jnp.float32))
        up = (jnp.dot(lo, wu_ref[0:PACK_W, :].astype(bf), preferred_element_type=jnp.float32)
              + jnp.dot(hi, wu_ref[PACK_W:D_MODEL, :].astype(bf), preferred_element_type=jnp.float32))
        hid = ((gate * jax.nn.sigmoid(gate)) * up).astype(bf)
        y = jnp.dot(hid, wd_ref[...].astype(bf), preferred_element_type=jnp.float32)
        ys_ref[...] = _pack_bf16_pair(y[:, 0:PACK_W], y[:, PACK_W:D_MODEL])


def _experts(hs, tile_expert, n_used, wg, wu, wd, layer):
    n_rows = hs.shape[0]
    weights = lambda i, te, nu: (layer, te[i], 0, 0)
    return pl.pallas_call(
        _experts_kernel,
        out_shape=jax.ShapeDtypeStruct((n_rows, PACK_W), jnp.int32),
        grid_spec=pltpu.PrefetchScalarGridSpec(
            num_scalar_prefetch=2,
            grid=(n_rows // ROW_TILE,),
            in_specs=[pl.BlockSpec((ROW_TILE, PACK_W), lambda i, te, nu: (i, 0)),
                      pl.BlockSpec((None, None, D_MODEL, D_EXPERT), weights),
                      pl.BlockSpec((None, None, D_MODEL, D_EXPERT), weights),
                      pl.BlockSpec((None, None, D_EXPERT, D_MODEL), weights)],
            out_specs=pl.BlockSpec((ROW_TILE, PACK_W), lambda i, te, nu: (i, 0))),
        compiler_params=pltpu.CompilerParams(dimension_semantics=("arbitrary",), vmem_limit_bytes=VMEM_LIMIT),
        name="moe_experts",
    )(tile_expert, n_used, hs, wg, wu, wd)


COMB_TM = 1024


def _combine_kernel(x1_ref, y1_ref, y2_ref, cw_ref, gf_ref, out_ref, *, final_norm):
    c1 = cw_ref[:, 0:1]
    c2 = cw_ref[:, 1:2]
    lo1, hi1 = _unpack_bf16_pair(y1_ref[...])
    lo2, hi2 = _unpack_bf16_pair(y2_ref[...])
    y = jnp.concatenate([x1_ref[:, 0:PACK_W] + (c1 * lo1 + c2 * lo2),
                         x1_ref[:, PACK_W:D_MODEL] + (c1 * hi1 + c2 * hi2)], axis=1)
    if final_norm:
        ms = jnp.mean(y * y, axis=1, keepdims=True)
        y = y * lax.rsqrt(ms + RMS_EPS) * gf_ref[...]
    out_ref[...] = y


def _combine(x1, y12, cw, gf, final_norm):
    n_tok = x1.shape[0]
    tm = min(COMB_TM, n_tok)
    n_blk = n_tok // tm
    return pl.pallas_call(
        functools.partial(_combine_kernel, final_norm=final_norm),
        out_shape=jax.ShapeDtypeStruct((n_tok, D_MODEL), jnp.float32),
        grid=(n_blk,),
        in_specs=[pl.BlockSpec((tm, D_MODEL), lambda i: (i, 0)),
                  pl.BlockSpec((tm, PACK_W), lambda i: (i, 0)),
                  pl.BlockSpec((tm, PACK_W), lambda i: (n_blk + i, 0)),
                  pl.BlockSpec((tm, ROUTER_W), lambda i: (i, 0)),
                  pl.BlockSpec((1, D_MODEL), lambda i: (0, 0))],
        out_specs=pl.BlockSpec((tm, D_MODEL), lambda i: (i, 0)),
        compiler_params=pltpu.CompilerParams(dimension_semantics=("arbitrary",), vmem_limit_bytes=VMEM_LIMIT),
        name="moe_combine",
    )(x1, y12, y12, cw, gf)


def _sparse_moe(x1, h2p, route, cw, w_gate, w_up, w_down, layer, gf, final_norm):
    n_tok = x1.shape[0]
    n_rows = 2 * n_tok + N_EXPERTS * ROW_TILE
    rank, counts = _dispatch(route)
    padded = (counts[:, 0].astype(jnp.int32) + ROW_TILE - 1) // ROW_TILE * ROW_TILE
    ends = jnp.cumsum(padded)
    starts = ends - padded
    tile_start = jnp.arange(n_rows // ROW_TILE, dtype=jnp.int32) * ROW_TILE
    tile_expert = jnp.minimum(jnp.sum(tile_start[:, None] >= ends[None, :], axis=1), N_EXPERTS - 1).astype(jnp.int32)
    n_used = (ends[N_EXPERTS - 1:] // ROW_TILE).astype(jnp.int32)

    pos = _positions(route, rank, starts.astype(jnp.float32).reshape(N_EXPERTS, 1))
    pos12 = jnp.concatenate([pos[0], pos[1]]).astype(jnp.int32)
    hs = _sc_scatter_rows(h2p, pos12, n_rows)
    ys = _experts(hs, tile_expert, n_used, w_gate, w_up, w_down, layer)
    y12 = _sc_gather_rows(ys, pos12)
    return _combine(x1, y12, cw, gf, final_norm)


def _split_bf16(w):
    hi = w.astype(jnp.bfloat16)
    lo = (w - hi.astype(jnp.float32)).astype(jnp.bfloat16)
    return jnp.stack([hi, lo])


def _qkv_col_scale():
    s = np.ones((QKV_W,), np.float32)
    for group, dim in enumerate((HEAD_DIM, DIFF_DIM, HEAD_DIM, HEAD_DIM)):
        s[3 * group * GROUP_W:(3 * group + 1) * GROUP_W] = LOG2E * dim ** -0.5
    return s


def _forget_weights(w_f, b_f):
    rows = jnp.zeros((F_ROWS, D_MODEL), jnp.float32)
    bias = jnp.zeros((F_ROWS, 1), jnp.float32)
    for h in range(HEADS_PER_MIXER):
        r = (h // 2) * 8 + (h % 2)
        rows = rows.at[r].set(w_f[:, h])
        bias = bias.at[r, 0].set(b_f[h])
    return _split_bf16(rows), bias


def kernel(x, rel_bias, ln1, w_in, w_out, lam_q1, lam_k1, lam_q2, lam_k2, subln_g, b_forget,
           ln2, w_group, b_group, w_expert, b_expert, w_gate, w_up, w_down, ln_f):
    batch, seq, _ = x.shape
    depth = ln1.shape[0]
    tabs = _bias_tables(rel_bias)
    gf = ln_f.reshape(1, D_MODEL)
    col_scale = _qkv_col_scale()
    layers = []
    for l in range(depth):
        wf, bf = _forget_weights(w_in[l][:, QKV_W:], b_forget[l])
        w_qkv = (w_in[l][:, :QKV_W] * col_scale).astype(jnp.bfloat16)
        lamv = jnp.zeros((8, LANES), jnp.float32)
        lamv = lamv.at[0:4, 0:DIFF_DIM].set(jnp.stack([lam_q1[l], lam_k1[l], lam_q2[l], lam_k2[l]]))
        g_sub = jnp.tile(subln_g[l], 2).reshape(1, PAIR_W)
        w_router = jnp.zeros((D_MODEL, ROUTER_W), jnp.float32)
        w_router = w_router.at[:, :N_EXPERTS].set(w_expert[l]).at[:, N_EXPERTS:N_EXPERTS + N_GROUPS].set(w_group[l])
        b_router = jnp.zeros((1, ROUTER_W), jnp.float32)
        b_router = b_router.at[0, :N_EXPERTS].set(b_expert[l].reshape(-1)).at[0, N_EXPERTS:N_EXPERTS + N_GROUPS].set(b_group[l])
        layers.append((wf, bf, w_qkv, lamv, g_sub, _split_bf16(w_router), b_router, w_out[l].astype(jnp.bfloat16)))

    n_groups = 2 if batch % 2 == 0 else 1
    gb = batch // n_groups
    outs = []
    for grp in range(n_groups):
        xt = x[grp * gb:(grp + 1) * gb].reshape(gb * seq, D_MODEL)
        for l, (wf, bf, w_qkv, lamv, g_sub, wr, br, wo) in enumerate(layers):
            lambda_init = 0.8 - 0.6 * math.exp(-0.3 * l)
            qkv, lf = _norm_proj(xt, ln1[l].reshape(1, D_MODEL), w_qkv, wf, bf, gb, seq)
            c = _cumsum(lf)
            o_a = _dilated(qkv, tabs, gb, seq)
            o_b = _diff(qkv, tabs, lamv, g_sub, lambda_init, gb, seq)
            o_c = _moba(qkv, tabs, gb, seq)
            o_d = _fox(qkv, c, gb, seq)
            x1, h2p, route, cw = _outproj_router([o_a, o_b, o_c, o_d], wo, xt, ln2[l].reshape(1, D_MODEL), wr, br)
            xt = _sparse_moe(x1, h2p, route, cw, w_gate, w_up, w_down, l, gf, final_norm=(l == depth - 1))
        outs.append(xt.reshape(gb, seq, D_MODEL))
    return jnp.concatenate(outs, axis=0)
```

```python
import functools
import math
from typing import NamedTuple

import jax
import jax.numpy as jnp
import numpy as np
from jax import lax
from jax.experimental import pallas as pl
from jax.experimental.pallas import tpu as pltpu

D_MODEL = 1024
HEAD_DIM = 64
HEADS_PER_MIXER = 4
GROUP_W = HEADS_PER_MIXER * HEAD_DIM
QKV_W = 12 * GROUP_W
DIFF_DIM = HEAD_DIM // 2
MOBA_BLOCK = 256
MOBA_TOPK = 3
N_BUCKETS = 32
MAX_DISTANCE = 2048
N_BIAS_HEADS = 12
N_GROUPS = 4
EXPERTS_PER_GROUP = 8
N_EXPERTS = 32
D_EXPERT = 256
RMS_EPS = 1e-6
NEG = -1e30
LOG2E = math.log2(math.e)

LANES = 128
PAIR_W = 2 * HEAD_DIM
ACC_W = 2 * PAIR_W
TQ = 256
TK = 256
MAX_SUB = 8
STEP_TILES = 2
STEP_KEYS = STEP_TILES * TK
N_DIL_TILES = MAX_DISTANCE // TK + 1
FAR_TILE = 7
MIN_DIST = 1 - MAX_SUB
N_TAB_TILES = N_DIL_TILES + 1 - MIN_DIST
VMEM_LIMIT = 48 * 1024 * 1024

_NT = (((1,), (1,)), ((), ()))


def _t5_thresholds():
    d = np.arange(0, 4 * MAX_DISTANCE, dtype=np.int64)
    max_exact = N_BUCKETS // 2
    df = np.maximum(d, 1).astype(np.float32)
    large = max_exact + (np.log(df / np.float32(max_exact)) / np.float32(math.log(MAX_DISTANCE / max_exact))
                         * np.float32(N_BUCKETS - max_exact)).astype(np.int32)
    bucket = np.where(d < max_exact, d, np.minimum(large, N_BUCKETS - 1))
    return [int(np.argmax(bucket >= b)) for b in range(N_BUCKETS)]


_T5_THR = _t5_thresholds()
assert (FAR_TILE - 1) * TK + 1 >= _T5_THR[N_BUCKETS - 1]


def _lane_iota(shape):
    return lax.broadcasted_iota(jnp.int32, shape, len(shape) - 1)


def _bias_table_kernel(bias_ref, out_ref):
    h = pl.program_id(0)
    n = pl.program_id(1) + MIN_DIST
    i = lax.broadcasted_iota(jnp.int32, (TQ, TK), 0)
    j = lax.broadcasted_iota(jnp.int32, (TQ, TK), 1)
    d = n * TK + i - j
    is_dil = h < HEADS_PER_MIXER

    def store(val):
        mult = ((d <= 128).astype(jnp.int32)
                + jnp.logical_and((d & 3) == 0, d <= 512).astype(jnp.int32)
                + jnp.logical_and((d & 15) == 0, d <= 2048).astype(jnp.int32))
        logm = jnp.where(mult == 3, math.log(3.0), jnp.where(mult == 2, math.log(2.0), 0.0))
        val = val + jnp.where(is_dil, logm, 0.0)
        valid = jnp.logical_and(d >= 0, jnp.logical_or(mult > 0, jnp.logical_not(is_dil)))
        out_ref[...] = jnp.where(valid, val * LOG2E, NEG)

    @pl.when(n < 0)
    def _():
        out_ref[...] = jnp.full((TQ, TK), NEG, jnp.float32)

    @pl.when(n >= FAR_TILE)
    def _():
        store(jnp.full((TQ, TK), bias_ref[h, N_BUCKETS - 1], jnp.float32))

    @pl.when(jnp.logical_and(n >= 0, n < FAR_TILE))
    def _():
        val = jnp.full((TQ, TK), bias_ref[h, 0], jnp.float32)
        for b in range(1, N_BUCKETS):
            val = jnp.where(d >= _T5_THR[b], bias_ref[h, b], val)
        store(val)


def _bias_tables(rel_bias):
    bias_h = rel_bias.T.astype(jnp.float32)
    return pl.pallas_call(
        _bias_table_kernel,
        out_shape=jax.ShapeDtypeStruct((N_BIAS_HEADS, N_TAB_TILES, TQ, TK), jnp.float32),
        grid=(N_BIAS_HEADS, N_TAB_TILES),
        in_specs=[pl.BlockSpec(memory_space=pltpu.SMEM)],
        out_specs=pl.BlockSpec((None, None, TQ, TK), lambda h, n: (h, n, 0, 0)),
        name="bias_tables",
    )(bias_h)


PROJ_TM = 1024
PROJ_CW = 512
F_ROWS = 16


def _norm_proj_kernel(x_ref, g_ref, w_ref, wf_ref, bf_ref, qkv_ref, lf_ref):
    x = x_ref[...]
    ms = jnp.mean(x * x, axis=1, keepdims=True)
    h = x * lax.rsqrt(ms + RMS_EPS) * g_ref[...]
    hb = h.astype(jnp.bfloat16)
    for c in range(QKV_W // PROJ_CW):
        cols = slice(c * PROJ_CW, (c + 1) * PROJ_CW)
        qkv_ref[:, cols] = jnp.dot(hb, w_ref[:, cols], preferred_element_type=jnp.float32).astype(jnp.bfloat16)
    h_lo = (h - hb.astype(jnp.float32)).astype(jnp.bfloat16)
    wf_hi = wf_ref[0]
    wf_lo = wf_ref[1]
    z = (lax.dot_general(wf_hi, hb, _NT, preferred_element_type=jnp.float32)
         + lax.dot_general(wf_hi, h_lo, _NT, preferred_element_type=jnp.float32)
         + lax.dot_general(wf_lo, hb, _NT, preferred_element_type=jnp.float32))
    z = z + bf_ref[...]
    lf_ref[...] = jnp.minimum(z, 0.0) - jnp.log(1.0 + jnp.exp(-jnp.abs(z)))


def _norm_proj(x2d, g, w_qkv, wf, bf, batch, seq):
    n_tok = x2d.shape[0]
    tm = min(PROJ_TM, seq)
    per_b = seq // tm
    return pl.pallas_call(
        _norm_proj_kernel,
        out_shape=(jax.ShapeDtypeStruct((n_tok, QKV_W), jnp.bfloat16),
                   jax.ShapeDtypeStruct((batch * F_ROWS, seq), jnp.float32)),
        grid=(n_tok // tm,),
        in_specs=[pl.BlockSpec((tm, D_MODEL), lambda i: (i, 0)),
                  pl.BlockSpec((1, D_MODEL), lambda i: (0, 0)),
                  pl.BlockSpec((D_MODEL, QKV_W), lambda i: (0, 0)),
                  pl.BlockSpec((2, F_ROWS, D_MODEL), lambda i: (0, 0, 0)),
                  pl.BlockSpec((F_ROWS, 1), lambda i: (0, 0))],
        out_specs=(pl.BlockSpec((tm, QKV_W), lambda i: (i, 0)),
                   pl.BlockSpec((F_ROWS, tm), lambda i: (i // per_b, i % per_b))),
        compiler_params=pltpu.CompilerParams(dimension_semantics=("arbitrary",), vmem_limit_bytes=VMEM_LIMIT),
        name="norm_proj",
    )(x2d, g, w_qkv, wf, bf)


def _split3(x):
    x1 = x.astype(jnp.bfloat16)
    r1 = x - x1.astype(jnp.float32)
    x2 = r1.astype(jnp.bfloat16)
    x3 = (r1 - x2.astype(jnp.float32)).astype(jnp.bfloat16)
    return x1, x2, x3


def _cumsum_kernel(lf_ref, c_ref):
    rows, seq = lf_ref.shape
    r = lax.broadcasted_iota(jnp.int32, (LANES, LANES), 0)
    c = lax.broadcasted_iota(jnp.int32, (LANES, LANES), 1)
    upper = jnp.where(r <= c, 1.0, 0.0).astype(jnp.bfloat16)

    def body(i, carry):
        off = pl.multiple_of(i * LANES, LANES)
        x1, x2, x3 = _split3(lf_ref[:, pl.ds(off, LANES)])
        cs = (jnp.dot(x1, upper, preferred_element_type=jnp.float32)
              + jnp.dot(x2, upper, preferred_element_type=jnp.float32)
              + jnp.dot(x3, upper, preferred_element_type=jnp.float32)) + carry
        c_ref[:, pl.ds(off, LANES)] = cs * LOG2E
        return cs[:, LANES - 1:LANES]

    lax.fori_loop(0, seq // LANES, body, jnp.zeros((rows, 1), jnp.float32))


def _cumsum(lf):
    return pl.pallas_call(
        _cumsum_kernel,
        out_shape=jax.ShapeDtypeStruct(lf.shape, jnp.float32),
        name="forget_cumsum",
    )(lf)


class _Geom(NamedTuple):
    n_streams: int
    n_sub: int

    @property
    def q_rows(self):
        return self.n_sub * TQ

    def rows(self, sub, st):
        rb = sub * self.n_streams + st
        return slice(rb * TQ, (rb + 1) * TQ)

    def blocks(self):
        return [(sub, st, self.rows(sub, st)) for sub in range(self.n_sub) for st in range(self.n_streams)]

    def first_tile(self, blk, m):
        return self.n_sub * blk + self.n_sub - STEP_TILES * (m + 1)

    def dist(self, m, sub, w):
        return STEP_TILES * (m + 1) - self.n_sub + sub - w

    def n_steps(self, blk):
        return (self.n_sub // STEP_TILES) * (blk + 1)

    def own_steps(self):
        return self.n_sub // STEP_TILES


def _stack_masked_q(q, qs_ref, geom):
    lane = _lane_iota((TQ, PAIR_W))
    width = PAIR_W // geom.n_streams
    for sub, st, rows in geom.blocks():
        q_tile = q[sub * TQ:(sub + 1) * TQ]
        qs_ref[rows, 0:PAIR_W] = jnp.where(lane // width == st, q_tile, jnp.zeros_like(q_tile))


def _init_state(m_ref, acc_ref):
    m_ref[...] = jnp.full(m_ref.shape, NEG, jnp.float32)
    acc_ref[...] = jnp.zeros(acc_ref.shape, jnp.float32)


def _flash_step(rows, s, v_aug, m_ref, acc_ref):
    m_prev = m_ref[rows, :]
    m_new = jnp.maximum(m_prev, jnp.max(s, axis=1, keepdims=True))
    alpha = jnp.exp2(m_prev - m_new)
    p = jnp.concatenate([jnp.exp2(s[:, c * LANES:(c + 1) * LANES] - m_new) for c in range(s.shape[1] // LANES)],
                        axis=1).astype(v_aug.dtype)
    pv = jnp.dot(p, v_aug, preferred_element_type=jnp.float32)
    acc_ref[rows, :] = jnp.concatenate([alpha, alpha], axis=1) * acc_ref[rows, :] + pv
    m_ref[rows, :] = m_new


def _normalised(acc_ref, rows):
    return acc_ref[rows, 0:PAIR_W] / acc_ref[rows, PAIR_W:ACC_W]


def _table_bias(tab_ref, hh, geom, m, sub, cap):
    return jnp.concatenate([tab_ref[hh, jnp.minimum(geom.dist(m, sub, w), cap) - MIN_DIST]
                            for w in range(STEP_TILES)], axis=1)


def _step_rows(ref, geom, blk, m):
    k0 = pl.multiple_of(geom.first_tile(blk, m) * TK, STEP_KEYS)
    return ref[pl.ds(k0, STEP_KEYS), :]


def _make_attend(geom, blk, scratch, v_ref, keys, finish):
    qs_ref, m_ref, acc_ref, s_ref = scratch
    blocks = geom.blocks()
    last_step = geom.n_steps(blk) - 1

    def raw(rows, rhs):
        return lax.dot_general(qs_ref[rows, :], rhs, _NT, preferred_element_type=jnp.float32)

    def v_aug(m):
        v = _step_rows(v_ref, geom, blk, m)
        return jnp.concatenate([v, jnp.ones_like(v)], axis=1)

    def all_future(m, sub):
        return isinstance(m, int) and geom.dist(m, sub, 0) < 0

    def single(m):
        rhs = keys(m)
        va = v_aug(m)
        for sub, st, rows in blocks:
            if not all_future(m, sub):
                _flash_step(rows, finish(sub, st, raw(rows, rhs), m), va, m_ref, acc_ref)

    def step(m):
        va = v_aug(m)
        rhs_next = keys(jnp.minimum(m + 1, last_step))
        for sub, st, rows in blocks:
            if all_future(m, sub):
                s_ref[rows, :] = raw(rows, rhs_next)
                continue
            cur = s_ref[rows, :]
            s_ref[rows, :] = raw(rows, rhs_next)
            _flash_step(rows, finish(sub, st, cur, m), va, m_ref, acc_ref)

    def run(m0, count):
        peeled = [m0 + i for i in range(geom.own_steps())
                  if isinstance(m0, int) and any(all_future(m0 + i, sub) for sub in range(geom.n_sub))]

        @pl.when(count > 0)
        def _():
            rhs = keys(m0)
            for sub, _, rows in blocks:
                if not all_future(m0, sub):
                    s_ref[rows, :] = raw(rows, rhs)

        for m in peeled:
            pl.when(count > m - m0)(functools.partial(step, m))

        def body(i, carry):
            step(m0 + i)
            return carry

        lax.fori_loop(len(peeled), count, body, 0)

    return single, run


def _attn_call(kernel, group, qkv, extra_inputs, extra_specs, geom, batch, seq, *, q_width=PAIR_W,
               extra_scratch=(), name=None):
    assert seq % geom.q_rows == 0 and seq % STEP_KEYS == 0
    nq = seq // geom.q_rows
    qc, kc, vc = 6 * group, 6 * group + 2, 6 * group + 4
    rows = geom.n_sub * geom.n_streams * TQ
    return pl.pallas_call(
        functools.partial(kernel, geom=geom),
        out_shape=jax.ShapeDtypeStruct((batch * seq, GROUP_W), jnp.bfloat16),
        grid=(2, batch, nq),
        in_specs=[pl.BlockSpec((geom.q_rows, PAIR_W), lambda p, b, i: (b * nq + i, qc + p)),
                  pl.BlockSpec((seq, PAIR_W), lambda p, b, i: (b, kc + p)),
                  pl.BlockSpec((seq, PAIR_W), lambda p, b, i: (b, vc + p))] + list(extra_specs),
        out_specs=pl.BlockSpec((geom.q_rows, PAIR_W), lambda p, b, i: (b * nq + i, p)),
        scratch_shapes=[pltpu.VMEM((rows, q_width), jnp.bfloat16),
                        pltpu.VMEM((rows, LANES), jnp.float32),
                        pltpu.VMEM((rows, ACC_W), jnp.float32),
                        pltpu.VMEM((rows, STEP_KEYS), jnp.float32)
                        ] + list(extra_scratch),
        compiler_params=pltpu.CompilerParams(dimension_semantics=("arbitrary", "arbitrary", "arbitrary"),
                                             vmem_limit_bytes=VMEM_LIMIT),
        name=name,
    )(qkv, qkv, qkv, *extra_inputs)


def _tab_spec(first_head):
    return pl.BlockSpec((2, N_TAB_TILES, TQ, TK), lambda p, b, i: (first_head // 2 + p, 0, 0, 0))


def _merge_pair(o0, o1):
    return jnp.where(_lane_iota(o0.shape) < HEAD_DIM, o0, o1)


def _store_pair_outputs(o_ref, acc_ref, geom):
    for sub in range(geom.n_sub):
        o = _merge_pair(_normalised(acc_ref, geom.rows(sub, 0)), _normalised(acc_ref, geom.rows(sub, 1)))
        o_ref[sub * TQ:(sub + 1) * TQ, :] = o.astype(o_ref.dtype)


PAIR_GEOM = _Geom(n_streams=2, n_sub=4)
MOBA_GEOM = _Geom(n_streams=2, n_sub=MAX_SUB)
FOX_GEOM = _Geom(n_streams=2, n_sub=2)
DIFF_GEOM = _Geom(n_streams=4, n_sub=4)


def _dilated_kernel(q_ref, k_ref, v_ref, tab_ref, o_ref, qs_ref, m_ref, acc_ref, s_ref, *, geom):
    blk = pl.program_id(2)
    _stack_masked_q(q_ref[...], qs_ref, geom)
    _init_state(m_ref, acc_ref)
    _, run = _make_attend(
        geom, blk, (qs_ref, m_ref, acc_ref, s_ref), v_ref,
        keys=lambda m: _step_rows(k_ref, geom, blk, m),
        finish=lambda sub, st, raw, m: raw + _table_bias(tab_ref, st, geom, m, sub, N_DIL_TILES))
    reach = -(-(N_DIL_TILES - 1 + geom.n_sub) // STEP_TILES)
    run(0, jnp.minimum(geom.n_steps(blk), reach))
    _store_pair_outputs(o_ref, acc_ref, geom)


def _dilated(qkv, tabs, batch, seq):
    return _attn_call(_dilated_kernel, 0, qkv, [tabs], [_tab_spec(0)], PAIR_GEOM, batch, seq, name="dilated_attn")


def _diff_kernel(q_ref, k_ref, v_ref, tab_ref, lam_ref, g_ref, o_ref, qs_ref, m_ref, acc_ref, s_ref, *,
                 lambda_init, geom):
    blk = pl.program_id(2)
    _stack_masked_q(q_ref[...], qs_ref, geom)
    _init_state(m_ref, acc_ref)
    _, run = _make_attend(
        geom, blk, (qs_ref, m_ref, acc_ref, s_ref), v_ref,
        keys=lambda m: _step_rows(k_ref, geom, blk, m),
        finish=lambda sub, st, raw, m: raw + _table_bias(tab_ref, st // 2, geom, m, sub, FAR_TILE))
    run(0, geom.n_steps(blk))

    lamv = lam_ref[...]
    lam = (jnp.exp(jnp.sum(lamv[0:1] * lamv[1:2], axis=1, keepdims=True))
           - jnp.exp(jnp.sum(lamv[2:3] * lamv[3:4], axis=1, keepdims=True)) + lambda_init)
    first = _lane_iota((TQ, PAIR_W)) < HEAD_DIM
    for sub in range(geom.n_sub):
        part = [_normalised(acc_ref, geom.rows(sub, st)) for st in range(geom.n_streams)]
        o = _merge_pair(part[0] - lam * part[1], part[2] - lam * part[3])
        sq = o * o
        ms0 = jnp.sum(jnp.where(first, sq, 0.0), axis=1, keepdims=True)
        ms1 = jnp.sum(jnp.where(first, 0.0, sq), axis=1, keepdims=True)
        ms = jnp.where(first, ms0, ms1) * (1.0 / HEAD_DIM)
        y = (o * lax.rsqrt(ms + RMS_EPS) * g_ref[...]) * (1.0 - lambda_init)
        o_ref[sub * TQ:(sub + 1) * TQ, :] = y.astype(o_ref.dtype)


def _diff(qkv, tabs, lamv, g_sub, lambda_init, batch, seq):
    return _attn_call(functools.partial(_diff_kernel, lambda_init=lambda_init), 1, qkv, [tabs, lamv, g_sub],
                      [_tab_spec(4), pl.BlockSpec((8, LANES), lambda p, b, i: (0, 0)),
                       pl.BlockSpec((1, PAIR_W), lambda p, b, i: (0, 0))],
                      DIFF_GEOM, batch, seq, name="diff_attn")


def _moba_kernel(q_ref, k_ref, v_ref, tab_ref, o_ref, qs_ref, m_ref, acc_ref, s_ref, km_ref, *, geom):
    blk = pl.program_id(2)
    n_blk = k_ref.shape[0] // MOBA_BLOCK
    assert n_blk <= LANES and MOBA_BLOCK == TQ == TK

    @pl.when(blk == 0)
    def _():
        km_ref[...] = jnp.zeros(km_ref.shape, jnp.float32)
        for n in range(n_blk):
            km_ref[n:n + 1, :] = jnp.sum(k_ref[n * MOBA_BLOCK:(n + 1) * MOBA_BLOCK, :].astype(jnp.float32),
                                         axis=0, keepdims=True) * (1.0 / MOBA_BLOCK)

    _stack_masked_q(q_ref[...], qs_ref, geom)
    _init_state(m_ref, acc_ref)
    nb = -(-n_blk // 8) * 8
    km = km_ref[0:nb, :]
    km_hi = km.astype(jnp.bfloat16)
    km_lo = (km - km_hi.astype(jnp.float32)).astype(jnp.bfloat16)
    qs = qs_ref[:, 0:PAIR_W]
    gate = (lax.dot_general(km_hi, qs, _NT, preferred_element_type=jnp.float32)
            + lax.dot_general(km_lo, qs, _NT, preferred_element_type=jnp.float32))
    cand = lax.broadcasted_iota(jnp.int32, gate.shape, 0)
    cand_f = cand.astype(jnp.float32)
    own = geom.n_sub * blk + _lane_iota(gate.shape) // (geom.n_streams * TQ)
    g = jnp.where(cand < own, gate, NEG)
    sel = cand == own
    for _ in range(MOBA_TOPK):
        mx = jnp.max(g, axis=0, keepdims=True)
        first = jnp.min(jnp.where(g == mx, cand_f, 2.0 * LANES), axis=0, keepdims=True)
        pick = cand_f == first
        sel = jnp.logical_or(sel, jnp.logical_and(pick, mx > 0.5 * NEG))
        g = jnp.where(pick, -3e38, g)
    mask = jnp.where(sel, 0.0, NEG)
    if nb < LANES:
        mask = jnp.concatenate([mask, jnp.full((LANES - nb, mask.shape[1]), NEG, jnp.float32)], axis=0)
    qs_ref[:, PAIR_W:2 * PAIR_W] = mask.T.astype(jnp.bfloat16)

    def keys(m):
        k = _step_rows(k_ref, geom, blk, m)
        block_of_key = geom.first_tile(blk, m) + lax.broadcasted_iota(jnp.int32, (STEP_KEYS, LANES), 0) // TK
        onehot = jnp.where(_lane_iota((STEP_KEYS, LANES)) == block_of_key, 1.0, 0.0).astype(k.dtype)
        return jnp.concatenate([k, onehot], axis=1)

    _, run = _make_attend(
        geom, blk, (qs_ref, m_ref, acc_ref, s_ref), v_ref, keys=keys,
        finish=lambda sub, st, raw, m: raw + _table_bias(tab_ref, st, geom, m, sub, FAR_TILE))
    run(0, geom.n_steps(blk))
    _store_pair_outputs(o_ref, acc_ref, geom)


def _moba(qkv, tabs, batch, seq):
    return _attn_call(_moba_kernel, 2, qkv, [tabs], [_tab_spec(8)], MOBA_GEOM, batch, seq, q_width=2 * PAIR_W,
                      extra_scratch=[pltpu.VMEM((LANES, PAIR_W), jnp.float32)], name="moba_attn")


FOX_SKIP_BITS = 48.0


def _head_row_norms(x, hh):
    xf = x.astype(jnp.float32)
    sq = jnp.where(_lane_iota(xf.shape) // HEAD_DIM == hh, xf * xf, 0.0)
    return jnp.sqrt(jnp.max(jnp.sum(sq, axis=1, keepdims=True), axis=0, keepdims=True))


def _fox_kernel(q_ref, k_ref, v_ref, c_ref, o_ref, qs_ref, m_ref, acc_ref, s_ref, cq_ref, stat_ref, *, geom):
    blk = pl.program_id(2)
    n_tiles = k_ref.shape[0] // TK
    assert n_tiles <= LANES

    @pl.when(blk == 0)
    def _():
        stat_ref[...] = jnp.zeros(stat_ref.shape, jnp.float32)
        for j in range(n_tiles):
            k_tile = k_ref[j * TK:(j + 1) * TK, :]
            for hh in range(2):
                stat_ref[hh:hh + 1, j:j + 1] = _head_row_norms(k_tile, hh)
                stat_ref[2 + hh:3 + hh, j:j + 1] = jnp.min(c_ref[hh:hh + 1, j * TK:(j + 1) * TK], axis=1, keepdims=True)

    q = q_ref[...]
    _stack_masked_q(q, qs_ref, geom)
    _init_state(m_ref, acc_ref)
    q0 = pl.multiple_of(blk * geom.q_rows, geom.q_rows)
    row = lax.broadcasted_iota(jnp.int32, (TQ, TQ), 0)
    col = lax.broadcasted_iota(jnp.int32, (TQ, TQ), 1)
    for sub, hh, rows in geom.blocks():
        c_row = c_ref[hh:hh + 1, pl.ds(q0 + sub * TQ, TQ)]
        c_col = jnp.sum(jnp.where(row == col, c_row, 0.0), axis=1, keepdims=True)
        cq_ref[rows, :] = jnp.broadcast_to(c_col, (TQ, LANES))

    key_pos = lax.broadcasted_iota(jnp.int32, (TQ, STEP_KEYS), 1)
    query_pos = lax.broadcasted_iota(jnp.int32, (TQ, STEP_KEYS), 0)

    def finish(sub, hh, raw, m):
        k0 = pl.multiple_of(geom.first_tile(blk, m) * TK, STEP_KEYS)
        cq = cq_ref[geom.rows(sub, hh), :]
        s = raw + (jnp.concatenate([cq] * (STEP_KEYS // LANES), axis=1) - c_ref[hh:hh + 1, pl.ds(k0, STEP_KEYS)])
        if isinstance(m, int) and m < geom.own_steps():
            first_key = (geom.n_sub - STEP_TILES * (m + 1)) * TK
            s = jnp.where(key_pos + first_key <= query_pos + sub * TQ, s, NEG)
        return s

    single, run = _make_attend(geom, blk, (qs_ref, m_ref, acc_ref, s_ref), v_ref,
                               keys=lambda m: _step_rows(k_ref, geom, blk, m), finish=finish)
    for m in range(geom.own_steps()):
        single(m)

    tile = _lane_iota((1, LANES))
    needed = jnp.zeros((1, LANES), jnp.bool_)
    for hh in range(2):
        rows_h = [geom.rows(sub, hh) for sub in range(geom.n_sub)]
        m_min = functools.reduce(jnp.minimum, [jnp.min(m_ref[r, :], axis=0, keepdims=True) for r in rows_h])
        cq_max = functools.reduce(jnp.maximum, [jnp.max(cq_ref[r, :], axis=0, keepdims=True) for r in rows_h])
        bound = _head_row_norms(q, hh) * stat_ref[hh:hh + 1, :] + cq_max - stat_ref[2 + hh:3 + hh, :] + 1.0
        needed = jnp.logical_or(needed, bound >= m_min - FOX_SKIP_BITS)
    n_before = geom.n_sub * blk
    first_needed = jnp.min(jnp.where(jnp.logical_and(needed, tile < n_before), tile, n_before))
    run(geom.own_steps(), (n_before - first_needed + STEP_TILES - 1) // STEP_TILES)
    _store_pair_outputs(o_ref, acc_ref, geom)


def _fox(qkv, c, batch, seq):
    return _attn_call(_fox_kernel, 3, qkv, [c.reshape(batch, 2, 8, seq)],
                      [pl.BlockSpec((None, None, 8, seq), lambda p, b, i: (b, p, 0, 0))],
                      FOX_GEOM, batch, seq,
                      extra_scratch=[pltpu.VMEM((FOX_GEOM.n_sub * FOX_GEOM.n_streams * TQ, LANES), jnp.float32),
                                     pltpu.VMEM((8, LANES), jnp.float32)], name="fox_attn")


OUT_TM = 1024
ROUTER_W = LANES
ROUTE_ROWS = 8


PACK_W = D_MODEL // 2


def _pack_bf16_pair(lo, hi):
    lo_bits = lax.bitcast_convert_type(lo.astype(jnp.bfloat16).astype(jnp.float32), jnp.uint32)
    hi_bits = lax.bitcast_convert_type(hi.astype(jnp.bfloat16).astype(jnp.float32), jnp.uint32)
    word = lax.shift_right_logical(lo_bits, jnp.uint32(16)) | (hi_bits & jnp.uint32(0xFFFF0000))
    return lax.bitcast_convert_type(word, jnp.int32)


def _unpack_bf16_pair(word):
    bits = lax.bitcast_convert_type(word, jnp.uint32)
    lo = lax.bitcast_convert_type(lax.shift_left(bits, jnp.uint32(16)), jnp.float32)
    hi = lax.bitcast_convert_type(bits & jnp.uint32(0xFFFF0000), jnp.float32)
    return lo, hi


def _outproj_router_kernel(oa_ref, ob_ref, oc_ref, od_ref, wo_ref, x_ref, g_ref, wr_ref, br_ref,
                           x1_ref, h2_ref, route_ref, cw_ref):
    acc = x_ref[...]
    for m, o_ref in enumerate((oa_ref, ob_ref, oc_ref, od_ref)):
        acc = acc + jnp.dot(o_ref[...], wo_ref[m * GROUP_W:(m + 1) * GROUP_W, :], preferred_element_type=jnp.float32)
    x1_ref[...] = acc
    ms = jnp.mean(acc * acc, axis=1, keepdims=True)
    h = acc * lax.rsqrt(ms + RMS_EPS) * g_ref[...]
    hb = h.astype(jnp.bfloat16)
    h2_ref[...] = _pack_bf16_pair(h[:, 0:PACK_W], h[:, PACK_W:D_MODEL])
    h_lo = (h - hb.astype(jnp.float32)).astype(jnp.bfloat16)
    logits = (jnp.dot(hb, wr_ref[0], preferred_element_type=jnp.float32)
              + jnp.dot(h_lo, wr_ref[0], preferred_element_type=jnp.float32)
              + jnp.dot(hb, wr_ref[1], preferred_element_type=jnp.float32)) + br_ref[...]
    lane = _lane_iota(logits.shape).astype(jnp.float32)
    big = 2.0 * LANES
    gmask = jnp.logical_and(lane >= N_EXPERTS, lane < N_EXPERTS + N_GROUPS)
    gl = jnp.where(gmask, logits, NEG)
    gmax = jnp.max(gl, axis=1, keepdims=True)
    glane = jnp.min(jnp.where(gl == gmax, lane, big), axis=1, keepdims=True)
    gsum = jnp.sum(jnp.where(gmask, jnp.exp(gl - gmax), 0.0), axis=1, keepdims=True)
    g_w = 1.0 / gsum
    e0 = (glane - N_EXPERTS) * EXPERTS_PER_GROUP
    emask = jnp.logical_and(lane >= e0, lane < e0 + EXPERTS_PER_GROUP)
    el = jnp.where(emask, logits, NEG)
    v1 = jnp.max(el, axis=1, keepdims=True)
    i1 = jnp.min(jnp.where(el == v1, lane, big), axis=1, keepdims=True)
    el2 = jnp.where(lane == i1, NEG, el)
    v2 = jnp.max(el2, axis=1, keepdims=True)
    i2 = jnp.min(jnp.where(el2 == v2, lane, big), axis=1, keepdims=True)
    e2 = jnp.exp(v2 - v1)
    den = 1.0 + e2
    c1 = g_w / den
    c2 = g_w * e2 / den
    quad = jnp.where(lane == 0.0, i1, jnp.where(lane == 1.0, i2, jnp.where(lane == 2.0, c1,
                                                                             jnp.where(lane == 3.0, c2, 0.0))))
    cw_ref[...] = jnp.where(lane == 0.0, c1, jnp.where(lane == 1.0, c2, 0.0))
    route_ref[...] = quad.T[0:ROUTE_ROWS, :]


def _outproj_router(o_parts, w_out, x2d, g2, wr, br):
    n_tok = x2d.shape[0]
    tm = min(OUT_TM, n_tok)
    row = lambda i: (i, 0)
    fixed = lambda i: (0, 0)
    return pl.pallas_call(
        _outproj_router_kernel,
        out_shape=(jax.ShapeDtypeStruct((n_tok, D_MODEL), jnp.float32),
                   jax.ShapeDtypeStruct((n_tok, PACK_W), jnp.int32),
                   jax.ShapeDtypeStruct((ROUTE_ROWS, n_tok), jnp.float32),
                   jax.ShapeDtypeStruct((n_tok, ROUTER_W), jnp.float32)),
        grid=(n_tok // tm,),
        in_specs=[pl.BlockSpec((tm, GROUP_W), row)] * 4 + [
            pl.BlockSpec((D_MODEL, D_MODEL), fixed),
            pl.BlockSpec((tm, D_MODEL), row),
            pl.BlockSpec((1, D_MODEL), fixed),
            pl.BlockSpec((2, D_MODEL, ROUTER_W), lambda i: (0, 0, 0)),
            pl.BlockSpec((1, ROUTER_W), fixed)],
        out_specs=(pl.BlockSpec((tm, D_MODEL), row), pl.BlockSpec((tm, PACK_W), row),
                   pl.BlockSpec((ROUTE_ROWS, tm), lambda i: (0, i)), pl.BlockSpec((tm, ROUTER_W), row)),
        compiler_params=pltpu.CompilerParams(dimension_semantics=("arbitrary",), vmem_limit_bytes=VMEM_LIMIT),
        name="outproj_router",
    )(*o_parts, w_out, x2d, g2, wr, br)


ROW_TILE = 512
DISP_TM = 1024
SC_CORES = 2
SC_SUBCORES = 16
SC_WORKERS = SC_CORES * SC_SUBCORES
SC_CHUNK = 64


def _dispatch_kernel(route_ref, rank_ref, cnt_ref, tri_ref, carry_ref):
    step = pl.program_id(0)
    tm = route_ref.shape[1]

    @pl.when(step == 0)
    def _():
        r = lax.broadcasted_iota(jnp.int32, (tm, tm), 0)
        c = lax.broadcasted_iota(jnp.int32, (tm, tm), 1)
        tri_ref[...] = jnp.where(r <= c, 1.0, 0.0).astype(jnp.bfloat16)
        carry_ref[...] = jnp.zeros(carry_ref.shape, jnp.float32)

    e1 = route_ref[0:1, :]
    e2 = route_ref[1:2, :]
    expert = lax.broadcasted_iota(jnp.int32, (N_EXPERTS, tm), 0).astype(jnp.float32)
    hit1 = e1 == expert
    hit2 = e2 == expert
    hits = jnp.where(jnp.logical_or(hit1, hit2), 1.0, 0.0).astype(jnp.bfloat16)
    cum = jnp.dot(hits, tri_ref[...], preferred_element_type=jnp.float32) + carry_ref[...]
    rank1 = jnp.sum(jnp.where(hit1, cum - 1.0, 0.0), axis=0, keepdims=True)
    rank2 = jnp.sum(jnp.where(hit2, cum - 1.0, 0.0), axis=0, keepdims=True)
    row = lax.broadcasted_iota(jnp.int32, (ROUTE_ROWS, tm), 0)
    rank_ref[...] = jnp.where(row == 0, rank1, jnp.where(row == 1, rank2, 0.0))
    total = cum[:, tm - 1:tm]
    carry_ref[...] = total
    cnt_ref[...] = jnp.broadcast_to(total, cnt_ref.shape)


def _dispatch(route):
    n_tok = route.shape[1]
    tm = min(DISP_TM, n_tok)
    return pl.pallas_call(
        _dispatch_kernel,
        out_shape=(jax.ShapeDtypeStruct((ROUTE_ROWS, n_tok), jnp.float32),
                   jax.ShapeDtypeStruct((N_EXPERTS, LANES), jnp.float32)),
        grid=(n_tok // tm,),
        in_specs=[pl.BlockSpec((ROUTE_ROWS, tm), lambda i: (0, i))],
        out_specs=(pl.BlockSpec((ROUTE_ROWS, tm), lambda i: (0, i)),
                   pl.BlockSpec((N_EXPERTS, LANES), lambda i: (0, 0))),
        scratch_shapes=[pltpu.VMEM((tm, tm), jnp.bfloat16), pltpu.VMEM((N_EXPERTS, 1), jnp.float32)],
        compiler_params=pltpu.CompilerParams(dimension_semantics=("arbitrary",)),
        name="moe_ranks",
    )(route)


def _positions_kernel(route_ref, rank_ref, off_ref, pos_ref):
    tm = route_ref.shape[1]
    expert = lax.broadcasted_iota(jnp.int32, (N_EXPERTS, tm), 0).astype(jnp.float32)
    off = off_ref[...]
    start1 = jnp.sum(jnp.where(route_ref[0:1, :] == expert, off, 0.0), axis=0, keepdims=True)
    start2 = jnp.sum(jnp.where(route_ref[1:2, :] == expert, off, 0.0), axis=0, keepdims=True)
    row = lax.broadcasted_iota(jnp.int32, (ROUTE_ROWS, tm), 0)
    pos_ref[...] = jnp.where(row == 0, start1 + rank_ref[0:1, :], jnp.where(row == 1, start2 + rank_ref[1:2, :], 0.0))


def _positions(route, rank, seg_start):
    n_tok = route.shape[1]
    tm = min(DISP_TM, n_tok)
    blk = pl.BlockSpec((ROUTE_ROWS, tm), lambda i: (0, i))
    return pl.pallas_call(
        _positions_kernel,
        out_shape=jax.ShapeDtypeStruct((ROUTE_ROWS, n_tok), jnp.float32),
        grid=(n_tok // tm,),
        in_specs=[blk, blk, pl.BlockSpec((N_EXPERTS, 1), lambda i: (0, 0))],
        out_specs=blk,
        name="moe_positions",
    )(route, rank, seg_start)


def _sc_mesh():
    from jax.experimental.pallas import tpu_sc as plsc
    return plsc.VectorSubcoreMesh(core_axis_name="c", subcore_axis_name="s")


def _sc_worker_id():
    return lax.axis_index("s") * SC_CORES + lax.axis_index("c")


def _sc_scatter_rows(rows, pos, n_out):
    n_tok, width = rows.shape
    per_worker = n_tok // SC_WORKERS
    n_pairs = per_worker // (2 * SC_CHUNK)
    assert per_worker % (2 * SC_CHUNK) == 0
    buf = [pltpu.VMEM((SC_CHUNK,), jnp.int32), pltpu.VMEM((SC_CHUNK,), jnp.int32),
           pltpu.VMEM((SC_CHUNK, width), rows.dtype), pltpu.SemaphoreType.DMA, pltpu.SemaphoreType.DMA]

    @functools.partial(pl.kernel, mesh=_sc_mesh(), out_type=jax.ShapeDtypeStruct((n_out, width), rows.dtype),
                       scratch_types=buf + buf)
    def scatter(rows_hbm, pos_hbm, out_hbm, *scratch):
        base = _sc_worker_id() * per_worker
        buf_a, buf_b = scratch[:5], scratch[5:]

        def load(chunk, p1_v, p2_v, rows_v, sem_in, sem_out):
            off = base + chunk * SC_CHUNK
            pltpu.sync_copy(pos_hbm.at[pl.ds(off, SC_CHUNK)], p1_v)
            pltpu.sync_copy(pos_hbm.at[pl.ds(n_tok + off, SC_CHUNK)], p2_v)
            pltpu.async_copy(rows_hbm.at[pl.ds(off, SC_CHUNK)], rows_v, sem_in)

        def flush(chunk, p1_v, p2_v, rows_v, sem_in, sem_out):
            off = base + chunk * SC_CHUNK
            pltpu.make_async_copy(rows_hbm.at[pl.ds(off, SC_CHUNK)], rows_v, sem_in).wait()
            first = pltpu.async_copy(rows_v, out_hbm.at[p1_v], sem_out)
            second = pltpu.async_copy(rows_v, out_hbm.at[p2_v], sem_out)
            first.wait()
            second.wait()

        load(0, *buf_a)
        def body(j, carry):
            load(2 * j + 1, *buf_b)
            flush(2 * j, *buf_a)

            @pl.when(j + 1 < n_pairs)
            def _():
                load(2 * j + 2, *buf_a)

            flush(2 * j + 1, *buf_b)
            return carry

        lax.fori_loop(0, n_pairs, body, 0)

    return scatter(rows, pos)


def _sc_gather_rows(table, idx):
    n_idx = idx.shape[0]
    width = table.shape[1]
    per_worker = n_idx // SC_WORKERS
    n_pairs = per_worker // (2 * SC_CHUNK)
    assert per_worker % (2 * SC_CHUNK) == 0
    buf = [pltpu.VMEM((SC_CHUNK,), jnp.int32), pltpu.VMEM((SC_CHUNK, width), table.dtype), pltpu.SemaphoreType.DMA]

    @functools.partial(pl.kernel, mesh=_sc_mesh(), out_type=jax.ShapeDtypeStruct((n_idx, width), table.dtype),
                       scratch_types=buf + buf)
    def gather(table_hbm, idx_hbm, out_hbm, *scratch):
        base = _sc_worker_id() * per_worker
        buf_a, buf_b = scratch[:3], scratch[3:]

        def fetch(chunk, idx_v, rows_v, sem):
            pltpu.sync_copy(idx_hbm.at[pl.ds(base + chunk * SC_CHUNK, SC_CHUNK)], idx_v)
            pltpu.async_copy(table_hbm.at[idx_v], rows_v, sem)

        def drain(chunk, idx_v, rows_v, sem):
            pltpu.make_async_copy(table_hbm.at[idx_v], rows_v, sem).wait()
            pltpu.sync_copy(rows_v, out_hbm.at[pl.ds(base + chunk * SC_CHUNK, SC_CHUNK)])

        fetch(0, *buf_a)
        def body(j, carry):
            fetch(2 * j + 1, *buf_b)
            drain(2 * j, *buf_a)

            @pl.when(j + 1 < n_pairs)
            def _():
                fetch(2 * j + 2, *buf_a)

            drain(2 * j + 1, *buf_b)
            return carry

        lax.fori_loop(0, n_pairs, body, 0)

    return gather(table, idx)


def _experts_kernel(tile_expert_ref, n_used_ref, hs_ref, wg_ref, wu_ref, wd_ref, ys_ref):
    del tile_expert_ref

    @pl.when(pl.program_id(0) < n_used_ref[0])
    def _():
        lo, hi = _unpack_bf16_pair(hs_ref[...])
        lo = lo.astype(jnp.bfloat16)
        hi = hi.astype(jnp.bfloat16)
        bf = jnp.bfloat16
        gate = (jnp.dot(lo, wg_ref[0:PACK_W, :].astype(bf), preferred_element_type=jnp.float32)
                + jnp.dot(hi, wg_ref[PACK_W:D_MODEL, :].astype(bf), preferred_element_type=jnp.float32))
        up = (jnp.dot(lo, wu_ref[0:PACK_W, :].astype(bf), preferred_element_type=jnp.float32)
              + jnp.dot(hi, wu_ref[PACK_W:D_MODEL, :].astype(bf), preferred_element_type=jnp.float32))
        hid = ((gate * jax.nn.sigmoid(gate)) * up).astype(bf)
        y = jnp.dot(hid, wd_ref[...].astype(bf), preferred_element_type=jnp.float32)
        ys_ref[...] = _pack_bf16_pair(y[:, 0:PACK_W], y[:, PACK_W:D_MODEL])


def _experts(hs, tile_expert, n_used, wg, wu, wd, layer):
    n_rows = hs.shape[0]
    weights = lambda i, te, nu: (layer, te[i], 0, 0)
    return pl.pallas_call(
        _experts_kernel,
        out_shape=jax.ShapeDtypeStruct((n_rows, PACK_W), jnp.int32),
        grid_spec=pltpu.PrefetchScalarGridSpec(
            num_scalar_prefetch=2,
            grid=(n_rows // ROW_TILE,),
            in_specs=[pl.BlockSpec((ROW_TILE, PACK_W), lambda i, te, nu: (i, 0)),
                      pl.BlockSpec((None, None, D_MODEL, D_EXPERT), weights),
                      pl.BlockSpec((None, None, D_MODEL, D_EXPERT), weights),
                      pl.BlockSpec((None, None, D_EXPERT, D_MODEL), weights)],
            out_specs=pl.BlockSpec((ROW_TILE, PACK_W), lambda i, te, nu: (i, 0))),
        compiler_params=pltpu.CompilerParams(dimension_semantics=("arbitrary",), vmem_limit_bytes=VMEM_LIMIT),
        name="moe_experts",
    )(tile_expert, n_used, hs, wg, wu, wd)


COMB_TM = 1024


def _combine_kernel(x1_ref, y1_ref, y2_ref, cw_ref, gf_ref, out_ref, *, final_norm):
    c1 = cw_ref[:, 0:1]
    c2 = cw_ref[:, 1:2]
    lo1, hi1 = _unpack_bf16_pair(y1_ref[...])
    lo2, hi2 = _unpack_bf16_pair(y2_ref[...])
    y = jnp.concatenate([x1_ref[:, 0:PACK_W] + (c1 * lo1 + c2 * lo2),
                         x1_ref[:, PACK_W:D_MODEL] + (c1 * hi1 + c2 * hi2)], axis=1)
    if final_norm:
        ms = jnp.mean(y * y, axis=1, keepdims=True)
        y = y * lax.rsqrt(ms + RMS_EPS) * gf_ref[...]
    out_ref[...] = y


def _combine(x1, y12, cw, gf, final_norm):
    n_tok = x1.shape[0]
    tm = min(COMB_TM, n_tok)
    n_blk = n_tok // tm
    return pl.pallas_call(
        functools.partial(_combine_kernel, final_norm=final_norm),
        out_shape=jax.ShapeDtypeStruct((n_tok, D_MODEL), jnp.float32),
        grid=(n_blk,),
        in_specs=[pl.BlockSpec((tm, D_MODEL), lambda i: (i, 0)),
                  pl.BlockSpec((tm, PACK_W), lambda i: (i, 0)),
                  pl.BlockSpec((tm, PACK_W), lambda i: (n_blk + i, 0)),
                  pl.BlockSpec((tm, ROUTER_W), lambda i: (i, 0)),
                  pl.BlockSpec((1, D_MODEL), lambda i: (0, 0))],
        out_specs=pl.BlockSpec((tm, D_MODEL), lambda i: (i, 0)),
        compiler_params=pltpu.CompilerParams(dimension_semantics=("arbitrary",), vmem_limit_bytes=VMEM_LIMIT),
        name="moe_combine",
    )(x1, y12, y12, cw, gf)


def _sparse_moe(x1, h2p, route, cw, w_gate, w_up, w_down, layer, gf, final_norm):
    n_tok = x1.shape[0]
    n_rows = 2 * n_tok + N_EXPERTS * ROW_TILE
    rank, counts = _dispatch(route)
    padded = (counts[:, 0].astype(jnp.int32) + ROW_TILE - 1) // ROW_TILE * ROW_TILE
    ends = jnp.cumsum(padded)
    starts = ends - padded
    tile_start = jnp.arange(n_rows // ROW_TILE, dtype=jnp.int32) * ROW_TILE
    tile_expert = jnp.minimum(jnp.sum(tile_start[:, None] >= ends[None, :], axis=1), N_EXPERTS - 1).astype(jnp.int32)
    n_used = (ends[N_EXPERTS - 1:] // ROW_TILE).astype(jnp.int32)

    pos = _positions(route, rank, starts.astype(jnp.float32).reshape(N_EXPERTS, 1))
    pos12 = jnp.concatenate([pos[0], pos[1]]).astype(jnp.int32)
    hs = _sc_scatter_rows(h2p, pos12, n_rows)
    ys = _experts(hs, tile_expert, n_used, w_gate, w_up, w_down, layer)
    y12 = _sc_gather_rows(ys, pos12)
    return _combine(x1, y12, cw, gf, final_norm)


def _split_bf16(w):
    hi = w.astype(jnp.bfloat16)
    lo = (w - hi.astype(jnp.float32)).astype(jnp.bfloat16)
    return jnp.stack([hi, lo])


def _qkv_col_scale():
    s = np.ones((QKV_W,), np.float32)
    for group, dim in enumerate((HEAD_DIM, DIFF_DIM, HEAD_DIM, HEAD_DIM)):
        s[3 * group * GROUP_W:(3 * group + 1) * GROUP_W] = LOG2E * dim ** -0.5
    return s


def _forget_weights(w_f, b_f):
    rows = jnp.zeros((F_ROWS, D_MODEL), jnp.float32)
    bias = jnp.zeros((F_ROWS, 1), jnp.float32)
    for h in range(HEADS_PER_MIXER):
        r = (h // 2) * 8 + (h % 2)
        rows = rows.at[r].set(w_f[:, h])
        bias = bias.at[r, 0].set(b_f[h])
    return _split_bf16(rows), bias


def kernel(x, rel_bias, ln1, w_in, w_out, lam_q1, lam_k1, lam_q2, lam_k2, subln_g, b_forget,
           ln2, w_group, b_group, w_expert, b_expert, w_gate, w_up, w_down, ln_f):
    batch, seq, _ = x.shape
    depth = ln1.shape[0]
    tabs = _bias_tables(rel_bias)
    xt = x.reshape(batch * seq, D_MODEL)
    gf = ln_f.reshape(1, D_MODEL)
    col_scale = _qkv_col_scale()
    for l in range(depth):
        lambda_init = 0.8 - 0.6 * math.exp(-0.3 * l)
        wf, bf = _forget_weights(w_in[l][:, QKV_W:], b_forget[l])
        w_qkv = (w_in[l][:, :QKV_W] * col_scale).astype(jnp.bfloat16)
        qkv, lf = _norm_proj(xt, ln1[l].reshape(1, D_MODEL), w_qkv, wf, bf, batch, seq)
        c = _cumsum(lf)

        o_a = _dilated(qkv, tabs, batch, seq)
        lamv = jnp.zeros((8, LANES), jnp.float32)
        lamv = lamv.at[0:4, 0:DIFF_DIM].set(jnp.stack([lam_q1[l], lam_k1[l], lam_q2[l], lam_k2[l]]))
        g_sub = jnp.tile(subln_g[l], 2).reshape(1, PAIR_W)
        o_b = _diff(qkv, tabs, lamv, g_sub, lambda_init, batch, seq)
        o_c = _moba(qkv, tabs, batch, seq)
        o_d = _fox(qkv, c, batch, seq)

        w_router = jnp.zeros((D_MODEL, ROUTER_W), jnp.float32)
        w_router = w_router.at[:, :N_EXPERTS].set(w_expert[l]).at[:, N_EXPERTS:N_EXPERTS + N_GROUPS].set(w_group[l])
        b_router = jnp.zeros((1, ROUTER_W), jnp.float32)
        b_router = b_router.at[0, :N_EXPERTS].set(b_expert[l].reshape(-1)).at[0, N_EXPERTS:N_EXPERTS + N_GROUPS].set(b_group[l])
        x1, h2p, route, cw = _outproj_router([o_a, o_b, o_c, o_d], w_out[l].astype(jnp.bfloat16), xt,
                                             ln2[l].reshape(1, D_MODEL), _split_bf16(w_router), b_router)
        xt = _sparse_moe(x1, h2p, route, cw, w_gate, w_up, w_down, l, gf, final_norm=(l == depth - 1))
    return xt.reshape(batch, seq, D_MODEL)
```

```python
import functools
import math
from typing import NamedTuple

import jax
import jax.numpy as jnp
import numpy as np
from jax import lax
from jax.experimental import pallas as pl
from jax.experimental.pallas import tpu as pltpu

D_MODEL = 1024
HEAD_DIM = 64
HEADS_PER_MIXER = 4
GROUP_W = HEADS_PER_MIXER * HEAD_DIM
QKV_W = 12 * GROUP_W
DIFF_DIM = HEAD_DIM // 2
MOBA_BLOCK = 256
MOBA_TOPK = 3
N_BUCKETS = 32
MAX_DISTANCE = 2048
N_BIAS_HEADS = 12
N_GROUPS = 4
EXPERTS_PER_GROUP = 8
N_EXPERTS = 32
D_EXPERT = 256
RMS_EPS = 1e-6
NEG = -1e30
LOG2E = math.log2(math.e)

LANES = 128
PAIR_W = 2 * HEAD_DIM
ACC_W = 2 * PAIR_W
TQ = 256
TK = 256
MAX_SUB = 8
STEP_TILES = 2
STEP_KEYS = STEP_TILES * TK
N_DIL_TILES = MAX_DISTANCE // TK + 1
FAR_TILE = 7
MIN_DIST = -1
N_TAB_TILES = N_DIL_TILES + 1 - MIN_DIST
VMEM_LIMIT = 48 * 1024 * 1024

_NT = (((1,), (1,)), ((), ()))


def _t5_thresholds():
    d = np.arange(0, 4 * MAX_DISTANCE, dtype=np.int64)
    max_exact = N_BUCKETS // 2
    df = np.maximum(d, 1).astype(np.float32)
    large = max_exact + (np.log(df / np.float32(max_exact)) / np.float32(math.log(MAX_DISTANCE / max_exact))
                         * np.float32(N_BUCKETS - max_exact)).astype(np.int32)
    bucket = np.where(d < max_exact, d, np.minimum(large, N_BUCKETS - 1))
    return [int(np.argmax(bucket >= b)) for b in range(N_BUCKETS)]


_T5_THR = _t5_thresholds()
assert (FAR_TILE - 1) * TK + 1 >= _T5_THR[N_BUCKETS - 1]


def _lane_iota(shape):
    return lax.broadcasted_iota(jnp.int32, shape, len(shape) - 1)


def _bias_table_kernel(bias_ref, out_ref):
    h = pl.program_id(0)
    n = pl.program_id(1) + MIN_DIST
    i = lax.broadcasted_iota(jnp.int32, (TQ, TK), 0)
    j = lax.broadcasted_iota(jnp.int32, (TQ, TK), 1)
    d = n * TK + i - j
    is_dil = h < HEADS_PER_MIXER

    def store(val):
        mult = ((d <= 128).astype(jnp.int32)
                + jnp.logical_and((d & 3) == 0, d <= 512).astype(jnp.int32)
                + jnp.logical_and((d & 15) == 0, d <= 2048).astype(jnp.int32))
        logm = jnp.where(mult == 3, math.log(3.0), jnp.where(mult == 2, math.log(2.0), 0.0))
        val = val + jnp.where(is_dil, logm, 0.0)
        valid = jnp.logical_and(d >= 0, jnp.logical_or(mult > 0, jnp.logical_not(is_dil)))
        out_ref[...] = jnp.where(valid, val * LOG2E, NEG)

    @pl.when(n < 0)
    def _():
        out_ref[...] = jnp.full((TQ, TK), NEG, jnp.float32)

    @pl.when(n >= FAR_TILE)
    def _():
        store(jnp.full((TQ, TK), bias_ref[h, N_BUCKETS - 1], jnp.float32))

    @pl.when(jnp.logical_and(n >= 0, n < FAR_TILE))
    def _():
        val = jnp.full((TQ, TK), bias_ref[h, 0], jnp.float32)
        for b in range(1, N_BUCKETS):
            val = jnp.where(d >= _T5_THR[b], bias_ref[h, b], val)
        store(val)


def _bias_tables(rel_bias):
    bias_h = rel_bias.T.astype(jnp.float32)
    return pl.pallas_call(
        _bias_table_kernel,
        out_shape=jax.ShapeDtypeStruct((N_BIAS_HEADS, N_TAB_TILES, TQ, TK), jnp.float32),
        grid=(N_BIAS_HEADS, N_TAB_TILES),
        in_specs=[pl.BlockSpec(memory_space=pltpu.SMEM)],
        out_specs=pl.BlockSpec((None, None, TQ, TK), lambda h, n: (h, n, 0, 0)),
        name="bias_tables",
    )(bias_h)


PROJ_TM = 1024
PROJ_CW = 512
F_ROWS = 16


def _norm_proj_kernel(x_ref, g_ref, w_ref, wf_ref, bf_ref, qkv_ref, lf_ref):
    x = x_ref[...]
    ms = jnp.mean(x * x, axis=1, keepdims=True)
    h = x * lax.rsqrt(ms + RMS_EPS) * g_ref[...]
    hb = h.astype(jnp.bfloat16)
    for c in range(QKV_W // PROJ_CW):
        cols = slice(c * PROJ_CW, (c + 1) * PROJ_CW)
        qkv_ref[:, cols] = jnp.dot(hb, w_ref[:, cols], preferred_element_type=jnp.float32).astype(jnp.bfloat16)
    h_lo = (h - hb.astype(jnp.float32)).astype(jnp.bfloat16)
    wf_hi = wf_ref[0]
    wf_lo = wf_ref[1]
    z = (lax.dot_general(wf_hi, hb, _NT, preferred_element_type=jnp.float32)
         + lax.dot_general(wf_hi, h_lo, _NT, preferred_element_type=jnp.float32)
         + lax.dot_general(wf_lo, hb, _NT, preferred_element_type=jnp.float32))
    z = z + bf_ref[...]
    lf_ref[...] = jnp.minimum(z, 0.0) - jnp.log(1.0 + jnp.exp(-jnp.abs(z)))


def _norm_proj(x2d, g, w_qkv, wf, bf, batch, seq):
    n_tok = x2d.shape[0]
    tm = min(PROJ_TM, seq)
    per_b = seq // tm
    return pl.pallas_call(
        _norm_proj_kernel,
        out_shape=(jax.ShapeDtypeStruct((n_tok, QKV_W), jnp.bfloat16),
                   jax.ShapeDtypeStruct((batch * F_ROWS, seq), jnp.float32)),
        grid=(n_tok // tm,),
        in_specs=[pl.BlockSpec((tm, D_MODEL), lambda i: (i, 0)),
                  pl.BlockSpec((1, D_MODEL), lambda i: (0, 0)),
                  pl.BlockSpec((D_MODEL, QKV_W), lambda i: (0, 0)),
                  pl.BlockSpec((2, F_ROWS, D_MODEL), lambda i: (0, 0, 0)),
                  pl.BlockSpec((F_ROWS, 1), lambda i: (0, 0))],
        out_specs=(pl.BlockSpec((tm, QKV_W), lambda i: (i, 0)),
                   pl.BlockSpec((F_ROWS, tm), lambda i: (i // per_b, i % per_b))),
        compiler_params=pltpu.CompilerParams(dimension_semantics=("arbitrary",), vmem_limit_bytes=VMEM_LIMIT),
        name="norm_proj",
    )(x2d, g, w_qkv, wf, bf)


def _split3(x):
    x1 = x.astype(jnp.bfloat16)
    r1 = x - x1.astype(jnp.float32)
    x2 = r1.astype(jnp.bfloat16)
    x3 = (r1 - x2.astype(jnp.float32)).astype(jnp.bfloat16)
    return x1, x2, x3


def _cumsum_kernel(lf_ref, c_ref):
    rows, seq = lf_ref.shape
    r = lax.broadcasted_iota(jnp.int32, (LANES, LANES), 0)
    c = lax.broadcasted_iota(jnp.int32, (LANES, LANES), 1)
    upper = jnp.where(r <= c, 1.0, 0.0).astype(jnp.bfloat16)

    def body(i, carry):
        off = pl.multiple_of(i * LANES, LANES)
        x1, x2, x3 = _split3(lf_ref[:, pl.ds(off, LANES)])
        cs = (jnp.dot(x1, upper, preferred_element_type=jnp.float32)
              + jnp.dot(x2, upper, preferred_element_type=jnp.float32)
              + jnp.dot(x3, upper, preferred_element_type=jnp.float32)) + carry
        c_ref[:, pl.ds(off, LANES)] = cs * LOG2E
        return cs[:, LANES - 1:LANES]

    lax.fori_loop(0, seq // LANES, body, jnp.zeros((rows, 1), jnp.float32))


def _cumsum(lf):
    return pl.pallas_call(
        _cumsum_kernel,
        out_shape=jax.ShapeDtypeStruct(lf.shape, jnp.float32),
        name="forget_cumsum",
    )(lf)


class _Geom(NamedTuple):
    n_streams: int
    n_sub: int

    @property
    def q_rows(self):
        return self.n_sub * TQ

    def rows(self, sub, st):
        rb = sub * self.n_streams + st
        return slice(rb * TQ, (rb + 1) * TQ)

    def blocks(self):
        return [(sub, st, self.rows(sub, st)) for sub in range(self.n_sub) for st in range(self.n_streams)]

    def first_tile(self, blk, m):
        return self.n_sub * blk + self.n_sub - STEP_TILES * (m + 1)

    def dist(self, m, sub, w):
        return STEP_TILES * (m + 1) - self.n_sub + sub - w

    def n_steps(self, blk):
        return (self.n_sub // STEP_TILES) * (blk + 1)

    def own_steps(self):
        return self.n_sub // STEP_TILES


def _stack_masked_q(q, qs_ref, geom):
    lane = _lane_iota((TQ, PAIR_W))
    width = PAIR_W // geom.n_streams
    for sub, st, rows in geom.blocks():
        q_tile = q[sub * TQ:(sub + 1) * TQ]
        qs_ref[rows, 0:PAIR_W] = jnp.where(lane // width == st, q_tile, jnp.zeros_like(q_tile))


def _init_state(m_ref, acc_ref):
    m_ref[...] = jnp.full(m_ref.shape, NEG, jnp.float32)
    acc_ref[...] = jnp.zeros(acc_ref.shape, jnp.float32)


def _flash_step(rows, s, v_aug, m_ref, acc_ref):
    m_prev = m_ref[rows, :]
    m_new = jnp.maximum(m_prev, jnp.max(s, axis=1, keepdims=True))
    alpha = jnp.exp2(m_prev - m_new)
    p = jnp.concatenate([jnp.exp2(s[:, c * LANES:(c + 1) * LANES] - m_new) for c in range(s.shape[1] // LANES)],
                        axis=1).astype(v_aug.dtype)
    pv = jnp.dot(p, v_aug, preferred_element_type=jnp.float32)
    acc_ref[rows, :] = jnp.concatenate([alpha, alpha], axis=1) * acc_ref[rows, :] + pv
    m_ref[rows, :] = m_new


def _normalised(acc_ref, rows):
    return acc_ref[rows, 0:PAIR_W] / acc_ref[rows, PAIR_W:ACC_W]


def _table_bias(tab_ref, hh, geom, m, sub, cap):
    return jnp.concatenate([tab_ref[hh, jnp.clip(geom.dist(m, sub, w), MIN_DIST, cap) - MIN_DIST]
                            for w in range(STEP_TILES)], axis=1)


def _step_rows(ref, geom, blk, m):
    k0 = pl.multiple_of(geom.first_tile(blk, m) * TK, STEP_KEYS)
    return ref[pl.ds(k0, STEP_KEYS), :]


def _make_attend(geom, blk, scratch, v_ref, keys, finish):
    qs_ref, m_ref, acc_ref, s_ref = scratch
    blocks = geom.blocks()
    last_step = geom.n_steps(blk) - 1

    def raw(rows, rhs):
        return lax.dot_general(qs_ref[rows, :], rhs, _NT, preferred_element_type=jnp.float32)

    def v_aug(m):
        v = _step_rows(v_ref, geom, blk, m)
        return jnp.concatenate([v, jnp.ones_like(v)], axis=1)

    def all_future(m, sub):
        return isinstance(m, int) and geom.dist(m, sub, 0) < 0

    def single(m):
        rhs = keys(m)
        va = v_aug(m)
        for sub, st, rows in blocks:
            if not all_future(m, sub):
                _flash_step(rows, finish(sub, st, raw(rows, rhs), m), va, m_ref, acc_ref)

    def step(m):
        va = v_aug(m)
        rhs_next = keys(jnp.minimum(m + 1, last_step))
        for sub, st, rows in blocks:
            if all_future(m, sub):
                s_ref[rows, :] = raw(rows, rhs_next)
                continue
            cur = s_ref[rows, :]
            s_ref[rows, :] = raw(rows, rhs_next)
            _flash_step(rows, finish(sub, st, cur, m), va, m_ref, acc_ref)

    def run(m0, count):
        peeled = [m0 + i for i in range(geom.own_steps())
                  if isinstance(m0, int) and any(all_future(m0 + i, sub) for sub in range(geom.n_sub))]

        @pl.when(count > 0)
        def _():
            rhs = keys(m0)
            for sub, _, rows in blocks:
                if not all_future(m0, sub):
                    s_ref[rows, :] = raw(rows, rhs)

        for m in peeled:
            pl.when(count > m - m0)(functools.partial(step, m))

        def body(i, carry):
            step(m0 + i)
            return carry

        lax.fori_loop(len(peeled), count, body, 0)

    return single, run


def _attn_call(kernel, group, qkv, extra_inputs, extra_specs, geom, batch, seq, *, q_width=PAIR_W,
               extra_scratch=(), name=None):
    assert seq % geom.q_rows == 0 and seq % STEP_KEYS == 0
    nq = seq // geom.q_rows
    qc, kc, vc = 6 * group, 6 * group + 2, 6 * group + 4
    rows = geom.n_sub * geom.n_streams * TQ
    return pl.pallas_call(
        functools.partial(kernel, geom=geom),
        out_shape=jax.ShapeDtypeStruct((batch * seq, GROUP_W), jnp.bfloat16),
        grid=(2, batch, nq),
        in_specs=[pl.BlockSpec((geom.q_rows, PAIR_W), lambda p, b, i: (b * nq + i, qc + p)),
                  pl.BlockSpec((seq, PAIR_W), lambda p, b, i: (b, kc + p)),
                  pl.BlockSpec((seq, PAIR_W), lambda p, b, i: (b, vc + p))] + list(extra_specs),
        out_specs=pl.BlockSpec((geom.q_rows, PAIR_W), lambda p, b, i: (b * nq + i, p)),
        scratch_shapes=[pltpu.VMEM((rows, q_width), jnp.bfloat16),
                        pltpu.VMEM((rows, LANES), jnp.float32),
                        pltpu.VMEM((rows, ACC_W), jnp.float32),
                        pltpu.VMEM((rows, STEP_KEYS), jnp.float32)
                        ] + list(extra_scratch),
        compiler_params=pltpu.CompilerParams(dimension_semantics=("arbitrary", "arbitrary", "arbitrary"),
                                             vmem_limit_bytes=VMEM_LIMIT),
        name=name,
    )(qkv, qkv, qkv, *extra_inputs)


def _tab_spec(first_head):
    return pl.BlockSpec((2, N_TAB_TILES, TQ, TK), lambda p, b, i: (first_head // 2 + p, 0, 0, 0))


def _merge_pair(o0, o1):
    return jnp.where(_lane_iota(o0.shape) < HEAD_DIM, o0, o1)


def _store_pair_outputs(o_ref, acc_ref, geom):
    for sub in range(geom.n_sub):
        o = _merge_pair(_normalised(acc_ref, geom.rows(sub, 0)), _normalised(acc_ref, geom.rows(sub, 1)))
        o_ref[sub * TQ:(sub + 1) * TQ, :] = o.astype(o_ref.dtype)


PAIR_GEOM = _Geom(n_streams=2, n_sub=4)
MOBA_GEOM = _Geom(n_streams=2, n_sub=MAX_SUB)
FOX_GEOM = _Geom(n_streams=2, n_sub=2)
DIFF_GEOM = _Geom(n_streams=4, n_sub=4)


def _dilated_kernel(q_ref, k_ref, v_ref, tab_ref, o_ref, qs_ref, m_ref, acc_ref, s_ref, *, geom):
    blk = pl.program_id(2)
    _stack_masked_q(q_ref[...], qs_ref, geom)
    _init_state(m_ref, acc_ref)
    _, run = _make_attend(
        geom, blk, (qs_ref, m_ref, acc_ref, s_ref), v_ref,
        keys=lambda m: _step_rows(k_ref, geom, blk, m),
        finish=lambda sub, st, raw, m: raw + _table_bias(tab_ref, st, geom, m, sub, N_DIL_TILES))
    reach = -(-(N_DIL_TILES - 1 + geom.n_sub) // STEP_TILES)
    run(0, jnp.minimum(geom.n_steps(blk), reach))
    _store_pair_outputs(o_ref, acc_ref, geom)


def _dilated(qkv, tabs, batch, seq):
    return _attn_call(_dilated_kernel, 0, qkv, [tabs], [_tab_spec(0)], PAIR_GEOM, batch, seq, name="dilated_attn")


def _diff_kernel(q_ref, k_ref, v_ref, tab_ref, lam_ref, g_ref, o_ref, qs_ref, m_ref, acc_ref, s_ref, *,
                 lambda_init, geom):
    blk = pl.program_id(2)
    _stack_masked_q(q_ref[...], qs_ref, geom)
    _init_state(m_ref, acc_ref)
    _, run = _make_attend(
        geom, blk, (qs_ref, m_ref, acc_ref, s_ref), v_ref,
        keys=lambda m: _step_rows(k_ref, geom, blk, m),
        finish=lambda sub, st, raw, m: raw + _table_bias(tab_ref, st // 2, geom, m, sub, FAR_TILE))
    run(0, geom.n_steps(blk))

    lamv = lam_ref[...]
    lam = (jnp.exp(jnp.sum(lamv[0:1] * lamv[1:2], axis=1, keepdims=True))
           - jnp.exp(jnp.sum(lamv[2:3] * lamv[3:4], axis=1, keepdims=True)) + lambda_init)
    first = _lane_iota((TQ, PAIR_W)) < HEAD_DIM
    for sub in range(geom.n_sub):
        part = [_normalised(acc_ref, geom.rows(sub, st)) for st in range(geom.n_streams)]
        o = _merge_pair(part[0] - lam * part[1], part[2] - lam * part[3])
        sq = o * o
        ms0 = jnp.sum(jnp.where(first, sq, 0.0), axis=1, keepdims=True)
        ms1 = jnp.sum(jnp.where(first, 0.0, sq), axis=1, keepdims=True)
        ms = jnp.where(first, ms0, ms1) * (1.0 / HEAD_DIM)
        y = (o * lax.rsqrt(ms + RMS_EPS) * g_ref[...]) * (1.0 - lambda_init)
        o_ref[sub * TQ:(sub + 1) * TQ, :] = y.astype(o_ref.dtype)


def _diff(qkv, tabs, lamv, g_sub, lambda_init, batch, seq):
    return _attn_call(functools.partial(_diff_kernel, lambda_init=lambda_init), 1, qkv, [tabs, lamv, g_sub],
                      [_tab_spec(4), pl.BlockSpec((8, LANES), lambda p, b, i: (0, 0)),
                       pl.BlockSpec((1, PAIR_W), lambda p, b, i: (0, 0))],
                      DIFF_GEOM, batch, seq, name="diff_attn")


def _moba_kernel(q_ref, k_ref, v_ref, tab_ref, o_ref, qs_ref, m_ref, acc_ref, s_ref, km_ref, *, geom):
    blk = pl.program_id(2)
    n_blk = k_ref.shape[0] // MOBA_BLOCK
    assert n_blk <= LANES and MOBA_BLOCK == TQ == TK

    @pl.when(blk == 0)
    def _():
        km_ref[...] = jnp.zeros(km_ref.shape, jnp.float32)
        for n in range(n_blk):
            km_ref[n:n + 1, :] = jnp.sum(k_ref[n * MOBA_BLOCK:(n + 1) * MOBA_BLOCK, :].astype(jnp.float32),
                                         axis=0, keepdims=True) * (1.0 / MOBA_BLOCK)

    _stack_masked_q(q_ref[...], qs_ref, geom)
    _init_state(m_ref, acc_ref)
    nb = -(-n_blk // 8) * 8
    km = km_ref[0:nb, :]
    km_hi = km.astype(jnp.bfloat16)
    km_lo = (km - km_hi.astype(jnp.float32)).astype(jnp.bfloat16)
    qs = qs_ref[:, 0:PAIR_W]
    gate = (lax.dot_general(km_hi, qs, _NT, preferred_element_type=jnp.float32)
            + lax.dot_general(km_lo, qs, _NT, preferred_element_type=jnp.float32))
    cand = lax.broadcasted_iota(jnp.int32, gate.shape, 0)
    cand_f = cand.astype(jnp.float32)
    own = geom.n_sub * blk + _lane_iota(gate.shape) // (geom.n_streams * TQ)
    g = jnp.where(cand < own, gate, NEG)
    sel = cand == own
    for _ in range(MOBA_TOPK):
        mx = jnp.max(g, axis=0, keepdims=True)
        first = jnp.min(jnp.where(g == mx, cand_f, 2.0 * LANES), axis=0, keepdims=True)
        pick = cand_f == first
        sel = jnp.logical_or(sel, jnp.logical_and(pick, mx > 0.5 * NEG))
        g = jnp.where(pick, -3e38, g)
    mask = jnp.where(sel, 0.0, NEG)
    if nb < LANES:
        mask = jnp.concatenate([mask, jnp.full((LANES - nb, mask.shape[1]), NEG, jnp.float32)], axis=0)
    qs_ref[:, PAIR_W:2 * PAIR_W] = mask.T.astype(jnp.bfloat16)

    def keys(m):
        k = _step_rows(k_ref, geom, blk, m)
        block_of_key = geom.first_tile(blk, m) + lax.broadcasted_iota(jnp.int32, (STEP_KEYS, LANES), 0) // TK
        onehot = jnp.where(_lane_iota((STEP_KEYS, LANES)) == block_of_key, 1.0, 0.0).astype(k.dtype)
        return jnp.concatenate([k, onehot], axis=1)

    _, run = _make_attend(
        geom, blk, (qs_ref, m_ref, acc_ref, s_ref), v_ref, keys=keys,
        finish=lambda sub, st, raw, m: raw + _table_bias(tab_ref, st, geom, m, sub, FAR_TILE))
    run(0, geom.n_steps(blk))
    _store_pair_outputs(o_ref, acc_ref, geom)


def _moba(qkv, tabs, batch, seq):
    return _attn_call(_moba_kernel, 2, qkv, [tabs], [_tab_spec(8)], MOBA_GEOM, batch, seq, q_width=2 * PAIR_W,
                      extra_scratch=[pltpu.VMEM((LANES, PAIR_W), jnp.float32)], name="moba_attn")


FOX_SKIP_BITS = 48.0


def _head_row_norms(x, hh):
    xf = x.astype(jnp.float32)
    sq = jnp.where(_lane_iota(xf.shape) // HEAD_DIM == hh, xf * xf, 0.0)
    return jnp.sqrt(jnp.max(jnp.sum(sq, axis=1, keepdims=True), axis=0, keepdims=True))


def _fox_kernel(q_ref, k_ref, v_ref, c_ref, o_ref, qs_ref, m_ref, acc_ref, s_ref, cq_ref, stat_ref, *, geom):
    blk = pl.program_id(2)
    n_tiles = k_ref.shape[0] // TK
    assert n_tiles <= LANES

    @pl.when(blk == 0)
    def _():
        stat_ref[...] = jnp.zeros(stat_ref.shape, jnp.float32)
        for j in range(n_tiles):
            k_tile = k_ref[j * TK:(j + 1) * TK, :]
            for hh in range(2):
                stat_ref[hh:hh + 1, j:j + 1] = _head_row_norms(k_tile, hh)
                stat_ref[2 + hh:3 + hh, j:j + 1] = jnp.min(c_ref[hh:hh + 1, j * TK:(j + 1) * TK], axis=1, keepdims=True)

    q = q_ref[...]
    _stack_masked_q(q, qs_ref, geom)
    _init_state(m_ref, acc_ref)
    q0 = pl.multiple_of(blk * geom.q_rows, geom.q_rows)
    row = lax.broadcasted_iota(jnp.int32, (TQ, TQ), 0)
    col = lax.broadcasted_iota(jnp.int32, (TQ, TQ), 1)
    for sub, hh, rows in geom.blocks():
        c_row = c_ref[hh:hh + 1, pl.ds(q0 + sub * TQ, TQ)]
        c_col = jnp.sum(jnp.where(row == col, c_row, 0.0), axis=1, keepdims=True)
        cq_ref[rows, :] = jnp.broadcast_to(c_col, (TQ, LANES))

    key_pos = lax.broadcasted_iota(jnp.int32, (TQ, STEP_KEYS), 1)
    query_pos = lax.broadcasted_iota(jnp.int32, (TQ, STEP_KEYS), 0)

    def finish(sub, hh, raw, m):
        k0 = pl.multiple_of(geom.first_tile(blk, m) * TK, STEP_KEYS)
        cq = cq_ref[geom.rows(sub, hh), :]
        s = raw + (jnp.concatenate([cq] * (STEP_KEYS // LANES), axis=1) - c_ref[hh:hh + 1, pl.ds(k0, STEP_KEYS)])
        if isinstance(m, int) and m < geom.own_steps():
            first_key = (geom.n_sub - STEP_TILES * (m + 1)) * TK
            s = jnp.where(key_pos + first_key <= query_pos + sub * TQ, s, NEG)
        return s

    single, run = _make_attend(geom, blk, (qs_ref, m_ref, acc_ref, s_ref), v_ref,
                               keys=lambda m: _step_rows(k_ref, geom, blk, m), finish=finish)
    for m in range(geom.own_steps()):
        single(m)

    tile = _lane_iota((1, LANES))
    needed = jnp.zeros((1, LANES), jnp.bool_)
    for hh in range(2):
        rows_h = [geom.rows(sub, hh) for sub in range(geom.n_sub)]
        m_min = functools.reduce(jnp.minimum, [jnp.min(m_ref[r, :], axis=0, keepdims=True) for r in rows_h])
        cq_max = functools.reduce(jnp.maximum, [jnp.max(cq_ref[r, :], axis=0, keepdims=True) for r in rows_h])
        bound = _head_row_norms(q, hh) * stat_ref[hh:hh + 1, :] + cq_max - stat_ref[2 + hh:3 + hh, :] + 1.0
        needed = jnp.logical_or(needed, bound >= m_min - FOX_SKIP_BITS)
    n_before = geom.n_sub * blk
    first_needed = jnp.min(jnp.where(jnp.logical_and(needed, tile < n_before), tile, n_before))
    run(geom.own_steps(), (n_before - first_needed + STEP_TILES - 1) // STEP_TILES)
    _store_pair_outputs(o_ref, acc_ref, geom)


def _fox(qkv, c, batch, seq):
    return _attn_call(_fox_kernel, 3, qkv, [c.reshape(batch, 2, 8, seq)],
                      [pl.BlockSpec((None, None, 8, seq), lambda p, b, i: (b, p, 0, 0))],
                      FOX_GEOM, batch, seq,
                      extra_scratch=[pltpu.VMEM((FOX_GEOM.n_sub * FOX_GEOM.n_streams * TQ, LANES), jnp.float32),
                                     pltpu.VMEM((8, LANES), jnp.float32)], name="fox_attn")


OUT_TM = 1024
ROUTER_W = LANES
ROUTE_ROWS = 8


PACK_W = D_MODEL // 2


def _pack_bf16_pair(lo, hi):
    lo_bits = lax.bitcast_convert_type(lo.astype(jnp.bfloat16).astype(jnp.float32), jnp.uint32)
    hi_bits = lax.bitcast_convert_type(hi.astype(jnp.bfloat16).astype(jnp.float32), jnp.uint32)
    word = lax.shift_right_logical(lo_bits, jnp.uint32(16)) | (hi_bits & jnp.uint32(0xFFFF0000))
    return lax.bitcast_convert_type(word, jnp.int32)


def _unpack_bf16_pair(word):
    bits = lax.bitcast_convert_type(word, jnp.uint32)
    lo = lax.bitcast_convert_type(lax.shift_left(bits, jnp.uint32(16)), jnp.float32)
    hi = lax.bitcast_convert_type(bits & jnp.uint32(0xFFFF0000), jnp.float32)
    return lo, hi


def _outproj_router_kernel(oa_ref, ob_ref, oc_ref, od_ref, wo_ref, x_ref, g_ref, wr_ref, br_ref,
                           x1_ref, h2_ref, route_ref, cw_ref):
    acc = x_ref[...]
    for m, o_ref in enumerate((oa_ref, ob_ref, oc_ref, od_ref)):
        acc = acc + jnp.dot(o_ref[...], wo_ref[m * GROUP_W:(m + 1) * GROUP_W, :], preferred_element_type=jnp.float32)
    x1_ref[...] = acc
    ms = jnp.mean(acc * acc, axis=1, keepdims=True)
    h = acc * lax.rsqrt(ms + RMS_EPS) * g_ref[...]
    hb = h.astype(jnp.bfloat16)
    h2_ref[...] = _pack_bf16_pair(h[:, 0:PACK_W], h[:, PACK_W:D_MODEL])
    h_lo = (h - hb.astype(jnp.float32)).astype(jnp.bfloat16)
    logits = (jnp.dot(hb, wr_ref[0], preferred_element_type=jnp.float32)
              + jnp.dot(h_lo, wr_ref[0], preferred_element_type=jnp.float32)
              + jnp.dot(hb, wr_ref[1], preferred_element_type=jnp.float32)) + br_ref[...]
    lane = _lane_iota(logits.shape).astype(jnp.float32)
    big = 2.0 * LANES
    gmask = jnp.logical_and(lane >= N_EXPERTS, lane < N_EXPERTS + N_GROUPS)
    gl = jnp.where(gmask, logits, NEG)
    gmax = jnp.max(gl, axis=1, keepdims=True)
    glane = jnp.min(jnp.where(gl == gmax, lane, big), axis=1, keepdims=True)
    gsum = jnp.sum(jnp.where(gmask, jnp.exp(gl - gmax), 0.0), axis=1, keepdims=True)
    g_w = 1.0 / gsum
    e0 = (glane - N_EXPERTS) * EXPERTS_PER_GROUP
    emask = jnp.logical_and(lane >= e0, lane < e0 + EXPERTS_PER_GROUP)
    el = jnp.where(emask, logits, NEG)
    v1 = jnp.max(el, axis=1, keepdims=True)
    i1 = jnp.min(jnp.where(el == v1, lane, big), axis=1, keepdims=True)
    el2 = jnp.where(lane == i1, NEG, el)
    v2 = jnp.max(el2, axis=1, keepdims=True)
    i2 = jnp.min(jnp.where(el2 == v2, lane, big), axis=1, keepdims=True)
    e2 = jnp.exp(v2 - v1)
    den = 1.0 + e2
    c1 = g_w / den
    c2 = g_w * e2 / den
    quad = jnp.where(lane == 0.0, i1, jnp.where(lane == 1.0, i2, jnp.where(lane == 2.0, c1,
                                                                             jnp.where(lane == 3.0, c2, 0.0))))
    cw_ref[...] = jnp.where(lane == 0.0, c1, jnp.where(lane == 1.0, c2, 0.0))
    route_ref[...] = quad.T[0:ROUTE_ROWS, :]


def _outproj_router(o_parts, w_out, x2d, g2, wr, br):
    n_tok = x2d.shape[0]
    tm = min(OUT_TM, n_tok)
    row = lambda i: (i, 0)
    fixed = lambda i: (0, 0)
    return pl.pallas_call(
        _outproj_router_kernel,
        out_shape=(jax.ShapeDtypeStruct((n_tok, D_MODEL), jnp.float32),
                   jax.ShapeDtypeStruct((n_tok, PACK_W), jnp.int32),
                   jax.ShapeDtypeStruct((ROUTE_ROWS, n_tok), jnp.float32),
                   jax.ShapeDtypeStruct((n_tok, ROUTER_W), jnp.float32)),
        grid=(n_tok // tm,),
        in_specs=[pl.BlockSpec((tm, GROUP_W), row)] * 4 + [
            pl.BlockSpec((D_MODEL, D_MODEL), fixed),
            pl.BlockSpec((tm, D_MODEL), row),
            pl.BlockSpec((1, D_MODEL), fixed),
            pl.BlockSpec((2, D_MODEL, ROUTER_W), lambda i: (0, 0, 0)),
            pl.BlockSpec((1, ROUTER_W), fixed)],
        out_specs=(pl.BlockSpec((tm, D_MODEL), row), pl.BlockSpec((tm, PACK_W), row),
                   pl.BlockSpec((ROUTE_ROWS, tm), lambda i: (0, i)), pl.BlockSpec((tm, ROUTER_W), row)),
        compiler_params=pltpu.CompilerParams(dimension_semantics=("arbitrary",), vmem_limit_bytes=VMEM_LIMIT),
        name="outproj_router",
    )(*o_parts, w_out, x2d, g2, wr, br)


ROW_TILE = 512
DISP_TM = 1024
SC_CORES = 2
SC_SUBCORES = 16
SC_WORKERS = SC_CORES * SC_SUBCORES
SC_CHUNK = 64


def _dispatch_kernel(route_ref, rank_ref, cnt_ref, tri_ref, carry_ref):
    step = pl.program_id(0)
    tm = route_ref.shape[1]

    @pl.when(step == 0)
    def _():
        r = lax.broadcasted_iota(jnp.int32, (tm, tm), 0)
        c = lax.broadcasted_iota(jnp.int32, (tm, tm), 1)
        tri_ref[...] = jnp.where(r <= c, 1.0, 0.0).astype(jnp.bfloat16)
        carry_ref[...] = jnp.zeros(carry_ref.shape, jnp.float32)

    e1 = route_ref[0:1, :]
    e2 = route_ref[1:2, :]
    expert = lax.broadcasted_iota(jnp.int32, (N_EXPERTS, tm), 0).astype(jnp.float32)
    hit1 = e1 == expert
    hit2 = e2 == expert
    hits = jnp.where(jnp.logical_or(hit1, hit2), 1.0, 0.0).astype(jnp.bfloat16)
    cum = jnp.dot(hits, tri_ref[...], preferred_element_type=jnp.float32) + carry_ref[...]
    rank1 = jnp.sum(jnp.where(hit1, cum - 1.0, 0.0), axis=0, keepdims=True)
    rank2 = jnp.sum(jnp.where(hit2, cum - 1.0, 0.0), axis=0, keepdims=True)
    row = lax.broadcasted_iota(jnp.int32, (ROUTE_ROWS, tm), 0)
    rank_ref[...] = jnp.where(row == 0, rank1, jnp.where(row == 1, rank2, 0.0))
    total = cum[:, tm - 1:tm]
    carry_ref[...] = total
    cnt_ref[...] = jnp.broadcast_to(total, cnt_ref.shape)


def _dispatch(route):
    n_tok = route.shape[1]
    tm = min(DISP_TM, n_tok)
    return pl.pallas_call(
        _dispatch_kernel,
        out_shape=(jax.ShapeDtypeStruct((ROUTE_ROWS, n_tok), jnp.float32),
                   jax.ShapeDtypeStruct((N_EXPERTS, LANES), jnp.float32)),
        grid=(n_tok // tm,),
        in_specs=[pl.BlockSpec((ROUTE_ROWS, tm), lambda i: (0, i))],
        out_specs=(pl.BlockSpec((ROUTE_ROWS, tm), lambda i: (0, i)),
                   pl.BlockSpec((N_EXPERTS, LANES), lambda i: (0, 0))),
        scratch_shapes=[pltpu.VMEM((tm, tm), jnp.bfloat16), pltpu.VMEM((N_EXPERTS, 1), jnp.float32)],
        compiler_params=pltpu.CompilerParams(dimension_semantics=("arbitrary",)),
        name="moe_ranks",
    )(route)


def _positions_kernel(route_ref, rank_ref, off_ref, pos_ref):
    tm = route_ref.shape[1]
    expert = lax.broadcasted_iota(jnp.int32, (N_EXPERTS, tm), 0).astype(jnp.float32)
    off = off_ref[...]
    start1 = jnp.sum(jnp.where(route_ref[0:1, :] == expert, off, 0.0), axis=0, keepdims=True)
    start2 = jnp.sum(jnp.where(route_ref[1:2, :] == expert, off, 0.0), axis=0, keepdims=True)
    row = lax.broadcasted_iota(jnp.int32, (ROUTE_ROWS, tm), 0)
    pos_ref[...] = jnp.where(row == 0, start1 + rank_ref[0:1, :], jnp.where(row == 1, start2 + rank_ref[1:2, :], 0.0))


def _positions(route, rank, seg_start):
    n_tok = route.shape[1]
    tm = min(DISP_TM, n_tok)
    blk = pl.BlockSpec((ROUTE_ROWS, tm), lambda i: (0, i))
    return pl.pallas_call(
        _positions_kernel,
        out_shape=jax.ShapeDtypeStruct((ROUTE_ROWS, n_tok), jnp.float32),
        grid=(n_tok // tm,),
        in_specs=[blk, blk, pl.BlockSpec((N_EXPERTS, 1), lambda i: (0, 0))],
        out_specs=blk,
        name="moe_positions",
    )(route, rank, seg_start)


def _sc_mesh():
    from jax.experimental.pallas import tpu_sc as plsc
    return plsc.VectorSubcoreMesh(core_axis_name="c", subcore_axis_name="s")


def _sc_worker_id():
    return lax.axis_index("s") * SC_CORES + lax.axis_index("c")


def _sc_scatter_rows(rows, pos, n_out):
    n_tok, width = rows.shape
    per_worker = n_tok // SC_WORKERS
    n_pairs = per_worker // (2 * SC_CHUNK)
    assert per_worker % (2 * SC_CHUNK) == 0
    buf = [pltpu.VMEM((SC_CHUNK,), jnp.int32), pltpu.VMEM((SC_CHUNK,), jnp.int32),
           pltpu.VMEM((SC_CHUNK, width), rows.dtype), pltpu.SemaphoreType.DMA, pltpu.SemaphoreType.DMA]

    @functools.partial(pl.kernel, mesh=_sc_mesh(), out_type=jax.ShapeDtypeStruct((n_out, width), rows.dtype),
                       scratch_types=buf + buf)
    def scatter(rows_hbm, pos_hbm, out_hbm, *scratch):
        base = _sc_worker_id() * per_worker
        buf_a, buf_b = scratch[:5], scratch[5:]

        def load(chunk, p1_v, p2_v, rows_v, sem_in, sem_out):
            off = base + chunk * SC_CHUNK
            pltpu.sync_copy(pos_hbm.at[pl.ds(off, SC_CHUNK)], p1_v)
            pltpu.sync_copy(pos_hbm.at[pl.ds(n_tok + off, SC_CHUNK)], p2_v)
            pltpu.async_copy(rows_hbm.at[pl.ds(off, SC_CHUNK)], rows_v, sem_in)

        def flush(chunk, p1_v, p2_v, rows_v, sem_in, sem_out):
            off = base + chunk * SC_CHUNK
            pltpu.make_async_copy(rows_hbm.at[pl.ds(off, SC_CHUNK)], rows_v, sem_in).wait()
            first = pltpu.async_copy(rows_v, out_hbm.at[p1_v], sem_out)
            second = pltpu.async_copy(rows_v, out_hbm.at[p2_v], sem_out)
            first.wait()
            second.wait()

        load(0, *buf_a)
        def body(j, carry):
            load(2 * j + 1, *buf_b)
            flush(2 * j, *buf_a)

            @pl.when(j + 1 < n_pairs)
            def _():
                load(2 * j + 2, *buf_a)

            flush(2 * j + 1, *buf_b)
            return carry

        lax.fori_loop(0, n_pairs, body, 0)

    return scatter(rows, pos)


def _sc_gather_rows(table, idx):
    n_idx = idx.shape[0]
    width = table.shape[1]
    per_worker = n_idx // SC_WORKERS
    n_pairs = per_worker // (2 * SC_CHUNK)
    assert per_worker % (2 * SC_CHUNK) == 0
    buf = [pltpu.VMEM((SC_CHUNK,), jnp.int32), pltpu.VMEM((SC_CHUNK, width), table.dtype), pltpu.SemaphoreType.DMA]

    @functools.partial(pl.kernel, mesh=_sc_mesh(), out_type=jax.ShapeDtypeStruct((n_idx, width), table.dtype),
                       scratch_types=buf + buf)
    def gather(table_hbm, idx_hbm, out_hbm, *scratch):
        base = _sc_worker_id() * per_worker
        buf_a, buf_b = scratch[:3], scratch[3:]

        def fetch(chunk, idx_v, rows_v, sem):
            pltpu.sync_copy(idx_hbm.at[pl.ds(base + chunk * SC_CHUNK, SC_CHUNK)], idx_v)
            pltpu.async_copy(table_hbm.at[idx_v], rows_v, sem)

        def drain(chunk, idx_v, rows_v, sem):
            pltpu.make_async_copy(table_hbm.at[idx_v], rows_v, sem).wait()
            pltpu.sync_copy(rows_v, out_hbm.at[pl.ds(base + chunk * SC_CHUNK, SC_CHUNK)])

        fetch(0, *buf_a)
        def body(j, carry):
            fetch(2 * j + 1, *buf_b)
            drain(2 * j, *buf_a)

            @pl.when(j + 1 < n_pairs)
            def _():
                fetch(2 * j + 2, *buf_a)

            drain(2 * j + 1, *buf_b)
            return carry

        lax.fori_loop(0, n_pairs, body, 0)

    return gather(table, idx)


def _experts_kernel(tile_expert_ref, n_used_ref, hs_ref, wg_ref, wu_ref, wd_ref, ys_ref):
    del tile_expert_ref

    @pl.when(pl.program_id(0) < n_used_ref[0])
    def _():
        lo, hi = _unpack_bf16_pair(hs_ref[...])
        lo = lo.astype(jnp.bfloat16)
        hi = hi.astype(jnp.bfloat16)
        bf = jnp.bfloat16
        gate = (jnp.dot(lo, wg_ref[0:PACK_W, :].astype(bf), preferred_element_type=jnp.float32)
                + jnp.dot(hi, wg_ref[PACK_W:D_MODEL, :].astype(bf), preferred_element_type=jnp.float32))
        up = (jnp.dot(lo, wu_ref[0:PACK_W, :].astype(bf), preferred_element_type=jnp.float32)
              + jnp.dot(hi, wu_ref[PACK_W:D_MODEL, :].astype(bf), preferred_element_type=jnp.float32))
        hid = ((gate * jax.nn.sigmoid(gate)) * up).astype(bf)
        y = jnp.dot(hid, wd_ref[...].astype(bf), preferred_element_type=jnp.float32)
        ys_ref[...] = _pack_bf16_pair(y[:, 0:PACK_W], y[:, PACK_W:D_MODEL])


def _experts(hs, tile_expert, n_used, wg, wu, wd, layer):
    n_rows = hs.shape[0]
    weights = lambda i, te, nu: (layer, te[i], 0, 0)
    return pl.pallas_call(
        _experts_kernel,
        out_shape=jax.ShapeDtypeStruct((n_rows, PACK_W), jnp.int32),
        grid_spec=pltpu.PrefetchScalarGridSpec(
            num_scalar_prefetch=2,
            grid=(n_rows // ROW_TILE,),
            in_specs=[pl.BlockSpec((ROW_TILE, PACK_W), lambda i, te, nu: (i, 0)),
                      pl.BlockSpec((None, None, D_MODEL, D_EXPERT), weights),
                      pl.BlockSpec((None, None, D_MODEL, D_EXPERT), weights),
                      pl.BlockSpec((None, None, D_EXPERT, D_MODEL), weights)],
            out_specs=pl.BlockSpec((ROW_TILE, PACK_W), lambda i, te, nu: (i, 0))),
        compiler_params=pltpu.CompilerParams(dimension_semantics=("arbitrary",), vmem_limit_bytes=VMEM_LIMIT),
        name="moe_experts",
    )(tile_expert, n_used, hs, wg, wu, wd)


COMB_TM = 1024


def _combine_kernel(x1_ref, y1_ref, y2_ref, cw_ref, gf_ref, out_ref, *, final_norm):
    c1 = cw_ref[:, 0:1]
    c2 = cw_ref[:, 1:2]
    lo1, hi1 = _unpack_bf16_pair(y1_ref[...])
    lo2, hi2 = _unpack_bf16_pair(y2_ref[...])
    y = jnp.concatenate([x1_ref[:, 0:PACK_W] + (c1 * lo1 + c2 * lo2),
                         x1_ref[:, PACK_W:D_MODEL] + (c1 * hi1 + c2 * hi2)], axis=1)
    if final_norm:
        ms = jnp.mean(y * y, axis=1, keepdims=True)
        y = y * lax.rsqrt(ms + RMS_EPS) * gf_ref[...]
    out_ref[...] = y


def _combine(x1, y12, cw, gf, final_norm):
    n_tok = x1.shape[0]
    tm = min(COMB_TM, n_tok)
    n_blk = n_tok // tm
    return pl.pallas_call(
        functools.partial(_combine_kernel, final_norm=final_norm),
        out_shape=jax.ShapeDtypeStruct((n_tok, D_MODEL), jnp.float32),
        grid=(n_blk,),
        in_specs=[pl.BlockSpec((tm, D_MODEL), lambda i: (i, 0)),
                  pl.BlockSpec((tm, PACK_W), lambda i: (i, 0)),
                  pl.BlockSpec((tm, PACK_W), lambda i: (n_blk + i, 0)),
                  pl.BlockSpec((tm, ROUTER_W), lambda i: (i, 0)),
                  pl.BlockSpec((1, D_MODEL), lambda i: (0, 0))],
        out_specs=pl.BlockSpec((tm, D_MODEL), lambda i: (i, 0)),
        compiler_params=pltpu.CompilerParams(dimension_semantics=("arbitrary",), vmem_limit_bytes=VMEM_LIMIT),
        name="moe_combine",
    )(x1, y12, y12, cw, gf)


def _sparse_moe(x1, h2p, route, cw, w_gate, w_up, w_down, layer, gf, final_norm):
    n_tok = x1.shape[0]
    n_rows = 2 * n_tok + N_EXPERTS * ROW_TILE
    rank, counts = _dispatch(route)
    padded = (counts[:, 0].astype(jnp.int32) + ROW_TILE - 1) // ROW_TILE * ROW_TILE
    ends = jnp.cumsum(padded)
    starts = ends - padded
    tile_start = jnp.arange(n_rows // ROW_TILE, dtype=jnp.int32) * ROW_TILE
    tile_expert = jnp.minimum(jnp.sum(tile_start[:, None] >= ends[None, :], axis=1), N_EXPERTS - 1).astype(jnp.int32)
    n_used = (ends[N_EXPERTS - 1:] // ROW_TILE).astype(jnp.int32)

    pos = _positions(route, rank, starts.astype(jnp.float32).reshape(N_EXPERTS, 1))
    pos12 = jnp.concatenate([pos[0], pos[1]]).astype(jnp.int32)
    hs = _sc_scatter_rows(h2p, pos12, n_rows)
    ys = _experts(hs, tile_expert, n_used, w_gate, w_up, w_down, layer)
    y12 = _sc_gather_rows(ys, pos12)
    return _combine(x1, y12, cw, gf, final_norm)


def _split_bf16(w):
    hi = w.astype(jnp.bfloat16)
    lo = (w - hi.astype(jnp.float32)).astype(jnp.bfloat16)
    return jnp.stack([hi, lo])


def _qkv_col_scale():
    s = np.ones((QKV_W,), np.float32)
    for group, dim in enumerate((HEAD_DIM, DIFF_DIM, HEAD_DIM, HEAD_DIM)):
        s[3 * group * GROUP_W:(3 * group + 1) * GROUP_W] = LOG2E * dim ** -0.5
    return s


def _forget_weights(w_f, b_f):
    rows = jnp.zeros((F_ROWS, D_MODEL), jnp.float32)
    bias = jnp.zeros((F_ROWS, 1), jnp.float32)
    for h in range(HEADS_PER_MIXER):
        r = (h // 2) * 8 + (h % 2)
        rows = rows.at[r].set(w_f[:, h])
        bias = bias.at[r, 0].set(b_f[h])
    return _split_bf16(rows), bias


def kernel(x, rel_bias, ln1, w_in, w_out, lam_q1, lam_k1, lam_q2, lam_k2, subln_g, b_forget,
           ln2, w_group, b_group, w_expert, b_expert, w_gate, w_up, w_down, ln_f):
    batch, seq, _ = x.shape
    depth = ln1.shape[0]
    tabs = _bias_tables(rel_bias)
    xt = x.reshape(batch * seq, D_MODEL)
    gf = ln_f.reshape(1, D_MODEL)
    col_scale = _qkv_col_scale()
    for l in range(depth):
        lambda_init = 0.8 - 0.6 * math.exp(-0.3 * l)
        wf, bf = _forget_weights(w_in[l][:, QKV_W:], b_forget[l])
        w_qkv = (w_in[l][:, :QKV_W] * col_scale).astype(jnp.bfloat16)
        qkv, lf = _norm_proj(xt, ln1[l].reshape(1, D_MODEL), w_qkv, wf, bf, batch, seq)
        c = _cumsum(lf)

        o_a = _dilated(qkv, tabs, batch, seq)
        lamv = jnp.zeros((8, LANES), jnp.float32)
        lamv = lamv.at[0:4, 0:DIFF_DIM].set(jnp.stack([lam_q1[l], lam_k1[l], lam_q2[l], lam_k2[l]]))
        g_sub = jnp.tile(subln_g[l], 2).reshape(1, PAIR_W)
        o_b = _diff(qkv, tabs, lamv, g_sub, lambda_init, batch, seq)
        o_c = _moba(qkv, tabs, batch, seq)
        o_d = _fox(qkv, c, batch, seq)

        w_router = jnp.zeros((D_MODEL, ROUTER_W), jnp.float32)
        w_router = w_router.at[:, :N_EXPERTS].set(w_expert[l]).at[:, N_EXPERTS:N_EXPERTS + N_GROUPS].set(w_group[l])
        b_router = jnp.zeros((1, ROUTER_W), jnp.float32)
        b_router = b_router.at[0, :N_EXPERTS].set(b_expert[l].reshape(-1)).at[0, N_EXPERTS:N_EXPERTS + N_GROUPS].set(b_group[l])
        x1, h2p, route, cw = _outproj_router([o_a, o_b, o_c, o_d], w_out[l].astype(jnp.bfloat16), xt,
                                             ln2[l].reshape(1, D_MODEL), _split_bf16(w_router), b_router)
        xt = _sparse_moe(x1, h2p, route, cw, w_gate, w_up, w_down, l, gf, final_norm=(l == depth - 1))
    return xt.reshape(batch, seq, D_MODEL)
```

```python
import functools
import math
from typing import NamedTuple

import jax
import jax.numpy as jnp
import numpy as np
from jax import lax
from jax.experimental import pallas as pl
from jax.experimental.pallas import tpu as pltpu

D_MODEL = 1024
HEAD_DIM = 64
HEADS_PER_MIXER = 4
GROUP_W = HEADS_PER_MIXER * HEAD_DIM
QKV_W = 12 * GROUP_W
DIFF_DIM = HEAD_DIM // 2
MOBA_BLOCK = 256
MOBA_TOPK = 3
N_BUCKETS = 32
MAX_DISTANCE = 2048
N_BIAS_HEADS = 12
N_GROUPS = 4
EXPERTS_PER_GROUP = 8
N_EXPERTS = 32
D_EXPERT = 256
RMS_EPS = 1e-6
NEG = -1e30
LOG2E = math.log2(math.e)

LANES = 128
PAIR_W = 2 * HEAD_DIM
ACC_W = 2 * PAIR_W
TQ = 256
TK = 256
MAX_SUB = 8
N_DIL_TILES = MAX_DISTANCE // TK + 1
FAR_TILE = 7
MIN_DIST = 1 - MAX_SUB
N_TAB_TILES = N_DIL_TILES + 1 - MIN_DIST
VMEM_LIMIT = 48 * 1024 * 1024

_NT = (((1,), (1,)), ((), ()))


def _t5_thresholds():
    d = np.arange(0, 4 * MAX_DISTANCE, dtype=np.int64)
    max_exact = N_BUCKETS // 2
    df = np.maximum(d, 1).astype(np.float32)
    large = max_exact + (np.log(df / np.float32(max_exact)) / np.float32(math.log(MAX_DISTANCE / max_exact))
                         * np.float32(N_BUCKETS - max_exact)).astype(np.int32)
    bucket = np.where(d < max_exact, d, np.minimum(large, N_BUCKETS - 1))
    return [int(np.argmax(bucket >= b)) for b in range(N_BUCKETS)]


_T5_THR = _t5_thresholds()
assert (FAR_TILE - 1) * TK + 1 >= _T5_THR[N_BUCKETS - 1]


def _lane_iota(shape):
    return lax.broadcasted_iota(jnp.int32, shape, len(shape) - 1)


def _bias_table_kernel(bias_ref, out_ref):
    h = pl.program_id(0)
    n = pl.program_id(1) + MIN_DIST
    i = lax.broadcasted_iota(jnp.int32, (TQ, TK), 0)
    j = lax.broadcasted_iota(jnp.int32, (TQ, TK), 1)
    d = n * TK + i - j
    is_dil = h < HEADS_PER_MIXER

    def store(val):
        mult = ((d <= 128).astype(jnp.int32)
                + jnp.logical_and((d & 3) == 0, d <= 512).astype(jnp.int32)
                + jnp.logical_and((d & 15) == 0, d <= 2048).astype(jnp.int32))
        logm = jnp.where(mult == 3, math.log(3.0), jnp.where(mult == 2, math.log(2.0), 0.0))
        val = val + jnp.where(is_dil, logm, 0.0)
        valid = jnp.logical_and(d >= 0, jnp.logical_or(mult > 0, jnp.logical_not(is_dil)))
        out_ref[...] = jnp.where(valid, val * LOG2E, NEG)

    @pl.when(n < 0)
    def _():
        out_ref[...] = jnp.full((TQ, TK), NEG, jnp.float32)

    @pl.when(n >= FAR_TILE)
    def _():
        store(jnp.full((TQ, TK), bias_ref[h, N_BUCKETS - 1], jnp.float32))

    @pl.when(jnp.logical_and(n >= 0, n < FAR_TILE))
    def _():
        val = jnp.full((TQ, TK), bias_ref[h, 0], jnp.float32)
        for b in range(1, N_BUCKETS):
            val = jnp.where(d >= _T5_THR[b], bias_ref[h, b], val)
        store(val)


def _bias_tables(rel_bias):
    bias_h = rel_bias.T.astype(jnp.float32)
    return pl.pallas_call(
        _bias_table_kernel,
        out_shape=jax.ShapeDtypeStruct((N_BIAS_HEADS, N_TAB_TILES, TQ, TK), jnp.float32),
        grid=(N_BIAS_HEADS, N_TAB_TILES),
        in_specs=[pl.BlockSpec(memory_space=pltpu.SMEM)],
        out_specs=pl.BlockSpec((None, None, TQ, TK), lambda h, n: (h, n, 0, 0)),
        name="bias_tables",
    )(bias_h)


PROJ_TM = 1024
PROJ_CW = 512
F_ROWS = 16


def _norm_proj_kernel(x_ref, g_ref, w_ref, wf_ref, bf_ref, qkv_ref, lf_ref):
    x = x_ref[...]
    ms = jnp.mean(x * x, axis=1, keepdims=True)
    h = x * lax.rsqrt(ms + RMS_EPS) * g_ref[...]
    hb = h.astype(jnp.bfloat16)
    for c in range(QKV_W // PROJ_CW):
        cols = slice(c * PROJ_CW, (c + 1) * PROJ_CW)
        qkv_ref[:, cols] = jnp.dot(hb, w_ref[:, cols], preferred_element_type=jnp.float32).astype(jnp.bfloat16)
    h_lo = (h - hb.astype(jnp.float32)).astype(jnp.bfloat16)
    wf_hi = wf_ref[0]
    wf_lo = wf_ref[1]
    z = (lax.dot_general(wf_hi, hb, _NT, preferred_element_type=jnp.float32)
         + lax.dot_general(wf_hi, h_lo, _NT, preferred_element_type=jnp.float32)
         + lax.dot_general(wf_lo, hb, _NT, preferred_element_type=jnp.float32))
    z = z + bf_ref[...]
    lf_ref[...] = jnp.minimum(z, 0.0) - jnp.log(1.0 + jnp.exp(-jnp.abs(z)))


def _norm_proj(x2d, g, w_qkv, wf, bf, batch, seq):
    n_tok = x2d.shape[0]
    tm = min(PROJ_TM, seq)
    per_b = seq // tm
    return pl.pallas_call(
        _norm_proj_kernel,
        out_shape=(jax.ShapeDtypeStruct((n_tok, QKV_W), jnp.bfloat16),
                   jax.ShapeDtypeStruct((batch * F_ROWS, seq), jnp.float32)),
        grid=(n_tok // tm,),
        in_specs=[pl.BlockSpec((tm, D_MODEL), lambda i: (i, 0)),
                  pl.BlockSpec((1, D_MODEL), lambda i: (0, 0)),
                  pl.BlockSpec((D_MODEL, QKV_W), lambda i: (0, 0)),
                  pl.BlockSpec((2, F_ROWS, D_MODEL), lambda i: (0, 0, 0)),
                  pl.BlockSpec((F_ROWS, 1), lambda i: (0, 0))],
        out_specs=(pl.BlockSpec((tm, QKV_W), lambda i: (i, 0)),
                   pl.BlockSpec((F_ROWS, tm), lambda i: (i // per_b, i % per_b))),
        compiler_params=pltpu.CompilerParams(dimension_semantics=("arbitrary",), vmem_limit_bytes=VMEM_LIMIT),
        name="norm_proj",
    )(x2d, g, w_qkv, wf, bf)


def _split3(x):
    x1 = x.astype(jnp.bfloat16)
    r1 = x - x1.astype(jnp.float32)
    x2 = r1.astype(jnp.bfloat16)
    x3 = (r1 - x2.astype(jnp.float32)).astype(jnp.bfloat16)
    return x1, x2, x3


def _cumsum_kernel(lf_ref, c_ref):
    rows, seq = lf_ref.shape
    r = lax.broadcasted_iota(jnp.int32, (LANES, LANES), 0)
    c = lax.broadcasted_iota(jnp.int32, (LANES, LANES), 1)
    upper = jnp.where(r <= c, 1.0, 0.0).astype(jnp.bfloat16)

    def body(i, carry):
        off = pl.multiple_of(i * LANES, LANES)
        x1, x2, x3 = _split3(lf_ref[:, pl.ds(off, LANES)])
        cs = (jnp.dot(x1, upper, preferred_element_type=jnp.float32)
              + jnp.dot(x2, upper, preferred_element_type=jnp.float32)
              + jnp.dot(x3, upper, preferred_element_type=jnp.float32)) + carry
        c_ref[:, pl.ds(off, LANES)] = cs * LOG2E
        return cs[:, LANES - 1:LANES]

    lax.fori_loop(0, seq // LANES, body, jnp.zeros((rows, 1), jnp.float32))


def _cumsum(lf):
    return pl.pallas_call(
        _cumsum_kernel,
        out_shape=jax.ShapeDtypeStruct(lf.shape, jnp.float32),
        name="forget_cumsum",
    )(lf)


class _Geom(NamedTuple):
    n_streams: int
    n_sub: int
    step_tiles: int = 2

    @property
    def q_rows(self):
        return self.n_sub * TQ

    @property
    def step_keys(self):
        return self.step_tiles * TK

    def rows(self, sub, st):
        rb = sub * self.n_streams + st
        return slice(rb * TQ, (rb + 1) * TQ)

    def blocks(self):
        return [(sub, st, self.rows(sub, st)) for sub in range(self.n_sub) for st in range(self.n_streams)]

    def first_tile(self, blk, m):
        return self.n_sub * blk + self.n_sub - self.step_tiles * (m + 1)

    def dist(self, m, sub, w):
        return self.step_tiles * (m + 1) - self.n_sub + sub - w

    def n_steps(self, blk):
        return (self.n_sub // self.step_tiles) * (blk + 1)

    def own_steps(self):
        return self.n_sub // self.step_tiles

    def tiles_needed(self, m, sub):
        if not isinstance(m, int):
            return self.step_tiles
        return min(max(self.dist(m, sub, 0) + 1, 0), self.step_tiles)


def _stack_masked_q(q, qs_ref, geom):
    lane = _lane_iota((TQ, PAIR_W))
    width = PAIR_W // geom.n_streams
    for sub, st, rows in geom.blocks():
        q_tile = q[sub * TQ:(sub + 1) * TQ]
        qs_ref[rows, 0:PAIR_W] = jnp.where(lane // width == st, q_tile, jnp.zeros_like(q_tile))


def _init_state(m_ref, acc_ref):
    m_ref[...] = jnp.full(m_ref.shape, NEG, jnp.float32)
    acc_ref[...] = jnp.zeros(acc_ref.shape, jnp.float32)


def _flash_step(rows, s, v_aug, m_ref, acc_ref):
    m_prev = m_ref[rows, :]
    m_new = jnp.maximum(m_prev, jnp.max(s, axis=1, keepdims=True))
    alpha = jnp.exp2(m_prev - m_new)
    p = jnp.concatenate([jnp.exp2(s[:, c * LANES:(c + 1) * LANES] - m_new) for c in range(s.shape[1] // LANES)],
                        axis=1).astype(v_aug.dtype)
    pv = jnp.dot(p, v_aug, preferred_element_type=jnp.float32)
    acc_ref[rows, :] = jnp.concatenate([alpha, alpha], axis=1) * acc_ref[rows, :] + pv
    m_ref[rows, :] = m_new


def _normalised(acc_ref, rows):
    return acc_ref[rows, 0:PAIR_W] / acc_ref[rows, PAIR_W:ACC_W]


def _table_bias(tab_ref, hh, geom, m, sub, cap, n_tiles):
    return jnp.concatenate([tab_ref[hh, jnp.minimum(geom.dist(m, sub, w), cap) - MIN_DIST]
                            for w in range(n_tiles)], axis=1)


def _step_rows(ref, geom, blk, m):
    k0 = pl.multiple_of(geom.first_tile(blk, m) * TK, geom.step_keys)
    return ref[pl.ds(k0, geom.step_keys), :]


def _make_attend(geom, blk, scratch, v_ref, keys, finish):
    qs_ref, m_ref, acc_ref, s_ref = scratch
    blocks = geom.blocks()
    last_step = geom.n_steps(blk) - 1

    def raw(rows, rhs):
        return lax.dot_general(qs_ref[rows, :], rhs, _NT, preferred_element_type=jnp.float32)

    def v_aug(m):
        v = _step_rows(v_ref, geom, blk, m)
        return jnp.concatenate([v, jnp.ones_like(v)], axis=1)

    def attend(rows, sub, st, scores, m, va, n_tiles):
        _flash_step(rows, finish(sub, st, scores, m, n_tiles), va[0:n_tiles * TK], m_ref, acc_ref)

    def single(m):
        rhs = keys(m)
        va = v_aug(m)
        for sub, st, rows in blocks:
            n_tiles = geom.tiles_needed(m, sub)
            if n_tiles:
                attend(rows, sub, st, raw(rows, rhs[0:n_tiles * TK]), m, va, n_tiles)

    def step(m, next_is_static=False):
        va = v_aug(m)
        rhs_next = keys(jnp.minimum(m + 1, last_step))
        for sub, st, rows in blocks:
            n_tiles = geom.tiles_needed(m, sub)
            cur = s_ref[rows, 0:n_tiles * TK] if n_tiles else None
            next_tiles = geom.tiles_needed(m + 1, sub) if next_is_static else geom.step_tiles
            if next_tiles:
                s_ref[rows, 0:next_tiles * TK] = raw(rows, rhs_next[0:next_tiles * TK])
            if n_tiles:
                attend(rows, sub, st, cur, m, va, n_tiles)

    def run(m0, count):
        peeled = [m0 + i for i in range(geom.own_steps()) if isinstance(m0, int)
                  and any(geom.tiles_needed(m0 + i, sub) < geom.step_tiles for sub in range(geom.n_sub))]

        @pl.when(count > 0)
        def _():
            rhs = keys(m0)
            for sub, _, rows in blocks:
                n_tiles = geom.tiles_needed(m0, sub)
                if n_tiles:
                    s_ref[rows, 0:n_tiles * TK] = raw(rows, rhs[0:n_tiles * TK])

        for m in peeled:
            pl.when(count > m - m0)(functools.partial(step, m, next_is_static=(m + 1) in peeled))

        def body(i, carry):
            step(m0 + i)
            return carry

        lax.fori_loop(len(peeled), count, body, 0)

    return single, run


def _attn_call(kernel, group, qkv, extra_inputs, extra_specs, geom, batch, seq, *, q_width=PAIR_W,
               extra_scratch=(), name=None):
    assert seq % geom.q_rows == 0 and seq % geom.step_keys == 0 and geom.n_sub % geom.step_tiles == 0
    nq = seq // geom.q_rows
    qc, kc, vc = 6 * group, 6 * group + 2, 6 * group + 4
    rows = geom.n_sub * geom.n_streams * TQ
    return pl.pallas_call(
        functools.partial(kernel, geom=geom),
        out_shape=jax.ShapeDtypeStruct((batch * seq, GROUP_W), jnp.bfloat16),
        grid=(2, batch, nq),
        in_specs=[pl.BlockSpec((geom.q_rows, PAIR_W), lambda p, b, i: (b * nq + i, qc + p)),
                  pl.BlockSpec((seq, PAIR_W), lambda p, b, i: (b, kc + p)),
                  pl.BlockSpec((seq, PAIR_W), lambda p, b, i: (b, vc + p))] + list(extra_specs),
        out_specs=pl.BlockSpec((geom.q_rows, PAIR_W), lambda p, b, i: (b * nq + i, p)),
        scratch_shapes=[pltpu.VMEM((rows, q_width), jnp.bfloat16),
                        pltpu.VMEM((rows, LANES), jnp.float32),
                        pltpu.VMEM((rows, ACC_W), jnp.float32),
                        pltpu.VMEM((rows, geom.step_keys), jnp.float32)
                        ] + list(extra_scratch),
        compiler_params=pltpu.CompilerParams(dimension_semantics=("arbitrary", "arbitrary", "arbitrary"),
                                             vmem_limit_bytes=VMEM_LIMIT),
        name=name,
    )(qkv, qkv, qkv, *extra_inputs)


def _tab_spec(first_head):
    return pl.BlockSpec((2, N_TAB_TILES, TQ, TK), lambda p, b, i: (first_head // 2 + p, 0, 0, 0))


def _merge_pair(o0, o1):
    return jnp.where(_lane_iota(o0.shape) < HEAD_DIM, o0, o1)


def _store_pair_outputs(o_ref, acc_ref, geom):
    for sub in range(geom.n_sub):
        o = _merge_pair(_normalised(acc_ref, geom.rows(sub, 0)), _normalised(acc_ref, geom.rows(sub, 1)))
        o_ref[sub * TQ:(sub + 1) * TQ, :] = o.astype(o_ref.dtype)


PAIR_GEOM = _Geom(n_streams=2, n_sub=4)
MOBA_GEOM = _Geom(n_streams=2, n_sub=MAX_SUB)
FOX_GEOM = _Geom(n_streams=2, n_sub=2)
DIFF_GEOM = _Geom(n_streams=4, n_sub=4)


def _dilated_kernel(q_ref, k_ref, v_ref, tab_ref, o_ref, qs_ref, m_ref, acc_ref, s_ref, *, geom):
    blk = pl.program_id(2)
    _stack_masked_q(q_ref[...], qs_ref, geom)
    _init_state(m_ref, acc_ref)
    _, run = _make_attend(
        geom, blk, (qs_ref, m_ref, acc_ref, s_ref), v_ref,
        keys=lambda m: _step_rows(k_ref, geom, blk, m),
        finish=lambda sub, st, raw, m, n_tiles: raw + _table_bias(tab_ref, st, geom, m, sub, N_DIL_TILES, n_tiles))
    reach = -(-(N_DIL_TILES - 1 + geom.n_sub) // geom.step_tiles)
    run(0, jnp.minimum(geom.n_steps(blk), reach))
    _store_pair_outputs(o_ref, acc_ref, geom)


def _dilated(qkv, tabs, batch, seq):
    return _attn_call(_dilated_kernel, 0, qkv, [tabs], [_tab_spec(0)], PAIR_GEOM, batch, seq, name="dilated_attn")


def _diff_kernel(q_ref, k_ref, v_ref, tab_ref, lam_ref, g_ref, o_ref, qs_ref, m_ref, acc_ref, s_ref, *,
                 lambda_init, geom):
    blk = pl.program_id(2)
    _stack_masked_q(q_ref[...], qs_ref, geom)
    _init_state(m_ref, acc_ref)
    _, run = _make_attend(
        geom, blk, (qs_ref, m_ref, acc_ref, s_ref), v_ref,
        keys=lambda m: _step_rows(k_ref, geom, blk, m),
        finish=lambda sub, st, raw, m, n_tiles: raw + _table_bias(tab_ref, st // 2, geom, m, sub, FAR_TILE, n_tiles))
    run(0, geom.n_steps(blk))

    lamv = lam_ref[...]
    lam = (jnp.exp(jnp.sum(lamv[0:1] * lamv[1:2], axis=1, keepdims=True))
           - jnp.exp(jnp.sum(lamv[2:3] * lamv[3:4], axis=1, keepdims=True)) + lambda_init)
    first = _lane_iota((TQ, PAIR_W)) < HEAD_DIM
    for sub in range(geom.n_sub):
        part = [_normalised(acc_ref, geom.rows(sub, st)) for st in range(geom.n_streams)]
        o = _merge_pair(part[0] - lam * part[1], part[2] - lam * part[3])
        sq = o * o
        ms0 = jnp.sum(jnp.where(first, sq, 0.0), axis=1, keepdims=True)
        ms1 = jnp.sum(jnp.where(first, 0.0, sq), axis=1, keepdims=True)
        ms = jnp.where(first, ms0, ms1) * (1.0 / HEAD_DIM)
        y = (o * lax.rsqrt(ms + RMS_EPS) * g_ref[...]) * (1.0 - lambda_init)
        o_ref[sub * TQ:(sub + 1) * TQ, :] = y.astype(o_ref.dtype)


def _diff(qkv, tabs, lamv, g_sub, lambda_init, batch, seq):
    return _attn_call(functools.partial(_diff_kernel, lambda_init=lambda_init), 1, qkv, [tabs, lamv, g_sub],
                      [_tab_spec(4), pl.BlockSpec((8, LANES), lambda p, b, i: (0, 0)),
                       pl.BlockSpec((1, PAIR_W), lambda p, b, i: (0, 0))],
                      DIFF_GEOM, batch, seq, name="diff_attn")


def _moba_kernel(q_ref, k_ref, v_ref, tab_ref, o_ref, qs_ref, m_ref, acc_ref, s_ref, km_ref, *, geom):
    blk = pl.program_id(2)
    n_blk = k_ref.shape[0] // MOBA_BLOCK
    assert n_blk <= LANES and MOBA_BLOCK == TQ == TK

    @pl.when(blk == 0)
    def _():
        km_ref[...] = jnp.zeros(km_ref.shape, jnp.float32)
        for n in range(n_blk):
            km_ref[n:n + 1, :] = jnp.sum(k_ref[n * MOBA_BLOCK:(n + 1) * MOBA_BLOCK, :].astype(jnp.float32),
                                         axis=0, keepdims=True) * (1.0 / MOBA_BLOCK)

    _stack_masked_q(q_ref[...], qs_ref, geom)
    _init_state(m_ref, acc_ref)
    nb = -(-n_blk // 8) * 8
    km = km_ref[0:nb, :]
    km_hi = km.astype(jnp.bfloat16)
    km_lo = (km - km_hi.astype(jnp.float32)).astype(jnp.bfloat16)
    qs = qs_ref[:, 0:PAIR_W]
    gate = (lax.dot_general(km_hi, qs, _NT, preferred_element_type=jnp.float32)
            + lax.dot_general(km_lo, qs, _NT, preferred_element_type=jnp.float32))
    cand = lax.broadcasted_iota(jnp.int32, gate.shape, 0)
    cand_f = cand.astype(jnp.float32)
    own = geom.n_sub * blk + _lane_iota(gate.shape) // (geom.n_streams * TQ)
    g = jnp.where(cand < own, gate, NEG)
    sel = cand == own
    for _ in range(MOBA_TOPK):
        mx = jnp.max(g, axis=0, keepdims=True)
        first = jnp.min(jnp.where(g == mx, cand_f, 2.0 * LANES), axis=0, keepdims=True)
        pick = cand_f == first
        sel = jnp.logical_or(sel, jnp.logical_and(pick, mx > 0.5 * NEG))
        g = jnp.where(pick, -3e38, g)
    mask = jnp.where(sel, 0.0, NEG)
    if nb < LANES:
        mask = jnp.concatenate([mask, jnp.full((LANES - nb, mask.shape[1]), NEG, jnp.float32)], axis=0)
    qs_ref[:, PAIR_W:2 * PAIR_W] = mask.T.astype(jnp.bfloat16)

    def keys(m):
        k = _step_rows(k_ref, geom, blk, m)
        block_of_key = geom.first_tile(blk, m) + lax.broadcasted_iota(jnp.int32, (geom.step_keys, LANES), 0) // TK
        onehot = jnp.where(_lane_iota((geom.step_keys, LANES)) == block_of_key, 1.0, 0.0).astype(k.dtype)
        return jnp.concatenate([k, onehot], axis=1)

    _, run = _make_attend(
        geom, blk, (qs_ref, m_ref, acc_ref, s_ref), v_ref, keys=keys,
        finish=lambda sub, st, raw, m, n_tiles: raw + _table_bias(tab_ref, st, geom, m, sub, FAR_TILE, n_tiles))
    run(0, geom.n_steps(blk))
    _store_pair_outputs(o_ref, acc_ref, geom)


def _moba(qkv, tabs, batch, seq):
    return _attn_call(_moba_kernel, 2, qkv, [tabs], [_tab_spec(8)], MOBA_GEOM, batch, seq, q_width=2 * PAIR_W,
                      extra_scratch=[pltpu.VMEM((LANES, PAIR_W), jnp.float32)], name="moba_attn")


FOX_SKIP_BITS = 48.0


def _head_row_norms(x, hh):
    xf = x.astype(jnp.float32)
    sq = jnp.where(_lane_iota(xf.shape) // HEAD_DIM == hh, xf * xf, 0.0)
    return jnp.sqrt(jnp.max(jnp.sum(sq, axis=1, keepdims=True), axis=0, keepdims=True))


def _fox_kernel(q_ref, k_ref, v_ref, c_ref, o_ref, qs_ref, m_ref, acc_ref, s_ref, cq_ref, stat_ref, *, geom):
    blk = pl.program_id(2)
    n_tiles = k_ref.shape[0] // TK
    assert n_tiles <= LANES

    @pl.when(blk == 0)
    def _():
        stat_ref[...] = jnp.zeros(stat_ref.shape, jnp.float32)
        for j in range(n_tiles):
            k_tile = k_ref[j * TK:(j + 1) * TK, :]
            for hh in range(2):
                stat_ref[hh:hh + 1, j:j + 1] = _head_row_norms(k_tile, hh)
                stat_ref[2 + hh:3 + hh, j:j + 1] = jnp.min(c_ref[hh:hh + 1, j * TK:(j + 1) * TK], axis=1, keepdims=True)

    q = q_ref[...]
    _stack_masked_q(q, qs_ref, geom)
    _init_state(m_ref, acc_ref)
    q0 = pl.multiple_of(blk * geom.q_rows, geom.q_rows)
    row = lax.broadcasted_iota(jnp.int32, (TQ, TQ), 0)
    col = lax.broadcasted_iota(jnp.int32, (TQ, TQ), 1)
    for sub, hh, rows in geom.blocks():
        c_row = c_ref[hh:hh + 1, pl.ds(q0 + sub * TQ, TQ)]
        c_col = jnp.sum(jnp.where(row == col, c_row, 0.0), axis=1, keepdims=True)
        cq_ref[rows, :] = jnp.broadcast_to(c_col, (TQ, LANES))

    def finish(sub, hh, raw, m, n_tiles):
        n_keys = n_tiles * TK
        k0 = pl.multiple_of(geom.first_tile(blk, m) * TK, TK)
        cq = cq_ref[geom.rows(sub, hh), :]
        s = raw + (jnp.concatenate([cq] * (n_keys // LANES), axis=1) - c_ref[hh:hh + 1, pl.ds(k0, n_keys)])
        if isinstance(m, int) and m < geom.own_steps():
            key_pos = lax.broadcasted_iota(jnp.int32, (TQ, n_keys), 1)
            query_pos = lax.broadcasted_iota(jnp.int32, (TQ, n_keys), 0)
            first_key = (geom.n_sub - geom.step_tiles * (m + 1)) * TK
            s = jnp.where(key_pos + first_key <= query_pos + sub * TQ, s, NEG)
        return s

    single, run = _make_attend(geom, blk, (qs_ref, m_ref, acc_ref, s_ref), v_ref,
                               keys=lambda m: _step_rows(k_ref, geom, blk, m), finish=finish)
    for m in range(geom.own_steps()):
        single(m)

    tile = _lane_iota((1, LANES))
    needed = jnp.zeros((1, LANES), jnp.bool_)
    for hh in range(2):
        rows_h = [geom.rows(sub, hh) for sub in range(geom.n_sub)]
        m_min = functools.reduce(jnp.minimum, [jnp.min(m_ref[r, :], axis=0, keepdims=True) for r in rows_h])
        cq_max = functools.reduce(jnp.maximum, [jnp.max(cq_ref[r, :], axis=0, keepdims=True) for r in rows_h])
        bound = _head_row_norms(q, hh) * stat_ref[hh:hh + 1, :] + cq_max - stat_ref[2 + hh:3 + hh, :] + 1.0
        needed = jnp.logical_or(needed, bound >= m_min - FOX_SKIP_BITS)
    n_before = geom.n_sub * blk
    first_needed = jnp.min(jnp.where(jnp.logical_and(needed, tile < n_before), tile, n_before))
    run(geom.own_steps(), (n_before - first_needed + geom.step_tiles - 1) // geom.step_tiles)
    _store_pair_outputs(o_ref, acc_ref, geom)


def _fox(qkv, c, batch, seq):
    return _attn_call(_fox_kernel, 3, qkv, [c.reshape(batch, 2, 8, seq)],
                      [pl.BlockSpec((None, None, 8, seq), lambda p, b, i: (b, p, 0, 0))],
                      FOX_GEOM, batch, seq,
                      extra_scratch=[pltpu.VMEM((FOX_GEOM.n_sub * FOX_GEOM.n_streams * TQ, LANES), jnp.float32),
                                     pltpu.VMEM((8, LANES), jnp.float32)], name="fox_attn")


OUT_TM = 1024
ROUTER_W = LANES
ROUTE_ROWS = 8


PACK_W = D_MODEL // 2


def _pack_bf16_pair(lo, hi):
    lo_bits = lax.bitcast_convert_type(lo.astype(jnp.bfloat16).astype(jnp.float32), jnp.uint32)
    hi_bits = lax.bitcast_convert_type(hi.astype(jnp.bfloat16).astype(jnp.float32), jnp.uint32)
    word = lax.shift_right_logical(lo_bits, jnp.uint32(16)) | (hi_bits & jnp.uint32(0xFFFF0000))
    return lax.bitcast_convert_type(word, jnp.int32)


def _unpack_bf16_pair(word):
    bits = lax.bitcast_convert_type(word, jnp.uint32)
    lo = lax.bitcast_convert_type(lax.shift_left(bits, jnp.uint32(16)), jnp.float32)
    hi = lax.bitcast_convert_type(bits & jnp.uint32(0xFFFF0000), jnp.float32)
    return lo, hi


def _outproj_router_kernel(oa_ref, ob_ref, oc_ref, od_ref, wo_ref, x_ref, g_ref, wr_ref, br_ref,
                           x1_ref, h2_ref, route_ref, cw_ref):
    acc = x_ref[...]
    for m, o_ref in enumerate((oa_ref, ob_ref, oc_ref, od_ref)):
        acc = acc + jnp.dot(o_ref[...], wo_ref[m * GROUP_W:(m + 1) * GROUP_W, :], preferred_element_type=jnp.float32)
    x1_ref[...] = acc
    ms = jnp.mean(acc * acc, axis=1, keepdims=True)
    h = acc * lax.rsqrt(ms + RMS_EPS) * g_ref[...]
    hb = h.astype(jnp.bfloat16)
    h2_ref[...] = _pack_bf16_pair(h[:, 0:PACK_W], h[:, PACK_W:D_MODEL])
    h_lo = (h - hb.astype(jnp.float32)).astype(jnp.bfloat16)
    logits = (jnp.dot(hb, wr_ref[0], preferred_element_type=jnp.float32)
              + jnp.dot(h_lo, wr_ref[0], preferred_element_type=jnp.float32)
              + jnp.dot(hb, wr_ref[1], preferred_element_type=jnp.float32)) + br_ref[...]
    lane = _lane_iota(logits.shape).astype(jnp.float32)
    big = 2.0 * LANES
    gmask = jnp.logical_and(lane >= N_EXPERTS, lane < N_EXPERTS + N_GROUPS)
    gl = jnp.where(gmask, logits, NEG)
    gmax = jnp.max(gl, axis=1, keepdims=True)
    glane = jnp.min(jnp.where(gl == gmax, lane, big), axis=1, keepdims=True)
    gsum = jnp.sum(jnp.where(gmask, jnp.exp(gl - gmax), 0.0), axis=1, keepdims=True)
    g_w = 1.0 / gsum
    e0 = (glane - N_EXPERTS) * EXPERTS_PER_GROUP
    emask = jnp.logical_and(lane >= e0, lane < e0 + EXPERTS_PER_GROUP)
    el = jnp.where(emask, logits, NEG)
    v1 = jnp.max(el, axis=1, keepdims=True)
    i1 = jnp.min(jnp.where(el == v1, lane, big), axis=1, keepdims=True)
    el2 = jnp.where(lane == i1, NEG, el)
    v2 = jnp.max(el2, axis=1, keepdims=True)
    i2 = jnp.min(jnp.where(el2 == v2, lane, big), axis=1, keepdims=True)
    e2 = jnp.exp(v2 - v1)
    den = 1.0 + e2
    c1 = g_w / den
    c2 = g_w * e2 / den
    quad = jnp.where(lane == 0.0, i1, jnp.where(lane == 1.0, i2, jnp.where(lane == 2.0, c1,
                                                                             jnp.where(lane == 3.0, c2, 0.0))))
    cw_ref[...] = jnp.where(lane == 0.0, c1, jnp.where(lane == 1.0, c2, 0.0))
    route_ref[...] = quad.T[0:ROUTE_ROWS, :]


def _outproj_router(o_parts, w_out, x2d, g2, wr, br):
    n_tok = x2d.shape[0]
    tm = min(OUT_TM, n_tok)
    row = lambda i: (i, 0)
    fixed = lambda i: (0, 0)
    return pl.pallas_call(
        _outproj_router_kernel,
        out_shape=(jax.ShapeDtypeStruct((n_tok, D_MODEL), jnp.float32),
                   jax.ShapeDtypeStruct((n_tok, PACK_W), jnp.int32),
                   jax.ShapeDtypeStruct((ROUTE_ROWS, n_tok), jnp.float32),
                   jax.ShapeDtypeStruct((n_tok, ROUTER_W), jnp.float32)),
        grid=(n_tok // tm,),
        in_specs=[pl.BlockSpec((tm, GROUP_W), row)] * 4 + [
            pl.BlockSpec((D_MODEL, D_MODEL), fixed),
            pl.BlockSpec((tm, D_MODEL), row),
            pl.BlockSpec((1, D_MODEL), fixed),
            pl.BlockSpec((2, D_MODEL, ROUTER_W), lambda i: (0, 0, 0)),
            pl.BlockSpec((1, ROUTER_W), fixed)],
        out_specs=(pl.BlockSpec((tm, D_MODEL), row), pl.BlockSpec((tm, PACK_W), row),
                   pl.BlockSpec((ROUTE_ROWS, tm), lambda i: (0, i)), pl.BlockSpec((tm, ROUTER_W), row)),
        compiler_params=pltpu.CompilerParams(dimension_semantics=("arbitrary",), vmem_limit_bytes=VMEM_LIMIT),
        name="outproj_router",
    )(*o_parts, w_out, x2d, g2, wr, br)


ROW_TILE = 512
DISP_TM = 1024
SC_CORES = 2
SC_SUBCORES = 16
SC_WORKERS = SC_CORES * SC_SUBCORES
SC_CHUNK = 64


def _dispatch_kernel(route_ref, rank_ref, cnt_ref, tri_ref, carry_ref):
    step = pl.program_id(0)
    tm = route_ref.shape[1]

    @pl.when(step == 0)
    def _():
        r = lax.broadcasted_iota(jnp.int32, (tm, tm), 0)
        c = lax.broadcasted_iota(jnp.int32, (tm, tm), 1)
        tri_ref[...] = jnp.where(r <= c, 1.0, 0.0).astype(jnp.bfloat16)
        carry_ref[...] = jnp.zeros(carry_ref.shape, jnp.float32)

    e1 = route_ref[0:1, :]
    e2 = route_ref[1:2, :]
    expert = lax.broadcasted_iota(jnp.int32, (N_EXPERTS, tm), 0).astype(jnp.float32)
    hit1 = e1 == expert
    hit2 = e2 == expert
    hits = jnp.where(jnp.logical_or(hit1, hit2), 1.0, 0.0).astype(jnp.bfloat16)
    cum = jnp.dot(hits, tri_ref[...], preferred_element_type=jnp.float32) + carry_ref[...]
    rank1 = jnp.sum(jnp.where(hit1, cum - 1.0, 0.0), axis=0, keepdims=True)
    rank2 = jnp.sum(jnp.where(hit2, cum - 1.0, 0.0), axis=0, keepdims=True)
    row = lax.broadcasted_iota(jnp.int32, (ROUTE_ROWS, tm), 0)
    rank_ref[...] = jnp.where(row == 0, rank1, jnp.where(row == 1, rank2, 0.0))
    total = cum[:, tm - 1:tm]
    carry_ref[...] = total
    cnt_ref[...] = jnp.broadcast_to(total, cnt_ref.shape)


def _dispatch(route):
    n_tok = route.shape[1]
    tm = min(DISP_TM, n_tok)
    return pl.pallas_call(
        _dispatch_kernel,
        out_shape=(jax.ShapeDtypeStruct((ROUTE_ROWS, n_tok), jnp.float32),
                   jax.ShapeDtypeStruct((N_EXPERTS, LANES), jnp.float32)),
        grid=(n_tok // tm,),
        in_specs=[pl.BlockSpec((ROUTE_ROWS, tm), lambda i: (0, i))],
        out_specs=(pl.BlockSpec((ROUTE_ROWS, tm), lambda i: (0, i)),
                   pl.BlockSpec((N_EXPERTS, LANES), lambda i: (0, 0))),
        scratch_shapes=[pltpu.VMEM((tm, tm), jnp.bfloat16), pltpu.VMEM((N_EXPERTS, 1), jnp.float32)],
        compiler_params=pltpu.CompilerParams(dimension_semantics=("arbitrary",)),
        name="moe_ranks",
    )(route)


def _positions_kernel(route_ref, rank_ref, off_ref, pos_ref):
    tm = route_ref.shape[1]
    expert = lax.broadcasted_iota(jnp.int32, (N_EXPERTS, tm), 0).astype(jnp.float32)
    off = off_ref[...]
    start1 = jnp.sum(jnp.where(route_ref[0:1, :] == expert, off, 0.0), axis=0, keepdims=True)
    start2 = jnp.sum(jnp.where(route_ref[1:2, :] == expert, off, 0.0), axis=0, keepdims=True)
    row = lax.broadcasted_iota(jnp.int32, (ROUTE_ROWS, tm), 0)
    pos_ref[...] = jnp.where(row == 0, start1 + rank_ref[0:1, :], jnp.where(row == 1, start2 + rank_ref[1:2, :], 0.0))


def _positions(route, rank, seg_start):
    n_tok = route.shape[1]
    tm = min(DISP_TM, n_tok)
    blk = pl.BlockSpec((ROUTE_ROWS, tm), lambda i: (0, i))
    return pl.pallas_call(
        _positions_kernel,
        out_shape=jax.ShapeDtypeStruct((ROUTE_ROWS, n_tok), jnp.float32),
        grid=(n_tok // tm,),
        in_specs=[blk, blk, pl.BlockSpec((N_EXPERTS, 1), lambda i: (0, 0))],
        out_specs=blk,
        name="moe_positions",
    )(route, rank, seg_start)


def _sc_mesh():
    from jax.experimental.pallas import tpu_sc as plsc
    return plsc.VectorSubcoreMesh(core_axis_name="c", subcore_axis_name="s")


def _sc_worker_id():
    return lax.axis_index("s") * SC_CORES + lax.axis_index("c")


def _sc_scatter_rows(rows, pos, n_out):
    n_tok, width = rows.shape
    per_worker = n_tok // SC_WORKERS
    n_pairs = per_worker // (2 * SC_CHUNK)
    assert per_worker % (2 * SC_CHUNK) == 0
    buf = [pltpu.VMEM((SC_CHUNK,), jnp.int32), pltpu.VMEM((SC_CHUNK,), jnp.int32),
           pltpu.VMEM((SC_CHUNK, width), rows.dtype), pltpu.SemaphoreType.DMA, pltpu.SemaphoreType.DMA]

    @functools.partial(pl.kernel, mesh=_sc_mesh(), out_type=jax.ShapeDtypeStruct((n_out, width), rows.dtype),
                       scratch_types=buf + buf)
    def scatter(rows_hbm, pos_hbm, out_hbm, *scratch):
        base = _sc_worker_id() * per_worker
        buf_a, buf_b = scratch[:5], scratch[5:]

        def load(chunk, p1_v, p2_v, rows_v, sem_in, sem_out):
            off = base + chunk * SC_CHUNK
            pltpu.sync_copy(pos_hbm.at[pl.ds(off, SC_CHUNK)], p1_v)
            pltpu.sync_copy(pos_hbm.at[pl.ds(n_tok + off, SC_CHUNK)], p2_v)
            pltpu.async_copy(rows_hbm.at[pl.ds(off, SC_CHUNK)], rows_v, sem_in)

        def flush(chunk, p1_v, p2_v, rows_v, sem_in, sem_out):
            off = base + chunk * SC_CHUNK
            pltpu.make_async_copy(rows_hbm.at[pl.ds(off, SC_CHUNK)], rows_v, sem_in).wait()
            first = pltpu.async_copy(rows_v, out_hbm.at[p1_v], sem_out)
            second = pltpu.async_copy(rows_v, out_hbm.at[p2_v], sem_out)
            first.wait()
            second.wait()

        load(0, *buf_a)
        def body(j, carry):
            load(2 * j + 1, *buf_b)
            flush(2 * j, *buf_a)

            @pl.when(j + 1 < n_pairs)
            def _():
                load(2 * j + 2, *buf_a)

            flush(2 * j + 1, *buf_b)
            return carry

        lax.fori_loop(0, n_pairs, body, 0)

    return scatter(rows, pos)


def _sc_gather_rows(table, idx):
    n_idx = idx.shape[0]
    width = table.shape[1]
    per_worker = n_idx // SC_WORKERS
    n_pairs = per_worker // (2 * SC_CHUNK)
    assert per_worker % (2 * SC_CHUNK) == 0
    buf = [pltpu.VMEM((SC_CHUNK,), jnp.int32), pltpu.VMEM((SC_CHUNK, width), table.dtype), pltpu.SemaphoreType.DMA]

    @functools.partial(pl.kernel, mesh=_sc_mesh(), out_type=jax.ShapeDtypeStruct((n_idx, width), table.dtype),
                       scratch_types=buf + buf)
    def gather(table_hbm, idx_hbm, out_hbm, *scratch):
        base = _sc_worker_id() * per_worker
        buf_a, buf_b = scratch[:3], scratch[3:]

        def fetch(chunk, idx_v, rows_v, sem):
            pltpu.sync_copy(idx_hbm.at[pl.ds(base + chunk * SC_CHUNK, SC_CHUNK)], idx_v)
            pltpu.async_copy(table_hbm.at[idx_v], rows_v, sem)

        def drain(chunk, idx_v, rows_v, sem):
            pltpu.make_async_copy(table_hbm.at[idx_v], rows_v, sem).wait()
            pltpu.sync_copy(rows_v, out_hbm.at[pl.ds(base + chunk * SC_CHUNK, SC_CHUNK)])

        fetch(0, *buf_a)
        def body(j, carry):
            fetch(2 * j + 1, *buf_b)
            drain(2 * j, *buf_a)

            @pl.when(j + 1 < n_pairs)
            def _():
                fetch(2 * j + 2, *buf_a)

            drain(2 * j + 1, *buf_b)
            return carry

        lax.fori_loop(0, n_pairs, body, 0)

    return gather(table, idx)


def _experts_kernel(tile_expert_ref, n_used_ref, hs_ref, wg_ref, wu_ref, wd_ref, ys_ref):
    del tile_expert_ref

    @pl.when(pl.program_id(0) < n_used_ref[0])
    def _():
        lo, hi = _unpack_bf16_pair(hs_ref[...])
        lo = lo.astype(jnp.bfloat16)
        hi = hi.astype(jnp.bfloat16)
        bf = jnp.bfloat16
        gate = (jnp.dot(lo, wg_ref[0:PACK_W, :].astype(bf), preferred_element_type=jnp.float32)
                + jnp.dot(hi, wg_ref[PACK_W:D_MODEL, :].astype(bf), preferred_element_type=jnp.float32))
        up = (jnp.dot(lo, wu_ref[0:PACK_W, :].astype(bf), preferred_element_type=jnp.float32)
              + jnp.dot(hi, wu_ref[PACK_W:D_MODEL, :].astype(bf), preferred_element_type=jnp.float32))
        hid = ((gate * jax.nn.sigmoid(gate)) * up).astype(bf)
        y = jnp.dot(hid, wd_ref[...].astype(bf), preferred_element_type=jnp.float32)
        ys_ref[...] = _pack_bf16_pair(y[:, 0:PACK_W], y[:, PACK_W:D_MODEL])


def _experts(hs, tile_expert, n_used, wg, wu, wd, layer):
    n_rows = hs.shape[0]
    weights = lambda i, te, nu: (layer, te[i], 0, 0)
    return pl.pallas_call(
        _experts_kernel,
        out_shape=jax.ShapeDtypeStruct((n_rows, PACK_W), jnp.int32),
        grid_spec=pltpu.PrefetchScalarGridSpec(
            num_scalar_prefetch=2,
            grid=(n_rows // ROW_TILE,),
            in_specs=[pl.BlockSpec((ROW_TILE, PACK_W), lambda i, te, nu: (i, 0)),
                      pl.BlockSpec((None, None, D_MODEL, D_EXPERT), weights),
                      pl.BlockSpec((None, None, D_MODEL, D_EXPERT), weights),
                      pl.BlockSpec((None, None, D_EXPERT, D_MODEL), weights)],
            out_specs=pl.BlockSpec((ROW_TILE, PACK_W), lambda i, te, nu: (i, 0))),
        compiler_params=pltpu.CompilerParams(dimension_semantics=("arbitrary",), vmem_limit_bytes=VMEM_LIMIT),
        name="moe_experts",
    )(tile_expert, n_used, hs, wg, wu, wd)


COMB_TM = 1024


def _combine_kernel(x1_ref, y1_ref, y2_ref, cw_ref, gf_ref, out_ref, *, final_norm):
    c1 = cw_ref[:, 0:1]
    c2 = cw_ref[:, 1:2]
    lo1, hi1 = _unpack_bf16_pair(y1_ref[...])
    lo2, hi2 = _unpack_bf16_pair(y2_ref[...])
    y = jnp.concatenate([x1_ref[:, 0:PACK_W] + (c1 * lo1 + c2 * lo2),
                         x1_ref[:, PACK_W:D_MODEL] + (c1 * hi1 + c2 * hi2)], axis=1)
    if final_norm:
        ms = jnp.mean(y * y, axis=1, keepdims=True)
        y = y * lax.rsqrt(ms + RMS_EPS) * gf_ref[...]
    out_ref[...] = y


def _combine(x1, y12, cw, gf, final_norm):
    n_tok = x1.shape[0]
    tm = min(COMB_TM, n_tok)
    n_blk = n_tok // tm
    return pl.pallas_call(
        functools.partial(_combine_kernel, final_norm=final_norm),
        out_shape=jax.ShapeDtypeStruct((n_tok, D_MODEL), jnp.float32),
        grid=(n_blk,),
        in_specs=[pl.BlockSpec((tm, D_MODEL), lambda i: (i, 0)),
                  pl.BlockSpec((tm, PACK_W), lambda i: (i, 0)),
                  pl.BlockSpec((tm, PACK_W), lambda i: (n_blk + i, 0)),
                  pl.BlockSpec((tm, ROUTER_W), lambda i: (i, 0)),
                  pl.BlockSpec((1, D_MODEL), lambda i: (0, 0))],
        out_specs=pl.BlockSpec((tm, D_MODEL), lambda i: (i, 0)),
        compiler_params=pltpu.CompilerParams(dimension_semantics=("arbitrary",), vmem_limit_bytes=VMEM_LIMIT),
        name="moe_combine",
    )(x1, y12, y12, cw, gf)


def _sparse_moe(x1, h2p, route, cw, w_gate, w_up, w_down, layer, gf, final_norm):
    n_tok = x1.shape[0]
    n_rows = 2 * n_tok + N_EXPERTS * ROW_TILE
    rank, counts = _dispatch(route)
    padded = (counts[:, 0].astype(jnp.int32) + ROW_TILE - 1) // ROW_TILE * ROW_TILE
    ends = jnp.cumsum(padded)
    starts = ends - padded
    tile_start = jnp.arange(n_rows // ROW_TILE, dtype=jnp.int32) * ROW_TILE
    tile_expert = jnp.minimum(jnp.sum(tile_start[:, None] >= ends[None, :], axis=1), N_EXPERTS - 1).astype(jnp.int32)
    n_used = (ends[N_EXPERTS - 1:] // ROW_TILE).astype(jnp.int32)

    pos = _positions(route, rank, starts.astype(jnp.float32).reshape(N_EXPERTS, 1))
    pos12 = jnp.concatenate([pos[0], pos[1]]).astype(jnp.int32)
    hs = _sc_scatter_rows(h2p, pos12, n_rows)
    ys = _experts(hs, tile_expert, n_used, w_gate, w_up, w_down, layer)
    y12 = _sc_gather_rows(ys, pos12)
    return _combine(x1, y12, cw, gf, final_norm)


def _split_bf16(w):
    hi = w.astype(jnp.bfloat16)
    lo = (w - hi.astype(jnp.float32)).astype(jnp.bfloat16)
    return jnp.stack([hi, lo])


def _qkv_col_scale():
    s = np.ones((QKV_W,), np.float32)
    for group, dim in enumerate((HEAD_DIM, DIFF_DIM, HEAD_DIM, HEAD_DIM)):
        s[3 * group * GROUP_W:(3 * group + 1) * GROUP_W] = LOG2E * dim ** -0.5
    return s


def _forget_weights(w_f, b_f):
    rows = jnp.zeros((F_ROWS, D_MODEL), jnp.float32)
    bias = jnp.zeros((F_ROWS, 1), jnp.float32)
    for h in range(HEADS_PER_MIXER):
        r = (h // 2) * 8 + (h % 2)
        rows = rows.at[r].set(w_f[:, h])
        bias = bias.at[r, 0].set(b_f[h])
    return _split_bf16(rows), bias


def kernel(x, rel_bias, ln1, w_in, w_out, lam_q1, lam_k1, lam_q2, lam_k2, subln_g, b_forget,
           ln2, w_group, b_group, w_expert, b_expert, w_gate, w_up, w_down, ln_f):
    batch, seq, _ = x.shape
    depth = ln1.shape[0]
    tabs = _bias_tables(rel_bias)
    xt = x.reshape(batch * seq, D_MODEL)
    gf = ln_f.reshape(1, D_MODEL)
    col_scale = _qkv_col_scale()
    for l in range(depth):
        lambda_init = 0.8 - 0.6 * math.exp(-0.3 * l)
        wf, bf = _forget_weights(w_in[l][:, QKV_W:], b_forget[l])
        w_qkv = (w_in[l][:, :QKV_W] * col_scale).astype(jnp.bfloat16)
        qkv, lf = _norm_proj(xt, ln1[l].reshape(1, D_MODEL), w_qkv, wf, bf, batch, seq)
        c = _cumsum(lf)

        o_a = _dilated(qkv, tabs, batch, seq)
        lamv = jnp.zeros((8, LANES), jnp.float32)
        lamv = lamv.at[0:4, 0:DIFF_DIM].set(jnp.stack([lam_q1[l], lam_k1[l], lam_q2[l], lam_k2[l]]))
        g_sub = jnp.tile(subln_g[l], 2).reshape(1, PAIR_W)
        o_b = _diff(qkv, tabs, lamv, g_sub, lambda_init, batch, seq)
        o_c = _moba(qkv, tabs, batch, seq)
        o_d = _fox(qkv, c, batch, seq)

        w_router = jnp.zeros((D_MODEL, ROUTER_W), jnp.float32)
        w_router = w_router.at[:, :N_EXPERTS].set(w_expert[l]).at[:, N_EXPERTS:N_EXPERTS + N_GROUPS].set(w_group[l])
        b_router = jnp.zeros((1, ROUTER_W), jnp.float32)
        b_router = b_router.at[0, :N_EXPERTS].set(b_expert[l].reshape(-1)).at[0, N_EXPERTS:N_EXPERTS + N_GROUPS].set(b_group[l])
        x1, h2p, route, cw = _outproj_router([o_a, o_b, o_c, o_d], w_out[l].astype(jnp.bfloat16), xt,
                                             ln2[l].reshape(1, D_MODEL), _split_bf16(w_router), b_router)
        xt = _sparse_moe(x1, h2p, route, cw, w_gate, w_up, w_down, l, gf, final_norm=(l == depth - 1))
    return xt.reshape(batch, seq, D_MODEL)
```

```python
import functools
import math
from typing import NamedTuple

import jax
import jax.numpy as jnp
import numpy as np
from jax import lax
from jax.experimental import pallas as pl
from jax.experimental.pallas import tpu as pltpu

D_MODEL = 1024
HEAD_DIM = 64
HEADS_PER_MIXER = 4
GROUP_W = HEADS_PER_MIXER * HEAD_DIM
QKV_W = 12 * GROUP_W
DIFF_DIM = HEAD_DIM // 2
MOBA_BLOCK = 256
MOBA_TOPK = 3
N_BUCKETS = 32
MAX_DISTANCE = 2048
N_BIAS_HEADS = 12
N_GROUPS = 4
EXPERTS_PER_GROUP = 8
N_EXPERTS = 32
D_EXPERT = 256
RMS_EPS = 1e-6
NEG = -1e30
LOG2E = math.log2(math.e)

LANES = 128
PAIR_W = 2 * HEAD_DIM
ACC_W = 2 * PAIR_W
TQ = 256
TK = 256
MAX_SUB = 8
N_DIL_TILES = MAX_DISTANCE // TK + 1
FAR_TILE = 7
MIN_DIST = 1 - MAX_SUB
N_TAB_TILES = N_DIL_TILES + 1 - MIN_DIST
VMEM_LIMIT = 48 * 1024 * 1024

_NT = (((1,), (1,)), ((), ()))


def _t5_thresholds():
    d = np.arange(0, 4 * MAX_DISTANCE, dtype=np.int64)
    max_exact = N_BUCKETS // 2
    df = np.maximum(d, 1).astype(np.float32)
    large = max_exact + (np.log(df / np.float32(max_exact)) / np.float32(math.log(MAX_DISTANCE / max_exact))
                         * np.float32(N_BUCKETS - max_exact)).astype(np.int32)
    bucket = np.where(d < max_exact, d, np.minimum(large, N_BUCKETS - 1))
    return [int(np.argmax(bucket >= b)) for b in range(N_BUCKETS)]


_T5_THR = _t5_thresholds()
assert (FAR_TILE - 1) * TK + 1 >= _T5_THR[N_BUCKETS - 1]


def _lane_iota(shape):
    return lax.broadcasted_iota(jnp.int32, shape, len(shape) - 1)


def _bias_table_kernel(bias_ref, out_ref):
    h = pl.program_id(0)
    n = pl.program_id(1) + MIN_DIST
    i = lax.broadcasted_iota(jnp.int32, (TQ, TK), 0)
    j = lax.broadcasted_iota(jnp.int32, (TQ, TK), 1)
    d = n * TK + i - j
    is_dil = h < HEADS_PER_MIXER

    def store(val):
        mult = ((d <= 128).astype(jnp.int32)
                + jnp.logical_and((d & 3) == 0, d <= 512).astype(jnp.int32)
                + jnp.logical_and((d & 15) == 0, d <= 2048).astype(jnp.int32))
        logm = jnp.where(mult == 3, math.log(3.0), jnp.where(mult == 2, math.log(2.0), 0.0))
        val = val + jnp.where(is_dil, logm, 0.0)
        valid = jnp.logical_and(d >= 0, jnp.logical_or(mult > 0, jnp.logical_not(is_dil)))
        out_ref[...] = jnp.where(valid, val * LOG2E, NEG)

    @pl.when(n < 0)
    def _():
        out_ref[...] = jnp.full((TQ, TK), NEG, jnp.float32)

    @pl.when(n >= FAR_TILE)
    def _():
        store(jnp.full((TQ, TK), bias_ref[h, N_BUCKETS - 1], jnp.float32))

    @pl.when(jnp.logical_and(n >= 0, n < FAR_TILE))
    def _():
        val = jnp.full((TQ, TK), bias_ref[h, 0], jnp.float32)
        for b in range(1, N_BUCKETS):
            val = jnp.where(d >= _T5_THR[b], bias_ref[h, b], val)
        store(val)


def _bias_tables(rel_bias):
    bias_h = rel_bias.T.astype(jnp.float32)
    return pl.pallas_call(
        _bias_table_kernel,
        out_shape=jax.ShapeDtypeStruct((N_BIAS_HEADS, N_TAB_TILES, TQ, TK), jnp.float32),
        grid=(N_BIAS_HEADS, N_TAB_TILES),
        in_specs=[pl.BlockSpec(memory_space=pltpu.SMEM)],
        out_specs=pl.BlockSpec((None, None, TQ, TK), lambda h, n: (h, n, 0, 0)),
        name="bias_tables",
    )(bias_h)


PROJ_TM = 1024
PROJ_CW = 512
F_ROWS = 16


def _norm_proj_kernel(x_ref, g_ref, w_ref, wf_ref, bf_ref, qkv_ref, lf_ref):
    x = x_ref[...]
    ms = jnp.mean(x * x, axis=1, keepdims=True)
    h = x * lax.rsqrt(ms + RMS_EPS) * g_ref[...]
    hb = h.astype(jnp.bfloat16)
    for c in range(QKV_W // PROJ_CW):
        cols = slice(c * PROJ_CW, (c + 1) * PROJ_CW)
        qkv_ref[:, cols] = jnp.dot(hb, w_ref[:, cols], preferred_element_type=jnp.float32).astype(jnp.bfloat16)
    h_lo = (h - hb.astype(jnp.float32)).astype(jnp.bfloat16)
    wf_hi = wf_ref[0]
    wf_lo = wf_ref[1]
    z = (lax.dot_general(wf_hi, hb, _NT, preferred_element_type=jnp.float32)
         + lax.dot_general(wf_hi, h_lo, _NT, preferred_element_type=jnp.float32)
         + lax.dot_general(wf_lo, hb, _NT, preferred_element_type=jnp.float32))
    z = z + bf_ref[...]
    lf_ref[...] = jnp.minimum(z, 0.0) - jnp.log(1.0 + jnp.exp(-jnp.abs(z)))


def _norm_proj(x2d, g, w_qkv, wf, bf, batch, seq):
    n_tok = x2d.shape[0]
    tm = min(PROJ_TM, seq)
    per_b = seq // tm
    return pl.pallas_call(
        _norm_proj_kernel,
        out_shape=(jax.ShapeDtypeStruct((n_tok, QKV_W), jnp.bfloat16),
                   jax.ShapeDtypeStruct((batch * F_ROWS, seq), jnp.float32)),
        grid=(n_tok // tm,),
        in_specs=[pl.BlockSpec((tm, D_MODEL), lambda i: (i, 0)),
                  pl.BlockSpec((1, D_MODEL), lambda i: (0, 0)),
                  pl.BlockSpec((D_MODEL, QKV_W), lambda i: (0, 0)),
                  pl.BlockSpec((2, F_ROWS, D_MODEL), lambda i: (0, 0, 0)),
                  pl.BlockSpec((F_ROWS, 1), lambda i: (0, 0))],
        out_specs=(pl.BlockSpec((tm, QKV_W), lambda i: (i, 0)),
                   pl.BlockSpec((F_ROWS, tm), lambda i: (i // per_b, i % per_b))),
        compiler_params=pltpu.CompilerParams(dimension_semantics=("arbitrary",), vmem_limit_bytes=VMEM_LIMIT),
        name="norm_proj",
    )(x2d, g, w_qkv, wf, bf)


def _split3(x):
    x1 = x.astype(jnp.bfloat16)
    r1 = x - x1.astype(jnp.float32)
    x2 = r1.astype(jnp.bfloat16)
    x3 = (r1 - x2.astype(jnp.float32)).astype(jnp.bfloat16)
    return x1, x2, x3


def _cumsum_kernel(lf_ref, c_ref):
    rows, seq = lf_ref.shape
    r = lax.broadcasted_iota(jnp.int32, (LANES, LANES), 0)
    c = lax.broadcasted_iota(jnp.int32, (LANES, LANES), 1)
    upper = jnp.where(r <= c, 1.0, 0.0).astype(jnp.bfloat16)

    def body(i, carry):
        off = pl.multiple_of(i * LANES, LANES)
        x1, x2, x3 = _split3(lf_ref[:, pl.ds(off, LANES)])
        cs = (jnp.dot(x1, upper, preferred_element_type=jnp.float32)
              + jnp.dot(x2, upper, preferred_element_type=jnp.float32)
              + jnp.dot(x3, upper, preferred_element_type=jnp.float32)) + carry
        c_ref[:, pl.ds(off, LANES)] = cs * LOG2E
        return cs[:, LANES - 1:LANES]

    lax.fori_loop(0, seq // LANES, body, jnp.zeros((rows, 1), jnp.float32))


def _cumsum(lf):
    return pl.pallas_call(
        _cumsum_kernel,
        out_shape=jax.ShapeDtypeStruct(lf.shape, jnp.float32),
        name="forget_cumsum",
    )(lf)


class _Geom(NamedTuple):
    n_streams: int
    n_sub: int
    step_tiles: int = 2

    @property
    def q_rows(self):
        return self.n_sub * TQ

    @property
    def step_keys(self):
        return self.step_tiles * TK

    def rows(self, sub, st):
        rb = sub * self.n_streams + st
        return slice(rb * TQ, (rb + 1) * TQ)

    def blocks(self):
        return [(sub, st, self.rows(sub, st)) for sub in range(self.n_sub) for st in range(self.n_streams)]

    def first_tile(self, blk, m):
        return self.n_sub * blk + self.n_sub - self.step_tiles * (m + 1)

    def dist(self, m, sub, w):
        return self.step_tiles * (m + 1) - self.n_sub + sub - w

    def n_steps(self, blk):
        return (self.n_sub // self.step_tiles) * (blk + 1)

    def own_steps(self):
        return self.n_sub // self.step_tiles

    def tiles_needed(self, m, sub):
        if not isinstance(m, int):
            return self.step_tiles
        return min(max(self.dist(m, sub, 0) + 1, 0), self.step_tiles)


def _stack_masked_q(q, qs_ref, geom):
    lane = _lane_iota((TQ, PAIR_W))
    width = PAIR_W // geom.n_streams
    for sub, st, rows in geom.blocks():
        q_tile = q[sub * TQ:(sub + 1) * TQ]
        qs_ref[rows, 0:PAIR_W] = jnp.where(lane // width == st, q_tile, jnp.zeros_like(q_tile))


def _flash_step(rows, s, v_aug, m_ref, acc_ref, first=False):
    s_max = jnp.max(s, axis=1, keepdims=True)
    m_new = jnp.broadcast_to(s_max, (s.shape[0], LANES)) if first else jnp.maximum(m_ref[rows, :], s_max)
    p = jnp.concatenate([jnp.exp2(s[:, c * LANES:(c + 1) * LANES] - m_new) for c in range(s.shape[1] // LANES)],
                        axis=1).astype(v_aug.dtype)
    pv = jnp.dot(p, v_aug, preferred_element_type=jnp.float32)
    if first:
        acc_ref[rows, :] = pv
    else:
        alpha = jnp.exp2(m_ref[rows, :] - m_new)
        acc_ref[rows, :] = jnp.concatenate([alpha, alpha], axis=1) * acc_ref[rows, :] + pv
    m_ref[rows, :] = m_new


def _normalised(acc_ref, rows):
    return acc_ref[rows, 0:PAIR_W] / acc_ref[rows, PAIR_W:ACC_W]


def _table_bias(tab_ref, hh, geom, m, sub, cap, n_tiles):
    return jnp.concatenate([tab_ref[hh, jnp.minimum(geom.dist(m, sub, w), cap) - MIN_DIST]
                            for w in range(n_tiles)], axis=1)


def _step_rows(ref, geom, blk, m):
    k0 = pl.multiple_of(geom.first_tile(blk, m) * TK, geom.step_keys)
    return ref[pl.ds(k0, geom.step_keys), :]


def _make_attend(geom, blk, scratch, v_ref, keys, finish):
    qs_ref, m_ref, acc_ref, s_ref = scratch
    blocks = geom.blocks()
    last_step = geom.n_steps(blk) - 1

    def raw(rows, rhs):
        return lax.dot_general(qs_ref[rows, :], rhs, _NT, preferred_element_type=jnp.float32)

    def v_aug(m):
        v = _step_rows(v_ref, geom, blk, m)
        return jnp.concatenate([v, jnp.ones_like(v)], axis=1)

    first_step = {sub: min(m for m in range(geom.own_steps()) if geom.tiles_needed(m, sub))
                  for sub in range(geom.n_sub)}

    def attend(rows, sub, st, scores, m, va, n_tiles):
        first = isinstance(m, int) and m == first_step[sub]
        _flash_step(rows, finish(sub, st, scores, m, n_tiles), va[0:n_tiles * TK], m_ref, acc_ref, first)

    def single(m):
        rhs = keys(m)
        va = v_aug(m)
        for sub, st, rows in blocks:
            n_tiles = geom.tiles_needed(m, sub)
            if n_tiles:
                attend(rows, sub, st, raw(rows, rhs[0:n_tiles * TK]), m, va, n_tiles)

    def step(m, next_is_static=False):
        va = v_aug(m)
        rhs_next = keys(jnp.minimum(m + 1, last_step))
        for sub, st, rows in blocks:
            n_tiles = geom.tiles_needed(m, sub)
            cur = s_ref[rows, 0:n_tiles * TK] if n_tiles else None
            next_tiles = geom.tiles_needed(m + 1, sub) if next_is_static else geom.step_tiles
            if next_tiles:
                s_ref[rows, 0:next_tiles * TK] = raw(rows, rhs_next[0:next_tiles * TK])
            if n_tiles:
                attend(rows, sub, st, cur, m, va, n_tiles)

    def run(m0, count):
        peeled = [m0 + i for i in range(geom.own_steps()) if isinstance(m0, int)
                  and any(geom.tiles_needed(m0 + i, sub) < geom.step_tiles for sub in range(geom.n_sub))]
        starts_at_own = isinstance(m0, int) and m0 == 0
        assert not starts_at_own or len(peeled) == geom.own_steps()

        @pl.when(count > 0)
        def _():
            rhs = keys(m0)
            for sub, _, rows in blocks:
                n_tiles = geom.tiles_needed(m0, sub)
                if n_tiles:
                    s_ref[rows, 0:n_tiles * TK] = raw(rows, rhs[0:n_tiles * TK])

        for m in peeled:
            pl.when(count > m - m0)(functools.partial(step, m, next_is_static=(m + 1) in peeled))

        def body(i, carry):
            step(m0 + i)
            return carry

        lax.fori_loop(len(peeled), count, body, 0)

    return single, run


def _attn_call(kernel, group, qkv, extra_inputs, extra_specs, geom, batch, seq, *, q_width=PAIR_W,
               extra_scratch=(), name=None):
    assert seq % geom.q_rows == 0 and seq % geom.step_keys == 0 and geom.n_sub % geom.step_tiles == 0
    nq = seq // geom.q_rows
    qc, kc, vc = 6 * group, 6 * group + 2, 6 * group + 4
    rows = geom.n_sub * geom.n_streams * TQ
    return pl.pallas_call(
        functools.partial(kernel, geom=geom),
        out_shape=jax.ShapeDtypeStruct((batch * seq, GROUP_W), jnp.bfloat16),
        grid=(2, batch, nq),
        in_specs=[pl.BlockSpec((geom.q_rows, PAIR_W), lambda p, b, i: (b * nq + i, qc + p)),
                  pl.BlockSpec((seq, PAIR_W), lambda p, b, i: (b, kc + p)),
                  pl.BlockSpec((seq, PAIR_W), lambda p, b, i: (b, vc + p))] + list(extra_specs),
        out_specs=pl.BlockSpec((geom.q_rows, PAIR_W), lambda p, b, i: (b * nq + i, p)),
        scratch_shapes=[pltpu.VMEM((rows, q_width), jnp.bfloat16),
                        pltpu.VMEM((rows, LANES), jnp.float32),
                        pltpu.VMEM((rows, ACC_W), jnp.float32),
                        pltpu.VMEM((rows, geom.step_keys), jnp.float32)
                        ] + list(extra_scratch),
        compiler_params=pltpu.CompilerParams(dimension_semantics=("arbitrary", "arbitrary", "arbitrary"),
                                             vmem_limit_bytes=VMEM_LIMIT),
        name=name,
    )(qkv, qkv, qkv, *extra_inputs)


def _tab_spec(first_head):
    return pl.BlockSpec((2, N_TAB_TILES, TQ, TK), lambda p, b, i: (first_head // 2 + p, 0, 0, 0))


def _merge_pair(o0, o1):
    return jnp.where(_lane_iota(o0.shape) < HEAD_DIM, o0, o1)


def _store_pair_outputs(o_ref, acc_ref, geom):
    for sub in range(geom.n_sub):
        o = _merge_pair(_normalised(acc_ref, geom.rows(sub, 0)), _normalised(acc_ref, geom.rows(sub, 1)))
        o_ref[sub * TQ:(sub + 1) * TQ, :] = o.astype(o_ref.dtype)


PAIR_GEOM = _Geom(n_streams=2, n_sub=4)
MOBA_GEOM = _Geom(n_streams=2, n_sub=MAX_SUB)
FOX_GEOM = _Geom(n_streams=2, n_sub=2)
DIFF_GEOM = _Geom(n_streams=4, n_sub=4)


def _dilated_kernel(q_ref, k_ref, v_ref, tab_ref, o_ref, qs_ref, m_ref, acc_ref, s_ref, *, geom):
    blk = pl.program_id(2)
    _stack_masked_q(q_ref[...], qs_ref, geom)
    _, run = _make_attend(
        geom, blk, (qs_ref, m_ref, acc_ref, s_ref), v_ref,
        keys=lambda m: _step_rows(k_ref, geom, blk, m),
        finish=lambda sub, st, raw, m, n_tiles: raw + _table_bias(tab_ref, st, geom, m, sub, N_DIL_TILES, n_tiles))
    reach = -(-(N_DIL_TILES - 1 + geom.n_sub) // geom.step_tiles)
    run(0, jnp.minimum(geom.n_steps(blk), reach))
    _store_pair_outputs(o_ref, acc_ref, geom)


def _dilated(qkv, tabs, batch, seq):
    return _attn_call(_dilated_kernel, 0, qkv, [tabs], [_tab_spec(0)], PAIR_GEOM, batch, seq, name="dilated_attn")


def _diff_kernel(q_ref, k_ref, v_ref, tab_ref, lam_ref, g_ref, o_ref, qs_ref, m_ref, acc_ref, s_ref, *,
                 lambda_init, geom):
    blk = pl.program_id(2)
    _stack_masked_q(q_ref[...], qs_ref, geom)
    _, run = _make_attend(
        geom, blk, (qs_ref, m_ref, acc_ref, s_ref), v_ref,
        keys=lambda m: _step_rows(k_ref, geom, blk, m),
        finish=lambda sub, st, raw, m, n_tiles: raw + _table_bias(tab_ref, st // 2, geom, m, sub, FAR_TILE, n_tiles))
    run(0, geom.n_steps(blk))

    lamv = lam_ref[...]
    lam = (jnp.exp(jnp.sum(lamv[0:1] * lamv[1:2], axis=1, keepdims=True))
           - jnp.exp(jnp.sum(lamv[2:3] * lamv[3:4], axis=1, keepdims=True)) + lambda_init)
    first = _lane_iota((TQ, PAIR_W)) < HEAD_DIM
    for sub in range(geom.n_sub):
        part = [_normalised(acc_ref, geom.rows(sub, st)) for st in range(geom.n_streams)]
        o = _merge_pair(part[0] - lam * part[1], part[2] - lam * part[3])
        sq = o * o
        ms0 = jnp.sum(jnp.where(first, sq, 0.0), axis=1, keepdims=True)
        ms1 = jnp.sum(jnp.where(first, 0.0, sq), axis=1, keepdims=True)
        ms = jnp.where(first, ms0, ms1) * (1.0 / HEAD_DIM)
        y = (o * lax.rsqrt(ms + RMS_EPS) * g_ref[...]) * (1.0 - lambda_init)
        o_ref[sub * TQ:(sub + 1) * TQ, :] = y.astype(o_ref.dtype)


def _diff(qkv, tabs, lamv, g_sub, lambda_init, batch, seq):
    return _attn_call(functools.partial(_diff_kernel, lambda_init=lambda_init), 1, qkv, [tabs, lamv, g_sub],
                      [_tab_spec(4), pl.BlockSpec((8, LANES), lambda p, b, i: (0, 0)),
                       pl.BlockSpec((1, PAIR_W), lambda p, b, i: (0, 0))],
                      DIFF_GEOM, batch, seq, name="diff_attn")


def _moba_kernel(q_ref, k_ref, v_ref, tab_ref, o_ref, qs_ref, m_ref, acc_ref, s_ref, km_ref, *, geom):
    blk = pl.program_id(2)
    n_blk = k_ref.shape[0] // MOBA_BLOCK
    assert n_blk <= LANES and MOBA_BLOCK == TQ == TK

    @pl.when(blk == 0)
    def _():
        km_ref[...] = jnp.zeros(km_ref.shape, jnp.float32)
        for n in range(n_blk):
            km_ref[n:n + 1, :] = jnp.sum(k_ref[n * MOBA_BLOCK:(n + 1) * MOBA_BLOCK, :].astype(jnp.float32),
                                         axis=0, keepdims=True) * (1.0 / MOBA_BLOCK)

    _stack_masked_q(q_ref[...], qs_ref, geom)
    nb = -(-n_blk // 8) * 8
    km = km_ref[0:nb, :]
    km_hi = km.astype(jnp.bfloat16)
    km_lo = (km - km_hi.astype(jnp.float32)).astype(jnp.bfloat16)
    qs = qs_ref[:, 0:PAIR_W]
    gate = (lax.dot_general(km_hi, qs, _NT, preferred_element_type=jnp.float32)
            + lax.dot_general(km_lo, qs, _NT, preferred_element_type=jnp.float32))
    cand = lax.broadcasted_iota(jnp.int32, gate.shape, 0)
    cand_f = cand.astype(jnp.float32)
    own = geom.n_sub * blk + _lane_iota(gate.shape) // (geom.n_streams * TQ)
    g = jnp.where(cand < own, gate, NEG)
    sel = cand == own
    for _ in range(MOBA_TOPK):
        mx = jnp.max(g, axis=0, keepdims=True)
        first = jnp.min(jnp.where(g == mx, cand_f, 2.0 * LANES), axis=0, keepdims=True)
        pick = cand_f == first
        sel = jnp.logical_or(sel, jnp.logical_and(pick, mx > 0.5 * NEG))
        g = jnp.where(pick, -3e38, g)
    mask = jnp.where(sel, 0.0, NEG)
    if nb < LANES:
        mask = jnp.concatenate([mask, jnp.full((LANES - nb, mask.shape[1]), NEG, jnp.float32)], axis=0)
    qs_ref[:, PAIR_W:2 * PAIR_W] = mask.T.astype(jnp.bfloat16)

    def keys(m):
        k = _step_rows(k_ref, geom, blk, m)
        block_of_key = geom.first_tile(blk, m) + lax.broadcasted_iota(jnp.int32, (geom.step_keys, LANES), 0) // TK
        onehot = jnp.where(_lane_iota((geom.step_keys, LANES)) == block_of_key, 1.0, 0.0).astype(k.dtype)
        return jnp.concatenate([k, onehot], axis=1)

    _, run = _make_attend(
        geom, blk, (qs_ref, m_ref, acc_ref, s_ref), v_ref, keys=keys,
        finish=lambda sub, st, raw, m, n_tiles: raw + _table_bias(tab_ref, st, geom, m, sub, FAR_TILE, n_tiles))
    run(0, geom.n_steps(blk))
    _store_pair_outputs(o_ref, acc_ref, geom)


def _moba(qkv, tabs, batch, seq):
    return _attn_call(_moba_kernel, 2, qkv, [tabs], [_tab_spec(8)], MOBA_GEOM, batch, seq, q_width=2 * PAIR_W,
                      extra_scratch=[pltpu.VMEM((LANES, PAIR_W), jnp.float32)], name="moba_attn")


FOX_SKIP_BITS = 48.0


def _head_row_norms(x, hh):
    xf = x.astype(jnp.float32)
    sq = jnp.where(_lane_iota(xf.shape) // HEAD_DIM == hh, xf * xf, 0.0)
    return jnp.sqrt(jnp.max(jnp.sum(sq, axis=1, keepdims=True), axis=0, keepdims=True))


def _fox_kernel(q_ref, k_ref, v_ref, c_ref, o_ref, qs_ref, m_ref, acc_ref, s_ref, cq_ref, stat_ref, *, geom):
    blk = pl.program_id(2)
    n_tiles = k_ref.shape[0] // TK
    assert n_tiles <= LANES

    @pl.when(blk == 0)
    def _():
        stat_ref[...] = jnp.zeros(stat_ref.shape, jnp.float32)
        for j in range(n_tiles):
            k_tile = k_ref[j * TK:(j + 1) * TK, :]
            for hh in range(2):
                stat_ref[hh:hh + 1, j:j + 1] = _head_row_norms(k_tile, hh)
                stat_ref[2 + hh:3 + hh, j:j + 1] = jnp.min(c_ref[hh:hh + 1, j * TK:(j + 1) * TK], axis=1, keepdims=True)

    q = q_ref[...]
    _stack_masked_q(q, qs_ref, geom)
    q0 = pl.multiple_of(blk * geom.q_rows, geom.q_rows)
    row = lax.broadcasted_iota(jnp.int32, (TQ, TQ), 0)
    col = lax.broadcasted_iota(jnp.int32, (TQ, TQ), 1)
    for sub, hh, rows in geom.blocks():
        c_row = c_ref[hh:hh + 1, pl.ds(q0 + sub * TQ, TQ)]
        c_col = jnp.sum(jnp.where(row == col, c_row, 0.0), axis=1, keepdims=True)
        cq_ref[rows, :] = jnp.broadcast_to(c_col, (TQ, LANES))

    def finish(sub, hh, raw, m, n_tiles):
        n_keys = n_tiles * TK
        k0 = pl.multiple_of(geom.first_tile(blk, m) * TK, TK)
        cq = cq_ref[geom.rows(sub, hh), :]
        s = raw + (jnp.concatenate([cq] * (n_keys // LANES), axis=1) - c_ref[hh:hh + 1, pl.ds(k0, n_keys)])
        if isinstance(m, int) and m < geom.own_steps():
            key_pos = lax.broadcasted_iota(jnp.int32, (TQ, n_keys), 1)
            query_pos = lax.broadcasted_iota(jnp.int32, (TQ, n_keys), 0)
            first_key = (geom.n_sub - geom.step_tiles * (m + 1)) * TK
            s = jnp.where(key_pos + first_key <= query_pos + sub * TQ, s, NEG)
        return s

    single, run = _make_attend(geom, blk, (qs_ref, m_ref, acc_ref, s_ref), v_ref,
                               keys=lambda m: _step_rows(k_ref, geom, blk, m), finish=finish)
    for m in range(geom.own_steps()):
        single(m)

    tile = _lane_iota((1, LANES))
    needed = jnp.zeros((1, LANES), jnp.bool_)
    for hh in range(2):
        rows_h = [geom.rows(sub, hh) for sub in range(geom.n_sub)]
        m_min = functools.reduce(jnp.minimum, [jnp.min(m_ref[r, :], axis=0, keepdims=True) for r in rows_h])
        cq_max = functools.reduce(jnp.maximum, [jnp.max(cq_ref[r, :], axis=0, keepdims=True) for r in rows_h])
        bound = _head_row_norms(q, hh) * stat_ref[hh:hh + 1, :] + cq_max - stat_ref[2 + hh:3 + hh, :] + 1.0
        needed = jnp.logical_or(needed, bound >= m_min - FOX_SKIP_BITS)
    n_before = geom.n_sub * blk
    first_needed = jnp.min(jnp.where(jnp.logical_and(needed, tile < n_before), tile, n_before))
    run(geom.own_steps(), (n_before - first_needed + geom.step_tiles - 1) // geom.step_tiles)
    _store_pair_outputs(o_ref, acc_ref, geom)


def _fox(qkv, c, batch, seq):
    return _attn_call(_fox_kernel, 3, qkv, [c.reshape(batch, 2, 8, seq)],
                      [pl.BlockSpec((None, None, 8, seq), lambda p, b, i: (b, p, 0, 0))],
                      FOX_GEOM, batch, seq,
                      extra_scratch=[pltpu.VMEM((FOX_GEOM.n_sub * FOX_GEOM.n_streams * TQ, LANES), jnp.float32),
                                     pltpu.VMEM((8, LANES), jnp.float32)], name="fox_attn")


OUT_TM = 1024
ROUTER_W = LANES
ROUTE_ROWS = 8


PACK_W = D_MODEL // 2


def _pack_bf16_pair(lo, hi):
    lo_bits = lax.bitcast_convert_type(lo.astype(jnp.bfloat16).astype(jnp.float32), jnp.uint32)
    hi_bits = lax.bitcast_convert_type(hi.astype(jnp.bfloat16).astype(jnp.float32), jnp.uint32)
    word = lax.shift_right_logical(lo_bits, jnp.uint32(16)) | (hi_bits & jnp.uint32(0xFFFF0000))
    return lax.bitcast_convert_type(word, jnp.int32)


def _unpack_bf16_pair(word):
    bits = lax.bitcast_convert_type(word, jnp.uint32)
    lo = lax.bitcast_convert_type(lax.shift_left(bits, jnp.uint32(16)), jnp.float32)
    hi = lax.bitcast_convert_type(bits & jnp.uint32(0xFFFF0000), jnp.float32)
    return lo, hi


def _outproj_router_kernel(oa_ref, ob_ref, oc_ref, od_ref, wo_ref, x_ref, g_ref, wr_ref, br_ref,
                           x1_ref, h2_ref, route_ref, cw_ref):
    acc = x_ref[...]
    for m, o_ref in enumerate((oa_ref, ob_ref, oc_ref, od_ref)):
        acc = acc + jnp.dot(o_ref[...], wo_ref[m * GROUP_W:(m + 1) * GROUP_W, :], preferred_element_type=jnp.float32)
    x1_ref[...] = acc
    ms = jnp.mean(acc * acc, axis=1, keepdims=True)
    h = acc * lax.rsqrt(ms + RMS_EPS) * g_ref[...]
    hb = h.astype(jnp.bfloat16)
    h2_ref[...] = _pack_bf16_pair(h[:, 0:PACK_W], h[:, PACK_W:D_MODEL])
    h_lo = (h - hb.astype(jnp.float32)).astype(jnp.bfloat16)
    logits = (jnp.dot(hb, wr_ref[0], preferred_element_type=jnp.float32)
              + jnp.dot(h_lo, wr_ref[0], preferred_element_type=jnp.float32)
              + jnp.dot(hb, wr_ref[1], preferred_element_type=jnp.float32)) + br_ref[...]
    lane = _lane_iota(logits.shape).astype(jnp.float32)
    big = 2.0 * LANES
    gmask = jnp.logical_and(lane >= N_EXPERTS, lane < N_EXPERTS + N_GROUPS)
    gl = jnp.where(gmask, logits, NEG)
    gmax = jnp.max(gl, axis=1, keepdims=True)
    glane = jnp.min(jnp.where(gl == gmax, lane, big), axis=1, keepdims=True)
    gsum = jnp.sum(jnp.where(gmask, jnp.exp(gl - gmax), 0.0), axis=1, keepdims=True)
    g_w = 1.0 / gsum
    e0 = (glane - N_EXPERTS) * EXPERTS_PER_GROUP
    emask = jnp.logical_and(lane >= e0, lane < e0 + EXPERTS_PER_GROUP)
    el = jnp.where(emask, logits, NEG)
    v1 = jnp.max(el, axis=1, keepdims=True)
    i1 = jnp.min(jnp.where(el == v1, lane, big), axis=1, keepdims=True)
    el2 = jnp.where(lane == i1, NEG, el)
    v2 = jnp.max(el2, axis=1, keepdims=True)
    i2 = jnp.min(jnp.where(el2 == v2, lane, big), axis=1, keepdims=True)
    e2 = jnp.exp(v2 - v1)
    den = 1.0 + e2
    c1 = g_w / den
    c2 = g_w * e2 / den
    quad = jnp.where(lane == 0.0, i1, jnp.where(lane == 1.0, i2, jnp.where(lane == 2.0, c1,
                                                                             jnp.where(lane == 3.0, c2, 0.0))))
    cw_ref[...] = jnp.where(lane == 0.0, c1, jnp.where(lane == 1.0, c2, 0.0))
    route_ref[...] = quad.T[0:ROUTE_ROWS, :]


def _outproj_router(o_parts, w_out, x2d, g2, wr, br):
    n_tok = x2d.shape[0]
    tm = min(OUT_TM, n_tok)
    row = lambda i: (i, 0)
    fixed = lambda i: (0, 0)
    return pl.pallas_call(
        _outproj_router_kernel,
        out_shape=(jax.ShapeDtypeStruct((n_tok, D_MODEL), jnp.float32),
                   jax.ShapeDtypeStruct((n_tok, PACK_W), jnp.int32),
                   jax.ShapeDtypeStruct((ROUTE_ROWS, n_tok), jnp.float32),
                   jax.ShapeDtypeStruct((n_tok, ROUTER_W), jnp.float32)),
        grid=(n_tok // tm,),
        in_specs=[pl.BlockSpec((tm, GROUP_W), row)] * 4 + [
            pl.BlockSpec((D_MODEL, D_MODEL), fixed),
            pl.BlockSpec((tm, D_MODEL), row),
            pl.BlockSpec((1, D_MODEL), fixed),
            pl.BlockSpec((2, D_MODEL, ROUTER_W), lambda i: (0, 0, 0)),
            pl.BlockSpec((1, ROUTER_W), fixed)],
        out_specs=(pl.BlockSpec((tm, D_MODEL), row), pl.BlockSpec((tm, PACK_W), row),
                   pl.BlockSpec((ROUTE_ROWS, tm), lambda i: (0, i)), pl.BlockSpec((tm, ROUTER_W), row)),
        compiler_params=pltpu.CompilerParams(dimension_semantics=("arbitrary",), vmem_limit_bytes=VMEM_LIMIT),
        name="outproj_router",
    )(*o_parts, w_out, x2d, g2, wr, br)


ROW_TILE = 512
DISP_TM = 1024
SC_CORES = 2
SC_SUBCORES = 16
SC_WORKERS = SC_CORES * SC_SUBCORES
SC_CHUNK = 64


def _dispatch_kernel(route_ref, rank_ref, cnt_ref, tri_ref, carry_ref):
    step = pl.program_id(0)
    tm = route_ref.shape[1]

    @pl.when(step == 0)
    def _():
        r = lax.broadcasted_iota(jnp.int32, (tm, tm), 0)
        c = lax.broadcasted_iota(jnp.int32, (tm, tm), 1)
        tri_ref[...] = jnp.where(r <= c, 1.0, 0.0).astype(jnp.bfloat16)
        carry_ref[...] = jnp.zeros(carry_ref.shape, jnp.float32)

    e1 = route_ref[0:1, :]
    e2 = route_ref[1:2, :]
    expert = lax.broadcasted_iota(jnp.int32, (N_EXPERTS, tm), 0).astype(jnp.float32)
    hit1 = e1 == expert
    hit2 = e2 == expert
    hits = jnp.where(jnp.logical_or(hit1, hit2), 1.0, 0.0).astype(jnp.bfloat16)
    cum = jnp.dot(hits, tri_ref[...], preferred_element_type=jnp.float32) + carry_ref[...]
    rank1 = jnp.sum(jnp.where(hit1, cum - 1.0, 0.0), axis=0, keepdims=True)
    rank2 = jnp.sum(jnp.where(hit2, cum - 1.0, 0.0), axis=0, keepdims=True)
    row = lax.broadcasted_iota(jnp.int32, (ROUTE_ROWS, tm), 0)
    rank_ref[...] = jnp.where(row == 0, rank1, jnp.where(row == 1, rank2, 0.0))
    total = cum[:, tm - 1:tm]
    carry_ref[...] = total
    cnt_ref[...] = jnp.broadcast_to(total, cnt_ref.shape)


def _dispatch(route):
    n_tok = route.shape[1]
    tm = min(DISP_TM, n_tok)
    return pl.pallas_call(
        _dispatch_kernel,
        out_shape=(jax.ShapeDtypeStruct((ROUTE_ROWS, n_tok), jnp.float32),
                   jax.ShapeDtypeStruct((N_EXPERTS, LANES), jnp.float32)),
        grid=(n_tok // tm,),
        in_specs=[pl.BlockSpec((ROUTE_ROWS, tm), lambda i: (0, i))],
        out_specs=(pl.BlockSpec((ROUTE_ROWS, tm), lambda i: (0, i)),
                   pl.BlockSpec((N_EXPERTS, LANES), lambda i: (0, 0))),
        scratch_shapes=[pltpu.VMEM((tm, tm), jnp.bfloat16), pltpu.VMEM((N_EXPERTS, 1), jnp.float32)],
        compiler_params=pltpu.CompilerParams(dimension_semantics=("arbitrary",)),
        name="moe_ranks",
    )(route)


def _positions_kernel(route_ref, rank_ref, off_ref, pos_ref):
    tm = route_ref.shape[1]
    expert = lax.broadcasted_iota(jnp.int32, (N_EXPERTS, tm), 0).astype(jnp.float32)
    off = off_ref[...]
    start1 = jnp.sum(jnp.where(route_ref[0:1, :] == expert, off, 0.0), axis=0, keepdims=True)
    start2 = jnp.sum(jnp.where(route_ref[1:2, :] == expert, off, 0.0), axis=0, keepdims=True)
    row = lax.broadcasted_iota(jnp.int32, (ROUTE_ROWS, tm), 0)
    pos_ref[...] = jnp.where(row == 0, start1 + rank_ref[0:1, :], jnp.where(row == 1, start2 + rank_ref[1:2, :], 0.0))


def _positions(route, rank, seg_start):
    n_tok = route.shape[1]
    tm = min(DISP_TM, n_tok)
    blk = pl.BlockSpec((ROUTE_ROWS, tm), lambda i: (0, i))
    return pl.pallas_call(
        _positions_kernel,
        out_shape=jax.ShapeDtypeStruct((ROUTE_ROWS, n_tok), jnp.float32),
        grid=(n_tok // tm,),
        in_specs=[blk, blk, pl.BlockSpec((N_EXPERTS, 1), lambda i: (0, 0))],
        out_specs=blk,
        name="moe_positions",
    )(route, rank, seg_start)


def _sc_mesh():
    from jax.experimental.pallas import tpu_sc as plsc
    return plsc.VectorSubcoreMesh(core_axis_name="c", subcore_axis_name="s")


def _sc_worker_id():
    return lax.axis_index("s") * SC_CORES + lax.axis_index("c")


def _sc_scatter_rows(rows, pos, n_out):
    n_tok, width = rows.shape
    per_worker = n_tok // SC_WORKERS
    n_pairs = per_worker // (2 * SC_CHUNK)
    assert per_worker % (2 * SC_CHUNK) == 0
    buf = [pltpu.VMEM((SC_CHUNK,), jnp.int32), pltpu.VMEM((SC_CHUNK,), jnp.int32),
           pltpu.VMEM((SC_CHUNK, width), rows.dtype), pltpu.SemaphoreType.DMA, pltpu.SemaphoreType.DMA]

    @functools.partial(pl.kernel, mesh=_sc_mesh(), out_type=jax.ShapeDtypeStruct((n_out, width), rows.dtype),
                       scratch_types=buf + buf)
    def scatter(rows_hbm, pos_hbm, out_hbm, *scratch):
        base = _sc_worker_id() * per_worker
        buf_a, buf_b = scratch[:5], scratch[5:]

        def load(chunk, p1_v, p2_v, rows_v, sem_in, sem_out):
            off = base + chunk * SC_CHUNK
            pltpu.sync_copy(pos_hbm.at[pl.ds(off, SC_CHUNK)], p1_v)
            pltpu.sync_copy(pos_hbm.at[pl.ds(n_tok + off, SC_CHUNK)], p2_v)
            pltpu.async_copy(rows_hbm.at[pl.ds(off, SC_CHUNK)], rows_v, sem_in)

        def flush(chunk, p1_v, p2_v, rows_v, sem_in, sem_out):
            off = base + chunk * SC_CHUNK
            pltpu.make_async_copy(rows_hbm.at[pl.ds(off, SC_CHUNK)], rows_v, sem_in).wait()
            first = pltpu.async_copy(rows_v, out_hbm.at[p1_v], sem_out)
            second = pltpu.async_copy(rows_v, out_hbm.at[p2_v], sem_out)
            first.wait()
            second.wait()

        load(0, *buf_a)
        def body(j, carry):
            load(2 * j + 1, *buf_b)
            flush(2 * j, *buf_a)

            @pl.when(j + 1 < n_pairs)
            def _():
                load(2 * j + 2, *buf_a)

            flush(2 * j + 1, *buf_b)
            return carry

        lax.fori_loop(0, n_pairs, body, 0)

    return scatter(rows, pos)


def _sc_gather_rows(table, idx):
    n_idx = idx.shape[0]
    width = table.shape[1]
    per_worker = n_idx // SC_WORKERS
    n_pairs = per_worker // (2 * SC_CHUNK)
    assert per_worker % (2 * SC_CHUNK) == 0
    buf = [pltpu.VMEM((SC_CHUNK,), jnp.int32), pltpu.VMEM((SC_CHUNK, width), table.dtype), pltpu.SemaphoreType.DMA]

    @functools.partial(pl.kernel, mesh=_sc_mesh(), out_type=jax.ShapeDtypeStruct((n_idx, width), table.dtype),
                       scratch_types=buf + buf)
    def gather(table_hbm, idx_hbm, out_hbm, *scratch):
        base = _sc_worker_id() * per_worker
        buf_a, buf_b = scratch[:3], scratch[3:]

        def fetch(chunk, idx_v, rows_v, sem):
            pltpu.sync_copy(idx_hbm.at[pl.ds(base + chunk * SC_CHUNK, SC_CHUNK)], idx_v)
            pltpu.async_copy(table_hbm.at[idx_v], rows_v, sem)

        def drain(chunk, idx_v, rows_v, sem):
            pltpu.make_async_copy(table_hbm.at[idx_v], rows_v, sem).wait()
            pltpu.sync_copy(rows_v, out_hbm.at[pl.ds(base + chunk * SC_CHUNK, SC_CHUNK)])

        fetch(0, *buf_a)
        def body(j, carry):
            fetch(2 * j + 1, *buf_b)
            drain(2 * j, *buf_a)

            @pl.when(j + 1 < n_pairs)
            def _():
                fetch(2 * j + 2, *buf_a)

            drain(2 * j + 1, *buf_b)
            return carry

        lax.fori_loop(0, n_pairs, body, 0)

    return gather(table, idx)


def _experts_kernel(tile_expert_ref, n_used_ref, hs_ref, wg_ref, wu_ref, wd_ref, ys_ref):
    del tile_expert_ref

    @pl.when(pl.program_id(0) < n_used_ref[0])
    def _():
        lo, hi = _unpack_bf16_pair(hs_ref[...])
        lo = lo.astype(jnp.bfloat16)
        hi = hi.astype(jnp.bfloat16)
        bf = jnp.bfloat16
        gate = (jnp.dot(lo, wg_ref[0:PACK_W, :].astype(bf), preferred_element_type=jnp.float32)
                + jnp.dot(hi, wg_ref[PACK_W:D_MODEL, :].astype(bf), preferred_element_type=jnp.float32))
        up = (jnp.dot(lo, wu_ref[0:PACK_W, :].astype(bf), preferred_element_type=jnp.float32)
              + jnp.dot(hi, wu_ref[PACK_W:D_MODEL, :].astype(bf), preferred_element_type=jnp.float32))
        hid = ((gate * jax.nn.sigmoid(gate)) * up).astype(bf)
        y = jnp.dot(hid, wd_ref[...].astype(bf), preferred_element_type=jnp.float32)
        ys_ref[...] = _pack_bf16_pair(y[:, 0:PACK_W], y[:, PACK_W:D_MODEL])


def _experts(hs, tile_expert, n_used, wg, wu, wd, layer):
    n_rows = hs.shape[0]
    weights = lambda i, te, nu: (layer, te[i], 0, 0)
    return pl.pallas_call(
        _experts_kernel,
        out_shape=jax.ShapeDtypeStruct((n_rows, PACK_W), jnp.int32),
        grid_spec=pltpu.PrefetchScalarGridSpec(
            num_scalar_prefetch=2,
            grid=(n_rows // ROW_TILE,),
            in_specs=[pl.BlockSpec((ROW_TILE, PACK_W), lambda i, te, nu: (i, 0)),
                      pl.BlockSpec((None, None, D_MODEL, D_EXPERT), weights),
                      pl.BlockSpec((None, None, D_MODEL, D_EXPERT), weights),
                      pl.BlockSpec((None, None, D_EXPERT, D_MODEL), weights)],
            out_specs=pl.BlockSpec((ROW_TILE, PACK_W), lambda i, te, nu: (i, 0))),
        compiler_params=pltpu.CompilerParams(dimension_semantics=("arbitrary",), vmem_limit_bytes=VMEM_LIMIT),
        name="moe_experts",
    )(tile_expert, n_used, hs, wg, wu, wd)


COMB_TM = 1024


def _combine_kernel(x1_ref, y1_ref, y2_ref, cw_ref, gf_ref, out_ref, *, final_norm):
    c1 = cw_ref[:, 0:1]
    c2 = cw_ref[:, 1:2]
    lo1, hi1 = _unpack_bf16_pair(y1_ref[...])
    lo2, hi2 = _unpack_bf16_pair(y2_ref[...])
    y = jnp.concatenate([x1_ref[:, 0:PACK_W] + (c1 * lo1 + c2 * lo2),
                         x1_ref[:, PACK_W:D_MODEL] + (c1 * hi1 + c2 * hi2)], axis=1)
    if final_norm:
        ms = jnp.mean(y * y, axis=1, keepdims=True)
        y = y * lax.rsqrt(ms + RMS_EPS) * gf_ref[...]
    out_ref[...] = y


def _combine(x1, y12, cw, gf, final_norm):
    n_tok = x1.shape[0]
    tm = min(COMB_TM, n_tok)
    n_blk = n_tok // tm
    return pl.pallas_call(
        functools.partial(_combine_kernel, final_norm=final_norm),
        out_shape=jax.ShapeDtypeStruct((n_tok, D_MODEL), jnp.float32),
        grid=(n_blk,),
        in_specs=[pl.BlockSpec((tm, D_MODEL), lambda i: (i, 0)),
                  pl.BlockSpec((tm, PACK_W), lambda i: (i, 0)),
                  pl.BlockSpec((tm, PACK_W), lambda i: (n_blk + i, 0)),
                  pl.BlockSpec((tm, ROUTER_W), lambda i: (i, 0)),
                  pl.BlockSpec((1, D_MODEL), lambda i: (0, 0))],
        out_specs=pl.BlockSpec((tm, D_MODEL), lambda i: (i, 0)),
        compiler_params=pltpu.CompilerParams(dimension_semantics=("arbitrary",), vmem_limit_bytes=VMEM_LIMIT),
        name="moe_combine",
    )(x1, y12, y12, cw, gf)


def _sparse_moe(x1, h2p, route, cw, w_gate, w_up, w_down, layer, gf, final_norm):
    n_tok = x1.shape[0]
    n_rows = 2 * n_tok + N_EXPERTS * ROW_TILE
    rank, counts = _dispatch(route)
    padded = (counts[:, 0].astype(jnp.int32) + ROW_TILE - 1) // ROW_TILE * ROW_TILE
    ends = jnp.cumsum(padded)
    starts = ends - padded
    tile_start = jnp.arange(n_rows // ROW_TILE, dtype=jnp.int32) * ROW_TILE
    tile_expert = jnp.minimum(jnp.sum(tile_start[:, None] >= ends[None, :], axis=1), N_EXPERTS - 1).astype(jnp.int32)
    n_used = (ends[N_EXPERTS - 1:] // ROW_TILE).astype(jnp.int32)

    pos = _positions(route, rank, starts.astype(jnp.float32).reshape(N_EXPERTS, 1))
    pos12 = jnp.concatenate([pos[0], pos[1]]).astype(jnp.int32)
    hs = _sc_scatter_rows(h2p, pos12, n_rows)
    ys = _experts(hs, tile_expert, n_used, w_gate, w_up, w_down, layer)
    y12 = _sc_gather_rows(ys, pos12)
    return _combine(x1, y12, cw, gf, final_norm)


def _split_bf16(w):
    hi = w.astype(jnp.bfloat16)
    lo = (w - hi.astype(jnp.float32)).astype(jnp.bfloat16)
    return jnp.stack([hi, lo])


def _qkv_col_scale():
    s = np.ones((QKV_W,), np.float32)
    for group, dim in enumerate((HEAD_DIM, DIFF_DIM, HEAD_DIM, HEAD_DIM)):
        s[3 * group * GROUP_W:(3 * group + 1) * GROUP_W] = LOG2E * dim ** -0.5
    return s


def _forget_weights(w_f, b_f):
    rows = jnp.zeros((F_ROWS, D_MODEL), jnp.float32)
    bias = jnp.zeros((F_ROWS, 1), jnp.float32)
    for h in range(HEADS_PER_MIXER):
        r = (h // 2) * 8 + (h % 2)
        rows = rows.at[r].set(w_f[:, h])
        bias = bias.at[r, 0].set(b_f[h])
    return _split_bf16(rows), bias


def kernel(x, rel_bias, ln1, w_in, w_out, lam_q1, lam_k1, lam_q2, lam_k2, subln_g, b_forget,
           ln2, w_group, b_group, w_expert, b_expert, w_gate, w_up, w_down, ln_f):
    batch, seq, _ = x.shape
    depth = ln1.shape[0]
    tabs = _bias_tables(rel_bias)
    xt = x.reshape(batch * seq, D_MODEL)
    gf = ln_f.reshape(1, D_MODEL)
    col_scale = _qkv_col_scale()
    for l in range(depth):
        lambda_init = 0.8 - 0.6 * math.exp(-0.3 * l)
        wf, bf = _forget_weights(w_in[l][:, QKV_W:], b_forget[l])
        w_qkv = (w_in[l][:, :QKV_W] * col_scale).astype(jnp.bfloat16)
        qkv, lf = _norm_proj(xt, ln1[l].reshape(1, D_MODEL), w_qkv, wf, bf, batch, seq)
        c = _cumsum(lf)

        o_a = _dilated(qkv, tabs, batch, seq)
        lamv = jnp.zeros((8, LANES), jnp.float32)
        lamv = lamv.at[0:4, 0:DIFF_DIM].set(jnp.stack([lam_q1[l], lam_k1[l], lam_q2[l], lam_k2[l]]))
        g_sub = jnp.tile(subln_g[l], 2).reshape(1, PAIR_W)
        o_b = _diff(qkv, tabs, lamv, g_sub, lambda_init, batch, seq)
        o_c = _moba(qkv, tabs, batch, seq)
        o_d = _fox(qkv, c, batch, seq)

        w_router = jnp.zeros((D_MODEL, ROUTER_W), jnp.float32)
        w_router = w_router.at[:, :N_EXPERTS].set(w_expert[l]).at[:, N_EXPERTS:N_EXPERTS + N_GROUPS].set(w_group[l])
        b_router = jnp.zeros((1, ROUTER_W), jnp.float32)
        b_router = b_router.at[0, :N_EXPERTS].set(b_expert[l].reshape(-1)).at[0, N_EXPERTS:N_EXPERTS + N_GROUPS].set(b_group[l])
        x1, h2p, route, cw = _outproj_router([o_a, o_b, o_c, o_d], w_out[l].astype(jnp.bfloat16), xt,
                                             ln2[l].reshape(1, D_MODEL), _split_bf16(w_router), b_router)
        xt = _sparse_moe(x1, h2p, route, cw, w_gate, w_up, w_down, l, gf, final_norm=(l == depth - 1))
    return xt.reshape(batch, seq, D_MODEL)
```

```python
import functools
import math
from typing import NamedTuple

import jax
import jax.numpy as jnp
import numpy as np
from jax import lax
from jax.experimental import pallas as pl
from jax.experimental.pallas import tpu as pltpu

D_MODEL = 1024
HEAD_DIM = 64
HEADS_PER_MIXER = 4
GROUP_W = HEADS_PER_MIXER * HEAD_DIM
QKV_W = 12 * GROUP_W
DIFF_DIM = HEAD_DIM // 2
MOBA_BLOCK = 256
MOBA_TOPK = 3
N_BUCKETS = 32
MAX_DISTANCE = 2048
N_BIAS_HEADS = 12
N_GROUPS = 4
EXPERTS_PER_GROUP = 8
N_EXPERTS = 32
D_EXPERT = 256
RMS_EPS = 1e-6
NEG = -1e30
LOG2E = math.log2(math.e)

LANES = 128
PAIR_W = 2 * HEAD_DIM
ACC_W = 2 * PAIR_W
TQ = 256
TK = 256
MAX_SUB = 8
N_DIL_TILES = MAX_DISTANCE // TK + 1
FAR_TILE = 7
MIN_DIST = 1 - MAX_SUB
N_TAB_TILES = N_DIL_TILES + 1 - MIN_DIST
VMEM_LIMIT = 48 * 1024 * 1024

_NT = (((1,), (1,)), ((), ()))


def _t5_thresholds():
    d = np.arange(0, 4 * MAX_DISTANCE, dtype=np.int64)
    max_exact = N_BUCKETS // 2
    df = np.maximum(d, 1).astype(np.float32)
    large = max_exact + (np.log(df / np.float32(max_exact)) / np.float32(math.log(MAX_DISTANCE / max_exact))
                         * np.float32(N_BUCKETS - max_exact)).astype(np.int32)
    bucket = np.where(d < max_exact, d, np.minimum(large, N_BUCKETS - 1))
    return [int(np.argmax(bucket >= b)) for b in range(N_BUCKETS)]


_T5_THR = _t5_thresholds()
assert (FAR_TILE - 1) * TK + 1 >= _T5_THR[N_BUCKETS - 1]


def _lane_iota(shape):
    return lax.broadcasted_iota(jnp.int32, shape, len(shape) - 1)


def _bias_table_kernel(bias_ref, out_ref):
    h = pl.program_id(0)
    n = pl.program_id(1) + MIN_DIST
    i = lax.broadcasted_iota(jnp.int32, (TQ, TK), 0)
    j = lax.broadcasted_iota(jnp.int32, (TQ, TK), 1)
    d = n * TK + i - j
    is_dil = h < HEADS_PER_MIXER

    def store(val):
        mult = ((d <= 128).astype(jnp.int32)
                + jnp.logical_and((d & 3) == 0, d <= 512).astype(jnp.int32)
                + jnp.logical_and((d & 15) == 0, d <= 2048).astype(jnp.int32))
        logm = jnp.where(mult == 3, math.log(3.0), jnp.where(mult == 2, math.log(2.0), 0.0))
        val = val + jnp.where(is_dil, logm, 0.0)
        valid = jnp.logical_and(d >= 0, jnp.logical_or(mult > 0, jnp.logical_not(is_dil)))
        out_ref[...] = jnp.where(valid, val * LOG2E, NEG)

    @pl.when(n < 0)
    def _():
        out_ref[...] = jnp.full((TQ, TK), NEG, jnp.float32)

    @pl.when(n >= FAR_TILE)
    def _():
        store(jnp.full((TQ, TK), bias_ref[h, N_BUCKETS - 1], jnp.float32))

    @pl.when(jnp.logical_and(n >= 0, n < FAR_TILE))
    def _():
        val = jnp.full((TQ, TK), bias_ref[h, 0], jnp.float32)
        for b in range(1, N_BUCKETS):
            val = jnp.where(d >= _T5_THR[b], bias_ref[h, b], val)
        store(val)


def _bias_tables(rel_bias):
    bias_h = rel_bias.T.astype(jnp.float32)
    return pl.pallas_call(
        _bias_table_kernel,
        out_shape=jax.ShapeDtypeStruct((N_BIAS_HEADS, N_TAB_TILES, TQ, TK), jnp.float32),
        grid=(N_BIAS_HEADS, N_TAB_TILES),
        in_specs=[pl.BlockSpec(memory_space=pltpu.SMEM)],
        out_specs=pl.BlockSpec((None, None, TQ, TK), lambda h, n: (h, n, 0, 0)),
        name="bias_tables",
    )(bias_h)


PROJ_TM = 1024
PROJ_CW = 512
F_ROWS = 16


def _norm_proj_kernel(x_ref, g_ref, w_ref, wf_ref, bf_ref, qkv_ref, lf_ref):
    x = x_ref[...]
    ms = jnp.mean(x * x, axis=1, keepdims=True)
    h = x * lax.rsqrt(ms + RMS_EPS) * g_ref[...]
    hb = h.astype(jnp.bfloat16)
    for c in range(QKV_W // PROJ_CW):
        cols = slice(c * PROJ_CW, (c + 1) * PROJ_CW)
        qkv_ref[:, cols] = jnp.dot(hb, w_ref[:, cols], preferred_element_type=jnp.float32).astype(jnp.bfloat16)
    h_lo = (h - hb.astype(jnp.float32)).astype(jnp.bfloat16)
    wf_hi = wf_ref[0]
    wf_lo = wf_ref[1]
    z = (lax.dot_general(wf_hi, hb, _NT, preferred_element_type=jnp.float32)
         + lax.dot_general(wf_hi, h_lo, _NT, preferred_element_type=jnp.float32)
         + lax.dot_general(wf_lo, hb, _NT, preferred_element_type=jnp.float32))
    z = z + bf_ref[...]
    lf_ref[...] = jnp.minimum(z, 0.0) - jnp.log(1.0 + jnp.exp(-jnp.abs(z)))


def _norm_proj(x2d, g, w_qkv, wf, bf, batch, seq):
    n_tok = x2d.shape[0]
    tm = min(PROJ_TM, seq)
    per_b = seq // tm
    return pl.pallas_call(
        _norm_proj_kernel,
        out_shape=(jax.ShapeDtypeStruct((n_tok, QKV_W), jnp.bfloat16),
                   jax.ShapeDtypeStruct((batch * F_ROWS, seq), jnp.float32)),
        grid=(n_tok // tm,),
        in_specs=[pl.BlockSpec((tm, D_MODEL), lambda i: (i, 0)),
                  pl.BlockSpec((1, D_MODEL), lambda i: (0, 0)),
                  pl.BlockSpec((D_MODEL, QKV_W), lambda i: (0, 0)),
                  pl.BlockSpec((2, F_ROWS, D_MODEL), lambda i: (0, 0, 0)),
                  pl.BlockSpec((F_ROWS, 1), lambda i: (0, 0))],
        out_specs=(pl.BlockSpec((tm, QKV_W), lambda i: (i, 0)),
                   pl.BlockSpec((F_ROWS, tm), lambda i: (i // per_b, i % per_b))),
        compiler_params=pltpu.CompilerParams(dimension_semantics=("arbitrary",), vmem_limit_bytes=VMEM_LIMIT),
        name="norm_proj",
    )(x2d, g, w_qkv, wf, bf)


def _split3(x):
    x1 = x.astype(jnp.bfloat16)
    r1 = x - x1.astype(jnp.float32)
    x2 = r1.astype(jnp.bfloat16)
    x3 = (r1 - x2.astype(jnp.float32)).astype(jnp.bfloat16)
    return x1, x2, x3


def _cumsum_kernel(lf_ref, c_ref):
    rows, seq = lf_ref.shape
    r = lax.broadcasted_iota(jnp.int32, (LANES, LANES), 0)
    c = lax.broadcasted_iota(jnp.int32, (LANES, LANES), 1)
    upper = jnp.where(r <= c, 1.0, 0.0).astype(jnp.bfloat16)

    def body(i, carry):
        off = pl.multiple_of(i * LANES, LANES)
        x1, x2, x3 = _split3(lf_ref[:, pl.ds(off, LANES)])
        cs = (jnp.dot(x1, upper, preferred_element_type=jnp.float32)
              + jnp.dot(x2, upper, preferred_element_type=jnp.float32)
              + jnp.dot(x3, upper, preferred_element_type=jnp.float32)) + carry
        c_ref[:, pl.ds(off, LANES)] = cs * LOG2E
        return cs[:, LANES - 1:LANES]

    lax.fori_loop(0, seq // LANES, body, jnp.zeros((rows, 1), jnp.float32))


def _cumsum(lf):
    return pl.pallas_call(
        _cumsum_kernel,
        out_shape=jax.ShapeDtypeStruct(lf.shape, jnp.float32),
        name="forget_cumsum",
    )(lf)


class _Geom(NamedTuple):
    n_streams: int
    n_sub: int
    step_tiles: int = 2

    @property
    def q_rows(self):
        return self.n_sub * TQ

    @property
    def step_keys(self):
        return self.step_tiles * TK

    def rows(self, sub, st):
        rb = sub * self.n_streams + st
        return slice(rb * TQ, (rb + 1) * TQ)

    def blocks(self):
        return [(sub, st, self.rows(sub, st)) for sub in range(self.n_sub) for st in range(self.n_streams)]

    def first_tile(self, blk, m):
        return self.n_sub * blk + self.n_sub - self.step_tiles * (m + 1)

    def dist(self, m, sub, w):
        return self.step_tiles * (m + 1) - self.n_sub + sub - w

    def n_steps(self, blk):
        return (self.n_sub // self.step_tiles) * (blk + 1)

    def own_steps(self):
        return self.n_sub // self.step_tiles

    def tiles_needed(self, m, sub):
        if not isinstance(m, int):
            return self.step_tiles
        return min(max(self.dist(m, sub, 0) + 1, 0), self.step_tiles)

    def tiles_beyond(self, m, sub, max_dist):
        if max_dist is None or not isinstance(m, int):
            return 0
        return min(max(self.dist(m, sub, 0) - max_dist, 0), self.step_tiles)


def _stack_masked_q(q, qs_ref, geom):
    lane = _lane_iota((TQ, PAIR_W))
    width = PAIR_W // geom.n_streams
    for sub, st, rows in geom.blocks():
        q_tile = q[sub * TQ:(sub + 1) * TQ]
        qs_ref[rows, 0:PAIR_W] = jnp.where(lane // width == st, q_tile, jnp.zeros_like(q_tile))


def _flash_step(rows, s, v_aug, m_ref, acc_ref, first=False):
    s_max = jnp.max(s, axis=1, keepdims=True)
    m_new = jnp.broadcast_to(s_max, (s.shape[0], LANES)) if first else jnp.maximum(m_ref[rows, :], s_max)
    p = jnp.concatenate([jnp.exp2(s[:, c * LANES:(c + 1) * LANES] - m_new) for c in range(s.shape[1] // LANES)],
                        axis=1).astype(v_aug.dtype)
    pv = jnp.dot(p, v_aug, preferred_element_type=jnp.float32)
    if first:
        acc_ref[rows, :] = pv
    else:
        alpha = jnp.exp2(m_ref[rows, :] - m_new)
        acc_ref[rows, :] = jnp.concatenate([alpha, alpha], axis=1) * acc_ref[rows, :] + pv
    m_ref[rows, :] = m_new


def _normalised(acc_ref, rows):
    return acc_ref[rows, 0:PAIR_W] / acc_ref[rows, PAIR_W:ACC_W]


def _table_bias(tab_ref, hh, geom, m, sub, cap, n_tiles):
    return jnp.concatenate([tab_ref[hh, jnp.minimum(geom.dist(m, sub, w), cap) - MIN_DIST]
                            for w in n_tiles], axis=1)


def _step_rows(ref, geom, blk, m):
    k0 = pl.multiple_of(geom.first_tile(blk, m) * TK, geom.step_keys)
    return ref[pl.ds(k0, geom.step_keys), :]


def _make_attend(geom, blk, scratch, v_ref, keys, finish, max_dist=None):
    qs_ref, m_ref, acc_ref, s_ref = scratch
    blocks = geom.blocks()
    last_step = geom.n_steps(blk) - 1

    def raw(rows, rhs):
        return lax.dot_general(qs_ref[rows, :], rhs, _NT, preferred_element_type=jnp.float32)

    def v_aug(m):
        v = _step_rows(v_ref, geom, blk, m)
        return jnp.concatenate([v, jnp.ones_like(v)], axis=1)

    first_step = {sub: min(m for m in range(geom.own_steps()) if geom.tiles_needed(m, sub))
                  for sub in range(geom.n_sub)}

    def window(m, sub):
        tiles = range(geom.tiles_beyond(m, sub, max_dist), geom.tiles_needed(m, sub))
        return tiles, slice(tiles.start * TK, max(tiles.stop, tiles.start) * TK)

    def attend(rows, sub, st, scores, m, va, tiles, cols):
        first = isinstance(m, int) and m == first_step[sub]
        _flash_step(rows, finish(sub, st, scores, m, tiles), va[cols], m_ref, acc_ref, first)

    def single(m):
        rhs = keys(m)
        va = v_aug(m)
        for sub, st, rows in blocks:
            tiles, cols = window(m, sub)
            if len(tiles):
                attend(rows, sub, st, raw(rows, rhs[cols]), m, va, tiles, cols)

    def step(m, next_is_static=False, has_next=True):
        va = v_aug(m)
        rhs_next = keys(jnp.minimum(m + 1, last_step)) if has_next else None
        for sub, st, rows in blocks:
            tiles, cols = window(m, sub)
            cur = s_ref[rows, cols] if len(tiles) else None
            if has_next:
                next_tiles, next_cols = window(m + 1, sub) if next_is_static else window(None, sub)
                if len(next_tiles):
                    s_ref[rows, next_cols] = raw(rows, rhs_next[next_cols])
            if len(tiles):
                attend(rows, sub, st, cur, m, va, tiles, cols)

    def prologue(m0):
        rhs = keys(m0)
        for sub, _, rows in blocks:
            tiles, cols = window(m0, sub)
            if len(tiles):
                s_ref[rows, cols] = raw(rows, rhs[cols])

    def run(m0, count):
        peeled = [m0 + i for i in range(geom.own_steps()) if isinstance(m0, int)
                  and any(geom.tiles_needed(m0 + i, sub) < geom.step_tiles for sub in range(geom.n_sub))]
        starts_at_own = isinstance(m0, int) and m0 == 0
        assert not starts_at_own or len(peeled) == geom.own_steps()
        pl.when(count > 0)(functools.partial(prologue, m0))

        for m in peeled:
            pl.when(count > m - m0)(functools.partial(step, m, next_is_static=(m + 1) in peeled))

        def body(i, carry):
            step(m0 + i)
            return carry

        lax.fori_loop(len(peeled), count, body, 0)

    def run_static(count):
        prologue(0)
        for m in range(count):
            step(m, next_is_static=True, has_next=m + 1 < count)

    return single, run, run_static


def _attn_call(kernel, group, qkv, extra_inputs, extra_specs, geom, batch, seq, *, q_width=PAIR_W,
               extra_scratch=(), name=None):
    assert seq % geom.q_rows == 0 and seq % geom.step_keys == 0 and geom.n_sub % geom.step_tiles == 0
    nq = seq // geom.q_rows
    qc, kc, vc = 6 * group, 6 * group + 2, 6 * group + 4
    rows = geom.n_sub * geom.n_streams * TQ
    return pl.pallas_call(
        functools.partial(kernel, geom=geom),
        out_shape=jax.ShapeDtypeStruct((batch * seq, GROUP_W), jnp.bfloat16),
        grid=(2, batch, nq),
        in_specs=[pl.BlockSpec((geom.q_rows, PAIR_W), lambda p, b, i: (b * nq + i, qc + p)),
                  pl.BlockSpec((seq, PAIR_W), lambda p, b, i: (b, kc + p)),
                  pl.BlockSpec((seq, PAIR_W), lambda p, b, i: (b, vc + p))] + list(extra_specs),
        out_specs=pl.BlockSpec((geom.q_rows, PAIR_W), lambda p, b, i: (b * nq + i, p)),
        scratch_shapes=[pltpu.VMEM((rows, q_width), jnp.bfloat16),
                        pltpu.VMEM((rows, LANES), jnp.float32),
                        pltpu.VMEM((rows, ACC_W), jnp.float32),
                        pltpu.VMEM((rows, geom.step_keys), jnp.float32)
                        ] + list(extra_scratch),
        compiler_params=pltpu.CompilerParams(dimension_semantics=("arbitrary", "arbitrary", "arbitrary"),
                                             vmem_limit_bytes=VMEM_LIMIT),
        name=name,
    )(qkv, qkv, qkv, *extra_inputs)


def _tab_spec(first_head):
    return pl.BlockSpec((2, N_TAB_TILES, TQ, TK), lambda p, b, i: (first_head // 2 + p, 0, 0, 0))


def _merge_pair(o0, o1):
    return jnp.where(_lane_iota(o0.shape) < HEAD_DIM, o0, o1)


def _store_pair_outputs(o_ref, acc_ref, geom):
    for sub in range(geom.n_sub):
        o = _merge_pair(_normalised(acc_ref, geom.rows(sub, 0)), _normalised(acc_ref, geom.rows(sub, 1)))
        o_ref[sub * TQ:(sub + 1) * TQ, :] = o.astype(o_ref.dtype)


PAIR_GEOM = _Geom(n_streams=2, n_sub=4)
MOBA_GEOM = _Geom(n_streams=2, n_sub=MAX_SUB)
FOX_GEOM = _Geom(n_streams=2, n_sub=2)
DIFF_GEOM = _Geom(n_streams=4, n_sub=4)


def _dilated_kernel(q_ref, k_ref, v_ref, tab_ref, o_ref, qs_ref, m_ref, acc_ref, s_ref, *, geom):
    blk = pl.program_id(2)
    _stack_masked_q(q_ref[...], qs_ref, geom)
    _, run, run_static = _make_attend(
        geom, blk, (qs_ref, m_ref, acc_ref, s_ref), v_ref,
        keys=lambda m: _step_rows(k_ref, geom, blk, m),
        finish=lambda sub, st, raw, m, tiles: raw + _table_bias(tab_ref, st, geom, m, sub, N_DIL_TILES, tiles),
        max_dist=N_DIL_TILES - 1)
    reach = -(-(N_DIL_TILES - 1 + geom.n_sub) // geom.step_tiles)
    n_steps = geom.n_steps(blk)
    pl.when(n_steps >= reach)(functools.partial(run_static, reach))
    pl.when(n_steps < reach)(functools.partial(run, 0, n_steps))
    _store_pair_outputs(o_ref, acc_ref, geom)


def _dilated(qkv, tabs, batch, seq):
    return _attn_call(_dilated_kernel, 0, qkv, [tabs], [_tab_spec(0)], PAIR_GEOM, batch, seq, name="dilated_attn")


def _diff_kernel(q_ref, k_ref, v_ref, tab_ref, lam_ref, g_ref, o_ref, qs_ref, m_ref, acc_ref, s_ref, *,
                 lambda_init, geom):
    blk = pl.program_id(2)
    _stack_masked_q(q_ref[...], qs_ref, geom)
    _, run, _ = _make_attend(
        geom, blk, (qs_ref, m_ref, acc_ref, s_ref), v_ref,
        keys=lambda m: _step_rows(k_ref, geom, blk, m),
        finish=lambda sub, st, raw, m, tiles: raw + _table_bias(tab_ref, st // 2, geom, m, sub, FAR_TILE, tiles))
    run(0, geom.n_steps(blk))

    lamv = lam_ref[...]
    lam = (jnp.exp(jnp.sum(lamv[0:1] * lamv[1:2], axis=1, keepdims=True))
           - jnp.exp(jnp.sum(lamv[2:3] * lamv[3:4], axis=1, keepdims=True)) + lambda_init)
    first = _lane_iota((TQ, PAIR_W)) < HEAD_DIM
    for sub in range(geom.n_sub):
        part = [_normalised(acc_ref, geom.rows(sub, st)) for st in range(geom.n_streams)]
        o = _merge_pair(part[0] - lam * part[1], part[2] - lam * part[3])
        sq = o * o
        ms0 = jnp.sum(jnp.where(first, sq, 0.0), axis=1, keepdims=True)
        ms1 = jnp.sum(jnp.where(first, 0.0, sq), axis=1, keepdims=True)
        ms = jnp.where(first, ms0, ms1) * (1.0 / HEAD_DIM)
        y = (o * lax.rsqrt(ms + RMS_EPS) * g_ref[...]) * (1.0 - lambda_init)
        o_ref[sub * TQ:(sub + 1) * TQ, :] = y.astype(o_ref.dtype)


def _diff(qkv, tabs, lamv, g_sub, lambda_init, batch, seq):
    return _attn_call(functools.partial(_diff_kernel, lambda_init=lambda_init), 1, qkv, [tabs, lamv, g_sub],
                      [_tab_spec(4), pl.BlockSpec((8, LANES), lambda p, b, i: (0, 0)),
                       pl.BlockSpec((1, PAIR_W), lambda p, b, i: (0, 0))],
                      DIFF_GEOM, batch, seq, name="diff_attn")


def _moba_kernel(q_ref, k_ref, v_ref, tab_ref, o_ref, qs_ref, m_ref, acc_ref, s_ref, km_ref, *, geom):
    blk = pl.program_id(2)
    n_blk = k_ref.shape[0] // MOBA_BLOCK
    assert n_blk <= LANES and MOBA_BLOCK == TQ == TK

    @pl.when(blk == 0)
    def _():
        km_ref[...] = jnp.zeros(km_ref.shape, jnp.float32)
        for n in range(n_blk):
            km_ref[n:n + 1, :] = jnp.sum(k_ref[n * MOBA_BLOCK:(n + 1) * MOBA_BLOCK, :].astype(jnp.float32),
                                         axis=0, keepdims=True) * (1.0 / MOBA_BLOCK)

    _stack_masked_q(q_ref[...], qs_ref, geom)
    nb = -(-n_blk // 8) * 8
    km = km_ref[0:nb, :]
    km_hi = km.astype(jnp.bfloat16)
    km_lo = (km - km_hi.astype(jnp.float32)).astype(jnp.bfloat16)
    qs = qs_ref[:, 0:PAIR_W]
    gate = (lax.dot_general(km_hi, qs, _NT, preferred_element_type=jnp.float32)
            + lax.dot_general(km_lo, qs, _NT, preferred_element_type=jnp.float32))
    cand = lax.broadcasted_iota(jnp.int32, gate.shape, 0)
    cand_f = cand.astype(jnp.float32)
    own = geom.n_sub * blk + _lane_iota(gate.shape) // (geom.n_streams * TQ)
    g = jnp.where(cand < own, gate, NEG)
    sel = cand == own
    for _ in range(MOBA_TOPK):
        mx = jnp.max(g, axis=0, keepdims=True)
        first = jnp.min(jnp.where(g == mx, cand_f, 2.0 * LANES), axis=0, keepdims=True)
        pick = cand_f == first
        sel = jnp.logical_or(sel, jnp.logical_and(pick, mx > 0.5 * NEG))
        g = jnp.where(pick, -3e38, g)
    mask = jnp.where(sel, 0.0, NEG)
    if nb < LANES:
        mask = jnp.concatenate([mask, jnp.full((LANES - nb, mask.shape[1]), NEG, jnp.float32)], axis=0)
    qs_ref[:, PAIR_W:2 * PAIR_W] = mask.T.astype(jnp.bfloat16)

    def keys(m):
        k = _step_rows(k_ref, geom, blk, m)
        block_of_key = geom.first_tile(blk, m) + lax.broadcasted_iota(jnp.int32, (geom.step_keys, LANES), 0) // TK
        onehot = jnp.where(_lane_iota((geom.step_keys, LANES)) == block_of_key, 1.0, 0.0).astype(k.dtype)
        return jnp.concatenate([k, onehot], axis=1)

    _, run, _ = _make_attend(
        geom, blk, (qs_ref, m_ref, acc_ref, s_ref), v_ref, keys=keys,
        finish=lambda sub, st, raw, m, tiles: raw + _table_bias(tab_ref, st, geom, m, sub, FAR_TILE, tiles))
    run(0, geom.n_steps(blk))
    _store_pair_outputs(o_ref, acc_ref, geom)


def _moba(qkv, tabs, batch, seq):
    return _attn_call(_moba_kernel, 2, qkv, [tabs], [_tab_spec(8)], MOBA_GEOM, batch, seq, q_width=2 * PAIR_W,
                      extra_scratch=[pltpu.VMEM((LANES, PAIR_W), jnp.float32)], name="moba_attn")


FOX_SKIP_BITS = 48.0


def _head_row_norms(x, hh):
    xf = x.astype(jnp.float32)
    sq = jnp.where(_lane_iota(xf.shape) // HEAD_DIM == hh, xf * xf, 0.0)
    return jnp.sqrt(jnp.max(jnp.sum(sq, axis=1, keepdims=True), axis=0, keepdims=True))


def _fox_kernel(q_ref, k_ref, v_ref, c_ref, o_ref, qs_ref, m_ref, acc_ref, s_ref, cq_ref, stat_ref, *, geom):
    blk = pl.program_id(2)
    n_tiles = k_ref.shape[0] // TK
    assert n_tiles <= LANES

    @pl.when(blk == 0)
    def _():
        stat_ref[...] = jnp.zeros(stat_ref.shape, jnp.float32)
        for j in range(n_tiles):
            k_tile = k_ref[j * TK:(j + 1) * TK, :]
            for hh in range(2):
                stat_ref[hh:hh + 1, j:j + 1] = _head_row_norms(k_tile, hh)
                stat_ref[2 + hh:3 + hh, j:j + 1] = jnp.min(c_ref[hh:hh + 1, j * TK:(j + 1) * TK], axis=1, keepdims=True)

    q = q_ref[...]
    _stack_masked_q(q, qs_ref, geom)
    q0 = pl.multiple_of(blk * geom.q_rows, geom.q_rows)
    row = lax.broadcasted_iota(jnp.int32, (TQ, TQ), 0)
    col = lax.broadcasted_iota(jnp.int32, (TQ, TQ), 1)
    for sub, hh, rows in geom.blocks():
        c_row = c_ref[hh:hh + 1, pl.ds(q0 + sub * TQ, TQ)]
        c_col = jnp.sum(jnp.where(row == col, c_row, 0.0), axis=1, keepdims=True)
        cq_ref[rows, :] = jnp.broadcast_to(c_col, (TQ, LANES))

    def finish(sub, hh, raw, m, tiles):
        assert tiles.start == 0
        n_keys = len(tiles) * TK
        k0 = pl.multiple_of(geom.first_tile(blk, m) * TK, TK)
        cq = cq_ref[geom.rows(sub, hh), :]
        s = raw + (jnp.concatenate([cq] * (n_keys // LANES), axis=1) - c_ref[hh:hh + 1, pl.ds(k0, n_keys)])
        if isinstance(m, int) and m < geom.own_steps():
            key_pos = lax.broadcasted_iota(jnp.int32, (TQ, n_keys), 1)
            query_pos = lax.broadcasted_iota(jnp.int32, (TQ, n_keys), 0)
            first_key = (geom.n_sub - geom.step_tiles * (m + 1)) * TK
            s = jnp.where(key_pos + first_key <= query_pos + sub * TQ, s, NEG)
        return s

    single, run, _ = _make_attend(geom, blk, (qs_ref, m_ref, acc_ref, s_ref), v_ref,
                               keys=lambda m: _step_rows(k_ref, geom, blk, m), finish=finish)
    for m in range(geom.own_steps()):
        single(m)

    tile = _lane_iota((1, LANES))
    needed = jnp.zeros((1, LANES), jnp.bool_)
    for hh in range(2):
        rows_h = [geom.rows(sub, hh) for sub in range(geom.n_sub)]
        m_min = functools.reduce(jnp.minimum, [jnp.min(m_ref[r, :], axis=0, keepdims=True) for r in rows_h])
        cq_max = functools.reduce(jnp.maximum, [jnp.max(cq_ref[r, :], axis=0, keepdims=True) for r in rows_h])
        bound = _head_row_norms(q, hh) * stat_ref[hh:hh + 1, :] + cq_max - stat_ref[2 + hh:3 + hh, :] + 1.0
        needed = jnp.logical_or(needed, bound >= m_min - FOX_SKIP_BITS)
    n_before = geom.n_sub * blk
    first_needed = jnp.min(jnp.where(jnp.logical_and(needed, tile < n_before), tile, n_before))
    run(geom.own_steps(), (n_before - first_needed + geom.step_tiles - 1) // geom.step_tiles)
    _store_pair_outputs(o_ref, acc_ref, geom)


def _fox(qkv, c, batch, seq):
    return _attn_call(_fox_kernel, 3, qkv, [c.reshape(batch, 2, 8, seq)],
                      [pl.BlockSpec((None, None, 8, seq), lambda p, b, i: (b, p, 0, 0))],
                      FOX_GEOM, batch, seq,
                      extra_scratch=[pltpu.VMEM((FOX_GEOM.n_sub * FOX_GEOM.n_streams * TQ, LANES), jnp.float32),
                                     pltpu.VMEM((8, LANES), jnp.float32)], name="fox_attn")


OUT_TM = 1024
ROUTER_W = LANES
ROUTE_ROWS = 8


PACK_W = D_MODEL // 2


def _pack_bf16_pair(lo, hi):
    lo_bits = lax.bitcast_convert_type(lo.astype(jnp.bfloat16).astype(jnp.float32), jnp.uint32)
    hi_bits = lax.bitcast_convert_type(hi.astype(jnp.bfloat16).astype(jnp.float32), jnp.uint32)
    word = lax.shift_right_logical(lo_bits, jnp.uint32(16)) | (hi_bits & jnp.uint32(0xFFFF0000))
    return lax.bitcast_convert_type(word, jnp.int32)


def _unpack_bf16_pair(word):
    bits = lax.bitcast_convert_type(word, jnp.uint32)
    lo = lax.bitcast_convert_type(lax.shift_left(bits, jnp.uint32(16)), jnp.float32)
    hi = lax.bitcast_convert_type(bits & jnp.uint32(0xFFFF0000), jnp.float32)
    return lo, hi


def _outproj_router_kernel(oa_ref, ob_ref, oc_ref, od_ref, wo_ref, x_ref, g_ref, wr_ref, br_ref,
                           x1_ref, h2_ref, route_ref, cw_ref):
    acc = x_ref[...]
    for m, o_ref in enumerate((oa_ref, ob_ref, oc_ref, od_ref)):
        acc = acc + jnp.dot(o_ref[...], wo_ref[m * GROUP_W:(m + 1) * GROUP_W, :], preferred_element_type=jnp.float32)
    x1_ref[...] = acc
    ms = jnp.mean(acc * acc, axis=1, keepdims=True)
    h = acc * lax.rsqrt(ms + RMS_EPS) * g_ref[...]
    hb = h.astype(jnp.bfloat16)
    h2_ref[...] = _pack_bf16_pair(h[:, 0:PACK_W], h[:, PACK_W:D_MODEL])
    h_lo = (h - hb.astype(jnp.float32)).astype(jnp.bfloat16)
    logits = (jnp.dot(hb, wr_ref[0], preferred_element_type=jnp.float32)
              + jnp.dot(h_lo, wr_ref[0], preferred_element_type=jnp.float32)
              + jnp.dot(hb, wr_ref[1], preferred_element_type=jnp.float32)) + br_ref[...]
    lane = _lane_iota(logits.shape).astype(jnp.float32)
    big = 2.0 * LANES
    gmask = jnp.logical_and(lane >= N_EXPERTS, lane < N_EXPERTS + N_GROUPS)
    gl = jnp.where(gmask, logits, NEG)
    gmax = jnp.max(gl, axis=1, keepdims=True)
    glane = jnp.min(jnp.where(gl == gmax, lane, big), axis=1, keepdims=True)
    gsum = jnp.sum(jnp.where(gmask, jnp.exp(gl - gmax), 0.0), axis=1, keepdims=True)
    g_w = 1.0 / gsum
    e0 = (glane - N_EXPERTS) * EXPERTS_PER_GROUP
    emask = jnp.logical_and(lane >= e0, lane < e0 + EXPERTS_PER_GROUP)
    el = jnp.where(emask, logits, NEG)
    v1 = jnp.max(el, axis=1, keepdims=True)
    i1 = jnp.min(jnp.where(el == v1, lane, big), axis=1, keepdims=True)
    el2 = jnp.where(lane == i1, NEG, el)
    v2 = jnp.max(el2, axis=1, keepdims=True)
    i2 = jnp.min(jnp.where(el2 == v2, lane, big), axis=1, keepdims=True)
    e2 = jnp.exp(v2 - v1)
    den = 1.0 + e2
    c1 = g_w / den
    c2 = g_w * e2 / den
    quad = jnp.where(lane == 0.0, i1, jnp.where(lane == 1.0, i2, jnp.where(lane == 2.0, c1,
                                                                             jnp.where(lane == 3.0, c2, 0.0))))
    cw_ref[...] = jnp.where(lane == 0.0, c1, jnp.where(lane == 1.0, c2, 0.0))
    route_ref[...] = quad.T[0:ROUTE_ROWS, :]


def _outproj_router(o_parts, w_out, x2d, g2, wr, br):
    n_tok = x2d.shape[0]
    tm = min(OUT_TM, n_tok)
    row = lambda i: (i, 0)
    fixed = lambda i: (0, 0)
    return pl.pallas_call(
        _outproj_router_kernel,
        out_shape=(jax.ShapeDtypeStruct((n_tok, D_MODEL), jnp.float32),
                   jax.ShapeDtypeStruct((n_tok, PACK_W), jnp.int32),
                   jax.ShapeDtypeStruct((ROUTE_ROWS, n_tok), jnp.float32),
                   jax.ShapeDtypeStruct((n_tok, ROUTER_W), jnp.float32)),
        grid=(n_tok // tm,),
        in_specs=[pl.BlockSpec((tm, GROUP_W), row)] * 4 + [
            pl.BlockSpec((D_MODEL, D_MODEL), fixed),
            pl.BlockSpec((tm, D_MODEL), row),
            pl.BlockSpec((1, D_MODEL), fixed),
            pl.BlockSpec((2, D_MODEL, ROUTER_W), lambda i: (0, 0, 0)),
            pl.BlockSpec((1, ROUTER_W), fixed)],
        out_specs=(pl.BlockSpec((tm, D_MODEL), row), pl.BlockSpec((tm, PACK_W), row),
                   pl.BlockSpec((ROUTE_ROWS, tm), lambda i: (0, i)), pl.BlockSpec((tm, ROUTER_W), row)),
        compiler_params=pltpu.CompilerParams(dimension_semantics=("arbitrary",), vmem_limit_bytes=VMEM_LIMIT),
        name="outproj_router",
    )(*o_parts, w_out, x2d, g2, wr, br)


ROW_TILE = 512
DISP_TM = 1024
SC_CORES = 2
SC_SUBCORES = 16
SC_WORKERS = SC_CORES * SC_SUBCORES
SC_CHUNK = 64


def _dispatch_kernel(route_ref, rank_ref, cnt_ref, tri_ref, carry_ref):
    step = pl.program_id(0)
    tm = route_ref.shape[1]

    @pl.when(step == 0)
    def _():
        r = lax.broadcasted_iota(jnp.int32, (tm, tm), 0)
        c = lax.broadcasted_iota(jnp.int32, (tm, tm), 1)
        tri_ref[...] = jnp.where(r <= c, 1.0, 0.0).astype(jnp.bfloat16)
        carry_ref[...] = jnp.zeros(carry_ref.shape, jnp.float32)

    e1 = route_ref[0:1, :]
    e2 = route_ref[1:2, :]
    expert = lax.broadcasted_iota(jnp.int32, (N_EXPERTS, tm), 0).astype(jnp.float32)
    hit1 = e1 == expert
    hit2 = e2 == expert
    hits = jnp.where(jnp.logical_or(hit1, hit2), 1.0, 0.0).astype(jnp.bfloat16)
    cum = jnp.dot(hits, tri_ref[...], preferred_element_type=jnp.float32) + carry_ref[...]
    rank1 = jnp.sum(jnp.where(hit1, cum - 1.0, 0.0), axis=0, keepdims=True)
    rank2 = jnp.sum(jnp.where(hit2, cum - 1.0, 0.0), axis=0, keepdims=True)
    row = lax.broadcasted_iota(jnp.int32, (ROUTE_ROWS, tm), 0)
    rank_ref[...] = jnp.where(row == 0, rank1, jnp.where(row == 1, rank2, 0.0))
    total = cum[:, tm - 1:tm]
    carry_ref[...] = total
    cnt_ref[...] = jnp.broadcast_to(total, cnt_ref.shape)


def _dispatch(route):
    n_tok = route.shape[1]
    tm = min(DISP_TM, n_tok)
    return pl.pallas_call(
        _dispatch_kernel,
        out_shape=(jax.ShapeDtypeStruct((ROUTE_ROWS, n_tok), jnp.float32),
                   jax.ShapeDtypeStruct((N_EXPERTS, LANES), jnp.float32)),
        grid=(n_tok // tm,),
        in_specs=[pl.BlockSpec((ROUTE_ROWS, tm), lambda i: (0, i))],
        out_specs=(pl.BlockSpec((ROUTE_ROWS, tm), lambda i: (0, i)),
                   pl.BlockSpec((N_EXPERTS, LANES), lambda i: (0, 0))),
        scratch_shapes=[pltpu.VMEM((tm, tm), jnp.bfloat16), pltpu.VMEM((N_EXPERTS, 1), jnp.float32)],
        compiler_params=pltpu.CompilerParams(dimension_semantics=("arbitrary",)),
        name="moe_ranks",
    )(route)


def _positions_kernel(route_ref, rank_ref, off_ref, pos_ref):
    tm = route_ref.shape[1]
    expert = lax.broadcasted_iota(jnp.int32, (N_EXPERTS, tm), 0).astype(jnp.float32)
    off = off_ref[...]
    start1 = jnp.sum(jnp.where(route_ref[0:1, :] == expert, off, 0.0), axis=0, keepdims=True)
    start2 = jnp.sum(jnp.where(route_ref[1:2, :] == expert, off, 0.0), axis=0, keepdims=True)
    row = lax.broadcasted_iota(jnp.int32, (ROUTE_ROWS, tm), 0)
    pos_ref[...] = jnp.where(row == 0, start1 + rank_ref[0:1, :], jnp.where(row == 1, start2 + rank_ref[1:2, :], 0.0))


def _positions(route, rank, seg_start):
    n_tok = route.shape[1]
    tm = min(DISP_TM, n_tok)
    blk = pl.BlockSpec((ROUTE_ROWS, tm), lambda i: (0, i))
    return pl.pallas_call(
        _positions_kernel,
        out_shape=jax.ShapeDtypeStruct((ROUTE_ROWS, n_tok), jnp.float32),
        grid=(n_tok // tm,),
        in_specs=[blk, blk, pl.BlockSpec((N_EXPERTS, 1), lambda i: (0, 0))],
        out_specs=blk,
        name="moe_positions",
    )(route, rank, seg_start)


def _sc_mesh():
    from jax.experimental.pallas import tpu_sc as plsc
    return plsc.VectorSubcoreMesh(core_axis_name="c", subcore_axis_name="s")


def _sc_worker_id():
    return lax.axis_index("s") * SC_CORES + lax.axis_index("c")


def _sc_scatter_rows(rows, pos, n_out):
    n_tok, width = rows.shape
    per_worker = n_tok // SC_WORKERS
    n_pairs = per_worker // (2 * SC_CHUNK)
    assert per_worker % (2 * SC_CHUNK) == 0
    buf = [pltpu.VMEM((SC_CHUNK,), jnp.int32), pltpu.VMEM((SC_CHUNK,), jnp.int32),
           pltpu.VMEM((SC_CHUNK, width), rows.dtype), pltpu.SemaphoreType.DMA, pltpu.SemaphoreType.DMA]

    @functools.partial(pl.kernel, mesh=_sc_mesh(), out_type=jax.ShapeDtypeStruct((n_out, width), rows.dtype),
                       scratch_types=buf + buf)
    def scatter(rows_hbm, pos_hbm, out_hbm, *scratch):
        base = _sc_worker_id() * per_worker
        buf_a, buf_b = scratch[:5], scratch[5:]

        def load(chunk, p1_v, p2_v, rows_v, sem_in, sem_out):
            off = base + chunk * SC_CHUNK
            pltpu.sync_copy(pos_hbm.at[pl.ds(off, SC_CHUNK)], p1_v)
            pltpu.sync_copy(pos_hbm.at[pl.ds(n_tok + off, SC_CHUNK)], p2_v)
            pltpu.async_copy(rows_hbm.at[pl.ds(off, SC_CHUNK)], rows_v, sem_in)

        def flush(chunk, p1_v, p2_v, rows_v, sem_in, sem_out):
            off = base + chunk * SC_CHUNK
            pltpu.make_async_copy(rows_hbm.at[pl.ds(off, SC_CHUNK)], rows_v, sem_in).wait()
            first = pltpu.async_copy(rows_v, out_hbm.at[p1_v], sem_out)
            second = pltpu.async_copy(rows_v, out_hbm.at[p2_v], sem_out)
            first.wait()
            second.wait()

        load(0, *buf_a)
        def body(j, carry):
            load(2 * j + 1, *buf_b)
            flush(2 * j, *buf_a)

            @pl.when(j + 1 < n_pairs)
            def _():
                load(2 * j + 2, *buf_a)

            flush(2 * j + 1, *buf_b)
            return carry

        lax.fori_loop(0, n_pairs, body, 0)

    return scatter(rows, pos)


def _sc_gather_rows(table, idx):
    n_idx = idx.shape[0]
    width = table.shape[1]
    per_worker = n_idx // SC_WORKERS
    n_pairs = per_worker // (2 * SC_CHUNK)
    assert per_worker % (2 * SC_CHUNK) == 0
    buf = [pltpu.VMEM((SC_CHUNK,), jnp.int32), pltpu.VMEM((SC_CHUNK, width), table.dtype), pltpu.SemaphoreType.DMA]

    @functools.partial(pl.kernel, mesh=_sc_mesh(), out_type=jax.ShapeDtypeStruct((n_idx, width), table.dtype),
                       scratch_types=buf + buf)
    def gather(table_hbm, idx_hbm, out_hbm, *scratch):
        base = _sc_worker_id() * per_worker
        buf_a, buf_b = scratch[:3], scratch[3:]

        def fetch(chunk, idx_v, rows_v, sem):
            pltpu.sync_copy(idx_hbm.at[pl.ds(base + chunk * SC_CHUNK, SC_CHUNK)], idx_v)
            pltpu.async_copy(table_hbm.at[idx_v], rows_v, sem)

        def drain(chunk, idx_v, rows_v, sem):
            pltpu.make_async_copy(table_hbm.at[idx_v], rows_v, sem).wait()
            pltpu.sync_copy(rows_v, out_hbm.at[pl.ds(base + chunk * SC_CHUNK, SC_CHUNK)])

        fetch(0, *buf_a)
        def body(j, carry):
            fetch(2 * j + 1, *buf_b)
            drain(2 * j, *buf_a)

            @pl.when(j + 1 < n_pairs)
            def _():
                fetch(2 * j + 2, *buf_a)

            drain(2 * j + 1, *buf_b)
            return carry

        lax.fori_loop(0, n_pairs, body, 0)

    return gather(table, idx)


def _experts_kernel(tile_expert_ref, n_used_ref, hs_ref, wg_ref, wu_ref, wd_ref, ys_ref):
    del tile_expert_ref

    @pl.when(pl.program_id(0) < n_used_ref[0])
    def _():
        lo, hi = _unpack_bf16_pair(hs_ref[...])
        lo = lo.astype(jnp.bfloat16)
        hi = hi.astype(jnp.bfloat16)
        bf = jnp.bfloat16
        gate = (jnp.dot(lo, wg_ref[0:PACK_W, :].astype(bf), preferred_element_type=jnp.float32)
                + jnp.dot(hi, wg_ref[PACK_W:D_MODEL, :].astype(bf), preferred_element_type=jnp.float32))
        up = (jnp.dot(lo, wu_ref[0:PACK_W, :].astype(bf), preferred_element_type=jnp.float32)
              + jnp.dot(hi, wu_ref[PACK_W:D_MODEL, :].astype(bf), preferred_element_type=jnp.float32))
        hid = ((gate * jax.nn.sigmoid(gate)) * up).astype(bf)
        y = jnp.dot(hid, wd_ref[...].astype(bf), preferred_element_type=jnp.float32)
        ys_ref[...] = _pack_bf16_pair(y[:, 0:PACK_W], y[:, PACK_W:D_MODEL])


def _experts(hs, tile_expert, n_used, wg, wu, wd, layer):
    n_rows = hs.shape[0]
    weights = lambda i, te, nu: (layer, te[i], 0, 0)
    return pl.pallas_call(
        _experts_kernel,
        out_shape=jax.ShapeDtypeStruct((n_rows, PACK_W), jnp.int32),
        grid_spec=pltpu.PrefetchScalarGridSpec(
            num_scalar_prefetch=2,
            grid=(n_rows // ROW_TILE,),
            in_specs=[pl.BlockSpec((ROW_TILE, PACK_W), lambda i, te, nu: (i, 0)),
                      pl.BlockSpec((None, None, D_MODEL, D_EXPERT), weights),
                      pl.BlockSpec((None, None, D_MODEL, D_EXPERT), weights),
                      pl.BlockSpec((None, None, D_EXPERT, D_MODEL), weights)],
            out_specs=pl.BlockSpec((ROW_TILE, PACK_W), lambda i, te, nu: (i, 0))),
        compiler_params=pltpu.CompilerParams(dimension_semantics=("arbitrary",), vmem_limit_bytes=VMEM_LIMIT),
        name="moe_experts",
    )(tile_expert, n_used, hs, wg, wu, wd)


COMB_TM = 1024


def _combine_kernel(x1_ref, y1_ref, y2_ref, cw_ref, gf_ref, out_ref, *, final_norm):
    c1 = cw_ref[:, 0:1]
    c2 = cw_ref[:, 1:2]
    lo1, hi1 = _unpack_bf16_pair(y1_ref[...])
    lo2, hi2 = _unpack_bf16_pair(y2_ref[...])
    y = jnp.concatenate([x1_ref[:, 0:PACK_W] + (c1 * lo1 + c2 * lo2),
                         x1_ref[:, PACK_W:D_MODEL] + (c1 * hi1 + c2 * hi2)], axis=1)
    if final_norm:
        ms = jnp.mean(y * y, axis=1, keepdims=True)
        y = y * lax.rsqrt(ms + RMS_EPS) * gf_ref[...]
    out_ref[...] = y


def _combine(x1, y12, cw, gf, final_norm):
    n_tok = x1.shape[0]
    tm = min(COMB_TM, n_tok)
    n_blk = n_tok // tm
    return pl.pallas_call(
        functools.partial(_combine_kernel, final_norm=final_norm),
        out_shape=jax.ShapeDtypeStruct((n_tok, D_MODEL), jnp.float32),
        grid=(n_blk,),
        in_specs=[pl.BlockSpec((tm, D_MODEL), lambda i: (i, 0)),
                  pl.BlockSpec((tm, PACK_W), lambda i: (i, 0)),
                  pl.BlockSpec((tm, PACK_W), lambda i: (n_blk + i, 0)),
                  pl.BlockSpec((tm, ROUTER_W), lambda i: (i, 0)),
                  pl.BlockSpec((1, D_MODEL), lambda i: (0, 0))],
        out_specs=pl.BlockSpec((tm, D_MODEL), lambda i: (i, 0)),
        compiler_params=pltpu.CompilerParams(dimension_semantics=("arbitrary",), vmem_limit_bytes=VMEM_LIMIT),
        name="moe_combine",
    )(x1, y12, y12, cw, gf)


def _sparse_moe(x1, h2p, route, cw, w_gate, w_up, w_down, layer, gf, final_norm):
    n_tok = x1.shape[0]
    n_rows = 2 * n_tok + N_EXPERTS * ROW_TILE
    rank, counts = _dispatch(route)
    padded = (counts[:, 0].astype(jnp.int32) + ROW_TILE - 1) // ROW_TILE * ROW_TILE
    ends = jnp.cumsum(padded)
    starts = ends - padded
    tile_start = jnp.arange(n_rows // ROW_TILE, dtype=jnp.int32) * ROW_TILE
    tile_expert = jnp.minimum(jnp.sum(tile_start[:, None] >= ends[None, :], axis=1), N_EXPERTS - 1).astype(jnp.int32)
    n_used = (ends[N_EXPERTS - 1:] // ROW_TILE).astype(jnp.int32)

    pos = _positions(route, rank, starts.astype(jnp.float32).reshape(N_EXPERTS, 1))
    pos12 = jnp.concatenate([pos[0], pos[1]]).astype(jnp.int32)
    hs = _sc_scatter_rows(h2p, pos12, n_rows)
    ys = _experts(hs, tile_expert, n_used, w_gate, w_up, w_down, layer)
    y12 = _sc_gather_rows(ys, pos12)
    return _combine(x1, y12, cw, gf, final_norm)


def _split_bf16(w):
    hi = w.astype(jnp.bfloat16)
    lo = (w - hi.astype(jnp.float32)).astype(jnp.bfloat16)
    return jnp.stack([hi, lo])


def _qkv_col_scale():
    s = np.ones((QKV_W,), np.float32)
    for group, dim in enumerate((HEAD_DIM, DIFF_DIM, HEAD_DIM, HEAD_DIM)):
        s[3 * group * GROUP_W:(3 * group + 1) * GROUP_W] = LOG2E * dim ** -0.5
    return s


def _forget_weights(w_f, b_f):
    rows = jnp.zeros((F_ROWS, D_MODEL), jnp.float32)
    bias = jnp.zeros((F_ROWS, 1), jnp.float32)
    for h in range(HEADS_PER_MIXER):
        r = (h // 2) * 8 + (h % 2)
        rows = rows.at[r].set(w_f[:, h])
        bias = bias.at[r, 0].set(b_f[h])
    return _split_bf16(rows), bias


def kernel(x, rel_bias, ln1, w_in, w_out, lam_q1, lam_k1, lam_q2, lam_k2, subln_g, b_forget,
           ln2, w_group, b_group, w_expert, b_expert, w_gate, w_up, w_down, ln_f):
    batch, seq, _ = x.shape
    depth = ln1.shape[0]
    tabs = _bias_tables(rel_bias)
    xt = x.reshape(batch * seq, D_MODEL)
    gf = ln_f.reshape(1, D_MODEL)
    col_scale = _qkv_col_scale()
    for l in range(depth):
        lambda_init = 0.8 - 0.6 * math.exp(-0.3 * l)
        wf, bf = _forget_weights(w_in[l][:, QKV_W:], b_forget[l])
        w_qkv = (w_in[l][:, :QKV_W] * col_scale).astype(jnp.bfloat16)
        qkv, lf = _norm_proj(xt, ln1[l].reshape(1, D_MODEL), w_qkv, wf, bf, batch, seq)
        c = _cumsum(lf)

        o_a = _dilated(qkv, tabs, batch, seq)
        lamv = jnp.zeros((8, LANES), jnp.float32)
        lamv = lamv.at[0:4, 0:DIFF_DIM].set(jnp.stack([lam_q1[l], lam_k1[l], lam_q2[l], lam_k2[l]]))
        g_sub = jnp.tile(subln_g[l], 2).reshape(1, PAIR_W)
        o_b = _diff(qkv, tabs, lamv, g_sub, lambda_init, batch, seq)
        o_c = _moba(qkv, tabs, batch, seq)
        o_d = _fox(qkv, c, batch, seq)

        w_router = jnp.zeros((D_MODEL, ROUTER_W), jnp.float32)
        w_router = w_router.at[:, :N_EXPERTS].set(w_expert[l]).at[:, N_EXPERTS:N_EXPERTS + N_GROUPS].set(w_group[l])
        b_router = jnp.zeros((1, ROUTER_W), jnp.float32)
        b_router = b_router.at[0, :N_EXPERTS].set(b_expert[l].reshape(-1)).at[0, N_EXPERTS:N_EXPERTS + N_GROUPS].set(b_group[l])
        x1, h2p, route, cw = _outproj_router([o_a, o_b, o_c, o_d], w_out[l].astype(jnp.bfloat16), xt,
                                             ln2[l].reshape(1, D_MODEL), _split_bf16(w_router), b_router)
        xt = _sparse_moe(x1, h2p, route, cw, w_gate, w_up, w_down, l, gf, final_norm=(l == depth - 1))
    return xt.reshape(batch, seq, D_MODEL)
```

```python
import functools
import math
from typing import NamedTuple

import jax
import jax.numpy as jnp
import numpy as np
from jax import lax
from jax.experimental import pallas as pl
from jax.experimental.pallas import tpu as pltpu

D_MODEL = 1024
HEAD_DIM = 64
HEADS_PER_MIXER = 4
GROUP_W = HEADS_PER_MIXER * HEAD_DIM
QKV_W = 12 * GROUP_W
DIFF_DIM = HEAD_DIM // 2
MOBA_BLOCK = 256
MOBA_TOPK = 3
N_BUCKETS = 32
MAX_DISTANCE = 2048
N_BIAS_HEADS = 12
N_GROUPS = 4
EXPERTS_PER_GROUP = 8
N_EXPERTS = 32
D_EXPERT = 256
RMS_EPS = 1e-6
NEG = -1e30
LOG2E = math.log2(math.e)

LANES = 128
PAIR_W = 2 * HEAD_DIM
ACC_W = 2 * PAIR_W
TQ = 256
TK = 256
MAX_SUB = 8
N_DIL_TILES = MAX_DISTANCE // TK + 1
FAR_TILE = 7
MIN_DIST = 1 - MAX_SUB
N_TAB_TILES = N_DIL_TILES + 1 - MIN_DIST
VMEM_LIMIT = 48 * 1024 * 1024

_NT = (((1,), (1,)), ((), ()))


def _t5_thresholds():
    d = np.arange(0, 4 * MAX_DISTANCE, dtype=np.int64)
    max_exact = N_BUCKETS // 2
    df = np.maximum(d, 1).astype(np.float32)
    large = max_exact + (np.log(df / np.float32(max_exact)) / np.float32(math.log(MAX_DISTANCE / max_exact))
                         * np.float32(N_BUCKETS - max_exact)).astype(np.int32)
    bucket = np.where(d < max_exact, d, np.minimum(large, N_BUCKETS - 1))
    return [int(np.argmax(bucket >= b)) for b in range(N_BUCKETS)]


_T5_THR = _t5_thresholds()
assert (FAR_TILE - 1) * TK + 1 >= _T5_THR[N_BUCKETS - 1]


def _lane_iota(shape):
    return lax.broadcasted_iota(jnp.int32, shape, len(shape) - 1)


def _bias_table_kernel(bias_ref, out_ref):
    h = pl.program_id(0)
    n = pl.program_id(1) + MIN_DIST
    i = lax.broadcasted_iota(jnp.int32, (TQ, TK), 0)
    j = lax.broadcasted_iota(jnp.int32, (TQ, TK), 1)
    d = n * TK + i - j
    is_dil = h < HEADS_PER_MIXER

    def store(val):
        mult = ((d <= 128).astype(jnp.int32)
                + jnp.logical_and((d & 3) == 0, d <= 512).astype(jnp.int32)
                + jnp.logical_and((d & 15) == 0, d <= 2048).astype(jnp.int32))
        logm = jnp.where(mult == 3, math.log(3.0), jnp.where(mult == 2, math.log(2.0), 0.0))
        val = val + jnp.where(is_dil, logm, 0.0)
        valid = jnp.logical_and(d >= 0, jnp.logical_or(mult > 0, jnp.logical_not(is_dil)))
        out_ref[...] = jnp.where(valid, val * LOG2E, NEG)

    @pl.when(n < 0)
    def _():
        out_ref[...] = jnp.full((TQ, TK), NEG, jnp.float32)

    @pl.when(n >= FAR_TILE)
    def _():
        store(jnp.full((TQ, TK), bias_ref[h, N_BUCKETS - 1], jnp.float32))

    @pl.when(jnp.logical_and(n >= 0, n < FAR_TILE))
    def _():
        val = jnp.full((TQ, TK), bias_ref[h, 0], jnp.float32)
        for b in range(1, N_BUCKETS):
            val = jnp.where(d >= _T5_THR[b], bias_ref[h, b], val)
        store(val)


def _bias_tables(rel_bias):
    bias_h = rel_bias.T.astype(jnp.float32)
    return pl.pallas_call(
        _bias_table_kernel,
        out_shape=jax.ShapeDtypeStruct((N_BIAS_HEADS, N_TAB_TILES, TQ, TK), jnp.float32),
        grid=(N_BIAS_HEADS, N_TAB_TILES),
        in_specs=[pl.BlockSpec(memory_space=pltpu.SMEM)],
        out_specs=pl.BlockSpec((None, None, TQ, TK), lambda h, n: (h, n, 0, 0)),
        name="bias_tables",
    )(bias_h)


PROJ_TM = 1024
PROJ_CW = 512
F_ROWS = 16


def _norm_proj_kernel(x_ref, g_ref, w_ref, wf_ref, bf_ref, qkv_ref, lf_ref):
    x = x_ref[...]
    ms = jnp.mean(x * x, axis=1, keepdims=True)
    h = x * lax.rsqrt(ms + RMS_EPS) * g_ref[...]
    hb = h.astype(jnp.bfloat16)
    for c in range(QKV_W // PROJ_CW):
        cols = slice(c * PROJ_CW, (c + 1) * PROJ_CW)
        qkv_ref[:, cols] = jnp.dot(hb, w_ref[:, cols], preferred_element_type=jnp.float32).astype(jnp.bfloat16)
    h_lo = (h - hb.astype(jnp.float32)).astype(jnp.bfloat16)
    wf_hi = wf_ref[0]
    wf_lo = wf_ref[1]
    z = (lax.dot_general(wf_hi, hb, _NT, preferred_element_type=jnp.float32)
         + lax.dot_general(wf_hi, h_lo, _NT, preferred_element_type=jnp.float32)
         + lax.dot_general(wf_lo, hb, _NT, preferred_element_type=jnp.float32))
    z = z + bf_ref[...]
    lf_ref[...] = jnp.minimum(z, 0.0) - jnp.log(1.0 + jnp.exp(-jnp.abs(z)))


def _norm_proj(x2d, g, w_qkv, wf, bf, batch, seq):
    n_tok = x2d.shape[0]
    tm = min(PROJ_TM, seq)
    per_b = seq // tm
    return pl.pallas_call(
        _norm_proj_kernel,
        out_shape=(jax.ShapeDtypeStruct((n_tok, QKV_W), jnp.bfloat16),
                   jax.ShapeDtypeStruct((batch * F_ROWS, seq), jnp.float32)),
        grid=(n_tok // tm,),
        in_specs=[pl.BlockSpec((tm, D_MODEL), lambda i: (i, 0)),
                  pl.BlockSpec((1, D_MODEL), lambda i: (0, 0)),
                  pl.BlockSpec((D_MODEL, QKV_W), lambda i: (0, 0)),
                  pl.BlockSpec((2, F_ROWS, D_MODEL), lambda i: (0, 0, 0)),
                  pl.BlockSpec((F_ROWS, 1), lambda i: (0, 0))],
        out_specs=(pl.BlockSpec((tm, QKV_W), lambda i: (i, 0)),
                   pl.BlockSpec((F_ROWS, tm), lambda i: (i // per_b, i % per_b))),
        compiler_params=pltpu.CompilerParams(dimension_semantics=("arbitrary",), vmem_limit_bytes=VMEM_LIMIT),
        name="norm_proj",
    )(x2d, g, w_qkv, wf, bf)


def _split3(x):
    x1 = x.astype(jnp.bfloat16)
    r1 = x - x1.astype(jnp.float32)
    x2 = r1.astype(jnp.bfloat16)
    x3 = (r1 - x2.astype(jnp.float32)).astype(jnp.bfloat16)
    return x1, x2, x3


def _cumsum_kernel(lf_ref, c_ref):
    rows, seq = lf_ref.shape
    r = lax.broadcasted_iota(jnp.int32, (LANES, LANES), 0)
    c = lax.broadcasted_iota(jnp.int32, (LANES, LANES), 1)
    upper = jnp.where(r <= c, 1.0, 0.0).astype(jnp.bfloat16)

    def body(i, carry):
        off = pl.multiple_of(i * LANES, LANES)
        x1, x2, x3 = _split3(lf_ref[:, pl.ds(off, LANES)])
        cs = (jnp.dot(x1, upper, preferred_element_type=jnp.float32)
              + jnp.dot(x2, upper, preferred_element_type=jnp.float32)
              + jnp.dot(x3, upper, preferred_element_type=jnp.float32)) + carry
        c_ref[:, pl.ds(off, LANES)] = cs * LOG2E
        return cs[:, LANES - 1:LANES]

    lax.fori_loop(0, seq // LANES, body, jnp.zeros((rows, 1), jnp.float32))


def _cumsum(lf):
    return pl.pallas_call(
        _cumsum_kernel,
        out_shape=jax.ShapeDtypeStruct(lf.shape, jnp.float32),
        name="forget_cumsum",
    )(lf)


class _Geom(NamedTuple):
    n_streams: int
    n_sub: int
    step_tiles: int = 2

    @property
    def q_rows(self):
        return self.n_sub * TQ

    @property
    def step_keys(self):
        return self.step_tiles * TK

    def rows(self, sub, st):
        rb = sub * self.n_streams + st
        return slice(rb * TQ, (rb + 1) * TQ)

    def blocks(self):
        return [(sub, st, self.rows(sub, st)) for sub in range(self.n_sub) for st in range(self.n_streams)]

    def first_tile(self, blk, m):
        return self.n_sub * blk + self.n_sub - self.step_tiles * (m + 1)

    def dist(self, m, sub, w):
        return self.step_tiles * (m + 1) - self.n_sub + sub - w

    def n_steps(self, blk):
        return (self.n_sub // self.step_tiles) * (blk + 1)

    def own_steps(self):
        return self.n_sub // self.step_tiles

    def tiles_needed(self, m, sub):
        if not isinstance(m, int):
            return self.step_tiles
        return min(max(self.dist(m, sub, 0) + 1, 0), self.step_tiles)

    def tiles_beyond(self, m, sub, max_dist):
        if max_dist is None or not isinstance(m, int):
            return 0
        return min(max(self.dist(m, sub, 0) - max_dist, 0), self.step_tiles)


def _stack_masked_q(q, qs_ref, geom):
    lane = _lane_iota((TQ, PAIR_W))
    width = PAIR_W // geom.n_streams
    for sub, st, rows in geom.blocks():
        q_tile = q[sub * TQ:(sub + 1) * TQ]
        qs_ref[rows, 0:PAIR_W] = jnp.where(lane // width == st, q_tile, jnp.zeros_like(q_tile))


def _flash_step(rows, s, v_aug, m_ref, acc_ref, first=False):
    s_max = jnp.max(s, axis=1, keepdims=True)
    m_new = jnp.broadcast_to(s_max, (s.shape[0], LANES)) if first else jnp.maximum(m_ref[rows, :], s_max)
    p = jnp.concatenate([jnp.exp2(s[:, c * LANES:(c + 1) * LANES] - m_new) for c in range(s.shape[1] // LANES)],
                        axis=1).astype(v_aug.dtype)
    pv = jnp.dot(p, v_aug, preferred_element_type=jnp.float32)
    if first:
        acc_ref[rows, :] = pv
    else:
        alpha = jnp.exp2(m_ref[rows, :] - m_new)
        acc_ref[rows, :] = jnp.concatenate([alpha, alpha], axis=1) * acc_ref[rows, :] + pv
    m_ref[rows, :] = m_new


def _normalised(acc_ref, rows):
    return acc_ref[rows, 0:PAIR_W] / acc_ref[rows, PAIR_W:ACC_W]


def _table_bias(tab_ref, hh, geom, m, sub, cap, n_tiles):
    return jnp.concatenate([tab_ref[hh, jnp.minimum(geom.dist(m, sub, w), cap) - MIN_DIST]
                            for w in n_tiles], axis=1)


def _step_rows(ref, geom, blk, m):
    k0 = pl.multiple_of(geom.first_tile(blk, m) * TK, geom.step_keys)
    return ref[pl.ds(k0, geom.step_keys), :]


def _make_attend(geom, blk, scratch, v_ref, keys, finish, max_dist=None):
    qs_ref, m_ref, acc_ref, s_ref = scratch
    blocks = geom.blocks()
    last_step = geom.n_steps(blk) - 1

    def raw(rows, rhs):
        return lax.dot_general(qs_ref[rows, :], rhs, _NT, preferred_element_type=jnp.float32)

    def v_aug(m):
        v = _step_rows(v_ref, geom, blk, m)
        return jnp.concatenate([v, jnp.ones_like(v)], axis=1)

    first_step = {sub: min(m for m in range(geom.own_steps()) if geom.tiles_needed(m, sub))
                  for sub in range(geom.n_sub)}

    def window(m, sub):
        tiles = range(geom.tiles_beyond(m, sub, max_dist), geom.tiles_needed(m, sub))
        return tiles, slice(tiles.start * TK, max(tiles.stop, tiles.start) * TK)

    def attend(rows, sub, st, scores, m, va, tiles, cols):
        first = isinstance(m, int) and m == first_step[sub]
        _flash_step(rows, finish(sub, st, scores, m, tiles), va[cols], m_ref, acc_ref, first)

    def single(m):
        rhs = keys(m)
        va = v_aug(m)
        for sub, st, rows in blocks:
            tiles, cols = window(m, sub)
            if len(tiles):
                attend(rows, sub, st, raw(rows, rhs[cols]), m, va, tiles, cols)

    def step(m, next_is_static=False, has_next=True):
        va = v_aug(m)
        rhs_next = keys(jnp.minimum(m + 1, last_step)) if has_next else None
        for sub, st, rows in blocks:
            tiles, cols = window(m, sub)
            cur = s_ref[rows, cols] if len(tiles) else None
            if has_next:
                next_tiles, next_cols = window(m + 1, sub) if next_is_static else window(None, sub)
                if len(next_tiles):
                    s_ref[rows, next_cols] = raw(rows, rhs_next[next_cols])
            if len(tiles):
                attend(rows, sub, st, cur, m, va, tiles, cols)

    def prologue(m0):
        rhs = keys(m0)
        for sub, _, rows in blocks:
            tiles, cols = window(m0, sub)
            if len(tiles):
                s_ref[rows, cols] = raw(rows, rhs[cols])

    def run(m0, count):
        peeled = [m0 + i for i in range(geom.own_steps()) if isinstance(m0, int)
                  and any(geom.tiles_needed(m0 + i, sub) < geom.step_tiles for sub in range(geom.n_sub))]
        starts_at_own = isinstance(m0, int) and m0 == 0
        assert not starts_at_own or len(peeled) == geom.own_steps()
        pl.when(count > 0)(functools.partial(prologue, m0))

        for m in peeled:
            pl.when(count > m - m0)(functools.partial(step, m, next_is_static=(m + 1) in peeled))

        def body(i, carry):
            step(m0 + i)
            return carry

        lax.fori_loop(len(peeled), count - 1, body, 0)
        pl.when(count > len(peeled))(functools.partial(step, m0 + count - 1, has_next=False))

    def run_static(count):
        prologue(0)
        for m in range(count):
            step(m, next_is_static=True, has_next=m + 1 < count)

    return single, run, run_static


def _attn_call(kernel, group, qkv, extra_inputs, extra_specs, geom, batch, seq, *, q_width=PAIR_W,
               extra_scratch=(), name=None):
    assert seq % geom.q_rows == 0 and seq % geom.step_keys == 0 and geom.n_sub % geom.step_tiles == 0
    nq = seq // geom.q_rows
    qc, kc, vc = 6 * group, 6 * group + 2, 6 * group + 4
    rows = geom.n_sub * geom.n_streams * TQ
    return pl.pallas_call(
        functools.partial(kernel, geom=geom),
        out_shape=jax.ShapeDtypeStruct((batch * seq, GROUP_W), jnp.bfloat16),
        grid=(2, batch, nq),
        in_specs=[pl.BlockSpec((geom.q_rows, PAIR_W), lambda p, b, i: (b * nq + i, qc + p)),
                  pl.BlockSpec((seq, PAIR_W), lambda p, b, i: (b, kc + p)),
                  pl.BlockSpec((seq, PAIR_W), lambda p, b, i: (b, vc + p))] + list(extra_specs),
        out_specs=pl.BlockSpec((geom.q_rows, PAIR_W), lambda p, b, i: (b * nq + i, p)),
        scratch_shapes=[pltpu.VMEM((rows, q_width), jnp.bfloat16),
                        pltpu.VMEM((rows, LANES), jnp.float32),
                        pltpu.VMEM((rows, ACC_W), jnp.float32),
                        pltpu.VMEM((rows, geom.step_keys), jnp.float32)
                        ] + list(extra_scratch),
        compiler_params=pltpu.CompilerParams(dimension_semantics=("arbitrary", "arbitrary", "arbitrary"),
                                             vmem_limit_bytes=VMEM_LIMIT),
        name=name,
    )(qkv, qkv, qkv, *extra_inputs)


def _tab_spec(first_head):
    return pl.BlockSpec((2, N_TAB_TILES, TQ, TK), lambda p, b, i: (first_head // 2 + p, 0, 0, 0))


def _merge_pair(o0, o1):
    return jnp.where(_lane_iota(o0.shape) < HEAD_DIM, o0, o1)


def _store_pair_outputs(o_ref, acc_ref, geom):
    for sub in range(geom.n_sub):
        o = _merge_pair(_normalised(acc_ref, geom.rows(sub, 0)), _normalised(acc_ref, geom.rows(sub, 1)))
        o_ref[sub * TQ:(sub + 1) * TQ, :] = o.astype(o_ref.dtype)


PAIR_GEOM = _Geom(n_streams=2, n_sub=4)
MOBA_GEOM = _Geom(n_streams=2, n_sub=MAX_SUB)
FOX_GEOM = _Geom(n_streams=2, n_sub=2)
DIFF_GEOM = _Geom(n_streams=4, n_sub=4)


def _dilated_kernel(q_ref, k_ref, v_ref, tab_ref, o_ref, qs_ref, m_ref, acc_ref, s_ref, *, geom):
    blk = pl.program_id(2)
    _stack_masked_q(q_ref[...], qs_ref, geom)
    _, run, run_static = _make_attend(
        geom, blk, (qs_ref, m_ref, acc_ref, s_ref), v_ref,
        keys=lambda m: _step_rows(k_ref, geom, blk, m),
        finish=lambda sub, st, raw, m, tiles: raw + _table_bias(tab_ref, st, geom, m, sub, N_DIL_TILES, tiles),
        max_dist=N_DIL_TILES - 1)
    reach = -(-(N_DIL_TILES - 1 + geom.n_sub) // geom.step_tiles)
    n_steps = geom.n_steps(blk)
    pl.when(n_steps >= reach)(functools.partial(run_static, reach))
    pl.when(n_steps < reach)(functools.partial(run, 0, n_steps))
    _store_pair_outputs(o_ref, acc_ref, geom)


def _dilated(qkv, tabs, batch, seq):
    return _attn_call(_dilated_kernel, 0, qkv, [tabs], [_tab_spec(0)], PAIR_GEOM, batch, seq, name="dilated_attn")


def _diff_kernel(q_ref, k_ref, v_ref, tab_ref, lam_ref, g_ref, o_ref, qs_ref, m_ref, acc_ref, s_ref, *,
                 lambda_init, geom):
    blk = pl.program_id(2)
    _stack_masked_q(q_ref[...], qs_ref, geom)
    _, run, _ = _make_attend(
        geom, blk, (qs_ref, m_ref, acc_ref, s_ref), v_ref,
        keys=lambda m: _step_rows(k_ref, geom, blk, m),
        finish=lambda sub, st, raw, m, tiles: raw + _table_bias(tab_ref, st // 2, geom, m, sub, FAR_TILE, tiles))
    run(0, geom.n_steps(blk))

    lamv = lam_ref[...]
    lam = (jnp.exp(jnp.sum(lamv[0:1] * lamv[1:2], axis=1, keepdims=True))
           - jnp.exp(jnp.sum(lamv[2:3] * lamv[3:4], axis=1, keepdims=True)) + lambda_init)
    first = _lane_iota((TQ, PAIR_W)) < HEAD_DIM
    for sub in range(geom.n_sub):
        part = [_normalised(acc_ref, geom.rows(sub, st)) for st in range(geom.n_streams)]
        o = _merge_pair(part[0] - lam * part[1], part[2] - lam * part[3])
        sq = o * o
        ms0 = jnp.sum(jnp.where(first, sq, 0.0), axis=1, keepdims=True)
        ms1 = jnp.sum(jnp.where(first, 0.0, sq), axis=1, keepdims=True)
        ms = jnp.where(first, ms0, ms1) * (1.0 / HEAD_DIM)
        y = (o * lax.rsqrt(ms + RMS_EPS) * g_ref[...]) * (1.0 - lambda_init)
        o_ref[sub * TQ:(sub + 1) * TQ, :] = y.astype(o_ref.dtype)


def _diff(qkv, tabs, lamv, g_sub, lambda_init, batch, seq):
    return _attn_call(functools.partial(_diff_kernel, lambda_init=lambda_init), 1, qkv, [tabs, lamv, g_sub],
                      [_tab_spec(4), pl.BlockSpec((8, LANES), lambda p, b, i: (0, 0)),
                       pl.BlockSpec((1, PAIR_W), lambda p, b, i: (0, 0))],
                      DIFF_GEOM, batch, seq, name="diff_attn")


def _moba_kernel(q_ref, k_ref, v_ref, tab_ref, o_ref, qs_ref, m_ref, acc_ref, s_ref, km_ref, *, geom):
    blk = pl.program_id(2)
    n_blk = k_ref.shape[0] // MOBA_BLOCK
    assert n_blk <= LANES and MOBA_BLOCK == TQ == TK

    @pl.when(blk == 0)
    def _():
        km_ref[...] = jnp.zeros(km_ref.shape, jnp.float32)
        for n in range(n_blk):
            km_ref[n:n + 1, :] = jnp.sum(k_ref[n * MOBA_BLOCK:(n + 1) * MOBA_BLOCK, :].astype(jnp.float32),
                                         axis=0, keepdims=True) * (1.0 / MOBA_BLOCK)

    _stack_masked_q(q_ref[...], qs_ref, geom)
    nb = -(-n_blk // 8) * 8
    km = km_ref[0:nb, :]
    km_hi = km.astype(jnp.bfloat16)
    km_lo = (km - km_hi.astype(jnp.float32)).astype(jnp.bfloat16)
    qs = qs_ref[:, 0:PAIR_W]
    gate = (lax.dot_general(km_hi, qs, _NT, preferred_element_type=jnp.float32)
            + lax.dot_general(km_lo, qs, _NT, preferred_element_type=jnp.float32))
    cand = lax.broadcasted_iota(jnp.int32, gate.shape, 0)
    cand_f = cand.astype(jnp.float32)
    own = geom.n_sub * blk + _lane_iota(gate.shape) // (geom.n_streams * TQ)
    g = jnp.where(cand < own, gate, NEG)
    sel = cand == own
    for _ in range(MOBA_TOPK):
        mx = jnp.max(g, axis=0, keepdims=True)
        first = jnp.min(jnp.where(g == mx, cand_f, 2.0 * LANES), axis=0, keepdims=True)
        pick = cand_f == first
        sel = jnp.logical_or(sel, jnp.logical_and(pick, mx > 0.5 * NEG))
        g = jnp.where(pick, -3e38, g)
    mask = jnp.where(sel, 0.0, NEG)
    if nb < LANES:
        mask = jnp.concatenate([mask, jnp.full((LANES - nb, mask.shape[1]), NEG, jnp.float32)], axis=0)
    qs_ref[:, PAIR_W:2 * PAIR_W] = mask.T.astype(jnp.bfloat16)

    def keys(m):
        k = _step_rows(k_ref, geom, blk, m)
        block_of_key = geom.first_tile(blk, m) + lax.broadcasted_iota(jnp.int32, (geom.step_keys, LANES), 0) // TK
        onehot = jnp.where(_lane_iota((geom.step_keys, LANES)) == block_of_key, 1.0, 0.0).astype(k.dtype)
        return jnp.concatenate([k, onehot], axis=1)

    _, run, _ = _make_attend(
        geom, blk, (qs_ref, m_ref, acc_ref, s_ref), v_ref, keys=keys,
        finish=lambda sub, st, raw, m, tiles: raw + _table_bias(tab_ref, st, geom, m, sub, FAR_TILE, tiles))
    run(0, geom.n_steps(blk))
    _store_pair_outputs(o_ref, acc_ref, geom)


def _moba(qkv, tabs, batch, seq):
    return _attn_call(_moba_kernel, 2, qkv, [tabs], [_tab_spec(8)], MOBA_GEOM, batch, seq, q_width=2 * PAIR_W,
                      extra_scratch=[pltpu.VMEM((LANES, PAIR_W), jnp.float32)], name="moba_attn")


FOX_SKIP_BITS = 48.0


def _head_row_norms(x, hh):
    xf = x.astype(jnp.float32)
    sq = jnp.where(_lane_iota(xf.shape) // HEAD_DIM == hh, xf * xf, 0.0)
    return jnp.sqrt(jnp.max(jnp.sum(sq, axis=1, keepdims=True), axis=0, keepdims=True))


def _fox_kernel(q_ref, k_ref, v_ref, c_ref, o_ref, qs_ref, m_ref, acc_ref, s_ref, cq_ref, stat_ref, *, geom):
    blk = pl.program_id(2)
    n_tiles = k_ref.shape[0] // TK
    assert n_tiles <= LANES

    @pl.when(blk == 0)
    def _():
        stat_ref[...] = jnp.zeros(stat_ref.shape, jnp.float32)
        for j in range(n_tiles):
            k_tile = k_ref[j * TK:(j + 1) * TK, :]
            for hh in range(2):
                stat_ref[hh:hh + 1, j:j + 1] = _head_row_norms(k_tile, hh)
                stat_ref[2 + hh:3 + hh, j:j + 1] = jnp.min(c_ref[hh:hh + 1, j * TK:(j + 1) * TK], axis=1, keepdims=True)

    q = q_ref[...]
    _stack_masked_q(q, qs_ref, geom)
    q0 = pl.multiple_of(blk * geom.q_rows, geom.q_rows)
    row = lax.broadcasted_iota(jnp.int32, (TQ, TQ), 0)
    col = lax.broadcasted_iota(jnp.int32, (TQ, TQ), 1)
    for sub, hh, rows in geom.blocks():
        c_row = c_ref[hh:hh + 1, pl.ds(q0 + sub * TQ, TQ)]
        c_col = jnp.sum(jnp.where(row == col, c_row, 0.0), axis=1, keepdims=True)
        cq_ref[rows, :] = jnp.broadcast_to(c_col, (TQ, LANES))

    def finish(sub, hh, raw, m, tiles):
        assert tiles.start == 0
        n_keys = len(tiles) * TK
        k0 = pl.multiple_of(geom.first_tile(blk, m) * TK, TK)
        cq = cq_ref[geom.rows(sub, hh), :]
        s = raw + (jnp.concatenate([cq] * (n_keys // LANES), axis=1) - c_ref[hh:hh + 1, pl.ds(k0, n_keys)])
        if isinstance(m, int) and m < geom.own_steps():
            key_pos = lax.broadcasted_iota(jnp.int32, (TQ, n_keys), 1)
            query_pos = lax.broadcasted_iota(jnp.int32, (TQ, n_keys), 0)
            first_key = (geom.n_sub - geom.step_tiles * (m + 1)) * TK
            s = jnp.where(key_pos + first_key <= query_pos + sub * TQ, s, NEG)
        return s

    single, run, _ = _make_attend(geom, blk, (qs_ref, m_ref, acc_ref, s_ref), v_ref,
                               keys=lambda m: _step_rows(k_ref, geom, blk, m), finish=finish)
    for m in range(geom.own_steps()):
        single(m)

    tile = _lane_iota((1, LANES))
    needed = jnp.zeros((1, LANES), jnp.bool_)
    for hh in range(2):
        rows_h = [geom.rows(sub, hh) for sub in range(geom.n_sub)]
        m_min = functools.reduce(jnp.minimum, [jnp.min(m_ref[r, :], axis=0, keepdims=True) for r in rows_h])
        cq_max = functools.reduce(jnp.maximum, [jnp.max(cq_ref[r, :], axis=0, keepdims=True) for r in rows_h])
        bound = _head_row_norms(q, hh) * stat_ref[hh:hh + 1, :] + cq_max - stat_ref[2 + hh:3 + hh, :] + 1.0
        needed = jnp.logical_or(needed, bound >= m_min - FOX_SKIP_BITS)
    n_before = geom.n_sub * blk
    first_needed = jnp.min(jnp.where(jnp.logical_and(needed, tile < n_before), tile, n_before))
    run(geom.own_steps(), (n_before - first_needed + geom.step_tiles - 1) // geom.step_tiles)
    _store_pair_outputs(o_ref, acc_ref, geom)


def _fox(qkv, c, batch, seq):
    return _attn_call(_fox_kernel, 3, qkv, [c.reshape(batch, 2, 8, seq)],
                      [pl.BlockSpec((None, None, 8, seq), lambda p, b, i: (b, p, 0, 0))],
                      FOX_GEOM, batch, seq,
                      extra_scratch=[pltpu.VMEM((FOX_GEOM.n_sub * FOX_GEOM.n_streams * TQ, LANES), jnp.float32),
                                     pltpu.VMEM((8, LANES), jnp.float32)], name="fox_attn")


OUT_TM = 1024
ROUTER_W = LANES
ROUTE_ROWS = 8


PACK_W = D_MODEL // 2


def _pack_bf16_pair(lo, hi):
    lo_bits = lax.bitcast_convert_type(lo.astype(jnp.bfloat16).astype(jnp.float32), jnp.uint32)
    hi_bits = lax.bitcast_convert_type(hi.astype(jnp.bfloat16).astype(jnp.float32), jnp.uint32)
    word = lax.shift_right_logical(lo_bits, jnp.uint32(16)) | (hi_bits & jnp.uint32(0xFFFF0000))
    return lax.bitcast_convert_type(word, jnp.int32)


def _unpack_bf16_pair(word):
    bits = lax.bitcast_convert_type(word, jnp.uint32)
    lo = lax.bitcast_convert_type(lax.shift_left(bits, jnp.uint32(16)), jnp.float32)
    hi = lax.bitcast_convert_type(bits & jnp.uint32(0xFFFF0000), jnp.float32)
    return lo, hi


def _outproj_router_kernel(oa_ref, ob_ref, oc_ref, od_ref, wo_ref, x_ref, g_ref, wr_ref, br_ref,
                           x1_ref, h2_ref, route_ref, cw_ref):
    acc = x_ref[...]
    for m, o_ref in enumerate((oa_ref, ob_ref, oc_ref, od_ref)):
        acc = acc + jnp.dot(o_ref[...], wo_ref[m * GROUP_W:(m + 1) * GROUP_W, :], preferred_element_type=jnp.float32)
    x1_ref[...] = acc
    ms = jnp.mean(acc * acc, axis=1, keepdims=True)
    h = acc * lax.rsqrt(ms + RMS_EPS) * g_ref[...]
    hb = h.astype(jnp.bfloat16)
    h2_ref[...] = _pack_bf16_pair(h[:, 0:PACK_W], h[:, PACK_W:D_MODEL])
    h_lo = (h - hb.astype(jnp.float32)).astype(jnp.bfloat16)
    logits = (jnp.dot(hb, wr_ref[0], preferred_element_type=jnp.float32)
              + jnp.dot(h_lo, wr_ref[0], preferred_element_type=jnp.float32)
              + jnp.dot(hb, wr_ref[1], preferred_element_type=jnp.float32)) + br_ref[...]
    lane = _lane_iota(logits.shape).astype(jnp.float32)
    big = 2.0 * LANES
    gmask = jnp.logical_and(lane >= N_EXPERTS, lane < N_EXPERTS + N_GROUPS)
    gl = jnp.where(gmask, logits, NEG)
    gmax = jnp.max(gl, axis=1, keepdims=True)
    glane = jnp.min(jnp.where(gl == gmax, lane, big), axis=1, keepdims=True)
    gsum = jnp.sum(jnp.where(gmask, jnp.exp(gl - gmax), 0.0), axis=1, keepdims=True)
    g_w = 1.0 / gsum
    e0 = (glane - N_EXPERTS) * EXPERTS_PER_GROUP
    emask = jnp.logical_and(lane >= e0, lane < e0 + EXPERTS_PER_GROUP)
    el = jnp.where(emask, logits, NEG)
    v1 = jnp.max(el, axis=1, keepdims=True)
    i1 = jnp.min(jnp.where(el == v1, lane, big), axis=1, keepdims=True)
    el2 = jnp.where(lane == i1, NEG, el)
    v2 = jnp.max(el2, axis=1, keepdims=True)
    i2 = jnp.min(jnp.where(el2 == v2, lane, big), axis=1, keepdims=True)
    e2 = jnp.exp(v2 - v1)
    den = 1.0 + e2
    c1 = g_w / den
    c2 = g_w * e2 / den
    quad = jnp.where(lane == 0.0, i1, jnp.where(lane == 1.0, i2, jnp.where(lane == 2.0, c1,
                                                                             jnp.where(lane == 3.0, c2, 0.0))))
    cw_ref[...] = jnp.where(lane == 0.0, c1, jnp.where(lane == 1.0, c2, 0.0))
    route_ref[...] = quad.T[0:ROUTE_ROWS, :]


def _outproj_router(o_parts, w_out, x2d, g2, wr, br):
    n_tok = x2d.shape[0]
    tm = min(OUT_TM, n_tok)
    row = lambda i: (i, 0)
    fixed = lambda i: (0, 0)
    return pl.pallas_call(
        _outproj_router_kernel,
        out_shape=(jax.ShapeDtypeStruct((n_tok, D_MODEL), jnp.float32),
                   jax.ShapeDtypeStruct((n_tok, PACK_W), jnp.int32),
                   jax.ShapeDtypeStruct((ROUTE_ROWS, n_tok), jnp.float32),
                   jax.ShapeDtypeStruct((n_tok, ROUTER_W), jnp.float32)),
        grid=(n_tok // tm,),
        in_specs=[pl.BlockSpec((tm, GROUP_W), row)] * 4 + [
            pl.BlockSpec((D_MODEL, D_MODEL), fixed),
            pl.BlockSpec((tm, D_MODEL), row),
            pl.BlockSpec((1, D_MODEL), fixed),
            pl.BlockSpec((2, D_MODEL, ROUTER_W), lambda i: (0, 0, 0)),
            pl.BlockSpec((1, ROUTER_W), fixed)],
        out_specs=(pl.BlockSpec((tm, D_MODEL), row), pl.BlockSpec((tm, PACK_W), row),
                   pl.BlockSpec((ROUTE_ROWS, tm), lambda i: (0, i)), pl.BlockSpec((tm, ROUTER_W), row)),
        compiler_params=pltpu.CompilerParams(dimension_semantics=("arbitrary",), vmem_limit_bytes=VMEM_LIMIT),
        name="outproj_router",
    )(*o_parts, w_out, x2d, g2, wr, br)


ROW_TILE = 512
DISP_TM = 1024
SC_CORES = 2
SC_SUBCORES = 16
SC_WORKERS = SC_CORES * SC_SUBCORES
SC_CHUNK = 64


def _dispatch_kernel(route_ref, rank_ref, cnt_ref, tri_ref, carry_ref):
    step = pl.program_id(0)
    tm = route_ref.shape[1]

    @pl.when(step == 0)
    def _():
        r = lax.broadcasted_iota(jnp.int32, (tm, tm), 0)
        c = lax.broadcasted_iota(jnp.int32, (tm, tm), 1)
        tri_ref[...] = jnp.where(r <= c, 1.0, 0.0).astype(jnp.bfloat16)
        carry_ref[...] = jnp.zeros(carry_ref.shape, jnp.float32)

    e1 = route_ref[0:1, :]
    e2 = route_ref[1:2, :]
    expert = lax.broadcasted_iota(jnp.int32, (N_EXPERTS, tm), 0).astype(jnp.float32)
    hit1 = e1 == expert
    hit2 = e2 == expert
    hits = jnp.where(jnp.logical_or(hit1, hit2), 1.0, 0.0).astype(jnp.bfloat16)
    cum = jnp.dot(hits, tri_ref[...], preferred_element_type=jnp.float32) + carry_ref[...]
    rank1 = jnp.sum(jnp.where(hit1, cum - 1.0, 0.0), axis=0, keepdims=True)
    rank2 = jnp.sum(jnp.where(hit2, cum - 1.0, 0.0), axis=0, keepdims=True)
    row = lax.broadcasted_iota(jnp.int32, (ROUTE_ROWS, tm), 0)
    rank_ref[...] = jnp.where(row == 0, rank1, jnp.where(row == 1, rank2, 0.0))
    total = cum[:, tm - 1:tm]
    carry_ref[...] = total
    cnt_ref[...] = jnp.broadcast_to(total, cnt_ref.shape)


def _dispatch(route):
    n_tok = route.shape[1]
    tm = min(DISP_TM, n_tok)
    return pl.pallas_call(
        _dispatch_kernel,
        out_shape=(jax.ShapeDtypeStruct((ROUTE_ROWS, n_tok), jnp.float32),
                   jax.ShapeDtypeStruct((N_EXPERTS, LANES), jnp.float32)),
        grid=(n_tok // tm,),
        in_specs=[pl.BlockSpec((ROUTE_ROWS, tm), lambda i: (0, i))],
        out_specs=(pl.BlockSpec((ROUTE_ROWS, tm), lambda i: (0, i)),
                   pl.BlockSpec((N_EXPERTS, LANES), lambda i: (0, 0))),
        scratch_shapes=[pltpu.VMEM((tm, tm), jnp.bfloat16), pltpu.VMEM((N_EXPERTS, 1), jnp.float32)],
        compiler_params=pltpu.CompilerParams(dimension_semantics=("arbitrary",)),
        name="moe_ranks",
    )(route)


def _positions_kernel(route_ref, rank_ref, off_ref, pos_ref):
    tm = route_ref.shape[1]
    expert = lax.broadcasted_iota(jnp.int32, (N_EXPERTS, tm), 0).astype(jnp.float32)
    off = off_ref[...]
    start1 = jnp.sum(jnp.where(route_ref[0:1, :] == expert, off, 0.0), axis=0, keepdims=True)
    start2 = jnp.sum(jnp.where(route_ref[1:2, :] == expert, off, 0.0), axis=0, keepdims=True)
    row = lax.broadcasted_iota(jnp.int32, (ROUTE_ROWS, tm), 0)
    pos_ref[...] = jnp.where(row == 0, start1 + rank_ref[0:1, :], jnp.where(row == 1, start2 + rank_ref[1:2, :], 0.0))


def _positions(route, rank, seg_start):
    n_tok = route.shape[1]
    tm = min(DISP_TM, n_tok)
    blk = pl.BlockSpec((ROUTE_ROWS, tm), lambda i: (0, i))
    return pl.pallas_call(
        _positions_kernel,
        out_shape=jax.ShapeDtypeStruct((ROUTE_ROWS, n_tok), jnp.float32),
        grid=(n_tok // tm,),
        in_specs=[blk, blk, pl.BlockSpec((N_EXPERTS, 1), lambda i: (0, 0))],
        out_specs=blk,
        name="moe_positions",
    )(route, rank, seg_start)


def _sc_mesh():
    from jax.experimental.pallas import tpu_sc as plsc
    return plsc.VectorSubcoreMesh(core_axis_name="c", subcore_axis_name="s")


def _sc_worker_id():
    return lax.axis_index("s") * SC_CORES + lax.axis_index("c")


def _sc_scatter_rows(rows, pos, n_out):
    n_tok, width = rows.shape
    per_worker = n_tok // SC_WORKERS
    n_pairs = per_worker // (2 * SC_CHUNK)
    assert per_worker % (2 * SC_CHUNK) == 0
    buf = [pltpu.VMEM((SC_CHUNK,), jnp.int32), pltpu.VMEM((SC_CHUNK,), jnp.int32),
           pltpu.VMEM((SC_CHUNK, width), rows.dtype), pltpu.SemaphoreType.DMA, pltpu.SemaphoreType.DMA]

    @functools.partial(pl.kernel, mesh=_sc_mesh(), out_type=jax.ShapeDtypeStruct((n_out, width), rows.dtype),
                       scratch_types=buf + buf)
    def scatter(rows_hbm, pos_hbm, out_hbm, *scratch):
        base = _sc_worker_id() * per_worker
        buf_a, buf_b = scratch[:5], scratch[5:]

        def load(chunk, p1_v, p2_v, rows_v, sem_in, sem_out):
            off = base + chunk * SC_CHUNK
            pltpu.sync_copy(pos_hbm.at[pl.ds(off, SC_CHUNK)], p1_v)
            pltpu.sync_copy(pos_hbm.at[pl.ds(n_tok + off, SC_CHUNK)], p2_v)
            pltpu.async_copy(rows_hbm.at[pl.ds(off, SC_CHUNK)], rows_v, sem_in)

        def flush(chunk, p1_v, p2_v, rows_v, sem_in, sem_out):
            off = base + chunk * SC_CHUNK
            pltpu.make_async_copy(rows_hbm.at[pl.ds(off, SC_CHUNK)], rows_v, sem_in).wait()
            first = pltpu.async_copy(rows_v, out_hbm.at[p1_v], sem_out)
            second = pltpu.async_copy(rows_v, out_hbm.at[p2_v], sem_out)
            first.wait()
            second.wait()

        load(0, *buf_a)
        def body(j, carry):
            load(2 * j + 1, *buf_b)
            flush(2 * j, *buf_a)

            @pl.when(j + 1 < n_pairs)
            def _():
                load(2 * j + 2, *buf_a)

            flush(2 * j + 1, *buf_b)
            return carry

        lax.fori_loop(0, n_pairs, body, 0)

    return scatter(rows, pos)


def _sc_gather_rows(table, idx):
    n_idx = idx.shape[0]
    width = table.shape[1]
    per_worker = n_idx // SC_WORKERS
    n_pairs = per_worker // (2 * SC_CHUNK)
    assert per_worker % (2 * SC_CHUNK) == 0
    buf = [pltpu.VMEM((SC_CHUNK,), jnp.int32), pltpu.VMEM((SC_CHUNK, width), table.dtype), pltpu.SemaphoreType.DMA]

    @functools.partial(pl.kernel, mesh=_sc_mesh(), out_type=jax.ShapeDtypeStruct((n_idx, width), table.dtype),
                       scratch_types=buf + buf)
    def gather(table_hbm, idx_hbm, out_hbm, *scratch):
        base = _sc_worker_id() * per_worker
        buf_a, buf_b = scratch[:3], scratch[3:]

        def fetch(chunk, idx_v, rows_v, sem):
            pltpu.sync_copy(idx_hbm.at[pl.ds(base + chunk * SC_CHUNK, SC_CHUNK)], idx_v)
            pltpu.async_copy(table_hbm.at[idx_v], rows_v, sem)

        def drain(chunk, idx_v, rows_v, sem):
            pltpu.make_async_copy(table_hbm.at[idx_v], rows_v, sem).wait()
            pltpu.sync_copy(rows_v, out_hbm.at[pl.ds(base + chunk * SC_CHUNK, SC_CHUNK)])

        fetch(0, *buf_a)
        def body(j, carry):
            fetch(2 * j + 1, *buf_b)
            drain(2 * j, *buf_a)

            @pl.when(j + 1 < n_pairs)
            def _():
                fetch(2 * j + 2, *buf_a)

            drain(2 * j + 1, *buf_b)
            return carry

        lax.fori_loop(0, n_pairs, body, 0)

    return gather(table, idx)


def _experts_kernel(tile_expert_ref, n_used_ref, hs_ref, wg_ref, wu_ref, wd_ref, ys_ref):
    del tile_expert_ref

    @pl.when(pl.program_id(0) < n_used_ref[0])
    def _():
        lo, hi = _unpack_bf16_pair(hs_ref[...])
        lo = lo.astype(jnp.bfloat16)
        hi = hi.astype(jnp.bfloat16)
        bf = jnp.bfloat16
        gate = (jnp.dot(lo, wg_ref[0:PACK_W, :].astype(bf), preferred_element_type=jnp.float32)
                + jnp.dot(hi, wg_ref[PACK_W:D_MODEL, :].astype(bf), preferred_element_type=jnp.float32))
        up = (jnp.dot(lo, wu_ref[0:PACK_W, :].astype(bf), preferred_element_type=jnp.float32)
              + jnp.dot(hi, wu_ref[PACK_W:D_MODEL, :].astype(bf), preferred_element_type=jnp.float32))
        hid = ((gate * jax.nn.sigmoid(gate)) * up).astype(bf)
        y = jnp.dot(hid, wd_ref[...].astype(bf), preferred_element_type=jnp.float32)
        ys_ref[...] = _pack_bf16_pair(y[:, 0:PACK_W], y[:, PACK_W:D_MODEL])


def _experts(hs, tile_expert, n_used, wg, wu, wd, layer):
    n_rows = hs.shape[0]
    weights = lambda i, te, nu: (layer, te[i], 0, 0)
    return pl.pallas_call(
        _experts_kernel,
        out_shape=jax.ShapeDtypeStruct((n_rows, PACK_W), jnp.int32),
        grid_spec=pltpu.PrefetchScalarGridSpec(
            num_scalar_prefetch=2,
            grid=(n_rows // ROW_TILE,),
            in_specs=[pl.BlockSpec((ROW_TILE, PACK_W), lambda i, te, nu: (i, 0)),
                      pl.BlockSpec((None, None, D_MODEL, D_EXPERT), weights),
                      pl.BlockSpec((None, None, D_MODEL, D_EXPERT), weights),
                      pl.BlockSpec((None, None, D_EXPERT, D_MODEL), weights)],
            out_specs=pl.BlockSpec((ROW_TILE, PACK_W), lambda i, te, nu: (i, 0))),
        compiler_params=pltpu.CompilerParams(dimension_semantics=("arbitrary",), vmem_limit_bytes=VMEM_LIMIT),
        name="moe_experts",
    )(tile_expert, n_used, hs, wg, wu, wd)


COMB_TM = 1024


def _combine_kernel(x1_ref, y1_ref, y2_ref, cw_ref, gf_ref, out_ref, *, final_norm):
    c1 = cw_ref[:, 0:1]
    c2 = cw_ref[:, 1:2]
    lo1, hi1 = _unpack_bf16_pair(y1_ref[...])
    lo2, hi2 = _unpack_bf16_pair(y2_ref[...])
    y = jnp.concatenate([x1_ref[:, 0:PACK_W] + (c1 * lo1 + c2 * lo2),
                         x1_ref[:, PACK_W:D_MODEL] + (c1 * hi1 + c2 * hi2)], axis=1)
    if final_norm:
        ms = jnp.mean(y * y, axis=1, keepdims=True)
        y = y * lax.rsqrt(ms + RMS_EPS) * gf_ref[...]
    out_ref[...] = y


def _combine(x1, y12, cw, gf, final_norm):
    n_tok = x1.shape[0]
    tm = min(COMB_TM, n_tok)
    n_blk = n_tok // tm
    return pl.pallas_call(
        functools.partial(_combine_kernel, final_norm=final_norm),
        out_shape=jax.ShapeDtypeStruct((n_tok, D_MODEL), jnp.float32),
        grid=(n_blk,),
        in_specs=[pl.BlockSpec((tm, D_MODEL), lambda i: (i, 0)),
                  pl.BlockSpec((tm, PACK_W), lambda i: (i, 0)),
                  pl.BlockSpec((tm, PACK_W), lambda i: (n_blk + i, 0)),
                  pl.BlockSpec((tm, ROUTER_W), lambda i: (i, 0)),
                  pl.BlockSpec((1, D_MODEL), lambda i: (0, 0))],
        out_specs=pl.BlockSpec((tm, D_MODEL), lambda i: (i, 0)),
        compiler_params=pltpu.CompilerParams(dimension_semantics=("arbitrary",), vmem_limit_bytes=VMEM_LIMIT),
        name="moe_combine",
    )(x1, y12, y12, cw, gf)


def _sparse_moe(x1, h2p, route, cw, w_gate, w_up, w_down, layer, gf, final_norm):
    n_tok = x1.shape[0]
    n_rows = 2 * n_tok + N_EXPERTS * ROW_TILE
    rank, counts = _dispatch(route)
    padded = (counts[:, 0].astype(jnp.int32) + ROW_TILE - 1) // ROW_TILE * ROW_TILE
    ends = jnp.cumsum(padded)
    starts = ends - padded
    tile_start = jnp.arange(n_rows // ROW_TILE, dtype=jnp.int32) * ROW_TILE
    tile_expert = jnp.minimum(jnp.sum(tile_start[:, None] >= ends[None, :], axis=1), N_EXPERTS - 1).astype(jnp.int32)
    n_used = (ends[N_EXPERTS - 1:] // ROW_TILE).astype(jnp.int32)

    pos = _positions(route, rank, starts.astype(jnp.float32).reshape(N_EXPERTS, 1))
    pos12 = jnp.concatenate([pos[0], pos[1]]).astype(jnp.int32)
    hs = _sc_scatter_rows(h2p, pos12, n_rows)
    ys = _experts(hs, tile_expert, n_used, w_gate, w_up, w_down, layer)
    y12 = _sc_gather_rows(ys, pos12)
    return _combine(x1, y12, cw, gf, final_norm)


def _split_bf16(w):
    hi = w.astype(jnp.bfloat16)
    lo = (w - hi.astype(jnp.float32)).astype(jnp.bfloat16)
    return jnp.stack([hi, lo])


def _qkv_col_scale():
    s = np.ones((QKV_W,), np.float32)
    for group, dim in enumerate((HEAD_DIM, DIFF_DIM, HEAD_DIM, HEAD_DIM)):
        s[3 * group * GROUP_W:(3 * group + 1) * GROUP_W] = LOG2E * dim ** -0.5
    return s


def _forget_weights(w_f, b_f):
    rows = jnp.zeros((F_ROWS, D_MODEL), jnp.float32)
    bias = jnp.zeros((F_ROWS, 1), jnp.float32)
    for h in range(HEADS_PER_MIXER):
        r = (h // 2) * 8 + (h % 2)
        rows = rows.at[r].set(w_f[:, h])
        bias = bias.at[r, 0].set(b_f[h])
    return _split_bf16(rows), bias


def kernel(x, rel_bias, ln1, w_in, w_out, lam_q1, lam_k1, lam_q2, lam_k2, subln_g, b_forget,
           ln2, w_group, b_group, w_expert, b_expert, w_gate, w_up, w_down, ln_f):
    batch, seq, _ = x.shape
    depth = ln1.shape[0]
    tabs = _bias_tables(rel_bias)
    xt = x.reshape(batch * seq, D_MODEL)
    gf = ln_f.reshape(1, D_MODEL)
    col_scale = _qkv_col_scale()
    for l in range(depth):
        lambda_init = 0.8 - 0.6 * math.exp(-0.3 * l)
        wf, bf = _forget_weights(w_in[l][:, QKV_W:], b_forget[l])
        w_qkv = (w_in[l][:, :QKV_W] * col_scale).astype(jnp.bfloat16)
        qkv, lf = _norm_proj(xt, ln1[l].reshape(1, D_MODEL), w_qkv, wf, bf, batch, seq)
        c = _cumsum(lf)

        o_a = _dilated(qkv, tabs, batch, seq)
        lamv = jnp.zeros((8, LANES), jnp.float32)
        lamv = lamv.at[0:4, 0:DIFF_DIM].set(jnp.stack([lam_q1[l], lam_k1[l], lam_q2[l], lam_k2[l]]))
        g_sub = jnp.tile(subln_g[l], 2).reshape(1, PAIR_W)
        o_b = _diff(qkv, tabs, lamv, g_sub, lambda_init, batch, seq)
        o_c = _moba(qkv, tabs, batch, seq)
        o_d = _fox(qkv, c, batch, seq)

        w_router = jnp.zeros((D_MODEL, ROUTER_W), jnp.float32)
        w_router = w_router.at[:, :N_EXPERTS].set(w_expert[l]).at[:, N_EXPERTS:N_EXPERTS + N_GROUPS].set(w_group[l])
        b_router = jnp.zeros((1, ROUTER_W), jnp.float32)
        b_router = b_router.at[0, :N_EXPERTS].set(b_expert[l].reshape(-1)).at[0, N_EXPERTS:N_EXPERTS + N_GROUPS].set(b_group[l])
        x1, h2p, route, cw = _outproj_router([o_a, o_b, o_c, o_d], w_out[l].astype(jnp.bfloat16), xt,
                                             ln2[l].reshape(1, D_MODEL), _split_bf16(w_router), b_router)
        xt = _sparse_moe(x1, h2p, route, cw, w_gate, w_up, w_down, l, gf, final_norm=(l == depth - 1))
    return xt.reshape(batch, seq, D_MODEL)
```
